```python
import math
import functools
import jax
import jax.numpy as jnp
from jax import lax
import numpy as np

D_MODEL = 1024
BATCH = 8
SEQ = 2048
DEPTH = 2
DEC_BATCH = 32
DEC_SEQ = 8
PAST_LEN = 8192
PAGE_SIZE = 128

HEAD_DIM = 64
GROUP_W = D_MODEL // 4
SB_HEADS = GROUP_W // HEAD_DIM
ML_HEADS = GROUP_W // HEAD_DIM
RT_HEADS = GROUP_W // HEAD_DIM
S5_CH = GROUP_W
S5_GROUP = 16
S5_GROUPS = S5_CH // S5_GROUP
S5_STATE = 64
IN_SPLITS = (GROUP_W, GROUP_W, GROUP_W,
             GROUP_W, GROUP_W, GROUP_W, GROUP_W,
             ML_HEADS, ML_HEADS,
             GROUP_W, GROUP_W, GROUP_W, GROUP_W,
             S5_CH)
IN_W = 11 * GROUP_W + 2 * ML_HEADS + S5_CH
MIX_W = 3 * GROUP_W + S5_CH
CA_HEADS = 4
CA_HD = 64
CA_W = CA_HEADS * CA_HD
N_MEM = 256
D_FF = 7 * D_MODEL // 2
N_EXPERTS = 8
TOP_K = 2
EXPERT_BLOCK = 128
N_DENSE = (DEPTH + 1) // 2
N_MOE = DEPTH // 2
QBLK = 128
CHUNK = 64
ROPE_BASE = 10000.0
LN_EPS = 1e-5
GN_EPS = 1e-6
DEEPNORM_ALPHA = (2 * DEPTH) ** 0.25
DEEPNORM_BETA = (8 * DEPTH) ** -0.25
F32 = jnp.float32

kernel_name = 'hybrid_sb_mlstm_retnet_s5_decoder_step'


def layer_norm(x, g, b):
    xf = x.astype(F32)
    mu = xf.mean(-1, keepdims=True)
    var = jnp.square(xf - mu).mean(-1, keepdims=True)
    return ((xf - mu) * lax.rsqrt(var + LN_EPS) * g.astype(F32) + b.astype(F32)).astype(x.dtype)


def head_norm(h, g):
    mu = h.mean(-1, keepdims=True)
    var = jnp.square(h - mu).mean(-1, keepdims=True)
    y = (h - mu) * lax.rsqrt(var + GN_EPS)
    return y.reshape(h.shape[0], h.shape[1], -1) * g.astype(F32)


def rope(x, pos):
    half = HEAD_DIM // 2
    freq = ROPE_BASE ** (-jnp.arange(half, dtype=F32) / half)
    ang = pos.astype(F32)[:, None] * freq[None, :]
    cos = jnp.cos(ang)[None, :, None, :]
    sin = jnp.sin(ang)[None, :, None, :]
    x1, x2 = x[..., :half], x[..., half:]
    return jnp.concatenate([x1 * cos - x2 * sin, x1 * sin + x2 * cos], axis=-1)


def _heads(a, n_heads):
    return a.reshape(a.shape[0], a.shape[1], n_heads, HEAD_DIM)


def _to_chunks(a, L):
    b, t, h = a.shape[:3]
    a = a.reshape((b, t // L, L, h) + a.shape[3:])
    return jnp.moveaxis(a, (1, 3), (0, 2))


def _from_chunks(a):
    nc, b, h, L, d = a.shape
    return a.transpose(1, 0, 3, 2, 4).reshape(b, nc * L, h, d)


def sb_block(q, qpos, k, v, kpos, bias):
    z = jnp.einsum('bhqd,bhkd->bhqk', q, k).astype(F32) * (HEAD_DIM ** -0.5) + bias.astype(F32)[None, :, None, None]
    causal = kpos[None, :] < qpos[:, None]
    log_beta = jax.nn.log_sigmoid(z)
    log_rem = jnp.where(causal, log_beta - z, 0.0)
    log_w = log_beta + lax.cumsum(log_rem, axis=3, reverse=True) - log_rem
    w = jnp.where(causal, jnp.exp(log_w), 0.0)
    return jnp.einsum('bhqk,bhkd->bhqd', w.astype(v.dtype), v)


def stick_breaking(q, k, v, qpos, kpos, bias):
    qh, kh, vh = (a.transpose(0, 2, 1, 3) for a in (q, k, v))
    b, h, tq, d = qh.shape
    if tq % QBLK == 0 and tq > QBLK:
        nb = tq // QBLK
        qb = qh.reshape(b, h, nb, QBLK, d).transpose(2, 0, 1, 3, 4)
        pb = qpos.reshape(nb, QBLK)
        ob = lax.map(lambda a: sb_block(a[0], a[1], kh, vh, kpos, bias), (qb, pb))
        o = ob.transpose(1, 2, 0, 3, 4).reshape(b, h, tq, d)
    else:
        o = sb_block(qh, qpos, kh, vh, kpos, bias)
    return o.transpose(0, 2, 1, 3)


def mlstm(q, k, v, ig, fg, C0, n0, m0):
    t = q.shape[1]
    L = math.gcd(t, CHUNK)
    k = k * (HEAD_DIM ** -0.5)
    lf = jax.nn.log_sigmoid(fg)
    causal = jnp.tril(jnp.ones((L, L), bool))

    def step(carry, xs):
        C, n, m = carry
        qc, kc, vc, igc, lfc = xs
        bcum = jnp.cumsum(lfc, axis=-1)
        a = bcum + m[..., None]
        Dm = jnp.where(causal, bcum[..., :, None] - bcum[..., None, :] + igc[..., None, :], -jnp.inf)
        m_new = jnp.maximum(a, Dm.max(-1))
        Dw = jnp.exp(Dm - m_new[..., None])
        inter = jnp.exp(a - m_new)
        s = jnp.einsum('bhtd,bhsd->bhts', qc, kc) * Dw
        num = inter[..., None] * jnp.einsum('bhtk,bhkv->bhtv', qc, C) + jnp.einsum('bhts,bhsv->bhtv', s, vc)
        den = inter * jnp.einsum('bhtk,bhk->bht', qc, n) + s.sum(-1)
        h = num / jnp.maximum(jnp.abs(den), jnp.exp(-m_new))[..., None]
        mL = m_new[..., -1]
        wl = jnp.exp(bcum[..., -1:] - bcum + igc - mL[..., None])
        dl = jnp.exp(a[..., -1] - mL)
        C = dl[..., None, None] * C + jnp.einsum('bhs,bhsk,bhsv->bhkv', wl, kc, vc)
        n = dl[..., None] * n + jnp.einsum('bhs,bhsk->bhk', wl, kc)
        return (C, n, mL), h

    xs = tuple(_to_chunks(a, L) for a in (q, k, v, ig, lf))
    (C, n, m), hs = lax.scan(step, (C0.astype(F32), n0.astype(F32), m0.astype(F32)), xs)
    return _from_chunks(hs), (C, n, m)


def retention(q, k, v, S0):
    t = q.shape[1]
    L = math.gcd(t, CHUNK)
    log_g = jnp.log(1.0 - jnp.exp2(-5.0 - jnp.arange(RT_HEADS, dtype=F32)))
    tau = jnp.arange(L, dtype=F32)
    rel = tau[:, None] - tau[None, :]
    decay = jnp.where(rel >= 0, jnp.exp(log_g[:, None, None] * jnp.maximum(rel, 0.0)), 0.0)
    inter = jnp.exp(log_g[:, None] * (tau + 1.0))[..., None]
    wl = jnp.exp(log_g[:, None] * (L - 1.0 - tau))
    dl = jnp.exp(log_g * L)[:, None, None]
    k = k * (HEAD_DIM ** -0.5)

    def step(S, xs):
        qc, kc, vc = xs
        s = jnp.einsum('bhtd,bhsd->bhts', qc, kc) * decay
        o = inter * jnp.einsum('bhtk,bhkv->bhtv', qc, S) + jnp.einsum('bhts,bhsv->bhtv', s, vc)
        S = dl * S + jnp.einsum('hs,bhsk,bhsv->bhkv', wl, kc, vc)
        return S, o

    S, os_ = lax.scan(step, S0.astype(F32), tuple(_to_chunks(a, L) for a in (q, k, v)))
    return _from_chunks(os_), S


def s5_scan(u, h0_re, h0_im, A_re, A_im, log_dt, B_re, B_im, C_re, C_im, Dp):
    b, t, _ = u.shape
    ug = u.reshape(b, t, S5_GROUPS, S5_GROUP)
    lam = lax.complex(A_re.astype(F32), A_im.astype(F32))
    dt = jnp.exp(log_dt.astype(F32))
    a_bar = jnp.exp(lam * dt)
    b_bar = ((a_bar - 1.0) / lam)[..., None] * lax.complex(B_re.astype(F32), B_im.astype(F32))
    bu = jnp.einsum('gpc,btgc->btgp', b_bar, ug.astype(jnp.complex64))
    h0 = lax.complex(h0_re.astype(F32), h0_im.astype(F32))
    bu = bu.at[:, 0].add(a_bar * h0)

    def combine(e1, e2):
        return (e1[0] * e2[0], e2[0] * e1[1] + e2[1])

    _, hs = lax.associative_scan(combine, (jnp.broadcast_to(a_bar, bu.shape), bu), axis=1)
    c = lax.complex(C_re.astype(F32), C_im.astype(F32))
    y = jnp.einsum('gcp,btgp->btgc', c, hs).real + Dp.astype(F32).reshape(S5_GROUPS, S5_GROUP) * ug
    h_last = hs[:, -1]
    return y.reshape(b, t, S5_CH), h_last.real, h_last.imag


def token_mixers(x, pos, sb_past, ml_state, rt_state, s5_state, lw):
    bsz, t, _ = x.shape
    offs = np.cumsum(IN_SPLITS)[:-1].tolist()
    (sq, sk, sv, mq, mk, mv, mo, mi, mf, rq, rk, rv, rg, su) = jnp.split(x @ lw['w_in'], offs, axis=-1)
    k_new, v_new = _heads(sk, SB_HEADS), _heads(sv, SB_HEADS)
    if sb_past is None:
        k_all, v_all = k_new, v_new
    else:
        k_all = jnp.concatenate([sb_past[0].astype(k_new.dtype), k_new], axis=1)
        v_all = jnp.concatenate([sb_past[1].astype(v_new.dtype), v_new], axis=1)
    kpos = jnp.arange(k_all.shape[1], dtype=jnp.int32)
    o_sb = stick_breaking(_heads(sq, SB_HEADS), k_all, v_all, pos, kpos, lw['sb_bias']).reshape(bsz, t, GROUP_W)
    h_ml, ml_new = mlstm(_heads(mq, ML_HEADS).astype(F32), _heads(mk, ML_HEADS).astype(F32),
                         _heads(mv, ML_HEADS).astype(F32),
                         mi.astype(F32) + lw['ml_b_i'].astype(F32), mf.astype(F32) + lw['ml_b_f'].astype(F32),
                         ml_state[0], ml_state[1], ml_state[2])
    o_ml = head_norm(h_ml, lw['ml_norm_g']) * jax.nn.sigmoid(mo.astype(F32))
    h_rt, rt_new = retention(rope(_heads(rq, RT_HEADS).astype(F32), pos), rope(_heads(rk, RT_HEADS).astype(F32), pos),
                             _heads(rv, RT_HEADS).astype(F32), rt_state)
    o_rt = head_norm(h_rt, lw['rt_norm_g']) * jax.nn.silu(rg.astype(F32))
    y5, s5_re, s5_im = s5_scan(su.astype(F32), s5_state[0], s5_state[1], lw['s5_A_re'], lw['s5_A_im'],
                               lw['s5_log_dt'], lw['s5_B_re'], lw['s5_B_im'], lw['s5_C_re'], lw['s5_C_im'], lw['s5_D'])
    g5 = jax.nn.gelu(y5)
    o_s5 = g5 * jax.nn.sigmoid(g5 @ lw['s5_glu_w'].astype(F32) + lw['s5_glu_b'].astype(F32))
    mixed = jnp.concatenate([o_sb.astype(x.dtype), o_ml.astype(x.dtype), o_rt.astype(x.dtype), o_s5.astype(x.dtype)], axis=-1)
    new_state = (k_new, v_new, ml_new[0], ml_new[1], ml_new[2], rt_new, s5_re, s5_im)
    return mixed @ lw['w_out'], new_state


def memory_kv(mem, wk, wv):
    b, m, _ = mem.shape
    return (mem @ wk).reshape(b, m, CA_HEADS, CA_HD), (mem @ wv).reshape(b, m, CA_HEADS, CA_HD)


def cross_attn(x, mem_k, mem_v, wq, wo):
    b, t, _ = x.shape
    q = (x @ wq).reshape(b, t, CA_HEADS, CA_HD)
    s = jnp.einsum('bthd,bmhd->bhtm', q, mem_k.astype(q.dtype)).astype(F32) * (CA_HD ** -0.5)
    p = jax.nn.softmax(s, axis=-1).astype(x.dtype)
    o = jnp.einsum('bhtm,bmhd->bthd', p, mem_v.astype(x.dtype)).reshape(b, t, CA_W)
    return o @ wo


def swiglu(x, w_gate, w_up, w_down):
    return (jax.nn.silu(x @ w_gate) * (x @ w_up)) @ w_down


def moe_swiglu(x, router_w, router_b, w_gate, w_up, w_down):
    shp = x.shape
    xt = x.reshape(-1, shp[-1])
    n_slot = xt.shape[0] * TOP_K
    logits = (xt @ router_w).astype(F32) + router_b.astype(F32)
    top_val, top_idx = lax.top_k(logits, TOP_K)
    gates = jax.nn.softmax(top_val, axis=-1)
    flat_e = top_idx.reshape(-1)
    order = jnp.argsort(flat_e)
    sorted_e = flat_e[order]
    slot_tok = order // TOP_K
    slot_gate = gates.reshape(-1)[order]
    counts = jnp.bincount(flat_e, length=N_EXPERTS)
    padded = (counts + EXPERT_BLOCK - 1) // EXPERT_BLOCK * EXPERT_BLOCK
    start = jnp.cumsum(counts) - counts
    pend = jnp.cumsum(padded)
    dest = (pend - padded)[sorted_e] + jnp.arange(n_slot, dtype=jnp.int32) - start[sorted_e]
    n_blk = -(-n_slot // EXPERT_BLOCK) + N_EXPERTS
    x_pad = jnp.zeros((n_blk * EXPERT_BLOCK, shp[-1]), x.dtype).at[dest].set(xt[slot_tok])
    blk_e = jnp.minimum(jnp.searchsorted(pend, jnp.arange(n_blk, dtype=jnp.int32) * EXPERT_BLOCK, side='right'),
                        N_EXPERTS - 1)

    def expert_block(args):
        xb, e = args
        return (jax.nn.silu(xb @ w_gate[e]) * (xb @ w_up[e])) @ w_down[e]

    y_pad = lax.map(expert_block, (x_pad.reshape(n_blk, EXPERT_BLOCK, shp[-1]), blk_e))
    y_pad = y_pad.reshape(n_blk * EXPERT_BLOCK, shp[-1])
    y = jnp.zeros_like(xt).at[slot_tok].add(y_pad[dest] * slot_gate[:, None].astype(x.dtype))
    return y.reshape(shp)


def decoder_layer(x, pos, mem_k, mem_v, sb_past, ml_state, rt_state, s5_state, lw, ffn):
    h, st = token_mixers(x, pos, sb_past, ml_state, rt_state, s5_state, lw)
    x = layer_norm(DEEPNORM_ALPHA * x + h, lw['ln_g'][0], lw['ln_b'][0])
    x = layer_norm(DEEPNORM_ALPHA * x + cross_attn(x, mem_k, mem_v, lw['ca_wq'], lw['ca_wo']), lw['ln_g'][1], lw['ln_b'][1])
    x = layer_norm(DEEPNORM_ALPHA * x + ffn(x), lw['ln_g'][2], lw['ln_b'][2])
    return x, st


def setup_inputs(seed: int = 0) -> dict:
    key = jax.random.key(seed)
    ks = iter(jax.random.split(key, 64))

    def nrm(shape, scale=1.0):
        return jax.random.normal(next(ks), shape, F32) * scale

    n_pages = PAST_LEN // PAGE_SIZE
    n_phys = (5 * DEC_BATCH * n_pages) // 4
    inp = {}
    inp['x_prompt'] = nrm((BATCH, SEQ, D_MODEL))
    inp['x_sample'] = nrm((DEC_BATCH, DEC_SEQ, D_MODEL))
    inp['cache_sb_k'] = nrm((n_phys, DEPTH, PAGE_SIZE, SB_HEADS, HEAD_DIM))
    inp['cache_sb_v'] = nrm((n_phys, DEPTH, PAGE_SIZE, SB_HEADS, HEAD_DIM))
    inp['cache_mem_k'] = nrm((DEC_BATCH, DEPTH, N_MEM, CA_HEADS, CA_HD))
    inp['cache_mem_v'] = nrm((DEC_BATCH, DEPTH, N_MEM, CA_HEADS, CA_HD))
    inp['state_ml_C'] = nrm((DEC_BATCH, DEPTH, ML_HEADS, HEAD_DIM, HEAD_DIM), 0.3)
    inp['state_ml_n'] = nrm((DEC_BATCH, DEPTH, ML_HEADS, HEAD_DIM), 0.3)
    inp['state_ml_m'] = nrm((DEC_BATCH, DEPTH, ML_HEADS))
    inp['state_rt_S'] = nrm((DEC_BATCH, DEPTH, RT_HEADS, HEAD_DIM, HEAD_DIM))
    inp['state_s5_re'] = nrm((DEC_BATCH, DEPTH, S5_GROUPS, S5_STATE), 0.5)
    inp['state_s5_im'] = nrm((DEC_BATCH, DEPTH, S5_GROUPS, S5_STATE), 0.5)
    inp['page_table'] = jax.random.permutation(next(ks), n_phys)[:DEC_BATCH * n_pages].reshape(DEC_BATCH, n_pages).astype(jnp.int32)
    inp['mem_prompt'] = nrm((BATCH, N_MEM, D_MODEL))
    inp['w_in'] = nrm((DEPTH, D_MODEL, IN_W), D_MODEL ** -0.5)
    inp['sb_bias'] = jnp.linspace(-4.0, -10.0, SB_HEADS, dtype=F32)[None, :] + nrm((DEPTH, SB_HEADS), 0.1)
    inp['ml_b_i'] = nrm((DEPTH, ML_HEADS), 0.1)
    inp['ml_b_f'] = jnp.linspace(3.0, 6.0, ML_HEADS, dtype=F32)[None, :] + nrm((DEPTH, ML_HEADS), 0.1)
    inp['ml_norm_g'] = 1.0 + nrm((DEPTH, GROUP_W), 0.02)
    inp['rt_norm_g'] = 1.0 + nrm((DEPTH, GROUP_W), 0.02)
    inp['s5_A_re'] = -0.5 + nrm((DEPTH, S5_GROUPS, S5_STATE), 0.01)
    inp['s5_A_im'] = np.pi * jnp.arange(S5_STATE, dtype=F32)[None, None, :] + nrm((DEPTH, S5_GROUPS, S5_STATE), 0.01)
    inp['s5_log_dt'] = jax.random.uniform(next(ks), (DEPTH, S5_GROUPS, S5_STATE), F32, math.log(1e-3), math.log(1e-1))
    inp['s5_B_re'] = nrm((DEPTH, S5_GROUPS, S5_STATE, S5_GROUP), (2 * S5_GROUP) ** -0.5)
    inp['s5_B_im'] = nrm((DEPTH, S5_GROUPS, S5_STATE, S5_GROUP), (2 * S5_GROUP) ** -0.5)
    inp['s5_C_re'] = nrm((DEPTH, S5_GROUPS, S5_GROUP, S5_STATE), 0.5)
    inp['s5_C_im'] = nrm((DEPTH, S5_GROUPS, S5_GROUP, S5_STATE), 0.5)
    inp['s5_D'] = nrm((DEPTH, S5_CH))
    inp['s5_glu_w'] = nrm((DEPTH, S5_CH, S5_CH), S5_CH ** -0.5)
    inp['s5_glu_b'] = nrm((DEPTH, S5_CH), 0.02)
    inp['w_out'] = nrm((DEPTH, MIX_W, D_MODEL), MIX_W ** -0.5 * DEEPNORM_BETA)
    inp['ca_wq'] = nrm((DEPTH, D_MODEL, CA_W), D_MODEL ** -0.5)
    inp['ca_wk'] = nrm((DEPTH, D_MODEL, CA_W), D_MODEL ** -0.5)
    inp['ca_wv'] = nrm((DEPTH, D_MODEL, CA_W), D_MODEL ** -0.5)
    inp['ca_wo'] = nrm((DEPTH, CA_W, D_MODEL), CA_W ** -0.5 * DEEPNORM_BETA)
    inp['ln_g'] = 1.0 + nrm((DEPTH, 3, D_MODEL), 0.02)
    inp['ln_b'] = nrm((DEPTH, 3, D_MODEL), 0.02)
    inp['ffn_w_gate'] = nrm((N_DENSE, D_MODEL, D_FF), D_MODEL ** -0.5)
    inp['ffn_w_up'] = nrm((N_DENSE, D_MODEL, D_FF), D_MODEL ** -0.5)
    inp['ffn_w_down'] = nrm((N_DENSE, D_FF, D_MODEL), D_FF ** -0.5 * DEEPNORM_BETA)
    inp['moe_router_w'] = nrm((N_MOE, D_MODEL, N_EXPERTS), D_MODEL ** -0.5)
    inp['moe_router_b'] = nrm((N_MOE, N_EXPERTS), 0.01)
    inp['moe_w_gate'] = nrm((N_MOE, N_EXPERTS, D_MODEL, D_FF), D_MODEL ** -0.5)
    inp['moe_w_up'] = nrm((N_MOE, N_EXPERTS, D_MODEL, D_FF), D_MODEL ** -0.5)
    inp['moe_w_down'] = nrm((N_MOE, N_EXPERTS, D_FF, D_MODEL), D_FF ** -0.5 * DEEPNORM_BETA)
    return inp


def reference(x_prompt, x_sample, cache_sb_k, cache_sb_v, cache_mem_k, cache_mem_v, state_ml_C, state_ml_n,
              state_ml_m, state_rt_S, state_s5_re, state_s5_im, page_table, mem_prompt, w_in, sb_bias, ml_b_i, ml_b_f,
              ml_norm_g, rt_norm_g, s5_A_re, s5_A_im, s5_log_dt, s5_B_re, s5_B_im, s5_C_re, s5_C_im, s5_D,
              s5_glu_w, s5_glu_b, w_out, ca_wq, ca_wk, ca_wv, ca_wo, ln_g, ln_b, ffn_w_gate, ffn_w_up, ffn_w_down,
              moe_router_w, moe_router_b, moe_w_gate, moe_w_up, moe_w_down):
    bp, tp, _ = x_prompt.shape
    bs, ts, _ = x_sample.shape
    n_pages = page_table.shape[1]
    pos_p = jnp.arange(tp, dtype=jnp.int32)
    pos_s = PAST_LEN + jnp.arange(ts, dtype=jnp.int32)
    zC = jnp.zeros((bp, ML_HEADS, HEAD_DIM, HEAD_DIM), F32)
    zn = jnp.zeros((bp, ML_HEADS, HEAD_DIM), F32)
    zm = jnp.zeros((bp, ML_HEADS), F32)
    zS = jnp.zeros((bp, RT_HEADS, HEAD_DIM, HEAD_DIM), F32)
    z5 = jnp.zeros((bp, S5_GROUPS, S5_STATE), F32)

    yp, ys = x_prompt, x_sample
    p_st = [[] for _ in range(8)]
    s_st = [[] for _ in range(8)]
    p_mk, p_mv = [], []
    for l in range(DEPTH):
        lw = dict(w_in=w_in[l], sb_bias=sb_bias[l], ml_b_i=ml_b_i[l], ml_b_f=ml_b_f[l], ml_norm_g=ml_norm_g[l],
                  rt_norm_g=rt_norm_g[l], s5_A_re=s5_A_re[l], s5_A_im=s5_A_im[l], s5_log_dt=s5_log_dt[l],
                  s5_B_re=s5_B_re[l], s5_B_im=s5_B_im[l], s5_C_re=s5_C_re[l], s5_C_im=s5_C_im[l], s5_D=s5_D[l],
                  s5_glu_w=s5_glu_w[l], s5_glu_b=s5_glu_b[l], w_out=w_out[l], ca_wq=ca_wq[l], ca_wo=ca_wo[l],
                  ln_g=ln_g[l], ln_b=ln_b[l])
        j = l // 2
        if l % 2 == 0:
            ffn = functools.partial(swiglu, w_gate=ffn_w_gate[j], w_up=ffn_w_up[j], w_down=ffn_w_down[j])
        else:
            ffn = functools.partial(moe_swiglu, router_w=moe_router_w[j], router_b=moe_router_b[j],
                                    w_gate=moe_w_gate[j], w_up=moe_w_up[j], w_down=moe_w_down[j])
        mk_p, mv_p = memory_kv(mem_prompt, ca_wk[l], ca_wv[l])
        yp, stp = decoder_layer(yp, pos_p, mk_p, mv_p, None, (zC, zn, zm), zS, (z5, z5), lw, ffn)
        p_mk.append(mk_p)
        p_mv.append(mv_p)
        past_k = cache_sb_k[page_table, l].reshape(bs, n_pages * PAGE_SIZE, SB_HEADS, HEAD_DIM)
        past_v = cache_sb_v[page_table, l].reshape(bs, n_pages * PAGE_SIZE, SB_HEADS, HEAD_DIM)
        ys, sts = decoder_layer(ys, pos_s, cache_mem_k[:, l], cache_mem_v[:, l], (past_k, past_v),
                                (state_ml_C[:, l], state_ml_n[:, l], state_ml_m[:, l]), state_rt_S[:, l],
                                (state_s5_re[:, l], state_s5_im[:, l]), lw, ffn)
        for i in range(8):
            p_st[i].append(stp[i])
            s_st[i].append(sts[i])

    p_sb_k, p_sb_v, p_ml_C, p_ml_n, p_ml_m, p_rt_S, p_s5_re, p_s5_im = [jnp.stack(a, axis=1) for a in p_st]
    s_sb_k, s_sb_v, s_ml_C, s_ml_n, s_ml_m, s_rt_S, s_s5_re, s_s5_im = [jnp.stack(a, axis=1) for a in s_st]
    p_mem_k = jnp.stack(p_mk, axis=1)
    p_mem_v = jnp.stack(p_mv, axis=1)
    return (yp, ys, p_sb_k, p_sb_v, p_mem_k, p_mem_v, p_ml_C, p_ml_n, p_ml_m, p_rt_S, p_s5_re, p_s5_im,
            s_sb_k, s_sb_v, s_ml_C, s_ml_n, s_ml_m, s_rt_S, s_s5_re, s_s5_im)
```

```python
import functools
import math

import numpy as np
import jax
import jax.numpy as jnp
from jax import lax
from jax.experimental import pallas as pl
from jax.experimental.pallas import tpu as pltpu

F32 = jnp.float32
BF16 = jnp.bfloat16

D_MODEL = 1024
DEPTH = 2
PAST_LEN = 8192
PAGE_SIZE = 128
HEAD_DIM = 64
N_HEADS = 4
GROUP_W = N_HEADS * HEAD_DIM
S5_GROUPS = 16
S5_GROUP = 16
S5_STATE = 64
S5_W = S5_GROUPS * S5_STATE
N_MEM = 256
D_FF = 3584
N_EXPERTS = 8
TOP_K = 2
CHUNK = 64
ROPE_BASE = 10000.0
LN_EPS = 1e-5
GN_EPS = 1e-6
ALPHA = (2 * DEPTH) ** 0.25
QK_SCALE = HEAD_DIM ** -0.5

LANES = 128
PROJ_W = 25 * LANES
C_SQ, C_SK, C_SV, C_MQ, C_MK, C_MV, C_MO, C_RQ, C_RK, C_RV, C_RG, C_SU = range(12)
C_GATES = 12 * GROUP_W // LANES
VMEM_LIMIT = 48 * 1024 * 1024
MOE_TILE = 512


def _cp(*sem):
    return pltpu.CompilerParams(dimension_semantics=sem, vmem_limit_bytes=VMEM_LIMIT)


def _dot(a, b):
    return jnp.dot(a, b, preferred_element_type=F32)


def _dot_nt(a, b):
    return lax.dot_general(a, b, (((1,), (1,)), ((), ())), preferred_element_type=F32)


def _dot_tn(a, b):
    return lax.dot_general(a, b, (((0,), (0,)), ((), ())), preferred_element_type=F32)


def _layer_norm(y, g, b):
    mu = jnp.mean(y, axis=-1, keepdims=True)
    yc = y - mu
    var = jnp.mean(yc * yc, axis=-1, keepdims=True)
    return yc * lax.rsqrt(var + LN_EPS) * g + b


def _head_norm(h):
    mu = jnp.mean(h, axis=-1, keepdims=True)
    hc = h - mu
    var = jnp.mean(hc * hc, axis=-1, keepdims=True)
    return hc * lax.rsqrt(var + GN_EPS)


def _neg_softplus(z):
    return -(jnp.maximum(z, 0.0) + jnp.log1p(jnp.exp(-jnp.abs(z))))


def _log_sigmoid(z):
    return _neg_softplus(-z)


def _linear_kernel(x_ref, w_ref, o_ref):
    o_ref[...] = _dot(x_ref[...].astype(BF16), w_ref[...]).astype(o_ref.dtype)


def linear(x, w, tm, out_dtype=F32):
    m, k = x.shape
    n = w.shape[1]
    return pl.pallas_call(
        _linear_kernel, grid=(m // tm,),
        in_specs=[pl.BlockSpec((tm, k), lambda i: (i, 0)), pl.BlockSpec((k, n), lambda i: (0, 0))],
        out_specs=pl.BlockSpec((tm, n), lambda i: (i, 0)),
        out_shape=jax.ShapeDtypeStruct((m, n), out_dtype),
        compiler_params=_cp("arbitrary"), name="linear")(x, w)


def _linear_res_ln_kernel(x_ref, w_ref, r_ref, g_ref, b_ref, o_ref):
    h = _dot(x_ref[...].astype(BF16), w_ref[...])
    o_ref[...] = _layer_norm(ALPHA * r_ref[...] + h, g_ref[...], b_ref[...])


def linear_res_ln(x, w, res, g, b, tm):
    m, k = x.shape
    n = w.shape[1]
    return pl.pallas_call(
        _linear_res_ln_kernel, grid=(m // tm,),
        in_specs=[pl.BlockSpec((tm, k), lambda i: (i, 0)), pl.BlockSpec((k, n), lambda i: (0, 0)),
                  pl.BlockSpec((tm, n), lambda i: (i, 0)),
                  pl.BlockSpec((1, n), lambda i: (0, 0)), pl.BlockSpec((1, n), lambda i: (0, 0))],
        out_specs=pl.BlockSpec((tm, n), lambda i: (i, 0)),
        out_shape=jax.ShapeDtypeStruct((m, n), F32),
        compiler_params=_cp("arbitrary"), name="linear_res_ln")(x, w, res, g, b)


def _suffix_matrix():
    j = np.arange(LANES)
    u = (j[:, None] >= j[None, :]).astype(np.float32)
    uu = np.concatenate([u, np.ones((LANES, LANES), np.float32)], axis=1)
    return jnp.asarray(np.concatenate([uu, uu], axis=0), dtype=BF16)


def _suffix_sums(lr, uu):
    hi = lr.astype(BF16)
    lo = (lr - hi.astype(F32)).astype(BF16)
    r = _dot(jnp.concatenate([hi, lo], axis=1), uu)
    return r[:, :LANES], r[:, LANES:]


def _sb_prompt_kernel(bias_ref, q_ref, k_ref, v_ref, uu_ref, o_ref, acc_ref, car_ref, *, tq):
    i = pl.program_id(1)
    tk = LANES
    nsub = tq // tk
    acc_ref[...] = jnp.zeros_like(acc_ref)
    car_ref[...] = jnp.zeros_like(car_ref)
    q = (q_ref[...] * QK_SCALE).astype(BF16)
    qh = [q[:, h * HEAD_DIM:(h + 1) * HEAD_DIM] for h in range(N_HEADS)]
    uu = uu_ref[...]
    row = lax.broadcasted_iota(jnp.int32, (tq, tk), 0) + i * tq
    col0 = lax.broadcasted_iota(jnp.int32, (tq, tk), 1)

    def body(jj, carry):
        j = (i + 1) * nsub - 1 - jj
        r0 = pl.multiple_of(j * tk, tk)
        kb = k_ref[pl.ds(r0, tk), :].astype(BF16)
        vb = v_ref[pl.ds(r0, tk), :].astype(BF16)
        causal = (col0 + j * tk) < row
        for h in range(N_HEADS):
            sl = slice(h * HEAD_DIM, (h + 1) * HEAD_DIM)
            z = _dot_nt(qh[h], kb[:, sl]) + bias_ref[h]
            lr_raw = _neg_softplus(z)
            lb = z + lr_raw
            lr = jnp.where(causal, lr_raw, 0.0)
            cs, tot = _suffix_sums(lr, uu)
            car = car_ref[h]
            w = jnp.where(causal, jnp.exp(lb + (cs - lr) + car), 0.0)
            acc_ref[h] += _dot(w.astype(BF16), vb[:, sl])
            car_ref[h] = car + tot
        return carry

    lax.fori_loop(0, (i + 1) * nsub, body, 0)
    o_ref[...] = jnp.concatenate([acc_ref[h] for h in range(N_HEADS)], axis=1)


def sb_attention_prompt(proj, sb_bias, uu, n_batch, seq, tq):
    nq = seq // tq
    kern = functools.partial(_sb_prompt_kernel, tq=tq)
    return pl.pallas_call(
        kern, grid=(n_batch, nq),
        in_specs=[pl.BlockSpec(memory_space=pltpu.SMEM),
                  pl.BlockSpec((tq, GROUP_W), lambda b, i: (b * nq + i, C_SQ)),
                  pl.BlockSpec((seq, GROUP_W), lambda b, i: (b, C_SK)),
                  pl.BlockSpec((seq, GROUP_W), lambda b, i: (b, C_SV)),
                  pl.BlockSpec((2 * LANES, 2 * LANES), lambda b, i: (0, 0))],
        out_specs=pl.BlockSpec((tq, GROUP_W), lambda b, i: (b * nq + i, 0)),
        out_shape=jax.ShapeDtypeStruct((n_batch * seq, GROUP_W), F32),
        scratch_shapes=[pltpu.VMEM((N_HEADS, tq, HEAD_DIM), F32), pltpu.VMEM((N_HEADS, tq, LANES), F32)],
        compiler_params=_cp("arbitrary", "arbitrary"), name="sb_prompt")(sb_bias, proj, proj, proj, uu)


def _sb_sample_kernel(pt_ref, bias_ref, q_ref, kn_ref, vn_ref, u8_ref, uu_ref, *rest, n_pp, n_steps):
    k_refs = rest[:n_pp]
    v_refs = rest[n_pp:2 * n_pp]
    o_ref = rest[2 * n_pp]
    acc_ref, car_ref = rest[2 * n_pp + 1:]
    s = pl.program_id(1)
    nq = q_ref.shape[0]
    q = (q_ref[...] * QK_SCALE).astype(BF16)
    qh = [q[:, h * HEAD_DIM:(h + 1) * HEAD_DIM] for h in range(N_HEADS)]

    @pl.when(s == 0)
    def _():
        kn = kn_ref[...].astype(BF16)
        vn = vn_ref[...].astype(BF16)
        t = lax.broadcasted_iota(jnp.int32, (nq, nq), 0)
        c = lax.broadcasted_iota(jnp.int32, (nq, nq), 1)
        causal = c < t
        for h in range(N_HEADS):
            sl = slice(h * HEAD_DIM, (h + 1) * HEAD_DIM)
            z = _dot_nt(qh[h], kn[:, sl]) + bias_ref[h]
            lr_raw = _neg_softplus(z)
            lb = z + lr_raw
            lr = jnp.where(causal, lr_raw, 0.0)
            cs = jnp.dot(lr, u8_ref[...], preferred_element_type=F32, precision=lax.Precision.HIGHEST)
            w = jnp.where(causal, jnp.exp(lb + (cs - lr)), 0.0)
            acc_ref[h] = _dot(w.astype(BF16), vn[:, sl])
            car_ref[h] = jnp.broadcast_to(jnp.sum(lr, axis=1, keepdims=True), (nq, LANES))

    uu = uu_ref[...]
    lrs, lbs = [], []
    for p in range(n_pp):
        for h in range(N_HEADS):
            z = _dot(qh[h], k_refs[p][0, 0, h].astype(BF16)) + bias_ref[h]
            lr = _neg_softplus(z)
            lrs.append(lr)
            lbs.append(z + lr)
    cs_all, tot_all = _suffix_sums(jnp.concatenate(lrs, axis=0), uu)
    for p in range(n_pp):
        for h in range(N_HEADS):
            r = (p * N_HEADS + h) * nq
            car = car_ref[h]
            w = jnp.exp(lbs[p * N_HEADS + h] + (cs_all[r:r + nq] - lrs[p * N_HEADS + h]) + car)
            acc_ref[h] += _dot_nt(w.astype(BF16), v_refs[p][0, 0, h].astype(BF16))
            car_ref[h] = car + tot_all[r:r + nq]

    @pl.when(s == n_steps - 1)
    def _():
        o_ref[...] = jnp.concatenate([acc_ref[h] for h in range(N_HEADS)], axis=1)


def sb_attention_sample(proj, row_blk0, cache_kt, cache_vt, page_table, layer, sb_bias, uu, n_pp):
    n_batch, n_pages = page_table.shape
    nq = 8
    n_steps = n_pages // n_pp
    u8 = jnp.asarray((np.arange(nq)[:, None] >= np.arange(nq)[None, :]).astype(np.float32))

    def page_spec(p):
        return pl.BlockSpec((1, 1, N_HEADS, HEAD_DIM, PAGE_SIZE),
                            lambda b, s, pt: (pt[b, n_pages - 1 - (s * n_pp + p)], layer, 0, 0, 0))

    def row_spec(cblk):
        return pl.BlockSpec((nq, GROUP_W), lambda b, s, pt: (row_blk0 + b, cblk))

    kern = functools.partial(_sb_sample_kernel, n_pp=n_pp, n_steps=n_steps)
    gs = pltpu.PrefetchScalarGridSpec(
        num_scalar_prefetch=1, grid=(n_batch, n_steps),
        in_specs=[pl.BlockSpec(memory_space=pltpu.SMEM), row_spec(C_SQ), row_spec(C_SK), row_spec(C_SV),
                  pl.BlockSpec((nq, nq), lambda b, s, pt: (0, 0)),
                  pl.BlockSpec((2 * LANES, 2 * LANES), lambda b, s, pt: (0, 0))]
                 + [page_spec(p) for p in range(n_pp)] * 2,
        out_specs=pl.BlockSpec((nq, GROUP_W), lambda b, s, pt: (b, 0)),
        scratch_shapes=[pltpu.VMEM((N_HEADS, nq, HEAD_DIM), F32), pltpu.VMEM((N_HEADS, nq, LANES), F32)])
    return pl.pallas_call(
        kern, grid_spec=gs, out_shape=jax.ShapeDtypeStruct((n_batch * nq, GROUP_W), F32),
        compiler_params=_cp("arbitrary", "arbitrary"), name="sb_sample")(
            page_table, sb_bias, proj, proj, proj, u8, uu, *([cache_kt] * n_pp), *([cache_vt] * n_pp))


def _mlstm_kernel(m0_ref, q_ref, k_ref, v_ref, og_ref, gt_ref, gb_ref, ng_ref, tril_ref, cn0_ref,
                  o_ref, cn_ref, m_ref, *, seq, chunk):
    b = pl.program_id(0)
    nc = seq // chunk
    cn_ref[...] = cn0_ref[...]
    tril = tril_ref[...]
    tri_mask = lax.broadcasted_iota(jnp.int32, (chunk, chunk), 1) <= lax.broadcasted_iota(jnp.int32, (chunk, chunk), 0)
    lane = lax.broadcasted_iota(jnp.int32, (chunk, HEAD_DIM), 1)
    ones_col = jnp.where(lane == 0, 1.0, 0.0).astype(F32)

    def body(c, ms):
        r0 = pl.multiple_of(c * chunk, chunk)
        gt = gt_ref[pl.ds(r0, chunk), :] + gb_ref[...]
        bc = jnp.dot(tril, _log_sigmoid(gt), preferred_element_type=F32, precision=lax.Precision.HIGHEST)
        gt_t = gt.T
        bc_t = bc.T
        q = q_ref[pl.ds(r0, chunk), :].astype(BF16)
        k = (k_ref[pl.ds(r0, chunk), :] * QK_SCALE).astype(BF16)
        v = v_ref[pl.ds(r0, chunk), :]
        og = og_ref[pl.ds(r0, chunk), :]
        outs, new_ms = [], []
        for h in range(N_HEADS):
            sl = slice(h * HEAD_DIM, (h + 1) * HEAD_DIM)
            ig_col = gt[:, h:h + 1]
            bc_col = bc[:, N_HEADS + h:N_HEADS + h + 1]
            g_row = bc_t[N_HEADS + h:N_HEADS + h + 1, :] - gt_t[h:h + 1, :]
            dm = jnp.where(tri_mask, bc_col - g_row, -jnp.inf)
            a = bc_col + ms[h]
            m_new = jnp.maximum(a, jnp.max(dm, axis=1, keepdims=True))
            dw = jnp.exp(dm - m_new)
            inter = jnp.exp(a - m_new)
            s = _dot_nt(q[:, sl], k[:, sl]) * dw
            v_ext = jnp.concatenate([v[:, sl], ones_col], axis=1)
            qc = _dot(q[:, sl], cn_ref[0, h].astype(BF16))
            sv = _dot(s.astype(BF16), v_ext.astype(BF16))
            num = inter * qc[:, :HEAD_DIM] + sv[:, :HEAD_DIM]
            den = inter * qc[:, HEAD_DIM:HEAD_DIM + 1] + jnp.sum(s, axis=1, keepdims=True)
            hh = num / jnp.maximum(jnp.abs(den), jnp.exp(-m_new))
            m_last = m_new[chunk - 1:chunk, :]
            wl = jnp.exp(bc_col[chunk - 1:chunk, :] - bc_col + ig_col - m_last)
            dl = jnp.exp(a[chunk - 1:chunk, :] - m_last)
            cn_ref[0, h] = dl * cn_ref[0, h] + _dot_tn(k[:, sl], (wl * v_ext).astype(BF16))
            new_ms.append(m_last)
            outs.append(_head_norm(hh))
        y = jnp.concatenate(outs, axis=1) * ng_ref[...] * jax.nn.sigmoid(og)
        o_ref[pl.ds(r0, chunk), :] = y
        return tuple(new_ms)

    ms0 = tuple(jnp.full((1, 1), m0_ref[b, h], F32) for h in range(N_HEADS))
    ms = lax.fori_loop(0, nc, body, ms0)
    lane_m = lax.broadcasted_iota(jnp.int32, (1, LANES), 1)
    m_out = jnp.zeros((1, LANES), F32)
    for h in range(N_HEADS):
        m_out = jnp.where(lane_m == h, ms[h], m_out)
    m_ref[0] = m_out


def mlstm_mixer(proj, row_blk0, n_batch, seq, gate_bias, norm_g, cn0, m0):
    chunk = math.gcd(seq, CHUNK)
    tril = jnp.asarray(np.tril(np.ones((chunk, chunk), np.float32)))

    def row_spec(cblk, w=GROUP_W):
        return pl.BlockSpec((seq, w), lambda b: (row_blk0 + b, cblk))

    kern = functools.partial(_mlstm_kernel, seq=seq, chunk=chunk)
    return pl.pallas_call(
        kern, grid=(n_batch,),
        in_specs=[pl.BlockSpec(memory_space=pltpu.SMEM),
                  row_spec(C_MQ), row_spec(C_MK), row_spec(C_MV), row_spec(C_MO), row_spec(C_GATES, LANES),
                  pl.BlockSpec((1, LANES), lambda b: (0, 0)), pl.BlockSpec((1, GROUP_W), lambda b: (0, 0)),
                  pl.BlockSpec((chunk, chunk), lambda b: (0, 0)),
                  pl.BlockSpec((1, N_HEADS, HEAD_DIM, LANES), lambda b: (b, 0, 0, 0))],
        out_specs=[pl.BlockSpec((seq, GROUP_W), lambda b: (b, 0)),
                   pl.BlockSpec((1, N_HEADS, HEAD_DIM, LANES), lambda b: (b, 0, 0, 0)),
                   pl.BlockSpec((1, 1, LANES), lambda b: (b, 0, 0))],
        out_shape=[jax.ShapeDtypeStruct((n_batch * seq, GROUP_W), F32),
                   jax.ShapeDtypeStruct((n_batch, N_HEADS, HEAD_DIM, LANES), F32),
                   jax.ShapeDtypeStruct((n_batch, 1, LANES), F32)],
        compiler_params=_cp("arbitrary"), name="mlstm")(
            m0, proj, proj, proj, proj, proj, gate_bias, norm_g, tril, cn0)


def _rope(x, cos, sin_signed):
    lane = lax.broadcasted_iota(jnp.int32, x.shape, 1)
    half = HEAD_DIM // 2
    swapped = jnp.where((lane % HEAD_DIM) < half, pltpu.roll(x, x.shape[1] - half, 1), pltpu.roll(x, half, 1))
    return x * cos + swapped * sin_signed


def _retention_kernel(q_ref, k_ref, v_ref, gg_ref, cos_ref, sin_ref, ng_ref, dec_ref, int_ref, wl_ref, dl_ref,
                      s0_ref, o_ref, s_ref, *, seq, chunk):
    nc = seq // chunk
    s_ref[...] = s0_ref[...]

    def body(c, carry):
        r0 = pl.multiple_of(c * chunk, chunk)
        cos = cos_ref[pl.ds(r0, chunk), :]
        sin = sin_ref[pl.ds(r0, chunk), :]
        q = _rope(q_ref[pl.ds(r0, chunk), :], cos, sin).astype(BF16)
        k = (_rope(k_ref[pl.ds(r0, chunk), :], cos, sin) * QK_SCALE).astype(BF16)
        v = v_ref[pl.ds(r0, chunk), :]
        gg = gg_ref[pl.ds(r0, chunk), :]
        outs = []
        for h in range(N_HEADS):
            sl = slice(h * HEAD_DIM, (h + 1) * HEAD_DIM)
            s = _dot_nt(q[:, sl], k[:, sl]) * dec_ref[h]
            st = s_ref[0, h]
            o = int_ref[h] * _dot(q[:, sl], st.astype(BF16)) + _dot(s.astype(BF16), v[:, sl].astype(BF16))
            s_ref[0, h] = dl_ref[h] * st + _dot_tn(k[:, sl], (wl_ref[h] * v[:, sl]).astype(BF16))
            outs.append(_head_norm(o))
        o_ref[pl.ds(r0, chunk), :] = jnp.concatenate(outs, axis=1) * ng_ref[...] * (gg * jax.nn.sigmoid(gg))
        return carry

    lax.fori_loop(0, nc, body, 0)


def _retention_consts(chunk):
    log_g = np.log(1.0 - np.exp2(-5.0 - np.arange(N_HEADS, dtype=np.float64)))
    tau = np.arange(chunk, dtype=np.float64)
    rel = tau[:, None] - tau[None, :]
    decay = np.where(rel >= 0, np.exp(log_g[:, None, None] * np.maximum(rel, 0.0)), 0.0)
    inter = np.exp(log_g[:, None] * (tau + 1.0))[..., None]
    wl = np.exp(log_g[:, None] * (chunk - 1.0 - tau))[..., None]
    dl = np.exp(log_g * chunk)[:, None, None]
    return tuple(jnp.asarray(a, F32) for a in (decay, inter, wl, dl))


def retention_mixer(proj, row_blk0, n_batch, seq, cos, sin_signed, norm_g, s0):
    chunk = math.gcd(seq, CHUNK)
    dec, inter, wl, dl = _retention_consts(chunk)

    def row_spec(cblk):
        return pl.BlockSpec((seq, GROUP_W), lambda b: (row_blk0 + b, cblk))

    def const_spec(shape):
        return pl.BlockSpec(shape, lambda b: (0,) * len(shape))

    kern = functools.partial(_retention_kernel, seq=seq, chunk=chunk)
    return pl.pallas_call(
        kern, grid=(n_batch,),
        in_specs=[row_spec(C_RQ), row_spec(C_RK), row_spec(C_RV), row_spec(C_RG),
                  const_spec((seq, GROUP_W)), const_spec((seq, GROUP_W)), const_spec((1, GROUP_W)),
                  const_spec(dec.shape), const_spec(inter.shape), const_spec(wl.shape), const_spec(dl.shape),
                  pl.BlockSpec((1, N_HEADS, HEAD_DIM, HEAD_DIM), lambda b: (b, 0, 0, 0))],
        out_specs=[pl.BlockSpec((seq, GROUP_W), lambda b: (b, 0)),
                   pl.BlockSpec((1, N_HEADS, HEAD_DIM, HEAD_DIM), lambda b: (b, 0, 0, 0))],
        out_shape=[jax.ShapeDtypeStruct((n_batch * seq, GROUP_W), F32),
                   jax.ShapeDtypeStruct((n_batch, N_HEADS, HEAD_DIM, HEAD_DIM), F32)],
        compiler_params=_cp("arbitrary"), name="retention")(
            proj, proj, proj, proj, cos, sin_signed, norm_g, dec, inter, wl, dl, s0)


def _s5_kernel(u_ref, wb_ref, a1_ref, a2_ref, h0_ref, wc_ref, d_ref, gw_ref, gb_ref, o_ref, hl_ref, hs_ref, *, nb, tt):
    c = pl.program_id(0)

    @pl.when(c == 0)
    def _():
        hl_ref[...] = h0_ref[...]

    u = u_ref[...]
    hs_ref[...] = _dot(u.astype(BF16), wb_ref[...])
    a1 = jnp.broadcast_to(a1_ref[...], (nb, 2 * S5_W))
    a2 = jnp.broadcast_to(a2_ref[...], (nb, 2 * S5_W))

    def step(t, h):
        r0 = pl.multiple_of(t * nb, nb)
        swapped = jnp.concatenate([h[:, S5_W:], h[:, :S5_W]], axis=1)
        h = a1 * h + a2 * swapped + hs_ref[pl.ds(r0, nb), :]
        hs_ref[pl.ds(r0, nb), :] = h
        return h

    hl_ref[...] = lax.fori_loop(0, tt, step, hl_ref[...])
    y = _dot(hs_ref[...].astype(BF16), wc_ref[...]) + d_ref[...] * u
    g5 = jax.nn.gelu(y)
    o_ref[...] = g5 * jax.nn.sigmoid(_dot(g5.astype(BF16), gw_ref[...]) + gb_ref[...])


def s5_mixer(u_tm, nb, seq, tt, wb, a1, a2, h0, wc, d, glu_w, glu_b):
    rows = tt * nb

    def const_spec(shape):
        return pl.BlockSpec(shape, lambda c: (0,) * len(shape))

    kern = functools.partial(_s5_kernel, nb=nb, tt=tt)
    return pl.pallas_call(
        kern, grid=(seq // tt,),
        in_specs=[pl.BlockSpec((rows, GROUP_W), lambda c: (c, 0)),
                  const_spec(wb.shape), const_spec(a1.shape), const_spec(a2.shape), const_spec(h0.shape),
                  const_spec(wc.shape), const_spec(d.shape), const_spec(glu_w.shape), const_spec(glu_b.shape)],
        out_specs=[pl.BlockSpec((rows, GROUP_W), lambda c: (c, 0)), const_spec(h0.shape)],
        out_shape=[jax.ShapeDtypeStruct((seq * nb, GROUP_W), F32), jax.ShapeDtypeStruct(h0.shape, F32)],
        scratch_shapes=[pltpu.VMEM((rows, 2 * S5_W), F32)],
        compiler_params=_cp("arbitrary"), name="s5")(u_tm, wb, a1, a2, h0, wc, d, glu_w, glu_b)


def _s5_weights(a_re, a_im, log_dt, b_re, b_im, c_re, c_im):
    lam = lax.complex(a_re, a_im)
    a_bar = jnp.exp(lam * jnp.exp(log_dt))
    b_bar = ((a_bar - 1.0) / lam)[..., None] * lax.complex(b_re, b_im)
    eye = jnp.eye(S5_GROUPS, dtype=F32)

    def in_map(m):
        return jnp.einsum('gpc,gh->gchp', m, eye).reshape(S5_GROUPS * S5_GROUP, S5_W)

    def out_map(m):
        return jnp.einsum('gcp,gh->gphc', m, eye).reshape(S5_W, S5_GROUPS * S5_GROUP)

    wb = jnp.concatenate([in_map(b_bar.real), in_map(b_bar.imag)], axis=1).astype(BF16)
    wc = jnp.concatenate([out_map(c_re), -out_map(c_im)], axis=0).astype(BF16)
    ar = a_bar.real.reshape(1, S5_W)
    ai = a_bar.imag.reshape(1, S5_W)
    return wb, jnp.concatenate([ar, ar], axis=1), jnp.concatenate([-ai, ai], axis=1), wc


def _cross_attn_kernel(x_ref, wq_ref, k_ref, v_ref, wo_ref, g_ref, b_ref, o_ref):
    x = x_ref[...]
    q = (_dot(x.astype(BF16), wq_ref[...]) * QK_SCALE).astype(BF16)
    k = k_ref[0].astype(BF16)
    v = v_ref[0].astype(BF16)
    outs = []
    for h in range(N_HEADS):
        sl = slice(h * HEAD_DIM, (h + 1) * HEAD_DIM)
        s = _dot_nt(q[:, sl], k[:, sl])
        p = jnp.exp(s - jnp.max(s, axis=1, keepdims=True))
        p = p / jnp.sum(p, axis=1, keepdims=True)
        outs.append(_dot(p.astype(BF16), v[:, sl]))
    o = jnp.concatenate(outs, axis=1)
    y = ALPHA * x + _dot(o.astype(BF16), wo_ref[...])
    o_ref[...] = _layer_norm(y, g_ref[...], b_ref[...])


def cross_attn_ln(x, prev_out, row_blk0, n_batch, seq, tq, mem_k, mem_v, wq, wo, g, b):
    nq = seq // tq

    def const_spec(shape):
        return pl.BlockSpec(shape, lambda bb, i: (0,) * len(shape))

    row_spec = pl.BlockSpec((tq, D_MODEL), lambda bb, i: (row_blk0 + bb * nq + i, 0))
    mem_spec = pl.BlockSpec((1, N_MEM, GROUP_W), lambda bb, i: (bb, 0, 0))
    in_specs = [row_spec, const_spec(wq.shape), mem_spec, mem_spec, const_spec(wo.shape),
                const_spec(g.shape), const_spec(b.shape)]
    args = [x, wq, mem_k, mem_v, wo, g, b]
    kern = _cross_attn_kernel
    aliases = {}
    if prev_out is not None:
        in_specs.append(pl.BlockSpec(memory_space=pl.ANY))
        args.append(prev_out)
        aliases = {len(args) - 1: 0}
        kern = lambda *refs: _cross_attn_kernel(*refs[:7], refs[8])
    return pl.pallas_call(
        kern, grid=(n_batch, nq), in_specs=in_specs, out_specs=row_spec,
        out_shape=jax.ShapeDtypeStruct(x.shape, F32), input_output_aliases=aliases,
        compiler_params=_cp("arbitrary", "arbitrary"), name="cross_attn")(*args)


def _ffn_kernel(x_ref, wg_ref, wu_ref, wd_ref, g_ref, b_ref, o_ref, xb_ref, acc_ref, *, nf):
    j = pl.program_id(1)

    @pl.when(j == 0)
    def _():
        xb_ref[...] = x_ref[...].astype(BF16)
        acc_ref[...] = jnp.zeros_like(acc_ref)

    xb = xb_ref[...]
    gate = _dot(xb, wg_ref[...])
    up = _dot(xb, wu_ref[...])
    hid = (gate * jax.nn.sigmoid(gate) * up).astype(BF16)
    acc_ref[...] += _dot(hid, wd_ref[...])

    @pl.when(j == nf - 1)
    def _():
        o_ref[...] = _layer_norm(ALPHA * x_ref[...] + acc_ref[...], g_ref[...], b_ref[...])


def ffn_ln(x, wg, wu, wd, g, b, tm, tf):
    m = x.shape[0]
    nf = D_FF // tf
    kern = functools.partial(_ffn_kernel, nf=nf)
    return pl.pallas_call(
        kern, grid=(m // tm, nf),
        in_specs=[pl.BlockSpec((tm, D_MODEL), lambda i, j: (i, 0)),
                  pl.BlockSpec((D_MODEL, tf), lambda i, j: (0, j)), pl.BlockSpec((D_MODEL, tf), lambda i, j: (0, j)),
                  pl.BlockSpec((tf, D_MODEL), lambda i, j: (j, 0)),
                  pl.BlockSpec((1, D_MODEL), lambda i, j: (0, 0)), pl.BlockSpec((1, D_MODEL), lambda i, j: (0, 0))],
        out_specs=pl.BlockSpec((tm, D_MODEL), lambda i, j: (i, 0)),
        out_shape=jax.ShapeDtypeStruct((m, D_MODEL), F32),
        scratch_shapes=[pltpu.VMEM((tm, D_MODEL), BF16), pltpu.VMEM((tm, D_MODEL), F32)],
        compiler_params=_cp("arbitrary", "arbitrary"), name="ffn")(x, wg, wu, wd, g, b)


def _router_kernel(x_ref, w_ref, b_ref, o_ref):
    logits = jnp.dot(x_ref[...], w_ref[...], preferred_element_type=F32, precision=lax.Precision.HIGHEST) + b_ref[...]
    lane = lax.broadcasted_iota(jnp.int32, logits.shape, 1)
    neg = jnp.float32(-jnp.inf)
    lg = jnp.where(lane < N_EXPERTS, logits, neg)
    m1 = jnp.max(lg, axis=1, keepdims=True)
    i1 = jnp.min(jnp.where(lg == m1, lane, LANES), axis=1, keepdims=True)
    lg2 = jnp.where(lane == i1, neg, lg)
    m2 = jnp.max(lg2, axis=1, keepdims=True)
    i2 = jnp.min(jnp.where(lg2 == m2, lane, LANES), axis=1, keepdims=True)
    e2 = jnp.exp(m2 - m1)
    g1 = 1.0 / (1.0 + e2)
    g2 = e2 / (1.0 + e2)
    out = jnp.where(lane == 0, i1.astype(F32), jnp.where(lane == 1, i2.astype(F32),
                    jnp.where(lane == 2, g1, jnp.where(lane == 3, g2, 0.0))))
    o_ref[...] = out


def router(x, w_pad, b_pad, tm):
    m = x.shape[0]
    return pl.pallas_call(
        _router_kernel, grid=(m // tm,),
        in_specs=[pl.BlockSpec((tm, D_MODEL), lambda i: (i, 0)), pl.BlockSpec((D_MODEL, LANES), lambda i: (0, 0)),
                  pl.BlockSpec((1, LANES), lambda i: (0, 0))],
        out_specs=pl.BlockSpec((tm, LANES), lambda i: (i, 0)),
        out_shape=jax.ShapeDtypeStruct((m, LANES), F32),
        compiler_params=_cp("arbitrary"), name="router")(x, w_pad, b_pad)


def _moe_ffn_kernel(te_ref, nu_ref, x_ref, gt_ref, wg_ref, wu_ref, wd_ref, o_ref, acc_ref, *, nf):
    i = pl.program_id(0)
    j = pl.program_id(1)
    used = i < nu_ref[0]

    @pl.when(used)
    def _():
        @pl.when(j == 0)
        def _():
            acc_ref[...] = jnp.zeros_like(acc_ref)

        xb = x_ref[...]
        gate = _dot(xb, wg_ref[0])
        up = _dot(xb, wu_ref[0])
        hid = (gate * jax.nn.sigmoid(gate) * up).astype(BF16)
        acc_ref[...] += _dot(hid, wd_ref[0])

        @pl.when(j == nf - 1)
        def _():
            o_ref[...] = acc_ref[...] * gt_ref[...]

    @pl.when(jnp.logical_and(jnp.logical_not(used), j == nf - 1))
    def _():
        o_ref[...] = jnp.zeros_like(o_ref)


def moe_ffn(x_sorted, row_gate, tile_expert, n_used, wg, wu, wd, tf):
    n_rows = x_sorted.shape[0]
    n_tiles = n_rows // MOE_TILE
    nf = D_FF // tf

    def jj(i, j, nu):
        return jnp.where(i < nu[0], j, nf - 1)

    kern = functools.partial(_moe_ffn_kernel, nf=nf)
    gs = pltpu.PrefetchScalarGridSpec(
        num_scalar_prefetch=2, grid=(n_tiles, nf),
        in_specs=[pl.BlockSpec((MOE_TILE, D_MODEL), lambda i, j, te, nu: (i, 0)),
                  pl.BlockSpec((MOE_TILE, 1), lambda i, j, te, nu: (i, 0)),
                  pl.BlockSpec((1, D_MODEL, tf), lambda i, j, te, nu: (te[i], 0, jj(i, j, nu))),
                  pl.BlockSpec((1, D_MODEL, tf), lambda i, j, te, nu: (te[i], 0, jj(i, j, nu))),
                  pl.BlockSpec((1, tf, D_MODEL), lambda i, j, te, nu: (te[i], jj(i, j, nu), 0))],
        out_specs=pl.BlockSpec((MOE_TILE, D_MODEL), lambda i, j, te, nu: (i, 0)),
        scratch_shapes=[pltpu.VMEM((MOE_TILE, D_MODEL), F32)])
    return pl.pallas_call(
        kern, grid_spec=gs, out_shape=jax.ShapeDtypeStruct((n_rows, D_MODEL), F32),
        compiler_params=_cp("arbitrary", "arbitrary"), name="moe_ffn")(
            tile_expert, n_used, x_sorted, row_gate, wg, wu, wd)


def _combine_ln_kernel(x_ref, ya_ref, yb_ref, g_ref, b_ref, o_ref):
    o_ref[...] = _layer_norm(ALPHA * x_ref[...] + (ya_ref[...] + yb_ref[...]), g_ref[...], b_ref[...])


def combine_ln(x, ya, yb, g, b, tm):
    m = x.shape[0]
    row = pl.BlockSpec((tm, D_MODEL), lambda i: (i, 0))
    vec = pl.BlockSpec((1, D_MODEL), lambda i: (0, 0))
    return pl.pallas_call(
        _combine_ln_kernel, grid=(m // tm,), in_specs=[row, row, row, vec, vec], out_specs=row,
        out_shape=jax.ShapeDtypeStruct((m, D_MODEL), F32),
        compiler_params=_cp("arbitrary"), name="combine_ln")(x, ya, yb, g, b)


def moe_ln(x, router_w, router_b, wg, wu, wd, g, b, tm):
    m = x.shape[0]
    w_pad = jnp.zeros((D_MODEL, LANES), F32).at[:, :N_EXPERTS].set(router_w)
    b_pad = jnp.zeros((1, LANES), F32).at[0, :N_EXPERTS].set(router_b)
    r = router(x, w_pad, b_pad, tm)
    top_idx = r[:, :TOP_K].astype(jnp.int32)
    gates = r[:, TOP_K:2 * TOP_K]
    flat_e = top_idx.reshape(-1)
    n_slot = m * TOP_K
    onehot = (flat_e[:, None] == jnp.arange(N_EXPERTS, dtype=jnp.int32)[None, :]).astype(jnp.int32)
    rank = jnp.take_along_axis(jnp.cumsum(onehot, axis=0) - onehot, flat_e[:, None], axis=1)[:, 0]
    counts = jnp.sum(onehot, axis=0)
    tiles_per = (counts + MOE_TILE - 1) // MOE_TILE
    tile_end = jnp.cumsum(tiles_per)
    dest = (tile_end - tiles_per)[flat_e] * MOE_TILE + rank
    n_tiles = -(-n_slot // MOE_TILE) + N_EXPERTS
    n_rows = n_tiles * MOE_TILE
    row_tok = jnp.zeros((n_rows,), jnp.int32).at[dest].set(jnp.arange(n_slot, dtype=jnp.int32) // TOP_K)
    row_gate = jnp.zeros((n_rows,), F32).at[dest].set(gates.reshape(-1))
    n_used = tile_end[-1:].astype(jnp.int32)
    tile_ids = jnp.minimum(jnp.arange(n_tiles, dtype=jnp.int32), n_used[0] - 1)
    tile_expert = jnp.minimum(jnp.searchsorted(tile_end, tile_ids, side='right'), N_EXPERTS - 1).astype(jnp.int32)
    x_sorted = x.astype(BF16)[row_tok]
    y_sorted = moe_ffn(x_sorted, row_gate[:, None], tile_expert, n_used, wg, wu, wd, tf=512)
    dest2 = dest.reshape(m, TOP_K)
    return combine_ln(x, y_sorted[dest2[:, 0]], y_sorted[dest2[:, 1]], g, b, tm)


def kernel(x_prompt, x_sample, cache_sb_k, cache_sb_v, cache_mem_k, cache_mem_v, state_ml_C, state_ml_n, state_ml_m, state_rt_S, state_s5_re, state_s5_im, page_table, mem_prompt, w_in, sb_bias, ml_b_i, ml_b_f, ml_norm_g, rt_norm_g, s5_A_re, s5_A_im, s5_log_dt, s5_B_re, s5_B_im, s5_C_re, s5_C_im, s5_D, s5_glu_w, s5_glu_b, w_out, ca_wq, ca_wk, ca_wv, ca_wo, ln_g, ln_b, ffn_w_gate, ffn_w_up, ffn_w_down, moe_router_w, moe_router_b, moe_w_gate, moe_w_up, moe_w_down):
    bp, tp, _ = x_prompt.shape
    bs, ts, _ = x_sample.shape
    n_p, n_s = bp * tp, bs * ts
    tm = 640
    assert (n_p + n_s) % tm == 0 and n_p % ts == 0 and tp % 512 == 0
    x = jnp.concatenate([x_prompt.reshape(n_p, D_MODEL), x_sample.reshape(n_s, D_MODEL)], axis=0)
    uu = _suffix_matrix()
    g_off = 7 * GROUP_W
    half = HEAD_DIM // 2
    freq = ROPE_BASE ** (-jnp.arange(half, dtype=F32) / half)

    def rope_tables(pos):
        ang = pos.astype(F32)[:, None] * freq[None, :]
        cos, sin = jnp.cos(ang), jnp.sin(ang)
        return (jnp.tile(jnp.concatenate([cos, cos], axis=1), (1, N_HEADS)),
                jnp.tile(jnp.concatenate([-sin, sin], axis=1), (1, N_HEADS)))

    cos_p, sin_p = rope_tables(jnp.arange(tp, dtype=jnp.int32))
    cos_s, sin_s = rope_tables(PAST_LEN + jnp.arange(ts, dtype=jnp.int32))
    cache_kt = cache_sb_k.transpose(0, 1, 3, 4, 2)
    cache_vt = cache_sb_v.transpose(0, 1, 3, 4, 2)

    p_st = [[] for _ in range(10)]
    s_st = [[] for _ in range(8)]
    for l in range(DEPTH):
        wl = w_in[l]
        w_cat = jnp.concatenate([wl[:, :g_off], wl[:, g_off + 2 * N_HEADS:], wl[:, g_off:g_off + 2 * N_HEADS],
                                 jnp.zeros((D_MODEL, PROJ_W - wl.shape[1]), F32)], axis=1).astype(BF16)
        proj = linear(x, w_cat, tm)
        gate_bias = jnp.zeros((1, LANES), F32).at[0, :2 * N_HEADS].set(jnp.concatenate([ml_b_i[l], ml_b_f[l]]))
        ml_g = ml_norm_g[l][None, :]
        rt_g = rt_norm_g[l][None, :]
        wb, a1, a2, wc = _s5_weights(s5_A_re[l], s5_A_im[l], s5_log_dt[l], s5_B_re[l], s5_B_im[l],
                                     s5_C_re[l], s5_C_im[l])
        s5_d = s5_D[l][None, :]
        glu_w = s5_glu_w[l].astype(BF16)
        glu_b = s5_glu_b[l][None, :]

        o_sb_p = sb_attention_prompt(proj, sb_bias[l], uu, bp, tp, tq=128)
        o_ml_p, cn_p, m_p = mlstm_mixer(proj, 0, bp, tp, gate_bias, ml_g,
                                        jnp.zeros((bp, N_HEADS, HEAD_DIM, LANES), F32), jnp.zeros((bp, N_HEADS), F32))
        o_rt_p, rs_p = retention_mixer(proj, 0, bp, tp, cos_p, sin_p, rt_g,
                                       jnp.zeros((bp, N_HEADS, HEAD_DIM, HEAD_DIM), F32))
        u_p = proj[:n_p, C_SU * GROUP_W:(C_SU + 1) * GROUP_W].reshape(bp, tp, GROUP_W).transpose(1, 0, 2)
        o_s5_p, h5_p = s5_mixer(u_p.reshape(n_p, GROUP_W), bp, tp, 64, wb, a1, a2,
                                jnp.zeros((bp, 2 * S5_W), F32), wc, s5_d, glu_w, glu_b)
        o_s5_p = o_s5_p.reshape(tp, bp, GROUP_W).transpose(1, 0, 2).reshape(n_p, GROUP_W)

        o_sb_s = sb_attention_sample(proj, n_p // ts, cache_kt, cache_vt, page_table, l, sb_bias[l], uu, n_pp=4)
        cn0 = jnp.concatenate([state_ml_C[:, l], state_ml_n[:, l][..., None],
                               jnp.zeros((bs, N_HEADS, HEAD_DIM, LANES - HEAD_DIM - 1), F32)], axis=-1)
        o_ml_s, cn_s, m_s = mlstm_mixer(proj, n_p // ts, bs, ts, gate_bias, ml_g, cn0, state_ml_m[:, l])
        o_rt_s, rs_s = retention_mixer(proj, n_p // ts, bs, ts, cos_s, sin_s, rt_g, state_rt_S[:, l])
        u_s = proj[n_p:, C_SU * GROUP_W:(C_SU + 1) * GROUP_W].reshape(bs, ts, GROUP_W).transpose(1, 0, 2)
        h0_s = jnp.concatenate([state_s5_re[:, l].reshape(bs, S5_W), state_s5_im[:, l].reshape(bs, S5_W)], axis=1)
        o_s5_s, h5_s = s5_mixer(u_s.reshape(n_s, GROUP_W), bs, ts, ts, wb, a1, a2, h0_s, wc, s5_d, glu_w, glu_b)
        o_s5_s = o_s5_s.reshape(ts, bs, GROUP_W).transpose(1, 0, 2).reshape(n_s, GROUP_W)

        mixed = jnp.concatenate([jnp.concatenate([o_sb_p, o_ml_p, o_rt_p, o_s5_p], axis=1),
                                 jnp.concatenate([o_sb_s, o_ml_s, o_rt_s, o_s5_s], axis=1)], axis=0)
        x = linear_res_ln(mixed, w_out[l].astype(BF16), x, ln_g[l, 0][None, :], ln_b[l, 0][None, :], tm)

        mem_kv = linear(mem_prompt.reshape(bp * N_MEM, D_MODEL),
                        jnp.concatenate([ca_wk[l], ca_wv[l]], axis=1).astype(BF16), 512)
        mk_p = mem_kv[:, :GROUP_W].reshape(bp, N_MEM, GROUP_W)
        mv_p = mem_kv[:, GROUP_W:].reshape(bp, N_MEM, GROUP_W)
        wq = ca_wq[l].astype(BF16)
        wo = ca_wo[l].astype(BF16)
        g1, b1 = ln_g[l, 1][None, :], ln_b[l, 1][None, :]
        xa = cross_attn_ln(x, None, 0, bp, tp, 512, mk_p, mv_p, wq, wo, g1, b1)
        x = cross_attn_ln(x, xa, n_p // ts, bs, ts, ts, cache_mem_k[:, l].reshape(bs, N_MEM, GROUP_W),
                          cache_mem_v[:, l].reshape(bs, N_MEM, GROUP_W), wq, wo, g1, b1)

        g2, b2 = ln_g[l, 2][None, :], ln_b[l, 2][None, :]
        j = l // 2
        if l % 2 == 0:
            x = ffn_ln(x, ffn_w_gate[j].astype(BF16), ffn_w_up[j].astype(BF16), ffn_w_down[j].astype(BF16),
                       g2, b2, tm, tf=512)
        else:
            x = moe_ln(x, moe_router_w[j], moe_router_b[j], moe_w_gate[j].astype(BF16), moe_w_up[j].astype(BF16),
                       moe_w_down[j].astype(BF16), g2, b2, tm)

        def heads(a, nb_, t_):
            return a.reshape(nb_, t_, N_HEADS, HEAD_DIM)

        p_st[0].append(heads(proj[:n_p, C_SK * GROUP_W:(C_SK + 1) * GROUP_W], bp, tp))
        p_st[1].append(heads(proj[:n_p, C_SV * GROUP_W:(C_SV + 1) * GROUP_W], bp, tp))
        p_st[2].append(heads(mk_p, bp, N_MEM))
        p_st[3].append(heads(mv_p, bp, N_MEM))
        s_st[0].append(heads(proj[n_p:, C_SK * GROUP_W:(C_SK + 1) * GROUP_W], bs, ts))
        s_st[1].append(heads(proj[n_p:, C_SV * GROUP_W:(C_SV + 1) * GROUP_W], bs, ts))
        for st, cn, mm, rs, h5, nb_ in ((p_st, cn_p, m_p, rs_p, h5_p, bp), (s_st, cn_s, m_s, rs_s, h5_s, bs)):
            off = 4 if st is p_st else 2
            st[off + 0].append(cn[..., :HEAD_DIM])
            st[off + 1].append(cn[..., HEAD_DIM])
            st[off + 2].append(mm[:, 0, :N_HEADS])
            st[off + 3].append(rs)
            st[off + 4].append(h5[:, :S5_W].reshape(nb_, S5_GROUPS, S5_STATE))
            st[off + 5].append(h5[:, S5_W:].reshape(nb_, S5_GROUPS, S5_STATE))

    y_prompt = x[:n_p].reshape(bp, tp, D_MODEL)
    y_sample = x[n_p:].reshape(bs, ts, D_MODEL)
    p_out = [jnp.stack(a, axis=1) for a in p_st]
    s_out = [jnp.stack(a, axis=1) for a in s_st]
    return (y_prompt, y_sample, *p_out, *s_out)
```

```python
import functools
import math

import numpy as np
import jax
import jax.numpy as jnp
from jax import lax
from jax.experimental import pallas as pl
from jax.experimental.pallas import tpu as pltpu

F32 = jnp.float32
BF16 = jnp.bfloat16

D_MODEL = 1024
DEPTH = 2
PAST_LEN = 8192
PAGE_SIZE = 128
HEAD_DIM = 64
N_HEADS = 4
GROUP_W = N_HEADS * HEAD_DIM
S5_GROUPS = 16
S5_GROUP = 16
S5_STATE = 64
S5_W = S5_GROUPS * S5_STATE
N_MEM = 256
D_FF = 3584
N_EXPERTS = 8
TOP_K = 2
CHUNK = 64
ROPE_BASE = 10000.0
LN_EPS = 1e-5
GN_EPS = 1e-6
ALPHA = (2 * DEPTH) ** 0.25
QK_SCALE = HEAD_DIM ** -0.5
LOG2E = math.log2(math.e)

LANES = 128
PROJ_W = 25 * LANES
C_SQ, C_SK, C_SV, C_MQ, C_MK, C_MV, C_MO, C_RQ, C_RK, C_RV, C_RG, C_SU = range(12)
C_GATES = 12 * GROUP_W // LANES
VMEM_LIMIT = 48 * 1024 * 1024
MOE_TILE = 1024


def _cp(*sem):
    return pltpu.CompilerParams(dimension_semantics=sem, vmem_limit_bytes=VMEM_LIMIT)


def _dot(a, b):
    return jnp.dot(a, b, preferred_element_type=F32)


def _dot_nt(a, b):
    return lax.dot_general(a, b, (((1,), (1,)), ((), ())), preferred_element_type=F32)


def _dot_tn(a, b):
    return lax.dot_general(a, b, (((0,), (0,)), ((), ())), preferred_element_type=F32)


def _layer_norm(y, g, b):
    mu = jnp.mean(y, axis=-1, keepdims=True)
    yc = y - mu
    var = jnp.mean(yc * yc, axis=-1, keepdims=True)
    return yc * lax.rsqrt(var + LN_EPS) * g + b


def _head_norm(h):
    mu = jnp.mean(h, axis=-1, keepdims=True)
    hc = h - mu
    var = jnp.mean(hc * hc, axis=-1, keepdims=True)
    return hc * lax.rsqrt(var + GN_EPS)


def _neg_softplus(z):
    return -(jnp.maximum(z, 0.0) + jnp.log1p(jnp.exp(-jnp.abs(z))))


def _log_sigmoid(z):
    return _neg_softplus(-z)


def _linear_kernel(x_ref, w_ref, o_ref):
    o_ref[...] = _dot(x_ref[...].astype(BF16), w_ref[...]).astype(o_ref.dtype)


def linear(x, w, tm, out_dtype=F32):
    m, k = x.shape
    n = w.shape[1]
    return pl.pallas_call(
        _linear_kernel, grid=(m // tm,),
        in_specs=[pl.BlockSpec((tm, k), lambda i: (i, 0)), pl.BlockSpec((k, n), lambda i: (0, 0))],
        out_specs=pl.BlockSpec((tm, n), lambda i: (i, 0)),
        out_shape=jax.ShapeDtypeStruct((m, n), out_dtype),
        compiler_params=_cp("arbitrary"), name="linear")(x, w)


def _linear_res_ln_kernel(x_ref, w_ref, r_ref, g_ref, b_ref, o_ref):
    h = _dot(x_ref[...].astype(BF16), w_ref[...])
    o_ref[...] = _layer_norm(ALPHA * r_ref[...] + h, g_ref[...], b_ref[...])


def linear_res_ln(x, w, res, g, b, tm):
    m, k = x.shape
    n = w.shape[1]
    return pl.pallas_call(
        _linear_res_ln_kernel, grid=(m // tm,),
        in_specs=[pl.BlockSpec((tm, k), lambda i: (i, 0)), pl.BlockSpec((k, n), lambda i: (0, 0)),
                  pl.BlockSpec((tm, n), lambda i: (i, 0)),
                  pl.BlockSpec((1, n), lambda i: (0, 0)), pl.BlockSpec((1, n), lambda i: (0, 0))],
        out_specs=pl.BlockSpec((tm, n), lambda i: (i, 0)),
        out_shape=jax.ShapeDtypeStruct((m, n), F32),
        compiler_params=_cp("arbitrary"), name="linear_res_ln")(x, w, res, g, b)


def _suffix_matrix():
    j = np.arange(LANES)
    u = (j[:, None] >= j[None, :]).astype(np.float32)
    uu = np.concatenate([u, np.ones((LANES, LANES), np.float32)], axis=1)
    return jnp.asarray(np.concatenate([uu, uu], axis=0), dtype=BF16)


def _suffix_sums(lr, uu):
    hi = lr.astype(BF16)
    lo = (lr - hi.astype(F32)).astype(BF16)
    r = _dot(jnp.concatenate([hi, lo], axis=1), uu)
    return r[:, :LANES], r[:, LANES:]


def _log2_rem(z2):
    return jnp.minimum(-z2, 0.0) - jnp.log2(1.0 + jnp.exp2(-jnp.abs(z2)))


def _sb_prompt_kernel(bias_ref, q_ref, k_ref, v_ref, uu_ref, o_ref, acc_ref, car_ref, *, tq):
    i = pl.program_id(1)
    tk = LANES
    nsub = tq // tk
    acc_ref[...] = jnp.zeros_like(acc_ref)
    car_ref[...] = jnp.zeros_like(car_ref)
    q = (q_ref[...] * (QK_SCALE * LOG2E)).astype(BF16)
    qh = [q[:, h * HEAD_DIM:(h + 1) * HEAD_DIM] for h in range(N_HEADS)]
    b2 = [bias_ref[h] * LOG2E for h in range(N_HEADS)]
    uu = uu_ref[...]
    row = lax.broadcasted_iota(jnp.int32, (tq, tk), 0)
    col = lax.broadcasted_iota(jnp.int32, (tq, tk), 1)
    heads = range(N_HEADS)

    def block(j, causal):
        r0 = pl.multiple_of(j * tk, tk)
        kb = k_ref[pl.ds(r0, tk), :].astype(BF16)
        vb = v_ref[pl.ds(r0, tk), :].astype(BF16)
        z2 = [_dot_nt(qh[h], kb[:, h * HEAD_DIM:(h + 1) * HEAD_DIM]) + b2[h] for h in heads]
        lr = [_log2_rem(z) for z in z2]
        if causal is not None:
            lr = [jnp.where(causal, a, 0.0) for a in lr]
        cs, tot = _suffix_sums(jnp.concatenate(lr, axis=0), uu)
        car = [car_ref[h] for h in heads]
        w = [jnp.exp2(z2[h] + cs[h * tq:(h + 1) * tq] + car[h]) for h in heads]
        if causal is not None:
            w = [jnp.where(causal, a, 0.0) for a in w]
        pv = [_dot(w[h].astype(BF16), vb[:, h * HEAD_DIM:(h + 1) * HEAD_DIM]) for h in heads]
        for h in heads:
            acc_ref[h] += pv[h]
            car_ref[h] = car[h] + tot[h * tq:(h + 1) * tq]

    for d in range(nsub):
        block(i * nsub + (nsub - 1 - d), (col + (nsub - 1 - d) * tk) < row)

    def body(jj, carry):
        block(i * nsub - 1 - jj, None)
        return carry

    lax.fori_loop(0, i * nsub, body, 0)
    o_ref[...] = jnp.concatenate([acc_ref[h] for h in range(N_HEADS)], axis=1)


def sb_attention_prompt(proj, sb_bias, uu, n_batch, seq, tq):
    nq = seq // tq
    kern = functools.partial(_sb_prompt_kernel, tq=tq)
    return pl.pallas_call(
        kern, grid=(n_batch, nq),
        in_specs=[pl.BlockSpec(memory_space=pltpu.SMEM),
                  pl.BlockSpec((tq, GROUP_W), lambda b, i: (b * nq + i, C_SQ)),
                  pl.BlockSpec((seq, GROUP_W), lambda b, i: (b, C_SK)),
                  pl.BlockSpec((seq, GROUP_W), lambda b, i: (b, C_SV)),
                  pl.BlockSpec((2 * LANES, 2 * LANES), lambda b, i: (0, 0))],
        out_specs=pl.BlockSpec((tq, GROUP_W), lambda b, i: (b * nq + i, 0)),
        out_shape=jax.ShapeDtypeStruct((n_batch * seq, GROUP_W), F32),
        scratch_shapes=[pltpu.VMEM((N_HEADS, tq, HEAD_DIM), F32), pltpu.VMEM((N_HEADS, tq, LANES), F32)],
        compiler_params=_cp("arbitrary", "arbitrary"), name="sb_prompt")(sb_bias, proj, proj, proj, uu)


def _sb_sample_kernel(pt_ref, bias_ref, q_ref, kn_ref, vn_ref, u8_ref, uu_ref, *rest, n_pp, n_steps):
    k_refs = rest[:n_pp]
    v_refs = rest[n_pp:2 * n_pp]
    o_ref = rest[2 * n_pp]
    acc_ref, car_ref = rest[2 * n_pp + 1:]
    s = pl.program_id(1)
    nq = q_ref.shape[0]
    q = (q_ref[...] * (QK_SCALE * LOG2E)).astype(BF16)
    qh = [q[:, h * HEAD_DIM:(h + 1) * HEAD_DIM] for h in range(N_HEADS)]
    b2 = [bias_ref[h] * LOG2E for h in range(N_HEADS)]

    @pl.when(s == 0)
    def _():
        kn = kn_ref[...].astype(BF16)
        vn = vn_ref[...].astype(BF16)
        t = lax.broadcasted_iota(jnp.int32, (nq, nq), 0)
        c = lax.broadcasted_iota(jnp.int32, (nq, nq), 1)
        causal = c < t
        for h in range(N_HEADS):
            sl = slice(h * HEAD_DIM, (h + 1) * HEAD_DIM)
            z2 = _dot_nt(qh[h], kn[:, sl]) + b2[h]
            lr = jnp.where(causal, _log2_rem(z2), 0.0)
            cs = jnp.dot(lr, u8_ref[...], preferred_element_type=F32, precision=lax.Precision.HIGHEST)
            w = jnp.where(causal, jnp.exp2(z2 + cs), 0.0)
            acc_ref[h] = _dot(w.astype(BF16), vn[:, sl])
            car_ref[h] = jnp.broadcast_to(jnp.sum(lr, axis=1, keepdims=True), (nq, LANES))

    uu = uu_ref[...]
    z2s = []
    for p in range(n_pp):
        for h in range(N_HEADS):
            z2s.append(_dot(qh[h], k_refs[p][0, 0, h].astype(BF16)) + b2[h])
    cs_all, tot_all = _suffix_sums(_log2_rem(jnp.concatenate(z2s, axis=0)), uu)
    for h in range(N_HEADS):
        car = car_ref[h]
        acc = acc_ref[h]
        for p in range(n_pp):
            r = (p * N_HEADS + h) * nq
            w = jnp.exp2(z2s[p * N_HEADS + h] + cs_all[r:r + nq] + car)
            acc = acc + _dot_nt(w.astype(BF16), v_refs[p][0, 0, h].astype(BF16))
            car = car + tot_all[r:r + nq]
        car_ref[h] = car
        acc_ref[h] = acc

    @pl.when(s == n_steps - 1)
    def _():
        o_ref[...] = jnp.concatenate([acc_ref[h] for h in range(N_HEADS)], axis=1)


def sb_attention_sample(proj, row_blk0, cache_kt, cache_vt, page_table, layer, sb_bias, uu, n_pp):
    n_batch, n_pages = page_table.shape
    nq = 8
    n_steps = n_pages // n_pp
    u8 = jnp.asarray((np.arange(nq)[:, None] >= np.arange(nq)[None, :]).astype(np.float32))

    def page_spec(p):
        return pl.BlockSpec((1, 1, N_HEADS, HEAD_DIM, PAGE_SIZE),
                            lambda b, s, pt: (pt[b, n_pages - 1 - (s * n_pp + p)], layer, 0, 0, 0))

    def row_spec(cblk):
        return pl.BlockSpec((nq, GROUP_W), lambda b, s, pt: (row_blk0 + b, cblk))

    kern = functools.partial(_sb_sample_kernel, n_pp=n_pp, n_steps=n_steps)
    gs = pltpu.PrefetchScalarGridSpec(
        num_scalar_prefetch=1, grid=(n_batch, n_steps),
        in_specs=[pl.BlockSpec(memory_space=pltpu.SMEM), row_spec(C_SQ), row_spec(C_SK), row_spec(C_SV),
                  pl.BlockSpec((nq, nq), lambda b, s, pt: (0, 0)),
                  pl.BlockSpec((2 * LANES, 2 * LANES), lambda b, s, pt: (0, 0))]
                 + [page_spec(p) for p in range(n_pp)] * 2,
        out_specs=pl.BlockSpec((nq, GROUP_W), lambda b, s, pt: (b, 0)),
        scratch_shapes=[pltpu.VMEM((N_HEADS, nq, HEAD_DIM), F32), pltpu.VMEM((N_HEADS, nq, LANES), F32)])
    return pl.pallas_call(
        kern, grid_spec=gs, out_shape=jax.ShapeDtypeStruct((n_batch * nq, GROUP_W), F32),
        compiler_params=_cp("arbitrary", "arbitrary"), name="sb_sample")(
            page_table, sb_bias, proj, proj, proj, u8, uu, *([cache_kt] * n_pp), *([cache_vt] * n_pp))


def _mlstm_kernel(m0_ref, q_ref, k_ref, v_ref, og_ref, gt_ref, gb_ref, ng_ref, tril_ref, cn0_ref,
                  o_ref, cn_ref, m_ref, *, seq, chunk):
    b = pl.program_id(0)
    nc = seq // chunk
    cn_ref[...] = cn0_ref[...]
    tril = tril_ref[...]
    tri_mask = lax.broadcasted_iota(jnp.int32, (chunk, chunk), 1) <= lax.broadcasted_iota(jnp.int32, (chunk, chunk), 0)
    lane = lax.broadcasted_iota(jnp.int32, (chunk, HEAD_DIM), 1)
    ones_col = jnp.where(lane == 0, 1.0, 0.0).astype(F32)

    def body(c, ms):
        r0 = pl.multiple_of(c * chunk, chunk)
        gt = gt_ref[pl.ds(r0, chunk), :] + gb_ref[...]
        bc = jnp.dot(tril, _log_sigmoid(gt), preferred_element_type=F32, precision=lax.Precision.HIGHEST)
        gt_t = gt.T
        bc_t = bc.T
        q = q_ref[pl.ds(r0, chunk), :].astype(BF16)
        k = (k_ref[pl.ds(r0, chunk), :] * QK_SCALE).astype(BF16)
        v = v_ref[pl.ds(r0, chunk), :]
        og = og_ref[pl.ds(r0, chunk), :]
        outs, new_ms = [], []
        for h in range(N_HEADS):
            sl = slice(h * HEAD_DIM, (h + 1) * HEAD_DIM)
            ig_col = gt[:, h:h + 1]
            bc_col = bc[:, N_HEADS + h:N_HEADS + h + 1]
            g_row = bc_t[N_HEADS + h:N_HEADS + h + 1, :] - gt_t[h:h + 1, :]
            dm = jnp.where(tri_mask, bc_col - g_row, -jnp.inf)
            a = bc_col + ms[h]
            m_new = jnp.maximum(a, jnp.max(dm, axis=1, keepdims=True))
            dw = jnp.exp(dm - m_new)
            inter = jnp.exp(a - m_new)
            s = _dot_nt(q[:, sl], k[:, sl]) * dw
            v_ext = jnp.concatenate([v[:, sl], ones_col], axis=1)
            qc = _dot(q[:, sl], cn_ref[0, h].astype(BF16))
            sv = _dot(s.astype(BF16), v_ext.astype(BF16))
            num = inter * qc[:, :HEAD_DIM] + sv[:, :HEAD_DIM]
            den = inter * qc[:, HEAD_DIM:HEAD_DIM + 1] + jnp.sum(s, axis=1, keepdims=True)
            hh = num / jnp.maximum(jnp.abs(den), jnp.exp(-m_new))
            m_last = m_new[chunk - 1:chunk, :]
            wl = jnp.exp(bc_col[chunk - 1:chunk, :] - bc_col + ig_col - m_last)
            dl = jnp.exp(a[chunk - 1:chunk, :] - m_last)
            cn_ref[0, h] = dl * cn_ref[0, h] + _dot_tn(k[:, sl], (wl * v_ext).astype(BF16))
            new_ms.append(m_last)
            outs.append(_head_norm(hh))
        y = jnp.concatenate(outs, axis=1) * ng_ref[...] * jax.nn.sigmoid(og)
        o_ref[pl.ds(r0, chunk), :] = y
        return tuple(new_ms)

    ms0 = tuple(jnp.full((1, 1), m0_ref[b, h], F32) for h in range(N_HEADS))
    ms = lax.fori_loop(0, nc, body, ms0)
    lane_m = lax.broadcasted_iota(jnp.int32, (1, LANES), 1)
    m_out = jnp.zeros((1, LANES), F32)
    for h in range(N_HEADS):
        m_out = jnp.where(lane_m == h, ms[h], m_out)
    m_ref[0] = m_out


def mlstm_mixer(proj, row_blk0, n_batch, seq, gate_bias, norm_g, cn0, m0):
    chunk = math.gcd(seq, CHUNK)
    tril = jnp.asarray(np.tril(np.ones((chunk, chunk), np.float32)))

    def row_spec(cblk, w=GROUP_W):
        return pl.BlockSpec((seq, w), lambda b: (row_blk0 + b, cblk))

    kern = functools.partial(_mlstm_kernel, seq=seq, chunk=chunk)
    return pl.pallas_call(
        kern, grid=(n_batch,),
        in_specs=[pl.BlockSpec(memory_space=pltpu.SMEM),
                  row_spec(C_MQ), row_spec(C_MK), row_spec(C_MV), row_spec(C_MO), row_spec(C_GATES, LANES),
                  pl.BlockSpec((1, LANES), lambda b: (0, 0)), pl.BlockSpec((1, GROUP_W), lambda b: (0, 0)),
                  pl.BlockSpec((chunk, chunk), lambda b: (0, 0)),
                  pl.BlockSpec((1, N_HEADS, HEAD_DIM, LANES), lambda b: (b, 0, 0, 0))],
        out_specs=[pl.BlockSpec((seq, GROUP_W), lambda b: (b, 0)),
                   pl.BlockSpec((1, N_HEADS, HEAD_DIM, LANES), lambda b: (b, 0, 0, 0)),
                   pl.BlockSpec((1, 1, LANES), lambda b: (b, 0, 0))],
        out_shape=[jax.ShapeDtypeStruct((n_batch * seq, GROUP_W), F32),
                   jax.ShapeDtypeStruct((n_batch, N_HEADS, HEAD_DIM, LANES), F32),
                   jax.ShapeDtypeStruct((n_batch, 1, LANES), F32)],
        compiler_params=_cp("arbitrary"), name="mlstm")(
            m0, proj, proj, proj, proj, proj, gate_bias, norm_g, tril, cn0)


def _rope(x, cos, sin_signed):
    lane = lax.broadcasted_iota(jnp.int32, x.shape, 1)
    half = HEAD_DIM // 2
    swapped = jnp.where((lane % HEAD_DIM) < half, pltpu.roll(x, x.shape[1] - half, 1), pltpu.roll(x, half, 1))
    return x * cos + swapped * sin_signed


def _retention_kernel(q_ref, k_ref, v_ref, gg_ref, cos_ref, sin_ref, ng_ref, dec_ref, int_ref, wl_ref, dl_ref,
                      s0_ref, o_ref, s_ref, *, seq, chunk):
    nc = seq // chunk
    s_ref[...] = s0_ref[...]

    def body(c, carry):
        r0 = pl.multiple_of(c * chunk, chunk)
        cos = cos_ref[pl.ds(r0, chunk), :]
        sin = sin_ref[pl.ds(r0, chunk), :]
        q = _rope(q_ref[pl.ds(r0, chunk), :], cos, sin).astype(BF16)
        k = (_rope(k_ref[pl.ds(r0, chunk), :], cos, sin) * QK_SCALE).astype(BF16)
        v = v_ref[pl.ds(r0, chunk), :]
        gg = gg_ref[pl.ds(r0, chunk), :]
        outs = []
        for h in range(N_HEADS):
            sl = slice(h * HEAD_DIM, (h + 1) * HEAD_DIM)
            s = _dot_nt(q[:, sl], k[:, sl]) * dec_ref[h]
            st = s_ref[0, h]
            o = int_ref[h] * _dot(q[:, sl], st.astype(BF16)) + _dot(s.astype(BF16), v[:, sl].astype(BF16))
            s_ref[0, h] = dl_ref[h] * st + _dot_tn(k[:, sl], (wl_ref[h] * v[:, sl]).astype(BF16))
            outs.append(_head_norm(o))
        o_ref[pl.ds(r0, chunk), :] = jnp.concatenate(outs, axis=1) * ng_ref[...] * (gg * jax.nn.sigmoid(gg))
        return carry

    lax.fori_loop(0, nc, body, 0)


def _retention_consts(chunk):
    log_g = np.log(1.0 - np.exp2(-5.0 - np.arange(N_HEADS, dtype=np.float64)))
    tau = np.arange(chunk, dtype=np.float64)
    rel = tau[:, None] - tau[None, :]
    decay = np.where(rel >= 0, np.exp(log_g[:, None, None] * np.maximum(rel, 0.0)), 0.0)
    inter = np.exp(log_g[:, None] * (tau + 1.0))[..., None]
    wl = np.exp(log_g[:, None] * (chunk - 1.0 - tau))[..., None]
    dl = np.exp(log_g * chunk)[:, None, None]
    return tuple(jnp.asarray(a, F32) for a in (decay, inter, wl, dl))


def retention_mixer(proj, row_blk0, n_batch, seq, cos, sin_signed, norm_g, s0):
    chunk = math.gcd(seq, CHUNK)
    dec, inter, wl, dl = _retention_consts(chunk)

    def row_spec(cblk):
        return pl.BlockSpec((seq, GROUP_W), lambda b: (row_blk0 + b, cblk))

    def const_spec(shape):
        return pl.BlockSpec(shape, lambda b: (0,) * len(shape))

    kern = functools.partial(_retention_kernel, seq=seq, chunk=chunk)
    return pl.pallas_call(
        kern, grid=(n_batch,),
        in_specs=[row_spec(C_RQ), row_spec(C_RK), row_spec(C_RV), row_spec(C_RG),
                  const_spec((seq, GROUP_W)), const_spec((seq, GROUP_W)), const_spec((1, GROUP_W)),
                  const_spec(dec.shape), const_spec(inter.shape), const_spec(wl.shape), const_spec(dl.shape),
                  pl.BlockSpec((1, N_HEADS, HEAD_DIM, HEAD_DIM), lambda b: (b, 0, 0, 0))],
        out_specs=[pl.BlockSpec((seq, GROUP_W), lambda b: (b, 0)),
                   pl.BlockSpec((1, N_HEADS, HEAD_DIM, HEAD_DIM), lambda b: (b, 0, 0, 0))],
        out_shape=[jax.ShapeDtypeStruct((n_batch * seq, GROUP_W), F32),
                   jax.ShapeDtypeStruct((n_batch, N_HEADS, HEAD_DIM, HEAD_DIM), F32)],
        compiler_params=_cp("arbitrary"), name="retention")(
            proj, proj, proj, proj, cos, sin_signed, norm_g, dec, inter, wl, dl, s0)


def _s5_kernel(u_ref, wb_ref, a1_ref, a2_ref, h0_ref, wc_ref, d_ref, gw_ref, gb_ref, o_ref, hl_ref, hs_ref, *, nb, tt):
    c = pl.program_id(0)

    @pl.when(c == 0)
    def _():
        hl_ref[...] = h0_ref[...]

    u = u_ref[...]
    hs_ref[...] = _dot(u.astype(BF16), wb_ref[...])
    a1 = jnp.broadcast_to(a1_ref[...], (nb, 2 * S5_W))
    a2 = jnp.broadcast_to(a2_ref[...], (nb, 2 * S5_W))

    def step(t, h):
        r0 = pl.multiple_of(t * nb, nb)
        swapped = jnp.concatenate([h[:, S5_W:], h[:, :S5_W]], axis=1)
        h = a1 * h + a2 * swapped + hs_ref[pl.ds(r0, nb), :]
        hs_ref[pl.ds(r0, nb), :] = h
        return h

    hl_ref[...] = lax.fori_loop(0, tt, step, hl_ref[...])
    y = _dot(hs_ref[...].astype(BF16), wc_ref[...]) + d_ref[...] * u
    g5 = jax.nn.gelu(y)
    o_ref[...] = g5 * jax.nn.sigmoid(_dot(g5.astype(BF16), gw_ref[...]) + gb_ref[...])


def s5_mixer(u_tm, nb, seq, tt, wb, a1, a2, h0, wc, d, glu_w, glu_b):
    rows = tt * nb

    def const_spec(shape):
        return pl.BlockSpec(shape, lambda c: (0,) * len(shape))

    kern = functools.partial(_s5_kernel, nb=nb, tt=tt)
    return pl.pallas_call(
        kern, grid=(seq // tt,),
        in_specs=[pl.BlockSpec((rows, GROUP_W), lambda c: (c, 0)),
                  const_spec(wb.shape), const_spec(a1.shape), const_spec(a2.shape), const_spec(h0.shape),
                  const_spec(wc.shape), const_spec(d.shape), const_spec(glu_w.shape), const_spec(glu_b.shape)],
        out_specs=[pl.BlockSpec((rows, GROUP_W), lambda c: (c, 0)), const_spec(h0.shape)],
        out_shape=[jax.ShapeDtypeStruct((seq * nb, GROUP_W), F32), jax.ShapeDtypeStruct(h0.shape, F32)],
        scratch_shapes=[pltpu.VMEM((rows, 2 * S5_W), F32)],
        compiler_params=_cp("arbitrary"), name="s5")(u_tm, wb, a1, a2, h0, wc, d, glu_w, glu_b)


def _s5_weights(a_re, a_im, log_dt, b_re, b_im, c_re, c_im):
    lam = lax.complex(a_re, a_im)
    a_bar = jnp.exp(lam * jnp.exp(log_dt))
    b_bar = ((a_bar - 1.0) / lam)[..., None] * lax.complex(b_re, b_im)
    eye = jnp.eye(S5_GROUPS, dtype=F32)

    def in_map(m):
        return jnp.einsum('gpc,gh->gchp', m, eye).reshape(S5_GROUPS * S5_GROUP, S5_W)

    def out_map(m):
        return jnp.einsum('gcp,gh->gphc', m, eye).reshape(S5_W, S5_GROUPS * S5_GROUP)

    wb = jnp.concatenate([in_map(b_bar.real), in_map(b_bar.imag)], axis=1).astype(BF16)
    wc = jnp.concatenate([out_map(c_re), -out_map(c_im)], axis=0).astype(BF16)
    ar = a_bar.real.reshape(1, S5_W)
    ai = a_bar.imag.reshape(1, S5_W)
    return wb, jnp.concatenate([ar, ar], axis=1), jnp.concatenate([-ai, ai], axis=1), wc


def _cross_attn_kernel(x_ref, wq_ref, k_ref, v_ref, wo_ref, g_ref, b_ref, o_ref):
    x = x_ref[...]
    q = (_dot(x.astype(BF16), wq_ref[...]) * QK_SCALE).astype(BF16)
    k = k_ref[0].astype(BF16)
    v = v_ref[0].astype(BF16)
    outs = []
    for h in range(N_HEADS):
        sl = slice(h * HEAD_DIM, (h + 1) * HEAD_DIM)
        s = _dot_nt(q[:, sl], k[:, sl])
        p = jnp.exp(s - jnp.max(s, axis=1, keepdims=True))
        p = p / jnp.sum(p, axis=1, keepdims=True)
        outs.append(_dot(p.astype(BF16), v[:, sl]))
    o = jnp.concatenate(outs, axis=1)
    y = ALPHA * x + _dot(o.astype(BF16), wo_ref[...])
    o_ref[...] = _layer_norm(y, g_ref[...], b_ref[...])


def cross_attn_ln(x, prev_out, row_blk0, n_batch, seq, tq, mem_k, mem_v, wq, wo, g, b):
    nq = seq // tq

    def const_spec(shape):
        return pl.BlockSpec(shape, lambda bb, i: (0,) * len(shape))

    row_spec = pl.BlockSpec((tq, D_MODEL), lambda bb, i: (row_blk0 + bb * nq + i, 0))
    mem_spec = pl.BlockSpec((1, N_MEM, GROUP_W), lambda bb, i: (bb, 0, 0))
    in_specs = [row_spec, const_spec(wq.shape), mem_spec, mem_spec, const_spec(wo.shape),
                const_spec(g.shape), const_spec(b.shape)]
    args = [x, wq, mem_k, mem_v, wo, g, b]
    kern = _cross_attn_kernel
    aliases = {}
    if prev_out is not None:
        in_specs.append(pl.BlockSpec(memory_space=pl.ANY))
        args.append(prev_out)
        aliases = {len(args) - 1: 0}
        kern = lambda *refs: _cross_attn_kernel(*refs[:7], refs[8])
    return pl.pallas_call(
        kern, grid=(n_batch, nq), in_specs=in_specs, out_specs=row_spec,
        out_shape=jax.ShapeDtypeStruct(x.shape, F32), input_output_aliases=aliases,
        compiler_params=_cp("arbitrary", "arbitrary"), name="cross_attn")(*args)


def _swiglu_chunk(xb, wg, wu, wd):
    gate = _dot(xb, wg.astype(BF16))
    up = _dot(xb, wu.astype(BF16))
    hid = (gate * jax.nn.sigmoid(gate) * up).astype(BF16)
    return _dot(hid, wd.astype(BF16))


def _ffn_kernel(x_ref, wg_ref, wu_ref, wd_ref, g_ref, b_ref, o_ref, xb_ref, acc_ref, *, nf):
    j = pl.program_id(1)

    @pl.when(j == 0)
    def _():
        xb_ref[...] = x_ref[...].astype(BF16)
        acc_ref[...] = jnp.zeros_like(acc_ref)

    xb = xb_ref[...]
    acc_ref[...] += _swiglu_chunk(xb, wg_ref[...], wu_ref[...], wd_ref[...])

    @pl.when(j == nf - 1)
    def _():
        o_ref[...] = _layer_norm(ALPHA * x_ref[...] + acc_ref[...], g_ref[...], b_ref[...])


def ffn_ln(x, wg, wu, wd, g, b, tm, tf):
    m = x.shape[0]
    nf = D_FF // tf
    kern = functools.partial(_ffn_kernel, nf=nf)
    return pl.pallas_call(
        kern, grid=(m // tm, nf),
        in_specs=[pl.BlockSpec((tm, D_MODEL), lambda i, j: (i, 0)),
                  pl.BlockSpec((D_MODEL, tf), lambda i, j: (0, j)), pl.BlockSpec((D_MODEL, tf), lambda i, j: (0, j)),
                  pl.BlockSpec((tf, D_MODEL), lambda i, j: (j, 0)),
                  pl.BlockSpec((1, D_MODEL), lambda i, j: (0, 0)), pl.BlockSpec((1, D_MODEL), lambda i, j: (0, 0))],
        out_specs=pl.BlockSpec((tm, D_MODEL), lambda i, j: (i, 0)),
        out_shape=jax.ShapeDtypeStruct((m, D_MODEL), F32),
        scratch_shapes=[pltpu.VMEM((tm, D_MODEL), BF16), pltpu.VMEM((tm, D_MODEL), F32)],
        compiler_params=_cp("arbitrary", "arbitrary"), name="ffn")(x, wg, wu, wd, g, b)


def _router_kernel(x_ref, w_ref, b_ref, o_ref):
    logits = jnp.dot(x_ref[...], w_ref[...], preferred_element_type=F32, precision=lax.Precision.HIGHEST) + b_ref[...]
    lane = lax.broadcasted_iota(jnp.int32, logits.shape, 1)
    neg = jnp.float32(-jnp.inf)
    lg = jnp.where(lane < N_EXPERTS, logits, neg)
    m1 = jnp.max(lg, axis=1, keepdims=True)
    i1 = jnp.min(jnp.where(lg == m1, lane, LANES), axis=1, keepdims=True)
    lg2 = jnp.where(lane == i1, neg, lg)
    m2 = jnp.max(lg2, axis=1, keepdims=True)
    i2 = jnp.min(jnp.where(lg2 == m2, lane, LANES), axis=1, keepdims=True)
    e2 = jnp.exp(m2 - m1)
    g1 = 1.0 / (1.0 + e2)
    g2 = e2 / (1.0 + e2)
    out = jnp.where(lane == 0, i1.astype(F32), jnp.where(lane == 1, i2.astype(F32),
                    jnp.where(lane == 2, g1, jnp.where(lane == 3, g2, 0.0))))
    o_ref[...] = out


def router(x, w_pad, b_pad, tm):
    m = x.shape[0]
    return pl.pallas_call(
        _router_kernel, grid=(m // tm,),
        in_specs=[pl.BlockSpec((tm, D_MODEL), lambda i: (i, 0)), pl.BlockSpec((D_MODEL, LANES), lambda i: (0, 0)),
                  pl.BlockSpec((1, LANES), lambda i: (0, 0))],
        out_specs=pl.BlockSpec((tm, LANES), lambda i: (i, 0)),
        out_shape=jax.ShapeDtypeStruct((m, LANES), F32),
        compiler_params=_cp("arbitrary"), name="router")(x, w_pad, b_pad)


def _moe_ffn_kernel(te_ref, nu_ref, x_ref, wg_ref, wu_ref, wd_ref, o_ref, acc_ref, *, nf):
    i = pl.program_id(0)
    j = pl.program_id(1)
    used = i < nu_ref[0]

    @pl.when(used)
    def _():
        @pl.when(j == 0)
        def _():
            acc_ref[...] = jnp.zeros_like(acc_ref)

        acc_ref[...] += _swiglu_chunk(x_ref[...], wg_ref[0], wu_ref[0], wd_ref[0])

        @pl.when(j == nf - 1)
        def _():
            o_ref[...] = acc_ref[...]

    @pl.when(jnp.logical_and(jnp.logical_not(used), j == nf - 1))
    def _():
        o_ref[...] = jnp.zeros_like(o_ref)


def moe_ffn(x_sorted, tile_expert, n_used, wg, wu, wd, tf):
    n_rows = x_sorted.shape[0]
    n_tiles = n_rows // MOE_TILE
    nf = D_FF // tf

    def jj(i, j, nu):
        return jnp.where(i < nu[0], j, nf - 1)

    kern = functools.partial(_moe_ffn_kernel, nf=nf)
    gs = pltpu.PrefetchScalarGridSpec(
        num_scalar_prefetch=2, grid=(n_tiles, nf),
        in_specs=[pl.BlockSpec((MOE_TILE, D_MODEL), lambda i, j, te, nu: (i, 0)),
                  pl.BlockSpec((1, D_MODEL, tf), lambda i, j, te, nu: (te[i], 0, jj(i, j, nu))),
                  pl.BlockSpec((1, D_MODEL, tf), lambda i, j, te, nu: (te[i], 0, jj(i, j, nu))),
                  pl.BlockSpec((1, tf, D_MODEL), lambda i, j, te, nu: (te[i], jj(i, j, nu), 0))],
        out_specs=pl.BlockSpec((MOE_TILE, D_MODEL), lambda i, j, te, nu: (i, 0)),
        scratch_shapes=[pltpu.VMEM((MOE_TILE, D_MODEL), F32)])
    return pl.pallas_call(
        kern, grid_spec=gs, out_shape=jax.ShapeDtypeStruct((n_rows, D_MODEL), F32),
        compiler_params=_cp("arbitrary", "arbitrary"), name="moe_ffn")(
            tile_expert, n_used, x_sorted, wg, wu, wd)


def _combine_ln_kernel(x_ref, r_ref, ya_ref, yb_ref, g_ref, b_ref, o_ref):
    r = r_ref[...]
    y = r[:, TOP_K:TOP_K + 1] * ya_ref[...] + r[:, TOP_K + 1:TOP_K + 2] * yb_ref[...]
    o_ref[...] = _layer_norm(ALPHA * x_ref[...] + y, g_ref[...], b_ref[...])


def combine_ln(x, r, ya, yb, g, b, tm):
    m = x.shape[0]
    row = pl.BlockSpec((tm, D_MODEL), lambda i: (i, 0))
    vec = pl.BlockSpec((1, D_MODEL), lambda i: (0, 0))
    return pl.pallas_call(
        _combine_ln_kernel, grid=(m // tm,),
        in_specs=[row, pl.BlockSpec((tm, LANES), lambda i: (i, 0)), row, row, vec, vec], out_specs=row,
        out_shape=jax.ShapeDtypeStruct((m, D_MODEL), F32),
        compiler_params=_cp("arbitrary"), name="combine_ln")(x, r, ya, yb, g, b)


def moe_ln(x, router_w, router_b, wg, wu, wd, g, b, tm):
    m = x.shape[0]
    w_pad = jnp.zeros((D_MODEL, LANES), F32).at[:, :N_EXPERTS].set(router_w)
    b_pad = jnp.zeros((1, LANES), F32).at[0, :N_EXPERTS].set(router_b)
    r = router(x, w_pad, b_pad, tm)
    top_idx = r[:, :TOP_K].astype(jnp.int32)
    flat_e = top_idx.reshape(-1)
    n_slot = m * TOP_K
    onehot = (flat_e[:, None] == jnp.arange(N_EXPERTS, dtype=jnp.int32)[None, :]).astype(jnp.int32)
    rank = jnp.take_along_axis(jnp.cumsum(onehot, axis=0) - onehot, flat_e[:, None], axis=1)[:, 0]
    counts = jnp.sum(onehot, axis=0)
    tiles_per = (counts + MOE_TILE - 1) // MOE_TILE
    tile_end = jnp.cumsum(tiles_per)
    dest = (tile_end - tiles_per)[flat_e] * MOE_TILE + rank
    n_tiles = -(-n_slot // MOE_TILE) + N_EXPERTS
    n_rows = n_tiles * MOE_TILE
    row_tok = jnp.zeros((n_rows,), jnp.int32).at[dest].set(jnp.arange(n_slot, dtype=jnp.int32) // TOP_K)
    n_used = tile_end[-1:].astype(jnp.int32)
    tile_ids = jnp.minimum(jnp.arange(n_tiles, dtype=jnp.int32), n_used[0] - 1)
    tile_expert = jnp.minimum(jnp.searchsorted(tile_end, tile_ids, side='right'), N_EXPERTS - 1).astype(jnp.int32)
    x_sorted = x.astype(BF16)[row_tok]
    y_sorted = moe_ffn(x_sorted, tile_expert, n_used, wg, wu, wd, tf=256)
    dest2 = dest.reshape(m, TOP_K)
    return combine_ln(x, r, y_sorted[dest2[:, 0]], y_sorted[dest2[:, 1]], g, b, tm)


def kernel(x_prompt, x_sample, cache_sb_k, cache_sb_v, cache_mem_k, cache_mem_v, state_ml_C, state_ml_n, state_ml_m, state_rt_S, state_s5_re, state_s5_im, page_table, mem_prompt, w_in, sb_bias, ml_b_i, ml_b_f, ml_norm_g, rt_norm_g, s5_A_re, s5_A_im, s5_log_dt, s5_B_re, s5_B_im, s5_C_re, s5_C_im, s5_D, s5_glu_w, s5_glu_b, w_out, ca_wq, ca_wk, ca_wv, ca_wo, ln_g, ln_b, ffn_w_gate, ffn_w_up, ffn_w_down, moe_router_w, moe_router_b, moe_w_gate, moe_w_up, moe_w_down):
    bp, tp, _ = x_prompt.shape
    bs, ts, _ = x_sample.shape
    n_p, n_s = bp * tp, bs * ts
    tm = 640
    assert (n_p + n_s) % tm == 0 and n_p % ts == 0 and tp % 512 == 0
    x = jnp.concatenate([x_prompt.reshape(n_p, D_MODEL), x_sample.reshape(n_s, D_MODEL)], axis=0)
    uu = _suffix_matrix()
    g_off = 7 * GROUP_W
    half = HEAD_DIM // 2
    freq = ROPE_BASE ** (-jnp.arange(half, dtype=F32) / half)

    def rope_tables(pos):
        ang = pos.astype(F32)[:, None] * freq[None, :]
        cos, sin = jnp.cos(ang), jnp.sin(ang)
        return (jnp.tile(jnp.concatenate([cos, cos], axis=1), (1, N_HEADS)),
                jnp.tile(jnp.concatenate([-sin, sin], axis=1), (1, N_HEADS)))

    cos_p, sin_p = rope_tables(jnp.arange(tp, dtype=jnp.int32))
    cos_s, sin_s = rope_tables(PAST_LEN + jnp.arange(ts, dtype=jnp.int32))
    cache_kt = cache_sb_k.transpose(0, 1, 3, 4, 2)
    cache_vt = cache_sb_v.transpose(0, 1, 3, 4, 2)

    p_st = [[] for _ in range(10)]
    s_st = [[] for _ in range(8)]
    for l in range(DEPTH):
        wl = w_in[l]
        w_cat = jnp.concatenate([wl[:, :g_off], wl[:, g_off + 2 * N_HEADS:], wl[:, g_off:g_off + 2 * N_HEADS],
                                 jnp.zeros((D_MODEL, PROJ_W - wl.shape[1]), F32)], axis=1).astype(BF16)
        proj = linear(x, w_cat, tm)
        gate_bias = jnp.zeros((1, LANES), F32).at[0, :2 * N_HEADS].set(jnp.concatenate([ml_b_i[l], ml_b_f[l]]))
        ml_g = ml_norm_g[l][None, :]
        rt_g = rt_norm_g[l][None, :]
        wb, a1, a2, wc = _s5_weights(s5_A_re[l], s5_A_im[l], s5_log_dt[l], s5_B_re[l], s5_B_im[l],
                                     s5_C_re[l], s5_C_im[l])
        s5_d = s5_D[l][None, :]
        glu_w = s5_glu_w[l].astype(BF16)
        glu_b = s5_glu_b[l][None, :]

        o_sb_p = sb_attention_prompt(proj, sb_bias[l], uu, bp, tp, tq=256)
        o_ml_p, cn_p, m_p = mlstm_mixer(proj, 0, bp, tp, gate_bias, ml_g,
                                        jnp.zeros((bp, N_HEADS, HEAD_DIM, LANES), F32), jnp.zeros((bp, N_HEADS), F32))
        o_rt_p, rs_p = retention_mixer(proj, 0, bp, tp, cos_p, sin_p, rt_g,
                                       jnp.zeros((bp, N_HEADS, HEAD_DIM, HEAD_DIM), F32))
        u_p = proj[:n_p, C_SU * GROUP_W:(C_SU + 1) * GROUP_W].reshape(bp, tp, GROUP_W).transpose(1, 0, 2)
        o_s5_p, h5_p = s5_mixer(u_p.reshape(n_p, GROUP_W), bp, tp, 64, wb, a1, a2,
                                jnp.zeros((bp, 2 * S5_W), F32), wc, s5_d, glu_w, glu_b)
        o_s5_p = o_s5_p.reshape(tp, bp, GROUP_W).transpose(1, 0, 2).reshape(n_p, GROUP_W)

        o_sb_s = sb_attention_sample(proj, n_p // ts, cache_kt, cache_vt, page_table, l, sb_bias[l], uu, n_pp=16)
        cn0 = jnp.concatenate([state_ml_C[:, l], state_ml_n[:, l][..., None],
                               jnp.zeros((bs, N_HEADS, HEAD_DIM, LANES - HEAD_DIM - 1), F32)], axis=-1)
        o_ml_s, cn_s, m_s = mlstm_mixer(proj, n_p // ts, bs, ts, gate_bias, ml_g, cn0, state_ml_m[:, l])
        o_rt_s, rs_s = retention_mixer(proj, n_p // ts, bs, ts, cos_s, sin_s, rt_g, state_rt_S[:, l])
        u_s = proj[n_p:, C_SU * GROUP_W:(C_SU + 1) * GROUP_W].reshape(bs, ts, GROUP_W).transpose(1, 0, 2)
        h0_s = jnp.concatenate([state_s5_re[:, l].reshape(bs, S5_W), state_s5_im[:, l].reshape(bs, S5_W)], axis=1)
        o_s5_s, h5_s = s5_mixer(u_s.reshape(n_s, GROUP_W), bs, ts, ts, wb, a1, a2, h0_s, wc, s5_d, glu_w, glu_b)
        o_s5_s = o_s5_s.reshape(ts, bs, GROUP_W).transpose(1, 0, 2).reshape(n_s, GROUP_W)

        mixed = jnp.concatenate([jnp.concatenate([o_sb_p, o_ml_p, o_rt_p, o_s5_p], axis=1),
                                 jnp.concatenate([o_sb_s, o_ml_s, o_rt_s, o_s5_s], axis=1)], axis=0)
        x = linear_res_ln(mixed, w_out[l].astype(BF16), x, ln_g[l, 0][None, :], ln_b[l, 0][None, :], tm)

        mem_kv = linear(mem_prompt.reshape(bp * N_MEM, D_MODEL),
                        jnp.concatenate([ca_wk[l], ca_wv[l]], axis=1).astype(BF16), 512)
        mk_p = mem_kv[:, :GROUP_W].reshape(bp, N_MEM, GROUP_W)
        mv_p = mem_kv[:, GROUP_W:].reshape(bp, N_MEM, GROUP_W)
        wq = ca_wq[l].astype(BF16)
        wo = ca_wo[l].astype(BF16)
        g1, b1 = ln_g[l, 1][None, :], ln_b[l, 1][None, :]
        xa = cross_attn_ln(x, None, 0, bp, tp, 512, mk_p, mv_p, wq, wo, g1, b1)
        x = cross_attn_ln(x, xa, n_p // ts, bs, ts, ts, cache_mem_k[:, l].reshape(bs, N_MEM, GROUP_W),
                          cache_mem_v[:, l].reshape(bs, N_MEM, GROUP_W), wq, wo, g1, b1)

        g2, b2 = ln_g[l, 2][None, :], ln_b[l, 2][None, :]
        j = l // 2
        if l % 2 == 0:
            x = ffn_ln(x, ffn_w_gate[j], ffn_w_up[j], ffn_w_down[j], g2, b2, 2 * tm, tf=256)
        else:
            x = moe_ln(x, moe_router_w[j], moe_router_b[j], moe_w_gate[j], moe_w_up[j], moe_w_down[j], g2, b2, tm)

        def heads(a, nb_, t_):
            return a.reshape(nb_, t_, N_HEADS, HEAD_DIM)

        p_st[0].append(heads(proj[:n_p, C_SK * GROUP_W:(C_SK + 1) * GROUP_W], bp, tp))
        p_st[1].append(heads(proj[:n_p, C_SV * GROUP_W:(C_SV + 1) * GROUP_W], bp, tp))
        p_st[2].append(heads(mk_p, bp, N_MEM))
        p_st[3].append(heads(mv_p, bp, N_MEM))
        s_st[0].append(heads(proj[n_p:, C_SK * GROUP_W:(C_SK + 1) * GROUP_W], bs, ts))
        s_st[1].append(heads(proj[n_p:, C_SV * GROUP_W:(C_SV + 1) * GROUP_W], bs, ts))
        for st, cn, mm, rs, h5, nb_ in ((p_st, cn_p, m_p, rs_p, h5_p, bp), (s_st, cn_s, m_s, rs_s, h5_s, bs)):
            off = 4 if st is p_st else 2
            st[off + 0].append(cn[..., :HEAD_DIM])
            st[off + 1].append(cn[..., HEAD_DIM])
            st[off + 2].append(mm[:, 0, :N_HEADS])
            st[off + 3].append(rs)
            st[off + 4].append(h5[:, :S5_W].reshape(nb_, S5_GROUPS, S5_STATE))
            st[off + 5].append(h5[:, S5_W:].reshape(nb_, S5_GROUPS, S5_STATE))

    y_prompt = x[:n_p].reshape(bp, tp, D_MODEL)
    y_sample = x[n_p:].reshape(bs, ts, D_MODEL)
    p_out = [jnp.stack(a, axis=1) for a in p_st]
    s_out = [jnp.stack(a, axis=1) for a in s_st]
    return (y_prompt, y_sample, *p_out, *s_out)
```

```python
import functools
import math

import numpy as np
import jax
import jax.numpy as jnp
from jax import lax
from jax.experimental import pallas as pl
from jax.experimental.pallas import tpu as pltpu

F32 = jnp.float32
BF16 = jnp.bfloat16

D_MODEL = 1024
DEPTH = 2
PAST_LEN = 8192
PAGE_SIZE = 128
HEAD_DIM = 64
N_HEADS = 4
GROUP_W = N_HEADS * HEAD_DIM
S5_GROUPS = 16
S5_GROUP = 16
S5_STATE = 64
S5_W = S5_GROUPS * S5_STATE
N_MEM = 256
D_FF = 3584
N_EXPERTS = 8
TOP_K = 2
CHUNK = 64
ROPE_BASE = 10000.0
LN_EPS = 1e-5
GN_EPS = 1e-6
ALPHA = (2 * DEPTH) ** 0.25
QK_SCALE = HEAD_DIM ** -0.5
LOG2E = math.log2(math.e)

LANES = 128
PROJ_W = 25 * LANES
C_SQ, C_SK, C_SV, C_MQ, C_MK, C_MV, C_MO, C_RQ, C_RK, C_RV, C_RG, C_SU = range(12)
C_GATES = 12 * GROUP_W // LANES
VMEM_LIMIT = 48 * 1024 * 1024
MOE_TILE = 1024


def _cp(*sem):
    return pltpu.CompilerParams(dimension_semantics=sem, vmem_limit_bytes=VMEM_LIMIT)


def _dot(a, b):
    return jnp.dot(a, b, preferred_element_type=F32)


def _dot_nt(a, b):
    return lax.dot_general(a, b, (((1,), (1,)), ((), ())), preferred_element_type=F32)


def _dot_tn(a, b):
    return lax.dot_general(a, b, (((0,), (0,)), ((), ())), preferred_element_type=F32)


def _layer_norm(y, g, b):
    mu = jnp.mean(y, axis=-1, keepdims=True)
    yc = y - mu
    var = jnp.mean(yc * yc, axis=-1, keepdims=True)
    return yc * lax.rsqrt(var + LN_EPS) * g + b


def _row_sum(x, scale=1.0):
    ones = jnp.full((x.shape[1], LANES), scale, BF16)
    hi = x.astype(BF16)
    lo = (x - hi.astype(F32)).astype(BF16)
    return (_dot(hi, ones) + _dot(lo, ones))[:, :x.shape[1]]


def _cumsum_rows(tril, x):
    hi = x.astype(BF16)
    lo = (x - hi.astype(F32)).astype(BF16)
    return _dot(tril, hi) + _dot(tril, lo)


def _head_norm_all(hd):
    inv = 1.0 / HEAD_DIM
    mu = {p: _row_sum(x, inv) for p, x in hd.items()}
    hc = {p: hd[p] - mu[p] for p in hd}
    var = {p: _row_sum(hc[p] * hc[p], inv) for p in hd}
    return {p: hc[p] * lax.rsqrt(var[p] + GN_EPS) for p in hd}


def _neg_softplus(z):
    return -(jnp.maximum(z, 0.0) + jnp.log1p(jnp.exp(-jnp.abs(z))))


def _log_sigmoid(z):
    return _neg_softplus(-z)


def _linear_kernel(x_ref, w_ref, o_ref):
    o_ref[...] = _dot(x_ref[...].astype(BF16), w_ref[...]).astype(o_ref.dtype)


def linear(x, w, tm, row_blk0=0, n_rows=None, out_dtype=F32):
    m, k = x.shape
    m = m if n_rows is None else n_rows
    n = w.shape[1]
    return pl.pallas_call(
        _linear_kernel, grid=(m // tm,),
        in_specs=[pl.BlockSpec((tm, k), lambda i: (row_blk0 + i, 0)), pl.BlockSpec((k, n), lambda i: (0, 0))],
        out_specs=pl.BlockSpec((tm, n), lambda i: (i, 0)),
        out_shape=jax.ShapeDtypeStruct((m, n), out_dtype),
        compiler_params=_cp("arbitrary"), name="linear")(x, w)


def _linear_res_ln_kernel(x_ref, w_ref, r_ref, g_ref, b_ref, o_ref):
    h = _dot(x_ref[...].astype(BF16), w_ref[...])
    o_ref[...] = _layer_norm(ALPHA * r_ref[...] + h, g_ref[...], b_ref[...])


def linear_res_ln(x, w, res, g, b, tm):
    m, k = x.shape
    n = w.shape[1]
    return pl.pallas_call(
        _linear_res_ln_kernel, grid=(m // tm,),
        in_specs=[pl.BlockSpec((tm, k), lambda i: (i, 0)), pl.BlockSpec((k, n), lambda i: (0, 0)),
                  pl.BlockSpec((tm, n), lambda i: (i, 0)),
                  pl.BlockSpec((1, n), lambda i: (0, 0)), pl.BlockSpec((1, n), lambda i: (0, 0))],
        out_specs=pl.BlockSpec((tm, n), lambda i: (i, 0)),
        out_shape=jax.ShapeDtypeStruct((m, n), F32),
        compiler_params=_cp("arbitrary"), name="linear_res_ln")(x, w, res, g, b)


def _suffix_matrix():
    j = np.arange(LANES)
    u = (j[:, None] >= j[None, :]).astype(np.float32)
    uu = np.concatenate([u, np.ones((LANES, LANES), np.float32)], axis=1)
    return jnp.asarray(np.concatenate([uu, uu], axis=0), dtype=BF16)


def _suffix_sums(lr, uu):
    hi = lr.astype(BF16)
    lo = (lr - hi.astype(F32)).astype(BF16)
    r = _dot(jnp.concatenate([hi, lo], axis=1), uu)
    return r[:, :LANES], r[:, LANES:]


def _log2_rem(z2):
    return jnp.minimum(-z2, 0.0) - jnp.log2(1.0 + jnp.exp2(-jnp.abs(z2)))


def _sb_prompt_kernel(bias_ref, q_ref, k_ref, v_ref, uu_ref, o_ref, acc_ref, car_ref, *, tq):
    i = pl.program_id(1)
    tk = LANES
    nsub = tq // tk
    acc_ref[...] = jnp.zeros_like(acc_ref)
    car_ref[...] = jnp.zeros_like(car_ref)
    q = (q_ref[...] * (QK_SCALE * LOG2E)).astype(BF16)
    qh = [q[:, h * HEAD_DIM:(h + 1) * HEAD_DIM] for h in range(N_HEADS)]
    b2 = [bias_ref[h] * LOG2E for h in range(N_HEADS)]
    uu = uu_ref[...]
    row = lax.broadcasted_iota(jnp.int32, (tq, tk), 0)
    col = lax.broadcasted_iota(jnp.int32, (tq, tk), 1)
    heads = range(N_HEADS)

    def block(j, causal):
        r0 = pl.multiple_of(j * tk, tk)
        kb = k_ref[pl.ds(r0, tk), :].astype(BF16)
        vb = v_ref[pl.ds(r0, tk), :].astype(BF16)
        z2 = [_dot_nt(qh[h], kb[:, h * HEAD_DIM:(h + 1) * HEAD_DIM]) + b2[h] for h in heads]
        lr = [_log2_rem(z) for z in z2]
        if causal is not None:
            lr = [jnp.where(causal, a, 0.0) for a in lr]
        cs, tot = _suffix_sums(jnp.concatenate(lr, axis=0), uu)
        car = [car_ref[h] for h in heads]
        w = [jnp.exp2(z2[h] + cs[h * tq:(h + 1) * tq] + car[h]) for h in heads]
        if causal is not None:
            w = [jnp.where(causal, a, 0.0) for a in w]
        pv = [_dot(w[h].astype(BF16), vb[:, h * HEAD_DIM:(h + 1) * HEAD_DIM]) for h in heads]
        for h in heads:
            acc_ref[h] += pv[h]
            car_ref[h] = car[h] + tot[h * tq:(h + 1) * tq]

    for d in range(nsub):
        block(i * nsub + (nsub - 1 - d), (col + (nsub - 1 - d) * tk) < row)

    def body(jj, carry):
        block(i * nsub - 1 - jj, None)
        return carry

    lax.fori_loop(0, i * nsub, body, 0)
    o_ref[...] = jnp.concatenate([acc_ref[h] for h in range(N_HEADS)], axis=1)


def sb_attention_prompt(proj, sb_bias, uu, n_batch, seq, tq):
    nq = seq // tq
    kern = functools.partial(_sb_prompt_kernel, tq=tq)
    return pl.pallas_call(
        kern, grid=(n_batch, nq),
        in_specs=[pl.BlockSpec(memory_space=pltpu.SMEM),
                  pl.BlockSpec((tq, GROUP_W), lambda b, i: (b * nq + i, C_SQ)),
                  pl.BlockSpec((seq, GROUP_W), lambda b, i: (b, C_SK)),
                  pl.BlockSpec((seq, GROUP_W), lambda b, i: (b, C_SV)),
                  pl.BlockSpec((2 * LANES, 2 * LANES), lambda b, i: (0, 0))],
        out_specs=pl.BlockSpec((tq, GROUP_W), lambda b, i: (b * nq + i, 0)),
        out_shape=jax.ShapeDtypeStruct((n_batch * seq, GROUP_W), F32),
        scratch_shapes=[pltpu.VMEM((N_HEADS, tq, HEAD_DIM), F32), pltpu.VMEM((N_HEADS, tq, LANES), F32)],
        compiler_params=_cp("arbitrary", "arbitrary"), name="sb_prompt")(sb_bias, proj, proj, proj, uu)


def _sb_sample_kernel(pt_ref, bias_ref, q_ref, kn_ref, vn_ref, u8_ref, uu_ref, *rest, n_pp, n_steps):
    k_refs = rest[:n_pp]
    v_refs = rest[n_pp:2 * n_pp]
    o_ref = rest[2 * n_pp]
    acc_ref, car_ref = rest[2 * n_pp + 1:]
    s = pl.program_id(1)
    nq = q_ref.shape[0]
    q = (q_ref[...] * (QK_SCALE * LOG2E)).astype(BF16)
    qh = [q[:, h * HEAD_DIM:(h + 1) * HEAD_DIM] for h in range(N_HEADS)]
    b2 = [bias_ref[h] * LOG2E for h in range(N_HEADS)]

    @pl.when(s == 0)
    def _():
        kn = kn_ref[...].astype(BF16)
        vn = vn_ref[...].astype(BF16)
        t = lax.broadcasted_iota(jnp.int32, (nq, nq), 0)
        c = lax.broadcasted_iota(jnp.int32, (nq, nq), 1)
        causal = c < t
        for h in range(N_HEADS):
            sl = slice(h * HEAD_DIM, (h + 1) * HEAD_DIM)
            z2 = _dot_nt(qh[h], kn[:, sl]) + b2[h]
            lr = jnp.where(causal, _log2_rem(z2), 0.0)
            cs = jnp.dot(lr, u8_ref[...], preferred_element_type=F32, precision=lax.Precision.HIGHEST)
            w = jnp.where(causal, jnp.exp2(z2 + cs), 0.0)
            acc_ref[h] = _dot(w.astype(BF16), vn[:, sl])
            car_ref[h] = jnp.broadcast_to(jnp.sum(lr, axis=1, keepdims=True), (nq, LANES))

    uu = uu_ref[...]
    z2s = []
    for p in range(n_pp):
        for h in range(N_HEADS):
            z2s.append(_dot(qh[h], k_refs[p][0, 0, h].astype(BF16)) + b2[h])
    cs_all, tot_all = _suffix_sums(_log2_rem(jnp.concatenate(z2s, axis=0)), uu)
    for h in range(N_HEADS):
        car = car_ref[h]
        acc = acc_ref[h]
        for p in range(n_pp):
            r = (p * N_HEADS + h) * nq
            w = jnp.exp2(z2s[p * N_HEADS + h] + cs_all[r:r + nq] + car)
            acc = acc + _dot_nt(w.astype(BF16), v_refs[p][0, 0, h].astype(BF16))
            car = car + tot_all[r:r + nq]
        car_ref[h] = car
        acc_ref[h] = acc

    @pl.when(s == n_steps - 1)
    def _():
        o_ref[...] = jnp.concatenate([acc_ref[h] for h in range(N_HEADS)], axis=1)


def sb_attention_sample(proj, row_blk0, cache_kt, cache_vt, page_table, layer, sb_bias, uu, n_pp):
    n_batch, n_pages = page_table.shape
    nq = 8
    n_steps = n_pages // n_pp
    u8 = jnp.asarray((np.arange(nq)[:, None] >= np.arange(nq)[None, :]).astype(np.float32))

    def page_spec(p):
        return pl.BlockSpec((1, 1, N_HEADS, HEAD_DIM, PAGE_SIZE),
                            lambda b, s, pt: (pt[b, n_pages - 1 - (s * n_pp + p)], layer, 0, 0, 0))

    def row_spec(cblk):
        return pl.BlockSpec((nq, GROUP_W), lambda b, s, pt: (row_blk0 + b, cblk))

    kern = functools.partial(_sb_sample_kernel, n_pp=n_pp, n_steps=n_steps)
    gs = pltpu.PrefetchScalarGridSpec(
        num_scalar_prefetch=1, grid=(n_batch, n_steps),
        in_specs=[pl.BlockSpec(memory_space=pltpu.SMEM), row_spec(C_SQ), row_spec(C_SK), row_spec(C_SV),
                  pl.BlockSpec((nq, nq), lambda b, s, pt: (0, 0)),
                  pl.BlockSpec((2 * LANES, 2 * LANES), lambda b, s, pt: (0, 0))]
                 + [page_spec(p) for p in range(n_pp)] * 2,
        out_specs=pl.BlockSpec((nq, GROUP_W), lambda b, s, pt: (b, 0)),
        scratch_shapes=[pltpu.VMEM((N_HEADS, nq, HEAD_DIM), F32), pltpu.VMEM((N_HEADS, nq, LANES), F32)])
    return pl.pallas_call(
        kern, grid_spec=gs, out_shape=jax.ShapeDtypeStruct((n_batch * nq, GROUP_W), F32),
        compiler_params=_cp("arbitrary", "arbitrary"), name="sb_sample")(
            page_table, sb_bias, proj, proj, proj, u8, uu, *([cache_kt] * n_pp), *([cache_vt] * n_pp))


def _mlstm_kernel(m0_ref, q_ref, k_ref, v_ref, og_ref, gt_ref, gb_ref, ng_ref, tril_ref, cn0_ref,
                  o_ref, cn_ref, m_ref, ms_ref, *, bb, chunk, n_t):
    bi = pl.program_id(0)
    t = pl.program_id(1)
    seqs = range(bb)
    pairs = [(b, h) for b in seqs for h in range(N_HEADS)]

    @pl.when(t == 0)
    def _():
        cn_ref[...] = cn0_ref[...]
        for b, h in pairs:
            ms_ref[b * N_HEADS + h] = jnp.full((1, LANES), m0_ref[bi * bb + b, h], F32)

    tril = tril_ref[...]
    tri_mask = lax.broadcasted_iota(jnp.int32, (chunk, chunk), 1) <= lax.broadcasted_iota(jnp.int32, (chunk, chunk), 0)
    lane = lax.broadcasted_iota(jnp.int32, (chunk, HEAD_DIM), 1)
    ones_col = jnp.where(lane == 0, 1.0, 0.0).astype(F32)
    hs = lambda h: slice(h * HEAD_DIM, (h + 1) * HEAD_DIM)

    gt = [gt_ref[b] + gb_ref[...] for b in seqs]
    bc = [_cumsum_rows(tril, _log_sigmoid(g)) for g in gt]
    gt_t = [g.T for g in gt]
    bc_t = [x.T for x in bc]
    q = [q_ref[b].astype(BF16) for b in seqs]
    k = [(k_ref[b] * QK_SCALE).astype(BF16) for b in seqs]
    v = [v_ref[b] for b in seqs]
    m_prev = {p: ms_ref[p[0] * N_HEADS + p[1]][:, :1] for p in pairs}
    ig_col = {(b, h): gt[b][:, h:h + 1] for b, h in pairs}
    bc_col = {(b, h): bc[b][:, N_HEADS + h:N_HEADS + h + 1] for b, h in pairs}
    dm = {(b, h): jnp.where(tri_mask, bc_col[b, h] - (bc_t[b][N_HEADS + h:N_HEADS + h + 1, :] - gt_t[b][h:h + 1, :]),
                            -jnp.inf) for b, h in pairs}
    a = {p: bc_col[p] + m_prev[p] for p in pairs}
    m_new = {p: jnp.maximum(a[p], jnp.max(dm[p], axis=1, keepdims=True)) for p in pairs}
    inter = {p: jnp.exp(a[p] - m_new[p]) for p in pairs}
    s = {(b, h): _dot_nt(q[b][:, hs(h)], k[b][:, hs(h)]) * jnp.exp(dm[b, h] - m_new[b, h]) for b, h in pairs}
    v_ext = {(b, h): jnp.concatenate([v[b][:, hs(h)], ones_col], axis=1) for b, h in pairs}
    qc = {(b, h): _dot(q[b][:, hs(h)], cn_ref[b, h].astype(BF16)) for b, h in pairs}
    sv = {p: _dot(s[p].astype(BF16), v_ext[p].astype(BF16)) for p in pairs}
    m_last = {p: m_new[p][chunk - 1:chunk, :] for p in pairs}
    wl = {p: jnp.exp(bc_col[p][chunk - 1:chunk, :] - bc_col[p] + ig_col[p] - m_last[p]) for p in pairs}
    dl = {p: jnp.exp(a[p][chunk - 1:chunk, :] - m_last[p]) for p in pairs}
    upd = {(b, h): _dot_tn(k[b][:, hs(h)], (wl[b, h] * v_ext[b, h]).astype(BF16)) for b, h in pairs}
    rs = {p: _row_sum(s[p])[:, :1] for p in pairs}
    num = {p: inter[p] * qc[p][:, :HEAD_DIM] + sv[p][:, :HEAD_DIM] for p in pairs}
    den = {p: inter[p] * qc[p][:, HEAD_DIM:HEAD_DIM + 1] + rs[p] for p in pairs}
    hh = _head_norm_all({p: num[p] / jnp.maximum(jnp.abs(den[p]), jnp.exp(-m_new[p])) for p in pairs})
    for b, h in pairs:
        cn_ref[b, h] = dl[b, h] * cn_ref[b, h] + upd[b, h]
        ms_ref[b * N_HEADS + h] = jnp.broadcast_to(m_last[b, h], (1, LANES))
    for b in seqs:
        y = jnp.concatenate([hh[b, h] for h in range(N_HEADS)], axis=1)
        o_ref[b] = y * ng_ref[...] * jax.nn.sigmoid(og_ref[b])

    @pl.when(t == n_t - 1)
    def _():
        lane_m = lax.broadcasted_iota(jnp.int32, (1, LANES), 1)
        for b in seqs:
            m_out = jnp.zeros((1, LANES), F32)
            for h in range(N_HEADS):
                m_out = jnp.where(lane_m == h, ms_ref[b * N_HEADS + h], m_out)
            m_ref[b] = m_out


def mlstm_mixer(proj3, gate_bias, norm_g, cn0, m0, bb):
    n_batch, seq, _ = proj3.shape
    chunk = math.gcd(seq, CHUNK)
    n_t = seq // chunk
    tril = jnp.asarray(np.tril(np.ones((chunk, chunk), np.float32)), dtype=BF16)

    def row_spec(cblk, w=GROUP_W):
        return pl.BlockSpec((bb, chunk, w), lambda bi, t: (bi, t, cblk))

    def const_spec(shape):
        return pl.BlockSpec(shape, lambda bi, t: (0,) * len(shape))

    state_spec = pl.BlockSpec((bb, N_HEADS, HEAD_DIM, LANES), lambda bi, t: (bi, 0, 0, 0))
    kern = functools.partial(_mlstm_kernel, bb=bb, chunk=chunk, n_t=n_t)
    return pl.pallas_call(
        kern, grid=(n_batch // bb, n_t),
        in_specs=[pl.BlockSpec(memory_space=pltpu.SMEM),
                  row_spec(C_MQ), row_spec(C_MK), row_spec(C_MV), row_spec(C_MO), row_spec(C_GATES, LANES),
                  const_spec((1, LANES)), const_spec((1, GROUP_W)), const_spec((chunk, chunk)), state_spec],
        out_specs=[pl.BlockSpec((bb, chunk, GROUP_W), lambda bi, t: (bi, t, 0)), state_spec,
                   pl.BlockSpec((bb, 1, LANES), lambda bi, t: (bi, 0, 0))],
        out_shape=[jax.ShapeDtypeStruct((n_batch, seq, GROUP_W), F32),
                   jax.ShapeDtypeStruct((n_batch, N_HEADS, HEAD_DIM, LANES), F32),
                   jax.ShapeDtypeStruct((n_batch, 1, LANES), F32)],
        scratch_shapes=[pltpu.VMEM((bb * N_HEADS, 1, LANES), F32)],
        compiler_params=_cp("arbitrary", "arbitrary"), name="mlstm")(
            m0, proj3, proj3, proj3, proj3, proj3, gate_bias, norm_g, tril, cn0)


def _rope(x, cos, sin_signed):
    lane = lax.broadcasted_iota(jnp.int32, x.shape, 1)
    half = HEAD_DIM // 2
    swapped = jnp.where((lane % HEAD_DIM) < half, pltpu.roll(x, x.shape[1] - half, 1), pltpu.roll(x, half, 1))
    return x * cos + swapped * sin_signed


def _retention_kernel(q_ref, k_ref, v_ref, gg_ref, cos_ref, sin_ref, ng_ref, dec_ref, int_ref, wl_ref, dl_ref,
                      s0_ref, o_ref, s_ref, *, bb):
    t = pl.program_id(1)
    seqs = range(bb)
    pairs = [(b, h) for b in seqs for h in range(N_HEADS)]
    hs = lambda h: slice(h * HEAD_DIM, (h + 1) * HEAD_DIM)

    @pl.when(t == 0)
    def _():
        s_ref[...] = s0_ref[...]

    cos = cos_ref[...]
    sin = sin_ref[...]
    q = [_rope(q_ref[b], cos, sin).astype(BF16) for b in seqs]
    k = [(_rope(k_ref[b], cos, sin) * QK_SCALE).astype(BF16) for b in seqs]
    v = [v_ref[b] for b in seqs]
    s = {(b, h): _dot_nt(q[b][:, hs(h)], k[b][:, hs(h)]) * dec_ref[h] for b, h in pairs}
    qs = {(b, h): _dot(q[b][:, hs(h)], s_ref[b, h].astype(BF16)) for b, h in pairs}
    sv = {(b, h): _dot(s[b, h].astype(BF16), v[b][:, hs(h)].astype(BF16)) for b, h in pairs}
    upd = {(b, h): _dot_tn(k[b][:, hs(h)], (wl_ref[h] * v[b][:, hs(h)]).astype(BF16)) for b, h in pairs}
    o = _head_norm_all({(b, h): int_ref[h] * qs[b, h] + sv[b, h] for b, h in pairs})
    for b, h in pairs:
        s_ref[b, h] = dl_ref[h] * s_ref[b, h] + upd[b, h]
    for b in seqs:
        gg = gg_ref[b]
        o_ref[b] = jnp.concatenate([o[b, h] for h in range(N_HEADS)], axis=1) * ng_ref[...] * (gg * jax.nn.sigmoid(gg))


def _retention_consts(chunk):
    log_g = np.log(1.0 - np.exp2(-5.0 - np.arange(N_HEADS, dtype=np.float64)))
    tau = np.arange(chunk, dtype=np.float64)
    rel = tau[:, None] - tau[None, :]
    decay = np.where(rel >= 0, np.exp(log_g[:, None, None] * np.maximum(rel, 0.0)), 0.0)
    inter = np.exp(log_g[:, None] * (tau + 1.0))[..., None]
    wl = np.exp(log_g[:, None] * (chunk - 1.0 - tau))[..., None]
    dl = np.exp(log_g * chunk)[:, None, None]
    return tuple(jnp.asarray(a, F32) for a in (decay, inter, wl, dl))


def retention_mixer(proj3, cos, sin_signed, norm_g, s0, bb):
    n_batch, seq, _ = proj3.shape
    chunk = math.gcd(seq, CHUNK)
    dec, inter, wl, dl = _retention_consts(chunk)

    def row_spec(cblk):
        return pl.BlockSpec((bb, chunk, GROUP_W), lambda bi, t: (bi, t, cblk))

    def const_spec(shape):
        return pl.BlockSpec(shape, lambda bi, t: (0,) * len(shape))

    pos_spec = pl.BlockSpec((chunk, GROUP_W), lambda bi, t: (t, 0))
    state_spec = pl.BlockSpec((bb, N_HEADS, HEAD_DIM, HEAD_DIM), lambda bi, t: (bi, 0, 0, 0))
    kern = functools.partial(_retention_kernel, bb=bb)
    return pl.pallas_call(
        kern, grid=(n_batch // bb, seq // chunk),
        in_specs=[row_spec(C_RQ), row_spec(C_RK), row_spec(C_RV), row_spec(C_RG), pos_spec, pos_spec,
                  const_spec((1, GROUP_W)), const_spec(dec.shape), const_spec(inter.shape), const_spec(wl.shape),
                  const_spec(dl.shape), state_spec],
        out_specs=[pl.BlockSpec((bb, chunk, GROUP_W), lambda bi, t: (bi, t, 0)), state_spec],
        out_shape=[jax.ShapeDtypeStruct((n_batch, seq, GROUP_W), F32),
                   jax.ShapeDtypeStruct((n_batch, N_HEADS, HEAD_DIM, HEAD_DIM), F32)],
        compiler_params=_cp("arbitrary", "arbitrary"), name="retention")(
            proj3, proj3, proj3, proj3, cos, sin_signed, norm_g, dec, inter, wl, dl, s0)


def _s5_kernel(u_ref, wb_ref, a1_ref, a2_ref, h0_ref, wc_ref, d_ref, gw_ref, gb_ref, o_ref, hl_ref, hs_ref, *, nb, tt):
    c = pl.program_id(0)

    @pl.when(c == 0)
    def _():
        hl_ref[...] = h0_ref[...]

    u = u_ref[...]
    hs_ref[...] = _dot(u.astype(BF16), wb_ref[...])
    a1 = jnp.broadcast_to(a1_ref[...], (nb, 2 * S5_W))
    a2 = jnp.broadcast_to(a2_ref[...], (nb, 2 * S5_W))

    def step(t, h):
        r0 = pl.multiple_of(t * nb, nb)
        swapped = jnp.concatenate([h[:, S5_W:], h[:, :S5_W]], axis=1)
        h = a1 * h + a2 * swapped + hs_ref[pl.ds(r0, nb), :]
        hs_ref[pl.ds(r0, nb), :] = h
        return h

    hl_ref[...] = lax.fori_loop(0, tt, step, hl_ref[...])
    y = _dot(hs_ref[...].astype(BF16), wc_ref[...]) + d_ref[...] * u
    g5 = jax.nn.gelu(y)
    o_ref[...] = g5 * jax.nn.sigmoid(_dot(g5.astype(BF16), gw_ref[...]) + gb_ref[...])


def s5_mixer(u_tm, nb, seq, tt, wb, a1, a2, h0, wc, d, glu_w, glu_b):
    rows = tt * nb

    def const_spec(shape):
        return pl.BlockSpec(shape, lambda c: (0,) * len(shape))

    kern = functools.partial(_s5_kernel, nb=nb, tt=tt)
    return pl.pallas_call(
        kern, grid=(seq // tt,),
        in_specs=[pl.BlockSpec((rows, GROUP_W), lambda c: (c, 0)),
                  const_spec(wb.shape), const_spec(a1.shape), const_spec(a2.shape), const_spec(h0.shape),
                  const_spec(wc.shape), const_spec(d.shape), const_spec(glu_w.shape), const_spec(glu_b.shape)],
        out_specs=[pl.BlockSpec((rows, GROUP_W), lambda c: (c, 0)), const_spec(h0.shape)],
        out_shape=[jax.ShapeDtypeStruct((seq * nb, GROUP_W), F32), jax.ShapeDtypeStruct(h0.shape, F32)],
        scratch_shapes=[pltpu.VMEM((rows, 2 * S5_W), F32)],
        compiler_params=_cp("arbitrary"), name="s5")(u_tm, wb, a1, a2, h0, wc, d, glu_w, glu_b)


def _s5_weights(a_re, a_im, log_dt, b_re, b_im, c_re, c_im):
    lam = lax.complex(a_re, a_im)
    a_bar = jnp.exp(lam * jnp.exp(log_dt))
    b_bar = ((a_bar - 1.0) / lam)[..., None] * lax.complex(b_re, b_im)
    eye = jnp.eye(S5_GROUPS, dtype=F32)

    def in_map(m):
        return jnp.einsum('gpc,gh->gchp', m, eye).reshape(S5_GROUPS * S5_GROUP, S5_W)

    def out_map(m):
        return jnp.einsum('gcp,gh->gphc', m, eye).reshape(S5_W, S5_GROUPS * S5_GROUP)

    wb = jnp.concatenate([in_map(b_bar.real), in_map(b_bar.imag)], axis=1).astype(BF16)
    wc = jnp.concatenate([out_map(c_re), -out_map(c_im)], axis=0).astype(BF16)
    ar = a_bar.real.reshape(1, S5_W)
    ai = a_bar.imag.reshape(1, S5_W)
    return wb, jnp.concatenate([ar, ar], axis=1), jnp.concatenate([-ai, ai], axis=1), wc


def _cross_attn_kernel(x_ref, wq_ref, k_ref, v_ref, wo_ref, g_ref, b_ref, o_ref):
    x = x_ref[...]
    q = (_dot(x.astype(BF16), wq_ref[...]) * QK_SCALE).astype(BF16)
    k = k_ref[0].astype(BF16)
    v = v_ref[0].astype(BF16)
    hs = [slice(h * HEAD_DIM, (h + 1) * HEAD_DIM) for h in range(N_HEADS)]
    s = [_dot_nt(q[:, sl], k[:, sl]) for sl in hs]
    e = [jnp.exp(a - jnp.max(a, axis=1, keepdims=True)) for a in s]
    p = [a / jnp.sum(a, axis=1, keepdims=True) for a in e]
    o = jnp.concatenate([_dot(p[h].astype(BF16), v[:, hs[h]]) for h in range(N_HEADS)], axis=1)
    y = ALPHA * x + _dot(o.astype(BF16), wo_ref[...])
    o_ref[...] = _layer_norm(y, g_ref[...], b_ref[...])


def cross_attn_ln(x, row_blk0, n_batch, seq, tq, mem_k, mem_v, wq, wo, g, b):
    nq = seq // tq

    def const_spec(shape):
        return pl.BlockSpec(shape, lambda bb, i: (0,) * len(shape))

    row_spec = pl.BlockSpec((tq, D_MODEL), lambda bb, i: (row_blk0 + bb * nq + i, 0))
    mem_spec = pl.BlockSpec((1, N_MEM, GROUP_W), lambda bb, i: (bb, 0, 0))
    return pl.pallas_call(
        _cross_attn_kernel, grid=(n_batch, nq),
        in_specs=[row_spec, const_spec(wq.shape), mem_spec, mem_spec, const_spec(wo.shape),
                  const_spec(g.shape), const_spec(b.shape)],
        out_specs=row_spec, out_shape=jax.ShapeDtypeStruct(x.shape, F32), input_output_aliases={0: 0},
        compiler_params=_cp("arbitrary", "arbitrary"), name="cross_attn")(x, wq, mem_k, mem_v, wo, g, b)


def _swiglu_chunk(xb, wg, wu, wd):
    gate = _dot(xb, wg.astype(BF16))
    up = _dot(xb, wu.astype(BF16))
    hid = (gate * jax.nn.sigmoid(gate) * up).astype(BF16)
    return _dot(hid, wd.astype(BF16))


def _ffn_kernel(x_ref, wg_ref, wu_ref, wd_ref, g_ref, b_ref, o_ref, xb_ref, *, nf):
    j = pl.program_id(1)

    @pl.when(j == 0)
    def _():
        xb_ref[...] = x_ref[...].astype(BF16)

    y = _swiglu_chunk(xb_ref[...], wg_ref[...], wu_ref[...], wd_ref[...])

    @pl.when(j == 0)
    def _():
        o_ref[...] = y

    @pl.when(jnp.logical_and(j > 0, j < nf - 1))
    def _():
        o_ref[...] += y

    @pl.when(j == nf - 1)
    def _():
        o_ref[...] = _layer_norm(ALPHA * x_ref[...] + (o_ref[...] + y), g_ref[...], b_ref[...])


def ffn_ln(x, wg, wu, wd, g, b, tm, tf):
    m = x.shape[0]
    nf = D_FF // tf
    kern = functools.partial(_ffn_kernel, nf=nf)
    return pl.pallas_call(
        kern, grid=(m // tm, nf),
        in_specs=[pl.BlockSpec((tm, D_MODEL), lambda i, j: (i, 0)),
                  pl.BlockSpec((D_MODEL, tf), lambda i, j: (0, j)), pl.BlockSpec((D_MODEL, tf), lambda i, j: (0, j)),
                  pl.BlockSpec((tf, D_MODEL), lambda i, j: (j, 0)),
                  pl.BlockSpec((1, D_MODEL), lambda i, j: (0, 0)), pl.BlockSpec((1, D_MODEL), lambda i, j: (0, 0))],
        out_specs=pl.BlockSpec((tm, D_MODEL), lambda i, j: (i, 0)),
        out_shape=jax.ShapeDtypeStruct((m, D_MODEL), F32),
        scratch_shapes=[pltpu.VMEM((tm, D_MODEL), BF16)],
        compiler_params=_cp("arbitrary", "arbitrary"), name="ffn")(x, wg, wu, wd, g, b)


def _router_kernel(x_ref, w_ref, b_ref, o_ref):
    logits = jnp.dot(x_ref[...], w_ref[...], preferred_element_type=F32, precision=lax.Precision.HIGHEST) + b_ref[...]
    lane = lax.broadcasted_iota(jnp.int32, logits.shape, 1)
    neg = jnp.float32(-jnp.inf)
    lg = jnp.where(lane < N_EXPERTS, logits, neg)
    m1 = jnp.max(lg, axis=1, keepdims=True)
    i1 = jnp.min(jnp.where(lg == m1, lane, LANES), axis=1, keepdims=True)
    lg2 = jnp.where(lane == i1, neg, lg)
    m2 = jnp.max(lg2, axis=1, keepdims=True)
    i2 = jnp.min(jnp.where(lg2 == m2, lane, LANES), axis=1, keepdims=True)
    e2 = jnp.exp(m2 - m1)
    g1 = 1.0 / (1.0 + e2)
    g2 = e2 / (1.0 + e2)
    out = jnp.where(lane == 0, i1.astype(F32), jnp.where(lane == 1, i2.astype(F32),
                    jnp.where(lane == 2, g1, jnp.where(lane == 3, g2, 0.0))))
    o_ref[...] = out


def router(x, w_pad, b_pad, tm):
    m = x.shape[0]
    return pl.pallas_call(
        _router_kernel, grid=(m // tm,),
        in_specs=[pl.BlockSpec((tm, D_MODEL), lambda i: (i, 0)), pl.BlockSpec((D_MODEL, LANES), lambda i: (0, 0)),
                  pl.BlockSpec((1, LANES), lambda i: (0, 0))],
        out_specs=pl.BlockSpec((tm, LANES), lambda i: (i, 0)),
        out_shape=jax.ShapeDtypeStruct((m, LANES), F32),
        compiler_params=_cp("arbitrary"), name="router")(x, w_pad, b_pad)


def _moe_ffn_kernel(te_ref, nu_ref, x_ref, wg_ref, wu_ref, wd_ref, o_ref, xb_ref):
    i = pl.program_id(0)
    j = pl.program_id(1)
    used = i < nu_ref[0]

    @pl.when(used)
    def _():
        @pl.when(j == 0)
        def _():
            xb_ref[...] = x_ref[...].astype(BF16)

        y = _swiglu_chunk(xb_ref[...], wg_ref[0], wu_ref[0], wd_ref[0])

        @pl.when(j == 0)
        def _():
            o_ref[...] = y

        @pl.when(j > 0)
        def _():
            o_ref[...] += y

    @pl.when(jnp.logical_and(jnp.logical_not(used), j == 0))
    def _():
        o_ref[...] = jnp.zeros_like(o_ref)


def moe_ffn(x_sorted, tile_expert, n_used, wg, wu, wd, tf):
    n_rows = x_sorted.shape[0]
    n_tiles = n_rows // MOE_TILE
    nf = D_FF // tf

    def jj(i, j, nu):
        return jnp.where(i < nu[0], j, nf - 1)

    kern = _moe_ffn_kernel
    gs = pltpu.PrefetchScalarGridSpec(
        num_scalar_prefetch=2, grid=(n_tiles, nf),
        in_specs=[pl.BlockSpec((MOE_TILE, D_MODEL), lambda i, j, te, nu: (i, 0)),
                  pl.BlockSpec((1, D_MODEL, tf), lambda i, j, te, nu: (te[i], 0, jj(i, j, nu))),
                  pl.BlockSpec((1, D_MODEL, tf), lambda i, j, te, nu: (te[i], 0, jj(i, j, nu))),
                  pl.BlockSpec((1, tf, D_MODEL), lambda i, j, te, nu: (te[i], jj(i, j, nu), 0))],
        out_specs=pl.BlockSpec((MOE_TILE, D_MODEL), lambda i, j, te, nu: (i, 0)),
        scratch_shapes=[pltpu.VMEM((MOE_TILE, D_MODEL), BF16)])
    return pl.pallas_call(
        kern, grid_spec=gs, out_shape=jax.ShapeDtypeStruct((n_rows, D_MODEL), F32),
        compiler_params=_cp("arbitrary", "arbitrary"), name="moe_ffn")(
            tile_expert, n_used, x_sorted, wg, wu, wd)


def _combine_ln_kernel(x_ref, r_ref, ya_ref, yb_ref, g_ref, b_ref, o_ref):
    r = r_ref[...]
    y = r[:, TOP_K:TOP_K + 1] * ya_ref[...] + r[:, TOP_K + 1:TOP_K + 2] * yb_ref[...]
    o_ref[...] = _layer_norm(ALPHA * x_ref[...] + y, g_ref[...], b_ref[...])


def combine_ln(x, r, ya, yb, g, b, tm):
    m = x.shape[0]
    row = pl.BlockSpec((tm, D_MODEL), lambda i: (i, 0))
    vec = pl.BlockSpec((1, D_MODEL), lambda i: (0, 0))
    return pl.pallas_call(
        _combine_ln_kernel, grid=(m // tm,),
        in_specs=[row, pl.BlockSpec((tm, LANES), lambda i: (i, 0)), row, row, vec, vec], out_specs=row,
        out_shape=jax.ShapeDtypeStruct((m, D_MODEL), F32),
        compiler_params=_cp("arbitrary"), name="combine_ln")(x, r, ya, yb, g, b)


def moe_ln(x, router_w, router_b, wg, wu, wd, g, b, tm):
    m = x.shape[0]
    w_pad = jnp.zeros((D_MODEL, LANES), F32).at[:, :N_EXPERTS].set(router_w)
    b_pad = jnp.zeros((1, LANES), F32).at[0, :N_EXPERTS].set(router_b)
    r = router(x, w_pad, b_pad, tm)
    top_idx = r[:, :TOP_K].astype(jnp.int32)
    flat_e = top_idx.reshape(-1)
    n_slot = m * TOP_K
    onehot = (flat_e[:, None] == jnp.arange(N_EXPERTS, dtype=jnp.int32)[None, :]).astype(jnp.int32)
    rank = jnp.take_along_axis(jnp.cumsum(onehot, axis=0) - onehot, flat_e[:, None], axis=1)[:, 0]
    counts = jnp.sum(onehot, axis=0)
    tiles_per = (counts + MOE_TILE - 1) // MOE_TILE
    tile_end = jnp.cumsum(tiles_per)
    dest = (tile_end - tiles_per)[flat_e] * MOE_TILE + rank
    n_tiles = -(-n_slot // MOE_TILE) + N_EXPERTS
    n_rows = n_tiles * MOE_TILE
    row_tok = (jnp.arange(n_rows, dtype=jnp.int32) % m).at[dest].set(jnp.arange(n_slot, dtype=jnp.int32) // TOP_K)
    n_used = tile_end[-1:].astype(jnp.int32)
    tile_ids = jnp.minimum(jnp.arange(n_tiles, dtype=jnp.int32), n_used[0] - 1)
    tile_expert = jnp.minimum(jnp.searchsorted(tile_end, tile_ids, side='right'), N_EXPERTS - 1).astype(jnp.int32)
    x_sorted = x[row_tok]
    y_sorted = moe_ffn(x_sorted, tile_expert, n_used, wg, wu, wd, tf=512)
    dest2 = dest.reshape(m, TOP_K)
    return combine_ln(x, r, y_sorted[dest2[:, 0]], y_sorted[dest2[:, 1]], g, b, tm)


def kernel(x_prompt, x_sample, cache_sb_k, cache_sb_v, cache_mem_k, cache_mem_v, state_ml_C, state_ml_n, state_ml_m, state_rt_S, state_s5_re, state_s5_im, page_table, mem_prompt, w_in, sb_bias, ml_b_i, ml_b_f, ml_norm_g, rt_norm_g, s5_A_re, s5_A_im, s5_log_dt, s5_B_re, s5_B_im, s5_C_re, s5_C_im, s5_D, s5_glu_w, s5_glu_b, w_out, ca_wq, ca_wk, ca_wv, ca_wo, ln_g, ln_b, ffn_w_gate, ffn_w_up, ffn_w_down, moe_router_w, moe_router_b, moe_w_gate, moe_w_up, moe_w_down):
    bp, tp, _ = x_prompt.shape
    bs, ts, _ = x_sample.shape
    n_p, n_s = bp * tp, bs * ts
    tm = 640
    assert (n_p + n_s) % (2 * tm) == 0 and n_p % n_s == 0 and tp % 512 == 0 and bp % 8 == 0 and bs % 8 == 0
    x = jnp.concatenate([x_prompt.reshape(n_p, D_MODEL), x_sample.reshape(n_s, D_MODEL)], axis=0)
    uu = _suffix_matrix()
    g_off = 7 * GROUP_W
    half = HEAD_DIM // 2
    freq = ROPE_BASE ** (-jnp.arange(half, dtype=F32) / half)

    def rope_tables(pos):
        ang = pos.astype(F32)[:, None] * freq[None, :]
        cos, sin = jnp.cos(ang), jnp.sin(ang)
        return (jnp.tile(jnp.concatenate([cos, cos], axis=1), (1, N_HEADS)),
                jnp.tile(jnp.concatenate([-sin, sin], axis=1), (1, N_HEADS)))

    cos_p, sin_p = rope_tables(jnp.arange(tp, dtype=jnp.int32))
    cos_s, sin_s = rope_tables(PAST_LEN + jnp.arange(ts, dtype=jnp.int32))
    cache_kt = cache_sb_k.transpose(0, 1, 3, 4, 2)
    cache_vt = cache_sb_v.transpose(0, 1, 3, 4, 2)

    p_st = [[] for _ in range(10)]
    s_st = [[] for _ in range(8)]
    for l in range(DEPTH):
        wl = w_in[l]
        w_cat = jnp.concatenate([wl[:, :g_off], wl[:, g_off + 2 * N_HEADS:], wl[:, g_off:g_off + 2 * N_HEADS],
                                 jnp.zeros((D_MODEL, PROJ_W - wl.shape[1]), F32)], axis=1).astype(BF16)
        proj_p = linear(x, w_cat, 512, 0, n_p)
        proj_s = linear(x, w_cat, n_s, n_p // n_s, n_s)
        proj_p3 = proj_p.reshape(bp, tp, PROJ_W)
        proj_s3 = proj_s.reshape(bs, ts, PROJ_W)
        gate_bias = jnp.zeros((1, LANES), F32).at[0, :2 * N_HEADS].set(jnp.concatenate([ml_b_i[l], ml_b_f[l]]))
        ml_g = ml_norm_g[l][None, :]
        rt_g = rt_norm_g[l][None, :]
        wb, a1, a2, wc = _s5_weights(s5_A_re[l], s5_A_im[l], s5_log_dt[l], s5_B_re[l], s5_B_im[l],
                                     s5_C_re[l], s5_C_im[l])
        s5_d = s5_D[l][None, :]
        glu_w = s5_glu_w[l].astype(BF16)
        glu_b = s5_glu_b[l][None, :]

        o_sb_p = sb_attention_prompt(proj_p, sb_bias[l], uu, bp, tp, tq=256)
        o_ml_p, cn_p, m_p = mlstm_mixer(proj_p3, gate_bias, ml_g, jnp.zeros((bp, N_HEADS, HEAD_DIM, LANES), F32),
                                        jnp.zeros((bp, N_HEADS), F32), bb=8)
        o_rt_p, rs_p = retention_mixer(proj_p3, cos_p, sin_p, rt_g,
                                       jnp.zeros((bp, N_HEADS, HEAD_DIM, HEAD_DIM), F32), bb=8)
        o_ml_p = o_ml_p.reshape(n_p, GROUP_W)
        o_rt_p = o_rt_p.reshape(n_p, GROUP_W)
        u_p = proj_p3[:, :, C_SU * GROUP_W:(C_SU + 1) * GROUP_W].transpose(1, 0, 2)
        o_s5_p, h5_p = s5_mixer(u_p.reshape(n_p, GROUP_W), bp, tp, 64, wb, a1, a2,
                                jnp.zeros((bp, 2 * S5_W), F32), wc, s5_d, glu_w, glu_b)
        o_s5_p = o_s5_p.reshape(tp, bp, GROUP_W).transpose(1, 0, 2).reshape(n_p, GROUP_W)

        o_sb_s = sb_attention_sample(proj_s, 0, cache_kt, cache_vt, page_table, l, sb_bias[l], uu, n_pp=16)
        cn0 = jnp.concatenate([state_ml_C[:, l], state_ml_n[:, l][..., None],
                               jnp.zeros((bs, N_HEADS, HEAD_DIM, LANES - HEAD_DIM - 1), F32)], axis=-1)
        o_ml_s, cn_s, m_s = mlstm_mixer(proj_s3, gate_bias, ml_g, cn0, state_ml_m[:, l], bb=8)
        o_rt_s, rs_s = retention_mixer(proj_s3, cos_s, sin_s, rt_g, state_rt_S[:, l], bb=8)
        o_ml_s = o_ml_s.reshape(n_s, GROUP_W)
        o_rt_s = o_rt_s.reshape(n_s, GROUP_W)
        u_s = proj_s3[:, :, C_SU * GROUP_W:(C_SU + 1) * GROUP_W].transpose(1, 0, 2)
        h0_s = jnp.concatenate([state_s5_re[:, l].reshape(bs, S5_W), state_s5_im[:, l].reshape(bs, S5_W)], axis=1)
        o_s5_s, h5_s = s5_mixer(u_s.reshape(n_s, GROUP_W), bs, ts, ts, wb, a1, a2, h0_s, wc, s5_d, glu_w, glu_b)
        o_s5_s = o_s5_s.reshape(ts, bs, GROUP_W).transpose(1, 0, 2).reshape(n_s, GROUP_W)

        mixed = jnp.concatenate([jnp.concatenate([o_sb_p, o_ml_p, o_rt_p, o_s5_p], axis=1),
                                 jnp.concatenate([o_sb_s, o_ml_s, o_rt_s, o_s5_s], axis=1)], axis=0)
        x = linear_res_ln(mixed, w_out[l].astype(BF16), x, ln_g[l, 0][None, :], ln_b[l, 0][None, :], tm)

        mem_kv = linear(mem_prompt.reshape(bp * N_MEM, D_MODEL),
                        jnp.concatenate([ca_wk[l], ca_wv[l]], axis=1).astype(BF16), 512)
        mk_p = mem_kv[:, :GROUP_W].reshape(bp, N_MEM, GROUP_W)
        mv_p = mem_kv[:, GROUP_W:].reshape(bp, N_MEM, GROUP_W)
        wq = ca_wq[l].astype(BF16)
        wo = ca_wo[l].astype(BF16)
        g1, b1 = ln_g[l, 1][None, :], ln_b[l, 1][None, :]
        x = cross_attn_ln(x, 0, bp, tp, 512, mk_p, mv_p, wq, wo, g1, b1)
        x = cross_attn_ln(x, n_p // ts, bs, ts, ts, cache_mem_k[:, l].reshape(bs, N_MEM, GROUP_W),
                          cache_mem_v[:, l].reshape(bs, N_MEM, GROUP_W), wq, wo, g1, b1)

        g2, b2 = ln_g[l, 2][None, :], ln_b[l, 2][None, :]
        j = l // 2
        if l % 2 == 0:
            x = ffn_ln(x, ffn_w_gate[j], ffn_w_up[j], ffn_w_down[j], g2, b2, 2 * tm, tf=256)
        else:
            x = moe_ln(x, moe_router_w[j], moe_router_b[j], moe_w_gate[j], moe_w_up[j], moe_w_down[j], g2, b2, tm)

        def heads(a, nb_, t_):
            return a.reshape(nb_, t_, N_HEADS, HEAD_DIM)

        p_st[0].append(heads(proj_p[:, C_SK * GROUP_W:(C_SK + 1) * GROUP_W], bp, tp))
        p_st[1].append(heads(proj_p[:, C_SV * GROUP_W:(C_SV + 1) * GROUP_W], bp, tp))
        p_st[2].append(heads(mk_p, bp, N_MEM))
        p_st[3].append(heads(mv_p, bp, N_MEM))
        s_st[0].append(heads(proj_s[:, C_SK * GROUP_W:(C_SK + 1) * GROUP_W], bs, ts))
        s_st[1].append(heads(proj_s[:, C_SV * GROUP_W:(C_SV + 1) * GROUP_W], bs, ts))
        for st, cn, mm, rs, h5, nb_ in ((p_st, cn_p, m_p, rs_p, h5_p, bp), (s_st, cn_s, m_s, rs_s, h5_s, bs)):
            off = 4 if st is p_st else 2
            st[off + 0].append(cn[..., :HEAD_DIM])
            st[off + 1].append(cn[..., HEAD_DIM])
            st[off + 2].append(mm[:, 0, :N_HEADS])
            st[off + 3].append(rs)
            st[off + 4].append(h5[:, :S5_W].reshape(nb_, S5_GROUPS, S5_STATE))
            st[off + 5].append(h5[:, S5_W:].reshape(nb_, S5_GROUPS, S5_STATE))

    y_prompt = x[:n_p].reshape(bp, tp, D_MODEL)
    y_sample = x[n_p:].reshape(bs, ts, D_MODEL)
    p_out = [jnp.stack(a, axis=1) for a in p_st]
    s_out = [jnp.stack(a, axis=1) for a in s_st]
    return (y_prompt, y_sample, *p_out, *s_out)
```

```python
import functools
import math

import numpy as np
import jax
import jax.numpy as jnp
from jax import lax
from jax.experimental import pallas as pl
from jax.experimental.pallas import tpu as pltpu

F32 = jnp.float32
BF16 = jnp.bfloat16

D_MODEL = 1024
DEPTH = 2
PAST_LEN = 8192
PAGE_SIZE = 128
HEAD_DIM = 64
N_HEADS = 4
GROUP_W = N_HEADS * HEAD_DIM
S5_GROUPS = 16
S5_GROUP = 16
S5_STATE = 64
S5_W = S5_GROUPS * S5_STATE
N_MEM = 256
D_FF = 3584
N_EXPERTS = 8
TOP_K = 2
CHUNK = 64
ROPE_BASE = 10000.0
LN_EPS = 1e-5
GN_EPS = 1e-6
ALPHA = (2 * DEPTH) ** 0.25
QK_SCALE = HEAD_DIM ** -0.5
LOG2E = math.log2(math.e)

LANES = 128
PROJ_W = 25 * LANES
C_SQ, C_SK, C_SV, C_MQ, C_MK, C_MV, C_MO, C_RQ, C_RK, C_RV, C_RG, C_SU = range(12)
C_GATES = 12 * GROUP_W // LANES
VMEM_LIMIT = 48 * 1024 * 1024
MOE_TILE = 1024


def _cp(*sem):
    return pltpu.CompilerParams(dimension_semantics=sem, vmem_limit_bytes=VMEM_LIMIT)


def _dot(a, b):
    return jnp.dot(a, b, preferred_element_type=F32)


def _dot_nt(a, b):
    return lax.dot_general(a, b, (((1,), (1,)), ((), ())), preferred_element_type=F32)


def _dot_tn(a, b):
    return lax.dot_general(a, b, (((0,), (0,)), ((), ())), preferred_element_type=F32)


def _layer_norm(y, g, b):
    mu = jnp.mean(y, axis=-1, keepdims=True)
    yc = y - mu
    var = jnp.mean(yc * yc, axis=-1, keepdims=True)
    return yc * lax.rsqrt(var + LN_EPS) * g + b


def _row_sum(x, scale=1.0):
    ones = jnp.full((x.shape[1], LANES), scale, BF16)
    hi = x.astype(BF16)
    lo = (x - hi.astype(F32)).astype(BF16)
    return (_dot(hi, ones) + _dot(lo, ones))[:, :x.shape[1]]


def _cumsum_rows(tril, x):
    hi = x.astype(BF16)
    lo = (x - hi.astype(F32)).astype(BF16)
    return _dot(tril, hi) + _dot(tril, lo)


def _head_norm_all(hd):
    inv = 1.0 / HEAD_DIM
    mu = {p: _row_sum(x, inv) for p, x in hd.items()}
    hc = {p: hd[p] - mu[p] for p in hd}
    var = {p: _row_sum(hc[p] * hc[p], inv) for p in hd}
    return {p: hc[p] * lax.rsqrt(var[p] + GN_EPS) for p in hd}


def _neg_softplus(z):
    return -(jnp.maximum(z, 0.0) + jnp.log1p(jnp.exp(-jnp.abs(z))))


def _log_sigmoid(z):
    return _neg_softplus(-z)


def _linear_kernel(x_ref, w_ref, o_ref):
    o_ref[...] = _dot(x_ref[...].astype(BF16), w_ref[...]).astype(o_ref.dtype)


def linear(x, w, tm, row_blk0=0, n_rows=None, out_dtype=F32):
    m, k = x.shape
    m = m if n_rows is None else n_rows
    n = w.shape[1]
    return pl.pallas_call(
        _linear_kernel, grid=(m // tm,),
        in_specs=[pl.BlockSpec((tm, k), lambda i: (row_blk0 + i, 0)), pl.BlockSpec((k, n), lambda i: (0, 0))],
        out_specs=pl.BlockSpec((tm, n), lambda i: (i, 0)),
        out_shape=jax.ShapeDtypeStruct((m, n), out_dtype),
        compiler_params=_cp("arbitrary"), name="linear")(x, w)


def _mix_out_ln_kernel(x_ref, a_ref, b_ref, c_ref, d_ref, w_ref, g_ref, bias_ref, o_ref):
    h = sum(_dot(p[...].astype(BF16), w_ref[pl.ds(n * GROUP_W, GROUP_W), :])
            for n, p in enumerate((a_ref, b_ref, c_ref, d_ref)))
    o_ref[...] = _layer_norm(ALPHA * x_ref[...] + h, g_ref[...], bias_ref[...])


def mix_out_ln(x, parts, w, g, b, tm, row_blk0):
    rows = parts[0].shape[0]
    row_spec = pl.BlockSpec((tm, D_MODEL), lambda i: (row_blk0 + i, 0))
    part_spec = pl.BlockSpec((tm, GROUP_W), lambda i: (i, 0))
    vec = pl.BlockSpec((1, D_MODEL), lambda i: (0, 0))
    return pl.pallas_call(
        _mix_out_ln_kernel, grid=(rows // tm,),
        in_specs=[row_spec, part_spec, part_spec, part_spec, part_spec,
                  pl.BlockSpec((D_MODEL, D_MODEL), lambda i: (0, 0)), vec, vec],
        out_specs=row_spec, out_shape=jax.ShapeDtypeStruct(x.shape, F32), input_output_aliases={0: 0},
        compiler_params=_cp("arbitrary"), name="mix_out_ln")(x, *parts, w, g, b)


def _suffix_matrix():
    j = np.arange(LANES)
    u = (j[:, None] >= j[None, :]).astype(np.float32)
    uu = np.concatenate([u, np.ones((LANES, LANES), np.float32)], axis=1)
    return jnp.asarray(np.concatenate([uu, uu], axis=0), dtype=BF16)


def _suffix_sums(lr, uu):
    hi = lr.astype(BF16)
    lo = (lr - hi.astype(F32)).astype(BF16)
    r = _dot(jnp.concatenate([hi, lo], axis=1), uu)
    return r[:, :LANES], r[:, LANES:]


def _log2_rem(z2):
    return jnp.minimum(-z2, 0.0) - jnp.log2(1.0 + jnp.exp2(-jnp.abs(z2)))


def _sb_prompt_kernel(bias_ref, q_ref, k_ref, v_ref, uu_ref, o_ref, acc_ref, car_ref, *, tq):
    i = pl.program_id(1)
    tk = LANES
    nsub = tq // tk
    acc_ref[...] = jnp.zeros_like(acc_ref)
    car_ref[...] = jnp.zeros_like(car_ref)
    q = (q_ref[...] * (QK_SCALE * LOG2E)).astype(BF16)
    qh = [q[:, h * HEAD_DIM:(h + 1) * HEAD_DIM] for h in range(N_HEADS)]
    b2 = [bias_ref[h] * LOG2E for h in range(N_HEADS)]
    uu = uu_ref[...]
    row = lax.broadcasted_iota(jnp.int32, (tq, tk), 0)
    col = lax.broadcasted_iota(jnp.int32, (tq, tk), 1)
    heads = range(N_HEADS)

    def block_pair(j_hi, causal_hi, causal_lo):
        kb, vb, z2 = [], [], []
        for d in range(2):
            r0 = pl.multiple_of((j_hi - d) * tk, tk)
            kb.append(k_ref[pl.ds(r0, tk), :].astype(BF16))
            vb.append(v_ref[pl.ds(r0, tk), :].astype(BF16))
            z2.append([_dot_nt(qh[h], kb[d][:, h * HEAD_DIM:(h + 1) * HEAD_DIM]) + b2[h] for h in heads])
        cs, tot = [], []
        for d, causal in enumerate((causal_hi, causal_lo)):
            lr = [_log2_rem(z) for z in z2[d]]
            if causal is not None:
                lr = [jnp.where(causal, a, 0.0) for a in lr]
            c, t = _suffix_sums(jnp.concatenate(lr, axis=0), uu)
            cs.append(c)
            tot.append(t)
        car = [car_ref[h] for h in heads]
        pv = []
        for d, causal in enumerate((causal_hi, causal_lo)):
            w = [jnp.exp2(z2[d][h] + cs[d][h * tq:(h + 1) * tq] + car[h]) for h in heads]
            if causal is not None:
                w = [jnp.where(causal, a, 0.0) for a in w]
            pv.append([_dot(w[h].astype(BF16), vb[d][:, h * HEAD_DIM:(h + 1) * HEAD_DIM]) for h in heads])
            car = [car[h] + tot[d][h * tq:(h + 1) * tq] for h in heads]
        for h in heads:
            acc_ref[h] += pv[0][h] + pv[1][h]
            car_ref[h] = car[h]

    assert nsub == 2
    block_pair(i * nsub + 1, (col + tk) < row, col < row)

    def body(jj, carry):
        block_pair(i * nsub - 1 - 2 * jj, None, None)
        return carry

    lax.fori_loop(0, i, body, 0)
    o_ref[...] = jnp.concatenate([acc_ref[h] for h in range(N_HEADS)], axis=1)


def sb_attention_prompt(proj, sb_bias, uu, n_batch, seq, tq):
    nq = seq // tq
    kern = functools.partial(_sb_prompt_kernel, tq=tq)
    return pl.pallas_call(
        kern, grid=(n_batch, nq),
        in_specs=[pl.BlockSpec(memory_space=pltpu.SMEM),
                  pl.BlockSpec((tq, GROUP_W), lambda b, i: (b * nq + i, C_SQ)),
                  pl.BlockSpec((seq, GROUP_W), lambda b, i: (b, C_SK)),
                  pl.BlockSpec((seq, GROUP_W), lambda b, i: (b, C_SV)),
                  pl.BlockSpec((2 * LANES, 2 * LANES), lambda b, i: (0, 0))],
        out_specs=pl.BlockSpec((tq, GROUP_W), lambda b, i: (b * nq + i, 0)),
        out_shape=jax.ShapeDtypeStruct((n_batch * seq, GROUP_W), F32),
        scratch_shapes=[pltpu.VMEM((N_HEADS, tq, HEAD_DIM), F32), pltpu.VMEM((N_HEADS, tq, LANES), F32)],
        compiler_params=_cp("arbitrary", "arbitrary"), name="sb_prompt")(sb_bias, proj, proj, proj, uu)


def _sb_sample_kernel(pt_ref, bias_ref, q_ref, kn_ref, vn_ref, u8_ref, uu_ref, *rest, n_pp, n_steps):
    k_refs = rest[:n_pp]
    v_refs = rest[n_pp:2 * n_pp]
    o_ref = rest[2 * n_pp]
    acc_ref, car_ref = rest[2 * n_pp + 1:]
    s = pl.program_id(1)
    nq = q_ref.shape[0]
    q = (q_ref[...] * (QK_SCALE * LOG2E)).astype(BF16)
    qh = [q[:, h * HEAD_DIM:(h + 1) * HEAD_DIM] for h in range(N_HEADS)]
    b2 = [bias_ref[h] * LOG2E for h in range(N_HEADS)]

    @pl.when(s == 0)
    def _():
        kn = kn_ref[...].astype(BF16)
        vn = vn_ref[...].astype(BF16)
        t = lax.broadcasted_iota(jnp.int32, (nq, nq), 0)
        c = lax.broadcasted_iota(jnp.int32, (nq, nq), 1)
        causal = c < t
        for h in range(N_HEADS):
            sl = slice(h * HEAD_DIM, (h + 1) * HEAD_DIM)
            z2 = _dot_nt(qh[h], kn[:, sl]) + b2[h]
            lr = jnp.where(causal, _log2_rem(z2), 0.0)
            cs = jnp.dot(lr, u8_ref[...], preferred_element_type=F32, precision=lax.Precision.HIGHEST)
            w = jnp.where(causal, jnp.exp2(z2 + cs), 0.0)
            acc_ref[h] = _dot(w.astype(BF16), vn[:, sl])
            car_ref[h] = jnp.broadcast_to(jnp.sum(lr, axis=1, keepdims=True), (nq, LANES))

    uu = uu_ref[...]
    z2s = []
    for p in range(n_pp):
        for h in range(N_HEADS):
            z2s.append(_dot(qh[h], k_refs[p][0, 0, h].astype(BF16)) + b2[h])
    cs_all, tot_all = _suffix_sums(_log2_rem(jnp.concatenate(z2s, axis=0)), uu)
    for h in range(N_HEADS):
        car = car_ref[h]
        acc = acc_ref[h]
        for p in range(n_pp):
            r = (p * N_HEADS + h) * nq
            w = jnp.exp2(z2s[p * N_HEADS + h] + cs_all[r:r + nq] + car)
            acc = acc + _dot_nt(w.astype(BF16), v_refs[p][0, 0, h].astype(BF16))
            car = car + tot_all[r:r + nq]
        car_ref[h] = car
        acc_ref[h] = acc

    @pl.when(s == n_steps - 1)
    def _():
        o_ref[...] = jnp.concatenate([acc_ref[h] for h in range(N_HEADS)], axis=1)


def sb_attention_sample(proj, row_blk0, cache_kt, cache_vt, page_table, layer, sb_bias, uu, n_pp):
    n_batch, n_pages = page_table.shape
    nq = 8
    n_steps = n_pages // n_pp
    u8 = jnp.asarray((np.arange(nq)[:, None] >= np.arange(nq)[None, :]).astype(np.float32))

    def page_spec(p):
        return pl.BlockSpec((1, 1, N_HEADS, HEAD_DIM, PAGE_SIZE),
                            lambda b, s, pt: (pt[b, n_pages - 1 - (s * n_pp + p)], layer, 0, 0, 0))

    def row_spec(cblk):
        return pl.BlockSpec((nq, GROUP_W), lambda b, s, pt: (row_blk0 + b, cblk))

    kern = functools.partial(_sb_sample_kernel, n_pp=n_pp, n_steps=n_steps)
    gs = pltpu.PrefetchScalarGridSpec(
        num_scalar_prefetch=1, grid=(n_batch, n_steps),
        in_specs=[pl.BlockSpec(memory_space=pltpu.SMEM), row_spec(C_SQ), row_spec(C_SK), row_spec(C_SV),
                  pl.BlockSpec((nq, nq), lambda b, s, pt: (0, 0)),
                  pl.BlockSpec((2 * LANES, 2 * LANES), lambda b, s, pt: (0, 0))]
                 + [page_spec(p) for p in range(n_pp)] * 2,
        out_specs=pl.BlockSpec((nq, GROUP_W), lambda b, s, pt: (b, 0)),
        scratch_shapes=[pltpu.VMEM((N_HEADS, nq, HEAD_DIM), F32), pltpu.VMEM((N_HEADS, nq, LANES), F32)])
    return pl.pallas_call(
        kern, grid_spec=gs, out_shape=jax.ShapeDtypeStruct((n_batch * nq, GROUP_W), F32),
        compiler_params=_cp("arbitrary", "arbitrary"), name="sb_sample")(
            page_table, sb_bias, proj, proj, proj, u8, uu, *([cache_kt] * n_pp), *([cache_vt] * n_pp))


def _mlstm_kernel(m0_ref, q_ref, k_ref, v_ref, og_ref, gt_ref, gb_ref, ng_ref, tril_ref, cn0_ref,
                  o_ref, cn_ref, m_ref, ms_ref, *, bb, chunk, n_t):
    bi = pl.program_id(0)
    t = pl.program_id(1)
    seqs = range(bb)
    pairs = [(b, h) for b in seqs for h in range(N_HEADS)]

    @pl.when(t == 0)
    def _():
        cn_ref[...] = cn0_ref[...]
        for b, h in pairs:
            ms_ref[b * N_HEADS + h] = jnp.full((1, LANES), m0_ref[bi * bb + b, h], F32)

    tril = tril_ref[...]
    tri_mask = lax.broadcasted_iota(jnp.int32, (chunk, chunk), 1) <= lax.broadcasted_iota(jnp.int32, (chunk, chunk), 0)
    lane = lax.broadcasted_iota(jnp.int32, (chunk, HEAD_DIM), 1)
    ones_col = jnp.where(lane == 0, 1.0, 0.0).astype(F32)
    hs = lambda h: slice(h * HEAD_DIM, (h + 1) * HEAD_DIM)

    gt = [gt_ref[b] + gb_ref[...] for b in seqs]
    bc = [_cumsum_rows(tril, _log_sigmoid(g)) for g in gt]
    gt_t = [g.T for g in gt]
    bc_t = [x.T for x in bc]
    q = [q_ref[b].astype(BF16) for b in seqs]
    k = [(k_ref[b] * QK_SCALE).astype(BF16) for b in seqs]
    v = [v_ref[b] for b in seqs]
    m_prev = {p: ms_ref[p[0] * N_HEADS + p[1]][:, :1] for p in pairs}
    ig_col = {(b, h): gt[b][:, h:h + 1] for b, h in pairs}
    bc_col = {(b, h): bc[b][:, N_HEADS + h:N_HEADS + h + 1] for b, h in pairs}
    dm = {(b, h): jnp.where(tri_mask, bc_col[b, h] - (bc_t[b][N_HEADS + h:N_HEADS + h + 1, :] - gt_t[b][h:h + 1, :]),
                            -jnp.inf) for b, h in pairs}
    a = {p: bc_col[p] + m_prev[p] for p in pairs}
    m_new = {p: jnp.maximum(a[p], jnp.max(dm[p], axis=1, keepdims=True)) for p in pairs}
    inter = {p: jnp.exp(a[p] - m_new[p]) for p in pairs}
    s = {(b, h): _dot_nt(q[b][:, hs(h)], k[b][:, hs(h)]) * jnp.exp(dm[b, h] - m_new[b, h]) for b, h in pairs}
    v_ext = {(b, h): jnp.concatenate([v[b][:, hs(h)], ones_col], axis=1) for b, h in pairs}
    qc = {(b, h): _dot(q[b][:, hs(h)], cn_ref[b, h].astype(BF16)) for b, h in pairs}
    sv = {p: _dot(s[p].astype(BF16), v_ext[p].astype(BF16)) for p in pairs}
    m_last = {p: m_new[p][chunk - 1:chunk, :] for p in pairs}
    wl = {p: jnp.exp(bc_col[p][chunk - 1:chunk, :] - bc_col[p] + ig_col[p] - m_last[p]) for p in pairs}
    dl = {p: jnp.exp(a[p][chunk - 1:chunk, :] - m_last[p]) for p in pairs}
    upd = {(b, h): _dot_tn(k[b][:, hs(h)], (wl[b, h] * v_ext[b, h]).astype(BF16)) for b, h in pairs}
    rs = {p: _row_sum(s[p])[:, :1] for p in pairs}
    num = {p: inter[p] * qc[p][:, :HEAD_DIM] + sv[p][:, :HEAD_DIM] for p in pairs}
    den = {p: inter[p] * qc[p][:, HEAD_DIM:HEAD_DIM + 1] + rs[p] for p in pairs}
    hh = _head_norm_all({p: num[p] / jnp.maximum(jnp.abs(den[p]), jnp.exp(-m_new[p])) for p in pairs})
    for b, h in pairs:
        cn_ref[b, h] = dl[b, h] * cn_ref[b, h] + upd[b, h]
        ms_ref[b * N_HEADS + h] = jnp.broadcast_to(m_last[b, h], (1, LANES))
    for b in seqs:
        y = jnp.concatenate([hh[b, h] for h in range(N_HEADS)], axis=1)
        o_ref[b] = y * ng_ref[...] * jax.nn.sigmoid(og_ref[b])

    @pl.when(t == n_t - 1)
    def _():
        lane_m = lax.broadcasted_iota(jnp.int32, (1, LANES), 1)
        for b in seqs:
            m_out = jnp.zeros((1, LANES), F32)
            for h in range(N_HEADS):
                m_out = jnp.where(lane_m == h, ms_ref[b * N_HEADS + h], m_out)
            m_ref[b] = m_out


def mlstm_mixer(proj3, gate_bias, norm_g, cn0, m0, bb):
    n_batch, seq, _ = proj3.shape
    chunk = math.gcd(seq, CHUNK)
    n_t = seq // chunk
    tril = jnp.asarray(np.tril(np.ones((chunk, chunk), np.float32)), dtype=BF16)

    def row_spec(cblk, w=GROUP_W):
        return pl.BlockSpec((bb, chunk, w), lambda bi, t: (bi, t, cblk))

    def const_spec(shape):
        return pl.BlockSpec(shape, lambda bi, t: (0,) * len(shape))

    state_spec = pl.BlockSpec((bb, N_HEADS, HEAD_DIM, LANES), lambda bi, t: (bi, 0, 0, 0))
    kern = functools.partial(_mlstm_kernel, bb=bb, chunk=chunk, n_t=n_t)
    return pl.pallas_call(
        kern, grid=(n_batch // bb, n_t),
        in_specs=[pl.BlockSpec(memory_space=pltpu.SMEM),
                  row_spec(C_MQ), row_spec(C_MK), row_spec(C_MV), row_spec(C_MO), row_spec(C_GATES, LANES),
                  const_spec((1, LANES)), const_spec((1, GROUP_W)), const_spec((chunk, chunk)), state_spec],
        out_specs=[pl.BlockSpec((bb, chunk, GROUP_W), lambda bi, t: (bi, t, 0)), state_spec,
                   pl.BlockSpec((bb, 1, LANES), lambda bi, t: (bi, 0, 0))],
        out_shape=[jax.ShapeDtypeStruct((n_batch, seq, GROUP_W), F32),
                   jax.ShapeDtypeStruct((n_batch, N_HEADS, HEAD_DIM, LANES), F32),
                   jax.ShapeDtypeStruct((n_batch, 1, LANES), F32)],
        scratch_shapes=[pltpu.VMEM((bb * N_HEADS, 1, LANES), F32)],
        compiler_params=_cp("arbitrary", "arbitrary"), name="mlstm")(
            m0, proj3, proj3, proj3, proj3, proj3, gate_bias, norm_g, tril, cn0)


def _rope(x, cos, sin_signed):
    lane = lax.broadcasted_iota(jnp.int32, x.shape, 1)
    half = HEAD_DIM // 2
    swapped = jnp.where((lane % HEAD_DIM) < half, pltpu.roll(x, x.shape[1] - half, 1), pltpu.roll(x, half, 1))
    return x * cos + swapped * sin_signed


def _retention_kernel(q_ref, k_ref, v_ref, gg_ref, cos_ref, sin_ref, ng_ref, dec_ref, int_ref, wl_ref, dl_ref,
                      s0_ref, o_ref, s_ref, *, bb):
    t = pl.program_id(1)
    seqs = range(bb)
    pairs = [(b, h) for b in seqs for h in range(N_HEADS)]
    hs = lambda h: slice(h * HEAD_DIM, (h + 1) * HEAD_DIM)

    @pl.when(t == 0)
    def _():
        s_ref[...] = s0_ref[...]

    cos = cos_ref[...]
    sin = sin_ref[...]
    q = [_rope(q_ref[b], cos, sin).astype(BF16) for b in seqs]
    k = [(_rope(k_ref[b], cos, sin) * QK_SCALE).astype(BF16) for b in seqs]
    v = [v_ref[b] for b in seqs]
    s = {(b, h): _dot_nt(q[b][:, hs(h)], k[b][:, hs(h)]) * dec_ref[h] for b, h in pairs}
    qs = {(b, h): _dot(q[b][:, hs(h)], s_ref[b, h].astype(BF16)) for b, h in pairs}
    sv = {(b, h): _dot(s[b, h].astype(BF16), v[b][:, hs(h)].astype(BF16)) for b, h in pairs}
    upd = {(b, h): _dot_tn(k[b][:, hs(h)], (wl_ref[h] * v[b][:, hs(h)]).astype(BF16)) for b, h in pairs}
    o = _head_norm_all({(b, h): int_ref[h] * qs[b, h] + sv[b, h] for b, h in pairs})
    for b, h in pairs:
        s_ref[b, h] = dl_ref[h] * s_ref[b, h] + upd[b, h]
    for b in seqs:
        gg = gg_ref[b]
        o_ref[b] = jnp.concatenate([o[b, h] for h in range(N_HEADS)], axis=1) * ng_ref[...] * (gg * jax.nn.sigmoid(gg))


def _retention_consts(chunk):
    log_g = np.log(1.0 - np.exp2(-5.0 - np.arange(N_HEADS, dtype=np.float64)))
    tau = np.arange(chunk, dtype=np.float64)
    rel = tau[:, None] - tau[None, :]
    decay = np.where(rel >= 0, np.exp(log_g[:, None, None] * np.maximum(rel, 0.0)), 0.0)
    inter = np.exp(log_g[:, None] * (tau + 1.0))[..., None]
    wl = np.exp(log_g[:, None] * (chunk - 1.0 - tau))[..., None]
    dl = np.exp(log_g * chunk)[:, None, None]
    return tuple(jnp.asarray(a, F32) for a in (decay, inter, wl, dl))


def retention_mixer(proj3, cos, sin_signed, norm_g, s0, bb):
    n_batch, seq, _ = proj3.shape
    chunk = math.gcd(seq, CHUNK)
    dec, inter, wl, dl = _retention_consts(chunk)

    def row_spec(cblk):
        return pl.BlockSpec((bb, chunk, GROUP_W), lambda bi, t: (bi, t, cblk))

    def const_spec(shape):
        return pl.BlockSpec(shape, lambda bi, t: (0,) * len(shape))

    pos_spec = pl.BlockSpec((chunk, GROUP_W), lambda bi, t: (t, 0))
    state_spec = pl.BlockSpec((bb, N_HEADS, HEAD_DIM, HEAD_DIM), lambda bi, t: (bi, 0, 0, 0))
    kern = functools.partial(_retention_kernel, bb=bb)
    return pl.pallas_call(
        kern, grid=(n_batch // bb, seq // chunk),
        in_specs=[row_spec(C_RQ), row_spec(C_RK), row_spec(C_RV), row_spec(C_RG), pos_spec, pos_spec,
                  const_spec((1, GROUP_W)), const_spec(dec.shape), const_spec(inter.shape), const_spec(wl.shape),
                  const_spec(dl.shape), state_spec],
        out_specs=[pl.BlockSpec((bb, chunk, GROUP_W), lambda bi, t: (bi, t, 0)), state_spec],
        out_shape=[jax.ShapeDtypeStruct((n_batch, seq, GROUP_W), F32),
                   jax.ShapeDtypeStruct((n_batch, N_HEADS, HEAD_DIM, HEAD_DIM), F32)],
        compiler_params=_cp("arbitrary", "arbitrary"), name="retention")(
            proj3, proj3, proj3, proj3, cos, sin_signed, norm_g, dec, inter, wl, dl, s0)


def _s5_kernel(u_ref, wb_ref, a1_ref, a2_ref, h0_ref, wc_ref, d_ref, gw_ref, gb_ref, o_ref, hl_ref,
               hs_ref, ut_ref, yt_ref, *, nb, tt):
    c = pl.program_id(0)

    @pl.when(c == 0)
    def _():
        hl_ref[...] = h0_ref[...]

    halves = range(GROUP_W // LANES)
    for b in range(nb):
        ub = u_ref[b]
        for hf in halves:
            ut_ref[hf, pl.ds(b, tt, stride=nb), :] = ub[:, hf * LANES:(hf + 1) * LANES]
    u = jnp.concatenate([ut_ref[hf] for hf in halves], axis=1)
    hs_ref[...] = _dot(u.astype(BF16), wb_ref[...])
    a1 = jnp.broadcast_to(a1_ref[...], (nb, 2 * S5_W))
    a2 = jnp.broadcast_to(a2_ref[...], (nb, 2 * S5_W))

    def step(t, h):
        r0 = pl.multiple_of(t * nb, nb)
        swapped = jnp.concatenate([h[:, S5_W:], h[:, :S5_W]], axis=1)
        h = a1 * h + a2 * swapped + hs_ref[pl.ds(r0, nb), :]
        hs_ref[pl.ds(r0, nb), :] = h
        return h

    hl_ref[...] = lax.fori_loop(0, tt, step, hl_ref[...])
    y = _dot(hs_ref[...].astype(BF16), wc_ref[...]) + d_ref[...] * u
    g5 = jax.nn.gelu(y)
    yt = g5 * jax.nn.sigmoid(_dot(g5.astype(BF16), gw_ref[...]) + gb_ref[...])
    for hf in halves:
        yt_ref[hf] = yt[:, hf * LANES:(hf + 1) * LANES]
    for b in range(nb):
        o_ref[b] = jnp.concatenate([yt_ref[hf, pl.ds(b, tt, stride=nb), :] for hf in halves], axis=1)


def s5_mixer(proj3, tt, wb, a1, a2, h0, wc, d, glu_w, glu_b):
    nb, seq, _ = proj3.shape
    rows = tt * nb

    def const_spec(shape):
        return pl.BlockSpec(shape, lambda c: (0,) * len(shape))

    kern = functools.partial(_s5_kernel, nb=nb, tt=tt)
    return pl.pallas_call(
        kern, grid=(seq // tt,),
        in_specs=[pl.BlockSpec((nb, tt, GROUP_W), lambda c: (0, c, C_SU)),
                  const_spec(wb.shape), const_spec(a1.shape), const_spec(a2.shape), const_spec(h0.shape),
                  const_spec(wc.shape), const_spec(d.shape), const_spec(glu_w.shape), const_spec(glu_b.shape)],
        out_specs=[pl.BlockSpec((nb, tt, GROUP_W), lambda c: (0, c, 0)), const_spec(h0.shape)],
        out_shape=[jax.ShapeDtypeStruct((nb, seq, GROUP_W), F32), jax.ShapeDtypeStruct(h0.shape, F32)],
        scratch_shapes=[pltpu.VMEM((rows, 2 * S5_W), F32), pltpu.VMEM((GROUP_W // LANES, rows, LANES), F32),
                        pltpu.VMEM((GROUP_W // LANES, rows, LANES), F32)],
        compiler_params=_cp("arbitrary"), name="s5")(proj3, wb, a1, a2, h0, wc, d, glu_w, glu_b)


def _s5_weights(a_re, a_im, log_dt, b_re, b_im, c_re, c_im):
    lam = lax.complex(a_re, a_im)
    a_bar = jnp.exp(lam * jnp.exp(log_dt))
    b_bar = ((a_bar - 1.0) / lam)[..., None] * lax.complex(b_re, b_im)
    eye = jnp.eye(S5_GROUPS, dtype=F32)

    def in_map(m):
        return jnp.einsum('gpc,gh->gchp', m, eye).reshape(S5_GROUPS * S5_GROUP, S5_W)

    def out_map(m):
        return jnp.einsum('gcp,gh->gphc', m, eye).reshape(S5_W, S5_GROUPS * S5_GROUP)

    wb = jnp.concatenate([in_map(b_bar.real), in_map(b_bar.imag)], axis=1).astype(BF16)
    wc = jnp.concatenate([out_map(c_re), -out_map(c_im)], axis=0).astype(BF16)
    ar = a_bar.real.reshape(1, S5_W)
    ai = a_bar.imag.reshape(1, S5_W)
    return wb, jnp.concatenate([ar, ar], axis=1), jnp.concatenate([-ai, ai], axis=1), wc


def _cross_attn_kernel(x_ref, wq_ref, k_ref, v_ref, wo_ref, g_ref, b_ref, o_ref):
    x = x_ref[...]
    q = (_dot(x.astype(BF16), wq_ref[...]) * QK_SCALE).astype(BF16)
    k = k_ref[0].astype(BF16)
    v = v_ref[0].astype(BF16)
    hs = [slice(h * HEAD_DIM, (h + 1) * HEAD_DIM) for h in range(N_HEADS)]
    s = [_dot_nt(q[:, sl], k[:, sl]) for sl in hs]
    e = [jnp.exp(a - jnp.max(a, axis=1, keepdims=True)) for a in s]
    p = [a / jnp.sum(a, axis=1, keepdims=True) for a in e]
    o = jnp.concatenate([_dot(p[h].astype(BF16), v[:, hs[h]]) for h in range(N_HEADS)], axis=1)
    y = ALPHA * x + _dot(o.astype(BF16), wo_ref[...])
    o_ref[...] = _layer_norm(y, g_ref[...], b_ref[...])


def cross_attn_ln(x, row_blk0, n_batch, seq, tq, mem_k, mem_v, wq, wo, g, b):
    nq = seq // tq

    def const_spec(shape):
        return pl.BlockSpec(shape, lambda bb, i: (0,) * len(shape))

    row_spec = pl.BlockSpec((tq, D_MODEL), lambda bb, i: (row_blk0 + bb * nq + i, 0))
    mem_spec = pl.BlockSpec((1, N_MEM, GROUP_W), lambda bb, i: (bb, 0, 0))
    return pl.pallas_call(
        _cross_attn_kernel, grid=(n_batch, nq),
        in_specs=[row_spec, const_spec(wq.shape), mem_spec, mem_spec, const_spec(wo.shape),
                  const_spec(g.shape), const_spec(b.shape)],
        out_specs=row_spec, out_shape=jax.ShapeDtypeStruct(x.shape, F32), input_output_aliases={0: 0},
        compiler_params=_cp("arbitrary", "arbitrary"), name="cross_attn")(x, wq, mem_k, mem_v, wo, g, b)


SWIGLU_ROWS = 256


def _swiglu_accumulate(xb_ref, wg, wu, wd, acc_ref):
    wgb, wub, wdb = wg.astype(BF16), wu.astype(BF16), wd.astype(BF16)
    n_sub = xb_ref.shape[0] // SWIGLU_ROWS

    def hidden(r):
        xb = xb_ref[pl.ds(r * SWIGLU_ROWS, SWIGLU_ROWS), :]
        gate = _dot(xb, wgb)
        up = _dot(xb, wub)
        return (gate * jax.nn.sigmoid(gate) * up).astype(BF16)

    hid = hidden(0)
    for r in range(n_sub):
        nxt = hidden(r + 1) if r + 1 < n_sub else None
        acc_ref[pl.ds(r * SWIGLU_ROWS, SWIGLU_ROWS), :] += _dot(hid, wdb)
        hid = nxt


def _ffn_kernel(x_ref, wg_ref, wu_ref, wd_ref, g_ref, b_ref, o_ref, xb_ref, *, nf):
    j = pl.program_id(1)

    @pl.when(j == 0)
    def _():
        xb_ref[...] = x_ref[...].astype(BF16)
        o_ref[...] = jnp.zeros_like(o_ref)

    _swiglu_accumulate(xb_ref, wg_ref[...], wu_ref[...], wd_ref[...], o_ref)

    @pl.when(j == nf - 1)
    def _():
        o_ref[...] = _layer_norm(ALPHA * x_ref[...] + o_ref[...], g_ref[...], b_ref[...])


def ffn_ln(x, wg, wu, wd, g, b, tm, tf):
    m = x.shape[0]
    nf = D_FF // tf
    kern = functools.partial(_ffn_kernel, nf=nf)
    return pl.pallas_call(
        kern, grid=(m // tm, nf),
        in_specs=[pl.BlockSpec((tm, D_MODEL), lambda i, j: (i, 0)),
                  pl.BlockSpec((D_MODEL, tf), lambda i, j: (0, j)), pl.BlockSpec((D_MODEL, tf), lambda i, j: (0, j)),
                  pl.BlockSpec((tf, D_MODEL), lambda i, j: (j, 0)),
                  pl.BlockSpec((1, D_MODEL), lambda i, j: (0, 0)), pl.BlockSpec((1, D_MODEL), lambda i, j: (0, 0))],
        out_specs=pl.BlockSpec((tm, D_MODEL), lambda i, j: (i, 0)),
        out_shape=jax.ShapeDtypeStruct((m, D_MODEL), F32),
        scratch_shapes=[pltpu.VMEM((tm, D_MODEL), BF16)],
        compiler_params=_cp("arbitrary", "arbitrary"), name="ffn")(x, wg, wu, wd, g, b)


def _router_kernel(x_ref, w_ref, b_ref, o_ref):
    logits = jnp.dot(x_ref[...], w_ref[...], preferred_element_type=F32, precision=lax.Precision.HIGHEST) + b_ref[...]
    lane = lax.broadcasted_iota(jnp.int32, logits.shape, 1)
    neg = jnp.float32(-jnp.inf)
    lg = jnp.where(lane < N_EXPERTS, logits, neg)
    m1 = jnp.max(lg, axis=1, keepdims=True)
    i1 = jnp.min(jnp.where(lg == m1, lane, LANES), axis=1, keepdims=True)
    lg2 = jnp.where(lane == i1, neg, lg)
    m2 = jnp.max(lg2, axis=1, keepdims=True)
    i2 = jnp.min(jnp.where(lg2 == m2, lane, LANES), axis=1, keepdims=True)
    e2 = jnp.exp(m2 - m1)
    g1 = 1.0 / (1.0 + e2)
    g2 = e2 / (1.0 + e2)
    out = jnp.where(lane == 0, i1.astype(F32), jnp.where(lane == 1, i2.astype(F32),
                    jnp.where(lane == 2, g1, jnp.where(lane == 3, g2, 0.0))))
    o_ref[...] = out


def router(x, w_pad, b_pad, tm):
    m = x.shape[0]
    return pl.pallas_call(
        _router_kernel, grid=(m // tm,),
        in_specs=[pl.BlockSpec((tm, D_MODEL), lambda i: (i, 0)), pl.BlockSpec((D_MODEL, LANES), lambda i: (0, 0)),
                  pl.BlockSpec((1, LANES), lambda i: (0, 0))],
        out_specs=pl.BlockSpec((tm, LANES), lambda i: (i, 0)),
        out_shape=jax.ShapeDtypeStruct((m, LANES), F32),
        compiler_params=_cp("arbitrary"), name="router")(x, w_pad, b_pad)


def _moe_ffn_kernel(te_ref, nu_ref, x_ref, wg_ref, wu_ref, wd_ref, o_ref, xb_ref):
    i = pl.program_id(0)
    j = pl.program_id(1)
    used = i < nu_ref[0]

    @pl.when(used)
    def _():
        @pl.when(j == 0)
        def _():
            xb_ref[...] = x_ref[...].astype(BF16)
            o_ref[...] = jnp.zeros_like(o_ref)

        _swiglu_accumulate(xb_ref, wg_ref[0], wu_ref[0], wd_ref[0], o_ref)

    @pl.when(jnp.logical_and(jnp.logical_not(used), j == 0))
    def _():
        o_ref[...] = jnp.zeros_like(o_ref)


def moe_ffn(x_sorted, tile_expert, n_used, wg, wu, wd, tf):
    n_rows = x_sorted.shape[0]
    n_tiles = n_rows // MOE_TILE
    nf = D_FF // tf

    def jj(i, j, nu):
        return jnp.where(i < nu[0], j, nf - 1)

    kern = _moe_ffn_kernel
    gs = pltpu.PrefetchScalarGridSpec(
        num_scalar_prefetch=2, grid=(n_tiles, nf),
        in_specs=[pl.BlockSpec((MOE_TILE, D_MODEL), lambda i, j, te, nu: (i, 0)),
                  pl.BlockSpec((1, D_MODEL, tf), lambda i, j, te, nu: (te[i], 0, jj(i, j, nu))),
                  pl.BlockSpec((1, D_MODEL, tf), lambda i, j, te, nu: (te[i], 0, jj(i, j, nu))),
                  pl.BlockSpec((1, tf, D_MODEL), lambda i, j, te, nu: (te[i], jj(i, j, nu), 0))],
        out_specs=pl.BlockSpec((MOE_TILE, D_MODEL), lambda i, j, te, nu: (i, 0)),
        scratch_shapes=[pltpu.VMEM((MOE_TILE, D_MODEL), BF16)])
    return pl.pallas_call(
        kern, grid_spec=gs, out_shape=jax.ShapeDtypeStruct((n_rows, D_MODEL), F32),
        compiler_params=_cp("arbitrary", "arbitrary"), name="moe_ffn")(
            tile_expert, n_used, x_sorted, wg, wu, wd)


def _combine_ln_kernel(x_ref, r_ref, ya_ref, yb_ref, g_ref, b_ref, o_ref):
    r = r_ref[...]
    y = r[:, TOP_K:TOP_K + 1] * ya_ref[...] + r[:, TOP_K + 1:TOP_K + 2] * yb_ref[...]
    o_ref[...] = _layer_norm(ALPHA * x_ref[...] + y, g_ref[...], b_ref[...])


def combine_ln(x, r, ya, yb, g, b, tm):
    m = x.shape[0]
    row = pl.BlockSpec((tm, D_MODEL), lambda i: (i, 0))
    vec = pl.BlockSpec((1, D_MODEL), lambda i: (0, 0))
    return pl.pallas_call(
        _combine_ln_kernel, grid=(m // tm,),
        in_specs=[row, pl.BlockSpec((tm, LANES), lambda i: (i, 0)), row, row, vec, vec], out_specs=row,
        out_shape=jax.ShapeDtypeStruct((m, D_MODEL), F32),
        compiler_params=_cp("arbitrary"), name="combine_ln")(x, r, ya, yb, g, b)


def moe_ln(x, router_w, router_b, wg, wu, wd, g, b, tm):
    m = x.shape[0]
    w_pad = jnp.zeros((D_MODEL, LANES), F32).at[:, :N_EXPERTS].set(router_w)
    b_pad = jnp.zeros((1, LANES), F32).at[0, :N_EXPERTS].set(router_b)
    r = router(x, w_pad, b_pad, tm)
    top_idx = r[:, :TOP_K].astype(jnp.int32)
    flat_e = top_idx.reshape(-1)
    n_slot = m * TOP_K
    onehot = (flat_e[:, None] == jnp.arange(N_EXPERTS, dtype=jnp.int32)[None, :]).astype(jnp.int32)
    rank = jnp.take_along_axis(jnp.cumsum(onehot, axis=0) - onehot, flat_e[:, None], axis=1)[:, 0]
    counts = jnp.sum(onehot, axis=0)
    tiles_per = (counts + MOE_TILE - 1) // MOE_TILE
    tile_end = jnp.cumsum(tiles_per)
    dest = (tile_end - tiles_per)[flat_e] * MOE_TILE + rank
    n_tiles = -(-n_slot // MOE_TILE) + N_EXPERTS
    n_rows = n_tiles * MOE_TILE
    row_tok = (jnp.arange(n_rows, dtype=jnp.int32) % m).at[dest].set(jnp.arange(n_slot, dtype=jnp.int32) // TOP_K)
    n_used = tile_end[-1:].astype(jnp.int32)
    tile_ids = jnp.minimum(jnp.arange(n_tiles, dtype=jnp.int32), n_used[0] - 1)
    tile_expert = jnp.minimum(jnp.searchsorted(tile_end, tile_ids, side='right'), N_EXPERTS - 1).astype(jnp.int32)
    x_sorted = x[row_tok]
    y_sorted = moe_ffn(x_sorted, tile_expert, n_used, wg, wu, wd, tf=512)
    dest2 = dest.reshape(m, TOP_K)
    return combine_ln(x, r, y_sorted[dest2[:, 0]], y_sorted[dest2[:, 1]], g, b, tm)


def kernel(x_prompt, x_sample, cache_sb_k, cache_sb_v, cache_mem_k, cache_mem_v, state_ml_C, state_ml_n, state_ml_m, state_rt_S, state_s5_re, state_s5_im, page_table, mem_prompt, w_in, sb_bias, ml_b_i, ml_b_f, ml_norm_g, rt_norm_g, s5_A_re, s5_A_im, s5_log_dt, s5_B_re, s5_B_im, s5_C_re, s5_C_im, s5_D, s5_glu_w, s5_glu_b, w_out, ca_wq, ca_wk, ca_wv, ca_wo, ln_g, ln_b, ffn_w_gate, ffn_w_up, ffn_w_down, moe_router_w, moe_router_b, moe_w_gate, moe_w_up, moe_w_down):
    bp, tp, _ = x_prompt.shape
    bs, ts, _ = x_sample.shape
    n_p, n_s = bp * tp, bs * ts
    tm = 640
    assert (n_p + n_s) % (2 * tm) == 0 and n_p % n_s == 0 and tp % 512 == 0 and bp % 8 == 0 and bs % 8 == 0
    x = jnp.concatenate([x_prompt.reshape(n_p, D_MODEL), x_sample.reshape(n_s, D_MODEL)], axis=0)
    uu = _suffix_matrix()
    g_off = 7 * GROUP_W
    half = HEAD_DIM // 2
    freq = ROPE_BASE ** (-jnp.arange(half, dtype=F32) / half)

    def rope_tables(pos):
        ang = pos.astype(F32)[:, None] * freq[None, :]
        cos, sin = jnp.cos(ang), jnp.sin(ang)
        return (jnp.tile(jnp.concatenate([cos, cos], axis=1), (1, N_HEADS)),
                jnp.tile(jnp.concatenate([-sin, sin], axis=1), (1, N_HEADS)))

    cos_p, sin_p = rope_tables(jnp.arange(tp, dtype=jnp.int32))
    cos_s, sin_s = rope_tables(PAST_LEN + jnp.arange(ts, dtype=jnp.int32))
    cache_kt = cache_sb_k.transpose(0, 1, 3, 4, 2)
    cache_vt = cache_sb_v.transpose(0, 1, 3, 4, 2)

    p_st = [[] for _ in range(10)]
    s_st = [[] for _ in range(8)]
    for l in range(DEPTH):
        wl = w_in[l]
        w_cat = jnp.concatenate([wl[:, :g_off], wl[:, g_off + 2 * N_HEADS:], wl[:, g_off:g_off + 2 * N_HEADS],
                                 jnp.zeros((D_MODEL, PROJ_W - wl.shape[1]), F32)], axis=1).astype(BF16)
        proj_p = linear(x, w_cat, 512, 0, n_p)
        proj_s = linear(x, w_cat, n_s, n_p // n_s, n_s)
        proj_p3 = proj_p.reshape(bp, tp, PROJ_W)
        proj_s3 = proj_s.reshape(bs, ts, PROJ_W)
        gate_bias = jnp.zeros((1, LANES), F32).at[0, :2 * N_HEADS].set(jnp.concatenate([ml_b_i[l], ml_b_f[l]]))
        ml_g = ml_norm_g[l][None, :]
        rt_g = rt_norm_g[l][None, :]
        wb, a1, a2, wc = _s5_weights(s5_A_re[l], s5_A_im[l], s5_log_dt[l], s5_B_re[l], s5_B_im[l],
                                     s5_C_re[l], s5_C_im[l])
        s5_d = s5_D[l][None, :]
        glu_w = s5_glu_w[l].astype(BF16)
        glu_b = s5_glu_b[l][None, :]

        o_sb_p = sb_attention_prompt(proj_p, sb_bias[l], uu, bp, tp, tq=256)
        o_ml_p, cn_p, m_p = mlstm_mixer(proj_p3, gate_bias, ml_g, jnp.zeros((bp, N_HEADS, HEAD_DIM, LANES), F32),
                                        jnp.zeros((bp, N_HEADS), F32), bb=8)
        o_rt_p, rs_p = retention_mixer(proj_p3, cos_p, sin_p, rt_g,
                                       jnp.zeros((bp, N_HEADS, HEAD_DIM, HEAD_DIM), F32), bb=8)
        o_ml_p = o_ml_p.reshape(n_p, GROUP_W)
        o_rt_p = o_rt_p.reshape(n_p, GROUP_W)
        o_s5_p, h5_p = s5_mixer(proj_p3, 64, wb, a1, a2, jnp.zeros((bp, 2 * S5_W), F32), wc, s5_d, glu_w, glu_b)
        o_s5_p = o_s5_p.reshape(n_p, GROUP_W)

        o_sb_s = sb_attention_sample(proj_s, 0, cache_kt, cache_vt, page_table, l, sb_bias[l], uu, n_pp=16)
        cn0 = jnp.concatenate([state_ml_C[:, l], state_ml_n[:, l][..., None],
                               jnp.zeros((bs, N_HEADS, HEAD_DIM, LANES - HEAD_DIM - 1), F32)], axis=-1)
        o_ml_s, cn_s, m_s = mlstm_mixer(proj_s3, gate_bias, ml_g, cn0, state_ml_m[:, l], bb=8)
        o_rt_s, rs_s = retention_mixer(proj_s3, cos_s, sin_s, rt_g, state_rt_S[:, l], bb=8)
        o_ml_s = o_ml_s.reshape(n_s, GROUP_W)
        o_rt_s = o_rt_s.reshape(n_s, GROUP_W)
        h0_s = jnp.concatenate([state_s5_re[:, l].reshape(bs, S5_W), state_s5_im[:, l].reshape(bs, S5_W)], axis=1)
        o_s5_s, h5_s = s5_mixer(proj_s3, ts, wb, a1, a2, h0_s, wc, s5_d, glu_w, glu_b)
        o_s5_s = o_s5_s.reshape(n_s, GROUP_W)

        wo_mix = w_out[l].astype(BF16)
        g0, b0 = ln_g[l, 0][None, :], ln_b[l, 0][None, :]
        x = mix_out_ln(x, (o_sb_p, o_ml_p, o_rt_p, o_s5_p), wo_mix, g0, b0, 512, 0)
        x = mix_out_ln(x, (o_sb_s, o_ml_s, o_rt_s, o_s5_s), wo_mix, g0, b0, n_s, n_p // n_s)

        mem_kv = linear(mem_prompt.reshape(bp * N_MEM, D_MODEL),
                        jnp.concatenate([ca_wk[l], ca_wv[l]], axis=1).astype(BF16), 512)
        mk_p = mem_kv[:, :GROUP_W].reshape(bp, N_MEM, GROUP_W)
        mv_p = mem_kv[:, GROUP_W:].reshape(bp, N_MEM, GROUP_W)
        wq = ca_wq[l].astype(BF16)
        wo = ca_wo[l].astype(BF16)
        g1, b1 = ln_g[l, 1][None, :], ln_b[l, 1][None, :]
        x = cross_attn_ln(x, 0, bp, tp, 512, mk_p, mv_p, wq, wo, g1, b1)
        x = cross_attn_ln(x, n_p // ts, bs, ts, ts, cache_mem_k[:, l].reshape(bs, N_MEM, GROUP_W),
                          cache_mem_v[:, l].reshape(bs, N_MEM, GROUP_W), wq, wo, g1, b1)

        g2, b2 = ln_g[l, 2][None, :], ln_b[l, 2][None, :]
        j = l // 2
        if l % 2 == 0:
            x = ffn_ln(x, ffn_w_gate[j], ffn_w_up[j], ffn_w_down[j], g2, b2, 2 * tm, tf=256)
        else:
            x = moe_ln(x, moe_router_w[j], moe_router_b[j], moe_w_gate[j], moe_w_up[j], moe_w_down[j], g2, b2, tm)

        def heads(a, nb_, t_):
            return a.reshape(nb_, t_, N_HEADS, HEAD_DIM)

        p_st[0].append(heads(proj_p[:, C_SK * GROUP_W:(C_SK + 1) * GROUP_W], bp, tp))
        p_st[1].append(heads(proj_p[:, C_SV * GROUP_W:(C_SV + 1) * GROUP_W], bp, tp))
        p_st[2].append(heads(mk_p, bp, N_MEM))
        p_st[3].append(heads(mv_p, bp, N_MEM))
        s_st[0].append(heads(proj_s[:, C_SK * GROUP_W:(C_SK + 1) * GROUP_W], bs, ts))
        s_st[1].append(heads(proj_s[:, C_SV * GROUP_W:(C_SV + 1) * GROUP_W], bs, ts))
        for st, cn, mm, rs, h5, nb_ in ((p_st, cn_p, m_p, rs_p, h5_p, bp), (s_st, cn_s, m_s, rs_s, h5_s, bs)):
            off = 4 if st is p_st else 2
            st[off + 0].append(cn[..., :HEAD_DIM])
            st[off + 1].append(cn[..., HEAD_DIM])
            st[off + 2].append(mm[:, 0, :N_HEADS])
            st[off + 3].append(rs)
            st[off + 4].append(h5[:, :S5_W].reshape(nb_, S5_GROUPS, S5_STATE))
            st[off + 5].append(h5[:, S5_W:].reshape(nb_, S5_GROUPS, S5_STATE))

    y_prompt = x[:n_p].reshape(bp, tp, D_MODEL)
    y_sample = x[n_p:].reshape(bs, ts, D_MODEL)
    p_out = [jnp.stack(a, axis=1) for a in p_st]
    s_out = [jnp.stack(a, axis=1) for a in s_st]
    return (y_prompt, y_sample, *p_out, *s_out)
```

```python
import functools
import math

import numpy as np
import jax
import jax.numpy as jnp
from jax import lax
from jax.experimental import pallas as pl
from jax.experimental.pallas import tpu as pltpu

F32 = jnp.float32
BF16 = jnp.bfloat16

D_MODEL = 1024
DEPTH = 2
PAST_LEN = 8192
PAGE_SIZE = 128
HEAD_DIM = 64
N_HEADS = 4
GROUP_W = N_HEADS * HEAD_DIM
S5_GROUPS = 16
S5_GROUP = 16
S5_STATE = 64
S5_W = S5_GROUPS * S5_STATE
N_MEM = 256
D_FF = 3584
N_EXPERTS = 8
TOP_K = 2
CHUNK = 64
ROPE_BASE = 10000.0
LN_EPS = 1e-5
GN_EPS = 1e-6
ALPHA = (2 * DEPTH) ** 0.25
QK_SCALE = HEAD_DIM ** -0.5
LOG2E = math.log2(math.e)

LANES = 128
PROJ_W = 25 * LANES
C_SQ, C_SK, C_SV, C_MQ, C_MK, C_MV, C_MO, C_RQ, C_RK, C_RV, C_RG, C_SU = range(12)
C_GATES = 12 * GROUP_W // LANES
VMEM_LIMIT = 48 * 1024 * 1024
MOE_TILE = 1024


def _cp(*sem):
    return pltpu.CompilerParams(dimension_semantics=sem, vmem_limit_bytes=VMEM_LIMIT)


def _dot(a, b):
    return jnp.dot(a, b, preferred_element_type=F32)


def _dot_nt(a, b):
    return lax.dot_general(a, b, (((1,), (1,)), ((), ())), preferred_element_type=F32)


def _dot_tn(a, b):
    return lax.dot_general(a, b, (((0,), (0,)), ((), ())), preferred_element_type=F32)


def _layer_norm(y, g, b):
    mu = jnp.mean(y, axis=-1, keepdims=True)
    yc = y - mu
    var = jnp.mean(yc * yc, axis=-1, keepdims=True)
    return yc * lax.rsqrt(var + LN_EPS) * g + b


def _row_sum(x, scale=1.0):
    ones = jnp.full((x.shape[1], LANES), scale, BF16)
    hi = x.astype(BF16)
    lo = (x - hi.astype(F32)).astype(BF16)
    return (_dot(hi, ones) + _dot(lo, ones))[:, :x.shape[1]]


def _cumsum_rows(tril, x):
    hi = x.astype(BF16)
    lo = (x - hi.astype(F32)).astype(BF16)
    return _dot(tril, hi) + _dot(tril, lo)


def _head_norm_all(hd):
    inv = 1.0 / HEAD_DIM
    mu = {p: _row_sum(x, inv) for p, x in hd.items()}
    hc = {p: hd[p] - mu[p] for p in hd}
    var = {p: _row_sum(hc[p] * hc[p], inv) for p in hd}
    return {p: hc[p] * lax.rsqrt(var[p] + GN_EPS) for p in hd}


def _neg_softplus(z):
    return -(jnp.maximum(z, 0.0) + jnp.log1p(jnp.exp(-jnp.abs(z))))


def _log_sigmoid(z):
    return _neg_softplus(-z)


def _linear_kernel(x_ref, w_ref, o_ref):
    o_ref[...] = _dot(x_ref[...].astype(BF16), w_ref[...]).astype(o_ref.dtype)


def linear(x, w, tm, row_blk0=0, n_rows=None, out_dtype=F32):
    m, k = x.shape
    m = m if n_rows is None else n_rows
    n = w.shape[1]
    return pl.pallas_call(
        _linear_kernel, grid=(m // tm,),
        in_specs=[pl.BlockSpec((tm, k), lambda i: (row_blk0 + i, 0)), pl.BlockSpec((k, n), lambda i: (0, 0))],
        out_specs=pl.BlockSpec((tm, n), lambda i: (i, 0)),
        out_shape=jax.ShapeDtypeStruct((m, n), out_dtype),
        compiler_params=_cp("arbitrary"), name="linear")(x, w)


def _in_proj_kernel(x_ref, w_ref, o_ref, kt_ref, vt_ref):
    o = _dot(x_ref[...].astype(BF16), w_ref[...])
    o_ref[...] = o
    kt_ref[0] = o[:, C_SK * GROUP_W:(C_SK + 1) * GROUP_W].T
    vt_ref[0] = o[:, C_SV * GROUP_W:(C_SV + 1) * GROUP_W].T


def in_proj_prompt(x, w, n_batch, seq, tm):
    k = x.shape[1]
    n = w.shape[1]
    nt = seq // tm
    t_spec = pl.BlockSpec((1, GROUP_W, tm), lambda i: (i // nt, 0, i % nt))
    t_shape = jax.ShapeDtypeStruct((n_batch, GROUP_W, seq), F32)
    return pl.pallas_call(
        _in_proj_kernel, grid=(n_batch * nt,),
        in_specs=[pl.BlockSpec((tm, k), lambda i: (i, 0)), pl.BlockSpec((k, n), lambda i: (0, 0))],
        out_specs=[pl.BlockSpec((tm, n), lambda i: (i, 0)), t_spec, t_spec],
        out_shape=[jax.ShapeDtypeStruct((n_batch * seq, n), F32), t_shape, t_shape],
        compiler_params=_cp("arbitrary"), name="in_proj")(x, w)


def _mix_out_ln_kernel(x_ref, a_ref, b_ref, c_ref, d_ref, w_ref, g_ref, bias_ref, o_ref):
    h = sum(_dot(p[...].astype(BF16), w_ref[pl.ds(n * GROUP_W, GROUP_W), :])
            for n, p in enumerate((a_ref, b_ref, c_ref, d_ref)))
    o_ref[...] = _layer_norm(ALPHA * x_ref[...] + h, g_ref[...], bias_ref[...])


def mix_out_ln(x, parts, w, g, b, tm, row_blk0):
    rows = parts[0].shape[0]
    row_spec = pl.BlockSpec((tm, D_MODEL), lambda i: (row_blk0 + i, 0))
    part_spec = pl.BlockSpec((tm, GROUP_W), lambda i: (i, 0))
    vec = pl.BlockSpec((1, D_MODEL), lambda i: (0, 0))
    return pl.pallas_call(
        _mix_out_ln_kernel, grid=(rows // tm,),
        in_specs=[row_spec, part_spec, part_spec, part_spec, part_spec,
                  pl.BlockSpec((D_MODEL, D_MODEL), lambda i: (0, 0)), vec, vec],
        out_specs=row_spec, out_shape=jax.ShapeDtypeStruct(x.shape, F32), input_output_aliases={0: 0},
        compiler_params=_cp("arbitrary"), name="mix_out_ln")(x, *parts, w, g, b)


def _suffix_matrix():
    j = np.arange(LANES)
    u = (j[:, None] >= j[None, :]).astype(np.float32)
    uu = np.concatenate([u, np.ones((LANES, LANES), np.float32)], axis=1)
    return jnp.asarray(np.concatenate([uu, uu], axis=0), dtype=BF16)


def _suffix_sums(lr, uu):
    hi = lr.astype(BF16)
    lo = (lr - hi.astype(F32)).astype(BF16)
    r = _dot(jnp.concatenate([hi, lo], axis=1), uu)
    return r[:, :LANES], r[:, LANES:]


def _log2_rem(z2):
    return jnp.minimum(-z2, 0.0) - jnp.log2(1.0 + jnp.exp2(-jnp.abs(z2)))


def _sb_prompt_kernel(bias_ref, q_ref, k_ref, v_ref, uu_ref, o_ref, acc_ref, car_ref, *, tq):
    i = pl.program_id(1)
    tk = LANES
    nsub = tq // tk
    acc_ref[...] = jnp.zeros_like(acc_ref)
    car_ref[...] = jnp.zeros_like(car_ref)
    q = (q_ref[...] * (QK_SCALE * LOG2E)).astype(BF16)
    qh = [q[:, h * HEAD_DIM:(h + 1) * HEAD_DIM] for h in range(N_HEADS)]
    b2 = [bias_ref[h] * LOG2E for h in range(N_HEADS)]
    uu = uu_ref[...]
    row = lax.broadcasted_iota(jnp.int32, (tq, tk), 0)
    col = lax.broadcasted_iota(jnp.int32, (tq, tk), 1)
    heads = range(N_HEADS)

    def block_pair(j_hi, causal_hi, causal_lo):
        kb, vb, z2 = [], [], []
        for d in range(2):
            r0 = pl.multiple_of((j_hi - d) * tk, tk)
            kb.append(k_ref[pl.ds(r0, tk), :].astype(BF16))
            vb.append(v_ref[pl.ds(r0, tk), :].astype(BF16))
            z2.append([_dot_nt(qh[h], kb[d][:, h * HEAD_DIM:(h + 1) * HEAD_DIM]) + b2[h] for h in heads])
        cs, tot = [], []
        for d, causal in enumerate((causal_hi, causal_lo)):
            lr = [_log2_rem(z) for z in z2[d]]
            if causal is not None:
                lr = [jnp.where(causal, a, 0.0) for a in lr]
            c, t = _suffix_sums(jnp.concatenate(lr, axis=0), uu)
            cs.append(c)
            tot.append(t)
        car = [car_ref[h] for h in heads]
        pv = []
        for d, causal in enumerate((causal_hi, causal_lo)):
            w = [jnp.exp2(z2[d][h] + cs[d][h * tq:(h + 1) * tq] + car[h]) for h in heads]
            if causal is not None:
                w = [jnp.where(causal, a, 0.0) for a in w]
            pv.append([_dot(w[h].astype(BF16), vb[d][:, h * HEAD_DIM:(h + 1) * HEAD_DIM]) for h in heads])
            car = [car[h] + tot[d][h * tq:(h + 1) * tq] for h in heads]
        for h in heads:
            acc_ref[h] += pv[0][h] + pv[1][h]
            car_ref[h] = car[h]

    assert nsub == 2
    block_pair(i * nsub + 1, (col + tk) < row, col < row)

    def body(jj, carry):
        block_pair(i * nsub - 1 - 2 * jj, None, None)
        return carry

    lax.fori_loop(0, i, body, 0)
    o_ref[...] = jnp.concatenate([acc_ref[h] for h in range(N_HEADS)], axis=1)


def sb_attention_prompt(proj, sb_bias, uu, n_batch, seq, tq):
    nq = seq // tq
    kern = functools.partial(_sb_prompt_kernel, tq=tq)
    return pl.pallas_call(
        kern, grid=(n_batch, nq),
        in_specs=[pl.BlockSpec(memory_space=pltpu.SMEM),
                  pl.BlockSpec((tq, GROUP_W), lambda b, i: (b * nq + i, C_SQ)),
                  pl.BlockSpec((seq, GROUP_W), lambda b, i: (b, C_SK)),
                  pl.BlockSpec((seq, GROUP_W), lambda b, i: (b, C_SV)),
                  pl.BlockSpec((2 * LANES, 2 * LANES), lambda b, i: (0, 0))],
        out_specs=pl.BlockSpec((tq, GROUP_W), lambda b, i: (b * nq + i, 0)),
        out_shape=jax.ShapeDtypeStruct((n_batch * seq, GROUP_W), F32),
        scratch_shapes=[pltpu.VMEM((N_HEADS, tq, HEAD_DIM), F32), pltpu.VMEM((N_HEADS, tq, LANES), F32)],
        compiler_params=_cp("arbitrary", "arbitrary"), name="sb_prompt")(sb_bias, proj, proj, proj, uu)


def _sb_sample_kernel(pt_ref, bias_ref, q_ref, kn_ref, vn_ref, u8_ref, uu_ref, *rest, n_pp, n_steps):
    k_refs = rest[:n_pp]
    v_refs = rest[n_pp:2 * n_pp]
    o_ref = rest[2 * n_pp]
    acc_ref, car_ref = rest[2 * n_pp + 1:]
    s = pl.program_id(1)
    nq = q_ref.shape[0]
    q = (q_ref[...] * (QK_SCALE * LOG2E)).astype(BF16)
    qh = [q[:, h * HEAD_DIM:(h + 1) * HEAD_DIM] for h in range(N_HEADS)]
    b2 = [bias_ref[h] * LOG2E for h in range(N_HEADS)]

    @pl.when(s == 0)
    def _():
        kn = kn_ref[...].astype(BF16)
        vn = vn_ref[...].astype(BF16)
        t = lax.broadcasted_iota(jnp.int32, (nq, nq), 0)
        c = lax.broadcasted_iota(jnp.int32, (nq, nq), 1)
        causal = c < t
        for h in range(N_HEADS):
            sl = slice(h * HEAD_DIM, (h + 1) * HEAD_DIM)
            z2 = _dot_nt(qh[h], kn[:, sl]) + b2[h]
            lr = jnp.where(causal, _log2_rem(z2), 0.0)
            cs = jnp.dot(lr, u8_ref[...], preferred_element_type=F32, precision=lax.Precision.HIGHEST)
            w = jnp.where(causal, jnp.exp2(z2 + cs), 0.0)
            acc_ref[h] = _dot(w.astype(BF16), vn[:, sl])
            car_ref[h] = jnp.broadcast_to(jnp.sum(lr, axis=1, keepdims=True), (nq, LANES))

    uu = uu_ref[...]
    z2s = []
    for p in range(n_pp):
        for h in range(N_HEADS):
            z2s.append(_dot(qh[h], k_refs[p][0, 0, h].astype(BF16)) + b2[h])
    cs_all, tot_all = _suffix_sums(_log2_rem(jnp.concatenate(z2s, axis=0)), uu)
    for h in range(N_HEADS):
        car = car_ref[h]
        acc = acc_ref[h]
        for p in range(n_pp):
            r = (p * N_HEADS + h) * nq
            w = jnp.exp2(z2s[p * N_HEADS + h] + cs_all[r:r + nq] + car)
            acc = acc + _dot_nt(w.astype(BF16), v_refs[p][0, 0, h].astype(BF16))
            car = car + tot_all[r:r + nq]
        car_ref[h] = car
        acc_ref[h] = acc

    @pl.when(s == n_steps - 1)
    def _():
        o_ref[...] = jnp.concatenate([acc_ref[h] for h in range(N_HEADS)], axis=1)


def sb_attention_sample(proj, row_blk0, cache_kt, cache_vt, page_table, layer, sb_bias, uu, n_pp):
    n_batch, n_pages = page_table.shape
    nq = 8
    n_steps = n_pages // n_pp
    u8 = jnp.asarray((np.arange(nq)[:, None] >= np.arange(nq)[None, :]).astype(np.float32))

    def page_spec(p):
        return pl.BlockSpec((1, 1, N_HEADS, HEAD_DIM, PAGE_SIZE),
                            lambda b, s, pt: (pt[b, n_pages - 1 - (s * n_pp + p)], layer, 0, 0, 0))

    def row_spec(cblk):
        return pl.BlockSpec((nq, GROUP_W), lambda b, s, pt: (row_blk0 + b, cblk))

    kern = functools.partial(_sb_sample_kernel, n_pp=n_pp, n_steps=n_steps)
    gs = pltpu.PrefetchScalarGridSpec(
        num_scalar_prefetch=1, grid=(n_batch, n_steps),
        in_specs=[pl.BlockSpec(memory_space=pltpu.SMEM), row_spec(C_SQ), row_spec(C_SK), row_spec(C_SV),
                  pl.BlockSpec((nq, nq), lambda b, s, pt: (0, 0)),
                  pl.BlockSpec((2 * LANES, 2 * LANES), lambda b, s, pt: (0, 0))]
                 + [page_spec(p) for p in range(n_pp)] * 2,
        out_specs=pl.BlockSpec((nq, GROUP_W), lambda b, s, pt: (b, 0)),
        scratch_shapes=[pltpu.VMEM((N_HEADS, nq, HEAD_DIM), F32), pltpu.VMEM((N_HEADS, nq, LANES), F32)])
    return pl.pallas_call(
        kern, grid_spec=gs, out_shape=jax.ShapeDtypeStruct((n_batch * nq, GROUP_W), F32),
        compiler_params=_cp("arbitrary", "arbitrary"), name="sb_sample")(
            page_table, sb_bias, proj, proj, proj, u8, uu, *([cache_kt] * n_pp), *([cache_vt] * n_pp))


def _mlstm_kernel(m0_ref, q_ref, k_ref, v_ref, og_ref, gt_ref, gb_ref, ng_ref, tril_ref, cn0_ref,
                  o_ref, cn_ref, m_ref, ms_ref, *, bb, chunk, n_t):
    bi = pl.program_id(0)
    t = pl.program_id(1)
    seqs = range(bb)
    pairs = [(b, h) for b in seqs for h in range(N_HEADS)]

    @pl.when(t == 0)
    def _():
        cn_ref[...] = cn0_ref[...]
        for b, h in pairs:
            ms_ref[b * N_HEADS + h] = jnp.full((1, LANES), m0_ref[bi * bb + b, h], F32)

    tril = tril_ref[...]
    tri_mask = lax.broadcasted_iota(jnp.int32, (chunk, chunk), 1) <= lax.broadcasted_iota(jnp.int32, (chunk, chunk), 0)
    lane = lax.broadcasted_iota(jnp.int32, (chunk, HEAD_DIM), 1)
    ones_col = jnp.where(lane == 0, 1.0, 0.0).astype(F32)
    hs = lambda h: slice(h * HEAD_DIM, (h + 1) * HEAD_DIM)

    gt = [gt_ref[b] + gb_ref[...] for b in seqs]
    bc = [_cumsum_rows(tril, _log_sigmoid(g)) for g in gt]
    gt_t = [g.T for g in gt]
    bc_t = [x.T for x in bc]
    q = [q_ref[b].astype(BF16) for b in seqs]
    k = [(k_ref[b] * QK_SCALE).astype(BF16) for b in seqs]
    v = [v_ref[b] for b in seqs]
    m_prev = {p: ms_ref[p[0] * N_HEADS + p[1]][:, :1] for p in pairs}
    ig_col = {(b, h): gt[b][:, h:h + 1] for b, h in pairs}
    bc_col = {(b, h): bc[b][:, N_HEADS + h:N_HEADS + h + 1] for b, h in pairs}
    dm = {(b, h): jnp.where(tri_mask, bc_col[b, h] - (bc_t[b][N_HEADS + h:N_HEADS + h + 1, :] - gt_t[b][h:h + 1, :]),
                            -jnp.inf) for b, h in pairs}
    a = {p: bc_col[p] + m_prev[p] for p in pairs}
    m_new = {p: jnp.maximum(a[p], jnp.max(dm[p], axis=1, keepdims=True)) for p in pairs}
    inter = {p: jnp.exp(a[p] - m_new[p]) for p in pairs}
    s = {(b, h): _dot_nt(q[b][:, hs(h)], k[b][:, hs(h)]) * jnp.exp(dm[b, h] - m_new[b, h]) for b, h in pairs}
    v_ext = {(b, h): jnp.concatenate([v[b][:, hs(h)], ones_col], axis=1) for b, h in pairs}
    qc = {(b, h): _dot(q[b][:, hs(h)], cn_ref[b, h].astype(BF16)) for b, h in pairs}
    sv = {p: _dot(s[p].astype(BF16), v_ext[p].astype(BF16)) for p in pairs}
    m_last = {p: m_new[p][chunk - 1:chunk, :] for p in pairs}
    wl = {p: jnp.exp(bc_col[p][chunk - 1:chunk, :] - bc_col[p] + ig_col[p] - m_last[p]) for p in pairs}
    dl = {p: jnp.exp(a[p][chunk - 1:chunk, :] - m_last[p]) for p in pairs}
    upd = {(b, h): _dot_tn(k[b][:, hs(h)], (wl[b, h] * v_ext[b, h]).astype(BF16)) for b, h in pairs}
    rs = {p: _row_sum(s[p])[:, :1] for p in pairs}
    num = {p: inter[p] * qc[p][:, :HEAD_DIM] + sv[p][:, :HEAD_DIM] for p in pairs}
    den = {p: inter[p] * qc[p][:, HEAD_DIM:HEAD_DIM + 1] + rs[p] for p in pairs}
    hh = _head_norm_all({p: num[p] / jnp.maximum(jnp.abs(den[p]), jnp.exp(-m_new[p])) for p in pairs})
    for b, h in pairs:
        cn_ref[b, h] = dl[b, h] * cn_ref[b, h] + upd[b, h]
        ms_ref[b * N_HEADS + h] = jnp.broadcast_to(m_last[b, h], (1, LANES))
    for b in seqs:
        y = jnp.concatenate([hh[b, h] for h in range(N_HEADS)], axis=1)
        o_ref[b] = y * ng_ref[...] * jax.nn.sigmoid(og_ref[b])

    @pl.when(t == n_t - 1)
    def _():
        lane_m = lax.broadcasted_iota(jnp.int32, (1, LANES), 1)
        for b in seqs:
            m_out = jnp.zeros((1, LANES), F32)
            for h in range(N_HEADS):
                m_out = jnp.where(lane_m == h, ms_ref[b * N_HEADS + h], m_out)
            m_ref[b] = m_out


def mlstm_mixer(proj3, gate_bias, norm_g, cn0, m0, bb):
    n_batch, seq, _ = proj3.shape
    chunk = math.gcd(seq, CHUNK)
    n_t = seq // chunk
    tril = jnp.asarray(np.tril(np.ones((chunk, chunk), np.float32)), dtype=BF16)

    def row_spec(cblk, w=GROUP_W):
        return pl.BlockSpec((bb, chunk, w), lambda bi, t: (bi, t, cblk))

    def const_spec(shape):
        return pl.BlockSpec(shape, lambda bi, t: (0,) * len(shape))

    state_spec = pl.BlockSpec((bb, N_HEADS, HEAD_DIM, LANES), lambda bi, t: (bi, 0, 0, 0))
    kern = functools.partial(_mlstm_kernel, bb=bb, chunk=chunk, n_t=n_t)
    return pl.pallas_call(
        kern, grid=(n_batch // bb, n_t),
        in_specs=[pl.BlockSpec(memory_space=pltpu.SMEM),
                  row_spec(C_MQ), row_spec(C_MK), row_spec(C_MV), row_spec(C_MO), row_spec(C_GATES, LANES),
                  const_spec((1, LANES)), const_spec((1, GROUP_W)), const_spec((chunk, chunk)), state_spec],
        out_specs=[pl.BlockSpec((bb, chunk, GROUP_W), lambda bi, t: (bi, t, 0)), state_spec,
                   pl.BlockSpec((bb, 1, LANES), lambda bi, t: (bi, 0, 0))],
        out_shape=[jax.ShapeDtypeStruct((n_batch, seq, GROUP_W), F32),
                   jax.ShapeDtypeStruct((n_batch, N_HEADS, HEAD_DIM, LANES), F32),
                   jax.ShapeDtypeStruct((n_batch, 1, LANES), F32)],
        scratch_shapes=[pltpu.VMEM((bb * N_HEADS, 1, LANES), F32)],
        compiler_params=_cp("arbitrary", "arbitrary"), name="mlstm")(
            m0, proj3, proj3, proj3, proj3, proj3, gate_bias, norm_g, tril, cn0)


def _rope(x, cos, sin_signed):
    lane = lax.broadcasted_iota(jnp.int32, x.shape, 1)
    half = HEAD_DIM // 2
    swapped = jnp.where((lane % HEAD_DIM) < half, pltpu.roll(x, x.shape[1] - half, 1), pltpu.roll(x, half, 1))
    return x * cos + swapped * sin_signed


def _retention_kernel(q_ref, k_ref, v_ref, gg_ref, cos_ref, sin_ref, ng_ref, dec_ref, int_ref, wl_ref, dl_ref,
                      s0_ref, o_ref, s_ref, *, bb):
    t = pl.program_id(1)
    seqs = range(bb)
    pairs = [(b, h) for b in seqs for h in range(N_HEADS)]
    hs = lambda h: slice(h * HEAD_DIM, (h + 1) * HEAD_DIM)

    @pl.when(t == 0)
    def _():
        s_ref[...] = s0_ref[...]

    cos = cos_ref[...]
    sin = sin_ref[...]
    q = [_rope(q_ref[b], cos, sin).astype(BF16) for b in seqs]
    k = [(_rope(k_ref[b], cos, sin) * QK_SCALE).astype(BF16) for b in seqs]
    v = [v_ref[b] for b in seqs]
    s = {(b, h): _dot_nt(q[b][:, hs(h)], k[b][:, hs(h)]) * dec_ref[h] for b, h in pairs}
    qs = {(b, h): _dot(q[b][:, hs(h)], s_ref[b, h].astype(BF16)) for b, h in pairs}
    sv = {(b, h): _dot(s[b, h].astype(BF16), v[b][:, hs(h)].astype(BF16)) for b, h in pairs}
    upd = {(b, h): _dot_tn(k[b][:, hs(h)], (wl_ref[h] * v[b][:, hs(h)]).astype(BF16)) for b, h in pairs}
    o = _head_norm_all({(b, h): int_ref[h] * qs[b, h] + sv[b, h] for b, h in pairs})
    for b, h in pairs:
        s_ref[b, h] = dl_ref[h] * s_ref[b, h] + upd[b, h]
    for b in seqs:
        gg = gg_ref[b]
        o_ref[b] = jnp.concatenate([o[b, h] for h in range(N_HEADS)], axis=1) * ng_ref[...] * (gg * jax.nn.sigmoid(gg))


def _retention_consts(chunk):
    log_g = np.log(1.0 - np.exp2(-5.0 - np.arange(N_HEADS, dtype=np.float64)))
    tau = np.arange(chunk, dtype=np.float64)
    rel = tau[:, None] - tau[None, :]
    decay = np.where(rel >= 0, np.exp(log_g[:, None, None] * np.maximum(rel, 0.0)), 0.0)
    inter = np.exp(log_g[:, None] * (tau + 1.0))[..., None]
    wl = np.exp(log_g[:, None] * (chunk - 1.0 - tau))[..., None]
    dl = np.exp(log_g * chunk)[:, None, None]
    return tuple(jnp.asarray(a, F32) for a in (decay, inter, wl, dl))


def retention_mixer(proj3, cos, sin_signed, norm_g, s0, bb):
    n_batch, seq, _ = proj3.shape
    chunk = math.gcd(seq, CHUNK)
    dec, inter, wl, dl = _retention_consts(chunk)

    def row_spec(cblk):
        return pl.BlockSpec((bb, chunk, GROUP_W), lambda bi, t: (bi, t, cblk))

    def const_spec(shape):
        return pl.BlockSpec(shape, lambda bi, t: (0,) * len(shape))

    pos_spec = pl.BlockSpec((chunk, GROUP_W), lambda bi, t: (t, 0))
    state_spec = pl.BlockSpec((bb, N_HEADS, HEAD_DIM, HEAD_DIM), lambda bi, t: (bi, 0, 0, 0))
    kern = functools.partial(_retention_kernel, bb=bb)
    return pl.pallas_call(
        kern, grid=(n_batch // bb, seq // chunk),
        in_specs=[row_spec(C_RQ), row_spec(C_RK), row_spec(C_RV), row_spec(C_RG), pos_spec, pos_spec,
                  const_spec((1, GROUP_W)), const_spec(dec.shape), const_spec(inter.shape), const_spec(wl.shape),
                  const_spec(dl.shape), state_spec],
        out_specs=[pl.BlockSpec((bb, chunk, GROUP_W), lambda bi, t: (bi, t, 0)), state_spec],
        out_shape=[jax.ShapeDtypeStruct((n_batch, seq, GROUP_W), F32),
                   jax.ShapeDtypeStruct((n_batch, N_HEADS, HEAD_DIM, HEAD_DIM), F32)],
        compiler_params=_cp("arbitrary", "arbitrary"), name="retention")(
            proj3, proj3, proj3, proj3, cos, sin_signed, norm_g, dec, inter, wl, dl, s0)


def _s5_kernel(u_ref, wb_ref, a1_ref, a2_ref, h0_ref, wc_ref, d_ref, gw_ref, gb_ref, o_ref, hl_ref,
               hs_ref, ut_ref, yt_ref, *, nb, tt):
    c = pl.program_id(0)

    @pl.when(c == 0)
    def _():
        hl_ref[...] = h0_ref[...]

    halves = range(GROUP_W // LANES)
    for b in range(nb):
        ub = u_ref[b]
        for hf in halves:
            ut_ref[hf, pl.ds(b, tt, stride=nb), :] = ub[:, hf * LANES:(hf + 1) * LANES]
    u = jnp.concatenate([ut_ref[hf] for hf in halves], axis=1)
    hs_ref[...] = _dot(u.astype(BF16), wb_ref[...])
    a1 = jnp.broadcast_to(a1_ref[...], (nb, 2 * S5_W))
    a2 = jnp.broadcast_to(a2_ref[...], (nb, 2 * S5_W))

    def step(t, h):
        r0 = pl.multiple_of(t * nb, nb)
        swapped = jnp.concatenate([h[:, S5_W:], h[:, :S5_W]], axis=1)
        h = a1 * h + a2 * swapped + hs_ref[pl.ds(r0, nb), :]
        hs_ref[pl.ds(r0, nb), :] = h
        return h

    hl_ref[...] = lax.fori_loop(0, tt, step, hl_ref[...])
    y = _dot(hs_ref[...].astype(BF16), wc_ref[...]) + d_ref[...] * u
    g5 = jax.nn.gelu(y)
    yt = g5 * jax.nn.sigmoid(_dot(g5.astype(BF16), gw_ref[...]) + gb_ref[...])
    for hf in halves:
        yt_ref[hf] = yt[:, hf * LANES:(hf + 1) * LANES]
    for b in range(nb):
        o_ref[b] = jnp.concatenate([yt_ref[hf, pl.ds(b, tt, stride=nb), :] for hf in halves], axis=1)


def s5_mixer(proj3, tt, wb, a1, a2, h0, wc, d, glu_w, glu_b):
    nb, seq, _ = proj3.shape
    rows = tt * nb

    def const_spec(shape):
        return pl.BlockSpec(shape, lambda c: (0,) * len(shape))

    kern = functools.partial(_s5_kernel, nb=nb, tt=tt)
    return pl.pallas_call(
        kern, grid=(seq // tt,),
        in_specs=[pl.BlockSpec((nb, tt, GROUP_W), lambda c: (0, c, C_SU)),
                  const_spec(wb.shape), const_spec(a1.shape), const_spec(a2.shape), const_spec(h0.shape),
                  const_spec(wc.shape), const_spec(d.shape), const_spec(glu_w.shape), const_spec(glu_b.shape)],
        out_specs=[pl.BlockSpec((nb, tt, GROUP_W), lambda c: (0, c, 0)), const_spec(h0.shape)],
        out_shape=[jax.ShapeDtypeStruct((nb, seq, GROUP_W), F32), jax.ShapeDtypeStruct(h0.shape, F32)],
        scratch_shapes=[pltpu.VMEM((rows, 2 * S5_W), F32), pltpu.VMEM((GROUP_W // LANES, rows, LANES), F32),
                        pltpu.VMEM((GROUP_W // LANES, rows, LANES), F32)],
        compiler_params=_cp("arbitrary"), name="s5")(proj3, wb, a1, a2, h0, wc, d, glu_w, glu_b)


def _s5_weights(a_re, a_im, log_dt, b_re, b_im, c_re, c_im):
    lam = lax.complex(a_re, a_im)
    a_bar = jnp.exp(lam * jnp.exp(log_dt))
    b_bar = ((a_bar - 1.0) / lam)[..., None] * lax.complex(b_re, b_im)
    eye = jnp.eye(S5_GROUPS, dtype=F32)

    def in_map(m):
        return jnp.einsum('gpc,gh->gchp', m, eye).reshape(S5_GROUPS * S5_GROUP, S5_W)

    def out_map(m):
        return jnp.einsum('gcp,gh->gphc', m, eye).reshape(S5_W, S5_GROUPS * S5_GROUP)

    wb = jnp.concatenate([in_map(b_bar.real), in_map(b_bar.imag)], axis=1).astype(BF16)
    wc = jnp.concatenate([out_map(c_re), -out_map(c_im)], axis=0).astype(BF16)
    ar = a_bar.real.reshape(1, S5_W)
    ai = a_bar.imag.reshape(1, S5_W)
    return wb, jnp.concatenate([ar, ar], axis=1), jnp.concatenate([-ai, ai], axis=1), wc


def _cross_attn_kernel(x_ref, wq_ref, k_ref, v_ref, wo_ref, g_ref, b_ref, o_ref):
    x = x_ref[...]
    q = (_dot(x.astype(BF16), wq_ref[...]) * QK_SCALE).astype(BF16)
    k = k_ref[0].astype(BF16)
    v = v_ref[0].astype(BF16)
    hs = [slice(h * HEAD_DIM, (h + 1) * HEAD_DIM) for h in range(N_HEADS)]
    s = [_dot_nt(q[:, sl], k[:, sl]) for sl in hs]
    e = [jnp.exp(a - jnp.max(a, axis=1, keepdims=True)) for a in s]
    p = [a / jnp.sum(a, axis=1, keepdims=True) for a in e]
    o = jnp.concatenate([_dot(p[h].astype(BF16), v[:, hs[h]]) for h in range(N_HEADS)], axis=1)
    y = ALPHA * x + _dot(o.astype(BF16), wo_ref[...])
    o_ref[...] = _layer_norm(y, g_ref[...], b_ref[...])


def cross_attn_ln(x, row_blk0, n_batch, seq, tq, mem_k, mem_v, wq, wo, g, b):
    nq = seq // tq

    def const_spec(shape):
        return pl.BlockSpec(shape, lambda bb, i: (0,) * len(shape))

    row_spec = pl.BlockSpec((tq, D_MODEL), lambda bb, i: (row_blk0 + bb * nq + i, 0))
    mem_spec = pl.BlockSpec((1, N_MEM, GROUP_W), lambda bb, i: (bb, 0, 0))
    return pl.pallas_call(
        _cross_attn_kernel, grid=(n_batch, nq),
        in_specs=[row_spec, const_spec(wq.shape), mem_spec, mem_spec, const_spec(wo.shape),
                  const_spec(g.shape), const_spec(b.shape)],
        out_specs=row_spec, out_shape=jax.ShapeDtypeStruct(x.shape, F32), input_output_aliases={0: 0},
        compiler_params=_cp("arbitrary", "arbitrary"), name="cross_attn")(x, wq, mem_k, mem_v, wo, g, b)


SWIGLU_ROWS = 256


def _swiglu_accumulate(xb_ref, wg, wu, wd, acc_ref, hid_ref=None, n_valid=None):
    wgb, wub, wdb = wg.astype(BF16), wu.astype(BF16), wd.astype(BF16)
    n_sub = xb_ref.shape[0] // SWIGLU_ROWS

    def hidden(r):
        xb = xb_ref[pl.ds(r * SWIGLU_ROWS, SWIGLU_ROWS), :]
        gate = _dot(xb, wgb)
        up = _dot(xb, wub)
        return (gate * jax.nn.sigmoid(gate) * up).astype(BF16)

    if n_valid is None:
        hid = hidden(0)
        for r in range(n_sub):
            nxt = hidden(r + 1) if r + 1 < n_sub else None
            acc_ref[pl.ds(r * SWIGLU_ROWS, SWIGLU_ROWS), :] += _dot(hid, wdb)
            hid = nxt
        return

    def stage_a(r):
        hid_ref[r % 2] = hidden(r)

    def stage_b(r):
        acc_ref[pl.ds(r * SWIGLU_ROWS, SWIGLU_ROWS), :] += _dot(hid_ref[r % 2], wdb)

    stage_a(0)
    for r in range(n_sub):
        here = r * SWIGLU_ROWS < n_valid
        if r + 1 < n_sub:
            more = (r + 1) * SWIGLU_ROWS < n_valid

            @pl.when(more)
            def _(r=r):
                stage_a(r + 1)
                stage_b(r)

            @pl.when(jnp.logical_and(here, jnp.logical_not(more)))
            def _(r=r):
                stage_b(r)
        else:
            @pl.when(here)
            def _(r=r):
                stage_b(r)


def _ffn_kernel(x_ref, wg_ref, wu_ref, wd_ref, g_ref, b_ref, o_ref, xb_ref, *, nf):
    j = pl.program_id(1)

    @pl.when(j == 0)
    def _():
        xb_ref[...] = x_ref[...].astype(BF16)
        o_ref[...] = jnp.zeros_like(o_ref)

    _swiglu_accumulate(xb_ref, wg_ref[...], wu_ref[...], wd_ref[...], o_ref)

    @pl.when(j == nf - 1)
    def _():
        o_ref[...] = _layer_norm(ALPHA * x_ref[...] + o_ref[...], g_ref[...], b_ref[...])


def ffn_ln(x, wg, wu, wd, g, b, tm, tf):
    m = x.shape[0]
    nf = D_FF // tf
    kern = functools.partial(_ffn_kernel, nf=nf)
    return pl.pallas_call(
        kern, grid=(m // tm, nf),
        in_specs=[pl.BlockSpec((tm, D_MODEL), lambda i, j: (i, 0)),
                  pl.BlockSpec((D_MODEL, tf), lambda i, j: (0, j)), pl.BlockSpec((D_MODEL, tf), lambda i, j: (0, j)),
                  pl.BlockSpec((tf, D_MODEL), lambda i, j: (j, 0)),
                  pl.BlockSpec((1, D_MODEL), lambda i, j: (0, 0)), pl.BlockSpec((1, D_MODEL), lambda i, j: (0, 0))],
        out_specs=pl.BlockSpec((tm, D_MODEL), lambda i, j: (i, 0)),
        out_shape=jax.ShapeDtypeStruct((m, D_MODEL), F32),
        scratch_shapes=[pltpu.VMEM((tm, D_MODEL), BF16)],
        compiler_params=_cp("arbitrary", "arbitrary"), name="ffn")(x, wg, wu, wd, g, b)


def _router_kernel(x_ref, w_ref, b_ref, o_ref):
    logits = jnp.dot(x_ref[...], w_ref[...], preferred_element_type=F32, precision=lax.Precision.HIGHEST) + b_ref[...]
    lane = lax.broadcasted_iota(jnp.int32, logits.shape, 1)
    neg = jnp.float32(-jnp.inf)
    lg = jnp.where(lane < N_EXPERTS, logits, neg)
    m1 = jnp.max(lg, axis=1, keepdims=True)
    i1 = jnp.min(jnp.where(lg == m1, lane, LANES), axis=1, keepdims=True)
    lg2 = jnp.where(lane == i1, neg, lg)
    m2 = jnp.max(lg2, axis=1, keepdims=True)
    i2 = jnp.min(jnp.where(lg2 == m2, lane, LANES), axis=1, keepdims=True)
    e2 = jnp.exp(m2 - m1)
    g1 = 1.0 / (1.0 + e2)
    g2 = e2 / (1.0 + e2)
    out = jnp.where(lane == 0, i1.astype(F32), jnp.where(lane == 1, i2.astype(F32),
                    jnp.where(lane == 2, g1, jnp.where(lane == 3, g2, 0.0))))
    o_ref[...] = out


def router(x, w_pad, b_pad, tm):
    m = x.shape[0]
    return pl.pallas_call(
        _router_kernel, grid=(m // tm,),
        in_specs=[pl.BlockSpec((tm, D_MODEL), lambda i: (i, 0)), pl.BlockSpec((D_MODEL, LANES), lambda i: (0, 0)),
                  pl.BlockSpec((1, LANES), lambda i: (0, 0))],
        out_specs=pl.BlockSpec((tm, LANES), lambda i: (i, 0)),
        out_shape=jax.ShapeDtypeStruct((m, LANES), F32),
        compiler_params=_cp("arbitrary"), name="router")(x, w_pad, b_pad)


def _moe_ffn_kernel(te_ref, nu_ref, tr_ref, x_ref, wg_ref, wu_ref, wd_ref, o_ref, xb_ref, hid_ref):
    i = pl.program_id(0)
    j = pl.program_id(1)
    used = i < nu_ref[0]

    @pl.when(used)
    def _():
        @pl.when(j == 0)
        def _():
            xb_ref[...] = x_ref[...].astype(BF16)
            o_ref[...] = jnp.zeros_like(o_ref)

        _swiglu_accumulate(xb_ref, wg_ref[0], wu_ref[0], wd_ref[0], o_ref, hid_ref, tr_ref[i])

    @pl.when(jnp.logical_and(jnp.logical_not(used), j == 0))
    def _():
        o_ref[...] = jnp.zeros_like(o_ref)


def moe_ffn(x_sorted, tile_expert, n_used, tile_rows, wg, wu, wd, tf):
    n_rows = x_sorted.shape[0]
    n_tiles = n_rows // MOE_TILE
    nf = D_FF // tf

    def jj(i, j, nu):
        return jnp.where(i < nu[0], j, nf - 1)

    gs = pltpu.PrefetchScalarGridSpec(
        num_scalar_prefetch=3, grid=(n_tiles, nf),
        in_specs=[pl.BlockSpec((MOE_TILE, D_MODEL), lambda i, j, te, nu, tr: (i, 0)),
                  pl.BlockSpec((1, D_MODEL, tf), lambda i, j, te, nu, tr: (te[i], 0, jj(i, j, nu))),
                  pl.BlockSpec((1, D_MODEL, tf), lambda i, j, te, nu, tr: (te[i], 0, jj(i, j, nu))),
                  pl.BlockSpec((1, tf, D_MODEL), lambda i, j, te, nu, tr: (te[i], jj(i, j, nu), 0))],
        out_specs=pl.BlockSpec((MOE_TILE, D_MODEL), lambda i, j, te, nu, tr: (i, 0)),
        scratch_shapes=[pltpu.VMEM((MOE_TILE, D_MODEL), BF16), pltpu.VMEM((2, SWIGLU_ROWS, tf), BF16)])
    return pl.pallas_call(
        _moe_ffn_kernel, grid_spec=gs, out_shape=jax.ShapeDtypeStruct((n_rows, D_MODEL), F32),
        compiler_params=_cp("arbitrary", "arbitrary"), name="moe_ffn")(
            tile_expert, n_used, tile_rows, x_sorted, wg, wu, wd)


def _combine_ln_kernel(x_ref, r_ref, ya_ref, yb_ref, g_ref, b_ref, o_ref):
    r = r_ref[...]
    y = r[:, TOP_K:TOP_K + 1] * ya_ref[...] + r[:, TOP_K + 1:TOP_K + 2] * yb_ref[...]
    o_ref[...] = _layer_norm(ALPHA * x_ref[...] + y, g_ref[...], b_ref[...])


def combine_ln(x, r, ya, yb, g, b, tm, row_blk0, rows):
    row = pl.BlockSpec((tm, D_MODEL), lambda i: (row_blk0 + i, 0))
    vec = pl.BlockSpec((1, D_MODEL), lambda i: (0, 0))
    return pl.pallas_call(
        _combine_ln_kernel, grid=(rows // tm,),
        in_specs=[row, pl.BlockSpec((tm, LANES), lambda i: (row_blk0 + i, 0)), row, row, vec, vec],
        out_specs=pl.BlockSpec((tm, D_MODEL), lambda i: (i, 0)),
        out_shape=jax.ShapeDtypeStruct((rows, D_MODEL), F32),
        compiler_params=_cp("arbitrary"), name="combine_ln")(x, r, ya, yb, g, b)


def moe_ln(x, router_w, router_b, wg, wu, wd, g, b, tm, splits):
    m = x.shape[0]
    w_pad = jnp.zeros((D_MODEL, LANES), F32).at[:, :N_EXPERTS].set(router_w)
    b_pad = jnp.zeros((1, LANES), F32).at[0, :N_EXPERTS].set(router_b)
    r = router(x, w_pad, b_pad, tm)
    top_idx = r[:, :TOP_K].astype(jnp.int32)
    n_slot = m * TOP_K
    onehot = (top_idx[:, :, None] == jnp.arange(N_EXPERTS, dtype=jnp.int32)).astype(jnp.int32)
    per_tok = onehot[:, 0] + onehot[:, 1]
    before = jnp.cumsum(per_tok, axis=0) - per_tok
    counts = jnp.sum(per_tok, axis=0)
    tiles_per = (counts + MOE_TILE - 1) // MOE_TILE
    tile_end = jnp.cumsum(tiles_per)
    row0 = (tile_end - tiles_per) * MOE_TILE
    dest = jnp.sum(onehot * (before + row0)[:, None, :], axis=2)
    n_tiles = -(-n_slot // MOE_TILE) + N_EXPERTS
    n_rows = n_tiles * MOE_TILE
    row_tok = (jnp.arange(n_rows, dtype=jnp.int32) % m).at[dest.reshape(-1)].set(
        jnp.arange(n_slot, dtype=jnp.int32) // TOP_K)
    n_used = tile_end[-1:].astype(jnp.int32)
    tile_ids = jnp.minimum(jnp.arange(n_tiles, dtype=jnp.int32), n_used[0] - 1)
    tile_expert = jnp.minimum(jnp.sum((tile_end[None, :] <= tile_ids[:, None]).astype(jnp.int32), axis=1),
                              N_EXPERTS - 1)
    tile_rows = jnp.clip(counts[tile_expert] - (tile_ids - (tile_end - tiles_per)[tile_expert]) * MOE_TILE,
                         0, MOE_TILE).astype(jnp.int32)
    x_sorted = x[row_tok]
    y_sorted = moe_ffn(x_sorted, tile_expert, n_used, tile_rows, wg, wu, wd, tf=512)
    ya, yb = y_sorted[dest[:, 0]], y_sorted[dest[:, 1]]
    return [combine_ln(x, r, ya, yb, g, b, t, blk0, rows) for blk0, rows, t in splits]


def kernel(x_prompt, x_sample, cache_sb_k, cache_sb_v, cache_mem_k, cache_mem_v, state_ml_C, state_ml_n, state_ml_m, state_rt_S, state_s5_re, state_s5_im, page_table, mem_prompt, w_in, sb_bias, ml_b_i, ml_b_f, ml_norm_g, rt_norm_g, s5_A_re, s5_A_im, s5_log_dt, s5_B_re, s5_B_im, s5_C_re, s5_C_im, s5_D, s5_glu_w, s5_glu_b, w_out, ca_wq, ca_wk, ca_wv, ca_wo, ln_g, ln_b, ffn_w_gate, ffn_w_up, ffn_w_down, moe_router_w, moe_router_b, moe_w_gate, moe_w_up, moe_w_down):
    bp, tp, _ = x_prompt.shape
    bs, ts, _ = x_sample.shape
    n_p, n_s = bp * tp, bs * ts
    tm = 640
    assert (n_p + n_s) % (2 * tm) == 0 and n_p % n_s == 0 and tp % 512 == 0 and bp % 8 == 0 and bs % 8 == 0
    x = jnp.concatenate([x_prompt.reshape(n_p, D_MODEL), x_sample.reshape(n_s, D_MODEL)], axis=0)
    uu = _suffix_matrix()
    g_off = 7 * GROUP_W
    half = HEAD_DIM // 2
    freq = ROPE_BASE ** (-jnp.arange(half, dtype=F32) / half)

    def rope_tables(pos):
        ang = pos.astype(F32)[:, None] * freq[None, :]
        cos, sin = jnp.cos(ang), jnp.sin(ang)
        return (jnp.tile(jnp.concatenate([cos, cos], axis=1), (1, N_HEADS)),
                jnp.tile(jnp.concatenate([-sin, sin], axis=1), (1, N_HEADS)))

    cos_p, sin_p = rope_tables(jnp.arange(tp, dtype=jnp.int32))
    cos_s, sin_s = rope_tables(PAST_LEN + jnp.arange(ts, dtype=jnp.int32))
    cache_kt = cache_sb_k.transpose(0, 1, 3, 4, 2)
    cache_vt = cache_sb_v.transpose(0, 1, 3, 4, 2)

    p_st = [[] for _ in range(10)]
    s_st = [[] for _ in range(8)]
    for l in range(DEPTH):
        wl = w_in[l]
        w_cat = jnp.concatenate([wl[:, :g_off], wl[:, g_off + 2 * N_HEADS:], wl[:, g_off:g_off + 2 * N_HEADS],
                                 jnp.zeros((D_MODEL, PROJ_W - wl.shape[1]), F32)], axis=1).astype(BF16)
        proj_p, kt_p, vt_p = in_proj_prompt(x, w_cat, bp, tp, 512)
        proj_s = linear(x, w_cat, n_s, n_p // n_s, n_s)
        proj_p3 = proj_p.reshape(bp, tp, PROJ_W)
        proj_s3 = proj_s.reshape(bs, ts, PROJ_W)
        gate_bias = jnp.zeros((1, LANES), F32).at[0, :2 * N_HEADS].set(jnp.concatenate([ml_b_i[l], ml_b_f[l]]))
        ml_g = ml_norm_g[l][None, :]
        rt_g = rt_norm_g[l][None, :]
        wb, a1, a2, wc = _s5_weights(s5_A_re[l], s5_A_im[l], s5_log_dt[l], s5_B_re[l], s5_B_im[l],
                                     s5_C_re[l], s5_C_im[l])
        s5_d = s5_D[l][None, :]
        glu_w = s5_glu_w[l].astype(BF16)
        glu_b = s5_glu_b[l][None, :]

        o_sb_p = sb_attention_prompt(proj_p, sb_bias[l], uu, bp, tp, tq=256)
        o_ml_p, cn_p, m_p = mlstm_mixer(proj_p3, gate_bias, ml_g, jnp.zeros((bp, N_HEADS, HEAD_DIM, LANES), F32),
                                        jnp.zeros((bp, N_HEADS), F32), bb=8)
        o_rt_p, rs_p = retention_mixer(proj_p3, cos_p, sin_p, rt_g,
                                       jnp.zeros((bp, N_HEADS, HEAD_DIM, HEAD_DIM), F32), bb=8)
        o_ml_p = o_ml_p.reshape(n_p, GROUP_W)
        o_rt_p = o_rt_p.reshape(n_p, GROUP_W)
        o_s5_p, h5_p = s5_mixer(proj_p3, 64, wb, a1, a2, jnp.zeros((bp, 2 * S5_W), F32), wc, s5_d, glu_w, glu_b)
        o_s5_p = o_s5_p.reshape(n_p, GROUP_W)

        o_sb_s = sb_attention_sample(proj_s, 0, cache_kt, cache_vt, page_table, l, sb_bias[l], uu, n_pp=16)
        cn0 = jnp.concatenate([state_ml_C[:, l], state_ml_n[:, l][..., None],
                               jnp.zeros((bs, N_HEADS, HEAD_DIM, LANES - HEAD_DIM - 1), F32)], axis=-1)
        o_ml_s, cn_s, m_s = mlstm_mixer(proj_s3, gate_bias, ml_g, cn0, state_ml_m[:, l], bb=8)
        o_rt_s, rs_s = retention_mixer(proj_s3, cos_s, sin_s, rt_g, state_rt_S[:, l], bb=8)
        o_ml_s = o_ml_s.reshape(n_s, GROUP_W)
        o_rt_s = o_rt_s.reshape(n_s, GROUP_W)
        h0_s = jnp.concatenate([state_s5_re[:, l].reshape(bs, S5_W), state_s5_im[:, l].reshape(bs, S5_W)], axis=1)
        o_s5_s, h5_s = s5_mixer(proj_s3, ts, wb, a1, a2, h0_s, wc, s5_d, glu_w, glu_b)
        o_s5_s = o_s5_s.reshape(n_s, GROUP_W)

        wo_mix = w_out[l].astype(BF16)
        g0, b0 = ln_g[l, 0][None, :], ln_b[l, 0][None, :]
        x = mix_out_ln(x, (o_sb_p, o_ml_p, o_rt_p, o_s5_p), wo_mix, g0, b0, 512, 0)
        x = mix_out_ln(x, (o_sb_s, o_ml_s, o_rt_s, o_s5_s), wo_mix, g0, b0, n_s, n_p // n_s)

        mem_kv = linear(mem_prompt.reshape(bp * N_MEM, D_MODEL),
                        jnp.concatenate([ca_wk[l], ca_wv[l]], axis=1).astype(BF16), 512)
        mk_p = mem_kv[:, :GROUP_W].reshape(bp, N_MEM, GROUP_W)
        mv_p = mem_kv[:, GROUP_W:].reshape(bp, N_MEM, GROUP_W)
        wq = ca_wq[l].astype(BF16)
        wo = ca_wo[l].astype(BF16)
        g1, b1 = ln_g[l, 1][None, :], ln_b[l, 1][None, :]
        x = cross_attn_ln(x, 0, bp, tp, 512, mk_p, mv_p, wq, wo, g1, b1)
        x = cross_attn_ln(x, n_p // ts, bs, ts, ts, cache_mem_k[:, l].reshape(bs, N_MEM, GROUP_W),
                          cache_mem_v[:, l].reshape(bs, N_MEM, GROUP_W), wq, wo, g1, b1)

        g2, b2 = ln_g[l, 2][None, :], ln_b[l, 2][None, :]
        j = l // 2
        last = l == DEPTH - 1
        if l % 2 == 0:
            x = ffn_ln(x, ffn_w_gate[j], ffn_w_up[j], ffn_w_down[j], g2, b2, 2 * tm, tf=512)
            y_out = (x[:n_p], x[n_p:]) if last else None
        else:
            splits = [(0, n_p, 512), (n_p // n_s, n_s, n_s)] if last else [(0, n_p + n_s, tm)]
            y_out = moe_ln(x, moe_router_w[j], moe_router_b[j], moe_w_gate[j], moe_w_up[j], moe_w_down[j],
                           g2, b2, tm, splits)
            x = None if last else y_out[0]

        def heads(a, nb_, t_):
            return a.reshape(nb_, t_, N_HEADS, HEAD_DIM)

        p_st[0].append(kt_p)
        p_st[1].append(vt_p)
        p_st[2].append(heads(mk_p, bp, N_MEM))
        p_st[3].append(heads(mv_p, bp, N_MEM))
        s_st[0].append(heads(proj_s[:, C_SK * GROUP_W:(C_SK + 1) * GROUP_W], bs, ts))
        s_st[1].append(heads(proj_s[:, C_SV * GROUP_W:(C_SV + 1) * GROUP_W], bs, ts))
        for st, cn, mm, rs, h5, nb_ in ((p_st, cn_p, m_p, rs_p, h5_p, bp), (s_st, cn_s, m_s, rs_s, h5_s, bs)):
            off = 4 if st is p_st else 2
            st[off + 0].append(cn[..., :HEAD_DIM])
            st[off + 1].append(cn[..., HEAD_DIM])
            st[off + 2].append(mm[:, 0, :N_HEADS])
            st[off + 3].append(rs)
            st[off + 4].append(h5[:, :S5_W].reshape(nb_, S5_GROUPS, S5_STATE))
            st[off + 5].append(h5[:, S5_W:].reshape(nb_, S5_GROUPS, S5_STATE))

    y_prompt = y_out[0].reshape(bp, tp, D_MODEL)
    y_sample = y_out[1].reshape(bs, ts, D_MODEL)
    p_out = [jnp.stack(a, axis=1) for a in p_st]
    for i in range(2):
        p_out[i] = p_out[i].reshape(bp, DEPTH, N_HEADS, HEAD_DIM, tp).transpose(0, 1, 4, 2, 3)
    s_out = [jnp.stack(a, axis=1) for a in s_st]
    return (y_prompt, y_sample, *p_out, *s_out)
```

```python
import functools
import math

import numpy as np
import jax
import jax.numpy as jnp
from jax import lax
from jax.experimental import pallas as pl
from jax.experimental.pallas import tpu as pltpu

F32 = jnp.float32
BF16 = jnp.bfloat16

D_MODEL = 1024
DEPTH = 2
PAST_LEN = 8192
PAGE_SIZE = 128
HEAD_DIM = 64
N_HEADS = 4
GROUP_W = N_HEADS * HEAD_DIM
S5_GROUPS = 16
S5_GROUP = 16
S5_STATE = 64
S5_W = S5_GROUPS * S5_STATE
N_MEM = 256
D_FF = 3584
N_EXPERTS = 8
TOP_K = 2
CHUNK = 64
ROPE_BASE = 10000.0
LN_EPS = 1e-5
GN_EPS = 1e-6
ALPHA = (2 * DEPTH) ** 0.25
QK_SCALE = HEAD_DIM ** -0.5
LOG2E = math.log2(math.e)

LANES = 128
PROJ_W = 25 * LANES
C_SQ, C_SK, C_SV, C_MQ, C_MK, C_MV, C_MO, C_RQ, C_RK, C_RV, C_RG, C_SU = range(12)
C_GATES = 12 * GROUP_W // LANES
VMEM_LIMIT = 48 * 1024 * 1024
MOE_TILE = 1024


def _cp(*sem):
    return pltpu.CompilerParams(dimension_semantics=sem, vmem_limit_bytes=VMEM_LIMIT)


def _dot(a, b):
    return jnp.dot(a, b, preferred_element_type=F32)


def _dot_nt(a, b):
    return lax.dot_general(a, b, (((1,), (1,)), ((), ())), preferred_element_type=F32)


def _dot_tn(a, b):
    return lax.dot_general(a, b, (((0,), (0,)), ((), ())), preferred_element_type=F32)


def _layer_norm(y, g, b):
    mu = jnp.mean(y, axis=-1, keepdims=True)
    yc = y - mu
    var = jnp.mean(yc * yc, axis=-1, keepdims=True)
    return yc * lax.rsqrt(var + LN_EPS) * g + b


def _row_sum(x, scale=1.0):
    ones = jnp.full((x.shape[1], LANES), scale, BF16)
    hi = x.astype(BF16)
    lo = (x - hi.astype(F32)).astype(BF16)
    return (_dot(hi, ones) + _dot(lo, ones))[:, :x.shape[1]]


def _cumsum_rows(tril, x):
    hi = x.astype(BF16)
    lo = (x - hi.astype(F32)).astype(BF16)
    return _dot(tril, hi) + _dot(tril, lo)


def _head_norm_all(hd):
    inv = 1.0 / HEAD_DIM
    mu = {p: _row_sum(x, inv) for p, x in hd.items()}
    hc = {p: hd[p] - mu[p] for p in hd}
    var = {p: _row_sum(hc[p] * hc[p], inv) for p in hd}
    return {p: hc[p] * lax.rsqrt(var[p] + GN_EPS) for p in hd}


def _neg_softplus(z):
    return -(jnp.maximum(z, 0.0) + jnp.log1p(jnp.exp(-jnp.abs(z))))


def _log_sigmoid(z):
    return _neg_softplus(-z)


def _linear_kernel(x_ref, w_ref, o_ref):
    o_ref[...] = _dot(x_ref[...].astype(BF16), w_ref[...]).astype(o_ref.dtype)


def linear(x, w, tm, row_blk0=0, n_rows=None, out_dtype=F32):
    m, k = x.shape
    m = m if n_rows is None else n_rows
    n = w.shape[1]
    return pl.pallas_call(
        _linear_kernel, grid=(m // tm,),
        in_specs=[pl.BlockSpec((tm, k), lambda i: (row_blk0 + i, 0)), pl.BlockSpec((k, n), lambda i: (0, 0))],
        out_specs=pl.BlockSpec((tm, n), lambda i: (i, 0)),
        out_shape=jax.ShapeDtypeStruct((m, n), out_dtype),
        compiler_params=_cp("arbitrary"), name="linear")(x, w)


def _in_proj_kernel(x_ref, w_ref, o_ref, kt_ref, vt_ref):
    o = _dot(x_ref[...].astype(BF16), w_ref[...])
    o_ref[...] = o
    kt_ref[0] = o[:, C_SK * GROUP_W:(C_SK + 1) * GROUP_W].T
    vt_ref[0] = o[:, C_SV * GROUP_W:(C_SV + 1) * GROUP_W].T


def in_proj_prompt(x, w, n_batch, seq, tm):
    k = x.shape[1]
    n = w.shape[1]
    nt = seq // tm
    t_spec = pl.BlockSpec((1, GROUP_W, tm), lambda i: (i // nt, 0, i % nt))
    t_shape = jax.ShapeDtypeStruct((n_batch, GROUP_W, seq), F32)
    return pl.pallas_call(
        _in_proj_kernel, grid=(n_batch * nt,),
        in_specs=[pl.BlockSpec((tm, k), lambda i: (i, 0)), pl.BlockSpec((k, n), lambda i: (0, 0))],
        out_specs=[pl.BlockSpec((tm, n), lambda i: (i, 0)), t_spec, t_spec],
        out_shape=[jax.ShapeDtypeStruct((n_batch * seq, n), F32), t_shape, t_shape],
        compiler_params=_cp("arbitrary"), name="in_proj")(x, w)


def _mix_out_ln_kernel(x_ref, a_ref, b_ref, c_ref, d_ref, w_ref, g_ref, bias_ref, o_ref):
    h = sum(_dot(p[...].astype(BF16), w_ref[pl.ds(n * GROUP_W, GROUP_W), :])
            for n, p in enumerate((a_ref, b_ref, c_ref, d_ref)))
    o_ref[...] = _layer_norm(ALPHA * x_ref[...] + h, g_ref[...], bias_ref[...])


def mix_out_ln(x, parts, w, g, b, tm, row_blk0):
    rows = parts[0].shape[0]
    row_spec = pl.BlockSpec((tm, D_MODEL), lambda i: (row_blk0 + i, 0))
    part_spec = pl.BlockSpec((tm, GROUP_W), lambda i: (i, 0))
    vec = pl.BlockSpec((1, D_MODEL), lambda i: (0, 0))
    return pl.pallas_call(
        _mix_out_ln_kernel, grid=(rows // tm,),
        in_specs=[row_spec, part_spec, part_spec, part_spec, part_spec,
                  pl.BlockSpec((D_MODEL, D_MODEL), lambda i: (0, 0)), vec, vec],
        out_specs=row_spec, out_shape=jax.ShapeDtypeStruct(x.shape, F32), input_output_aliases={0: 0},
        compiler_params=_cp("arbitrary"), name="mix_out_ln")(x, *parts, w, g, b)


def _suffix_matrix():
    j = np.arange(LANES)
    u = (j[:, None] >= j[None, :]).astype(np.float32)
    uu = np.concatenate([u, np.ones((LANES, LANES), np.float32)], axis=1)
    return jnp.asarray(np.concatenate([uu, uu], axis=0), dtype=BF16)


def _suffix_sums(lr, uu):
    hi = lr.astype(BF16)
    lo = (lr - hi.astype(F32)).astype(BF16)
    r = _dot(jnp.concatenate([hi, lo], axis=1), uu)
    return r[:, :LANES], r[:, LANES:]


def _log2_rem(z2):
    return jnp.minimum(-z2, 0.0) - jnp.log2(1.0 + jnp.exp2(-jnp.abs(z2)))


def _sb_prompt_kernel(bias_ref, q_ref, k_ref, v_ref, uu_ref, o_ref, acc_ref, car_ref, *, tq):
    i = pl.program_id(1)
    tk = LANES
    nsub = tq // tk
    acc_ref[...] = jnp.zeros_like(acc_ref)
    car_ref[...] = jnp.zeros_like(car_ref)
    q = (q_ref[...] * (QK_SCALE * LOG2E)).astype(BF16)
    qh = [q[:, h * HEAD_DIM:(h + 1) * HEAD_DIM] for h in range(N_HEADS)]
    b2 = [bias_ref[h] * LOG2E for h in range(N_HEADS)]
    uu = uu_ref[...]
    row = lax.broadcasted_iota(jnp.int32, (tq, tk), 0)
    col = lax.broadcasted_iota(jnp.int32, (tq, tk), 1)
    heads = range(N_HEADS)

    def block_pair(j_hi, causal_hi, causal_lo):
        kb, vb, z2 = [], [], []
        for d in range(2):
            r0 = pl.multiple_of((j_hi - d) * tk, tk)
            kb.append(k_ref[pl.ds(r0, tk), :].astype(BF16))
            vb.append(v_ref[pl.ds(r0, tk), :].astype(BF16))
            z2.append([_dot_nt(qh[h], kb[d][:, h * HEAD_DIM:(h + 1) * HEAD_DIM]) + b2[h] for h in heads])
        cs, tot = [], []
        for d, causal in enumerate((causal_hi, causal_lo)):
            lr = [_log2_rem(z) for z in z2[d]]
            if causal is not None:
                lr = [jnp.where(causal, a, 0.0) for a in lr]
            c, t = _suffix_sums(jnp.concatenate(lr, axis=0), uu)
            cs.append(c)
            tot.append(t)
        car = [car_ref[h] for h in heads]
        pv = []
        for d, causal in enumerate((causal_hi, causal_lo)):
            w = [jnp.exp2(z2[d][h] + cs[d][h * tq:(h + 1) * tq] + car[h]) for h in heads]
            if causal is not None:
                w = [jnp.where(causal, a, 0.0) for a in w]
            pv.append([_dot(w[h].astype(BF16), vb[d][:, h * HEAD_DIM:(h + 1) * HEAD_DIM]) for h in heads])
            car = [car[h] + tot[d][h * tq:(h + 1) * tq] for h in heads]
        for h in heads:
            acc_ref[h] += pv[0][h] + pv[1][h]
            car_ref[h] = car[h]

    assert nsub == 2
    block_pair(i * nsub + 1, (col + tk) < row, col < row)

    def body(jj, carry):
        block_pair(i * nsub - 1 - 2 * jj, None, None)
        return carry

    lax.fori_loop(0, i, body, 0)
    o_ref[...] = jnp.concatenate([acc_ref[h] for h in range(N_HEADS)], axis=1)


def sb_attention_prompt(proj, sb_bias, uu, n_batch, seq, tq):
    nq = seq // tq
    kern = functools.partial(_sb_prompt_kernel, tq=tq)
    return pl.pallas_call(
        kern, grid=(n_batch, nq),
        in_specs=[pl.BlockSpec(memory_space=pltpu.SMEM),
                  pl.BlockSpec((tq, GROUP_W), lambda b, i: (b * nq + i, C_SQ)),
                  pl.BlockSpec((seq, GROUP_W), lambda b, i: (b, C_SK)),
                  pl.BlockSpec((seq, GROUP_W), lambda b, i: (b, C_SV)),
                  pl.BlockSpec((2 * LANES, 2 * LANES), lambda b, i: (0, 0))],
        out_specs=pl.BlockSpec((tq, GROUP_W), lambda b, i: (b * nq + i, 0)),
        out_shape=jax.ShapeDtypeStruct((n_batch * seq, GROUP_W), F32),
        scratch_shapes=[pltpu.VMEM((N_HEADS, tq, HEAD_DIM), F32), pltpu.VMEM((N_HEADS, tq, LANES), F32)],
        compiler_params=_cp("arbitrary", "arbitrary"), name="sb_prompt")(sb_bias, proj, proj, proj, uu)


def _sb_sample_kernel(pt_ref, bias_ref, q_ref, kn_ref, vn_ref, u8_ref, uu_ref, *rest, n_pp, n_steps):
    k_refs = rest[:n_pp]
    v_refs = rest[n_pp:2 * n_pp]
    o_ref = rest[2 * n_pp]
    acc_ref, car_ref = rest[2 * n_pp + 1:]
    s = pl.program_id(1)
    nq = q_ref.shape[0]
    q = (q_ref[...] * (QK_SCALE * LOG2E)).astype(BF16)
    qh = [q[:, h * HEAD_DIM:(h + 1) * HEAD_DIM] for h in range(N_HEADS)]
    b2 = [bias_ref[h] * LOG2E for h in range(N_HEADS)]

    @pl.when(s == 0)
    def _():
        kn = kn_ref[...].astype(BF16)
        vn = vn_ref[...].astype(BF16)
        t = lax.broadcasted_iota(jnp.int32, (nq, nq), 0)
        c = lax.broadcasted_iota(jnp.int32, (nq, nq), 1)
        causal = c < t
        for h in range(N_HEADS):
            sl = slice(h * HEAD_DIM, (h + 1) * HEAD_DIM)
            z2 = _dot_nt(qh[h], kn[:, sl]) + b2[h]
            lr = jnp.where(causal, _log2_rem(z2), 0.0)
            cs = jnp.dot(lr, u8_ref[...], preferred_element_type=F32, precision=lax.Precision.HIGHEST)
            w = jnp.where(causal, jnp.exp2(z2 + cs), 0.0)
            acc_ref[h] = _dot(w.astype(BF16), vn[:, sl])
            car_ref[h] = jnp.broadcast_to(jnp.sum(lr, axis=1, keepdims=True), (nq, LANES))

    uu = uu_ref[...]
    z2s = []
    for p in range(n_pp):
        for h in range(N_HEADS):
            z2s.append(_dot(qh[h], k_refs[p][0, 0, h].astype(BF16)) + b2[h])
    cs_all, tot_all = _suffix_sums(_log2_rem(jnp.concatenate(z2s, axis=0)), uu)
    for h in range(N_HEADS):
        car = car_ref[h]
        acc = acc_ref[h]
        for p in range(n_pp):
            r = (p * N_HEADS + h) * nq
            w = jnp.exp2(z2s[p * N_HEADS + h] + cs_all[r:r + nq] + car)
            acc = acc + _dot_nt(w.astype(BF16), v_refs[p][0, 0, h].astype(BF16))
            car = car + tot_all[r:r + nq]
        car_ref[h] = car
        acc_ref[h] = acc

    @pl.when(s == n_steps - 1)
    def _():
        o_ref[...] = jnp.concatenate([acc_ref[h] for h in range(N_HEADS)], axis=1)


def sb_attention_sample(proj, row_blk0, cache_kt, cache_vt, page_table, layer, sb_bias, uu, n_pp):
    n_batch, n_pages = page_table.shape
    nq = 8
    n_steps = n_pages // n_pp
    u8 = jnp.asarray((np.arange(nq)[:, None] >= np.arange(nq)[None, :]).astype(np.float32))

    def page_spec(p):
        return pl.BlockSpec((1, 1, N_HEADS, HEAD_DIM, PAGE_SIZE),
                            lambda b, s, pt: (pt[b, n_pages - 1 - (s * n_pp + p)], layer, 0, 0, 0))

    def row_spec(cblk):
        return pl.BlockSpec((nq, GROUP_W), lambda b, s, pt: (row_blk0 + b, cblk))

    kern = functools.partial(_sb_sample_kernel, n_pp=n_pp, n_steps=n_steps)
    gs = pltpu.PrefetchScalarGridSpec(
        num_scalar_prefetch=1, grid=(n_batch, n_steps),
        in_specs=[pl.BlockSpec(memory_space=pltpu.SMEM), row_spec(C_SQ), row_spec(C_SK), row_spec(C_SV),
                  pl.BlockSpec((nq, nq), lambda b, s, pt: (0, 0)),
                  pl.BlockSpec((2 * LANES, 2 * LANES), lambda b, s, pt: (0, 0))]
                 + [page_spec(p) for p in range(n_pp)] * 2,
        out_specs=pl.BlockSpec((nq, GROUP_W), lambda b, s, pt: (b, 0)),
        scratch_shapes=[pltpu.VMEM((N_HEADS, nq, HEAD_DIM), F32), pltpu.VMEM((N_HEADS, nq, LANES), F32)])
    return pl.pallas_call(
        kern, grid_spec=gs, out_shape=jax.ShapeDtypeStruct((n_batch * nq, GROUP_W), F32),
        compiler_params=_cp("arbitrary", "arbitrary"), name="sb_sample")(
            page_table, sb_bias, proj, proj, proj, u8, uu, *([cache_kt] * n_pp), *([cache_vt] * n_pp))


def _mlstm_kernel(m0_ref, q_ref, k_ref, v_ref, og_ref, gt_ref, gb_ref, ng_ref, tril_ref, cn0_ref,
                  o_ref, cn_ref, m_ref, ms_ref, *, bb, chunk, n_t):
    bi = pl.program_id(0)
    t = pl.program_id(1)
    seqs = range(bb)
    pairs = [(b, h) for b in seqs for h in range(N_HEADS)]

    @pl.when(t == 0)
    def _():
        cn_ref[...] = cn0_ref[...]
        for b, h in pairs:
            ms_ref[b * N_HEADS + h] = jnp.full((1, LANES), m0_ref[bi * bb + b, h], F32)

    tril = tril_ref[...]
    tri_mask = lax.broadcasted_iota(jnp.int32, (chunk, chunk), 1) <= lax.broadcasted_iota(jnp.int32, (chunk, chunk), 0)
    lane = lax.broadcasted_iota(jnp.int32, (chunk, HEAD_DIM), 1)
    ones_col = jnp.where(lane == 0, 1.0, 0.0).astype(F32)
    hs = lambda h: slice(h * HEAD_DIM, (h + 1) * HEAD_DIM)

    gt = [gt_ref[b] + gb_ref[...] for b in seqs]
    bc = [_cumsum_rows(tril, _log_sigmoid(g)) for g in gt]
    gt_t = [g.T for g in gt]
    bc_t = [x.T for x in bc]
    q = [q_ref[b].astype(BF16) for b in seqs]
    k = [(k_ref[b] * QK_SCALE).astype(BF16) for b in seqs]
    v = [v_ref[b] for b in seqs]
    m_prev = {p: ms_ref[p[0] * N_HEADS + p[1]][:, :1] for p in pairs}
    ig_col = {(b, h): gt[b][:, h:h + 1] for b, h in pairs}
    bc_col = {(b, h): bc[b][:, N_HEADS + h:N_HEADS + h + 1] for b, h in pairs}
    dm = {(b, h): jnp.where(tri_mask, bc_col[b, h] - (bc_t[b][N_HEADS + h:N_HEADS + h + 1, :] - gt_t[b][h:h + 1, :]),
                            -jnp.inf) for b, h in pairs}
    a = {p: bc_col[p] + m_prev[p] for p in pairs}
    m_new = {p: jnp.maximum(a[p], jnp.max(dm[p], axis=1, keepdims=True)) for p in pairs}
    inter = {p: jnp.exp(a[p] - m_new[p]) for p in pairs}
    s = {(b, h): _dot_nt(q[b][:, hs(h)], k[b][:, hs(h)]) * jnp.exp(dm[b, h] - m_new[b, h]) for b, h in pairs}
    v_ext = {(b, h): jnp.concatenate([v[b][:, hs(h)], ones_col], axis=1) for b, h in pairs}
    qc = {(b, h): _dot(q[b][:, hs(h)], cn_ref[b, h].astype(BF16)) for b, h in pairs}
    sv = {p: _dot(s[p].astype(BF16), v_ext[p].astype(BF16)) for p in pairs}
    m_last = {p: m_new[p][chunk - 1:chunk, :] for p in pairs}
    wl = {p: jnp.exp(bc_col[p][chunk - 1:chunk, :] - bc_col[p] + ig_col[p] - m_last[p]) for p in pairs}
    dl = {p: jnp.exp(a[p][chunk - 1:chunk, :] - m_last[p]) for p in pairs}
    upd = {(b, h): _dot_tn(k[b][:, hs(h)], (wl[b, h] * v_ext[b, h]).astype(BF16)) for b, h in pairs}
    rs = {p: _row_sum(s[p])[:, :1] for p in pairs}
    num = {p: inter[p] * qc[p][:, :HEAD_DIM] + sv[p][:, :HEAD_DIM] for p in pairs}
    den = {p: inter[p] * qc[p][:, HEAD_DIM:HEAD_DIM + 1] + rs[p] for p in pairs}
    hh = _head_norm_all({p: num[p] / jnp.maximum(jnp.abs(den[p]), jnp.exp(-m_new[p])) for p in pairs})
    for b, h in pairs:
        cn_ref[b, h] = dl[b, h] * cn_ref[b, h] + upd[b, h]
        ms_ref[b * N_HEADS + h] = jnp.broadcast_to(m_last[b, h], (1, LANES))
    for b in seqs:
        y = jnp.concatenate([hh[b, h] for h in range(N_HEADS)], axis=1)
        o_ref[b] = y * ng_ref[...] * jax.nn.sigmoid(og_ref[b])

    @pl.when(t == n_t - 1)
    def _():
        lane_m = lax.broadcasted_iota(jnp.int32, (1, LANES), 1)
        for b in seqs:
            m_out = jnp.zeros((1, LANES), F32)
            for h in range(N_HEADS):
                m_out = jnp.where(lane_m == h, ms_ref[b * N_HEADS + h], m_out)
            m_ref[b] = m_out


def mlstm_mixer(proj3, gate_bias, norm_g, cn0, m0, bb):
    n_batch, seq, _ = proj3.shape
    chunk = math.gcd(seq, CHUNK)
    n_t = seq // chunk
    tril = jnp.asarray(np.tril(np.ones((chunk, chunk), np.float32)), dtype=BF16)

    def row_spec(cblk, w=GROUP_W):
        return pl.BlockSpec((bb, chunk, w), lambda bi, t: (bi, t, cblk))

    def const_spec(shape):
        return pl.BlockSpec(shape, lambda bi, t: (0,) * len(shape))

    state_spec = pl.BlockSpec((bb, N_HEADS, HEAD_DIM, LANES), lambda bi, t: (bi, 0, 0, 0))
    kern = functools.partial(_mlstm_kernel, bb=bb, chunk=chunk, n_t=n_t)
    return pl.pallas_call(
        kern, grid=(n_batch // bb, n_t),
        in_specs=[pl.BlockSpec(memory_space=pltpu.SMEM),
                  row_spec(C_MQ), row_spec(C_MK), row_spec(C_MV), row_spec(C_MO), row_spec(C_GATES, LANES),
                  const_spec((1, LANES)), const_spec((1, GROUP_W)), const_spec((chunk, chunk)), state_spec],
        out_specs=[pl.BlockSpec((bb, chunk, GROUP_W), lambda bi, t: (bi, t, 0)), state_spec,
                   pl.BlockSpec((bb, 1, LANES), lambda bi, t: (bi, 0, 0))],
        out_shape=[jax.ShapeDtypeStruct((n_batch, seq, GROUP_W), F32),
                   jax.ShapeDtypeStruct((n_batch, N_HEADS, HEAD_DIM, LANES), F32),
                   jax.ShapeDtypeStruct((n_batch, 1, LANES), F32)],
        scratch_shapes=[pltpu.VMEM((bb * N_HEADS, 1, LANES), F32)],
        compiler_params=_cp("arbitrary", "arbitrary"), name="mlstm")(
            m0, proj3, proj3, proj3, proj3, proj3, gate_bias, norm_g, tril, cn0)


def _rope(x, cos, sin_signed):
    lane = lax.broadcasted_iota(jnp.int32, x.shape, 1)
    half = HEAD_DIM // 2
    swapped = jnp.where((lane % HEAD_DIM) < half, pltpu.roll(x, x.shape[1] - half, 1), pltpu.roll(x, half, 1))
    return x * cos + swapped * sin_signed


def _retention_kernel(q_ref, k_ref, v_ref, gg_ref, cos_ref, sin_ref, ng_ref, dec_ref, int_ref, wl_ref, dl_ref,
                      s0_ref, o_ref, s_ref, *, bb):
    t = pl.program_id(1)
    seqs = range(bb)
    pairs = [(b, h) for b in seqs for h in range(N_HEADS)]
    hs = lambda h: slice(h * HEAD_DIM, (h + 1) * HEAD_DIM)

    @pl.when(t == 0)
    def _():
        s_ref[...] = s0_ref[...]

    cos = cos_ref[...]
    sin = sin_ref[...]
    q = [_rope(q_ref[b], cos, sin).astype(BF16) for b in seqs]
    k = [(_rope(k_ref[b], cos, sin) * QK_SCALE).astype(BF16) for b in seqs]
    v = [v_ref[b] for b in seqs]
    s = {(b, h): _dot_nt(q[b][:, hs(h)], k[b][:, hs(h)]) * dec_ref[h] for b, h in pairs}
    qs = {(b, h): _dot(q[b][:, hs(h)], s_ref[b, h].astype(BF16)) for b, h in pairs}
    sv = {(b, h): _dot(s[b, h].astype(BF16), v[b][:, hs(h)].astype(BF16)) for b, h in pairs}
    upd = {(b, h): _dot_tn(k[b][:, hs(h)], (wl_ref[h] * v[b][:, hs(h)]).astype(BF16)) for b, h in pairs}
    o = _head_norm_all({(b, h): int_ref[h] * qs[b, h] + sv[b, h] for b, h in pairs})
    for b, h in pairs:
        s_ref[b, h] = dl_ref[h] * s_ref[b, h] + upd[b, h]
    for b in seqs:
        gg = gg_ref[b]
        o_ref[b] = jnp.concatenate([o[b, h] for h in range(N_HEADS)], axis=1) * ng_ref[...] * (gg * jax.nn.sigmoid(gg))


def _retention_consts(chunk):
    log_g = np.log(1.0 - np.exp2(-5.0 - np.arange(N_HEADS, dtype=np.float64)))
    tau = np.arange(chunk, dtype=np.float64)
    rel = tau[:, None] - tau[None, :]
    decay = np.where(rel >= 0, np.exp(log_g[:, None, None] * np.maximum(rel, 0.0)), 0.0)
    inter = np.exp(log_g[:, None] * (tau + 1.0))[..., None]
    wl = np.exp(log_g[:, None] * (chunk - 1.0 - tau))[..., None]
    dl = np.exp(log_g * chunk)[:, None, None]
    return tuple(jnp.asarray(a, F32) for a in (decay, inter, wl, dl))


def retention_mixer(proj3, cos, sin_signed, norm_g, s0, bb):
    n_batch, seq, _ = proj3.shape
    chunk = math.gcd(seq, CHUNK)
    dec, inter, wl, dl = _retention_consts(chunk)

    def row_spec(cblk):
        return pl.BlockSpec((bb, chunk, GROUP_W), lambda bi, t: (bi, t, cblk))

    def const_spec(shape):
        return pl.BlockSpec(shape, lambda bi, t: (0,) * len(shape))

    pos_spec = pl.BlockSpec((chunk, GROUP_W), lambda bi, t: (t, 0))
    state_spec = pl.BlockSpec((bb, N_HEADS, HEAD_DIM, HEAD_DIM), lambda bi, t: (bi, 0, 0, 0))
    kern = functools.partial(_retention_kernel, bb=bb)
    return pl.pallas_call(
        kern, grid=(n_batch // bb, seq // chunk),
        in_specs=[row_spec(C_RQ), row_spec(C_RK), row_spec(C_RV), row_spec(C_RG), pos_spec, pos_spec,
                  const_spec((1, GROUP_W)), const_spec(dec.shape), const_spec(inter.shape), const_spec(wl.shape),
                  const_spec(dl.shape), state_spec],
        out_specs=[pl.BlockSpec((bb, chunk, GROUP_W), lambda bi, t: (bi, t, 0)), state_spec],
        out_shape=[jax.ShapeDtypeStruct((n_batch, seq, GROUP_W), F32),
                   jax.ShapeDtypeStruct((n_batch, N_HEADS, HEAD_DIM, HEAD_DIM), F32)],
        compiler_params=_cp("arbitrary", "arbitrary"), name="retention")(
            proj3, proj3, proj3, proj3, cos, sin_signed, norm_g, dec, inter, wl, dl, s0)


def _s5_kernel(u_ref, wb_ref, a1_ref, a2_ref, h0_ref, wc_ref, d_ref, gw_ref, gb_ref, o_ref, hl_ref,
               hs_ref, ut_ref, yt_ref, *, nb, tt):
    c = pl.program_id(0)

    @pl.when(c == 0)
    def _():
        hl_ref[...] = h0_ref[...]

    halves = range(GROUP_W // LANES)
    for b in range(nb):
        ub = u_ref[b]
        for hf in halves:
            ut_ref[hf, pl.ds(b, tt, stride=nb), :] = ub[:, hf * LANES:(hf + 1) * LANES]
    u = jnp.concatenate([ut_ref[hf] for hf in halves], axis=1)
    hs_ref[...] = _dot(u.astype(BF16), wb_ref[...])
    a1 = jnp.broadcast_to(a1_ref[...], (nb, 2 * S5_W))
    a2 = jnp.broadcast_to(a2_ref[...], (nb, 2 * S5_W))

    def step(t, h):
        r0 = pl.multiple_of(t * nb, nb)
        swapped = jnp.concatenate([h[:, S5_W:], h[:, :S5_W]], axis=1)
        h = a1 * h + a2 * swapped + hs_ref[pl.ds(r0, nb), :]
        hs_ref[pl.ds(r0, nb), :] = h
        return h

    hl_ref[...] = lax.fori_loop(0, tt, step, hl_ref[...])
    y = _dot(hs_ref[...].astype(BF16), wc_ref[...]) + d_ref[...] * u
    g5 = jax.nn.gelu(y)
    yt = g5 * jax.nn.sigmoid(_dot(g5.astype(BF16), gw_ref[...]) + gb_ref[...])
    for hf in halves:
        yt_ref[hf] = yt[:, hf * LANES:(hf + 1) * LANES]
    for b in range(nb):
        o_ref[b] = jnp.concatenate([yt_ref[hf, pl.ds(b, tt, stride=nb), :] for hf in halves], axis=1)


def s5_mixer(proj3, tt, wb, a1, a2, h0, wc, d, glu_w, glu_b):
    nb, seq, _ = proj3.shape
    rows = tt * nb

    def const_spec(shape):
        return pl.BlockSpec(shape, lambda c: (0,) * len(shape))

    kern = functools.partial(_s5_kernel, nb=nb, tt=tt)
    return pl.pallas_call(
        kern, grid=(seq // tt,),
        in_specs=[pl.BlockSpec((nb, tt, GROUP_W), lambda c: (0, c, C_SU)),
                  const_spec(wb.shape), const_spec(a1.shape), const_spec(a2.shape), const_spec(h0.shape),
                  const_spec(wc.shape), const_spec(d.shape), const_spec(glu_w.shape), const_spec(glu_b.shape)],
        out_specs=[pl.BlockSpec((nb, tt, GROUP_W), lambda c: (0, c, 0)), const_spec(h0.shape)],
        out_shape=[jax.ShapeDtypeStruct((nb, seq, GROUP_W), F32), jax.ShapeDtypeStruct(h0.shape, F32)],
        scratch_shapes=[pltpu.VMEM((rows, 2 * S5_W), F32), pltpu.VMEM((GROUP_W // LANES, rows, LANES), F32),
                        pltpu.VMEM((GROUP_W // LANES, rows, LANES), F32)],
        compiler_params=_cp("arbitrary"), name="s5")(proj3, wb, a1, a2, h0, wc, d, glu_w, glu_b)


def _s5_weights(a_re, a_im, log_dt, b_re, b_im, c_re, c_im):
    lam = lax.complex(a_re, a_im)
    a_bar = jnp.exp(lam * jnp.exp(log_dt))
    b_bar = ((a_bar - 1.0) / lam)[..., None] * lax.complex(b_re, b_im)
    eye = jnp.eye(S5_GROUPS, dtype=F32)

    def in_map(m):
        return jnp.einsum('gpc,gh->gchp', m, eye).reshape(S5_GROUPS * S5_GROUP, S5_W)

    def out_map(m):
        return jnp.einsum('gcp,gh->gphc', m, eye).reshape(S5_W, S5_GROUPS * S5_GROUP)

    wb = jnp.concatenate([in_map(b_bar.real), in_map(b_bar.imag)], axis=1).astype(BF16)
    wc = jnp.concatenate([out_map(c_re), -out_map(c_im)], axis=0).astype(BF16)
    ar = a_bar.real.reshape(1, S5_W)
    ai = a_bar.imag.reshape(1, S5_W)
    return wb, jnp.concatenate([ar, ar], axis=1), jnp.concatenate([-ai, ai], axis=1), wc


def _cross_attn_kernel(x_ref, wq_ref, k_ref, v_ref, wo_ref, g_ref, b_ref, o_ref):
    x = x_ref[...]
    q = (_dot(x.astype(BF16), wq_ref[...]) * QK_SCALE).astype(BF16)
    k = k_ref[0].astype(BF16)
    v = v_ref[0].astype(BF16)
    hs = [slice(h * HEAD_DIM, (h + 1) * HEAD_DIM) for h in range(N_HEADS)]
    s = [_dot_nt(q[:, sl], k[:, sl]) for sl in hs]
    e = [jnp.exp(a - jnp.max(a, axis=1, keepdims=True)) for a in s]
    p = [a / jnp.sum(a, axis=1, keepdims=True) for a in e]
    o = jnp.concatenate([_dot(p[h].astype(BF16), v[:, hs[h]]) for h in range(N_HEADS)], axis=1)
    y = ALPHA * x + _dot(o.astype(BF16), wo_ref[...])
    o_ref[...] = _layer_norm(y, g_ref[...], b_ref[...])


def cross_attn_ln(x, row_blk0, n_batch, seq, tq, mem_k, mem_v, wq, wo, g, b):
    nq = seq // tq

    def const_spec(shape):
        return pl.BlockSpec(shape, lambda bb, i: (0,) * len(shape))

    row_spec = pl.BlockSpec((tq, D_MODEL), lambda bb, i: (row_blk0 + bb * nq + i, 0))
    mem_spec = pl.BlockSpec((1, N_MEM, GROUP_W), lambda bb, i: (bb, 0, 0))
    return pl.pallas_call(
        _cross_attn_kernel, grid=(n_batch, nq),
        in_specs=[row_spec, const_spec(wq.shape), mem_spec, mem_spec, const_spec(wo.shape),
                  const_spec(g.shape), const_spec(b.shape)],
        out_specs=row_spec, out_shape=jax.ShapeDtypeStruct(x.shape, F32), input_output_aliases={0: 0},
        compiler_params=_cp("arbitrary", "arbitrary"), name="cross_attn")(x, wq, mem_k, mem_v, wo, g, b)


SWIGLU_ROWS = 256


def _swiglu_accumulate(xb_ref, wg, wu, wd, acc_ref, n_valid=None):
    wgb, wub, wdb = wg.astype(BF16), wu.astype(BF16), wd.astype(BF16)
    n_sub = xb_ref.shape[0] // SWIGLU_ROWS

    def hidden(r):
        xb = xb_ref[pl.ds(r * SWIGLU_ROWS, SWIGLU_ROWS), :]
        gate = _dot(xb, wgb)
        up = _dot(xb, wub)
        return (gate * jax.nn.sigmoid(gate) * up).astype(BF16)

    def first_sub_blocks(n):
        hid = hidden(0)
        for r in range(n):
            nxt = hidden(r + 1) if r + 1 < n else None
            acc_ref[pl.ds(r * SWIGLU_ROWS, SWIGLU_ROWS), :] += _dot(hid, wdb)
            hid = nxt

    if n_valid is None:
        first_sub_blocks(n_sub)
        return
    need = (n_valid + SWIGLU_ROWS - 1) // SWIGLU_ROWS
    for n in range(1, n_sub + 1):
        pl.when(need == n)(functools.partial(first_sub_blocks, n))


def _ffn_kernel(x_ref, wg_ref, wu_ref, wd_ref, g_ref, b_ref, o_ref, xb_ref, *, nf):
    j = pl.program_id(1)

    @pl.when(j == 0)
    def _():
        xb_ref[...] = x_ref[...].astype(BF16)
        o_ref[...] = jnp.zeros_like(o_ref)

    _swiglu_accumulate(xb_ref, wg_ref[...], wu_ref[...], wd_ref[...], o_ref)

    @pl.when(j == nf - 1)
    def _():
        o_ref[...] = _layer_norm(ALPHA * x_ref[...] + o_ref[...], g_ref[...], b_ref[...])


def ffn_ln(x, wg, wu, wd, g, b, tm, tf):
    m = x.shape[0]
    nf = D_FF // tf
    kern = functools.partial(_ffn_kernel, nf=nf)
    return pl.pallas_call(
        kern, grid=(m // tm, nf),
        in_specs=[pl.BlockSpec((tm, D_MODEL), lambda i, j: (i, 0)),
                  pl.BlockSpec((D_MODEL, tf), lambda i, j: (0, j)), pl.BlockSpec((D_MODEL, tf), lambda i, j: (0, j)),
                  pl.BlockSpec((tf, D_MODEL), lambda i, j: (j, 0)),
                  pl.BlockSpec((1, D_MODEL), lambda i, j: (0, 0)), pl.BlockSpec((1, D_MODEL), lambda i, j: (0, 0))],
        out_specs=pl.BlockSpec((tm, D_MODEL), lambda i, j: (i, 0)),
        out_shape=jax.ShapeDtypeStruct((m, D_MODEL), F32),
        scratch_shapes=[pltpu.VMEM((tm, D_MODEL), BF16)],
        compiler_params=_cp("arbitrary", "arbitrary"), name="ffn")(x, wg, wu, wd, g, b)


def _router_kernel(x_ref, w_ref, b_ref, o_ref):
    logits = jnp.dot(x_ref[...], w_ref[...], preferred_element_type=F32, precision=lax.Precision.HIGHEST) + b_ref[...]
    lane = lax.broadcasted_iota(jnp.int32, logits.shape, 1)
    neg = jnp.float32(-jnp.inf)
    lg = jnp.where(lane < N_EXPERTS, logits, neg)
    m1 = jnp.max(lg, axis=1, keepdims=True)
    i1 = jnp.min(jnp.where(lg == m1, lane, LANES), axis=1, keepdims=True)
    lg2 = jnp.where(lane == i1, neg, lg)
    m2 = jnp.max(lg2, axis=1, keepdims=True)
    i2 = jnp.min(jnp.where(lg2 == m2, lane, LANES), axis=1, keepdims=True)
    e2 = jnp.exp(m2 - m1)
    g1 = 1.0 / (1.0 + e2)
    g2 = e2 / (1.0 + e2)
    out = jnp.where(lane == 0, i1.astype(F32), jnp.where(lane == 1, i2.astype(F32),
                    jnp.where(lane == 2, g1, jnp.where(lane == 3, g2, 0.0))))
    o_ref[...] = out


def router(x, w_pad, b_pad, tm):
    m = x.shape[0]
    return pl.pallas_call(
        _router_kernel, grid=(m // tm,),
        in_specs=[pl.BlockSpec((tm, D_MODEL), lambda i: (i, 0)), pl.BlockSpec((D_MODEL, LANES), lambda i: (0, 0)),
                  pl.BlockSpec((1, LANES), lambda i: (0, 0))],
        out_specs=pl.BlockSpec((tm, LANES), lambda i: (i, 0)),
        out_shape=jax.ShapeDtypeStruct((m, LANES), F32),
        compiler_params=_cp("arbitrary"), name="router")(x, w_pad, b_pad)


def _moe_ffn_kernel(te_ref, nu_ref, tr_ref, x_ref, wg_ref, wu_ref, wd_ref, o_ref, xb_ref):
    i = pl.program_id(0)
    j = pl.program_id(1)
    used = i < nu_ref[0]

    @pl.when(used)
    def _():
        @pl.when(j == 0)
        def _():
            xb_ref[...] = x_ref[...].astype(BF16)
            o_ref[...] = jnp.zeros_like(o_ref)

        _swiglu_accumulate(xb_ref, wg_ref[0], wu_ref[0], wd_ref[0], o_ref, tr_ref[i])

    @pl.when(jnp.logical_and(jnp.logical_not(used), j == 0))
    def _():
        o_ref[...] = jnp.zeros_like(o_ref)


def moe_ffn(x_sorted, tile_expert, n_used, tile_rows, wg, wu, wd, tf):
    n_rows = x_sorted.shape[0]
    n_tiles = n_rows // MOE_TILE
    nf = D_FF // tf

    def jj(i, j, nu):
        return jnp.where(i < nu[0], j, nf - 1)

    gs = pltpu.PrefetchScalarGridSpec(
        num_scalar_prefetch=3, grid=(n_tiles, nf),
        in_specs=[pl.BlockSpec((MOE_TILE, D_MODEL), lambda i, j, te, nu, tr: (i, 0)),
                  pl.BlockSpec((1, D_MODEL, tf), lambda i, j, te, nu, tr: (te[i], 0, jj(i, j, nu))),
                  pl.BlockSpec((1, D_MODEL, tf), lambda i, j, te, nu, tr: (te[i], 0, jj(i, j, nu))),
                  pl.BlockSpec((1, tf, D_MODEL), lambda i, j, te, nu, tr: (te[i], jj(i, j, nu), 0))],
        out_specs=pl.BlockSpec((MOE_TILE, D_MODEL), lambda i, j, te, nu, tr: (i, 0)),
        scratch_shapes=[pltpu.VMEM((MOE_TILE, D_MODEL), BF16)])
    return pl.pallas_call(
        _moe_ffn_kernel, grid_spec=gs, out_shape=jax.ShapeDtypeStruct((n_rows, D_MODEL), F32),
        compiler_params=_cp("arbitrary", "arbitrary"), name="moe_ffn")(
            tile_expert, n_used, tile_rows, x_sorted, wg, wu, wd)


def _combine_ln_kernel(x_ref, r_ref, ya_ref, yb_ref, g_ref, b_ref, o_ref):
    r = r_ref[...]
    y = r[:, TOP_K:TOP_K + 1] * ya_ref[...] + r[:, TOP_K + 1:TOP_K + 2] * yb_ref[...]
    o_ref[...] = _layer_norm(ALPHA * x_ref[...] + y, g_ref[...], b_ref[...])


def combine_ln(x, r, ya, yb, g, b, tm, row_blk0, rows):
    row = pl.BlockSpec((tm, D_MODEL), lambda i: (row_blk0 + i, 0))
    vec = pl.BlockSpec((1, D_MODEL), lambda i: (0, 0))
    return pl.pallas_call(
        _combine_ln_kernel, grid=(rows // tm,),
        in_specs=[row, pl.BlockSpec((tm, LANES), lambda i: (row_blk0 + i, 0)), row, row, vec, vec],
        out_specs=pl.BlockSpec((tm, D_MODEL), lambda i: (i, 0)),
        out_shape=jax.ShapeDtypeStruct((rows, D_MODEL), F32),
        compiler_params=_cp("arbitrary"), name="combine_ln")(x, r, ya, yb, g, b)


def moe_ln(x, router_w, router_b, wg, wu, wd, g, b, tm, splits):
    m = x.shape[0]
    w_pad = jnp.zeros((D_MODEL, LANES), F32).at[:, :N_EXPERTS].set(router_w)
    b_pad = jnp.zeros((1, LANES), F32).at[0, :N_EXPERTS].set(router_b)
    r = router(x, w_pad, b_pad, tm)
    top_idx = r[:, :TOP_K].astype(jnp.int32)
    n_slot = m * TOP_K
    onehot = (top_idx[:, :, None] == jnp.arange(N_EXPERTS, dtype=jnp.int32)).astype(jnp.int32)
    per_tok = onehot[:, 0] + onehot[:, 1]
    before = jnp.cumsum(per_tok, axis=0) - per_tok
    counts = jnp.sum(per_tok, axis=0)
    tiles_per = (counts + MOE_TILE - 1) // MOE_TILE
    tile_end = jnp.cumsum(tiles_per)
    row0 = (tile_end - tiles_per) * MOE_TILE
    dest = jnp.sum(onehot * (before + row0)[:, None, :], axis=2)
    n_tiles = -(-n_slot // MOE_TILE) + N_EXPERTS
    n_rows = n_tiles * MOE_TILE
    row_tok = (jnp.arange(n_rows, dtype=jnp.int32) % m).at[dest.reshape(-1)].set(
        jnp.arange(n_slot, dtype=jnp.int32) // TOP_K)
    n_used = tile_end[-1:].astype(jnp.int32)
    tile_ids = jnp.minimum(jnp.arange(n_tiles, dtype=jnp.int32), n_used[0] - 1)
    tile_expert = jnp.minimum(jnp.sum((tile_end[None, :] <= tile_ids[:, None]).astype(jnp.int32), axis=1),
                              N_EXPERTS - 1)
    tile_rows = jnp.clip(counts[tile_expert] - (tile_ids - (tile_end - tiles_per)[tile_expert]) * MOE_TILE,
                         0, MOE_TILE).astype(jnp.int32)
    x_sorted = x[row_tok]
    y_sorted = moe_ffn(x_sorted, tile_expert, n_used, tile_rows, wg, wu, wd, tf=512)
    ya, yb = y_sorted[dest[:, 0]], y_sorted[dest[:, 1]]
    return [combine_ln(x, r, ya, yb, g, b, t, blk0, rows) for blk0, rows, t in splits]


def kernel(x_prompt, x_sample, cache_sb_k, cache_sb_v, cache_mem_k, cache_mem_v, state_ml_C, state_ml_n, state_ml_m, state_rt_S, state_s5_re, state_s5_im, page_table, mem_prompt, w_in, sb_bias, ml_b_i, ml_b_f, ml_norm_g, rt_norm_g, s5_A_re, s5_A_im, s5_log_dt, s5_B_re, s5_B_im, s5_C_re, s5_C_im, s5_D, s5_glu_w, s5_glu_b, w_out, ca_wq, ca_wk, ca_wv, ca_wo, ln_g, ln_b, ffn_w_gate, ffn_w_up, ffn_w_down, moe_router_w, moe_router_b, moe_w_gate, moe_w_up, moe_w_down):
    bp, tp, _ = x_prompt.shape
    bs, ts, _ = x_sample.shape
    n_p, n_s = bp * tp, bs * ts
    tm = 640
    assert (n_p + n_s) % (2 * tm) == 0 and n_p % n_s == 0 and tp % 512 == 0 and bp % 8 == 0 and bs % 8 == 0
    x = jnp.concatenate([x_prompt.reshape(n_p, D_MODEL), x_sample.reshape(n_s, D_MODEL)], axis=0)
    uu = _suffix_matrix()
    g_off = 7 * GROUP_W
    half = HEAD_DIM // 2
    freq = ROPE_BASE ** (-jnp.arange(half, dtype=F32) / half)

    def rope_tables(pos):
        ang = pos.astype(F32)[:, None] * freq[None, :]
        cos, sin = jnp.cos(ang), jnp.sin(ang)
        return (jnp.tile(jnp.concatenate([cos, cos], axis=1), (1, N_HEADS)),
                jnp.tile(jnp.concatenate([-sin, sin], axis=1), (1, N_HEADS)))

    cos_p, sin_p = rope_tables(jnp.arange(tp, dtype=jnp.int32))
    cos_s, sin_s = rope_tables(PAST_LEN + jnp.arange(ts, dtype=jnp.int32))
    cache_kt = cache_sb_k.transpose(0, 1, 3, 4, 2)
    cache_vt = cache_sb_v.transpose(0, 1, 3, 4, 2)

    p_st = [[] for _ in range(10)]
    s_st = [[] for _ in range(8)]
    for l in range(DEPTH):
        wl = w_in[l]
        w_cat = jnp.concatenate([wl[:, :g_off], wl[:, g_off + 2 * N_HEADS:], wl[:, g_off:g_off + 2 * N_HEADS],
                                 jnp.zeros((D_MODEL, PROJ_W - wl.shape[1]), F32)], axis=1).astype(BF16)
        proj_p, kt_p, vt_p = in_proj_prompt(x, w_cat, bp, tp, 512)
        proj_s = linear(x, w_cat, n_s, n_p // n_s, n_s)
        proj_p3 = proj_p.reshape(bp, tp, PROJ_W)
        proj_s3 = proj_s.reshape(bs, ts, PROJ_W)
        gate_bias = jnp.zeros((1, LANES), F32).at[0, :2 * N_HEADS].set(jnp.concatenate([ml_b_i[l], ml_b_f[l]]))
        ml_g = ml_norm_g[l][None, :]
        rt_g = rt_norm_g[l][None, :]
        wb, a1, a2, wc = _s5_weights(s5_A_re[l], s5_A_im[l], s5_log_dt[l], s5_B_re[l], s5_B_im[l],
                                     s5_C_re[l], s5_C_im[l])
        s5_d = s5_D[l][None, :]
        glu_w = s5_glu_w[l].astype(BF16)
        glu_b = s5_glu_b[l][None, :]

        o_sb_p = sb_attention_prompt(proj_p, sb_bias[l], uu, bp, tp, tq=256)
        o_ml_p, cn_p, m_p = mlstm_mixer(proj_p3, gate_bias, ml_g, jnp.zeros((bp, N_HEADS, HEAD_DIM, LANES), F32),
                                        jnp.zeros((bp, N_HEADS), F32), bb=8)
        o_rt_p, rs_p = retention_mixer(proj_p3, cos_p, sin_p, rt_g,
                                       jnp.zeros((bp, N_HEADS, HEAD_DIM, HEAD_DIM), F32), bb=8)
        o_ml_p = o_ml_p.reshape(n_p, GROUP_W)
        o_rt_p = o_rt_p.reshape(n_p, GROUP_W)
        o_s5_p, h5_p = s5_mixer(proj_p3, 64, wb, a1, a2, jnp.zeros((bp, 2 * S5_W), F32), wc, s5_d, glu_w, glu_b)
        o_s5_p = o_s5_p.reshape(n_p, GROUP_W)

        o_sb_s = sb_attention_sample(proj_s, 0, cache_kt, cache_vt, page_table, l, sb_bias[l], uu, n_pp=16)
        cn0 = jnp.concatenate([state_ml_C[:, l], state_ml_n[:, l][..., None],
                               jnp.zeros((bs, N_HEADS, HEAD_DIM, LANES - HEAD_DIM - 1), F32)], axis=-1)
        o_ml_s, cn_s, m_s = mlstm_mixer(proj_s3, gate_bias, ml_g, cn0, state_ml_m[:, l], bb=8)
        o_rt_s, rs_s = retention_mixer(proj_s3, cos_s, sin_s, rt_g, state_rt_S[:, l], bb=8)
        o_ml_s = o_ml_s.reshape(n_s, GROUP_W)
        o_rt_s = o_rt_s.reshape(n_s, GROUP_W)
        h0_s = jnp.concatenate([state_s5_re[:, l].reshape(bs, S5_W), state_s5_im[:, l].reshape(bs, S5_W)], axis=1)
        o_s5_s, h5_s = s5_mixer(proj_s3, ts, wb, a1, a2, h0_s, wc, s5_d, glu_w, glu_b)
        o_s5_s = o_s5_s.reshape(n_s, GROUP_W)

        wo_mix = w_out[l].astype(BF16)
        g0, b0 = ln_g[l, 0][None, :], ln_b[l, 0][None, :]
        x = mix_out_ln(x, (o_sb_p, o_ml_p, o_rt_p, o_s5_p), wo_mix, g0, b0, 512, 0)
        x = mix_out_ln(x, (o_sb_s, o_ml_s, o_rt_s, o_s5_s), wo_mix, g0, b0, n_s, n_p // n_s)

        mem_kv = linear(mem_prompt.reshape(bp * N_MEM, D_MODEL),
                        jnp.concatenate([ca_wk[l], ca_wv[l]], axis=1).astype(BF16), 512)
        mk_p = mem_kv[:, :GROUP_W].reshape(bp, N_MEM, GROUP_W)
        mv_p = mem_kv[:, GROUP_W:].reshape(bp, N_MEM, GROUP_W)
        wq = ca_wq[l].astype(BF16)
        wo = ca_wo[l].astype(BF16)
        g1, b1 = ln_g[l, 1][None, :], ln_b[l, 1][None, :]
        x = cross_attn_ln(x, 0, bp, tp, 512, mk_p, mv_p, wq, wo, g1, b1)
        x = cross_attn_ln(x, n_p // ts, bs, ts, ts, cache_mem_k[:, l].reshape(bs, N_MEM, GROUP_W),
                          cache_mem_v[:, l].reshape(bs, N_MEM, GROUP_W), wq, wo, g1, b1)

        g2, b2 = ln_g[l, 2][None, :], ln_b[l, 2][None, :]
        j = l // 2
        last = l == DEPTH - 1
        if l % 2 == 0:
            x = ffn_ln(x, ffn_w_gate[j], ffn_w_up[j], ffn_w_down[j], g2, b2, 2 * tm, tf=512)
            y_out = (x[:n_p], x[n_p:]) if last else None
        else:
            splits = [(0, n_p, 512), (n_p // n_s, n_s, n_s)] if last else [(0, n_p + n_s, tm)]
            y_out = moe_ln(x, moe_router_w[j], moe_router_b[j], moe_w_gate[j], moe_w_up[j], moe_w_down[j],
                           g2, b2, tm, splits)
            x = None if last else y_out[0]

        def heads(a, nb_, t_):
            return a.reshape(nb_, t_, N_HEADS, HEAD_DIM)

        p_st[0].append(kt_p)
        p_st[1].append(vt_p)
        p_st[2].append(heads(mk_p, bp, N_MEM))
        p_st[3].append(heads(mv_p, bp, N_MEM))
        s_st[0].append(heads(proj_s[:, C_SK * GROUP_W:(C_SK + 1) * GROUP_W], bs, ts))
        s_st[1].append(heads(proj_s[:, C_SV * GROUP_W:(C_SV + 1) * GROUP_W], bs, ts))
        for st, cn, mm, rs, h5, nb_ in ((p_st, cn_p, m_p, rs_p, h5_p, bp), (s_st, cn_s, m_s, rs_s, h5_s, bs)):
            off = 4 if st is p_st else 2
            st[off + 0].append(cn[..., :HEAD_DIM])
            st[off + 1].append(cn[..., HEAD_DIM])
            st[off + 2].append(mm[:, 0, :N_HEADS])
            st[off + 3].append(rs)
            st[off + 4].append(h5[:, :S5_W].reshape(nb_, S5_GROUPS, S5_STATE))
            st[off + 5].append(h5[:, S5_W:].reshape(nb_, S5_GROUPS, S5_STATE))

    y_prompt = y_out[0].reshape(bp, tp, D_MODEL)
    y_sample = y_out[1].reshape(bs, ts, D_MODEL)
    p_out = [jnp.stack(a, axis=1) for a in p_st]
    for i in range(2):
        p_out[i] = p_out[i].reshape(bp, DEPTH, N_HEADS, HEAD_DIM, tp).transpose(0, 1, 4, 2, 3)
    s_out = [jnp.stack(a, axis=1) for a in s_st]
    return (y_prompt, y_sample, *p_out, *s_out)
```

```python
import functools
import math

import numpy as np
import jax
import jax.numpy as jnp
from jax import lax
from jax.experimental import pallas as pl
from jax.experimental.pallas import tpu as pltpu

F32 = jnp.float32
BF16 = jnp.bfloat16

D_MODEL = 1024
DEPTH = 2
PAST_LEN = 8192
PAGE_SIZE = 128
HEAD_DIM = 64
N_HEADS = 4
GROUP_W = N_HEADS * HEAD_DIM
S5_GROUPS = 16
S5_GROUP = 16
S5_STATE = 64
S5_W = S5_GROUPS * S5_STATE
N_MEM = 256
D_FF = 3584
N_EXPERTS = 8
TOP_K = 2
CHUNK = 64
ROPE_BASE = 10000.0
LN_EPS = 1e-5
GN_EPS = 1e-6
ALPHA = (2 * DEPTH) ** 0.25
QK_SCALE = HEAD_DIM ** -0.5
LOG2E = math.log2(math.e)

LANES = 128
PROJ_W = 25 * LANES
C_SQ, C_SK, C_SV, C_MQ, C_MK, C_MV, C_MO, C_RQ, C_RK, C_RV, C_RG, C_SU = range(12)
C_GATES = 12 * GROUP_W // LANES
VMEM_LIMIT = 48 * 1024 * 1024
MOE_TILE = 2048
MOE_VMEM_LIMIT = 56 * 1024 * 1024


def _cp(*sem, vmem=VMEM_LIMIT):
    return pltpu.CompilerParams(dimension_semantics=sem, vmem_limit_bytes=vmem)


def _dot(a, b):
    return jnp.dot(a, b, preferred_element_type=F32)


def _dot_nt(a, b):
    return lax.dot_general(a, b, (((1,), (1,)), ((), ())), preferred_element_type=F32)


def _dot_tn(a, b):
    return lax.dot_general(a, b, (((0,), (0,)), ((), ())), preferred_element_type=F32)


def _layer_norm(y, g, b):
    mu = jnp.mean(y, axis=-1, keepdims=True)
    yc = y - mu
    var = jnp.mean(yc * yc, axis=-1, keepdims=True)
    return yc * lax.rsqrt(var + LN_EPS) * g + b


def _row_sum(x, scale=1.0):
    ones = jnp.full((x.shape[1], LANES), scale, BF16)
    hi = x.astype(BF16)
    lo = (x - hi.astype(F32)).astype(BF16)
    return (_dot(hi, ones) + _dot(lo, ones))[:, :x.shape[1]]


def _cumsum_rows(tril, x):
    hi = x.astype(BF16)
    lo = (x - hi.astype(F32)).astype(BF16)
    return _dot(tril, hi) + _dot(tril, lo)


def _head_norm_all(hd):
    inv = 1.0 / HEAD_DIM
    mu = {p: _row_sum(x, inv) for p, x in hd.items()}
    hc = {p: hd[p] - mu[p] for p in hd}
    var = {p: _row_sum(hc[p] * hc[p], inv) for p in hd}
    return {p: hc[p] * lax.rsqrt(var[p] + GN_EPS) for p in hd}


def _neg_softplus(z):
    return -(jnp.maximum(z, 0.0) + jnp.log1p(jnp.exp(-jnp.abs(z))))


def _log_sigmoid(z):
    return _neg_softplus(-z)


def _linear_kernel(x_ref, w_ref, o_ref):
    o_ref[...] = _dot(x_ref[...].astype(BF16), w_ref[...]).astype(o_ref.dtype)


def linear(x, w, tm, row_blk0=0, n_rows=None, out_dtype=F32):
    m, k = x.shape
    m = m if n_rows is None else n_rows
    n = w.shape[1]
    return pl.pallas_call(
        _linear_kernel, grid=(m // tm,),
        in_specs=[pl.BlockSpec((tm, k), lambda i: (row_blk0 + i, 0)), pl.BlockSpec((k, n), lambda i: (0, 0))],
        out_specs=pl.BlockSpec((tm, n), lambda i: (i, 0)),
        out_shape=jax.ShapeDtypeStruct((m, n), out_dtype),
        compiler_params=_cp("arbitrary"), name="linear")(x, w)


def _in_proj_kernel(x_ref, w_ref, o_ref, kt_ref, vt_ref):
    o = _dot(x_ref[...].astype(BF16), w_ref[...])
    o_ref[...] = o
    kt_ref[0] = o[:, C_SK * GROUP_W:(C_SK + 1) * GROUP_W].T
    vt_ref[0] = o[:, C_SV * GROUP_W:(C_SV + 1) * GROUP_W].T


def in_proj_prompt(x, w, n_batch, seq, tm):
    k = x.shape[1]
    n = w.shape[1]
    nt = seq // tm
    t_spec = pl.BlockSpec((1, GROUP_W, tm), lambda i: (i // nt, 0, i % nt))
    t_shape = jax.ShapeDtypeStruct((n_batch, GROUP_W, seq), F32)
    return pl.pallas_call(
        _in_proj_kernel, grid=(n_batch * nt,),
        in_specs=[pl.BlockSpec((tm, k), lambda i: (i, 0)), pl.BlockSpec((k, n), lambda i: (0, 0))],
        out_specs=[pl.BlockSpec((tm, n), lambda i: (i, 0)), t_spec, t_spec],
        out_shape=[jax.ShapeDtypeStruct((n_batch * seq, n), F32), t_shape, t_shape],
        compiler_params=_cp("arbitrary"), name="in_proj")(x, w)


def _mix_out_ln_kernel(x_ref, a_ref, b_ref, c_ref, d_ref, w_ref, g_ref, bias_ref, o_ref):
    h = sum(_dot(p[...].astype(BF16), w_ref[pl.ds(n * GROUP_W, GROUP_W), :])
            for n, p in enumerate((a_ref, b_ref, c_ref, d_ref)))
    o_ref[...] = _layer_norm(ALPHA * x_ref[...] + h, g_ref[...], bias_ref[...])


def mix_out_ln(x, parts, w, g, b, tm, row_blk0):
    rows = parts[0].shape[0]
    row_spec = pl.BlockSpec((tm, D_MODEL), lambda i: (row_blk0 + i, 0))
    part_spec = pl.BlockSpec((tm, GROUP_W), lambda i: (i, 0))
    vec = pl.BlockSpec((1, D_MODEL), lambda i: (0, 0))
    return pl.pallas_call(
        _mix_out_ln_kernel, grid=(rows // tm,),
        in_specs=[row_spec, part_spec, part_spec, part_spec, part_spec,
                  pl.BlockSpec((D_MODEL, D_MODEL), lambda i: (0, 0)), vec, vec],
        out_specs=row_spec, out_shape=jax.ShapeDtypeStruct(x.shape, F32), input_output_aliases={0: 0},
        compiler_params=_cp("arbitrary"), name="mix_out_ln")(x, *parts, w, g, b)


def _suffix_matrix():
    j = np.arange(LANES)
    u = (j[:, None] >= j[None, :]).astype(np.float32)
    uu = np.concatenate([u, np.ones((LANES, LANES), np.float32)], axis=1)
    return jnp.asarray(np.concatenate([uu, uu], axis=0), dtype=BF16)


def _suffix_sums(lr, uu):
    hi = lr.astype(BF16)
    lo = (lr - hi.astype(F32)).astype(BF16)
    r = _dot(jnp.concatenate([hi, lo], axis=1), uu)
    return r[:, :LANES], r[:, LANES:]


def _log2_rem(z2):
    return jnp.minimum(-z2, 0.0) - jnp.log2(1.0 + jnp.exp2(-jnp.abs(z2)))


def _sb_prompt_kernel(bias_ref, q_ref, k_ref, v_ref, uu_ref, o_ref, acc_ref, car_ref, *, tq):
    i = pl.program_id(1)
    tk = LANES
    nsub = tq // tk
    acc_ref[...] = jnp.zeros_like(acc_ref)
    car_ref[...] = jnp.zeros_like(car_ref)
    q = (q_ref[...] * (QK_SCALE * LOG2E)).astype(BF16)
    qh = [q[:, h * HEAD_DIM:(h + 1) * HEAD_DIM] for h in range(N_HEADS)]
    b2 = [bias_ref[h] * LOG2E for h in range(N_HEADS)]
    uu = uu_ref[...]
    row = lax.broadcasted_iota(jnp.int32, (tq, tk), 0)
    col = lax.broadcasted_iota(jnp.int32, (tq, tk), 1)
    heads = range(N_HEADS)

    def block_pair(j_hi, causal_hi, causal_lo):
        kb, vb, z2 = [], [], []
        for d in range(2):
            r0 = pl.multiple_of((j_hi - d) * tk, tk)
            kb.append(k_ref[pl.ds(r0, tk), :].astype(BF16))
            vb.append(v_ref[pl.ds(r0, tk), :].astype(BF16))
            z2.append([_dot_nt(qh[h], kb[d][:, h * HEAD_DIM:(h + 1) * HEAD_DIM]) + b2[h] for h in heads])
        cs, tot = [], []
        for d, causal in enumerate((causal_hi, causal_lo)):
            lr = [_log2_rem(z) for z in z2[d]]
            if causal is not None:
                lr = [jnp.where(causal, a, 0.0) for a in lr]
            c, t = _suffix_sums(jnp.concatenate(lr, axis=0), uu)
            cs.append(c)
            tot.append(t)
        car = [car_ref[h] for h in heads]
        pv = []
        for d, causal in enumerate((causal_hi, causal_lo)):
            w = [jnp.exp2(z2[d][h] + cs[d][h * tq:(h + 1) * tq] + car[h]) for h in heads]
            if causal is not None:
                w = [jnp.where(causal, a, 0.0) for a in w]
            pv.append([_dot(w[h].astype(BF16), vb[d][:, h * HEAD_DIM:(h + 1) * HEAD_DIM]) for h in heads])
            car = [car[h] + tot[d][h * tq:(h + 1) * tq] for h in heads]
        for h in heads:
            acc_ref[h] += pv[0][h] + pv[1][h]
            car_ref[h] = car[h]

    assert nsub == 2
    block_pair(i * nsub + 1, (col + tk) < row, col < row)

    def body(jj, carry):
        block_pair(i * nsub - 1 - 2 * jj, None, None)
        return carry

    lax.fori_loop(0, i, body, 0)
    o_ref[...] = jnp.concatenate([acc_ref[h] for h in range(N_HEADS)], axis=1)


def sb_attention_prompt(proj, sb_bias, uu, n_batch, seq, tq):
    nq = seq // tq
    kern = functools.partial(_sb_prompt_kernel, tq=tq)
    return pl.pallas_call(
        kern, grid=(n_batch, nq),
        in_specs=[pl.BlockSpec(memory_space=pltpu.SMEM),
                  pl.BlockSpec((tq, GROUP_W), lambda b, i: (b * nq + i, C_SQ)),
                  pl.BlockSpec((seq, GROUP_W), lambda b, i: (b, C_SK)),
                  pl.BlockSpec((seq, GROUP_W), lambda b, i: (b, C_SV)),
                  pl.BlockSpec((2 * LANES, 2 * LANES), lambda b, i: (0, 0))],
        out_specs=pl.BlockSpec((tq, GROUP_W), lambda b, i: (b * nq + i, 0)),
        out_shape=jax.ShapeDtypeStruct((n_batch * seq, GROUP_W), F32),
        scratch_shapes=[pltpu.VMEM((N_HEADS, tq, HEAD_DIM), F32), pltpu.VMEM((N_HEADS, tq, LANES), F32)],
        compiler_params=_cp("arbitrary", "arbitrary"), name="sb_prompt")(sb_bias, proj, proj, proj, uu)


def _sb_sample_kernel(pt_ref, bias_ref, q_ref, kn_ref, vn_ref, u8_ref, uu_ref, *rest, n_pp, n_steps):
    k_refs = rest[:n_pp]
    v_refs = rest[n_pp:2 * n_pp]
    o_ref = rest[2 * n_pp]
    acc_ref, car_ref = rest[2 * n_pp + 1:]
    s = pl.program_id(1)
    nq = q_ref.shape[0]
    q = (q_ref[...] * (QK_SCALE * LOG2E)).astype(BF16)
    qh = [q[:, h * HEAD_DIM:(h + 1) * HEAD_DIM] for h in range(N_HEADS)]
    b2 = [bias_ref[h] * LOG2E for h in range(N_HEADS)]

    @pl.when(s == 0)
    def _():
        kn = kn_ref[...].astype(BF16)
        vn = vn_ref[...].astype(BF16)
        t = lax.broadcasted_iota(jnp.int32, (nq, nq), 0)
        c = lax.broadcasted_iota(jnp.int32, (nq, nq), 1)
        causal = c < t
        for h in range(N_HEADS):
            sl = slice(h * HEAD_DIM, (h + 1) * HEAD_DIM)
            z2 = _dot_nt(qh[h], kn[:, sl]) + b2[h]
            lr = jnp.where(causal, _log2_rem(z2), 0.0)
            cs = jnp.dot(lr, u8_ref[...], preferred_element_type=F32, precision=lax.Precision.HIGHEST)
            w = jnp.where(causal, jnp.exp2(z2 + cs), 0.0)
            acc_ref[h] = _dot(w.astype(BF16), vn[:, sl])
            car_ref[h] = jnp.broadcast_to(jnp.sum(lr, axis=1, keepdims=True), (nq, LANES))

    uu = uu_ref[...]
    z2s = []
    for p in range(n_pp):
        for h in range(N_HEADS):
            z2s.append(_dot(qh[h], k_refs[p][0, 0, h].astype(BF16)) + b2[h])
    cs_all, tot_all = _suffix_sums(_log2_rem(jnp.concatenate(z2s, axis=0)), uu)
    for h in range(N_HEADS):
        car = car_ref[h]
        acc = acc_ref[h]
        for p in range(n_pp):
            r = (p * N_HEADS + h) * nq
            w = jnp.exp2(z2s[p * N_HEADS + h] + cs_all[r:r + nq] + car)
            acc = acc + _dot_nt(w.astype(BF16), v_refs[p][0, 0, h].astype(BF16))
            car = car + tot_all[r:r + nq]
        car_ref[h] = car
        acc_ref[h] = acc

    @pl.when(s == n_steps - 1)
    def _():
        o_ref[...] = jnp.concatenate([acc_ref[h] for h in range(N_HEADS)], axis=1)


def sb_attention_sample(proj, row_blk0, cache_kt, cache_vt, page_table, layer, sb_bias, uu, n_pp):
    n_batch, n_pages = page_table.shape
    nq = 8
    n_steps = n_pages // n_pp
    u8 = jnp.asarray((np.arange(nq)[:, None] >= np.arange(nq)[None, :]).astype(np.float32))

    def page_spec(p):
        return pl.BlockSpec((1, 1, N_HEADS, HEAD_DIM, PAGE_SIZE),
                            lambda b, s, pt: (pt[b, n_pages - 1 - (s * n_pp + p)], layer, 0, 0, 0))

    def row_spec(cblk):
        return pl.BlockSpec((nq, GROUP_W), lambda b, s, pt: (row_blk0 + b, cblk))

    kern = functools.partial(_sb_sample_kernel, n_pp=n_pp, n_steps=n_steps)
    gs = pltpu.PrefetchScalarGridSpec(
        num_scalar_prefetch=1, grid=(n_batch, n_steps),
        in_specs=[pl.BlockSpec(memory_space=pltpu.SMEM), row_spec(C_SQ), row_spec(C_SK), row_spec(C_SV),
                  pl.BlockSpec((nq, nq), lambda b, s, pt: (0, 0)),
                  pl.BlockSpec((2 * LANES, 2 * LANES), lambda b, s, pt: (0, 0))]
                 + [page_spec(p) for p in range(n_pp)] * 2,
        out_specs=pl.BlockSpec((nq, GROUP_W), lambda b, s, pt: (b, 0)),
        scratch_shapes=[pltpu.VMEM((N_HEADS, nq, HEAD_DIM), F32), pltpu.VMEM((N_HEADS, nq, LANES), F32)])
    return pl.pallas_call(
        kern, grid_spec=gs, out_shape=jax.ShapeDtypeStruct((n_batch * nq, GROUP_W), F32),
        compiler_params=_cp("arbitrary", "arbitrary"), name="sb_sample")(
            page_table, sb_bias, proj, proj, proj, u8, uu, *([cache_kt] * n_pp), *([cache_vt] * n_pp))


def _mlstm_kernel(m0_ref, q_ref, k_ref, v_ref, og_ref, gt_ref, gb_ref, ng_ref, tril_ref, cn0_ref,
                  o_ref, cn_ref, m_ref, ms_ref, *, bb, chunk, n_t):
    bi = pl.program_id(0)
    t = pl.program_id(1)
    seqs = range(bb)
    pairs = [(b, h) for b in seqs for h in range(N_HEADS)]

    @pl.when(t == 0)
    def _():
        cn_ref[...] = cn0_ref[...]
        for b, h in pairs:
            ms_ref[b * N_HEADS + h] = jnp.full((1, LANES), m0_ref[bi * bb + b, h], F32)

    tril = tril_ref[...]
    tri_mask = lax.broadcasted_iota(jnp.int32, (chunk, chunk), 1) <= lax.broadcasted_iota(jnp.int32, (chunk, chunk), 0)
    lane = lax.broadcasted_iota(jnp.int32, (chunk, HEAD_DIM), 1)
    ones_col = jnp.where(lane == 0, 1.0, 0.0).astype(F32)
    hs = lambda h: slice(h * HEAD_DIM, (h + 1) * HEAD_DIM)

    gt = [gt_ref[b] + gb_ref[...] for b in seqs]
    bc = [_cumsum_rows(tril, _log_sigmoid(g)) for g in gt]
    gt_t = [g.T for g in gt]
    bc_t = [x.T for x in bc]
    q = [q_ref[b].astype(BF16) for b in seqs]
    k = [(k_ref[b] * QK_SCALE).astype(BF16) for b in seqs]
    v = [v_ref[b] for b in seqs]
    m_prev = {p: ms_ref[p[0] * N_HEADS + p[1]][:, :1] for p in pairs}
    ig_col = {(b, h): gt[b][:, h:h + 1] for b, h in pairs}
    bc_col = {(b, h): bc[b][:, N_HEADS + h:N_HEADS + h + 1] for b, h in pairs}
    dm = {(b, h): jnp.where(tri_mask, bc_col[b, h] - (bc_t[b][N_HEADS + h:N_HEADS + h + 1, :] - gt_t[b][h:h + 1, :]),
                            -jnp.inf) for b, h in pairs}
    a = {p: bc_col[p] + m_prev[p] for p in pairs}
    m_new = {p: jnp.maximum(a[p], jnp.max(dm[p], axis=1, keepdims=True)) for p in pairs}
    inter = {p: jnp.exp(a[p] - m_new[p]) for p in pairs}
    s = {(b, h): _dot_nt(q[b][:, hs(h)], k[b][:, hs(h)]) * jnp.exp(dm[b, h] - m_new[b, h]) for b, h in pairs}
    v_ext = {(b, h): jnp.concatenate([v[b][:, hs(h)], ones_col], axis=1) for b, h in pairs}
    qc = {(b, h): _dot(q[b][:, hs(h)], cn_ref[b, h].astype(BF16)) for b, h in pairs}
    sv = {p: _dot(s[p].astype(BF16), v_ext[p].astype(BF16)) for p in pairs}
    m_last = {p: m_new[p][chunk - 1:chunk, :] for p in pairs}
    wl = {p: jnp.exp(bc_col[p][chunk - 1:chunk, :] - bc_col[p] + ig_col[p] - m_last[p]) for p in pairs}
    dl = {p: jnp.exp(a[p][chunk - 1:chunk, :] - m_last[p]) for p in pairs}
    upd = {(b, h): _dot_tn(k[b][:, hs(h)], (wl[b, h] * v_ext[b, h]).astype(BF16)) for b, h in pairs}
    rs = {p: _row_sum(s[p])[:, :1] for p in pairs}
    num = {p: inter[p] * qc[p][:, :HEAD_DIM] + sv[p][:, :HEAD_DIM] for p in pairs}
    den = {p: inter[p] * qc[p][:, HEAD_DIM:HEAD_DIM + 1] + rs[p] for p in pairs}
    hh = _head_norm_all({p: num[p] / jnp.maximum(jnp.abs(den[p]), jnp.exp(-m_new[p])) for p in pairs})
    for b, h in pairs:
        cn_ref[b, h] = dl[b, h] * cn_ref[b, h] + upd[b, h]
        ms_ref[b * N_HEADS + h] = jnp.broadcast_to(m_last[b, h], (1, LANES))
    for b in seqs:
        y = jnp.concatenate([hh[b, h] for h in range(N_HEADS)], axis=1)
        o_ref[b] = y * ng_ref[...] * jax.nn.sigmoid(og_ref[b])

    @pl.when(t == n_t - 1)
    def _():
        lane_m = lax.broadcasted_iota(jnp.int32, (1, LANES), 1)
        for b in seqs:
            m_out = jnp.zeros((1, LANES), F32)
            for h in range(N_HEADS):
                m_out = jnp.where(lane_m == h, ms_ref[b * N_HEADS + h], m_out)
            m_ref[b] = m_out


def mlstm_mixer(proj3, gate_bias, norm_g, cn0, m0, bb):
    n_batch, seq, _ = proj3.shape
    chunk = math.gcd(seq, CHUNK)
    n_t = seq // chunk
    tril = jnp.asarray(np.tril(np.ones((chunk, chunk), np.float32)), dtype=BF16)

    def row_spec(cblk, w=GROUP_W):
        return pl.BlockSpec((bb, chunk, w), lambda bi, t: (bi, t, cblk))

    def const_spec(shape):
        return pl.BlockSpec(shape, lambda bi, t: (0,) * len(shape))

    state_spec = pl.BlockSpec((bb, N_HEADS, HEAD_DIM, LANES), lambda bi, t: (bi, 0, 0, 0))
    kern = functools.partial(_mlstm_kernel, bb=bb, chunk=chunk, n_t=n_t)
    return pl.pallas_call(
        kern, grid=(n_batch // bb, n_t),
        in_specs=[pl.BlockSpec(memory_space=pltpu.SMEM),
                  row_spec(C_MQ), row_spec(C_MK), row_spec(C_MV), row_spec(C_MO), row_spec(C_GATES, LANES),
                  const_spec((1, LANES)), const_spec((1, GROUP_W)), const_spec((chunk, chunk)), state_spec],
        out_specs=[pl.BlockSpec((bb, chunk, GROUP_W), lambda bi, t: (bi, t, 0)), state_spec,
                   pl.BlockSpec((bb, 1, LANES), lambda bi, t: (bi, 0, 0))],
        out_shape=[jax.ShapeDtypeStruct((n_batch, seq, GROUP_W), F32),
                   jax.ShapeDtypeStruct((n_batch, N_HEADS, HEAD_DIM, LANES), F32),
                   jax.ShapeDtypeStruct((n_batch, 1, LANES), F32)],
        scratch_shapes=[pltpu.VMEM((bb * N_HEADS, 1, LANES), F32)],
        compiler_params=_cp("arbitrary", "arbitrary"), name="mlstm")(
            m0, proj3, proj3, proj3, proj3, proj3, gate_bias, norm_g, tril, cn0)


def _rope(x, cos, sin_signed):
    lane = lax.broadcasted_iota(jnp.int32, x.shape, 1)
    half = HEAD_DIM // 2
    swapped = jnp.where((lane % HEAD_DIM) < half, pltpu.roll(x, x.shape[1] - half, 1), pltpu.roll(x, half, 1))
    return x * cos + swapped * sin_signed


def _retention_kernel(q_ref, k_ref, v_ref, gg_ref, cos_ref, sin_ref, ng_ref, dec_ref, int_ref, wl_ref, dl_ref,
                      s0_ref, o_ref, s_ref, *, bb):
    t = pl.program_id(1)
    seqs = range(bb)
    pairs = [(b, h) for b in seqs for h in range(N_HEADS)]
    hs = lambda h: slice(h * HEAD_DIM, (h + 1) * HEAD_DIM)

    @pl.when(t == 0)
    def _():
        s_ref[...] = s0_ref[...]

    cos = cos_ref[...]
    sin = sin_ref[...]
    q = [_rope(q_ref[b], cos, sin).astype(BF16) for b in seqs]
    k = [(_rope(k_ref[b], cos, sin) * QK_SCALE).astype(BF16) for b in seqs]
    v = [v_ref[b] for b in seqs]
    s = {(b, h): _dot_nt(q[b][:, hs(h)], k[b][:, hs(h)]) * dec_ref[h] for b, h in pairs}
    qs = {(b, h): _dot(q[b][:, hs(h)], s_ref[b, h].astype(BF16)) for b, h in pairs}
    sv = {(b, h): _dot(s[b, h].astype(BF16), v[b][:, hs(h)].astype(BF16)) for b, h in pairs}
    upd = {(b, h): _dot_tn(k[b][:, hs(h)], (wl_ref[h] * v[b][:, hs(h)]).astype(BF16)) for b, h in pairs}
    o = _head_norm_all({(b, h): int_ref[h] * qs[b, h] + sv[b, h] for b, h in pairs})
    for b, h in pairs:
        s_ref[b, h] = dl_ref[h] * s_ref[b, h] + upd[b, h]
    for b in seqs:
        gg = gg_ref[b]
        o_ref[b] = jnp.concatenate([o[b, h] for h in range(N_HEADS)], axis=1) * ng_ref[...] * (gg * jax.nn.sigmoid(gg))


def _retention_consts(chunk):
    log_g = np.log(1.0 - np.exp2(-5.0 - np.arange(N_HEADS, dtype=np.float64)))
    tau = np.arange(chunk, dtype=np.float64)
    rel = tau[:, None] - tau[None, :]
    decay = np.where(rel >= 0, np.exp(log_g[:, None, None] * np.maximum(rel, 0.0)), 0.0)
    inter = np.exp(log_g[:, None] * (tau + 1.0))[..., None]
    wl = np.exp(log_g[:, None] * (chunk - 1.0 - tau))[..., None]
    dl = np.exp(log_g * chunk)[:, None, None]
    return tuple(jnp.asarray(a, F32) for a in (decay, inter, wl, dl))


def retention_mixer(proj3, cos, sin_signed, norm_g, s0, bb):
    n_batch, seq, _ = proj3.shape
    chunk = math.gcd(seq, CHUNK)
    dec, inter, wl, dl = _retention_consts(chunk)

    def row_spec(cblk):
        return pl.BlockSpec((bb, chunk, GROUP_W), lambda bi, t: (bi, t, cblk))

    def const_spec(shape):
        return pl.BlockSpec(shape, lambda bi, t: (0,) * len(shape))

    pos_spec = pl.BlockSpec((chunk, GROUP_W), lambda bi, t: (t, 0))
    state_spec = pl.BlockSpec((bb, N_HEADS, HEAD_DIM, HEAD_DIM), lambda bi, t: (bi, 0, 0, 0))
    kern = functools.partial(_retention_kernel, bb=bb)
    return pl.pallas_call(
        kern, grid=(n_batch // bb, seq // chunk),
        in_specs=[row_spec(C_RQ), row_spec(C_RK), row_spec(C_RV), row_spec(C_RG), pos_spec, pos_spec,
                  const_spec((1, GROUP_W)), const_spec(dec.shape), const_spec(inter.shape), const_spec(wl.shape),
                  const_spec(dl.shape), state_spec],
        out_specs=[pl.BlockSpec((bb, chunk, GROUP_W), lambda bi, t: (bi, t, 0)), state_spec],
        out_shape=[jax.ShapeDtypeStruct((n_batch, seq, GROUP_W), F32),
                   jax.ShapeDtypeStruct((n_batch, N_HEADS, HEAD_DIM, HEAD_DIM), F32)],
        compiler_params=_cp("arbitrary", "arbitrary"), name="retention")(
            proj3, proj3, proj3, proj3, cos, sin_signed, norm_g, dec, inter, wl, dl, s0)


def _s5_kernel(u_ref, wb_ref, a1_ref, a2_ref, h0_ref, wc_ref, d_ref, gw_ref, gb_ref, o_ref, hl_ref,
               hs_ref, ut_ref, yt_ref, *, nb, tt):
    c = pl.program_id(0)

    @pl.when(c == 0)
    def _():
        hl_ref[...] = h0_ref[...]

    halves = range(GROUP_W // LANES)
    for b in range(nb):
        ub = u_ref[b]
        for hf in halves:
            ut_ref[hf, pl.ds(b, tt, stride=nb), :] = ub[:, hf * LANES:(hf + 1) * LANES]
    u = jnp.concatenate([ut_ref[hf] for hf in halves], axis=1)
    hs_ref[...] = _dot(u.astype(BF16), wb_ref[...])
    a1 = jnp.broadcast_to(a1_ref[...], (nb, 2 * S5_W))
    a2 = jnp.broadcast_to(a2_ref[...], (nb, 2 * S5_W))

    def step(t, h):
        r0 = pl.multiple_of(t * nb, nb)
        swapped = jnp.concatenate([h[:, S5_W:], h[:, :S5_W]], axis=1)
        h = a1 * h + a2 * swapped + hs_ref[pl.ds(r0, nb), :]
        hs_ref[pl.ds(r0, nb), :] = h
        return h

    hl_ref[...] = lax.fori_loop(0, tt, step, hl_ref[...])
    y = _dot(hs_ref[...].astype(BF16), wc_ref[...]) + d_ref[...] * u
    g5 = jax.nn.gelu(y)
    yt = g5 * jax.nn.sigmoid(_dot(g5.astype(BF16), gw_ref[...]) + gb_ref[...])
    for hf in halves:
        yt_ref[hf] = yt[:, hf * LANES:(hf + 1) * LANES]
    for b in range(nb):
        o_ref[b] = jnp.concatenate([yt_ref[hf, pl.ds(b, tt, stride=nb), :] for hf in halves], axis=1)


def s5_mixer(proj3, tt, wb, a1, a2, h0, wc, d, glu_w, glu_b):
    nb, seq, _ = proj3.shape
    rows = tt * nb

    def const_spec(shape):
        return pl.BlockSpec(shape, lambda c: (0,) * len(shape))

    kern = functools.partial(_s5_kernel, nb=nb, tt=tt)
    return pl.pallas_call(
        kern, grid=(seq // tt,),
        in_specs=[pl.BlockSpec((nb, tt, GROUP_W), lambda c: (0, c, C_SU)),
                  const_spec(wb.shape), const_spec(a1.shape), const_spec(a2.shape), const_spec(h0.shape),
                  const_spec(wc.shape), const_spec(d.shape), const_spec(glu_w.shape), const_spec(glu_b.shape)],
        out_specs=[pl.BlockSpec((nb, tt, GROUP_W), lambda c: (0, c, 0)), const_spec(h0.shape)],
        out_shape=[jax.ShapeDtypeStruct((nb, seq, GROUP_W), F32), jax.ShapeDtypeStruct(h0.shape, F32)],
        scratch_shapes=[pltpu.VMEM((rows, 2 * S5_W), F32), pltpu.VMEM((GROUP_W // LANES, rows, LANES), F32),
                        pltpu.VMEM((GROUP_W // LANES, rows, LANES), F32)],
        compiler_params=_cp("arbitrary"), name="s5")(proj3, wb, a1, a2, h0, wc, d, glu_w, glu_b)


def _s5_weights(a_re, a_im, log_dt, b_re, b_im, c_re, c_im):
    lam = lax.complex(a_re, a_im)
    a_bar = jnp.exp(lam * jnp.exp(log_dt))
    b_bar = ((a_bar - 1.0) / lam)[..., None] * lax.complex(b_re, b_im)
    eye = jnp.eye(S5_GROUPS, dtype=F32)

    def in_map(m):
        return jnp.einsum('gpc,gh->gchp', m, eye).reshape(S5_GROUPS * S5_GROUP, S5_W)

    def out_map(m):
        return jnp.einsum('gcp,gh->gphc', m, eye).reshape(S5_W, S5_GROUPS * S5_GROUP)

    wb = jnp.concatenate([in_map(b_bar.real), in_map(b_bar.imag)], axis=1).astype(BF16)
    wc = jnp.concatenate([out_map(c_re), -out_map(c_im)], axis=0).astype(BF16)
    ar = a_bar.real.reshape(1, S5_W)
    ai = a_bar.imag.reshape(1, S5_W)
    return wb, jnp.concatenate([ar, ar], axis=1), jnp.concatenate([-ai, ai], axis=1), wc


def _cross_attn_kernel(x_ref, wq_ref, k_ref, v_ref, wo_ref, g_ref, b_ref, o_ref):
    x = x_ref[...]
    q = (_dot(x.astype(BF16), wq_ref[...]) * QK_SCALE).astype(BF16)
    k = k_ref[0].astype(BF16)
    v = v_ref[0].astype(BF16)
    hs = [slice(h * HEAD_DIM, (h + 1) * HEAD_DIM) for h in range(N_HEADS)]
    s = [_dot_nt(q[:, sl], k[:, sl]) for sl in hs]
    e = [jnp.exp(a - jnp.max(a, axis=1, keepdims=True)) for a in s]
    p = [a / jnp.sum(a, axis=1, keepdims=True) for a in e]
    o = jnp.concatenate([_dot(p[h].astype(BF16), v[:, hs[h]]) for h in range(N_HEADS)], axis=1)
    y = ALPHA * x + _dot(o.astype(BF16), wo_ref[...])
    o_ref[...] = _layer_norm(y, g_ref[...], b_ref[...])


def cross_attn_ln(x, row_blk0, n_batch, seq, tq, mem_k, mem_v, wq, wo, g, b):
    nq = seq // tq

    def const_spec(shape):
        return pl.BlockSpec(shape, lambda bb, i: (0,) * len(shape))

    row_spec = pl.BlockSpec((tq, D_MODEL), lambda bb, i: (row_blk0 + bb * nq + i, 0))
    mem_spec = pl.BlockSpec((1, N_MEM, GROUP_W), lambda bb, i: (bb, 0, 0))
    return pl.pallas_call(
        _cross_attn_kernel, grid=(n_batch, nq),
        in_specs=[row_spec, const_spec(wq.shape), mem_spec, mem_spec, const_spec(wo.shape),
                  const_spec(g.shape), const_spec(b.shape)],
        out_specs=row_spec, out_shape=jax.ShapeDtypeStruct(x.shape, F32), input_output_aliases={0: 0},
        compiler_params=_cp("arbitrary", "arbitrary"), name="cross_attn")(x, wq, mem_k, mem_v, wo, g, b)


SWIGLU_ROWS = 256


def _swiglu_accumulate(xb_ref, wg, wu, wd, acc_ref, n_valid=None):
    wgb, wub, wdb = wg.astype(BF16), wu.astype(BF16), wd.astype(BF16)
    n_sub = xb_ref.shape[0] // SWIGLU_ROWS

    def hidden(r):
        xb = xb_ref[pl.ds(r * SWIGLU_ROWS, SWIGLU_ROWS), :]
        gate = _dot(xb, wgb)
        up = _dot(xb, wub)
        return (gate * jax.nn.sigmoid(gate) * up).astype(BF16)

    def first_sub_blocks(n):
        hid = hidden(0)
        for r in range(n):
            nxt = hidden(r + 1) if r + 1 < n else None
            acc_ref[pl.ds(r * SWIGLU_ROWS, SWIGLU_ROWS), :] += _dot(hid, wdb)
            hid = nxt

    if n_valid is None:
        first_sub_blocks(n_sub)
        return
    need = (n_valid + SWIGLU_ROWS - 1) // SWIGLU_ROWS
    for n in range(1, n_sub + 1):
        pl.when(need == n)(functools.partial(first_sub_blocks, n))


def _ffn_kernel(x_ref, wg_ref, wu_ref, wd_ref, g_ref, b_ref, o_ref, xb_ref, *, nf):
    j = pl.program_id(1)

    @pl.when(j == 0)
    def _():
        xb_ref[...] = x_ref[...].astype(BF16)
        o_ref[...] = jnp.zeros_like(o_ref)

    _swiglu_accumulate(xb_ref, wg_ref[...], wu_ref[...], wd_ref[...], o_ref)

    @pl.when(j == nf - 1)
    def _():
        o_ref[...] = _layer_norm(ALPHA * x_ref[...] + o_ref[...], g_ref[...], b_ref[...])


def ffn_ln(x, wg, wu, wd, g, b, tm, tf):
    m = x.shape[0]
    nf = D_FF // tf
    kern = functools.partial(_ffn_kernel, nf=nf)
    return pl.pallas_call(
        kern, grid=(m // tm, nf),
        in_specs=[pl.BlockSpec((tm, D_MODEL), lambda i, j: (i, 0)),
                  pl.BlockSpec((D_MODEL, tf), lambda i, j: (0, j)), pl.BlockSpec((D_MODEL, tf), lambda i, j: (0, j)),
                  pl.BlockSpec((tf, D_MODEL), lambda i, j: (j, 0)),
                  pl.BlockSpec((1, D_MODEL), lambda i, j: (0, 0)), pl.BlockSpec((1, D_MODEL), lambda i, j: (0, 0))],
        out_specs=pl.BlockSpec((tm, D_MODEL), lambda i, j: (i, 0)),
        out_shape=jax.ShapeDtypeStruct((m, D_MODEL), F32),
        scratch_shapes=[pltpu.VMEM((tm, D_MODEL), BF16)],
        compiler_params=_cp("arbitrary", "arbitrary"), name="ffn")(x, wg, wu, wd, g, b)


def _router_kernel(x_ref, w_ref, b_ref, o_ref):
    logits = jnp.dot(x_ref[...], w_ref[...], preferred_element_type=F32, precision=lax.Precision.HIGHEST) + b_ref[...]
    lane = lax.broadcasted_iota(jnp.int32, logits.shape, 1)
    neg = jnp.float32(-jnp.inf)
    lg = jnp.where(lane < N_EXPERTS, logits, neg)
    m1 = jnp.max(lg, axis=1, keepdims=True)
    i1 = jnp.min(jnp.where(lg == m1, lane, LANES), axis=1, keepdims=True)
    lg2 = jnp.where(lane == i1, neg, lg)
    m2 = jnp.max(lg2, axis=1, keepdims=True)
    i2 = jnp.min(jnp.where(lg2 == m2, lane, LANES), axis=1, keepdims=True)
    e2 = jnp.exp(m2 - m1)
    g1 = 1.0 / (1.0 + e2)
    g2 = e2 / (1.0 + e2)
    out = jnp.where(lane == 0, i1.astype(F32), jnp.where(lane == 1, i2.astype(F32),
                    jnp.where(lane == 2, g1, jnp.where(lane == 3, g2, 0.0))))
    o_ref[...] = out


def router(x, w_pad, b_pad, tm):
    m = x.shape[0]
    return pl.pallas_call(
        _router_kernel, grid=(m // tm,),
        in_specs=[pl.BlockSpec((tm, D_MODEL), lambda i: (i, 0)), pl.BlockSpec((D_MODEL, LANES), lambda i: (0, 0)),
                  pl.BlockSpec((1, LANES), lambda i: (0, 0))],
        out_specs=pl.BlockSpec((tm, LANES), lambda i: (i, 0)),
        out_shape=jax.ShapeDtypeStruct((m, LANES), F32),
        compiler_params=_cp("arbitrary"), name="router")(x, w_pad, b_pad)


def _moe_ffn_kernel(te_ref, nu_ref, tr_ref, x_ref, wg_ref, wu_ref, wd_ref, o_ref, xb_ref):
    i = pl.program_id(0)
    j = pl.program_id(1)
    used = i < nu_ref[0]

    @pl.when(used)
    def _():
        @pl.when(j == 0)
        def _():
            xb_ref[...] = x_ref[...].astype(BF16)
            o_ref[...] = jnp.zeros_like(o_ref)

        _swiglu_accumulate(xb_ref, wg_ref[0], wu_ref[0], wd_ref[0], o_ref, tr_ref[i])

    @pl.when(jnp.logical_and(jnp.logical_not(used), j == 0))
    def _():
        o_ref[...] = jnp.zeros_like(o_ref)


def moe_ffn(x_sorted, tile_expert, n_used, tile_rows, wg, wu, wd, tf):
    n_rows = x_sorted.shape[0]
    n_tiles = n_rows // MOE_TILE
    nf = D_FF // tf

    def jj(i, j, nu):
        return jnp.where(i < nu[0], j, nf - 1)

    gs = pltpu.PrefetchScalarGridSpec(
        num_scalar_prefetch=3, grid=(n_tiles, nf),
        in_specs=[pl.BlockSpec((MOE_TILE, D_MODEL), lambda i, j, te, nu, tr: (i, 0)),
                  pl.BlockSpec((1, D_MODEL, tf), lambda i, j, te, nu, tr: (te[i], 0, jj(i, j, nu))),
                  pl.BlockSpec((1, D_MODEL, tf), lambda i, j, te, nu, tr: (te[i], 0, jj(i, j, nu))),
                  pl.BlockSpec((1, tf, D_MODEL), lambda i, j, te, nu, tr: (te[i], jj(i, j, nu), 0))],
        out_specs=pl.BlockSpec((MOE_TILE, D_MODEL), lambda i, j, te, nu, tr: (i, 0)),
        scratch_shapes=[pltpu.VMEM((MOE_TILE, D_MODEL), BF16)])
    return pl.pallas_call(
        _moe_ffn_kernel, grid_spec=gs, out_shape=jax.ShapeDtypeStruct((n_rows, D_MODEL), F32),
        compiler_params=_cp("arbitrary", "arbitrary", vmem=MOE_VMEM_LIMIT), name="moe_ffn")(
            tile_expert, n_used, tile_rows, x_sorted, wg, wu, wd)


def _combine_ln_kernel(x_ref, r_ref, ya_ref, yb_ref, g_ref, b_ref, o_ref):
    r = r_ref[...]
    y = r[:, TOP_K:TOP_K + 1] * ya_ref[...] + r[:, TOP_K + 1:TOP_K + 2] * yb_ref[...]
    o_ref[...] = _layer_norm(ALPHA * x_ref[...] + y, g_ref[...], b_ref[...])


def combine_ln(x, r, ya, yb, g, b, tm, row_blk0, rows):
    row = pl.BlockSpec((tm, D_MODEL), lambda i: (row_blk0 + i, 0))
    vec = pl.BlockSpec((1, D_MODEL), lambda i: (0, 0))
    return pl.pallas_call(
        _combine_ln_kernel, grid=(rows // tm,),
        in_specs=[row, pl.BlockSpec((tm, LANES), lambda i: (row_blk0 + i, 0)), row, row, vec, vec],
        out_specs=pl.BlockSpec((tm, D_MODEL), lambda i: (i, 0)),
        out_shape=jax.ShapeDtypeStruct((rows, D_MODEL), F32),
        compiler_params=_cp("arbitrary"), name="combine_ln")(x, r, ya, yb, g, b)


def moe_ln(x, router_w, router_b, wg, wu, wd, g, b, tm, splits):
    m = x.shape[0]
    w_pad = jnp.zeros((D_MODEL, LANES), F32).at[:, :N_EXPERTS].set(router_w)
    b_pad = jnp.zeros((1, LANES), F32).at[0, :N_EXPERTS].set(router_b)
    r = router(x, w_pad, b_pad, tm)
    top_idx = r[:, :TOP_K].astype(jnp.int32)
    n_slot = m * TOP_K
    onehot = (top_idx[:, :, None] == jnp.arange(N_EXPERTS, dtype=jnp.int32)).astype(jnp.int32)
    per_tok = onehot[:, 0] + onehot[:, 1]
    before = jnp.cumsum(per_tok, axis=0) - per_tok
    counts = jnp.sum(per_tok, axis=0)
    tiles_per = (counts + MOE_TILE - 1) // MOE_TILE
    tile_end = jnp.cumsum(tiles_per)
    row0 = (tile_end - tiles_per) * MOE_TILE
    dest = jnp.sum(onehot * (before + row0)[:, None, :], axis=2)
    n_tiles = -(-n_slot // MOE_TILE) + N_EXPERTS
    n_rows = n_tiles * MOE_TILE
    row_tok = (jnp.arange(n_rows, dtype=jnp.int32) % m).at[dest.reshape(-1)].set(
        jnp.arange(n_slot, dtype=jnp.int32) // TOP_K)
    n_used = tile_end[-1:].astype(jnp.int32)
    tile_ids = jnp.minimum(jnp.arange(n_tiles, dtype=jnp.int32), n_used[0] - 1)
    tile_expert = jnp.minimum(jnp.sum((tile_end[None, :] <= tile_ids[:, None]).astype(jnp.int32), axis=1),
                              N_EXPERTS - 1)
    tile_rows = jnp.clip(counts[tile_expert] - (tile_ids - (tile_end - tiles_per)[tile_expert]) * MOE_TILE,
                         0, MOE_TILE).astype(jnp.int32)
    x_sorted = x[row_tok]
    y_sorted = moe_ffn(x_sorted, tile_expert, n_used, tile_rows, wg, wu, wd, tf=512)
    ya, yb = y_sorted[dest[:, 0]], y_sorted[dest[:, 1]]
    return [combine_ln(x, r, ya, yb, g, b, t, blk0, rows) for blk0, rows, t in splits]


def kernel(x_prompt, x_sample, cache_sb_k, cache_sb_v, cache_mem_k, cache_mem_v, state_ml_C, state_ml_n, state_ml_m, state_rt_S, state_s5_re, state_s5_im, page_table, mem_prompt, w_in, sb_bias, ml_b_i, ml_b_f, ml_norm_g, rt_norm_g, s5_A_re, s5_A_im, s5_log_dt, s5_B_re, s5_B_im, s5_C_re, s5_C_im, s5_D, s5_glu_w, s5_glu_b, w_out, ca_wq, ca_wk, ca_wv, ca_wo, ln_g, ln_b, ffn_w_gate, ffn_w_up, ffn_w_down, moe_router_w, moe_router_b, moe_w_gate, moe_w_up, moe_w_down):
    bp, tp, _ = x_prompt.shape
    bs, ts, _ = x_sample.shape
    n_p, n_s = bp * tp, bs * ts
    tm = 640
    assert (n_p + n_s) % (2 * tm) == 0 and n_p % n_s == 0 and tp % 512 == 0 and bp % 8 == 0 and bs % 8 == 0
    x = jnp.concatenate([x_prompt.reshape(n_p, D_MODEL), x_sample.reshape(n_s, D_MODEL)], axis=0)
    uu = _suffix_matrix()
    g_off = 7 * GROUP_W
    half = HEAD_DIM // 2
    freq = ROPE_BASE ** (-jnp.arange(half, dtype=F32) / half)

    def rope_tables(pos):
        ang = pos.astype(F32)[:, None] * freq[None, :]
        cos, sin = jnp.cos(ang), jnp.sin(ang)
        return (jnp.tile(jnp.concatenate([cos, cos], axis=1), (1, N_HEADS)),
                jnp.tile(jnp.concatenate([-sin, sin], axis=1), (1, N_HEADS)))

    cos_p, sin_p = rope_tables(jnp.arange(tp, dtype=jnp.int32))
    cos_s, sin_s = rope_tables(PAST_LEN + jnp.arange(ts, dtype=jnp.int32))
    cache_kt = cache_sb_k.transpose(0, 1, 3, 4, 2)
    cache_vt = cache_sb_v.transpose(0, 1, 3, 4, 2)

    p_st = [[] for _ in range(10)]
    s_st = [[] for _ in range(8)]
    for l in range(DEPTH):
        wl = w_in[l]
        w_cat = jnp.concatenate([wl[:, :g_off], wl[:, g_off + 2 * N_HEADS:], wl[:, g_off:g_off + 2 * N_HEADS],
                                 jnp.zeros((D_MODEL, PROJ_W - wl.shape[1]), F32)], axis=1).astype(BF16)
        proj_p, kt_p, vt_p = in_proj_prompt(x, w_cat, bp, tp, 512)
        proj_s = linear(x, w_cat, n_s, n_p // n_s, n_s)
        proj_p3 = proj_p.reshape(bp, tp, PROJ_W)
        proj_s3 = proj_s.reshape(bs, ts, PROJ_W)
        gate_bias = jnp.zeros((1, LANES), F32).at[0, :2 * N_HEADS].set(jnp.concatenate([ml_b_i[l], ml_b_f[l]]))
        ml_g = ml_norm_g[l][None, :]
        rt_g = rt_norm_g[l][None, :]
        wb, a1, a2, wc = _s5_weights(s5_A_re[l], s5_A_im[l], s5_log_dt[l], s5_B_re[l], s5_B_im[l],
                                     s5_C_re[l], s5_C_im[l])
        s5_d = s5_D[l][None, :]
        glu_w = s5_glu_w[l].astype(BF16)
        glu_b = s5_glu_b[l][None, :]

        o_sb_p = sb_attention_prompt(proj_p, sb_bias[l], uu, bp, tp, tq=256)
        o_ml_p, cn_p, m_p = mlstm_mixer(proj_p3, gate_bias, ml_g, jnp.zeros((bp, N_HEADS, HEAD_DIM, LANES), F32),
                                        jnp.zeros((bp, N_HEADS), F32), bb=8)
        o_rt_p, rs_p = retention_mixer(proj_p3, cos_p, sin_p, rt_g,
                                       jnp.zeros((bp, N_HEADS, HEAD_DIM, HEAD_DIM), F32), bb=8)
        o_ml_p = o_ml_p.reshape(n_p, GROUP_W)
        o_rt_p = o_rt_p.reshape(n_p, GROUP_W)
        o_s5_p, h5_p = s5_mixer(proj_p3, 64, wb, a1, a2, jnp.zeros((bp, 2 * S5_W), F32), wc, s5_d, glu_w, glu_b)
        o_s5_p = o_s5_p.reshape(n_p, GROUP_W)

        o_sb_s = sb_attention_sample(proj_s, 0, cache_kt, cache_vt, page_table, l, sb_bias[l], uu, n_pp=16)
        cn0 = jnp.concatenate([state_ml_C[:, l], state_ml_n[:, l][..., None],
                               jnp.zeros((bs, N_HEADS, HEAD_DIM, LANES - HEAD_DIM - 1), F32)], axis=-1)
        o_ml_s, cn_s, m_s = mlstm_mixer(proj_s3, gate_bias, ml_g, cn0, state_ml_m[:, l], bb=8)
        o_rt_s, rs_s = retention_mixer(proj_s3, cos_s, sin_s, rt_g, state_rt_S[:, l], bb=8)
        o_ml_s = o_ml_s.reshape(n_s, GROUP_W)
        o_rt_s = o_rt_s.reshape(n_s, GROUP_W)
        h0_s = jnp.concatenate([state_s5_re[:, l].reshape(bs, S5_W), state_s5_im[:, l].reshape(bs, S5_W)], axis=1)
        o_s5_s, h5_s = s5_mixer(proj_s3, ts, wb, a1, a2, h0_s, wc, s5_d, glu_w, glu_b)
        o_s5_s = o_s5_s.reshape(n_s, GROUP_W)

        wo_mix = w_out[l].astype(BF16)
        g0, b0 = ln_g[l, 0][None, :], ln_b[l, 0][None, :]
        x = mix_out_ln(x, (o_sb_p, o_ml_p, o_rt_p, o_s5_p), wo_mix, g0, b0, 512, 0)
        x = mix_out_ln(x, (o_sb_s, o_ml_s, o_rt_s, o_s5_s), wo_mix, g0, b0, n_s, n_p // n_s)

        mem_kv = linear(mem_prompt.reshape(bp * N_MEM, D_MODEL),
                        jnp.concatenate([ca_wk[l], ca_wv[l]], axis=1).astype(BF16), 512)
        mk_p = mem_kv[:, :GROUP_W].reshape(bp, N_MEM, GROUP_W)
        mv_p = mem_kv[:, GROUP_W:].reshape(bp, N_MEM, GROUP_W)
        wq = ca_wq[l].astype(BF16)
        wo = ca_wo[l].astype(BF16)
        g1, b1 = ln_g[l, 1][None, :], ln_b[l, 1][None, :]
        x = cross_attn_ln(x, 0, bp, tp, 512, mk_p, mv_p, wq, wo, g1, b1)
        x = cross_attn_ln(x, n_p // ts, bs, ts, ts, cache_mem_k[:, l].reshape(bs, N_MEM, GROUP_W),
                          cache_mem_v[:, l].reshape(bs, N_MEM, GROUP_W), wq, wo, g1, b1)

        g2, b2 = ln_g[l, 2][None, :], ln_b[l, 2][None, :]
        j = l // 2
        last = l == DEPTH - 1
        if l % 2 == 0:
            x = ffn_ln(x, ffn_w_gate[j].astype(BF16), ffn_w_up[j].astype(BF16), ffn_w_down[j].astype(BF16),
                       g2, b2, 2 * tm, tf=512)
            y_out = (x[:n_p], x[n_p:]) if last else None
        else:
            splits = [(0, n_p, 512), (n_p // n_s, n_s, n_s)] if last else [(0, n_p + n_s, tm)]
            y_out = moe_ln(x, moe_router_w[j], moe_router_b[j], moe_w_gate[j], moe_w_up[j], moe_w_down[j],
                           g2, b2, tm, splits)
            x = None if last else y_out[0]

        def heads(a, nb_, t_):
            return a.reshape(nb_, t_, N_HEADS, HEAD_DIM)

        p_st[0].append(kt_p)
        p_st[1].append(vt_p)
        p_st[2].append(heads(mk_p, bp, N_MEM))
        p_st[3].append(heads(mv_p, bp, N_MEM))
        s_st[0].append(heads(proj_s[:, C_SK * GROUP_W:(C_SK + 1) * GROUP_W], bs, ts))
        s_st[1].append(heads(proj_s[:, C_SV * GROUP_W:(C_SV + 1) * GROUP_W], bs, ts))
        for st, cn, mm, rs, h5, nb_ in ((p_st, cn_p, m_p, rs_p, h5_p, bp), (s_st, cn_s, m_s, rs_s, h5_s, bs)):
            off = 4 if st is p_st else 2
            st[off + 0].append(cn[..., :HEAD_DIM])
            st[off + 1].append(cn[..., HEAD_DIM])
            st[off + 2].append(mm[:, 0, :N_HEADS])
            st[off + 3].append(rs)
            st[off + 4].append(h5[:, :S5_W].reshape(nb_, S5_GROUPS, S5_STATE))
            st[off + 5].append(h5[:, S5_W:].reshape(nb_, S5_GROUPS, S5_STATE))

    y_prompt = y_out[0].reshape(bp, tp, D_MODEL)
    y_sample = y_out[1].reshape(bs, ts, D_MODEL)
    p_out = [jnp.stack(a, axis=1) for a in p_st]
    for i in range(2):
        p_out[i] = p_out[i].reshape(bp, DEPTH, N_HEADS, HEAD_DIM, tp).transpose(0, 1, 4, 2, 3)
    s_out = [jnp.stack(a, axis=1) for a in s_st]
    return (y_prompt, y_sample, *p_out, *s_out)
```

```python
import functools
import math

import numpy as np
import jax
import jax.numpy as jnp
from jax import lax
from jax.experimental import pallas as pl
from jax.experimental.pallas import tpu as pltpu

F32 = jnp.float32
BF16 = jnp.bfloat16

D_MODEL = 1024
DEPTH = 2
PAST_LEN = 8192
PAGE_SIZE = 128
HEAD_DIM = 64
N_HEADS = 4
GROUP_W = N_HEADS * HEAD_DIM
S5_GROUPS = 16
S5_GROUP = 16
S5_STATE = 64
S5_W = S5_GROUPS * S5_STATE
N_MEM = 256
D_FF = 3584
N_EXPERTS = 8
TOP_K = 2
CHUNK = 64
ROPE_BASE = 10000.0
LN_EPS = 1e-5
GN_EPS = 1e-6
ALPHA = (2 * DEPTH) ** 0.25
QK_SCALE = HEAD_DIM ** -0.5
LOG2E = math.log2(math.e)

LANES = 128
PROJ_W = 25 * LANES
C_SQ, C_SK, C_SV, C_MQ, C_MK, C_MV, C_MO, C_RQ, C_RK, C_RV, C_RG, C_SU = range(12)
C_GATES = 12 * GROUP_W // LANES
VMEM_LIMIT = 48 * 1024 * 1024
MOE_TILE = 1024
MIX_DTYPE = BF16


def _cp(*sem):
    return pltpu.CompilerParams(dimension_semantics=sem, vmem_limit_bytes=VMEM_LIMIT)


def _dot(a, b):
    return jnp.dot(a, b, preferred_element_type=F32)


def _dot_nt(a, b):
    return lax.dot_general(a, b, (((1,), (1,)), ((), ())), preferred_element_type=F32)


def _dot_tn(a, b):
    return lax.dot_general(a, b, (((0,), (0,)), ((), ())), preferred_element_type=F32)


def _layer_norm(y, g, b):
    mu = jnp.mean(y, axis=-1, keepdims=True)
    yc = y - mu
    var = jnp.mean(yc * yc, axis=-1, keepdims=True)
    return yc * lax.rsqrt(var + LN_EPS) * g + b


def _row_sum(x, scale=1.0):
    ones = jnp.full((x.shape[1], LANES), scale, BF16)
    hi = x.astype(BF16)
    lo = (x - hi.astype(F32)).astype(BF16)
    return (_dot(hi, ones) + _dot(lo, ones))[:, :x.shape[1]]


def _cumsum_rows(tril, x):
    hi = x.astype(BF16)
    lo = (x - hi.astype(F32)).astype(BF16)
    return _dot(tril, hi) + _dot(tril, lo)


def _head_norm_all(hd):
    inv = 1.0 / HEAD_DIM
    mu = {p: _row_sum(x, inv) for p, x in hd.items()}
    hc = {p: hd[p] - mu[p] for p in hd}
    var = {p: _row_sum(hc[p] * hc[p], inv) for p in hd}
    return {p: hc[p] * lax.rsqrt(var[p] + GN_EPS) for p in hd}


def _neg_softplus(z):
    return -(jnp.maximum(z, 0.0) + jnp.log1p(jnp.exp(-jnp.abs(z))))


def _log_sigmoid(z):
    return _neg_softplus(-z)


def _linear_kernel(x_ref, w_ref, o_ref):
    o_ref[...] = _dot(x_ref[...].astype(BF16), w_ref[...]).astype(o_ref.dtype)


def linear(x, w, tm, row_blk0=0, n_rows=None, out_dtype=F32):
    m, k = x.shape
    m = m if n_rows is None else n_rows
    n = w.shape[1]
    return pl.pallas_call(
        _linear_kernel, grid=(m // tm,),
        in_specs=[pl.BlockSpec((tm, k), lambda i: (row_blk0 + i, 0)), pl.BlockSpec((k, n), lambda i: (0, 0))],
        out_specs=pl.BlockSpec((tm, n), lambda i: (i, 0)),
        out_shape=jax.ShapeDtypeStruct((m, n), out_dtype),
        compiler_params=_cp("arbitrary"), name="linear")(x, w)


def _in_proj_kernel(x_ref, w_ref, o_ref, kt_ref, vt_ref):
    o = _dot(x_ref[...].astype(BF16), w_ref[...])
    o_ref[...] = o
    kt_ref[0] = o[:, C_SK * GROUP_W:(C_SK + 1) * GROUP_W].T
    vt_ref[0] = o[:, C_SV * GROUP_W:(C_SV + 1) * GROUP_W].T


def in_proj_prompt(x, w, n_batch, seq, tm):
    k = x.shape[1]
    n = w.shape[1]
    nt = seq // tm
    t_spec = pl.BlockSpec((1, GROUP_W, tm), lambda i: (i // nt, 0, i % nt))
    t_shape = jax.ShapeDtypeStruct((n_batch, GROUP_W, seq), F32)
    return pl.pallas_call(
        _in_proj_kernel, grid=(n_batch * nt,),
        in_specs=[pl.BlockSpec((tm, k), lambda i: (i, 0)), pl.BlockSpec((k, n), lambda i: (0, 0))],
        out_specs=[pl.BlockSpec((tm, n), lambda i: (i, 0)), t_spec, t_spec],
        out_shape=[jax.ShapeDtypeStruct((n_batch * seq, n), F32), t_shape, t_shape],
        compiler_params=_cp("arbitrary"), name="in_proj")(x, w)


def _mix_out_ln_kernel(x_ref, a_ref, b_ref, c_ref, d_ref, w_ref, g_ref, bias_ref, o_ref):
    h = sum(_dot(p[...].astype(BF16), w_ref[pl.ds(n * GROUP_W, GROUP_W), :])
            for n, p in enumerate((a_ref, b_ref, c_ref, d_ref)))
    o_ref[...] = _layer_norm(ALPHA * x_ref[...] + h, g_ref[...], bias_ref[...])


def mix_out_ln(x, parts, w, g, b, tm, row_blk0):
    rows = parts[0].shape[0]
    row_spec = pl.BlockSpec((tm, D_MODEL), lambda i: (row_blk0 + i, 0))
    part_spec = pl.BlockSpec((tm, GROUP_W), lambda i: (i, 0))
    vec = pl.BlockSpec((1, D_MODEL), lambda i: (0, 0))
    return pl.pallas_call(
        _mix_out_ln_kernel, grid=(rows // tm,),
        in_specs=[row_spec, part_spec, part_spec, part_spec, part_spec,
                  pl.BlockSpec((D_MODEL, D_MODEL), lambda i: (0, 0)), vec, vec],
        out_specs=row_spec, out_shape=jax.ShapeDtypeStruct(x.shape, F32), input_output_aliases={0: 0},
        compiler_params=_cp("arbitrary"), name="mix_out_ln")(x, *parts, w, g, b)


def _suffix_matrix():
    j = np.arange(LANES)
    u = (j[:, None] >= j[None, :]).astype(np.float32)
    uu = np.concatenate([u, np.ones((LANES, LANES), np.float32)], axis=1)
    return jnp.asarray(np.concatenate([uu, uu], axis=0), dtype=BF16)


def _suffix_sums(lr, uu):
    hi = lr.astype(BF16)
    lo = (lr - hi.astype(F32)).astype(BF16)
    r = _dot(jnp.concatenate([hi, lo], axis=1), uu)
    return r[:, :LANES], r[:, LANES:]


def _log2_rem(z2):
    return jnp.minimum(-z2, 0.0) - jnp.log2(1.0 + jnp.exp2(-jnp.abs(z2)))


def _sb_prompt_kernel(bias_ref, q_ref, k_ref, v_ref, uu_ref, o_ref, acc_ref, car_ref, *, tq):
    i = pl.program_id(1)
    tk = LANES
    nsub = tq // tk
    acc_ref[...] = jnp.zeros_like(acc_ref)
    car_ref[...] = jnp.zeros_like(car_ref)
    q = (q_ref[...] * (QK_SCALE * LOG2E)).astype(BF16)
    qh = [q[:, h * HEAD_DIM:(h + 1) * HEAD_DIM] for h in range(N_HEADS)]
    b2 = [bias_ref[h] * LOG2E for h in range(N_HEADS)]
    uu = uu_ref[...]
    row = lax.broadcasted_iota(jnp.int32, (tq, tk), 0)
    col = lax.broadcasted_iota(jnp.int32, (tq, tk), 1)
    heads = range(N_HEADS)

    def block_pair(j_hi, causal_hi, causal_lo):
        kb, vb, z2 = [], [], []
        for d in range(2):
            r0 = pl.multiple_of((j_hi - d) * tk, tk)
            kb.append(k_ref[pl.ds(r0, tk), :].astype(BF16))
            vb.append(v_ref[pl.ds(r0, tk), :].astype(BF16))
            z2.append([_dot_nt(qh[h], kb[d][:, h * HEAD_DIM:(h + 1) * HEAD_DIM]) + b2[h] for h in heads])
        cs, tot = [], []
        for d, causal in enumerate((causal_hi, causal_lo)):
            lr = [_log2_rem(z) for z in z2[d]]
            if causal is not None:
                lr = [jnp.where(causal, a, 0.0) for a in lr]
            c, t = _suffix_sums(jnp.concatenate(lr, axis=0), uu)
            cs.append(c)
            tot.append(t)
        car = [car_ref[h] for h in heads]
        pv = []
        for d, causal in enumerate((causal_hi, causal_lo)):
            w = [jnp.exp2(z2[d][h] + cs[d][h * tq:(h + 1) * tq] + car[h]) for h in heads]
            if causal is not None:
                w = [jnp.where(causal, a, 0.0) for a in w]
            pv.append([_dot(w[h].astype(BF16), vb[d][:, h * HEAD_DIM:(h + 1) * HEAD_DIM]) for h in heads])
            car = [car[h] + tot[d][h * tq:(h + 1) * tq] for h in heads]
        for h in heads:
            acc_ref[h] += pv[0][h] + pv[1][h]
            car_ref[h] = car[h]

    assert nsub == 2
    block_pair(i * nsub + 1, (col + tk) < row, col < row)

    def body(jj, carry):
        block_pair(i * nsub - 1 - 2 * jj, None, None)
        return carry

    lax.fori_loop(0, i, body, 0)
    o_ref[...] = jnp.concatenate([acc_ref[h] for h in range(N_HEADS)], axis=1).astype(o_ref.dtype)


def sb_attention_prompt(proj, sb_bias, uu, n_batch, seq, tq):
    nq = seq // tq
    kern = functools.partial(_sb_prompt_kernel, tq=tq)
    return pl.pallas_call(
        kern, grid=(n_batch, nq),
        in_specs=[pl.BlockSpec(memory_space=pltpu.SMEM),
                  pl.BlockSpec((tq, GROUP_W), lambda b, i: (b * nq + i, C_SQ)),
                  pl.BlockSpec((seq, GROUP_W), lambda b, i: (b, C_SK)),
                  pl.BlockSpec((seq, GROUP_W), lambda b, i: (b, C_SV)),
                  pl.BlockSpec((2 * LANES, 2 * LANES), lambda b, i: (0, 0))],
        out_specs=pl.BlockSpec((tq, GROUP_W), lambda b, i: (b * nq + i, 0)),
        out_shape=jax.ShapeDtypeStruct((n_batch * seq, GROUP_W), MIX_DTYPE),
        scratch_shapes=[pltpu.VMEM((N_HEADS, tq, HEAD_DIM), F32), pltpu.VMEM((N_HEADS, tq, LANES), F32)],
        compiler_params=_cp("arbitrary", "arbitrary"), name="sb_prompt")(sb_bias, proj, proj, proj, uu)


def _sb_sample_kernel(pt_ref, bias_ref, q_ref, kn_ref, vn_ref, u8_ref, uu_ref, *rest, n_pp, n_steps):
    k_refs = rest[:n_pp]
    v_refs = rest[n_pp:2 * n_pp]
    o_ref = rest[2 * n_pp]
    acc_ref, car_ref = rest[2 * n_pp + 1:]
    s = pl.program_id(1)
    nq = q_ref.shape[0]
    rows = N_HEADS * nq
    q = (q_ref[...] * (QK_SCALE * LOG2E)).astype(BF16)
    row_head = lax.broadcasted_iota(jnp.int32, (rows, GROUP_W), 0) // nq
    own = row_head == lax.broadcasted_iota(jnp.int32, (rows, GROUP_W), 1) // HEAD_DIM
    q_bd = jnp.where(own, jnp.concatenate([q] * N_HEADS, axis=0), jnp.zeros((), BF16))
    row_head_l = lax.broadcasted_iota(jnp.int32, (rows, LANES), 0) // nq
    b2 = jnp.zeros((rows, LANES), F32)
    for h in range(N_HEADS):
        b2 = jnp.where(row_head_l == h, bias_ref[h] * LOG2E, b2)

    @pl.when(s == 0)
    def _():
        kn = kn_ref[...].astype(BF16)
        vn = vn_ref[...].astype(BF16)
        t = lax.broadcasted_iota(jnp.int32, (rows, nq), 0) % nq
        causal = lax.broadcasted_iota(jnp.int32, (rows, nq), 1) < t
        z2 = _dot_nt(q_bd, kn) + b2[:, :nq]
        lr = jnp.where(causal, _log2_rem(z2), 0.0)
        cs = jnp.dot(lr, u8_ref[...], preferred_element_type=F32, precision=lax.Precision.HIGHEST)
        w = jnp.where(causal, jnp.exp2(z2 + cs), 0.0)
        acc_ref[...] = _dot(w.astype(BF16), vn)
        car_ref[...] = jnp.broadcast_to(jnp.sum(lr, axis=1, keepdims=True), (rows, LANES))

    z2s = [_dot(q_bd, k_refs[p][0, 0].reshape(GROUP_W, PAGE_SIZE).astype(BF16)) + b2 for p in range(n_pp)]
    cs_all, tot_all = _suffix_sums(_log2_rem(jnp.concatenate(z2s, axis=0)), uu_ref[...])
    car = car_ref[...]
    acc = acc_ref[...]
    for p in range(n_pp):
        w = jnp.exp2(z2s[p] + cs_all[p * rows:(p + 1) * rows] + car)
        acc = acc + _dot_nt(w.astype(BF16), v_refs[p][0, 0].reshape(GROUP_W, PAGE_SIZE).astype(BF16))
        car = car + tot_all[p * rows:(p + 1) * rows]
    car_ref[...] = car
    acc_ref[...] = acc

    @pl.when(s == n_steps - 1)
    def _():
        kept = jnp.where(own, acc, 0.0)
        o_ref[...] = sum(kept[h * nq:(h + 1) * nq] for h in range(N_HEADS))


def sb_attention_sample(proj, row_blk0, cache_kt, cache_vt, page_table, layer, sb_bias, uu, n_pp):
    n_batch, n_pages = page_table.shape
    nq = 8
    n_steps = n_pages // n_pp
    u8 = jnp.asarray((np.arange(nq)[:, None] >= np.arange(nq)[None, :]).astype(np.float32))

    def page_spec(p):
        return pl.BlockSpec((1, 1, N_HEADS, HEAD_DIM, PAGE_SIZE),
                            lambda b, s, pt: (pt[b, n_pages - 1 - (s * n_pp + p)], layer, 0, 0, 0))

    def row_spec(cblk):
        return pl.BlockSpec((nq, GROUP_W), lambda b, s, pt: (row_blk0 + b, cblk))

    kern = functools.partial(_sb_sample_kernel, n_pp=n_pp, n_steps=n_steps)
    gs = pltpu.PrefetchScalarGridSpec(
        num_scalar_prefetch=1, grid=(n_batch, n_steps),
        in_specs=[pl.BlockSpec(memory_space=pltpu.SMEM), row_spec(C_SQ), row_spec(C_SK), row_spec(C_SV),
                  pl.BlockSpec((nq, nq), lambda b, s, pt: (0, 0)),
                  pl.BlockSpec((2 * LANES, 2 * LANES), lambda b, s, pt: (0, 0))]
                 + [page_spec(p) for p in range(n_pp)] * 2,
        out_specs=pl.BlockSpec((nq, GROUP_W), lambda b, s, pt: (b, 0)),
        scratch_shapes=[pltpu.VMEM((N_HEADS * nq, GROUP_W), F32), pltpu.VMEM((N_HEADS * nq, LANES), F32)])
    return pl.pallas_call(
        kern, grid_spec=gs, out_shape=jax.ShapeDtypeStruct((n_batch * nq, GROUP_W), F32),
        compiler_params=_cp("arbitrary", "arbitrary"), name="sb_sample")(
            page_table, sb_bias, proj, proj, proj, u8, uu, *([cache_kt] * n_pp), *([cache_vt] * n_pp))


def _mlstm_kernel(m0_ref, q_ref, k_ref, v_ref, og_ref, gt_ref, gb_ref, ng_ref, tril_ref, cn0_ref,
                  o_ref, cn_ref, m_ref, ms_ref, *, bb, chunk, n_t):
    bi = pl.program_id(0)
    t = pl.program_id(1)
    seqs = range(bb)
    pairs = [(b, h) for b in seqs for h in range(N_HEADS)]

    @pl.when(t == 0)
    def _():
        cn_ref[...] = cn0_ref[...]
        for b, h in pairs:
            ms_ref[b * N_HEADS + h] = jnp.full((1, LANES), m0_ref[bi * bb + b, h], F32)

    tril = tril_ref[...]
    tri_mask = lax.broadcasted_iota(jnp.int32, (chunk, chunk), 1) <= lax.broadcasted_iota(jnp.int32, (chunk, chunk), 0)
    lane = lax.broadcasted_iota(jnp.int32, (chunk, HEAD_DIM), 1)
    ones_col = jnp.where(lane == 0, 1.0, 0.0).astype(F32)
    hs = lambda h: slice(h * HEAD_DIM, (h + 1) * HEAD_DIM)

    gt = [gt_ref[b] + gb_ref[...] for b in seqs]
    bc = [_cumsum_rows(tril, _log_sigmoid(g)) for g in gt]
    gt_t = [g.T for g in gt]
    bc_t = [x.T for x in bc]
    q = [q_ref[b].astype(BF16) for b in seqs]
    k = [(k_ref[b] * QK_SCALE).astype(BF16) for b in seqs]
    v = [v_ref[b] for b in seqs]
    m_prev = {p: ms_ref[p[0] * N_HEADS + p[1]][:, :1] for p in pairs}
    ig_col = {(b, h): gt[b][:, h:h + 1] for b, h in pairs}
    bc_col = {(b, h): bc[b][:, N_HEADS + h:N_HEADS + h + 1] for b, h in pairs}
    dm = {(b, h): jnp.where(tri_mask, bc_col[b, h] - (bc_t[b][N_HEADS + h:N_HEADS + h + 1, :] - gt_t[b][h:h + 1, :]),
                            -jnp.inf) for b, h in pairs}
    a = {p: bc_col[p] + m_prev[p] for p in pairs}
    m_new = {p: jnp.maximum(a[p], jnp.max(dm[p], axis=1, keepdims=True)) for p in pairs}
    inter = {p: jnp.exp(a[p] - m_new[p]) for p in pairs}
    s = {(b, h): _dot_nt(q[b][:, hs(h)], k[b][:, hs(h)]) * jnp.exp(dm[b, h] - m_new[b, h]) for b, h in pairs}
    v_ext = {(b, h): jnp.concatenate([v[b][:, hs(h)], ones_col], axis=1) for b, h in pairs}
    qc = {(b, h): _dot(q[b][:, hs(h)], cn_ref[b, h].astype(BF16)) for b, h in pairs}
    sv = {p: _dot(s[p].astype(BF16), v_ext[p].astype(BF16)) for p in pairs}
    m_last = {p: m_new[p][chunk - 1:chunk, :] for p in pairs}
    wl = {p: jnp.exp(bc_col[p][chunk - 1:chunk, :] - bc_col[p] + ig_col[p] - m_last[p]) for p in pairs}
    dl = {p: jnp.exp(a[p][chunk - 1:chunk, :] - m_last[p]) for p in pairs}
    upd = {(b, h): _dot_tn(k[b][:, hs(h)], (wl[b, h] * v_ext[b, h]).astype(BF16)) for b, h in pairs}
    rs = {p: _row_sum(s[p])[:, :1] for p in pairs}
    num = {p: inter[p] * qc[p][:, :HEAD_DIM] + sv[p][:, :HEAD_DIM] for p in pairs}
    den = {p: inter[p] * qc[p][:, HEAD_DIM:HEAD_DIM + 1] + rs[p] for p in pairs}
    hh = _head_norm_all({p: num[p] / jnp.maximum(jnp.abs(den[p]), jnp.exp(-m_new[p])) for p in pairs})
    for b, h in pairs:
        cn_ref[b, h] = dl[b, h] * cn_ref[b, h] + upd[b, h]
        ms_ref[b * N_HEADS + h] = jnp.broadcast_to(m_last[b, h], (1, LANES))
    for b in seqs:
        y = jnp.concatenate([hh[b, h] for h in range(N_HEADS)], axis=1)
        o_ref[b] = (y * ng_ref[...] * jax.nn.sigmoid(og_ref[b])).astype(o_ref.dtype)

    @pl.when(t == n_t - 1)
    def _():
        lane_m = lax.broadcasted_iota(jnp.int32, (1, LANES), 1)
        for b in seqs:
            m_out = jnp.zeros((1, LANES), F32)
            for h in range(N_HEADS):
                m_out = jnp.where(lane_m == h, ms_ref[b * N_HEADS + h], m_out)
            m_ref[b] = m_out


def mlstm_mixer(proj3, gate_bias, norm_g, cn0, m0, bb, out_dtype):
    n_batch, seq, _ = proj3.shape
    chunk = math.gcd(seq, CHUNK)
    n_t = seq // chunk
    tril = jnp.asarray(np.tril(np.ones((chunk, chunk), np.float32)), dtype=BF16)

    def row_spec(cblk, w=GROUP_W):
        return pl.BlockSpec((bb, chunk, w), lambda bi, t: (bi, t, cblk))

    def const_spec(shape):
        return pl.BlockSpec(shape, lambda bi, t: (0,) * len(shape))

    state_spec = pl.BlockSpec((bb, N_HEADS, HEAD_DIM, LANES), lambda bi, t: (bi, 0, 0, 0))
    kern = functools.partial(_mlstm_kernel, bb=bb, chunk=chunk, n_t=n_t)
    return pl.pallas_call(
        kern, grid=(n_batch // bb, n_t),
        in_specs=[pl.BlockSpec(memory_space=pltpu.SMEM),
                  row_spec(C_MQ), row_spec(C_MK), row_spec(C_MV), row_spec(C_MO), row_spec(C_GATES, LANES),
                  const_spec((1, LANES)), const_spec((1, GROUP_W)), const_spec((chunk, chunk)), state_spec],
        out_specs=[pl.BlockSpec((bb, chunk, GROUP_W), lambda bi, t: (bi, t, 0)), state_spec,
                   pl.BlockSpec((bb, 1, LANES), lambda bi, t: (bi, 0, 0))],
        out_shape=[jax.ShapeDtypeStruct((n_batch, seq, GROUP_W), out_dtype),
                   jax.ShapeDtypeStruct((n_batch, N_HEADS, HEAD_DIM, LANES), F32),
                   jax.ShapeDtypeStruct((n_batch, 1, LANES), F32)],
        scratch_shapes=[pltpu.VMEM((bb * N_HEADS, 1, LANES), F32)],
        compiler_params=_cp("arbitrary", "arbitrary"), name="mlstm")(
            m0, proj3, proj3, proj3, proj3, proj3, gate_bias, norm_g, tril, cn0)


def _rope(x, cos, sin_signed):
    lane = lax.broadcasted_iota(jnp.int32, x.shape, 1)
    half = HEAD_DIM // 2
    swapped = jnp.where((lane % HEAD_DIM) < half, pltpu.roll(x, x.shape[1] - half, 1), pltpu.roll(x, half, 1))
    return x * cos + swapped * sin_signed


def _retention_kernel(q_ref, k_ref, v_ref, gg_ref, cos_ref, sin_ref, ng_ref, dec_ref, int_ref, wl_ref, dl_ref,
                      s0_ref, o_ref, s_ref, *, bb):
    t = pl.program_id(1)
    seqs = range(bb)
    pairs = [(b, h) for b in seqs for h in range(N_HEADS)]
    hs = lambda h: slice(h * HEAD_DIM, (h + 1) * HEAD_DIM)

    @pl.when(t == 0)
    def _():
        s_ref[...] = s0_ref[...]

    cos = cos_ref[...]
    sin = sin_ref[...]
    q = [_rope(q_ref[b], cos, sin).astype(BF16) for b in seqs]
    k = [(_rope(k_ref[b], cos, sin) * QK_SCALE).astype(BF16) for b in seqs]
    v = [v_ref[b] for b in seqs]
    s = {(b, h): _dot_nt(q[b][:, hs(h)], k[b][:, hs(h)]) * dec_ref[h] for b, h in pairs}
    qs = {(b, h): _dot(q[b][:, hs(h)], s_ref[b, h].astype(BF16)) for b, h in pairs}
    sv = {(b, h): _dot(s[b, h].astype(BF16), v[b][:, hs(h)].astype(BF16)) for b, h in pairs}
    upd = {(b, h): _dot_tn(k[b][:, hs(h)], (wl_ref[h] * v[b][:, hs(h)]).astype(BF16)) for b, h in pairs}
    o = _head_norm_all({(b, h): int_ref[h] * qs[b, h] + sv[b, h] for b, h in pairs})
    for b, h in pairs:
        s_ref[b, h] = dl_ref[h] * s_ref[b, h] + upd[b, h]
    for b in seqs:
        gg = gg_ref[b]
        y = jnp.concatenate([o[b, h] for h in range(N_HEADS)], axis=1) * ng_ref[...] * (gg * jax.nn.sigmoid(gg))
        o_ref[b] = y.astype(o_ref.dtype)


def _retention_consts(chunk):
    log_g = np.log(1.0 - np.exp2(-5.0 - np.arange(N_HEADS, dtype=np.float64)))
    tau = np.arange(chunk, dtype=np.float64)
    rel = tau[:, None] - tau[None, :]
    decay = np.where(rel >= 0, np.exp(log_g[:, None, None] * np.maximum(rel, 0.0)), 0.0)
    inter = np.exp(log_g[:, None] * (tau + 1.0))[..., None]
    wl = np.exp(log_g[:, None] * (chunk - 1.0 - tau))[..., None]
    dl = np.exp(log_g * chunk)[:, None, None]
    return tuple(jnp.asarray(a, F32) for a in (decay, inter, wl, dl))


def retention_mixer(proj3, cos, sin_signed, norm_g, s0, bb, out_dtype):
    n_batch, seq, _ = proj3.shape
    chunk = math.gcd(seq, CHUNK)
    dec, inter, wl, dl = _retention_consts(chunk)

    def row_spec(cblk):
        return pl.BlockSpec((bb, chunk, GROUP_W), lambda bi, t: (bi, t, cblk))

    def const_spec(shape):
        return pl.BlockSpec(shape, lambda bi, t: (0,) * len(shape))

    pos_spec = pl.BlockSpec((chunk, GROUP_W), lambda bi, t: (t, 0))
    state_spec = pl.BlockSpec((bb, N_HEADS, HEAD_DIM, HEAD_DIM), lambda bi, t: (bi, 0, 0, 0))
    kern = functools.partial(_retention_kernel, bb=bb)
    return pl.pallas_call(
        kern, grid=(n_batch // bb, seq // chunk),
        in_specs=[row_spec(C_RQ), row_spec(C_RK), row_spec(C_RV), row_spec(C_RG), pos_spec, pos_spec,
                  const_spec((1, GROUP_W)), const_spec(dec.shape), const_spec(inter.shape), const_spec(wl.shape),
                  const_spec(dl.shape), state_spec],
        out_specs=[pl.BlockSpec((bb, chunk, GROUP_W), lambda bi, t: (bi, t, 0)), state_spec],
        out_shape=[jax.ShapeDtypeStruct((n_batch, seq, GROUP_W), out_dtype),
                   jax.ShapeDtypeStruct((n_batch, N_HEADS, HEAD_DIM, HEAD_DIM), F32)],
        compiler_params=_cp("arbitrary", "arbitrary"), name="retention")(
            proj3, proj3, proj3, proj3, cos, sin_signed, norm_g, dec, inter, wl, dl, s0)


def _s5_kernel(u_ref, wb_ref, a1_ref, a2_ref, h0_ref, wc_ref, d_ref, gw_ref, gb_ref, o_ref, hl_ref,
               hs_ref, ut_ref, yt_ref, *, nb, tt):
    c = pl.program_id(0)

    @pl.when(c == 0)
    def _():
        hl_ref[...] = h0_ref[...]

    halves = range(GROUP_W // LANES)
    for b in range(nb):
        ub = u_ref[b]
        for hf in halves:
            ut_ref[hf, pl.ds(b, tt, stride=nb), :] = ub[:, hf * LANES:(hf + 1) * LANES]
    u = jnp.concatenate([ut_ref[hf] for hf in halves], axis=1)
    hs_ref[...] = _dot(u.astype(BF16), wb_ref[...])
    a1 = jnp.broadcast_to(a1_ref[...], (nb, 2 * S5_W))
    a2 = jnp.broadcast_to(a2_ref[...], (nb, 2 * S5_W))

    def step(t, h):
        r0 = pl.multiple_of(t * nb, nb)
        swapped = jnp.concatenate([h[:, S5_W:], h[:, :S5_W]], axis=1)
        h = a1 * h + a2 * swapped + hs_ref[pl.ds(r0, nb), :]
        hs_ref[pl.ds(r0, nb), :] = h
        return h

    hl_ref[...] = lax.fori_loop(0, tt, step, hl_ref[...])
    y = _dot(hs_ref[...].astype(BF16), wc_ref[...]) + d_ref[...] * u
    g5 = jax.nn.gelu(y)
    yt = g5 * jax.nn.sigmoid(_dot(g5.astype(BF16), gw_ref[...]) + gb_ref[...])
    for hf in halves:
        yt_ref[hf] = yt[:, hf * LANES:(hf + 1) * LANES]
    for b in range(nb):
        o_ref[b] = jnp.concatenate([yt_ref[hf, pl.ds(b, tt, stride=nb), :] for hf in halves],
                                   axis=1).astype(o_ref.dtype)


def s5_mixer(proj3, tt, wb, a1, a2, h0, wc, d, glu_w, glu_b, out_dtype):
    nb, seq, _ = proj3.shape
    rows = tt * nb

    def const_spec(shape):
        return pl.BlockSpec(shape, lambda c: (0,) * len(shape))

    kern = functools.partial(_s5_kernel, nb=nb, tt=tt)
    return pl.pallas_call(
        kern, grid=(seq // tt,),
        in_specs=[pl.BlockSpec((nb, tt, GROUP_W), lambda c: (0, c, C_SU)),
                  const_spec(wb.shape), const_spec(a1.shape), const_spec(a2.shape), const_spec(h0.shape),
                  const_spec(wc.shape), const_spec(d.shape), const_spec(glu_w.shape), const_spec(glu_b.shape)],
        out_specs=[pl.BlockSpec((nb, tt, GROUP_W), lambda c: (0, c, 0)), const_spec(h0.shape)],
        out_shape=[jax.ShapeDtypeStruct((nb, seq, GROUP_W), out_dtype), jax.ShapeDtypeStruct(h0.shape, F32)],
        scratch_shapes=[pltpu.VMEM((rows, 2 * S5_W), F32), pltpu.VMEM((GROUP_W // LANES, rows, LANES), F32),
                        pltpu.VMEM((GROUP_W // LANES, rows, LANES), F32)],
        compiler_params=_cp("arbitrary"), name="s5")(proj3, wb, a1, a2, h0, wc, d, glu_w, glu_b)


def _s5_weights(a_re, a_im, log_dt, b_re, b_im, c_re, c_im):
    lam = lax.complex(a_re, a_im)
    a_bar = jnp.exp(lam * jnp.exp(log_dt))
    b_bar = ((a_bar - 1.0) / lam)[..., None] * lax.complex(b_re, b_im)
    eye = jnp.eye(S5_GROUPS, dtype=F32)

    def in_map(m):
        return jnp.einsum('gpc,gh->gchp', m, eye).reshape(S5_GROUPS * S5_GROUP, S5_W)

    def out_map(m):
        return jnp.einsum('gcp,gh->gphc', m, eye).reshape(S5_W, S5_GROUPS * S5_GROUP)

    wb = jnp.concatenate([in_map(b_bar.real), in_map(b_bar.imag)], axis=1).astype(BF16)
    wc = jnp.concatenate([out_map(c_re), -out_map(c_im)], axis=0).astype(BF16)
    ar = a_bar.real.reshape(1, S5_W)
    ai = a_bar.imag.reshape(1, S5_W)
    return wb, jnp.concatenate([ar, ar], axis=1), jnp.concatenate([-ai, ai], axis=1), wc


def _cross_attn_kernel(x_ref, wq_ref, k_ref, v_ref, wo_ref, g_ref, b_ref, o_ref):
    x = x_ref[...]
    q = (_dot(x.astype(BF16), wq_ref[...]) * QK_SCALE).astype(BF16)
    k = k_ref[0].astype(BF16)
    v = v_ref[0].astype(BF16)
    hs = [slice(h * HEAD_DIM, (h + 1) * HEAD_DIM) for h in range(N_HEADS)]
    s = [_dot_nt(q[:, sl], k[:, sl]) for sl in hs]
    e = [jnp.exp(a - jnp.max(a, axis=1, keepdims=True)) for a in s]
    p = [a / jnp.sum(a, axis=1, keepdims=True) for a in e]
    o = jnp.concatenate([_dot(p[h].astype(BF16), v[:, hs[h]]) for h in range(N_HEADS)], axis=1)
    y = ALPHA * x + _dot(o.astype(BF16), wo_ref[...])
    o_ref[...] = _layer_norm(y, g_ref[...], b_ref[...])


def cross_attn_ln(x, row_blk0, n_batch, seq, tq, mem_k, mem_v, wq, wo, g, b):
    nq = seq // tq

    def const_spec(shape):
        return pl.BlockSpec(shape, lambda bb, i: (0,) * len(shape))

    row_spec = pl.BlockSpec((tq, D_MODEL), lambda bb, i: (row_blk0 + bb * nq + i, 0))
    mem_spec = pl.BlockSpec((1, N_MEM, GROUP_W), lambda bb, i: (bb, 0, 0))
    return pl.pallas_call(
        _cross_attn_kernel, grid=(n_batch, nq),
        in_specs=[row_spec, const_spec(wq.shape), mem_spec, mem_spec, const_spec(wo.shape),
                  const_spec(g.shape), const_spec(b.shape)],
        out_specs=row_spec, out_shape=jax.ShapeDtypeStruct(x.shape, F32), input_output_aliases={0: 0},
        compiler_params=_cp("arbitrary", "arbitrary"), name="cross_attn")(x, wq, mem_k, mem_v, wo, g, b)


SWIGLU_ROWS = 256


def _swiglu_accumulate(xb_ref, wg, wu, wd, acc_ref, n_valid=None):
    wgb, wub, wdb = wg.astype(BF16), wu.astype(BF16), wd.astype(BF16)
    n_sub = xb_ref.shape[0] // SWIGLU_ROWS

    def hidden(r):
        xb = xb_ref[pl.ds(r * SWIGLU_ROWS, SWIGLU_ROWS), :]
        gate = _dot(xb, wgb)
        up = _dot(xb, wub)
        return (gate * jax.nn.sigmoid(gate) * up).astype(BF16)

    def first_sub_blocks(n):
        hid = hidden(0)
        for r in range(n):
            nxt = hidden(r + 1) if r + 1 < n else None
            acc_ref[pl.ds(r * SWIGLU_ROWS, SWIGLU_ROWS), :] += _dot(hid, wdb)
            hid = nxt

    if n_valid is None:
        first_sub_blocks(n_sub)
        return
    need = (n_valid + SWIGLU_ROWS - 1) // SWIGLU_ROWS
    for n in range(1, n_sub + 1):
        pl.when(need == n)(functools.partial(first_sub_blocks, n))


def _ffn_kernel(x_ref, wg_ref, wu_ref, wd_ref, g_ref, b_ref, o_ref, xb_ref, *, nf):
    j = pl.program_id(1)

    @pl.when(j == 0)
    def _():
        xb_ref[...] = x_ref[...].astype(BF16)
        o_ref[...] = jnp.zeros_like(o_ref)

    _swiglu_accumulate(xb_ref, wg_ref[...], wu_ref[...], wd_ref[...], o_ref)

    @pl.when(j == nf - 1)
    def _():
        o_ref[...] = _layer_norm(ALPHA * x_ref[...] + o_ref[...], g_ref[...], b_ref[...])


def ffn_ln(x, wg, wu, wd, g, b, tm, tf):
    m = x.shape[0]
    nf = D_FF // tf
    kern = functools.partial(_ffn_kernel, nf=nf)
    return pl.pallas_call(
        kern, grid=(m // tm, nf),
        in_specs=[pl.BlockSpec((tm, D_MODEL), lambda i, j: (i, 0)),
                  pl.BlockSpec((D_MODEL, tf), lambda i, j: (0, j)), pl.BlockSpec((D_MODEL, tf), lambda i, j: (0, j)),
                  pl.BlockSpec((tf, D_MODEL), lambda i, j: (j, 0)),
                  pl.BlockSpec((1, D_MODEL), lambda i, j: (0, 0)), pl.BlockSpec((1, D_MODEL), lambda i, j: (0, 0))],
        out_specs=pl.BlockSpec((tm, D_MODEL), lambda i, j: (i, 0)),
        out_shape=jax.ShapeDtypeStruct((m, D_MODEL), F32),
        scratch_shapes=[pltpu.VMEM((tm, D_MODEL), BF16)],
        compiler_params=_cp("arbitrary", "arbitrary"), name="ffn")(x, wg, wu, wd, g, b)


def _router_kernel(x_ref, w_ref, b_ref, o_ref):
    logits = jnp.dot(x_ref[...], w_ref[...], preferred_element_type=F32, precision=lax.Precision.HIGHEST) + b_ref[...]
    lane = lax.broadcasted_iota(jnp.int32, logits.shape, 1)
    neg = jnp.float32(-jnp.inf)
    lg = jnp.where(lane < N_EXPERTS, logits, neg)
    m1 = jnp.max(lg, axis=1, keepdims=True)
    i1 = jnp.min(jnp.where(lg == m1, lane, LANES), axis=1, keepdims=True)
    lg2 = jnp.where(lane == i1, neg, lg)
    m2 = jnp.max(lg2, axis=1, keepdims=True)
    i2 = jnp.min(jnp.where(lg2 == m2, lane, LANES), axis=1, keepdims=True)
    e2 = jnp.exp(m2 - m1)
    g1 = 1.0 / (1.0 + e2)
    g2 = e2 / (1.0 + e2)
    out = jnp.where(lane == 0, i1.astype(F32), jnp.where(lane == 1, i2.astype(F32),
                    jnp.where(lane == 2, g1, jnp.where(lane == 3, g2, 0.0))))
    o_ref[...] = out


def router(x, w_pad, b_pad, tm):
    m = x.shape[0]
    return pl.pallas_call(
        _router_kernel, grid=(m // tm,),
        in_specs=[pl.BlockSpec((tm, D_MODEL), lambda i: (i, 0)), pl.BlockSpec((D_MODEL, LANES), lambda i: (0, 0)),
                  pl.BlockSpec((1, LANES), lambda i: (0, 0))],
        out_specs=pl.BlockSpec((tm, LANES), lambda i: (i, 0)),
        out_shape=jax.ShapeDtypeStruct((m, LANES), F32),
        compiler_params=_cp("arbitrary"), name="router")(x, w_pad, b_pad)


def _moe_ffn_kernel(te_ref, nu_ref, tr_ref, x_ref, wg_ref, wu_ref, wd_ref, o_ref, xb_ref):
    i = pl.program_id(0)
    j = pl.program_id(1)
    used = i < nu_ref[0]

    @pl.when(used)
    def _():
        @pl.when(j == 0)
        def _():
            xb_ref[...] = x_ref[...].astype(BF16)
            o_ref[...] = jnp.zeros_like(o_ref)

        _swiglu_accumulate(xb_ref, wg_ref[0], wu_ref[0], wd_ref[0], o_ref, tr_ref[i])

    @pl.when(jnp.logical_and(jnp.logical_not(used), j == 0))
    def _():
        o_ref[...] = jnp.zeros_like(o_ref)


def moe_ffn(x_sorted, tile_expert, n_used, tile_rows, wg, wu, wd, tf):
    n_rows = x_sorted.shape[0]
    n_tiles = n_rows // MOE_TILE
    nf = D_FF // tf

    def jj(i, j, nu):
        return jnp.where(i < nu[0], j, nf - 1)

    gs = pltpu.PrefetchScalarGridSpec(
        num_scalar_prefetch=3, grid=(n_tiles, nf),
        in_specs=[pl.BlockSpec((MOE_TILE, D_MODEL), lambda i, j, te, nu, tr: (i, 0)),
                  pl.BlockSpec((1, D_MODEL, tf), lambda i, j, te, nu, tr: (te[i], 0, jj(i, j, nu))),
                  pl.BlockSpec((1, D_MODEL, tf), lambda i, j, te, nu, tr: (te[i], 0, jj(i, j, nu))),
                  pl.BlockSpec((1, tf, D_MODEL), lambda i, j, te, nu, tr: (te[i], jj(i, j, nu), 0))],
        out_specs=pl.BlockSpec((MOE_TILE, D_MODEL), lambda i, j, te, nu, tr: (i, 0)),
        scratch_shapes=[pltpu.VMEM((MOE_TILE, D_MODEL), BF16)])
    return pl.pallas_call(
        _moe_ffn_kernel, grid_spec=gs, out_shape=jax.ShapeDtypeStruct((n_rows, D_MODEL), F32),
        compiler_params=_cp("arbitrary", "arbitrary"), name="moe_ffn")(
            tile_expert, n_used, tile_rows, x_sorted, wg, wu, wd)


def _combine_ln_kernel(x_ref, r_ref, ya_ref, yb_ref, g_ref, b_ref, o_ref):
    r = r_ref[...]
    y = r[:, TOP_K:TOP_K + 1] * ya_ref[...] + r[:, TOP_K + 1:TOP_K + 2] * yb_ref[...]
    o_ref[...] = _layer_norm(ALPHA * x_ref[...] + y, g_ref[...], b_ref[...])


def combine_ln(x, r, ya, yb, g, b, tm, row_blk0, rows):
    row = pl.BlockSpec((tm, D_MODEL), lambda i: (row_blk0 + i, 0))
    vec = pl.BlockSpec((1, D_MODEL), lambda i: (0, 0))
    return pl.pallas_call(
        _combine_ln_kernel, grid=(rows // tm,),
        in_specs=[row, pl.BlockSpec((tm, LANES), lambda i: (row_blk0 + i, 0)), row, row, vec, vec],
        out_specs=pl.BlockSpec((tm, D_MODEL), lambda i: (i, 0)),
        out_shape=jax.ShapeDtypeStruct((rows, D_MODEL), F32),
        compiler_params=_cp("arbitrary"), name="combine_ln")(x, r, ya, yb, g, b)


def moe_ln(x, router_w, router_b, wg, wu, wd, g, b, tm, splits):
    m = x.shape[0]
    w_pad = jnp.zeros((D_MODEL, LANES), F32).at[:, :N_EXPERTS].set(router_w)
    b_pad = jnp.zeros((1, LANES), F32).at[0, :N_EXPERTS].set(router_b)
    r = router(x, w_pad, b_pad, tm)
    top_idx = r[:, :TOP_K].astype(jnp.int32)
    n_slot = m * TOP_K
    onehot = (top_idx[:, :, None] == jnp.arange(N_EXPERTS, dtype=jnp.int32)).astype(jnp.int32)
    per_tok = onehot[:, 0] + onehot[:, 1]
    before = jnp.cumsum(per_tok, axis=0) - per_tok
    counts = jnp.sum(per_tok, axis=0)
    tiles_per = (counts + MOE_TILE - 1) // MOE_TILE
    tile_end = jnp.cumsum(tiles_per)
    row0 = (tile_end - tiles_per) * MOE_TILE
    dest = jnp.sum(onehot * (before + row0)[:, None, :], axis=2)
    n_tiles = -(-n_slot // MOE_TILE) + N_EXPERTS
    n_rows = n_tiles * MOE_TILE
    row_tok = (jnp.arange(n_rows, dtype=jnp.int32) % m).at[dest.reshape(-1)].set(
        jnp.arange(n_slot, dtype=jnp.int32) // TOP_K)
    n_used = tile_end[-1:].astype(jnp.int32)
    tile_ids = jnp.minimum(jnp.arange(n_tiles, dtype=jnp.int32), n_used[0] - 1)
    tile_expert = jnp.minimum(jnp.sum((tile_end[None, :] <= tile_ids[:, None]).astype(jnp.int32), axis=1),
                              N_EXPERTS - 1)
    tile_rows = jnp.clip(counts[tile_expert] - (tile_ids - (tile_end - tiles_per)[tile_expert]) * MOE_TILE,
                         0, MOE_TILE).astype(jnp.int32)
    x_sorted = x[row_tok]
    y_sorted = moe_ffn(x_sorted, tile_expert, n_used, tile_rows, wg, wu, wd, tf=512)
    ya, yb = y_sorted[dest[:, 0]], y_sorted[dest[:, 1]]
    return [combine_ln(x, r, ya, yb, g, b, t, blk0, rows) for blk0, rows, t in splits]


def kernel(x_prompt, x_sample, cache_sb_k, cache_sb_v, cache_mem_k, cache_mem_v, state_ml_C, state_ml_n, state_ml_m, state_rt_S, state_s5_re, state_s5_im, page_table, mem_prompt, w_in, sb_bias, ml_b_i, ml_b_f, ml_norm_g, rt_norm_g, s5_A_re, s5_A_im, s5_log_dt, s5_B_re, s5_B_im, s5_C_re, s5_C_im, s5_D, s5_glu_w, s5_glu_b, w_out, ca_wq, ca_wk, ca_wv, ca_wo, ln_g, ln_b, ffn_w_gate, ffn_w_up, ffn_w_down, moe_router_w, moe_router_b, moe_w_gate, moe_w_up, moe_w_down):
    bp, tp, _ = x_prompt.shape
    bs, ts, _ = x_sample.shape
    n_p, n_s = bp * tp, bs * ts
    tm = 640
    assert (n_p + n_s) % (2 * tm) == 0 and n_p % n_s == 0 and tp % 512 == 0 and bp % 8 == 0 and bs % 8 == 0
    x = jnp.concatenate([x_prompt.reshape(n_p, D_MODEL), x_sample.reshape(n_s, D_MODEL)], axis=0)
    uu = _suffix_matrix()
    g_off = 7 * GROUP_W
    half = HEAD_DIM // 2
    freq = ROPE_BASE ** (-jnp.arange(half, dtype=F32) / half)

    def rope_tables(pos):
        ang = pos.astype(F32)[:, None] * freq[None, :]
        cos, sin = jnp.cos(ang), jnp.sin(ang)
        return (jnp.tile(jnp.concatenate([cos, cos], axis=1), (1, N_HEADS)),
                jnp.tile(jnp.concatenate([-sin, sin], axis=1), (1, N_HEADS)))

    cos_p, sin_p = rope_tables(jnp.arange(tp, dtype=jnp.int32))
    cos_s, sin_s = rope_tables(PAST_LEN + jnp.arange(ts, dtype=jnp.int32))
    cache_kt = cache_sb_k.transpose(0, 1, 3, 4, 2)
    cache_vt = cache_sb_v.transpose(0, 1, 3, 4, 2)

    p_st = [[] for _ in range(10)]
    s_st = [[] for _ in range(8)]
    for l in range(DEPTH):
        wl = w_in[l]
        w_cat = jnp.concatenate([wl[:, :g_off], wl[:, g_off + 2 * N_HEADS:], wl[:, g_off:g_off + 2 * N_HEADS],
                                 jnp.zeros((D_MODEL, PROJ_W - wl.shape[1]), F32)], axis=1).astype(BF16)
        proj_p, kt_p, vt_p = in_proj_prompt(x, w_cat, bp, tp, 512)
        proj_s = linear(x, w_cat, n_s, n_p // n_s, n_s)
        proj_p3 = proj_p.reshape(bp, tp, PROJ_W)
        proj_s3 = proj_s.reshape(bs, ts, PROJ_W)
        gate_bias = jnp.zeros((1, LANES), F32).at[0, :2 * N_HEADS].set(jnp.concatenate([ml_b_i[l], ml_b_f[l]]))
        ml_g = ml_norm_g[l][None, :]
        rt_g = rt_norm_g[l][None, :]
        wb, a1, a2, wc = _s5_weights(s5_A_re[l], s5_A_im[l], s5_log_dt[l], s5_B_re[l], s5_B_im[l],
                                     s5_C_re[l], s5_C_im[l])
        s5_d = s5_D[l][None, :]
        glu_w = s5_glu_w[l].astype(BF16)
        glu_b = s5_glu_b[l][None, :]

        o_sb_p = sb_attention_prompt(proj_p, sb_bias[l], uu, bp, tp, tq=256)
        o_ml_p, cn_p, m_p = mlstm_mixer(proj_p3, gate_bias, ml_g, jnp.zeros((bp, N_HEADS, HEAD_DIM, LANES), F32),
                                        jnp.zeros((bp, N_HEADS), F32), 8, MIX_DTYPE)
        o_rt_p, rs_p = retention_mixer(proj_p3, cos_p, sin_p, rt_g,
                                       jnp.zeros((bp, N_HEADS, HEAD_DIM, HEAD_DIM), F32), 8, MIX_DTYPE)
        o_ml_p = o_ml_p.reshape(n_p, GROUP_W)
        o_rt_p = o_rt_p.reshape(n_p, GROUP_W)
        o_s5_p, h5_p = s5_mixer(proj_p3, 64, wb, a1, a2, jnp.zeros((bp, 2 * S5_W), F32), wc, s5_d, glu_w, glu_b,
                                MIX_DTYPE)
        o_s5_p = o_s5_p.reshape(n_p, GROUP_W)

        o_sb_s = sb_attention_sample(proj_s, 0, cache_kt, cache_vt, page_table, l, sb_bias[l], uu, n_pp=16)
        cn0 = jnp.concatenate([state_ml_C[:, l], state_ml_n[:, l][..., None],
                               jnp.zeros((bs, N_HEADS, HEAD_DIM, LANES - HEAD_DIM - 1), F32)], axis=-1)
        o_ml_s, cn_s, m_s = mlstm_mixer(proj_s3, gate_bias, ml_g, cn0, state_ml_m[:, l], 8, F32)
        o_rt_s, rs_s = retention_mixer(proj_s3, cos_s, sin_s, rt_g, state_rt_S[:, l], 8, F32)
        o_ml_s = o_ml_s.reshape(n_s, GROUP_W)
        o_rt_s = o_rt_s.reshape(n_s, GROUP_W)
        h0_s = jnp.concatenate([state_s5_re[:, l].reshape(bs, S5_W), state_s5_im[:, l].reshape(bs, S5_W)], axis=1)
        o_s5_s, h5_s = s5_mixer(proj_s3, ts, wb, a1, a2, h0_s, wc, s5_d, glu_w, glu_b, F32)
        o_s5_s = o_s5_s.reshape(n_s, GROUP_W)

        wo_mix = w_out[l].astype(BF16)
        g0, b0 = ln_g[l, 0][None, :], ln_b[l, 0][None, :]
        x = mix_out_ln(x, (o_sb_p, o_ml_p, o_rt_p, o_s5_p), wo_mix, g0, b0, 512, 0)
        x = mix_out_ln(x, (o_sb_s, o_ml_s, o_rt_s, o_s5_s), wo_mix, g0, b0, n_s, n_p // n_s)

        mem_kv = linear(mem_prompt.reshape(bp * N_MEM, D_MODEL),
                        jnp.concatenate([ca_wk[l], ca_wv[l]], axis=1).astype(BF16), 512)
        mk_p = mem_kv[:, :GROUP_W].reshape(bp, N_MEM, GROUP_W)
        mv_p = mem_kv[:, GROUP_W:].reshape(bp, N_MEM, GROUP_W)
        wq = ca_wq[l].astype(BF16)
        wo = ca_wo[l].astype(BF16)
        g1, b1 = ln_g[l, 1][None, :], ln_b[l, 1][None, :]
        x = cross_attn_ln(x, 0, bp, tp, 512, mk_p, mv_p, wq, wo, g1, b1)
        x = cross_attn_ln(x, n_p // ts, bs, ts, ts, cache_mem_k[:, l].reshape(bs, N_MEM, GROUP_W),
                          cache_mem_v[:, l].reshape(bs, N_MEM, GROUP_W), wq, wo, g1, b1)

        g2, b2 = ln_g[l, 2][None, :], ln_b[l, 2][None, :]
        j = l // 2
        last = l == DEPTH - 1
        if l % 2 == 0:
            x = ffn_ln(x, ffn_w_gate[j], ffn_w_up[j], ffn_w_down[j], g2, b2, 2 * tm, tf=512)
            y_out = (x[:n_p], x[n_p:]) if last else None
        else:
            splits = [(0, n_p, 512), (n_p // n_s, n_s, n_s)] if last else [(0, n_p + n_s, tm)]
            y_out = moe_ln(x, moe_router_w[j], moe_router_b[j], moe_w_gate[j], moe_w_up[j], moe_w_down[j],
                           g2, b2, tm, splits)
            x = None if last else y_out[0]

        def heads(a, nb_, t_):
            return a.reshape(nb_, t_, N_HEADS, HEAD_DIM)

        p_st[0].append(kt_p)
        p_st[1].append(vt_p)
        p_st[2].append(heads(mk_p, bp, N_MEM))
        p_st[3].append(heads(mv_p, bp, N_MEM))
        s_st[0].append(heads(proj_s[:, C_SK * GROUP_W:(C_SK + 1) * GROUP_W], bs, ts))
        s_st[1].append(heads(proj_s[:, C_SV * GROUP_W:(C_SV + 1) * GROUP_W], bs, ts))
        for st, cn, mm, rs, h5, nb_ in ((p_st, cn_p, m_p, rs_p, h5_p, bp), (s_st, cn_s, m_s, rs_s, h5_s, bs)):
            off = 4 if st is p_st else 2
            st[off + 0].append(cn[..., :HEAD_DIM])
            st[off + 1].append(cn[..., HEAD_DIM])
            st[off + 2].append(mm[:, 0, :N_HEADS])
            st[off + 3].append(rs)
            st[off + 4].append(h5[:, :S5_W].reshape(nb_, S5_GROUPS, S5_STATE))
            st[off + 5].append(h5[:, S5_W:].reshape(nb_, S5_GROUPS, S5_STATE))

    y_prompt = y_out[0].reshape(bp, tp, D_MODEL)
    y_sample = y_out[1].reshape(bs, ts, D_MODEL)
    p_out = [jnp.stack(a, axis=1) for a in p_st]
    for i in range(2):
        p_out[i] = p_out[i].reshape(bp, DEPTH, N_HEADS, HEAD_DIM, tp).transpose(0, 1, 4, 2, 3)
    s_out = [jnp.stack(a, axis=1) for a in s_st]
    return (y_prompt, y_sample, *p_out, *s_out)
```

```python
import functools
import math

import numpy as np
import jax
import jax.numpy as jnp
from jax import lax
from jax.experimental import pallas as pl
from jax.experimental.pallas import tpu as pltpu

F32 = jnp.float32
BF16 = jnp.bfloat16

D_MODEL = 1024
DEPTH = 2
PAST_LEN = 8192
PAGE_SIZE = 128
HEAD_DIM = 64
N_HEADS = 4
GROUP_W = N_HEADS * HEAD_DIM
S5_GROUPS = 16
S5_GROUP = 16
S5_STATE = 64
S5_W = S5_GROUPS * S5_STATE
N_MEM = 256
D_FF = 3584
N_EXPERTS = 8
TOP_K = 2
CHUNK = 64
ROPE_BASE = 10000.0
LN_EPS = 1e-5
GN_EPS = 1e-6
ALPHA = (2 * DEPTH) ** 0.25
QK_SCALE = HEAD_DIM ** -0.5
LOG2E = math.log2(math.e)

LANES = 128
PROJ_W = 25 * LANES
C_SQ, C_SK, C_SV, C_MQ, C_MK, C_MV, C_MO, C_RQ, C_RK, C_RV, C_RG, C_SU = range(12)
C_GATES = 12 * GROUP_W // LANES
VMEM_LIMIT = 48 * 1024 * 1024
MOE_TILE = 1024
MIX_DTYPE = BF16


def _cp(*sem):
    return pltpu.CompilerParams(dimension_semantics=sem, vmem_limit_bytes=VMEM_LIMIT)


def _dot(a, b):
    return jnp.dot(a, b, preferred_element_type=F32)


def _dot_nt(a, b):
    return lax.dot_general(a, b, (((1,), (1,)), ((), ())), preferred_element_type=F32)


def _dot_tn(a, b):
    return lax.dot_general(a, b, (((0,), (0,)), ((), ())), preferred_element_type=F32)


def _layer_norm(y, g, b):
    mu = jnp.mean(y, axis=-1, keepdims=True)
    yc = y - mu
    var = jnp.mean(yc * yc, axis=-1, keepdims=True)
    return yc * lax.rsqrt(var + LN_EPS) * g + b


def _row_sum(x, scale=1.0):
    ones = jnp.full((x.shape[1], LANES), scale, BF16)
    hi = x.astype(BF16)
    lo = (x - hi.astype(F32)).astype(BF16)
    return (_dot(hi, ones) + _dot(lo, ones))[:, :x.shape[1]]


def _cumsum_rows(tril, x):
    hi = x.astype(BF16)
    lo = (x - hi.astype(F32)).astype(BF16)
    return _dot(tril, hi) + _dot(tril, lo)


def _head_norm_all(hd):
    inv = 1.0 / HEAD_DIM
    mu = {p: _row_sum(x, inv) for p, x in hd.items()}
    hc = {p: hd[p] - mu[p] for p in hd}
    var = {p: _row_sum(hc[p] * hc[p], inv) for p in hd}
    return {p: hc[p] * lax.rsqrt(var[p] + GN_EPS) for p in hd}


def _neg_softplus(z):
    return -(jnp.maximum(z, 0.0) + jnp.log1p(jnp.exp(-jnp.abs(z))))


def _log_sigmoid(z):
    return _neg_softplus(-z)


def _linear_kernel(x_ref, w_ref, o_ref):
    o_ref[...] = _dot(x_ref[...].astype(BF16), w_ref[...]).astype(o_ref.dtype)


def linear(x, w, tm, row_blk0=0, n_rows=None, out_dtype=F32):
    m, k = x.shape
    m = m if n_rows is None else n_rows
    n = w.shape[1]
    return pl.pallas_call(
        _linear_kernel, grid=(m // tm,),
        in_specs=[pl.BlockSpec((tm, k), lambda i: (row_blk0 + i, 0)), pl.BlockSpec((k, n), lambda i: (0, 0))],
        out_specs=pl.BlockSpec((tm, n), lambda i: (i, 0)),
        out_shape=jax.ShapeDtypeStruct((m, n), out_dtype),
        compiler_params=_cp("arbitrary"), name="linear")(x, w)


def _in_proj_kernel(x_ref, w_ref, o_ref, kt_ref, vt_ref):
    o = _dot(x_ref[...].astype(BF16), w_ref[...])
    o_ref[...] = o
    kt_ref[0] = o[:, C_SK * GROUP_W:(C_SK + 1) * GROUP_W].T
    vt_ref[0] = o[:, C_SV * GROUP_W:(C_SV + 1) * GROUP_W].T


def in_proj_prompt(x, w, n_batch, seq, tm):
    k = x.shape[1]
    n = w.shape[1]
    nt = seq // tm
    t_spec = pl.BlockSpec((1, GROUP_W, tm), lambda i: (i // nt, 0, i % nt))
    t_shape = jax.ShapeDtypeStruct((n_batch, GROUP_W, seq), F32)
    return pl.pallas_call(
        _in_proj_kernel, grid=(n_batch * nt,),
        in_specs=[pl.BlockSpec((tm, k), lambda i: (i, 0)), pl.BlockSpec((k, n), lambda i: (0, 0))],
        out_specs=[pl.BlockSpec((tm, n), lambda i: (i, 0)), t_spec, t_spec],
        out_shape=[jax.ShapeDtypeStruct((n_batch * seq, n), F32), t_shape, t_shape],
        compiler_params=_cp("arbitrary"), name="in_proj")(x, w)


def _mix_out_ln_kernel(x_ref, a_ref, b_ref, c_ref, d_ref, w_ref, g_ref, bias_ref, o_ref):
    h = sum(_dot(p[...].astype(BF16), w_ref[pl.ds(n * GROUP_W, GROUP_W), :])
            for n, p in enumerate((a_ref, b_ref, c_ref, d_ref)))
    o_ref[...] = _layer_norm(ALPHA * x_ref[...] + h, g_ref[...], bias_ref[...])


def mix_out_ln(x, parts, w, g, b, tm, row_blk0):
    rows = parts[0].shape[0]
    row_spec = pl.BlockSpec((tm, D_MODEL), lambda i: (row_blk0 + i, 0))
    part_spec = pl.BlockSpec((tm, GROUP_W), lambda i: (i, 0))
    vec = pl.BlockSpec((1, D_MODEL), lambda i: (0, 0))
    return pl.pallas_call(
        _mix_out_ln_kernel, grid=(rows // tm,),
        in_specs=[row_spec, part_spec, part_spec, part_spec, part_spec,
                  pl.BlockSpec((D_MODEL, D_MODEL), lambda i: (0, 0)), vec, vec],
        out_specs=row_spec, out_shape=jax.ShapeDtypeStruct(x.shape, F32), input_output_aliases={0: 0},
        compiler_params=_cp("arbitrary"), name="mix_out_ln")(x, *parts, w, g, b)


def _suffix_matrix():
    j = np.arange(LANES)
    u = (j[:, None] >= j[None, :]).astype(np.float32)
    uu = np.concatenate([u, np.ones((LANES, LANES), np.float32)], axis=1)
    return jnp.asarray(np.concatenate([uu, uu], axis=0), dtype=BF16)


def _suffix_sums(lr, uu):
    hi = lr.astype(BF16)
    lo = (lr - hi.astype(F32)).astype(BF16)
    r = _dot(jnp.concatenate([hi, lo], axis=1), uu)
    return r[:, :LANES], r[:, LANES:]


def _log2_rem(z2):
    return jnp.minimum(-z2, 0.0) - jnp.log2(1.0 + jnp.exp2(-jnp.abs(z2)))


def _sb_prompt_kernel(bias_ref, q_ref, k_ref, v_ref, uu_ref, o_ref, acc_ref, car_ref, *, tq):
    i = pl.program_id(1)
    tk = LANES
    nsub = tq // tk
    acc_ref[...] = jnp.zeros_like(acc_ref)
    car_ref[...] = jnp.zeros_like(car_ref)
    q = (q_ref[...] * (QK_SCALE * LOG2E)).astype(BF16)
    qh = [q[:, h * HEAD_DIM:(h + 1) * HEAD_DIM] for h in range(N_HEADS)]
    b2 = [bias_ref[h] * LOG2E for h in range(N_HEADS)]
    uu = uu_ref[...]
    row = lax.broadcasted_iota(jnp.int32, (tq, tk), 0)
    col = lax.broadcasted_iota(jnp.int32, (tq, tk), 1)
    heads = range(N_HEADS)

    def block_pair(j_hi, causal_hi, causal_lo):
        kb, vb, z2 = [], [], []
        for d in range(2):
            r0 = pl.multiple_of((j_hi - d) * tk, tk)
            kb.append(k_ref[pl.ds(r0, tk), :].astype(BF16))
            vb.append(v_ref[pl.ds(r0, tk), :].astype(BF16))
            z2.append([_dot_nt(qh[h], kb[d][:, h * HEAD_DIM:(h + 1) * HEAD_DIM]) + b2[h] for h in heads])
        cs, tot = [], []
        for d, causal in enumerate((causal_hi, causal_lo)):
            lr = [_log2_rem(z) for z in z2[d]]
            if causal is not None:
                lr = [jnp.where(causal, a, 0.0) for a in lr]
            c, t = _suffix_sums(jnp.concatenate(lr, axis=0), uu)
            cs.append(c)
            tot.append(t)
        car = [car_ref[h] for h in heads]
        pv = []
        for d, causal in enumerate((causal_hi, causal_lo)):
            w = [jnp.exp2(z2[d][h] + cs[d][h * tq:(h + 1) * tq] + car[h]) for h in heads]
            if causal is not None:
                w = [jnp.where(causal, a, 0.0) for a in w]
            pv.append([_dot(w[h].astype(BF16), vb[d][:, h * HEAD_DIM:(h + 1) * HEAD_DIM]) for h in heads])
            car = [car[h] + tot[d][h * tq:(h + 1) * tq] for h in heads]
        for h in heads:
            acc_ref[h] += pv[0][h] + pv[1][h]
            car_ref[h] = car[h]

    assert nsub == 2
    block_pair(i * nsub + 1, (col + tk) < row, col < row)

    def body(jj, carry):
        block_pair(i * nsub - 1 - 2 * jj, None, None)
        return carry

    lax.fori_loop(0, i, body, 0)
    o_ref[...] = jnp.concatenate([acc_ref[h] for h in range(N_HEADS)], axis=1).astype(o_ref.dtype)


def sb_attention_prompt(proj, sb_bias, uu, n_batch, seq, tq):
    nq = seq // tq
    kern = functools.partial(_sb_prompt_kernel, tq=tq)
    return pl.pallas_call(
        kern, grid=(n_batch, nq),
        in_specs=[pl.BlockSpec(memory_space=pltpu.SMEM),
                  pl.BlockSpec((tq, GROUP_W), lambda b, i: (b * nq + i, C_SQ)),
                  pl.BlockSpec((seq, GROUP_W), lambda b, i: (b, C_SK)),
                  pl.BlockSpec((seq, GROUP_W), lambda b, i: (b, C_SV)),
                  pl.BlockSpec((2 * LANES, 2 * LANES), lambda b, i: (0, 0))],
        out_specs=pl.BlockSpec((tq, GROUP_W), lambda b, i: (b * nq + i, 0)),
        out_shape=jax.ShapeDtypeStruct((n_batch * seq, GROUP_W), MIX_DTYPE),
        scratch_shapes=[pltpu.VMEM((N_HEADS, tq, HEAD_DIM), F32), pltpu.VMEM((N_HEADS, tq, LANES), F32)],
        compiler_params=_cp("arbitrary", "arbitrary"), name="sb_prompt")(sb_bias, proj, proj, proj, uu)


def _sb_sample_kernel(pt_ref, bias_ref, q_ref, kn_ref, vn_ref, u8_ref, uu_ref, *rest, n_pp, n_steps):
    k_refs = rest[:n_pp]
    v_refs = rest[n_pp:2 * n_pp]
    o_ref = rest[2 * n_pp]
    acc_ref, car_ref = rest[2 * n_pp + 1:]
    s = pl.program_id(1)
    nq = q_ref.shape[0]
    rows = N_HEADS * nq
    q = (q_ref[...] * (QK_SCALE * LOG2E)).astype(BF16)
    row_head = lax.broadcasted_iota(jnp.int32, (rows, GROUP_W), 0) // nq
    own = row_head == lax.broadcasted_iota(jnp.int32, (rows, GROUP_W), 1) // HEAD_DIM
    q_bd = jnp.where(own, jnp.concatenate([q] * N_HEADS, axis=0), jnp.zeros((), BF16))
    row_head_l = lax.broadcasted_iota(jnp.int32, (rows, LANES), 0) // nq
    b2 = jnp.zeros((rows, LANES), F32)
    for h in range(N_HEADS):
        b2 = jnp.where(row_head_l == h, bias_ref[h] * LOG2E, b2)

    @pl.when(s == 0)
    def _():
        kn = kn_ref[...].astype(BF16)
        vn = vn_ref[...].astype(BF16)
        t = lax.broadcasted_iota(jnp.int32, (rows, nq), 0) % nq
        causal = lax.broadcasted_iota(jnp.int32, (rows, nq), 1) < t
        z2 = _dot_nt(q_bd, kn) + b2[:, :nq]
        lr = jnp.where(causal, _log2_rem(z2), 0.0)
        cs = jnp.dot(lr, u8_ref[...], preferred_element_type=F32, precision=lax.Precision.HIGHEST)
        w = jnp.where(causal, jnp.exp2(z2 + cs), 0.0)
        acc_ref[...] = _dot(w.astype(BF16), vn)
        car_ref[...] = jnp.broadcast_to(jnp.sum(lr, axis=1, keepdims=True), (rows, LANES))

    z2s = [_dot(q_bd, k_refs[p][0, 0].reshape(GROUP_W, PAGE_SIZE).astype(BF16)) + b2 for p in range(n_pp)]
    cs_all, tot_all = _suffix_sums(_log2_rem(jnp.concatenate(z2s, axis=0)), uu_ref[...])
    car = car_ref[...]
    acc = acc_ref[...]
    for p in range(n_pp):
        w = jnp.exp2(z2s[p] + cs_all[p * rows:(p + 1) * rows] + car)
        acc = acc + _dot_nt(w.astype(BF16), v_refs[p][0, 0].reshape(GROUP_W, PAGE_SIZE).astype(BF16))
        car = car + tot_all[p * rows:(p + 1) * rows]
    car_ref[...] = car
    acc_ref[...] = acc

    @pl.when(s == n_steps - 1)
    def _():
        kept = jnp.where(own, acc, 0.0)
        o_ref[...] = sum(kept[h * nq:(h + 1) * nq] for h in range(N_HEADS))


def sb_attention_sample(proj, row_blk0, cache_kt, cache_vt, page_table, layer, sb_bias, uu, n_pp):
    n_batch, n_pages = page_table.shape
    nq = 8
    n_steps = n_pages // n_pp
    u8 = jnp.asarray((np.arange(nq)[:, None] >= np.arange(nq)[None, :]).astype(np.float32))

    def page_spec(p):
        return pl.BlockSpec((1, 1, N_HEADS, HEAD_DIM, PAGE_SIZE),
                            lambda b, s, pt: (pt[b, n_pages - 1 - (s * n_pp + p)], layer, 0, 0, 0))

    def row_spec(cblk):
        return pl.BlockSpec((nq, GROUP_W), lambda b, s, pt: (row_blk0 + b, cblk))

    kern = functools.partial(_sb_sample_kernel, n_pp=n_pp, n_steps=n_steps)
    gs = pltpu.PrefetchScalarGridSpec(
        num_scalar_prefetch=1, grid=(n_batch, n_steps),
        in_specs=[pl.BlockSpec(memory_space=pltpu.SMEM), row_spec(C_SQ), row_spec(C_SK), row_spec(C_SV),
                  pl.BlockSpec((nq, nq), lambda b, s, pt: (0, 0)),
                  pl.BlockSpec((2 * LANES, 2 * LANES), lambda b, s, pt: (0, 0))]
                 + [page_spec(p) for p in range(n_pp)] * 2,
        out_specs=pl.BlockSpec((nq, GROUP_W), lambda b, s, pt: (b, 0)),
        scratch_shapes=[pltpu.VMEM((N_HEADS * nq, GROUP_W), F32), pltpu.VMEM((N_HEADS * nq, LANES), F32)])
    return pl.pallas_call(
        kern, grid_spec=gs, out_shape=jax.ShapeDtypeStruct((n_batch * nq, GROUP_W), F32),
        compiler_params=_cp("arbitrary", "arbitrary"), name="sb_sample")(
            page_table, sb_bias, proj, proj, proj, u8, uu, *([cache_kt] * n_pp), *([cache_vt] * n_pp))


def _mlstm_kernel(m0_ref, q_ref, k_ref, v_ref, og_ref, gt_ref, gb_ref, ng_ref, tril_ref, cn0_ref,
                  o_ref, cn_ref, m_ref, ms_ref, *, bb, chunk, n_t):
    bi = pl.program_id(0)
    t = pl.program_id(1)
    seqs = range(bb)
    pairs = [(b, h) for b in seqs for h in range(N_HEADS)]

    @pl.when(t == 0)
    def _():
        cn_ref[...] = cn0_ref[...]
        for b, h in pairs:
            ms_ref[b * N_HEADS + h] = jnp.full((1, LANES), m0_ref[bi * bb + b, h], F32)

    tril = tril_ref[...]
    tri_mask = lax.broadcasted_iota(jnp.int32, (chunk, chunk), 1) <= lax.broadcasted_iota(jnp.int32, (chunk, chunk), 0)
    lane = lax.broadcasted_iota(jnp.int32, (chunk, HEAD_DIM), 1)
    ones_col = jnp.where(lane == 0, 1.0, 0.0).astype(F32)
    hs = lambda h: slice(h * HEAD_DIM, (h + 1) * HEAD_DIM)

    gt = [gt_ref[b] + gb_ref[...] for b in seqs]
    bc = [_cumsum_rows(tril, _log_sigmoid(g)) for g in gt]
    gt_t = [g.T for g in gt]
    bc_t = [x.T for x in bc]
    q = [q_ref[b].astype(BF16) for b in seqs]
    k = [(k_ref[b] * QK_SCALE).astype(BF16) for b in seqs]
    v = [v_ref[b] for b in seqs]
    m_prev = {p: ms_ref[p[0] * N_HEADS + p[1]][:, :1] for p in pairs}
    ig_col = {(b, h): gt[b][:, h:h + 1] for b, h in pairs}
    bc_col = {(b, h): bc[b][:, N_HEADS + h:N_HEADS + h + 1] for b, h in pairs}
    dm = {(b, h): jnp.where(tri_mask, bc_col[b, h] - (bc_t[b][N_HEADS + h:N_HEADS + h + 1, :] - gt_t[b][h:h + 1, :]),
                            -jnp.inf) for b, h in pairs}
    a = {p: bc_col[p] + m_prev[p] for p in pairs}
    m_new = {p: jnp.maximum(a[p], jnp.max(dm[p], axis=1, keepdims=True)) for p in pairs}
    inter = {p: jnp.exp(a[p] - m_new[p]) for p in pairs}
    s = {(b, h): _dot_nt(q[b][:, hs(h)], k[b][:, hs(h)]) * jnp.exp(dm[b, h] - m_new[b, h]) for b, h in pairs}
    v_ext = {(b, h): jnp.concatenate([v[b][:, hs(h)], ones_col], axis=1) for b, h in pairs}
    qc = {(b, h): _dot(q[b][:, hs(h)], cn_ref[b, h].astype(BF16)) for b, h in pairs}
    sv = {p: _dot(s[p].astype(BF16), v_ext[p].astype(BF16)) for p in pairs}
    m_last = {p: m_new[p][chunk - 1:chunk, :] for p in pairs}
    wl = {p: jnp.exp(bc_col[p][chunk - 1:chunk, :] - bc_col[p] + ig_col[p] - m_last[p]) for p in pairs}
    dl = {p: jnp.exp(a[p][chunk - 1:chunk, :] - m_last[p]) for p in pairs}
    upd = {(b, h): _dot_tn(k[b][:, hs(h)], (wl[b, h] * v_ext[b, h]).astype(BF16)) for b, h in pairs}
    rs = {p: _row_sum(s[p])[:, :1] for p in pairs}
    num = {p: inter[p] * qc[p][:, :HEAD_DIM] + sv[p][:, :HEAD_DIM] for p in pairs}
    den = {p: inter[p] * qc[p][:, HEAD_DIM:HEAD_DIM + 1] + rs[p] for p in pairs}
    hh = _head_norm_all({p: num[p] / jnp.maximum(jnp.abs(den[p]), jnp.exp(-m_new[p])) for p in pairs})
    for b, h in pairs:
        cn_ref[b, h] = dl[b, h] * cn_ref[b, h] + upd[b, h]
        ms_ref[b * N_HEADS + h] = jnp.broadcast_to(m_last[b, h], (1, LANES))
    for b in seqs:
        y = jnp.concatenate([hh[b, h] for h in range(N_HEADS)], axis=1)
        o_ref[b] = (y * ng_ref[...] * jax.nn.sigmoid(og_ref[b])).astype(o_ref.dtype)

    @pl.when(t == n_t - 1)
    def _():
        lane_m = lax.broadcasted_iota(jnp.int32, (1, LANES), 1)
        for b in seqs:
            m_out = jnp.zeros((1, LANES), F32)
            for h in range(N_HEADS):
                m_out = jnp.where(lane_m == h, ms_ref[b * N_HEADS + h], m_out)
            m_ref[b] = m_out


def mlstm_mixer(proj3, gate_bias, norm_g, cn0, m0, bb, out_dtype):
    n_batch, seq, _ = proj3.shape
    chunk = math.gcd(seq, CHUNK)
    n_t = seq // chunk
    tril = jnp.asarray(np.tril(np.ones((chunk, chunk), np.float32)), dtype=BF16)

    def row_spec(cblk, w=GROUP_W):
        return pl.BlockSpec((bb, chunk, w), lambda bi, t: (bi, t, cblk))

    def const_spec(shape):
        return pl.BlockSpec(shape, lambda bi, t: (0,) * len(shape))

    state_spec = pl.BlockSpec((bb, N_HEADS, HEAD_DIM, LANES), lambda bi, t: (bi, 0, 0, 0))
    kern = functools.partial(_mlstm_kernel, bb=bb, chunk=chunk, n_t=n_t)
    return pl.pallas_call(
        kern, grid=(n_batch // bb, n_t),
        in_specs=[pl.BlockSpec(memory_space=pltpu.SMEM),
                  row_spec(C_MQ), row_spec(C_MK), row_spec(C_MV), row_spec(C_MO), row_spec(C_GATES, LANES),
                  const_spec((1, LANES)), const_spec((1, GROUP_W)), const_spec((chunk, chunk)), state_spec],
        out_specs=[pl.BlockSpec((bb, chunk, GROUP_W), lambda bi, t: (bi, t, 0)), state_spec,
                   pl.BlockSpec((bb, 1, LANES), lambda bi, t: (bi, 0, 0))],
        out_shape=[jax.ShapeDtypeStruct((n_batch, seq, GROUP_W), out_dtype),
                   jax.ShapeDtypeStruct((n_batch, N_HEADS, HEAD_DIM, LANES), F32),
                   jax.ShapeDtypeStruct((n_batch, 1, LANES), F32)],
        scratch_shapes=[pltpu.VMEM((bb * N_HEADS, 1, LANES), F32)],
        compiler_params=_cp("arbitrary", "arbitrary"), name="mlstm")(
            m0, proj3, proj3, proj3, proj3, proj3, gate_bias, norm_g, tril, cn0)


def _rope(x, cos, sin_signed):
    lane = lax.broadcasted_iota(jnp.int32, x.shape, 1)
    half = HEAD_DIM // 2
    swapped = jnp.where((lane % HEAD_DIM) < half, pltpu.roll(x, x.shape[1] - half, 1), pltpu.roll(x, half, 1))
    return x * cos + swapped * sin_signed


def _retention_kernel(q_ref, k_ref, v_ref, gg_ref, cos_ref, sin_ref, ng_ref, dec_ref, int_ref, wl_ref, dl_ref,
                      s0_ref, o_ref, s_ref, *, bb):
    t = pl.program_id(1)
    seqs = range(bb)
    pairs = [(b, h) for b in seqs for h in range(N_HEADS)]
    hs = lambda h: slice(h * HEAD_DIM, (h + 1) * HEAD_DIM)

    @pl.when(t == 0)
    def _():
        s_ref[...] = s0_ref[...]

    cos = cos_ref[...]
    sin = sin_ref[...]
    q = [_rope(q_ref[b], cos, sin).astype(BF16) for b in seqs]
    k = [(_rope(k_ref[b], cos, sin) * QK_SCALE).astype(BF16) for b in seqs]
    v = [v_ref[b] for b in seqs]
    s = {(b, h): _dot_nt(q[b][:, hs(h)], k[b][:, hs(h)]) * dec_ref[h] for b, h in pairs}
    qs = {(b, h): _dot(q[b][:, hs(h)], s_ref[b, h].astype(BF16)) for b, h in pairs}
    sv = {(b, h): _dot(s[b, h].astype(BF16), v[b][:, hs(h)].astype(BF16)) for b, h in pairs}
    upd = {(b, h): _dot_tn(k[b][:, hs(h)], (wl_ref[h] * v[b][:, hs(h)]).astype(BF16)) for b, h in pairs}
    o = _head_norm_all({(b, h): int_ref[h] * qs[b, h] + sv[b, h] for b, h in pairs})
    for b, h in pairs:
        s_ref[b, h] = dl_ref[h] * s_ref[b, h] + upd[b, h]
    for b in seqs:
        gg = gg_ref[b]
        y = jnp.concatenate([o[b, h] for h in range(N_HEADS)], axis=1) * ng_ref[...] * (gg * jax.nn.sigmoid(gg))
        o_ref[b] = y.astype(o_ref.dtype)


def _retention_consts(chunk):
    log_g = np.log(1.0 - np.exp2(-5.0 - np.arange(N_HEADS, dtype=np.float64)))
    tau = np.arange(chunk, dtype=np.float64)
    rel = tau[:, None] - tau[None, :]
    decay = np.where(rel >= 0, np.exp(log_g[:, None, None] * np.maximum(rel, 0.0)), 0.0)
    inter = np.exp(log_g[:, None] * (tau + 1.0))[..., None]
    wl = np.exp(log_g[:, None] * (chunk - 1.0 - tau))[..., None]
    dl = np.exp(log_g * chunk)[:, None, None]
    return tuple(jnp.asarray(a, F32) for a in (decay, inter, wl, dl))


def retention_mixer(proj3, cos, sin_signed, norm_g, s0, bb, out_dtype):
    n_batch, seq, _ = proj3.shape
    chunk = math.gcd(seq, CHUNK)
    dec, inter, wl, dl = _retention_consts(chunk)

    def row_spec(cblk):
        return pl.BlockSpec((bb, chunk, GROUP_W), lambda bi, t: (bi, t, cblk))

    def const_spec(shape):
        return pl.BlockSpec(shape, lambda bi, t: (0,) * len(shape))

    pos_spec = pl.BlockSpec((chunk, GROUP_W), lambda bi, t: (t, 0))
    state_spec = pl.BlockSpec((bb, N_HEADS, HEAD_DIM, HEAD_DIM), lambda bi, t: (bi, 0, 0, 0))
    kern = functools.partial(_retention_kernel, bb=bb)
    return pl.pallas_call(
        kern, grid=(n_batch // bb, seq // chunk),
        in_specs=[row_spec(C_RQ), row_spec(C_RK), row_spec(C_RV), row_spec(C_RG), pos_spec, pos_spec,
                  const_spec((1, GROUP_W)), const_spec(dec.shape), const_spec(inter.shape), const_spec(wl.shape),
                  const_spec(dl.shape), state_spec],
        out_specs=[pl.BlockSpec((bb, chunk, GROUP_W), lambda bi, t: (bi, t, 0)), state_spec],
        out_shape=[jax.ShapeDtypeStruct((n_batch, seq, GROUP_W), out_dtype),
                   jax.ShapeDtypeStruct((n_batch, N_HEADS, HEAD_DIM, HEAD_DIM), F32)],
        compiler_params=_cp("arbitrary", "arbitrary"), name="retention")(
            proj3, proj3, proj3, proj3, cos, sin_signed, norm_g, dec, inter, wl, dl, s0)


def _s5_kernel(u_ref, wb_ref, a1_ref, a2_ref, h0_ref, wc_ref, d_ref, gw_ref, gb_ref, o_ref, hl_ref,
               hs_ref, ut_ref, yt_ref, *, nb, tt):
    c = pl.program_id(0)

    @pl.when(c == 0)
    def _():
        hl_ref[...] = h0_ref[...]

    halves = range(GROUP_W // LANES)
    for b in range(nb):
        ub = u_ref[b]
        for hf in halves:
            ut_ref[hf, pl.ds(b, tt, stride=nb), :] = ub[:, hf * LANES:(hf + 1) * LANES]
    u = jnp.concatenate([ut_ref[hf] for hf in halves], axis=1)
    hs_ref[...] = _dot(u.astype(BF16), wb_ref[...])
    a1 = jnp.broadcast_to(a1_ref[...], (nb, 2 * S5_W))
    a2 = jnp.broadcast_to(a2_ref[...], (nb, 2 * S5_W))

    def step(t, h):
        r0 = pl.multiple_of(t * nb, nb)
        swapped = jnp.concatenate([h[:, S5_W:], h[:, :S5_W]], axis=1)
        h = a1 * h + a2 * swapped + hs_ref[pl.ds(r0, nb), :]
        hs_ref[pl.ds(r0, nb), :] = h
        return h

    hl_ref[...] = lax.fori_loop(0, tt, step, hl_ref[...])
    y = _dot(hs_ref[...].astype(BF16), wc_ref[...]) + d_ref[...] * u
    g5 = jax.nn.gelu(y)
    yt = g5 * jax.nn.sigmoid(_dot(g5.astype(BF16), gw_ref[...]) + gb_ref[...])
    for hf in halves:
        yt_ref[hf] = yt[:, hf * LANES:(hf + 1) * LANES]
    for b in range(nb):
        o_ref[b] = jnp.concatenate([yt_ref[hf, pl.ds(b, tt, stride=nb), :] for hf in halves],
                                   axis=1).astype(o_ref.dtype)


def s5_mixer(proj3, tt, wb, a1, a2, h0, wc, d, glu_w, glu_b, out_dtype):
    nb, seq, _ = proj3.shape
    rows = tt * nb

    def const_spec(shape):
        return pl.BlockSpec(shape, lambda c: (0,) * len(shape))

    kern = functools.partial(_s5_kernel, nb=nb, tt=tt)
    return pl.pallas_call(
        kern, grid=(seq // tt,),
        in_specs=[pl.BlockSpec((nb, tt, GROUP_W), lambda c: (0, c, C_SU)),
                  const_spec(wb.shape), const_spec(a1.shape), const_spec(a2.shape), const_spec(h0.shape),
                  const_spec(wc.shape), const_spec(d.shape), const_spec(glu_w.shape), const_spec(glu_b.shape)],
        out_specs=[pl.BlockSpec((nb, tt, GROUP_W), lambda c: (0, c, 0)), const_spec(h0.shape)],
        out_shape=[jax.ShapeDtypeStruct((nb, seq, GROUP_W), out_dtype), jax.ShapeDtypeStruct(h0.shape, F32)],
        scratch_shapes=[pltpu.VMEM((rows, 2 * S5_W), F32), pltpu.VMEM((GROUP_W // LANES, rows, LANES), F32),
                        pltpu.VMEM((GROUP_W // LANES, rows, LANES), F32)],
        compiler_params=_cp("arbitrary"), name="s5")(proj3, wb, a1, a2, h0, wc, d, glu_w, glu_b)


def _s5_weights(a_re, a_im, log_dt, b_re, b_im, c_re, c_im):
    lam = lax.complex(a_re, a_im)
    a_bar = jnp.exp(lam * jnp.exp(log_dt))
    b_bar = ((a_bar - 1.0) / lam)[..., None] * lax.complex(b_re, b_im)
    eye = jnp.eye(S5_GROUPS, dtype=F32)

    def in_map(m):
        return jnp.einsum('gpc,gh->gchp', m, eye).reshape(S5_GROUPS * S5_GROUP, S5_W)

    def out_map(m):
        return jnp.einsum('gcp,gh->gphc', m, eye).reshape(S5_W, S5_GROUPS * S5_GROUP)

    wb = jnp.concatenate([in_map(b_bar.real), in_map(b_bar.imag)], axis=1).astype(BF16)
    wc = jnp.concatenate([out_map(c_re), -out_map(c_im)], axis=0).astype(BF16)
    ar = a_bar.real.reshape(1, S5_W)
    ai = a_bar.imag.reshape(1, S5_W)
    return wb, jnp.concatenate([ar, ar], axis=1), jnp.concatenate([-ai, ai], axis=1), wc


def _cross_attn_kernel(x_ref, wq_ref, k_ref, v_ref, wo_ref, g_ref, b_ref, o_ref):
    x = x_ref[...]
    q = (_dot(x.astype(BF16), wq_ref[...]) * QK_SCALE).astype(BF16)
    k = k_ref[0].astype(BF16)
    v = v_ref[0].astype(BF16)
    hs = [slice(h * HEAD_DIM, (h + 1) * HEAD_DIM) for h in range(N_HEADS)]
    s = [_dot_nt(q[:, sl], k[:, sl]) for sl in hs]
    e = [jnp.exp(a - jnp.max(a, axis=1, keepdims=True)) for a in s]
    p = [a / jnp.sum(a, axis=1, keepdims=True) for a in e]
    o = jnp.concatenate([_dot(p[h].astype(BF16), v[:, hs[h]]) for h in range(N_HEADS)], axis=1)
    y = ALPHA * x + _dot(o.astype(BF16), wo_ref[...])
    o_ref[...] = _layer_norm(y, g_ref[...], b_ref[...])


def cross_attn_ln(x, row_blk0, n_batch, seq, tq, mem_k, mem_v, wq, wo, g, b):
    nq = seq // tq

    def const_spec(shape):
        return pl.BlockSpec(shape, lambda bb, i: (0,) * len(shape))

    row_spec = pl.BlockSpec((tq, D_MODEL), lambda bb, i: (row_blk0 + bb * nq + i, 0))
    mem_spec = pl.BlockSpec((1, N_MEM, GROUP_W), lambda bb, i: (bb, 0, 0))
    return pl.pallas_call(
        _cross_attn_kernel, grid=(n_batch, nq),
        in_specs=[row_spec, const_spec(wq.shape), mem_spec, mem_spec, const_spec(wo.shape),
                  const_spec(g.shape), const_spec(b.shape)],
        out_specs=row_spec, out_shape=jax.ShapeDtypeStruct(x.shape, F32), input_output_aliases={0: 0},
        compiler_params=_cp("arbitrary", "arbitrary"), name="cross_attn")(x, wq, mem_k, mem_v, wo, g, b)


SWIGLU_ROWS = 256


def _swiglu_accumulate(xb_ref, wg, wu, wd, acc_ref, n_valid=None):
    wgb, wub, wdb = wg.astype(BF16), wu.astype(BF16), wd.astype(BF16)
    n_sub = xb_ref.shape[0] // SWIGLU_ROWS

    def hidden(r):
        xb = xb_ref[pl.ds(r * SWIGLU_ROWS, SWIGLU_ROWS), :]
        gate = _dot(xb, wgb)
        up = _dot(xb, wub)
        return (gate * jax.nn.sigmoid(gate) * up).astype(BF16)

    def first_sub_blocks(n):
        hid = hidden(0)
        for r in range(n):
            nxt = hidden(r + 1) if r + 1 < n else None
            acc_ref[pl.ds(r * SWIGLU_ROWS, SWIGLU_ROWS), :] += _dot(hid, wdb)
            hid = nxt

    if n_valid is None:
        first_sub_blocks(n_sub)
        return
    need = (n_valid + SWIGLU_ROWS - 1) // SWIGLU_ROWS
    for n in range(1, n_sub + 1):
        pl.when(need == n)(functools.partial(first_sub_blocks, n))


def _ffn_kernel(x_ref, wg_ref, wu_ref, wd_ref, g_ref, b_ref, o_ref, xb_ref, *, nf):
    j = pl.program_id(1)

    @pl.when(j == 0)
    def _():
        xb_ref[...] = x_ref[...].astype(BF16)
        o_ref[...] = jnp.zeros_like(o_ref)

    _swiglu_accumulate(xb_ref, wg_ref[...], wu_ref[...], wd_ref[...], o_ref)

    @pl.when(j == nf - 1)
    def _():
        o_ref[...] = _layer_norm(ALPHA * x_ref[...] + o_ref[...], g_ref[...], b_ref[...])


def ffn_ln(x, wg, wu, wd, g, b, tm, tf):
    m = x.shape[0]
    nf = D_FF // tf
    kern = functools.partial(_ffn_kernel, nf=nf)
    return pl.pallas_call(
        kern, grid=(m // tm, nf),
        in_specs=[pl.BlockSpec((tm, D_MODEL), lambda i, j: (i, 0)),
                  pl.BlockSpec((D_MODEL, tf), lambda i, j: (0, j)), pl.BlockSpec((D_MODEL, tf), lambda i, j: (0, j)),
                  pl.BlockSpec((tf, D_MODEL), lambda i, j: (j, 0)),
                  pl.BlockSpec((1, D_MODEL), lambda i, j: (0, 0)), pl.BlockSpec((1, D_MODEL), lambda i, j: (0, 0))],
        out_specs=pl.BlockSpec((tm, D_MODEL), lambda i, j: (i, 0)),
        out_shape=jax.ShapeDtypeStruct((m, D_MODEL), F32),
        scratch_shapes=[pltpu.VMEM((tm, D_MODEL), BF16)],
        compiler_params=_cp("arbitrary", "arbitrary"), name="ffn")(x, wg, wu, wd, g, b)


def _router_kernel(x_ref, w_ref, b_ref, o_ref):
    logits = jnp.dot(x_ref[...], w_ref[...], preferred_element_type=F32, precision=lax.Precision.HIGHEST) + b_ref[...]
    lane = lax.broadcasted_iota(jnp.int32, logits.shape, 1)
    neg = jnp.float32(-jnp.inf)
    lg = jnp.where(lane < N_EXPERTS, logits, neg)
    m1 = jnp.max(lg, axis=1, keepdims=True)
    i1 = jnp.min(jnp.where(lg == m1, lane, LANES), axis=1, keepdims=True)
    lg2 = jnp.where(lane == i1, neg, lg)
    m2 = jnp.max(lg2, axis=1, keepdims=True)
    i2 = jnp.min(jnp.where(lg2 == m2, lane, LANES), axis=1, keepdims=True)
    e2 = jnp.exp(m2 - m1)
    g1 = 1.0 / (1.0 + e2)
    g2 = e2 / (1.0 + e2)
    out = jnp.where(lane == 0, i1.astype(F32), jnp.where(lane == 1, i2.astype(F32),
                    jnp.where(lane == 2, g1, jnp.where(lane == 3, g2, 0.0))))
    o_ref[...] = out


def router(x, w_pad, b_pad, tm):
    m = x.shape[0]
    return pl.pallas_call(
        _router_kernel, grid=(m // tm,),
        in_specs=[pl.BlockSpec((tm, D_MODEL), lambda i: (i, 0)), pl.BlockSpec((D_MODEL, LANES), lambda i: (0, 0)),
                  pl.BlockSpec((1, LANES), lambda i: (0, 0))],
        out_specs=pl.BlockSpec((tm, LANES), lambda i: (i, 0)),
        out_shape=jax.ShapeDtypeStruct((m, LANES), F32),
        compiler_params=_cp("arbitrary"), name="router")(x, w_pad, b_pad)


def _moe_ffn_kernel(te_ref, nu_ref, tr_ref, x_ref, wg_ref, wu_ref, wd_ref, o_ref, xb_ref):
    i = pl.program_id(0)
    j = pl.program_id(1)
    used = i < nu_ref[0]

    @pl.when(used)
    def _():
        @pl.when(j == 0)
        def _():
            xb_ref[...] = x_ref[...].astype(BF16)
            o_ref[...] = jnp.zeros_like(o_ref)

        _swiglu_accumulate(xb_ref, wg_ref[0], wu_ref[0], wd_ref[0], o_ref, tr_ref[i])

    @pl.when(jnp.logical_and(jnp.logical_not(used), j == 0))
    def _():
        o_ref[...] = jnp.zeros_like(o_ref)


def moe_ffn(x_sorted, tile_expert, n_used, tile_rows, wg, wu, wd, tf):
    n_rows = x_sorted.shape[0]
    n_tiles = n_rows // MOE_TILE
    nf = D_FF // tf

    def jj(i, j, nu):
        return jnp.where(i < nu[0], j, nf - 1)

    gs = pltpu.PrefetchScalarGridSpec(
        num_scalar_prefetch=3, grid=(n_tiles, nf),
        in_specs=[pl.BlockSpec((MOE_TILE, D_MODEL), lambda i, j, te, nu, tr: (i, 0)),
                  pl.BlockSpec((1, D_MODEL, tf), lambda i, j, te, nu, tr: (te[i], 0, jj(i, j, nu))),
                  pl.BlockSpec((1, D_MODEL, tf), lambda i, j, te, nu, tr: (te[i], 0, jj(i, j, nu))),
                  pl.BlockSpec((1, tf, D_MODEL), lambda i, j, te, nu, tr: (te[i], jj(i, j, nu), 0))],
        out_specs=pl.BlockSpec((MOE_TILE, D_MODEL), lambda i, j, te, nu, tr: (i, 0)),
        scratch_shapes=[pltpu.VMEM((MOE_TILE, D_MODEL), BF16)])
    return pl.pallas_call(
        _moe_ffn_kernel, grid_spec=gs, out_shape=jax.ShapeDtypeStruct((n_rows, D_MODEL), F32),
        compiler_params=_cp("arbitrary", "arbitrary"), name="moe_ffn")(
            tile_expert, n_used, tile_rows, x_sorted, wg, wu, wd)


def _combine_ln_kernel(x_ref, r_ref, ya_ref, yb_ref, g_ref, b_ref, o_ref):
    r = r_ref[...]
    y = r[:, TOP_K:TOP_K + 1] * ya_ref[...] + r[:, TOP_K + 1:TOP_K + 2] * yb_ref[...]
    o_ref[...] = _layer_norm(ALPHA * x_ref[...] + y, g_ref[...], b_ref[...])


def combine_ln(x, r, ya, yb, g, b, tm, row_blk0, rows):
    row = pl.BlockSpec((tm, D_MODEL), lambda i: (row_blk0 + i, 0))
    vec = pl.BlockSpec((1, D_MODEL), lambda i: (0, 0))
    return pl.pallas_call(
        _combine_ln_kernel, grid=(rows // tm,),
        in_specs=[row, pl.BlockSpec((tm, LANES), lambda i: (row_blk0 + i, 0)), row, row, vec, vec],
        out_specs=pl.BlockSpec((tm, D_MODEL), lambda i: (i, 0)),
        out_shape=jax.ShapeDtypeStruct((rows, D_MODEL), F32),
        compiler_params=_cp("arbitrary"), name="combine_ln")(x, r, ya, yb, g, b)


def moe_ln(x, router_w, router_b, wg, wu, wd, g, b, tm, splits):
    m = x.shape[0]
    w_pad = jnp.zeros((D_MODEL, LANES), F32).at[:, :N_EXPERTS].set(router_w)
    b_pad = jnp.zeros((1, LANES), F32).at[0, :N_EXPERTS].set(router_b)
    r = router(x, w_pad, b_pad, tm)
    top_idx = r[:, :TOP_K].astype(jnp.int32)
    n_slot = m * TOP_K
    onehot = (top_idx[:, :, None] == jnp.arange(N_EXPERTS, dtype=jnp.int32)).astype(jnp.int32)
    per_tok = onehot[:, 0] + onehot[:, 1]
    before = jnp.cumsum(per_tok, axis=0) - per_tok
    counts = jnp.sum(per_tok, axis=0)
    tiles_per = (counts + MOE_TILE - 1) // MOE_TILE
    tile_end = jnp.cumsum(tiles_per)
    row0 = (tile_end - tiles_per) * MOE_TILE
    dest = jnp.sum(onehot * (before + row0)[:, None, :], axis=2)
    n_tiles = -(-n_slot // MOE_TILE) + N_EXPERTS
    n_rows = n_tiles * MOE_TILE
    row_tok = (jnp.arange(n_rows, dtype=jnp.int32) % m).at[dest.reshape(-1)].set(
        jnp.arange(n_slot, dtype=jnp.int32) // TOP_K, unique_indices=True, mode='promise_in_bounds')
    n_used = tile_end[-1:].astype(jnp.int32)
    tile_ids = jnp.minimum(jnp.arange(n_tiles, dtype=jnp.int32), n_used[0] - 1)
    tile_expert = jnp.minimum(jnp.sum((tile_end[None, :] <= tile_ids[:, None]).astype(jnp.int32), axis=1),
                              N_EXPERTS - 1)
    tile_rows = jnp.clip(counts[tile_expert] - (tile_ids - (tile_end - tiles_per)[tile_expert]) * MOE_TILE,
                         0, MOE_TILE).astype(jnp.int32)
    x_sorted = x.at[row_tok].get(mode='promise_in_bounds')
    y_sorted = moe_ffn(x_sorted, tile_expert, n_used, tile_rows, wg, wu, wd, tf=512)
    ya = y_sorted.at[dest[:, 0]].get(mode='promise_in_bounds')
    yb = y_sorted.at[dest[:, 1]].get(mode='promise_in_bounds')
    return [combine_ln(x, r, ya, yb, g, b, t, blk0, rows) for blk0, rows, t in splits]


def kernel(x_prompt, x_sample, cache_sb_k, cache_sb_v, cache_mem_k, cache_mem_v, state_ml_C, state_ml_n, state_ml_m, state_rt_S, state_s5_re, state_s5_im, page_table, mem_prompt, w_in, sb_bias, ml_b_i, ml_b_f, ml_norm_g, rt_norm_g, s5_A_re, s5_A_im, s5_log_dt, s5_B_re, s5_B_im, s5_C_re, s5_C_im, s5_D, s5_glu_w, s5_glu_b, w_out, ca_wq, ca_wk, ca_wv, ca_wo, ln_g, ln_b, ffn_w_gate, ffn_w_up, ffn_w_down, moe_router_w, moe_router_b, moe_w_gate, moe_w_up, moe_w_down):
    bp, tp, _ = x_prompt.shape
    bs, ts, _ = x_sample.shape
    n_p, n_s = bp * tp, bs * ts
    tm = 640
    assert (n_p + n_s) % (2 * tm) == 0 and n_p % n_s == 0 and tp % 512 == 0 and bp % 8 == 0 and bs % 8 == 0
    x = jnp.concatenate([x_prompt.reshape(n_p, D_MODEL), x_sample.reshape(n_s, D_MODEL)], axis=0)
    uu = _suffix_matrix()
    g_off = 7 * GROUP_W
    half = HEAD_DIM // 2
    freq = ROPE_BASE ** (-jnp.arange(half, dtype=F32) / half)

    def rope_tables(pos):
        ang = pos.astype(F32)[:, None] * freq[None, :]
        cos, sin = jnp.cos(ang), jnp.sin(ang)
        return (jnp.tile(jnp.concatenate([cos, cos], axis=1), (1, N_HEADS)),
                jnp.tile(jnp.concatenate([-sin, sin], axis=1), (1, N_HEADS)))

    cos_p, sin_p = rope_tables(jnp.arange(tp, dtype=jnp.int32))
    cos_s, sin_s = rope_tables(PAST_LEN + jnp.arange(ts, dtype=jnp.int32))
    cache_kt = cache_sb_k.transpose(0, 1, 3, 4, 2)
    cache_vt = cache_sb_v.transpose(0, 1, 3, 4, 2)

    p_st = [[] for _ in range(10)]
    s_st = [[] for _ in range(8)]
    for l in range(DEPTH):
        wl = w_in[l]
        w_cat = jnp.concatenate([wl[:, :g_off], wl[:, g_off + 2 * N_HEADS:], wl[:, g_off:g_off + 2 * N_HEADS],
                                 jnp.zeros((D_MODEL, PROJ_W - wl.shape[1]), F32)], axis=1).astype(BF16)
        proj_p, kt_p, vt_p = in_proj_prompt(x, w_cat, bp, tp, 512)
        proj_s = linear(x, w_cat, n_s, n_p // n_s, n_s)
        proj_p3 = proj_p.reshape(bp, tp, PROJ_W)
        proj_s3 = proj_s.reshape(bs, ts, PROJ_W)
        gate_bias = jnp.zeros((1, LANES), F32).at[0, :2 * N_HEADS].set(jnp.concatenate([ml_b_i[l], ml_b_f[l]]))
        ml_g = ml_norm_g[l][None, :]
        rt_g = rt_norm_g[l][None, :]
        wb, a1, a2, wc = _s5_weights(s5_A_re[l], s5_A_im[l], s5_log_dt[l], s5_B_re[l], s5_B_im[l],
                                     s5_C_re[l], s5_C_im[l])
        s5_d = s5_D[l][None, :]
        glu_w = s5_glu_w[l].astype(BF16)
        glu_b = s5_glu_b[l][None, :]

        o_sb_p = sb_attention_prompt(proj_p, sb_bias[l], uu, bp, tp, tq=256)
        o_ml_p, cn_p, m_p = mlstm_mixer(proj_p3, gate_bias, ml_g, jnp.zeros((bp, N_HEADS, HEAD_DIM, LANES), F32),
                                        jnp.zeros((bp, N_HEADS), F32), 8, MIX_DTYPE)
        o_rt_p, rs_p = retention_mixer(proj_p3, cos_p, sin_p, rt_g,
                                       jnp.zeros((bp, N_HEADS, HEAD_DIM, HEAD_DIM), F32), 8, MIX_DTYPE)
        o_ml_p = o_ml_p.reshape(n_p, GROUP_W)
        o_rt_p = o_rt_p.reshape(n_p, GROUP_W)
        o_s5_p, h5_p = s5_mixer(proj_p3, 64, wb, a1, a2, jnp.zeros((bp, 2 * S5_W), F32), wc, s5_d, glu_w, glu_b,
                                MIX_DTYPE)
        o_s5_p = o_s5_p.reshape(n_p, GROUP_W)

        o_sb_s = sb_attention_sample(proj_s, 0, cache_kt, cache_vt, page_table, l, sb_bias[l], uu, n_pp=16)
        cn0 = jnp.concatenate([state_ml_C[:, l], state_ml_n[:, l][..., None],
                               jnp.zeros((bs, N_HEADS, HEAD_DIM, LANES - HEAD_DIM - 1), F32)], axis=-1)
        o_ml_s, cn_s, m_s = mlstm_mixer(proj_s3, gate_bias, ml_g, cn0, state_ml_m[:, l], 8, F32)
        o_rt_s, rs_s = retention_mixer(proj_s3, cos_s, sin_s, rt_g, state_rt_S[:, l], 8, F32)
        o_ml_s = o_ml_s.reshape(n_s, GROUP_W)
        o_rt_s = o_rt_s.reshape(n_s, GROUP_W)
        h0_s = jnp.concatenate([state_s5_re[:, l].reshape(bs, S5_W), state_s5_im[:, l].reshape(bs, S5_W)], axis=1)
        o_s5_s, h5_s = s5_mixer(proj_s3, ts, wb, a1, a2, h0_s, wc, s5_d, glu_w, glu_b, F32)
        o_s5_s = o_s5_s.reshape(n_s, GROUP_W)

        wo_mix = w_out[l].astype(BF16)
        g0, b0 = ln_g[l, 0][None, :], ln_b[l, 0][None, :]
        x = mix_out_ln(x, (o_sb_p, o_ml_p, o_rt_p, o_s5_p), wo_mix, g0, b0, 512, 0)
        x = mix_out_ln(x, (o_sb_s, o_ml_s, o_rt_s, o_s5_s), wo_mix, g0, b0, n_s, n_p // n_s)

        mem_kv = linear(mem_prompt.reshape(bp * N_MEM, D_MODEL),
                        jnp.concatenate([ca_wk[l], ca_wv[l]], axis=1).astype(BF16), 512)
        mk_p = mem_kv[:, :GROUP_W].reshape(bp, N_MEM, GROUP_W)
        mv_p = mem_kv[:, GROUP_W:].reshape(bp, N_MEM, GROUP_W)
        wq = ca_wq[l].astype(BF16)
        wo = ca_wo[l].astype(BF16)
        g1, b1 = ln_g[l, 1][None, :], ln_b[l, 1][None, :]
        x = cross_attn_ln(x, 0, bp, tp, 512, mk_p, mv_p, wq, wo, g1, b1)
        x = cross_attn_ln(x, n_p // ts, bs, ts, ts, cache_mem_k[:, l].reshape(bs, N_MEM, GROUP_W),
                          cache_mem_v[:, l].reshape(bs, N_MEM, GROUP_W), wq, wo, g1, b1)

        g2, b2 = ln_g[l, 2][None, :], ln_b[l, 2][None, :]
        j = l // 2
        last = l == DEPTH - 1
        if l % 2 == 0:
            x = ffn_ln(x, ffn_w_gate[j], ffn_w_up[j], ffn_w_down[j], g2, b2, 2 * tm, tf=512)
            y_out = (x[:n_p], x[n_p:]) if last else None
        else:
            splits = [(0, n_p, 512), (n_p // n_s, n_s, n_s)] if last else [(0, n_p + n_s, tm)]
            y_out = moe_ln(x, moe_router_w[j], moe_router_b[j], moe_w_gate[j], moe_w_up[j], moe_w_down[j],
                           g2, b2, tm, splits)
            x = None if last else y_out[0]

        def heads(a, nb_, t_):
            return a.reshape(nb_, t_, N_HEADS, HEAD_DIM)

        p_st[0].append(kt_p)
        p_st[1].append(vt_p)
        p_st[2].append(heads(mk_p, bp, N_MEM))
        p_st[3].append(heads(mv_p, bp, N_MEM))
        s_st[0].append(heads(proj_s[:, C_SK * GROUP_W:(C_SK + 1) * GROUP_W], bs, ts))
        s_st[1].append(heads(proj_s[:, C_SV * GROUP_W:(C_SV + 1) * GROUP_W], bs, ts))
        for st, cn, mm, rs, h5, nb_ in ((p_st, cn_p, m_p, rs_p, h5_p, bp), (s_st, cn_s, m_s, rs_s, h5_s, bs)):
            off = 4 if st is p_st else 2
            st[off + 0].append(cn[..., :HEAD_DIM])
            st[off + 1].append(cn[..., HEAD_DIM])
            st[off + 2].append(mm[:, 0, :N_HEADS])
            st[off + 3].append(rs)
            st[off + 4].append(h5[:, :S5_W].reshape(nb_, S5_GROUPS, S5_STATE))
            st[off + 5].append(h5[:, S5_W:].reshape(nb_, S5_GROUPS, S5_STATE))

    y_prompt = y_out[0].reshape(bp, tp, D_MODEL)
    y_sample = y_out[1].reshape(bs, ts, D_MODEL)
    p_out = [jnp.stack(a, axis=1) for a in p_st]
    for i in range(2):
        p_out[i] = p_out[i].reshape(bp, DEPTH, N_HEADS, HEAD_DIM, tp).transpose(0, 1, 4, 2, 3)
    s_out = [jnp.stack(a, axis=1) for a in s_st]
    return (y_prompt, y_sample, *p_out, *s_out)
```

```python
import functools
import math

import numpy as np
import jax
import jax.numpy as jnp
from jax import lax
from jax.experimental import pallas as pl
from jax.experimental.pallas import tpu as pltpu

F32 = jnp.float32
BF16 = jnp.bfloat16

D_MODEL = 1024
DEPTH = 2
PAST_LEN = 8192
PAGE_SIZE = 128
HEAD_DIM = 64
N_HEADS = 4
GROUP_W = N_HEADS * HEAD_DIM
S5_GROUPS = 16
S5_GROUP = 16
S5_STATE = 64
S5_W = S5_GROUPS * S5_STATE
N_MEM = 256
D_FF = 3584
N_EXPERTS = 8
TOP_K = 2
CHUNK = 64
ROPE_BASE = 10000.0
LN_EPS = 1e-5
GN_EPS = 1e-6
ALPHA = (2 * DEPTH) ** 0.25
QK_SCALE = HEAD_DIM ** -0.5
LOG2E = math.log2(math.e)

LANES = 128
PROJ_W = 25 * LANES
C_SQ, C_SK, C_SV, C_MQ, C_MK, C_MV, C_MO, C_RQ, C_RK, C_RV, C_RG, C_SU = range(12)
C_GATES = 12 * GROUP_W // LANES
VMEM_LIMIT = 48 * 1024 * 1024
MOE_TILE = 1024
MIX_DTYPE = BF16


def _cp(*sem):
    return pltpu.CompilerParams(dimension_semantics=sem, vmem_limit_bytes=VMEM_LIMIT)


def _dot(a, b):
    return jnp.dot(a, b, preferred_element_type=F32)


def _dot_nt(a, b):
    return lax.dot_general(a, b, (((1,), (1,)), ((), ())), preferred_element_type=F32)


def _dot_tn(a, b):
    return lax.dot_general(a, b, (((0,), (0,)), ((), ())), preferred_element_type=F32)


def _layer_norm(y, g, b):
    mu = jnp.mean(y, axis=-1, keepdims=True)
    yc = y - mu
    var = jnp.mean(yc * yc, axis=-1, keepdims=True)
    return yc * lax.rsqrt(var + LN_EPS) * g + b


def _row_sum(x, scale=1.0):
    ones = jnp.full((x.shape[1], LANES), scale, BF16)
    hi = x.astype(BF16)
    lo = (x - hi.astype(F32)).astype(BF16)
    return (_dot(hi, ones) + _dot(lo, ones))[:, :x.shape[1]]


def _cumsum_rows(tril, x):
    hi = x.astype(BF16)
    lo = (x - hi.astype(F32)).astype(BF16)
    return _dot(tril, hi) + _dot(tril, lo)


def _head_norm_all(hd):
    inv = 1.0 / HEAD_DIM
    mu = {p: _row_sum(x, inv) for p, x in hd.items()}
    hc = {p: hd[p] - mu[p] for p in hd}
    var = {p: _row_sum(hc[p] * hc[p], inv) for p in hd}
    return {p: hc[p] * lax.rsqrt(var[p] + GN_EPS) for p in hd}


def _neg_softplus(z):
    return -(jnp.maximum(z, 0.0) + jnp.log1p(jnp.exp(-jnp.abs(z))))


def _log_sigmoid(z):
    return _neg_softplus(-z)


def _linear_kernel(x_ref, w_ref, o_ref):
    o_ref[...] = _dot(x_ref[...].astype(BF16), w_ref[...]).astype(o_ref.dtype)


def linear(x, w, tm, row_blk0=0, n_rows=None, out_dtype=F32):
    m, k = x.shape
    m = m if n_rows is None else n_rows
    n = w.shape[1]
    return pl.pallas_call(
        _linear_kernel, grid=(m // tm,),
        in_specs=[pl.BlockSpec((tm, k), lambda i: (row_blk0 + i, 0)), pl.BlockSpec((k, n), lambda i: (0, 0))],
        out_specs=pl.BlockSpec((tm, n), lambda i: (i, 0)),
        out_shape=jax.ShapeDtypeStruct((m, n), out_dtype),
        compiler_params=_cp("arbitrary"), name="linear")(x, w)


def _in_proj_kernel(x_ref, w_ref, o_ref, kt_ref, vt_ref):
    o = _dot(x_ref[...].astype(BF16), w_ref[...])
    o_ref[...] = o
    kt_ref[0] = o[:, C_SK * GROUP_W:(C_SK + 1) * GROUP_W].T
    vt_ref[0] = o[:, C_SV * GROUP_W:(C_SV + 1) * GROUP_W].T


def in_proj_prompt(x, w, n_batch, seq, tm):
    k = x.shape[1]
    n = w.shape[1]
    nt = seq // tm
    t_spec = pl.BlockSpec((1, GROUP_W, tm), lambda i: (i // nt, 0, i % nt))
    t_shape = jax.ShapeDtypeStruct((n_batch, GROUP_W, seq), F32)
    return pl.pallas_call(
        _in_proj_kernel, grid=(n_batch * nt,),
        in_specs=[pl.BlockSpec((tm, k), lambda i: (i, 0)), pl.BlockSpec((k, n), lambda i: (0, 0))],
        out_specs=[pl.BlockSpec((tm, n), lambda i: (i, 0)), t_spec, t_spec],
        out_shape=[jax.ShapeDtypeStruct((n_batch * seq, n), F32), t_shape, t_shape],
        compiler_params=_cp("arbitrary"), name="in_proj")(x, w)


def _mix_out_ln_kernel(x_ref, a_ref, b_ref, c_ref, d_ref, w_ref, g_ref, bias_ref, o_ref):
    h = sum(_dot(p[...].astype(BF16), w_ref[pl.ds(n * GROUP_W, GROUP_W), :])
            for n, p in enumerate((a_ref, b_ref, c_ref, d_ref)))
    o_ref[...] = _layer_norm(ALPHA * x_ref[...] + h, g_ref[...], bias_ref[...])


def mix_out_ln(x, parts, w, g, b, tm, row_blk0):
    rows = parts[0].shape[0]
    row_spec = pl.BlockSpec((tm, D_MODEL), lambda i: (row_blk0 + i, 0))
    part_spec = pl.BlockSpec((tm, GROUP_W), lambda i: (i, 0))
    vec = pl.BlockSpec((1, D_MODEL), lambda i: (0, 0))
    return pl.pallas_call(
        _mix_out_ln_kernel, grid=(rows // tm,),
        in_specs=[row_spec, part_spec, part_spec, part_spec, part_spec,
                  pl.BlockSpec((D_MODEL, D_MODEL), lambda i: (0, 0)), vec, vec],
        out_specs=row_spec, out_shape=jax.ShapeDtypeStruct(x.shape, F32), input_output_aliases={0: 0},
        compiler_params=_cp("arbitrary"), name="mix_out_ln")(x, *parts, w, g, b)


def _suffix_matrix():
    j = np.arange(LANES)
    u = (j[:, None] >= j[None, :]).astype(np.float32)
    uu = np.concatenate([u, np.ones((LANES, LANES), np.float32)], axis=1)
    return jnp.asarray(np.concatenate([uu, uu], axis=0), dtype=BF16)


def _suffix_sums(lr, uu):
    hi = lr.astype(BF16)
    lo = (lr - hi.astype(F32)).astype(BF16)
    r = _dot(jnp.concatenate([hi, lo], axis=1), uu)
    return r[:, :LANES], r[:, LANES:]


def _log2_rem(z2):
    return jnp.minimum(-z2, 0.0) - jnp.log2(1.0 + jnp.exp2(-jnp.abs(z2)))


def _sb_prompt_kernel(bias_ref, q_ref, k_ref, v_ref, uu_ref, o_ref, acc_ref, car_ref, kb_ref, vb_ref, *, tq):
    i = pl.program_id(1)
    tk = LANES
    nsub = tq // tk
    acc_ref[...] = jnp.zeros_like(acc_ref)
    car_ref[...] = jnp.zeros_like(car_ref)

    @pl.when(i == 0)
    def _():
        for h in range(N_HEADS):
            kb_ref[h] = k_ref[:, h * HEAD_DIM:(h + 1) * HEAD_DIM].astype(BF16)
            vb_ref[h] = v_ref[:, h * HEAD_DIM:(h + 1) * HEAD_DIM].astype(BF16)

    q = (q_ref[...] * (QK_SCALE * LOG2E)).astype(BF16)
    qh = [q[:, h * HEAD_DIM:(h + 1) * HEAD_DIM] for h in range(N_HEADS)]
    b2 = [bias_ref[h] * LOG2E for h in range(N_HEADS)]
    uu = uu_ref[...]
    row = lax.broadcasted_iota(jnp.int32, (tq, tk), 0)
    col = lax.broadcasted_iota(jnp.int32, (tq, tk), 1)
    heads = range(N_HEADS)

    def block_pair(j_hi, causal_hi, causal_lo):
        r0 = [pl.multiple_of((j_hi - d) * tk, tk) for d in range(2)]
        z2 = [[_dot_nt(qh[h], kb_ref[h, pl.ds(r0[d], tk), :]) + b2[h] for h in heads] for d in range(2)]
        cs, tot = [], []
        for d, causal in enumerate((causal_hi, causal_lo)):
            lr = [_log2_rem(z) for z in z2[d]]
            if causal is not None:
                lr = [jnp.where(causal, a, 0.0) for a in lr]
            c, t = _suffix_sums(jnp.concatenate(lr, axis=0), uu)
            cs.append(c)
            tot.append(t)
        car = [car_ref[h] for h in heads]
        pv = []
        for d, causal in enumerate((causal_hi, causal_lo)):
            w = [jnp.exp2(z2[d][h] + cs[d][h * tq:(h + 1) * tq] + car[h]) for h in heads]
            if causal is not None:
                w = [jnp.where(causal, a, 0.0) for a in w]
            pv.append([_dot(w[h].astype(BF16), vb_ref[h, pl.ds(r0[d], tk), :]) for h in heads])
            car = [car[h] + tot[d][h * tq:(h + 1) * tq] for h in heads]
        for h in heads:
            acc_ref[h] += pv[0][h] + pv[1][h]
            car_ref[h] = car[h]

    assert nsub == 2
    block_pair(i * nsub + 1, (col + tk) < row, col < row)

    def body(jj, carry):
        block_pair(i * nsub - 1 - 2 * jj, None, None)
        return carry

    lax.fori_loop(0, i, body, 0)
    o_ref[...] = jnp.concatenate([acc_ref[h] for h in range(N_HEADS)], axis=1).astype(o_ref.dtype)


def sb_attention_prompt(proj, sb_bias, uu, n_batch, seq, tq):
    nq = seq // tq
    kern = functools.partial(_sb_prompt_kernel, tq=tq)
    return pl.pallas_call(
        kern, grid=(n_batch, nq),
        in_specs=[pl.BlockSpec(memory_space=pltpu.SMEM),
                  pl.BlockSpec((tq, GROUP_W), lambda b, i: (b * nq + i, C_SQ)),
                  pl.BlockSpec((seq, GROUP_W), lambda b, i: (b, C_SK)),
                  pl.BlockSpec((seq, GROUP_W), lambda b, i: (b, C_SV)),
                  pl.BlockSpec((2 * LANES, 2 * LANES), lambda b, i: (0, 0))],
        out_specs=pl.BlockSpec((tq, GROUP_W), lambda b, i: (b * nq + i, 0)),
        out_shape=jax.ShapeDtypeStruct((n_batch * seq, GROUP_W), MIX_DTYPE),
        scratch_shapes=[pltpu.VMEM((N_HEADS, tq, HEAD_DIM), F32), pltpu.VMEM((N_HEADS, tq, LANES), F32),
                        pltpu.VMEM((N_HEADS, seq, HEAD_DIM), BF16), pltpu.VMEM((N_HEADS, seq, HEAD_DIM), BF16)],
        compiler_params=_cp("arbitrary", "arbitrary"), name="sb_prompt")(sb_bias, proj, proj, proj, uu)


def _sb_sample_kernel(pt_ref, bias_ref, q_ref, kn_ref, vn_ref, u8_ref, uu_ref, *rest, n_pp, n_steps):
    k_refs = rest[:n_pp]
    v_refs = rest[n_pp:2 * n_pp]
    o_ref = rest[2 * n_pp]
    acc_ref, car_ref = rest[2 * n_pp + 1:]
    s = pl.program_id(1)
    nq = q_ref.shape[0]
    rows = N_HEADS * nq
    q = (q_ref[...] * (QK_SCALE * LOG2E)).astype(BF16)
    row_head = lax.broadcasted_iota(jnp.int32, (rows, GROUP_W), 0) // nq
    own = row_head == lax.broadcasted_iota(jnp.int32, (rows, GROUP_W), 1) // HEAD_DIM
    q_bd = jnp.where(own, jnp.concatenate([q] * N_HEADS, axis=0), jnp.zeros((), BF16))
    row_head_l = lax.broadcasted_iota(jnp.int32, (rows, LANES), 0) // nq
    b2 = jnp.zeros((rows, LANES), F32)
    for h in range(N_HEADS):
        b2 = jnp.where(row_head_l == h, bias_ref[h] * LOG2E, b2)

    @pl.when(s == 0)
    def _():
        kn = kn_ref[...].astype(BF16)
        vn = vn_ref[...].astype(BF16)
        t = lax.broadcasted_iota(jnp.int32, (rows, nq), 0) % nq
        causal = lax.broadcasted_iota(jnp.int32, (rows, nq), 1) < t
        z2 = _dot_nt(q_bd, kn) + b2[:, :nq]
        lr = jnp.where(causal, _log2_rem(z2), 0.0)
        cs = jnp.dot(lr, u8_ref[...], preferred_element_type=F32, precision=lax.Precision.HIGHEST)
        w = jnp.where(causal, jnp.exp2(z2 + cs), 0.0)
        acc_ref[...] = _dot(w.astype(BF16), vn)
        car_ref[...] = jnp.broadcast_to(jnp.sum(lr, axis=1, keepdims=True), (rows, LANES))

    z2s = [_dot(q_bd, k_refs[p][0, 0].reshape(GROUP_W, PAGE_SIZE).astype(BF16)) + b2 for p in range(n_pp)]
    cs_all, tot_all = _suffix_sums(_log2_rem(jnp.concatenate(z2s, axis=0)), uu_ref[...])
    car = car_ref[...]
    acc = acc_ref[...]
    for p in range(n_pp):
        w = jnp.exp2(z2s[p] + cs_all[p * rows:(p + 1) * rows] + car)
        acc = acc + _dot_nt(w.astype(BF16), v_refs[p][0, 0].reshape(GROUP_W, PAGE_SIZE).astype(BF16))
        car = car + tot_all[p * rows:(p + 1) * rows]
    car_ref[...] = car
    acc_ref[...] = acc

    @pl.when(s == n_steps - 1)
    def _():
        kept = jnp.where(own, acc, 0.0)
        o_ref[...] = sum(kept[h * nq:(h + 1) * nq] for h in range(N_HEADS))


def sb_attention_sample(proj, row_blk0, cache_kt, cache_vt, page_table, layer, sb_bias, uu, n_pp):
    n_batch, n_pages = page_table.shape
    nq = 8
    n_steps = n_pages // n_pp
    u8 = jnp.asarray((np.arange(nq)[:, None] >= np.arange(nq)[None, :]).astype(np.float32))

    def page_spec(p):
        return pl.BlockSpec((1, 1, N_HEADS, HEAD_DIM, PAGE_SIZE),
                            lambda b, s, pt: (pt[b, n_pages - 1 - (s * n_pp + p)], layer, 0, 0, 0))

    def row_spec(cblk):
        return pl.BlockSpec((nq, GROUP_W), lambda b, s, pt: (row_blk0 + b, cblk))

    kern = functools.partial(_sb_sample_kernel, n_pp=n_pp, n_steps=n_steps)
    gs = pltpu.PrefetchScalarGridSpec(
        num_scalar_prefetch=1, grid=(n_batch, n_steps),
        in_specs=[pl.BlockSpec(memory_space=pltpu.SMEM), row_spec(C_SQ), row_spec(C_SK), row_spec(C_SV),
                  pl.BlockSpec((nq, nq), lambda b, s, pt: (0, 0)),
                  pl.BlockSpec((2 * LANES, 2 * LANES), lambda b, s, pt: (0, 0))]
                 + [page_spec(p) for p in range(n_pp)] * 2,
        out_specs=pl.BlockSpec((nq, GROUP_W), lambda b, s, pt: (b, 0)),
        scratch_shapes=[pltpu.VMEM((N_HEADS * nq, GROUP_W), F32), pltpu.VMEM((N_HEADS * nq, LANES), F32)])
    return pl.pallas_call(
        kern, grid_spec=gs, out_shape=jax.ShapeDtypeStruct((n_batch * nq, GROUP_W), F32),
        compiler_params=_cp("arbitrary", "arbitrary"), name="sb_sample")(
            page_table, sb_bias, proj, proj, proj, u8, uu, *([cache_kt] * n_pp), *([cache_vt] * n_pp))


def _mlstm_kernel(m0_ref, q_ref, k_ref, v_ref, og_ref, gt_ref, gb_ref, ng_ref, tril_ref, cn0_ref,
                  o_ref, cn_ref, m_ref, ms_ref, *, bb, chunk, n_t):
    bi = pl.program_id(0)
    t = pl.program_id(1)
    seqs = range(bb)
    pairs = [(b, h) for b in seqs for h in range(N_HEADS)]

    @pl.when(t == 0)
    def _():
        cn_ref[...] = cn0_ref[...]
        for b, h in pairs:
            ms_ref[b * N_HEADS + h] = jnp.full((1, LANES), m0_ref[bi * bb + b, h], F32)

    tril = tril_ref[...]
    tri_mask = lax.broadcasted_iota(jnp.int32, (chunk, chunk), 1) <= lax.broadcasted_iota(jnp.int32, (chunk, chunk), 0)
    lane = lax.broadcasted_iota(jnp.int32, (chunk, HEAD_DIM), 1)
    ones_col = jnp.where(lane == 0, 1.0, 0.0).astype(F32)
    hs = lambda h: slice(h * HEAD_DIM, (h + 1) * HEAD_DIM)

    gt = [gt_ref[b] + gb_ref[...] for b in seqs]
    bc = [_cumsum_rows(tril, _log_sigmoid(g)) for g in gt]
    gt_t = [g.T for g in gt]
    bc_t = [x.T for x in bc]
    q = [q_ref[b].astype(BF16) for b in seqs]
    k = [(k_ref[b] * QK_SCALE).astype(BF16) for b in seqs]
    v = [v_ref[b] for b in seqs]
    m_prev = {p: ms_ref[p[0] * N_HEADS + p[1]][:, :1] for p in pairs}
    ig_col = {(b, h): gt[b][:, h:h + 1] for b, h in pairs}
    bc_col = {(b, h): bc[b][:, N_HEADS + h:N_HEADS + h + 1] for b, h in pairs}
    dm = {(b, h): jnp.where(tri_mask, bc_col[b, h] - (bc_t[b][N_HEADS + h:N_HEADS + h + 1, :] - gt_t[b][h:h + 1, :]),
                            -jnp.inf) for b, h in pairs}
    a = {p: bc_col[p] + m_prev[p] for p in pairs}
    m_new = {p: jnp.maximum(a[p], jnp.max(dm[p], axis=1, keepdims=True)) for p in pairs}
    inter = {p: jnp.exp(a[p] - m_new[p]) for p in pairs}
    s = {(b, h): _dot_nt(q[b][:, hs(h)], k[b][:, hs(h)]) * jnp.exp(dm[b, h] - m_new[b, h]) for b, h in pairs}
    v_ext = {(b, h): jnp.concatenate([v[b][:, hs(h)], ones_col], axis=1) for b, h in pairs}
    qc = {(b, h): _dot(q[b][:, hs(h)], cn_ref[b, h].astype(BF16)) for b, h in pairs}
    sv = {p: _dot(s[p].astype(BF16), v_ext[p].astype(BF16)) for p in pairs}
    m_last = {p: m_new[p][chunk - 1:chunk, :] for p in pairs}
    wl = {p: jnp.exp(bc_col[p][chunk - 1:chunk, :] - bc_col[p] + ig_col[p] - m_last[p]) for p in pairs}
    dl = {p: jnp.exp(a[p][chunk - 1:chunk, :] - m_last[p]) for p in pairs}
    upd = {(b, h): _dot_tn(k[b][:, hs(h)], (wl[b, h] * v_ext[b, h]).astype(BF16)) for b, h in pairs}
    rs = {p: _row_sum(s[p])[:, :1] for p in pairs}
    num = {p: inter[p] * qc[p][:, :HEAD_DIM] + sv[p][:, :HEAD_DIM] for p in pairs}
    den = {p: inter[p] * qc[p][:, HEAD_DIM:HEAD_DIM + 1] + rs[p] for p in pairs}
    hh = _head_norm_all({p: num[p] / jnp.maximum(jnp.abs(den[p]), jnp.exp(-m_new[p])) for p in pairs})
    for b, h in pairs:
        cn_ref[b, h] = dl[b, h] * cn_ref[b, h] + upd[b, h]
        ms_ref[b * N_HEADS + h] = jnp.broadcast_to(m_last[b, h], (1, LANES))
    for b in seqs:
        y = jnp.concatenate([hh[b, h] for h in range(N_HEADS)], axis=1)
        o_ref[b] = (y * ng_ref[...] * jax.nn.sigmoid(og_ref[b])).astype(o_ref.dtype)

    @pl.when(t == n_t - 1)
    def _():
        lane_m = lax.broadcasted_iota(jnp.int32, (1, LANES), 1)
        for b in seqs:
            m_out = jnp.zeros((1, LANES), F32)
            for h in range(N_HEADS):
                m_out = jnp.where(lane_m == h, ms_ref[b * N_HEADS + h], m_out)
            m_ref[b] = m_out


def mlstm_mixer(proj3, gate_bias, norm_g, cn0, m0, bb, out_dtype):
    n_batch, seq, _ = proj3.shape
    chunk = math.gcd(seq, CHUNK)
    n_t = seq // chunk
    tril = jnp.asarray(np.tril(np.ones((chunk, chunk), np.float32)), dtype=BF16)

    def row_spec(cblk, w=GROUP_W):
        return pl.BlockSpec((bb, chunk, w), lambda bi, t: (bi, t, cblk))

    def const_spec(shape):
        return pl.BlockSpec(shape, lambda bi, t: (0,) * len(shape))

    state_spec = pl.BlockSpec((bb, N_HEADS, HEAD_DIM, LANES), lambda bi, t: (bi, 0, 0, 0))
    kern = functools.partial(_mlstm_kernel, bb=bb, chunk=chunk, n_t=n_t)
    return pl.pallas_call(
        kern, grid=(n_batch // bb, n_t),
        in_specs=[pl.BlockSpec(memory_space=pltpu.SMEM),
                  row_spec(C_MQ), row_spec(C_MK), row_spec(C_MV), row_spec(C_MO), row_spec(C_GATES, LANES),
                  const_spec((1, LANES)), const_spec((1, GROUP_W)), const_spec((chunk, chunk)), state_spec],
        out_specs=[pl.BlockSpec((bb, chunk, GROUP_W), lambda bi, t: (bi, t, 0)), state_spec,
                   pl.BlockSpec((bb, 1, LANES), lambda bi, t: (bi, 0, 0))],
        out_shape=[jax.ShapeDtypeStruct((n_batch, seq, GROUP_W), out_dtype),
                   jax.ShapeDtypeStruct((n_batch, N_HEADS, HEAD_DIM, LANES), F32),
                   jax.ShapeDtypeStruct((n_batch, 1, LANES), F32)],
        scratch_shapes=[pltpu.VMEM((bb * N_HEADS, 1, LANES), F32)],
        compiler_params=_cp("arbitrary", "arbitrary"), name="mlstm")(
            m0, proj3, proj3, proj3, proj3, proj3, gate_bias, norm_g, tril, cn0)


def _split_dot(x, w):
    hi = x.astype(BF16)
    lo = (x - hi.astype(F32)).astype(BF16)
    return _dot(hi, w) + _dot(lo, w)


def _mlstm_pair_kernel(q_ref, k_ref, v_ref, og_ref, gt_ref, gb_ref, ng_ref, tril_ref, sel_ref, blk_ref,
                       o_ref, cn_ref, st_ref, *, bb, chunk):
    t_id = pl.program_id(1)

    @pl.when(t_id == 0)
    def _():
        cn_ref[...] = jnp.zeros_like(cn_ref)
        st_ref[...] = jnp.zeros_like(st_ref)

    seqs = range(bb)
    n_pair = N_HEADS // 2
    chains = [(b, p) for b in seqs for p in range(n_pair)]
    lane = lax.broadcasted_iota(jnp.int32, (chunk, LANES), 1)
    row = lax.broadcasted_iota(jnp.int32, (chunk, LANES), 0)
    low = lane < HEAD_DIM
    causal = (lane % HEAD_DIM) <= row
    eye2 = (lane % HEAD_DIM) == row
    blk = blk_ref[...]
    same = (lax.broadcasted_iota(jnp.int32, (LANES, LANES), 0) // HEAD_DIM
            == lax.broadcasted_iota(jnp.int32, (LANES, LANES), 1) // HEAD_DIM)
    ones_ll = jnp.ones((chunk, chunk), BF16)
    tril = tril_ref[...]
    pl_ = lambda p: slice(p * LANES, (p + 1) * LANES)
    zero_b = jnp.zeros((), BF16)

    gt = [gt_ref[b] + gb_ref[...] for b in seqs]
    bc = [_cumsum_rows(tril, _log_sigmoid(g)) for g in gt]
    def stacked(fn, xs):
        keys = list(xs)
        y = fn(jnp.concatenate([xs[c] for c in keys], axis=0))
        return {c: y[i * chunk:(i + 1) * chunk] for i, c in enumerate(keys)}

    by_blk = lambda x: _split_dot(x, blk)
    ig_all = stacked(lambda x: _split_dot(x, sel_ref[0]), dict(enumerate(gt)))
    bc_all = stacked(lambda x: _split_dot(x, sel_ref[1]), dict(enumerate(bc)))
    igc = {(b, p): ig_all[b][:, pl_(p)] for b, p in chains}
    bcc = {(b, p): bc_all[b][:, pl_(p)] for b, p in chains}
    gc = {c: igc[c] - bcc[c] for c in chains}
    grow = {c: _split_dot_left(ones_ll, jnp.where(eye2, gc[c], 0.0)) for c in chains}
    gmax = {c: jnp.max(gc[c], axis=0, keepdims=True) for c in chains}
    m_prev = {(b, p): st_ref[b, p, 1:2, :] for b, p in chains}
    a = {c: bcc[c] + m_prev[c] for c in chains}
    m_stab = {c: jnp.maximum(a[c], bcc[c] + gmax[c]) for c in chains}
    inter = {c: jnp.exp(a[c] - m_stab[c]) for c in chains}
    dw = {c: jnp.exp(jnp.where(causal, bcc[c] + grow[c], -jnp.inf) - m_stab[c]) for c in chains}
    q2 = {(b, p): q_ref[b, :, pl_(p)].astype(BF16) for b, p in chains}
    k2f = {(b, p): k_ref[b, :, pl_(p)] * QK_SCALE for b, p in chains}
    k2 = {c: k2f[c].astype(BF16) for c in chains}
    v2 = {(b, p): v_ref[b, :, pl_(p)] for b, p in chains}
    kbd = {c: jnp.concatenate([jnp.where(low, k2[c], zero_b), jnp.where(low, zero_b, k2[c])], axis=0) for c in chains}
    v2b = {c: v2[c].astype(BF16) for c in chains}
    vbd = {c: jnp.concatenate([jnp.where(low, v2b[c], zero_b), jnp.where(low, zero_b, v2b[c])], axis=0)
           for c in chains}
    s = {c: _dot_nt(q2[c], kbd[c]) * dw[c] for c in chains}
    rs = stacked(by_blk, s)
    sv = {c: _dot(s[c].astype(BF16), vbd[c]) for c in chains}
    qc = {(b, p): _dot(q2[b, p], cn_ref[b, p].astype(BF16)) for b, p in chains}
    n_row = {(b, p): st_ref[b, p, 0:1, :] for b, p in chains}
    qn = stacked(by_blk, {c: q2[c].astype(F32) * n_row[c] for c in chains})
    hh = {}
    for c in chains:
        num = inter[c] * qc[c] + sv[c]
        den = inter[c] * qn[c] + rs[c]
        hh[c] = num / jnp.maximum(jnp.abs(den), jnp.exp(-m_stab[c]))
    inv = 1.0 / HEAD_DIM
    mu = stacked(by_blk, hh)
    hc = {c: hh[c] - mu[c] * inv for c in chains}
    var = stacked(by_blk, {c: hc[c] * hc[c] for c in chains})
    m_last = {c: m_stab[c][chunk - 1:chunk, :] for c in chains}
    wl = {c: jnp.exp(bcc[c][chunk - 1:chunk, :] - bcc[c] + igc[c] - m_last[c]) for c in chains}
    dl = {c: jnp.exp(a[c][chunk - 1:chunk, :] - m_last[c]) for c in chains}
    upd = {c: _dot_tn(k2[c], (wl[c] * v2[c]).astype(BF16)) for c in chains}
    for b, p in chains:
        c = (b, p)
        y = hc[c] * lax.rsqrt(var[c] * inv + GN_EPS)
        o_ref[b, :, pl_(p)] = (y * ng_ref[:, pl_(p)] * jax.nn.sigmoid(og_ref[b, :, pl_(p)])).astype(o_ref.dtype)
        cn_ref[b, p] = dl[c] * cn_ref[b, p] + jnp.where(same, upd[c], 0.0)
        st_ref[b, p, 0:1, :] = dl[c] * n_row[c] + jnp.sum(wl[c] * k2f[c], axis=0, keepdims=True)
        st_ref[b, p, 1:2, :] = m_last[c]


def _split_dot_left(w, x):
    hi = x.astype(BF16)
    lo = (x - hi.astype(F32)).astype(BF16)
    return _dot(w, hi) + _dot(w, lo)


def mlstm_mixer_prompt(proj3, gate_bias, norm_g, bb, out_dtype):
    n_batch, seq, _ = proj3.shape
    chunk = CHUNK
    n_pair = N_HEADS // 2
    tril = jnp.asarray(np.tril(np.ones((chunk, chunk), np.float32)), dtype=BF16)
    sel = np.zeros((2, LANES, GROUP_W), np.float32)
    for h in range(N_HEADS):
        sel[0, h, h * HEAD_DIM:(h + 1) * HEAD_DIM] = 1.0
        sel[1, N_HEADS + h, h * HEAD_DIM:(h + 1) * HEAD_DIM] = 1.0
    idx = np.arange(LANES) // HEAD_DIM
    blk = (idx[:, None] == idx[None, :]).astype(np.float32)

    def row_spec(cblk, w=GROUP_W):
        return pl.BlockSpec((bb, chunk, w), lambda bi, t: (bi, t, cblk))

    def const_spec(shape):
        return pl.BlockSpec(shape, lambda bi, t: (0,) * len(shape))

    cn_spec = pl.BlockSpec((bb, n_pair, LANES, LANES), lambda bi, t: (bi, 0, 0, 0))
    st_spec = pl.BlockSpec((bb, n_pair, 8, LANES), lambda bi, t: (bi, 0, 0, 0))
    kern = functools.partial(_mlstm_pair_kernel, bb=bb, chunk=chunk)
    o, cn, st = pl.pallas_call(
        kern, grid=(n_batch // bb, seq // chunk),
        in_specs=[row_spec(C_MQ), row_spec(C_MK), row_spec(C_MV), row_spec(C_MO), row_spec(C_GATES, LANES),
                  const_spec((1, LANES)), const_spec((1, GROUP_W)), const_spec((chunk, chunk)),
                  const_spec(sel.shape), const_spec(blk.shape)],
        out_specs=[pl.BlockSpec((bb, chunk, GROUP_W), lambda bi, t: (bi, t, 0)), cn_spec, st_spec],
        out_shape=[jax.ShapeDtypeStruct((n_batch, seq, GROUP_W), out_dtype),
                   jax.ShapeDtypeStruct((n_batch, n_pair, LANES, LANES), F32),
                   jax.ShapeDtypeStruct((n_batch, n_pair, 8, LANES), F32)],
        compiler_params=_cp("arbitrary", "arbitrary"), name="mlstm_prompt")(
            proj3, proj3, proj3, proj3, proj3, gate_bias, norm_g, tril,
            jnp.asarray(sel, dtype=BF16), jnp.asarray(blk, dtype=BF16))
    c5 = cn.reshape(n_batch, n_pair, 2, HEAD_DIM, 2, HEAD_DIM)
    c_state = jnp.stack([c5[:, :, j, :, j, :] for j in range(2)], axis=2).reshape(n_batch, N_HEADS, HEAD_DIM, HEAD_DIM)
    n_state = st[:, :, 0, :].reshape(n_batch, N_HEADS, HEAD_DIM)
    m_state = st[:, :, 1, :].reshape(n_batch, N_HEADS, HEAD_DIM)[:, :, 0]
    return o, c_state, n_state, m_state


def _rope(x, cos, sin_signed):
    lane = lax.broadcasted_iota(jnp.int32, x.shape, 1)
    half = HEAD_DIM // 2
    swapped = jnp.where((lane % HEAD_DIM) < half, pltpu.roll(x, x.shape[1] - half, 1), pltpu.roll(x, half, 1))
    return x * cos + swapped * sin_signed


def _retention_kernel(q_ref, k_ref, v_ref, gg_ref, cos_ref, sin_ref, ng_ref, dec_ref, int_ref, wl_ref, dl_ref,
                      s0_ref, o_ref, s_ref, *, bb):
    t = pl.program_id(1)
    seqs = range(bb)
    pairs = [(b, h) for b in seqs for h in range(N_HEADS)]
    hs = lambda h: slice(h * HEAD_DIM, (h + 1) * HEAD_DIM)

    @pl.when(t == 0)
    def _():
        s_ref[...] = s0_ref[...]

    cos = cos_ref[...]
    sin = sin_ref[...]
    q = [_rope(q_ref[b], cos, sin).astype(BF16) for b in seqs]
    k = [(_rope(k_ref[b], cos, sin) * QK_SCALE).astype(BF16) for b in seqs]
    v = [v_ref[b] for b in seqs]
    s = {(b, h): _dot_nt(q[b][:, hs(h)], k[b][:, hs(h)]) * dec_ref[h] for b, h in pairs}
    qs = {(b, h): _dot(q[b][:, hs(h)], s_ref[b, h].astype(BF16)) for b, h in pairs}
    sv = {(b, h): _dot(s[b, h].astype(BF16), v[b][:, hs(h)].astype(BF16)) for b, h in pairs}
    upd = {(b, h): _dot_tn(k[b][:, hs(h)], (wl_ref[h] * v[b][:, hs(h)]).astype(BF16)) for b, h in pairs}
    o = _head_norm_all({(b, h): int_ref[h] * qs[b, h] + sv[b, h] for b, h in pairs})
    for b, h in pairs:
        s_ref[b, h] = dl_ref[h] * s_ref[b, h] + upd[b, h]
    for b in seqs:
        gg = gg_ref[b]
        y = jnp.concatenate([o[b, h] for h in range(N_HEADS)], axis=1) * ng_ref[...] * (gg * jax.nn.sigmoid(gg))
        o_ref[b] = y.astype(o_ref.dtype)


def _retention_consts(chunk):
    log_g = np.log(1.0 - np.exp2(-5.0 - np.arange(N_HEADS, dtype=np.float64)))
    tau = np.arange(chunk, dtype=np.float64)
    rel = tau[:, None] - tau[None, :]
    decay = np.where(rel >= 0, np.exp(log_g[:, None, None] * np.maximum(rel, 0.0)), 0.0)
    inter = np.exp(log_g[:, None] * (tau + 1.0))[..., None]
    wl = np.exp(log_g[:, None] * (chunk - 1.0 - tau))[..., None]
    dl = np.exp(log_g * chunk)[:, None, None]
    return tuple(jnp.asarray(a, F32) for a in (decay, inter, wl, dl))


def retention_mixer(proj3, cos, sin_signed, norm_g, s0, bb, out_dtype):
    n_batch, seq, _ = proj3.shape
    chunk = math.gcd(seq, CHUNK)
    dec, inter, wl, dl = _retention_consts(chunk)

    def row_spec(cblk):
        return pl.BlockSpec((bb, chunk, GROUP_W), lambda bi, t: (bi, t, cblk))

    def const_spec(shape):
        return pl.BlockSpec(shape, lambda bi, t: (0,) * len(shape))

    pos_spec = pl.BlockSpec((chunk, GROUP_W), lambda bi, t: (t, 0))
    state_spec = pl.BlockSpec((bb, N_HEADS, HEAD_DIM, HEAD_DIM), lambda bi, t: (bi, 0, 0, 0))
    kern = functools.partial(_retention_kernel, bb=bb)
    return pl.pallas_call(
        kern, grid=(n_batch // bb, seq // chunk),
        in_specs=[row_spec(C_RQ), row_spec(C_RK), row_spec(C_RV), row_spec(C_RG), pos_spec, pos_spec,
                  const_spec((1, GROUP_W)), const_spec(dec.shape), const_spec(inter.shape), const_spec(wl.shape),
                  const_spec(dl.shape), state_spec],
        out_specs=[pl.BlockSpec((bb, chunk, GROUP_W), lambda bi, t: (bi, t, 0)), state_spec],
        out_shape=[jax.ShapeDtypeStruct((n_batch, seq, GROUP_W), out_dtype),
                   jax.ShapeDtypeStruct((n_batch, N_HEADS, HEAD_DIM, HEAD_DIM), F32)],
        compiler_params=_cp("arbitrary", "arbitrary"), name="retention")(
            proj3, proj3, proj3, proj3, cos, sin_signed, norm_g, dec, inter, wl, dl, s0)


def _s5_kernel(u_ref, wb_ref, a1_ref, a2_ref, h0_ref, wc_ref, d_ref, gw_ref, gb_ref, o_ref, hl_ref,
               hs_ref, ut_ref, yt_ref, *, nb, tt):
    c = pl.program_id(0)

    @pl.when(c == 0)
    def _():
        hl_ref[...] = h0_ref[...]

    halves = range(GROUP_W // LANES)
    for b in range(nb):
        ub = u_ref[b]
        for hf in halves:
            ut_ref[hf, pl.ds(b, tt, stride=nb), :] = ub[:, hf * LANES:(hf + 1) * LANES]
    u = jnp.concatenate([ut_ref[hf] for hf in halves], axis=1)
    hs_ref[...] = _dot(u.astype(BF16), wb_ref[...])
    a1 = jnp.broadcast_to(a1_ref[...], (nb, 2 * S5_W))
    a2 = jnp.broadcast_to(a2_ref[...], (nb, 2 * S5_W))

    def step(t, h):
        r0 = pl.multiple_of(t * nb, nb)
        swapped = jnp.concatenate([h[:, S5_W:], h[:, :S5_W]], axis=1)
        h = a1 * h + a2 * swapped + hs_ref[pl.ds(r0, nb), :]
        hs_ref[pl.ds(r0, nb), :] = h
        return h

    hl_ref[...] = lax.fori_loop(0, tt, step, hl_ref[...])
    y = _dot(hs_ref[...].astype(BF16), wc_ref[...]) + d_ref[...] * u
    g5 = jax.nn.gelu(y)
    yt = g5 * jax.nn.sigmoid(_dot(g5.astype(BF16), gw_ref[...]) + gb_ref[...])
    for hf in halves:
        yt_ref[hf] = yt[:, hf * LANES:(hf + 1) * LANES]
    for b in range(nb):
        o_ref[b] = jnp.concatenate([yt_ref[hf, pl.ds(b, tt, stride=nb), :] for hf in halves],
                                   axis=1).astype(o_ref.dtype)


def s5_mixer(proj3, tt, wb, a1, a2, h0, wc, d, glu_w, glu_b, out_dtype):
    nb, seq, _ = proj3.shape
    rows = tt * nb

    def const_spec(shape):
        return pl.BlockSpec(shape, lambda c: (0,) * len(shape))

    kern = functools.partial(_s5_kernel, nb=nb, tt=tt)
    return pl.pallas_call(
        kern, grid=(seq // tt,),
        in_specs=[pl.BlockSpec((nb, tt, GROUP_W), lambda c: (0, c, C_SU)),
                  const_spec(wb.shape), const_spec(a1.shape), const_spec(a2.shape), const_spec(h0.shape),
                  const_spec(wc.shape), const_spec(d.shape), const_spec(glu_w.shape), const_spec(glu_b.shape)],
        out_specs=[pl.BlockSpec((nb, tt, GROUP_W), lambda c: (0, c, 0)), const_spec(h0.shape)],
        out_shape=[jax.ShapeDtypeStruct((nb, seq, GROUP_W), out_dtype), jax.ShapeDtypeStruct(h0.shape, F32)],
        scratch_shapes=[pltpu.VMEM((rows, 2 * S5_W), F32), pltpu.VMEM((GROUP_W // LANES, rows, LANES), F32),
                        pltpu.VMEM((GROUP_W // LANES, rows, LANES), F32)],
        compiler_params=_cp("arbitrary"), name="s5")(proj3, wb, a1, a2, h0, wc, d, glu_w, glu_b)


def _s5_weights(a_re, a_im, log_dt, b_re, b_im, c_re, c_im):
    lam = lax.complex(a_re, a_im)
    a_bar = jnp.exp(lam * jnp.exp(log_dt))
    b_bar = ((a_bar - 1.0) / lam)[..., None] * lax.complex(b_re, b_im)
    eye = jnp.eye(S5_GROUPS, dtype=F32)

    def in_map(m):
        return jnp.einsum('gpc,gh->gchp', m, eye).reshape(S5_GROUPS * S5_GROUP, S5_W)

    def out_map(m):
        return jnp.einsum('gcp,gh->gphc', m, eye).reshape(S5_W, S5_GROUPS * S5_GROUP)

    wb = jnp.concatenate([in_map(b_bar.real), in_map(b_bar.imag)], axis=1).astype(BF16)
    wc = jnp.concatenate([out_map(c_re), -out_map(c_im)], axis=0).astype(BF16)
    ar = a_bar.real.reshape(1, S5_W)
    ai = a_bar.imag.reshape(1, S5_W)
    return wb, jnp.concatenate([ar, ar], axis=1), jnp.concatenate([-ai, ai], axis=1), wc


def _cross_attn_kernel(x_ref, wq_ref, k_ref, v_ref, wo_ref, g_ref, b_ref, o_ref):
    x = x_ref[...]
    q = (_dot(x.astype(BF16), wq_ref[...]) * QK_SCALE).astype(BF16)
    k = k_ref[0].astype(BF16)
    v = v_ref[0].astype(BF16)
    hs = [slice(h * HEAD_DIM, (h + 1) * HEAD_DIM) for h in range(N_HEADS)]
    s = [_dot_nt(q[:, sl], k[:, sl]) for sl in hs]
    e = [jnp.exp(a - jnp.max(a, axis=1, keepdims=True)) for a in s]
    p = [a / jnp.sum(a, axis=1, keepdims=True) for a in e]
    o = jnp.concatenate([_dot(p[h].astype(BF16), v[:, hs[h]]) for h in range(N_HEADS)], axis=1)
    y = ALPHA * x + _dot(o.astype(BF16), wo_ref[...])
    o_ref[...] = _layer_norm(y, g_ref[...], b_ref[...])


def cross_attn_ln(x, row_blk0, n_batch, seq, tq, mem_k, mem_v, wq, wo, g, b):
    nq = seq // tq

    def const_spec(shape):
        return pl.BlockSpec(shape, lambda bb, i: (0,) * len(shape))

    row_spec = pl.BlockSpec((tq, D_MODEL), lambda bb, i: (row_blk0 + bb * nq + i, 0))
    mem_spec = pl.BlockSpec((1, N_MEM, GROUP_W), lambda bb, i: (bb, 0, 0))
    return pl.pallas_call(
        _cross_attn_kernel, grid=(n_batch, nq),
        in_specs=[row_spec, const_spec(wq.shape), mem_spec, mem_spec, const_spec(wo.shape),
                  const_spec(g.shape), const_spec(b.shape)],
        out_specs=row_spec, out_shape=jax.ShapeDtypeStruct(x.shape, F32), input_output_aliases={0: 0},
        compiler_params=_cp("arbitrary", "arbitrary"), name="cross_attn")(x, wq, mem_k, mem_v, wo, g, b)


SWIGLU_ROWS = 256


def _swiglu_accumulate(xb_ref, wg, wu, wd, acc_ref, n_valid=None):
    wgb, wub, wdb = wg.astype(BF16), wu.astype(BF16), wd.astype(BF16)
    n_sub = xb_ref.shape[0] // SWIGLU_ROWS

    def hidden(r):
        xb = xb_ref[pl.ds(r * SWIGLU_ROWS, SWIGLU_ROWS), :]
        gate = _dot(xb, wgb)
        up = _dot(xb, wub)
        return (gate * jax.nn.sigmoid(gate) * up).astype(BF16)

    def first_sub_blocks(n):
        hid = hidden(0)
        for r in range(n):
            nxt = hidden(r + 1) if r + 1 < n else None
            acc_ref[pl.ds(r * SWIGLU_ROWS, SWIGLU_ROWS), :] += _dot(hid, wdb)
            hid = nxt

    if n_valid is None:
        first_sub_blocks(n_sub)
        return
    need = (n_valid + SWIGLU_ROWS - 1) // SWIGLU_ROWS
    for n in range(1, n_sub + 1):
        pl.when(need == n)(functools.partial(first_sub_blocks, n))


def _ffn_kernel(x_ref, wg_ref, wu_ref, wd_ref, g_ref, b_ref, o_ref, xb_ref, *, nf):
    j = pl.program_id(1)

    @pl.when(j == 0)
    def _():
        xb_ref[...] = x_ref[...].astype(BF16)
        o_ref[...] = jnp.zeros_like(o_ref)

    _swiglu_accumulate(xb_ref, wg_ref[...], wu_ref[...], wd_ref[...], o_ref)

    @pl.when(j == nf - 1)
    def _():
        o_ref[...] = _layer_norm(ALPHA * x_ref[...] + o_ref[...], g_ref[...], b_ref[...])


def ffn_ln(x, wg, wu, wd, g, b, tm, tf):
    m = x.shape[0]
    nf = D_FF // tf
    kern = functools.partial(_ffn_kernel, nf=nf)
    return pl.pallas_call(
        kern, grid=(m // tm, nf),
        in_specs=[pl.BlockSpec((tm, D_MODEL), lambda i, j: (i, 0)),
                  pl.BlockSpec((D_MODEL, tf), lambda i, j: (0, j)), pl.BlockSpec((D_MODEL, tf), lambda i, j: (0, j)),
                  pl.BlockSpec((tf, D_MODEL), lambda i, j: (j, 0)),
                  pl.BlockSpec((1, D_MODEL), lambda i, j: (0, 0)), pl.BlockSpec((1, D_MODEL), lambda i, j: (0, 0))],
        out_specs=pl.BlockSpec((tm, D_MODEL), lambda i, j: (i, 0)),
        out_shape=jax.ShapeDtypeStruct((m, D_MODEL), F32),
        scratch_shapes=[pltpu.VMEM((tm, D_MODEL), BF16)],
        compiler_params=_cp("arbitrary", "arbitrary"), name="ffn")(x, wg, wu, wd, g, b)


def _router_kernel(x_ref, w_ref, b_ref, o_ref):
    logits = jnp.dot(x_ref[...], w_ref[...], preferred_element_type=F32, precision=lax.Precision.HIGHEST) + b_ref[...]
    lane = lax.broadcasted_iota(jnp.int32, logits.shape, 1)
    neg = jnp.float32(-jnp.inf)
    lg = jnp.where(lane < N_EXPERTS, logits, neg)
    m1 = jnp.max(lg, axis=1, keepdims=True)
    i1 = jnp.min(jnp.where(lg == m1, lane, LANES), axis=1, keepdims=True)
    lg2 = jnp.where(lane == i1, neg, lg)
    m2 = jnp.max(lg2, axis=1, keepdims=True)
    i2 = jnp.min(jnp.where(lg2 == m2, lane, LANES), axis=1, keepdims=True)
    e2 = jnp.exp(m2 - m1)
    g1 = 1.0 / (1.0 + e2)
    g2 = e2 / (1.0 + e2)
    out = jnp.where(lane == 0, i1.astype(F32), jnp.where(lane == 1, i2.astype(F32),
                    jnp.where(lane == 2, g1, jnp.where(lane == 3, g2, 0.0))))
    o_ref[...] = out


def router(x, w_pad, b_pad, tm):
    m = x.shape[0]
    return pl.pallas_call(
        _router_kernel, grid=(m // tm,),
        in_specs=[pl.BlockSpec((tm, D_MODEL), lambda i: (i, 0)), pl.BlockSpec((D_MODEL, LANES), lambda i: (0, 0)),
                  pl.BlockSpec((1, LANES), lambda i: (0, 0))],
        out_specs=pl.BlockSpec((tm, LANES), lambda i: (i, 0)),
        out_shape=jax.ShapeDtypeStruct((m, LANES), F32),
        compiler_params=_cp("arbitrary"), name="router")(x, w_pad, b_pad)


def _moe_ffn_kernel(te_ref, nu_ref, tr_ref, x_ref, wg_ref, wu_ref, wd_ref, o_ref, xb_ref):
    i = pl.program_id(0)
    j = pl.program_id(1)
    used = i < nu_ref[0]

    @pl.when(used)
    def _():
        @pl.when(j == 0)
        def _():
            xb_ref[...] = x_ref[...].astype(BF16)
            o_ref[...] = jnp.zeros_like(o_ref)

        _swiglu_accumulate(xb_ref, wg_ref[0], wu_ref[0], wd_ref[0], o_ref, tr_ref[i])

    @pl.when(jnp.logical_and(jnp.logical_not(used), j == 0))
    def _():
        o_ref[...] = jnp.zeros_like(o_ref)


def moe_ffn(x_sorted, tile_expert, n_used, tile_rows, wg, wu, wd, tf):
    n_rows = x_sorted.shape[0]
    n_tiles = n_rows // MOE_TILE
    nf = D_FF // tf

    def jj(i, j, nu):
        return jnp.where(i < nu[0], j, nf - 1)

    gs = pltpu.PrefetchScalarGridSpec(
        num_scalar_prefetch=3, grid=(n_tiles, nf),
        in_specs=[pl.BlockSpec((MOE_TILE, D_MODEL), lambda i, j, te, nu, tr: (i, 0)),
                  pl.BlockSpec((1, D_MODEL, tf), lambda i, j, te, nu, tr: (te[i], 0, jj(i, j, nu))),
                  pl.BlockSpec((1, D_MODEL, tf), lambda i, j, te, nu, tr: (te[i], 0, jj(i, j, nu))),
                  pl.BlockSpec((1, tf, D_MODEL), lambda i, j, te, nu, tr: (te[i], jj(i, j, nu), 0))],
        out_specs=pl.BlockSpec((MOE_TILE, D_MODEL), lambda i, j, te, nu, tr: (i, 0)),
        scratch_shapes=[pltpu.VMEM((MOE_TILE, D_MODEL), BF16)])
    return pl.pallas_call(
        _moe_ffn_kernel, grid_spec=gs, out_shape=jax.ShapeDtypeStruct((n_rows, D_MODEL), F32),
        compiler_params=_cp("arbitrary", "arbitrary"), name="moe_ffn")(
            tile_expert, n_used, tile_rows, x_sorted, wg, wu, wd)


def _combine_ln_kernel(x_ref, r_ref, ya_ref, yb_ref, g_ref, b_ref, o_ref):
    r = r_ref[...]
    y = r[:, TOP_K:TOP_K + 1] * ya_ref[...] + r[:, TOP_K + 1:TOP_K + 2] * yb_ref[...]
    o_ref[...] = _layer_norm(ALPHA * x_ref[...] + y, g_ref[...], b_ref[...])


def combine_ln(x, r, ya, yb, g, b, tm, row_blk0, rows):
    row = pl.BlockSpec((tm, D_MODEL), lambda i: (row_blk0 + i, 0))
    vec = pl.BlockSpec((1, D_MODEL), lambda i: (0, 0))
    return pl.pallas_call(
        _combine_ln_kernel, grid=(rows // tm,),
        in_specs=[row, pl.BlockSpec((tm, LANES), lambda i: (row_blk0 + i, 0)), row, row, vec, vec],
        out_specs=pl.BlockSpec((tm, D_MODEL), lambda i: (i, 0)),
        out_shape=jax.ShapeDtypeStruct((rows, D_MODEL), F32),
        compiler_params=_cp("arbitrary"), name="combine_ln")(x, r, ya, yb, g, b)


def moe_ln(x, router_w, router_b, wg, wu, wd, g, b, tm, splits):
    m = x.shape[0]
    w_pad = jnp.zeros((D_MODEL, LANES), F32).at[:, :N_EXPERTS].set(router_w)
    b_pad = jnp.zeros((1, LANES), F32).at[0, :N_EXPERTS].set(router_b)
    r = router(x, w_pad, b_pad, tm)
    top_idx = r[:, :TOP_K].astype(jnp.int32)
    n_slot = m * TOP_K
    onehot = (top_idx[:, :, None] == jnp.arange(N_EXPERTS, dtype=jnp.int32)).astype(jnp.int32)
    per_tok = onehot[:, 0] + onehot[:, 1]
    before = jnp.cumsum(per_tok, axis=0) - per_tok
    counts = jnp.sum(per_tok, axis=0)
    tiles_per = (counts + MOE_TILE - 1) // MOE_TILE
    tile_end = jnp.cumsum(tiles_per)
    row0 = (tile_end - tiles_per) * MOE_TILE
    dest = jnp.sum(onehot * (before + row0)[:, None, :], axis=2)
    n_tiles = -(-n_slot // MOE_TILE) + N_EXPERTS
    n_rows = n_tiles * MOE_TILE
    row_tok = (jnp.arange(n_rows, dtype=jnp.int32) % m).at[dest.reshape(-1)].set(
        jnp.arange(n_slot, dtype=jnp.int32) // TOP_K, unique_indices=True, mode='promise_in_bounds')
    n_used = tile_end[-1:].astype(jnp.int32)
    tile_ids = jnp.minimum(jnp.arange(n_tiles, dtype=jnp.int32), n_used[0] - 1)
    tile_expert = jnp.minimum(jnp.sum((tile_end[None, :] <= tile_ids[:, None]).astype(jnp.int32), axis=1),
                              N_EXPERTS - 1)
    tile_rows = jnp.clip(counts[tile_expert] - (tile_ids - (tile_end - tiles_per)[tile_expert]) * MOE_TILE,
                         0, MOE_TILE).astype(jnp.int32)
    x_sorted = x.at[row_tok].get(mode='promise_in_bounds')
    y_sorted = moe_ffn(x_sorted, tile_expert, n_used, tile_rows, wg, wu, wd, tf=512)
    ya = y_sorted.at[dest[:, 0]].get(mode='promise_in_bounds')
    yb = y_sorted.at[dest[:, 1]].get(mode='promise_in_bounds')
    return [combine_ln(x, r, ya, yb, g, b, t, blk0, rows) for blk0, rows, t in splits]


def kernel(x_prompt, x_sample, cache_sb_k, cache_sb_v, cache_mem_k, cache_mem_v, state_ml_C, state_ml_n, state_ml_m, state_rt_S, state_s5_re, state_s5_im, page_table, mem_prompt, w_in, sb_bias, ml_b_i, ml_b_f, ml_norm_g, rt_norm_g, s5_A_re, s5_A_im, s5_log_dt, s5_B_re, s5_B_im, s5_C_re, s5_C_im, s5_D, s5_glu_w, s5_glu_b, w_out, ca_wq, ca_wk, ca_wv, ca_wo, ln_g, ln_b, ffn_w_gate, ffn_w_up, ffn_w_down, moe_router_w, moe_router_b, moe_w_gate, moe_w_up, moe_w_down):
    bp, tp, _ = x_prompt.shape
    bs, ts, _ = x_sample.shape
    n_p, n_s = bp * tp, bs * ts
    tm = 640
    assert (n_p + n_s) % (2 * tm) == 0 and n_p % n_s == 0 and tp % 512 == 0 and bp % 8 == 0 and bs % 8 == 0
    x = jnp.concatenate([x_prompt.reshape(n_p, D_MODEL), x_sample.reshape(n_s, D_MODEL)], axis=0)
    uu = _suffix_matrix()
    g_off = 7 * GROUP_W
    half = HEAD_DIM // 2
    freq = ROPE_BASE ** (-jnp.arange(half, dtype=F32) / half)

    def rope_tables(pos):
        ang = pos.astype(F32)[:, None] * freq[None, :]
        cos, sin = jnp.cos(ang), jnp.sin(ang)
        return (jnp.tile(jnp.concatenate([cos, cos], axis=1), (1, N_HEADS)),
                jnp.tile(jnp.concatenate([-sin, sin], axis=1), (1, N_HEADS)))

    cos_p, sin_p = rope_tables(jnp.arange(tp, dtype=jnp.int32))
    cos_s, sin_s = rope_tables(PAST_LEN + jnp.arange(ts, dtype=jnp.int32))
    cache_kt = cache_sb_k.transpose(0, 1, 3, 4, 2)
    cache_vt = cache_sb_v.transpose(0, 1, 3, 4, 2)

    p_st = [[] for _ in range(10)]
    s_st = [[] for _ in range(8)]
    for l in range(DEPTH):
        wl = w_in[l]
        w_cat = jnp.concatenate([wl[:, :g_off], wl[:, g_off + 2 * N_HEADS:], wl[:, g_off:g_off + 2 * N_HEADS],
                                 jnp.zeros((D_MODEL, PROJ_W - wl.shape[1]), F32)], axis=1).astype(BF16)
        proj_p, kt_p, vt_p = in_proj_prompt(x, w_cat, bp, tp, 512)
        proj_s = linear(x, w_cat, n_s, n_p // n_s, n_s)
        proj_p3 = proj_p.reshape(bp, tp, PROJ_W)
        proj_s3 = proj_s.reshape(bs, ts, PROJ_W)
        gate_bias = jnp.zeros((1, LANES), F32).at[0, :2 * N_HEADS].set(jnp.concatenate([ml_b_i[l], ml_b_f[l]]))
        ml_g = ml_norm_g[l][None, :]
        rt_g = rt_norm_g[l][None, :]
        wb, a1, a2, wc = _s5_weights(s5_A_re[l], s5_A_im[l], s5_log_dt[l], s5_B_re[l], s5_B_im[l],
                                     s5_C_re[l], s5_C_im[l])
        s5_d = s5_D[l][None, :]
        glu_w = s5_glu_w[l].astype(BF16)
        glu_b = s5_glu_b[l][None, :]

        o_sb_p = sb_attention_prompt(proj_p, sb_bias[l], uu, bp, tp, tq=256)
        o_ml_p, ml_c_p, ml_n_p, ml_m_p = mlstm_mixer_prompt(proj_p3, gate_bias, ml_g, 8, MIX_DTYPE)
        o_rt_p, rs_p = retention_mixer(proj_p3, cos_p, sin_p, rt_g,
                                       jnp.zeros((bp, N_HEADS, HEAD_DIM, HEAD_DIM), F32), 8, MIX_DTYPE)
        o_ml_p = o_ml_p.reshape(n_p, GROUP_W)
        o_rt_p = o_rt_p.reshape(n_p, GROUP_W)
        o_s5_p, h5_p = s5_mixer(proj_p3, 64, wb, a1, a2, jnp.zeros((bp, 2 * S5_W), F32), wc, s5_d, glu_w, glu_b,
                                MIX_DTYPE)
        o_s5_p = o_s5_p.reshape(n_p, GROUP_W)

        o_sb_s = sb_attention_sample(proj_s, 0, cache_kt, cache_vt, page_table, l, sb_bias[l], uu, n_pp=16)
        cn0 = jnp.concatenate([state_ml_C[:, l], state_ml_n[:, l][..., None],
                               jnp.zeros((bs, N_HEADS, HEAD_DIM, LANES - HEAD_DIM - 1), F32)], axis=-1)
        o_ml_s, cn_s, m_s = mlstm_mixer(proj_s3, gate_bias, ml_g, cn0, state_ml_m[:, l], 8, F32)
        o_rt_s, rs_s = retention_mixer(proj_s3, cos_s, sin_s, rt_g, state_rt_S[:, l], 8, F32)
        o_ml_s = o_ml_s.reshape(n_s, GROUP_W)
        o_rt_s = o_rt_s.reshape(n_s, GROUP_W)
        h0_s = jnp.concatenate([state_s5_re[:, l].reshape(bs, S5_W), state_s5_im[:, l].reshape(bs, S5_W)], axis=1)
        o_s5_s, h5_s = s5_mixer(proj_s3, ts, wb, a1, a2, h0_s, wc, s5_d, glu_w, glu_b, F32)
        o_s5_s = o_s5_s.reshape(n_s, GROUP_W)

        wo_mix = w_out[l].astype(BF16)
        g0, b0 = ln_g[l, 0][None, :], ln_b[l, 0][None, :]
        x = mix_out_ln(x, (o_sb_p, o_ml_p, o_rt_p, o_s5_p), wo_mix, g0, b0, 512, 0)
        x = mix_out_ln(x, (o_sb_s, o_ml_s, o_rt_s, o_s5_s), wo_mix, g0, b0, n_s, n_p // n_s)

        mem_kv = linear(mem_prompt.reshape(bp * N_MEM, D_MODEL),
                        jnp.concatenate([ca_wk[l], ca_wv[l]], axis=1).astype(BF16), 512)
        mk_p = mem_kv[:, :GROUP_W].reshape(bp, N_MEM, GROUP_W)
        mv_p = mem_kv[:, GROUP_W:].reshape(bp, N_MEM, GROUP_W)
        wq = ca_wq[l].astype(BF16)
        wo = ca_wo[l].astype(BF16)
        g1, b1 = ln_g[l, 1][None, :], ln_b[l, 1][None, :]
        x = cross_attn_ln(x, 0, bp, tp, 512, mk_p, mv_p, wq, wo, g1, b1)
        x = cross_attn_ln(x, n_p // ts, bs, ts, ts, cache_mem_k[:, l].reshape(bs, N_MEM, GROUP_W),
                          cache_mem_v[:, l].reshape(bs, N_MEM, GROUP_W), wq, wo, g1, b1)

        g2, b2 = ln_g[l, 2][None, :], ln_b[l, 2][None, :]
        j = l // 2
        last = l == DEPTH - 1
        if l % 2 == 0:
            x = ffn_ln(x, ffn_w_gate[j], ffn_w_up[j], ffn_w_down[j], g2, b2, 2 * tm, tf=512)
            y_out = (x[:n_p], x[n_p:]) if last else None
        else:
            splits = [(0, n_p, 512), (n_p // n_s, n_s, n_s)] if last else [(0, n_p + n_s, tm)]
            y_out = moe_ln(x, moe_router_w[j], moe_router_b[j], moe_w_gate[j], moe_w_up[j], moe_w_down[j],
                           g2, b2, tm, splits)
            x = None if last else y_out[0]

        def heads(a, nb_, t_):
            return a.reshape(nb_, t_, N_HEADS, HEAD_DIM)

        p_st[0].append(kt_p)
        p_st[1].append(vt_p)
        p_st[2].append(heads(mk_p, bp, N_MEM))
        p_st[3].append(heads(mv_p, bp, N_MEM))
        s_st[0].append(heads(proj_s[:, C_SK * GROUP_W:(C_SK + 1) * GROUP_W], bs, ts))
        s_st[1].append(heads(proj_s[:, C_SV * GROUP_W:(C_SV + 1) * GROUP_W], bs, ts))
        ml_p = (ml_c_p, ml_n_p, ml_m_p)
        ml_s = (cn_s[..., :HEAD_DIM], cn_s[..., HEAD_DIM], m_s[:, 0, :N_HEADS])
        for st, ml, rs, h5, nb_ in ((p_st, ml_p, rs_p, h5_p, bp), (s_st, ml_s, rs_s, h5_s, bs)):
            off = 4 if st is p_st else 2
            st[off + 0].append(ml[0])
            st[off + 1].append(ml[1])
            st[off + 2].append(ml[2])
            st[off + 3].append(rs)
            st[off + 4].append(h5[:, :S5_W].reshape(nb_, S5_GROUPS, S5_STATE))
            st[off + 5].append(h5[:, S5_W:].reshape(nb_, S5_GROUPS, S5_STATE))

    y_prompt = y_out[0].reshape(bp, tp, D_MODEL)
    y_sample = y_out[1].reshape(bs, ts, D_MODEL)
    p_out = [jnp.stack(a, axis=1) for a in p_st]
    for i in range(2):
        p_out[i] = p_out[i].reshape(bp, DEPTH, N_HEADS, HEAD_DIM, tp).transpose(0, 1, 4, 2, 3)
    s_out = [jnp.stack(a, axis=1) for a in s_st]
    return (y_prompt, y_sample, *p_out, *s_out)
```

```python
import functools
import math

import numpy as np
import jax
import jax.numpy as jnp
from jax import lax
from jax.experimental import pallas as pl
from jax.experimental.pallas import tpu as pltpu

F32 = jnp.float32
BF16 = jnp.bfloat16

D_MODEL = 1024
DEPTH = 2
PAST_LEN = 8192
PAGE_SIZE = 128
HEAD_DIM = 64
N_HEADS = 4
GROUP_W = N_HEADS * HEAD_DIM
S5_GROUPS = 16
S5_GROUP = 16
S5_STATE = 64
S5_W = S5_GROUPS * S5_STATE
N_MEM = 256
D_FF = 3584
N_EXPERTS = 8
TOP_K = 2
CHUNK = 64
ROPE_BASE = 10000.0
LN_EPS = 1e-5
GN_EPS = 1e-6
ALPHA = (2 * DEPTH) ** 0.25
QK_SCALE = HEAD_DIM ** -0.5
LOG2E = math.log2(math.e)

LANES = 128
PROJ_W = 25 * LANES
C_SQ, C_SK, C_SV, C_MQ, C_MK, C_MV, C_MO, C_RQ, C_RK, C_RV, C_RG, C_SU = range(12)
C_GATES = 12 * GROUP_W // LANES
VMEM_LIMIT = 48 * 1024 * 1024
MOE_TILE = 1024
MIX_DTYPE = BF16


def _cp(*sem):
    return pltpu.CompilerParams(dimension_semantics=sem, vmem_limit_bytes=VMEM_LIMIT)


def _dot(a, b):
    return jnp.dot(a, b, preferred_element_type=F32)


def _dot_nt(a, b):
    return lax.dot_general(a, b, (((1,), (1,)), ((), ())), preferred_element_type=F32)


def _dot_tn(a, b):
    return lax.dot_general(a, b, (((0,), (0,)), ((), ())), preferred_element_type=F32)


def _layer_norm(y, g, b):
    mu = jnp.mean(y, axis=-1, keepdims=True)
    yc = y - mu
    var = jnp.mean(yc * yc, axis=-1, keepdims=True)
    return yc * lax.rsqrt(var + LN_EPS) * g + b


def _row_sum(x, scale=1.0):
    ones = jnp.full((x.shape[1], LANES), scale, BF16)
    hi = x.astype(BF16)
    lo = (x - hi.astype(F32)).astype(BF16)
    return (_dot(hi, ones) + _dot(lo, ones))[:, :x.shape[1]]


def _cumsum_rows(tril, x):
    hi = x.astype(BF16)
    lo = (x - hi.astype(F32)).astype(BF16)
    return _dot(tril, hi) + _dot(tril, lo)


def _head_norm_all(hd):
    inv = 1.0 / HEAD_DIM
    mu = {p: _row_sum(x, inv) for p, x in hd.items()}
    hc = {p: hd[p] - mu[p] for p in hd}
    var = {p: _row_sum(hc[p] * hc[p], inv) for p in hd}
    return {p: hc[p] * lax.rsqrt(var[p] + GN_EPS) for p in hd}


def _neg_softplus(z):
    return -(jnp.maximum(z, 0.0) + jnp.log1p(jnp.exp(-jnp.abs(z))))


def _log_sigmoid(z):
    return _neg_softplus(-z)


def _linear_kernel(x_ref, w_ref, o_ref):
    o_ref[...] = _dot(x_ref[...].astype(BF16), w_ref[...]).astype(o_ref.dtype)


def linear(x, w, tm, row_blk0=0, n_rows=None, out_dtype=F32):
    m, k = x.shape
    m = m if n_rows is None else n_rows
    n = w.shape[1]
    return pl.pallas_call(
        _linear_kernel, grid=(m // tm,),
        in_specs=[pl.BlockSpec((tm, k), lambda i: (row_blk0 + i, 0)), pl.BlockSpec((k, n), lambda i: (0, 0))],
        out_specs=pl.BlockSpec((tm, n), lambda i: (i, 0)),
        out_shape=jax.ShapeDtypeStruct((m, n), out_dtype),
        compiler_params=_cp("arbitrary"), name="linear")(x, w)


def _in_proj_kernel(x_ref, w_ref, o_ref, kt_ref, vt_ref):
    o = _dot(x_ref[...].astype(BF16), w_ref[...])
    o_ref[...] = o
    kt_ref[0] = o[:, C_SK * GROUP_W:(C_SK + 1) * GROUP_W].T
    vt_ref[0] = o[:, C_SV * GROUP_W:(C_SV + 1) * GROUP_W].T


def in_proj_prompt(x, w, n_batch, seq, tm):
    k = x.shape[1]
    n = w.shape[1]
    nt = seq // tm
    t_spec = pl.BlockSpec((1, GROUP_W, tm), lambda i: (i // nt, 0, i % nt))
    t_shape = jax.ShapeDtypeStruct((n_batch, GROUP_W, seq), F32)
    return pl.pallas_call(
        _in_proj_kernel, grid=(n_batch * nt,),
        in_specs=[pl.BlockSpec((tm, k), lambda i: (i, 0)), pl.BlockSpec((k, n), lambda i: (0, 0))],
        out_specs=[pl.BlockSpec((tm, n), lambda i: (i, 0)), t_spec, t_spec],
        out_shape=[jax.ShapeDtypeStruct((n_batch * seq, n), F32), t_shape, t_shape],
        compiler_params=_cp("arbitrary"), name="in_proj")(x, w)


def _mix_out_ln_kernel(x_ref, a_ref, b_ref, c_ref, d_ref, w_ref, g_ref, bias_ref, o_ref):
    h = sum(_dot(p[...].astype(BF16), w_ref[pl.ds(n * GROUP_W, GROUP_W), :])
            for n, p in enumerate((a_ref, b_ref, c_ref, d_ref)))
    o_ref[...] = _layer_norm(ALPHA * x_ref[...] + h, g_ref[...], bias_ref[...])


def mix_out_ln(x, parts, w, g, b, tm, row_blk0):
    rows = parts[0].shape[0]
    row_spec = pl.BlockSpec((tm, D_MODEL), lambda i: (row_blk0 + i, 0))
    part_spec = pl.BlockSpec((tm, GROUP_W), lambda i: (i, 0))
    vec = pl.BlockSpec((1, D_MODEL), lambda i: (0, 0))
    return pl.pallas_call(
        _mix_out_ln_kernel, grid=(rows // tm,),
        in_specs=[row_spec, part_spec, part_spec, part_spec, part_spec,
                  pl.BlockSpec((D_MODEL, D_MODEL), lambda i: (0, 0)), vec, vec],
        out_specs=row_spec, out_shape=jax.ShapeDtypeStruct(x.shape, F32), input_output_aliases={0: 0},
        compiler_params=_cp("arbitrary"), name="mix_out_ln")(x, *parts, w, g, b)


def _suffix_matrix():
    j = np.arange(LANES)
    u = (j[:, None] >= j[None, :]).astype(np.float32)
    uu = np.concatenate([u, np.ones((LANES, LANES), np.float32)], axis=1)
    return jnp.asarray(np.concatenate([uu, uu], axis=0), dtype=BF16)


def _suffix_sums(lr, uu):
    hi = lr.astype(BF16)
    lo = (lr - hi.astype(F32)).astype(BF16)
    r = _dot(jnp.concatenate([hi, lo], axis=1), uu)
    return r[:, :LANES], r[:, LANES:]


def _log2_rem(z2):
    return jnp.minimum(-z2, 0.0) - jnp.log2(1.0 + jnp.exp2(-jnp.abs(z2)))


def _sb_prompt_kernel(bias_ref, q_ref, k_ref, v_ref, uu_ref, o_ref, acc_ref, car_ref, kb_ref, vb_ref, *, tq):
    i = pl.program_id(1)
    tk = LANES
    nsub = tq // tk
    acc_ref[...] = jnp.zeros_like(acc_ref)
    car_ref[...] = jnp.zeros_like(car_ref)

    @pl.when(i == 0)
    def _():
        for h in range(N_HEADS):
            kb_ref[h] = k_ref[:, h * HEAD_DIM:(h + 1) * HEAD_DIM].astype(BF16)
            vb_ref[h] = v_ref[:, h * HEAD_DIM:(h + 1) * HEAD_DIM].astype(BF16)

    q = (q_ref[...] * (QK_SCALE * LOG2E)).astype(BF16)
    qh = [q[:, h * HEAD_DIM:(h + 1) * HEAD_DIM] for h in range(N_HEADS)]
    b2 = [bias_ref[h] * LOG2E for h in range(N_HEADS)]
    uu = uu_ref[...]
    row = lax.broadcasted_iota(jnp.int32, (tq, tk), 0)
    col = lax.broadcasted_iota(jnp.int32, (tq, tk), 1)
    heads = range(N_HEADS)

    def block_pair(j_hi, causal_hi, causal_lo):
        r0 = [pl.multiple_of((j_hi - d) * tk, tk) for d in range(2)]
        z2 = [[_dot_nt(qh[h], kb_ref[h, pl.ds(r0[d], tk), :]) + b2[h] for h in heads] for d in range(2)]
        cs, tot = [], []
        for d, causal in enumerate((causal_hi, causal_lo)):
            lr = [_log2_rem(z) for z in z2[d]]
            if causal is not None:
                lr = [jnp.where(causal, a, 0.0) for a in lr]
            c, t = _suffix_sums(jnp.concatenate(lr, axis=0), uu)
            cs.append(c)
            tot.append(t)
        car = [car_ref[h] for h in heads]
        pv = []
        for d, causal in enumerate((causal_hi, causal_lo)):
            w = [jnp.exp2(z2[d][h] + cs[d][h * tq:(h + 1) * tq] + car[h]) for h in heads]
            if causal is not None:
                w = [jnp.where(causal, a, 0.0) for a in w]
            pv.append([_dot(w[h].astype(BF16), vb_ref[h, pl.ds(r0[d], tk), :]) for h in heads])
            car = [car[h] + tot[d][h * tq:(h + 1) * tq] for h in heads]
        for h in heads:
            acc_ref[h] += pv[0][h] + pv[1][h]
            car_ref[h] = car[h]

    assert nsub == 2
    block_pair(i * nsub + 1, (col + tk) < row, col < row)

    def body(jj, carry):
        block_pair(i * nsub - 1 - 2 * jj, None, None)
        return carry

    lax.fori_loop(0, i, body, 0)
    o_ref[...] = jnp.concatenate([acc_ref[h] for h in range(N_HEADS)], axis=1).astype(o_ref.dtype)


def sb_attention_prompt(proj, sb_bias, uu, n_batch, seq, tq):
    nq = seq // tq
    kern = functools.partial(_sb_prompt_kernel, tq=tq)
    return pl.pallas_call(
        kern, grid=(n_batch, nq),
        in_specs=[pl.BlockSpec(memory_space=pltpu.SMEM),
                  pl.BlockSpec((tq, GROUP_W), lambda b, i: (b * nq + i, C_SQ)),
                  pl.BlockSpec((seq, GROUP_W), lambda b, i: (b, C_SK)),
                  pl.BlockSpec((seq, GROUP_W), lambda b, i: (b, C_SV)),
                  pl.BlockSpec((2 * LANES, 2 * LANES), lambda b, i: (0, 0))],
        out_specs=pl.BlockSpec((tq, GROUP_W), lambda b, i: (b * nq + i, 0)),
        out_shape=jax.ShapeDtypeStruct((n_batch * seq, GROUP_W), MIX_DTYPE),
        scratch_shapes=[pltpu.VMEM((N_HEADS, tq, HEAD_DIM), F32), pltpu.VMEM((N_HEADS, tq, LANES), F32),
                        pltpu.VMEM((N_HEADS, seq, HEAD_DIM), BF16), pltpu.VMEM((N_HEADS, seq, HEAD_DIM), BF16)],
        compiler_params=_cp("arbitrary", "arbitrary"), name="sb_prompt")(sb_bias, proj, proj, proj, uu)


def _sb_sample_kernel(pt_ref, bias_ref, q_ref, kn_ref, vn_ref, u8_ref, uu_ref, *rest, n_pp, n_steps):
    k_refs = rest[:n_pp]
    v_refs = rest[n_pp:2 * n_pp]
    o_ref = rest[2 * n_pp]
    acc_ref, car_ref = rest[2 * n_pp + 1:]
    s = pl.program_id(1)
    nq = q_ref.shape[0]
    rows = N_HEADS * nq
    q = (q_ref[...] * (QK_SCALE * LOG2E)).astype(BF16)
    row_head = lax.broadcasted_iota(jnp.int32, (rows, GROUP_W), 0) // nq
    own = row_head == lax.broadcasted_iota(jnp.int32, (rows, GROUP_W), 1) // HEAD_DIM
    q_bd = jnp.where(own, jnp.concatenate([q] * N_HEADS, axis=0), jnp.zeros((), BF16))
    row_head_l = lax.broadcasted_iota(jnp.int32, (rows, LANES), 0) // nq
    b2 = jnp.zeros((rows, LANES), F32)
    for h in range(N_HEADS):
        b2 = jnp.where(row_head_l == h, bias_ref[h] * LOG2E, b2)

    @pl.when(s == 0)
    def _():
        kn = kn_ref[...].astype(BF16)
        vn = vn_ref[...].astype(BF16)
        t = lax.broadcasted_iota(jnp.int32, (rows, nq), 0) % nq
        causal = lax.broadcasted_iota(jnp.int32, (rows, nq), 1) < t
        z2 = _dot_nt(q_bd, kn) + b2[:, :nq]
        lr = jnp.where(causal, _log2_rem(z2), 0.0)
        cs = jnp.dot(lr, u8_ref[...], preferred_element_type=F32, precision=lax.Precision.HIGHEST)
        w = jnp.where(causal, jnp.exp2(z2 + cs), 0.0)
        acc_ref[...] = _dot(w.astype(BF16), vn)
        car_ref[...] = jnp.broadcast_to(jnp.sum(lr, axis=1, keepdims=True), (rows, LANES))

    z2s = [_dot(q_bd, k_refs[p][0, 0].reshape(GROUP_W, PAGE_SIZE).astype(BF16)) + b2 for p in range(n_pp)]
    cs_all, tot_all = _suffix_sums(_log2_rem(jnp.concatenate(z2s, axis=0)), uu_ref[...])
    car = car_ref[...]
    acc = acc_ref[...]
    for p in range(n_pp):
        w = jnp.exp2(z2s[p] + cs_all[p * rows:(p + 1) * rows] + car)
        acc = acc + _dot_nt(w.astype(BF16), v_refs[p][0, 0].reshape(GROUP_W, PAGE_SIZE).astype(BF16))
        car = car + tot_all[p * rows:(p + 1) * rows]
    car_ref[...] = car
    acc_ref[...] = acc

    @pl.when(s == n_steps - 1)
    def _():
        kept = jnp.where(own, acc, 0.0)
        o_ref[...] = sum(kept[h * nq:(h + 1) * nq] for h in range(N_HEADS))


def sb_attention_sample(proj, row_blk0, cache_kt, cache_vt, page_table, layer, sb_bias, uu, n_pp):
    n_batch, n_pages = page_table.shape
    nq = 8
    n_steps = n_pages // n_pp
    u8 = jnp.asarray((np.arange(nq)[:, None] >= np.arange(nq)[None, :]).astype(np.float32))

    def page_spec(p):
        return pl.BlockSpec((1, 1, N_HEADS, HEAD_DIM, PAGE_SIZE),
                            lambda b, s, pt: (pt[b, n_pages - 1 - (s * n_pp + p)], layer, 0, 0, 0))

    def row_spec(cblk):
        return pl.BlockSpec((nq, GROUP_W), lambda b, s, pt: (row_blk0 + b, cblk))

    kern = functools.partial(_sb_sample_kernel, n_pp=n_pp, n_steps=n_steps)
    gs = pltpu.PrefetchScalarGridSpec(
        num_scalar_prefetch=1, grid=(n_batch, n_steps),
        in_specs=[pl.BlockSpec(memory_space=pltpu.SMEM), row_spec(C_SQ), row_spec(C_SK), row_spec(C_SV),
                  pl.BlockSpec((nq, nq), lambda b, s, pt: (0, 0)),
                  pl.BlockSpec((2 * LANES, 2 * LANES), lambda b, s, pt: (0, 0))]
                 + [page_spec(p) for p in range(n_pp)] * 2,
        out_specs=pl.BlockSpec((nq, GROUP_W), lambda b, s, pt: (b, 0)),
        scratch_shapes=[pltpu.VMEM((N_HEADS * nq, GROUP_W), F32), pltpu.VMEM((N_HEADS * nq, LANES), F32)])
    return pl.pallas_call(
        kern, grid_spec=gs, out_shape=jax.ShapeDtypeStruct((n_batch * nq, GROUP_W), F32),
        compiler_params=_cp("arbitrary", "arbitrary"), name="sb_sample")(
            page_table, sb_bias, proj, proj, proj, u8, uu, *([cache_kt] * n_pp), *([cache_vt] * n_pp))


def _mlstm_kernel(m0_ref, q_ref, k_ref, v_ref, og_ref, gt_ref, gb_ref, ng_ref, tril_ref, cn0_ref,
                  o_ref, cn_ref, m_ref, ms_ref, *, bb, chunk, n_t):
    bi = pl.program_id(0)
    t = pl.program_id(1)
    seqs = range(bb)
    pairs = [(b, h) for b in seqs for h in range(N_HEADS)]

    @pl.when(t == 0)
    def _():
        cn_ref[...] = cn0_ref[...]
        for b, h in pairs:
            ms_ref[b * N_HEADS + h] = jnp.full((1, LANES), m0_ref[bi * bb + b, h], F32)

    tril = tril_ref[...]
    tri_mask = lax.broadcasted_iota(jnp.int32, (chunk, chunk), 1) <= lax.broadcasted_iota(jnp.int32, (chunk, chunk), 0)
    lane = lax.broadcasted_iota(jnp.int32, (chunk, HEAD_DIM), 1)
    ones_col = jnp.where(lane == 0, 1.0, 0.0).astype(F32)
    hs = lambda h: slice(h * HEAD_DIM, (h + 1) * HEAD_DIM)

    gt = [gt_ref[b] + gb_ref[...] for b in seqs]
    bc = [_cumsum_rows(tril, _log_sigmoid(g)) for g in gt]
    gt_t = [g.T for g in gt]
    bc_t = [x.T for x in bc]
    q = [q_ref[b].astype(BF16) for b in seqs]
    k = [(k_ref[b] * QK_SCALE).astype(BF16) for b in seqs]
    v = [v_ref[b] for b in seqs]
    m_prev = {p: ms_ref[p[0] * N_HEADS + p[1]][:, :1] for p in pairs}
    ig_col = {(b, h): gt[b][:, h:h + 1] for b, h in pairs}
    bc_col = {(b, h): bc[b][:, N_HEADS + h:N_HEADS + h + 1] for b, h in pairs}
    dm = {(b, h): jnp.where(tri_mask, bc_col[b, h] - (bc_t[b][N_HEADS + h:N_HEADS + h + 1, :] - gt_t[b][h:h + 1, :]),
                            -jnp.inf) for b, h in pairs}
    a = {p: bc_col[p] + m_prev[p] for p in pairs}
    m_new = {p: jnp.maximum(a[p], jnp.max(dm[p], axis=1, keepdims=True)) for p in pairs}
    inter = {p: jnp.exp(a[p] - m_new[p]) for p in pairs}
    s = {(b, h): _dot_nt(q[b][:, hs(h)], k[b][:, hs(h)]) * jnp.exp(dm[b, h] - m_new[b, h]) for b, h in pairs}
    v_ext = {(b, h): jnp.concatenate([v[b][:, hs(h)], ones_col], axis=1) for b, h in pairs}
    qc = {(b, h): _dot(q[b][:, hs(h)], cn_ref[b, h].astype(BF16)) for b, h in pairs}
    sv = {p: _dot(s[p].astype(BF16), v_ext[p].astype(BF16)) for p in pairs}
    m_last = {p: m_new[p][chunk - 1:chunk, :] for p in pairs}
    wl = {p: jnp.exp(bc_col[p][chunk - 1:chunk, :] - bc_col[p] + ig_col[p] - m_last[p]) for p in pairs}
    dl = {p: jnp.exp(a[p][chunk - 1:chunk, :] - m_last[p]) for p in pairs}
    upd = {(b, h): _dot_tn(k[b][:, hs(h)], (wl[b, h] * v_ext[b, h]).astype(BF16)) for b, h in pairs}
    rs = {p: _row_sum(s[p])[:, :1] for p in pairs}
    num = {p: inter[p] * qc[p][:, :HEAD_DIM] + sv[p][:, :HEAD_DIM] for p in pairs}
    den = {p: inter[p] * qc[p][:, HEAD_DIM:HEAD_DIM + 1] + rs[p] for p in pairs}
    hh = _head_norm_all({p: num[p] / jnp.maximum(jnp.abs(den[p]), jnp.exp(-m_new[p])) for p in pairs})
    for b, h in pairs:
        cn_ref[b, h] = dl[b, h] * cn_ref[b, h] + upd[b, h]
        ms_ref[b * N_HEADS + h] = jnp.broadcast_to(m_last[b, h], (1, LANES))
    for b in seqs:
        y = jnp.concatenate([hh[b, h] for h in range(N_HEADS)], axis=1)
        o_ref[b] = (y * ng_ref[...] * jax.nn.sigmoid(og_ref[b])).astype(o_ref.dtype)

    @pl.when(t == n_t - 1)
    def _():
        lane_m = lax.broadcasted_iota(jnp.int32, (1, LANES), 1)
        for b in seqs:
            m_out = jnp.zeros((1, LANES), F32)
            for h in range(N_HEADS):
                m_out = jnp.where(lane_m == h, ms_ref[b * N_HEADS + h], m_out)
            m_ref[b] = m_out


def mlstm_mixer(proj3, gate_bias, norm_g, cn0, m0, bb, out_dtype):
    n_batch, seq, _ = proj3.shape
    chunk = math.gcd(seq, CHUNK)
    n_t = seq // chunk
    tril = jnp.asarray(np.tril(np.ones((chunk, chunk), np.float32)), dtype=BF16)

    def row_spec(cblk, w=GROUP_W):
        return pl.BlockSpec((bb, chunk, w), lambda bi, t: (bi, t, cblk))

    def const_spec(shape):
        return pl.BlockSpec(shape, lambda bi, t: (0,) * len(shape))

    state_spec = pl.BlockSpec((bb, N_HEADS, HEAD_DIM, LANES), lambda bi, t: (bi, 0, 0, 0))
    kern = functools.partial(_mlstm_kernel, bb=bb, chunk=chunk, n_t=n_t)
    return pl.pallas_call(
        kern, grid=(n_batch // bb, n_t),
        in_specs=[pl.BlockSpec(memory_space=pltpu.SMEM),
                  row_spec(C_MQ), row_spec(C_MK), row_spec(C_MV), row_spec(C_MO), row_spec(C_GATES, LANES),
                  const_spec((1, LANES)), const_spec((1, GROUP_W)), const_spec((chunk, chunk)), state_spec],
        out_specs=[pl.BlockSpec((bb, chunk, GROUP_W), lambda bi, t: (bi, t, 0)), state_spec,
                   pl.BlockSpec((bb, 1, LANES), lambda bi, t: (bi, 0, 0))],
        out_shape=[jax.ShapeDtypeStruct((n_batch, seq, GROUP_W), out_dtype),
                   jax.ShapeDtypeStruct((n_batch, N_HEADS, HEAD_DIM, LANES), F32),
                   jax.ShapeDtypeStruct((n_batch, 1, LANES), F32)],
        scratch_shapes=[pltpu.VMEM((bb * N_HEADS, 1, LANES), F32)],
        compiler_params=_cp("arbitrary", "arbitrary"), name="mlstm")(
            m0, proj3, proj3, proj3, proj3, proj3, gate_bias, norm_g, tril, cn0)


def _split_dot(x, w):
    hi = x.astype(BF16)
    lo = (x - hi.astype(F32)).astype(BF16)
    return _dot(hi, w) + _dot(lo, w)


def _mlstm_pair_kernel(q_ref, k_ref, v_ref, og_ref, gt_ref, gb_ref, ng_ref, tril_ref, sel_ref, blk_ref,
                       o_ref, cn_ref, st_ref, *, bb, chunk):
    t_id = pl.program_id(1)

    @pl.when(t_id == 0)
    def _():
        cn_ref[...] = jnp.zeros_like(cn_ref)
        st_ref[...] = jnp.zeros_like(st_ref)

    seqs = range(bb)
    n_pair = N_HEADS // 2
    chains = [(b, p) for b in seqs for p in range(n_pair)]
    lane = lax.broadcasted_iota(jnp.int32, (chunk, LANES), 1)
    row = lax.broadcasted_iota(jnp.int32, (chunk, LANES), 0)
    low = lane < HEAD_DIM
    causal = (lane % HEAD_DIM) <= row
    eye2 = (lane % HEAD_DIM) == row
    blk = blk_ref[...]
    same = (lax.broadcasted_iota(jnp.int32, (LANES, LANES), 0) // HEAD_DIM
            == lax.broadcasted_iota(jnp.int32, (LANES, LANES), 1) // HEAD_DIM)
    ones_ll = jnp.ones((chunk, chunk), BF16)
    tril = tril_ref[...]
    pl_ = lambda p: slice(p * LANES, (p + 1) * LANES)
    zero_b = jnp.zeros((), BF16)

    gt = [gt_ref[b] + gb_ref[...] for b in seqs]
    bc = [_cumsum_rows(tril, _log_sigmoid(g)) for g in gt]
    def stacked(fn, xs):
        keys = list(xs)
        y = fn(jnp.concatenate([xs[c] for c in keys], axis=0))
        return {c: y[i * chunk:(i + 1) * chunk] for i, c in enumerate(keys)}

    by_blk = lambda x: _split_dot(x, blk)
    ig_all = stacked(lambda x: _split_dot(x, sel_ref[0]), dict(enumerate(gt)))
    bc_all = stacked(lambda x: _split_dot(x, sel_ref[1]), dict(enumerate(bc)))
    igc = {(b, p): ig_all[b][:, pl_(p)] for b, p in chains}
    bcc = {(b, p): bc_all[b][:, pl_(p)] for b, p in chains}
    gc = {c: igc[c] - bcc[c] for c in chains}
    grow = {c: _split_dot_left(ones_ll, jnp.where(eye2, gc[c], 0.0)) for c in chains}
    gmax = {c: jnp.max(gc[c], axis=0, keepdims=True) for c in chains}
    m_prev = {(b, p): st_ref[b, p, 1:2, :] for b, p in chains}
    a = {c: bcc[c] + m_prev[c] for c in chains}
    m_stab = {c: jnp.maximum(a[c], bcc[c] + gmax[c]) for c in chains}
    inter = {c: jnp.exp(a[c] - m_stab[c]) for c in chains}
    dw = {c: jnp.exp(jnp.where(causal, bcc[c] + grow[c], -jnp.inf) - m_stab[c]) for c in chains}
    q2 = {(b, p): q_ref[b, :, pl_(p)].astype(BF16) for b, p in chains}
    k2f = {(b, p): k_ref[b, :, pl_(p)] * QK_SCALE for b, p in chains}
    k2 = {c: k2f[c].astype(BF16) for c in chains}
    v2 = {(b, p): v_ref[b, :, pl_(p)] for b, p in chains}
    kbd = {c: jnp.concatenate([jnp.where(low, k2[c], zero_b), jnp.where(low, zero_b, k2[c])], axis=0) for c in chains}
    v2b = {c: v2[c].astype(BF16) for c in chains}
    vbd = {c: jnp.concatenate([jnp.where(low, v2b[c], zero_b), jnp.where(low, zero_b, v2b[c])], axis=0)
           for c in chains}
    s = {c: _dot_nt(q2[c], kbd[c]) * dw[c] for c in chains}
    rs = stacked(by_blk, s)
    sv = {c: _dot(s[c].astype(BF16), vbd[c]) for c in chains}
    qc = {(b, p): _dot(q2[b, p], cn_ref[b, p].astype(BF16)) for b, p in chains}
    n_row = {(b, p): st_ref[b, p, 0:1, :] for b, p in chains}
    qn = stacked(by_blk, {c: q2[c].astype(F32) * n_row[c] for c in chains})
    hh = {}
    for c in chains:
        num = inter[c] * qc[c] + sv[c]
        den = inter[c] * qn[c] + rs[c]
        hh[c] = num / jnp.maximum(jnp.abs(den), jnp.exp(-m_stab[c]))
    inv = 1.0 / HEAD_DIM
    mu = stacked(by_blk, hh)
    hc = {c: hh[c] - mu[c] * inv for c in chains}
    var = stacked(by_blk, {c: hc[c] * hc[c] for c in chains})
    m_last = {c: m_stab[c][chunk - 1:chunk, :] for c in chains}
    wl = {c: jnp.exp(bcc[c][chunk - 1:chunk, :] - bcc[c] + igc[c] - m_last[c]) for c in chains}
    dl = {c: jnp.exp(a[c][chunk - 1:chunk, :] - m_last[c]) for c in chains}
    upd = {c: _dot_tn(k2[c], (wl[c] * v2[c]).astype(BF16)) for c in chains}
    for b, p in chains:
        c = (b, p)
        y = hc[c] * lax.rsqrt(var[c] * inv + GN_EPS)
        o_ref[b, :, pl_(p)] = (y * ng_ref[:, pl_(p)] * jax.nn.sigmoid(og_ref[b, :, pl_(p)])).astype(o_ref.dtype)
        cn_ref[b, p] = dl[c] * cn_ref[b, p] + jnp.where(same, upd[c], 0.0)
        st_ref[b, p, 0:1, :] = dl[c] * n_row[c] + jnp.sum(wl[c] * k2f[c], axis=0, keepdims=True)
        st_ref[b, p, 1:2, :] = m_last[c]


def _split_dot_left(w, x):
    hi = x.astype(BF16)
    lo = (x - hi.astype(F32)).astype(BF16)
    return _dot(w, hi) + _dot(w, lo)


def mlstm_mixer_prompt(proj3, gate_bias, norm_g, bb, out_dtype):
    n_batch, seq, _ = proj3.shape
    chunk = CHUNK
    n_pair = N_HEADS // 2
    tril = jnp.asarray(np.tril(np.ones((chunk, chunk), np.float32)), dtype=BF16)
    sel = np.zeros((2, LANES, GROUP_W), np.float32)
    for h in range(N_HEADS):
        sel[0, h, h * HEAD_DIM:(h + 1) * HEAD_DIM] = 1.0
        sel[1, N_HEADS + h, h * HEAD_DIM:(h + 1) * HEAD_DIM] = 1.0
    idx = np.arange(LANES) // HEAD_DIM
    blk = (idx[:, None] == idx[None, :]).astype(np.float32)

    def row_spec(cblk, w=GROUP_W):
        return pl.BlockSpec((bb, chunk, w), lambda bi, t: (bi, t, cblk))

    def const_spec(shape):
        return pl.BlockSpec(shape, lambda bi, t: (0,) * len(shape))

    cn_spec = pl.BlockSpec((bb, n_pair, LANES, LANES), lambda bi, t: (bi, 0, 0, 0))
    st_spec = pl.BlockSpec((bb, n_pair, 8, LANES), lambda bi, t: (bi, 0, 0, 0))
    kern = functools.partial(_mlstm_pair_kernel, bb=bb, chunk=chunk)
    o, cn, st = pl.pallas_call(
        kern, grid=(n_batch // bb, seq // chunk),
        in_specs=[row_spec(C_MQ), row_spec(C_MK), row_spec(C_MV), row_spec(C_MO), row_spec(C_GATES, LANES),
                  const_spec((1, LANES)), const_spec((1, GROUP_W)), const_spec((chunk, chunk)),
                  const_spec(sel.shape), const_spec(blk.shape)],
        out_specs=[pl.BlockSpec((bb, chunk, GROUP_W), lambda bi, t: (bi, t, 0)), cn_spec, st_spec],
        out_shape=[jax.ShapeDtypeStruct((n_batch, seq, GROUP_W), out_dtype),
                   jax.ShapeDtypeStruct((n_batch, n_pair, LANES, LANES), F32),
                   jax.ShapeDtypeStruct((n_batch, n_pair, 8, LANES), F32)],
        compiler_params=_cp("arbitrary", "arbitrary"), name="mlstm_prompt")(
            proj3, proj3, proj3, proj3, proj3, gate_bias, norm_g, tril,
            jnp.asarray(sel, dtype=BF16), jnp.asarray(blk, dtype=BF16))
    c5 = cn.reshape(n_batch, n_pair, 2, HEAD_DIM, 2, HEAD_DIM)
    c_state = jnp.stack([c5[:, :, j, :, j, :] for j in range(2)], axis=2).reshape(n_batch, N_HEADS, HEAD_DIM, HEAD_DIM)
    n_state = st[:, :, 0, :].reshape(n_batch, N_HEADS, HEAD_DIM)
    m_state = st[:, :, 1, :].reshape(n_batch, N_HEADS, HEAD_DIM)[:, :, 0]
    return o, c_state, n_state, m_state


def _rope(x, cos, sin_signed):
    lane = lax.broadcasted_iota(jnp.int32, x.shape, 1)
    half = HEAD_DIM // 2
    swapped = jnp.where((lane % HEAD_DIM) < half, pltpu.roll(x, x.shape[1] - half, 1), pltpu.roll(x, half, 1))
    return x * cos + swapped * sin_signed


def _retention_kernel(q_ref, k_ref, v_ref, gg_ref, cos_ref, sin_ref, ng_ref, dec_ref, int_ref, wl_ref, dl_ref,
                      s0_ref, o_ref, s_ref, *, bb):
    t = pl.program_id(1)
    seqs = range(bb)
    pairs = [(b, h) for b in seqs for h in range(N_HEADS)]
    hs = lambda h: slice(h * HEAD_DIM, (h + 1) * HEAD_DIM)

    @pl.when(t == 0)
    def _():
        s_ref[...] = s0_ref[...]

    cos = cos_ref[...]
    sin = sin_ref[...]
    q = [_rope(q_ref[b], cos, sin).astype(BF16) for b in seqs]
    k = [(_rope(k_ref[b], cos, sin) * QK_SCALE).astype(BF16) for b in seqs]
    v = [v_ref[b] for b in seqs]
    s = {(b, h): _dot_nt(q[b][:, hs(h)], k[b][:, hs(h)]) * dec_ref[h] for b, h in pairs}
    qs = {(b, h): _dot(q[b][:, hs(h)], s_ref[b, h].astype(BF16)) for b, h in pairs}
    sv = {(b, h): _dot(s[b, h].astype(BF16), v[b][:, hs(h)].astype(BF16)) for b, h in pairs}
    upd = {(b, h): _dot_tn(k[b][:, hs(h)], (wl_ref[h] * v[b][:, hs(h)]).astype(BF16)) for b, h in pairs}
    o = _head_norm_all({(b, h): int_ref[h] * qs[b, h] + sv[b, h] for b, h in pairs})
    for b, h in pairs:
        s_ref[b, h] = dl_ref[h] * s_ref[b, h] + upd[b, h]
    for b in seqs:
        gg = gg_ref[b]
        y = jnp.concatenate([o[b, h] for h in range(N_HEADS)], axis=1) * ng_ref[...] * (gg * jax.nn.sigmoid(gg))
        o_ref[b] = y.astype(o_ref.dtype)


def _retention_consts(chunk):
    log_g = np.log(1.0 - np.exp2(-5.0 - np.arange(N_HEADS, dtype=np.float64)))
    tau = np.arange(chunk, dtype=np.float64)
    rel = tau[:, None] - tau[None, :]
    decay = np.where(rel >= 0, np.exp(log_g[:, None, None] * np.maximum(rel, 0.0)), 0.0)
    inter = np.exp(log_g[:, None] * (tau + 1.0))[..., None]
    wl = np.exp(log_g[:, None] * (chunk - 1.0 - tau))[..., None]
    dl = np.exp(log_g * chunk)[:, None, None]
    return tuple(jnp.asarray(a, F32) for a in (decay, inter, wl, dl))


def _retention_pair_kernel(q_ref, k_ref, v_ref, gg_ref, cos_ref, sin_ref, ng_ref, dec_ref, int_ref, wl_ref, dl_ref,
                           swap_ref, blk_ref, o_ref, s_ref, *, bb, chunk):
    t_id = pl.program_id(1)

    @pl.when(t_id == 0)
    def _():
        s_ref[...] = jnp.zeros_like(s_ref)

    seqs = range(bb)
    n_pair = N_HEADS // 2
    chains = [(b, p) for b in seqs for p in range(n_pair)]
    low = lax.broadcasted_iota(jnp.int32, (chunk, LANES), 1) < HEAD_DIM
    same = (lax.broadcasted_iota(jnp.int32, (LANES, LANES), 0) // HEAD_DIM
            == lax.broadcasted_iota(jnp.int32, (LANES, LANES), 1) // HEAD_DIM)
    pl_ = lambda p: slice(p * LANES, (p + 1) * LANES)
    zero_b = jnp.zeros((), BF16)
    blk = blk_ref[...]
    cos = cos_ref[...]
    sin = sin_ref[...]

    def stacked(fn, xs):
        keys = list(xs)
        y = fn(jnp.concatenate([xs[c] for c in keys], axis=0))
        return {c: y[i * chunk:(i + 1) * chunk] for i, c in enumerate(keys)}

    swap = lambda x: _split_dot(x, swap_ref[...])
    qf = {b: q_ref[b] for b in seqs}
    kf = {b: k_ref[b] for b in seqs}
    q_sw = stacked(swap, qf)
    k_sw = stacked(swap, kf)
    q = {b: (qf[b] * cos + q_sw[b] * sin).astype(BF16) for b in seqs}
    k = {b: ((kf[b] * cos + k_sw[b] * sin) * QK_SCALE).astype(BF16) for b in seqs}
    k2 = {(b, p): k[b][:, pl_(p)] for b, p in chains}
    v2 = {(b, p): v_ref[b, :, pl_(p)] for b, p in chains}
    v2b = {c: v2[c].astype(BF16) for c in chains}
    kbd = {c: jnp.concatenate([jnp.where(low, k2[c], zero_b), jnp.where(low, zero_b, k2[c])], axis=0) for c in chains}
    vbd = {c: jnp.concatenate([jnp.where(low, v2b[c], zero_b), jnp.where(low, zero_b, v2b[c])], axis=0)
           for c in chains}
    s = {(b, p): _dot_nt(q[b][:, pl_(p)], kbd[b, p]) * dec_ref[p] for b, p in chains}
    qs = {(b, p): _dot(q[b][:, pl_(p)], s_ref[b, p].astype(BF16)) for b, p in chains}
    sv = {c: _dot(s[c].astype(BF16), vbd[c]) for c in chains}
    upd = {(b, p): _dot_tn(k2[b, p], (wl_ref[p] * v2[b, p]).astype(BF16)) for b, p in chains}
    o = {(b, p): int_ref[p] * qs[b, p] + sv[b, p] for b, p in chains}
    inv = 1.0 / HEAD_DIM
    by_blk = lambda x: _split_dot(x, blk)
    mu = stacked(by_blk, o)
    oc = {c: o[c] - mu[c] * inv for c in chains}
    var = stacked(by_blk, {c: oc[c] * oc[c] for c in chains})
    for b, p in chains:
        c = (b, p)
        gg = gg_ref[b, :, pl_(p)]
        y = oc[c] * lax.rsqrt(var[c] * inv + GN_EPS) * ng_ref[:, pl_(p)] * (gg * jax.nn.sigmoid(gg))
        o_ref[b, :, pl_(p)] = y.astype(o_ref.dtype)
        s_ref[b, p] = dl_ref[p] * s_ref[b, p] + jnp.where(same, upd[c], 0.0)


def retention_mixer_prompt(proj3, cos, sin_signed, norm_g, bb, out_dtype):
    n_batch, seq, _ = proj3.shape
    chunk = CHUNK
    n_pair = N_HEADS // 2
    dec, inter, wl, dl = _retention_consts(chunk)
    rep = lambda a: jnp.broadcast_to(a, a.shape[:-1] + (HEAD_DIM,))
    pair = lambda a: jnp.concatenate([a[0::2], a[1::2]], axis=-1)
    dec2, int2, wl2, dl2 = pair(dec), pair(rep(inter)), pair(rep(wl)), pair(rep(dl))
    src = np.arange(GROUP_W)
    partner = (src // HEAD_DIM) * HEAD_DIM + (src % HEAD_DIM + HEAD_DIM // 2) % HEAD_DIM
    swap = np.zeros((GROUP_W, GROUP_W), np.float32)
    swap[partner, src] = 1.0
    idx = np.arange(LANES) // HEAD_DIM
    blk = (idx[:, None] == idx[None, :]).astype(np.float32)

    def row_spec(cblk):
        return pl.BlockSpec((bb, chunk, GROUP_W), lambda bi, t: (bi, t, cblk))

    def const_spec(shape):
        return pl.BlockSpec(shape, lambda bi, t: (0,) * len(shape))

    pos_spec = pl.BlockSpec((chunk, GROUP_W), lambda bi, t: (t, 0))
    state_spec = pl.BlockSpec((bb, n_pair, LANES, LANES), lambda bi, t: (bi, 0, 0, 0))
    kern = functools.partial(_retention_pair_kernel, bb=bb, chunk=chunk)
    o, sbd = pl.pallas_call(
        kern, grid=(n_batch // bb, seq // chunk),
        in_specs=[row_spec(C_RQ), row_spec(C_RK), row_spec(C_RV), row_spec(C_RG), pos_spec, pos_spec,
                  const_spec((1, GROUP_W)), const_spec(dec2.shape), const_spec(int2.shape), const_spec(wl2.shape),
                  const_spec(dl2.shape), const_spec(swap.shape), const_spec(blk.shape)],
        out_specs=[pl.BlockSpec((bb, chunk, GROUP_W), lambda bi, t: (bi, t, 0)), state_spec],
        out_shape=[jax.ShapeDtypeStruct((n_batch, seq, GROUP_W), out_dtype),
                   jax.ShapeDtypeStruct((n_batch, n_pair, LANES, LANES), F32)],
        compiler_params=_cp("arbitrary", "arbitrary"), name="retention_prompt")(
            proj3, proj3, proj3, proj3, cos, sin_signed, norm_g, dec2, int2, wl2, dl2,
            jnp.asarray(swap, dtype=BF16), jnp.asarray(blk, dtype=BF16))
    s5d = sbd.reshape(n_batch, n_pair, 2, HEAD_DIM, 2, HEAD_DIM)
    state = jnp.stack([s5d[:, :, j, :, j, :] for j in range(2)], axis=2).reshape(n_batch, N_HEADS, HEAD_DIM, HEAD_DIM)
    return o, state


def retention_mixer(proj3, cos, sin_signed, norm_g, s0, bb, out_dtype):
    n_batch, seq, _ = proj3.shape
    chunk = math.gcd(seq, CHUNK)
    dec, inter, wl, dl = _retention_consts(chunk)

    def row_spec(cblk):
        return pl.BlockSpec((bb, chunk, GROUP_W), lambda bi, t: (bi, t, cblk))

    def const_spec(shape):
        return pl.BlockSpec(shape, lambda bi, t: (0,) * len(shape))

    pos_spec = pl.BlockSpec((chunk, GROUP_W), lambda bi, t: (t, 0))
    state_spec = pl.BlockSpec((bb, N_HEADS, HEAD_DIM, HEAD_DIM), lambda bi, t: (bi, 0, 0, 0))
    kern = functools.partial(_retention_kernel, bb=bb)
    return pl.pallas_call(
        kern, grid=(n_batch // bb, seq // chunk),
        in_specs=[row_spec(C_RQ), row_spec(C_RK), row_spec(C_RV), row_spec(C_RG), pos_spec, pos_spec,
                  const_spec((1, GROUP_W)), const_spec(dec.shape), const_spec(inter.shape), const_spec(wl.shape),
                  const_spec(dl.shape), state_spec],
        out_specs=[pl.BlockSpec((bb, chunk, GROUP_W), lambda bi, t: (bi, t, 0)), state_spec],
        out_shape=[jax.ShapeDtypeStruct((n_batch, seq, GROUP_W), out_dtype),
                   jax.ShapeDtypeStruct((n_batch, N_HEADS, HEAD_DIM, HEAD_DIM), F32)],
        compiler_params=_cp("arbitrary", "arbitrary"), name="retention")(
            proj3, proj3, proj3, proj3, cos, sin_signed, norm_g, dec, inter, wl, dl, s0)


def _s5_kernel(u_ref, wb_ref, a1_ref, a2_ref, h0_ref, wc_ref, d_ref, gw_ref, gb_ref, o_ref, hl_ref,
               hs_ref, ut_ref, yt_ref, *, nb, tt):
    c = pl.program_id(0)

    @pl.when(c == 0)
    def _():
        hl_ref[...] = h0_ref[...]

    halves = range(GROUP_W // LANES)
    for b in range(nb):
        ub = u_ref[b]
        for hf in halves:
            ut_ref[hf, pl.ds(b, tt, stride=nb), :] = ub[:, hf * LANES:(hf + 1) * LANES]
    u = jnp.concatenate([ut_ref[hf] for hf in halves], axis=1)
    hs_ref[...] = _dot(u.astype(BF16), wb_ref[...])
    a1 = jnp.broadcast_to(a1_ref[...], (nb, 2 * S5_W))
    a2 = jnp.broadcast_to(a2_ref[...], (nb, 2 * S5_W))

    def step(t, h):
        r0 = pl.multiple_of(t * nb, nb)
        swapped = jnp.concatenate([h[:, S5_W:], h[:, :S5_W]], axis=1)
        h = a1 * h + a2 * swapped + hs_ref[pl.ds(r0, nb), :]
        hs_ref[pl.ds(r0, nb), :] = h
        return h

    hl_ref[...] = lax.fori_loop(0, tt, step, hl_ref[...])
    y = _dot(hs_ref[...].astype(BF16), wc_ref[...]) + d_ref[...] * u
    g5 = jax.nn.gelu(y)
    yt = g5 * jax.nn.sigmoid(_dot(g5.astype(BF16), gw_ref[...]) + gb_ref[...])
    for hf in halves:
        yt_ref[hf] = yt[:, hf * LANES:(hf + 1) * LANES]
    for b in range(nb):
        o_ref[b] = jnp.concatenate([yt_ref[hf, pl.ds(b, tt, stride=nb), :] for hf in halves],
                                   axis=1).astype(o_ref.dtype)


def s5_mixer(proj3, tt, wb, a1, a2, h0, wc, d, glu_w, glu_b, out_dtype):
    nb, seq, _ = proj3.shape
    rows = tt * nb

    def const_spec(shape):
        return pl.BlockSpec(shape, lambda c: (0,) * len(shape))

    kern = functools.partial(_s5_kernel, nb=nb, tt=tt)
    return pl.pallas_call(
        kern, grid=(seq // tt,),
        in_specs=[pl.BlockSpec((nb, tt, GROUP_W), lambda c: (0, c, C_SU)),
                  const_spec(wb.shape), const_spec(a1.shape), const_spec(a2.shape), const_spec(h0.shape),
                  const_spec(wc.shape), const_spec(d.shape), const_spec(glu_w.shape), const_spec(glu_b.shape)],
        out_specs=[pl.BlockSpec((nb, tt, GROUP_W), lambda c: (0, c, 0)), const_spec(h0.shape)],
        out_shape=[jax.ShapeDtypeStruct((nb, seq, GROUP_W), out_dtype), jax.ShapeDtypeStruct(h0.shape, F32)],
        scratch_shapes=[pltpu.VMEM((rows, 2 * S5_W), F32), pltpu.VMEM((GROUP_W // LANES, rows, LANES), F32),
                        pltpu.VMEM((GROUP_W // LANES, rows, LANES), F32)],
        compiler_params=_cp("arbitrary"), name="s5")(proj3, wb, a1, a2, h0, wc, d, glu_w, glu_b)


def _s5_weights(a_re, a_im, log_dt, b_re, b_im, c_re, c_im):
    lam = lax.complex(a_re, a_im)
    a_bar = jnp.exp(lam * jnp.exp(log_dt))
    b_bar = ((a_bar - 1.0) / lam)[..., None] * lax.complex(b_re, b_im)
    eye = jnp.eye(S5_GROUPS, dtype=F32)

    def in_map(m):
        return jnp.einsum('gpc,gh->gchp', m, eye).reshape(S5_GROUPS * S5_GROUP, S5_W)

    def out_map(m):
        return jnp.einsum('gcp,gh->gphc', m, eye).reshape(S5_W, S5_GROUPS * S5_GROUP)

    wb = jnp.concatenate([in_map(b_bar.real), in_map(b_bar.imag)], axis=1).astype(BF16)
    wc = jnp.concatenate([out_map(c_re), -out_map(c_im)], axis=0).astype(BF16)
    ar = a_bar.real.reshape(1, S5_W)
    ai = a_bar.imag.reshape(1, S5_W)
    return wb, jnp.concatenate([ar, ar], axis=1), jnp.concatenate([-ai, ai], axis=1), wc


def _cross_attn_kernel(x_ref, wq_ref, k_ref, v_ref, wo_ref, g_ref, b_ref, o_ref):
    x = x_ref[...]
    q = (_dot(x.astype(BF16), wq_ref[...]) * QK_SCALE).astype(BF16)
    k = k_ref[0].astype(BF16)
    v = v_ref[0].astype(BF16)
    hs = [slice(h * HEAD_DIM, (h + 1) * HEAD_DIM) for h in range(N_HEADS)]
    s = [_dot_nt(q[:, sl], k[:, sl]) for sl in hs]
    e = [jnp.exp(a - jnp.max(a, axis=1, keepdims=True)) for a in s]
    p = [a / jnp.sum(a, axis=1, keepdims=True) for a in e]
    o = jnp.concatenate([_dot(p[h].astype(BF16), v[:, hs[h]]) for h in range(N_HEADS)], axis=1)
    y = ALPHA * x + _dot(o.astype(BF16), wo_ref[...])
    o_ref[...] = _layer_norm(y, g_ref[...], b_ref[...])


def cross_attn_ln(x, row_blk0, n_batch, seq, tq, mem_k, mem_v, wq, wo, g, b):
    nq = seq // tq

    def const_spec(shape):
        return pl.BlockSpec(shape, lambda bb, i: (0,) * len(shape))

    row_spec = pl.BlockSpec((tq, D_MODEL), lambda bb, i: (row_blk0 + bb * nq + i, 0))
    mem_spec = pl.BlockSpec((1, N_MEM, GROUP_W), lambda bb, i: (bb, 0, 0))
    return pl.pallas_call(
        _cross_attn_kernel, grid=(n_batch, nq),
        in_specs=[row_spec, const_spec(wq.shape), mem_spec, mem_spec, const_spec(wo.shape),
                  const_spec(g.shape), const_spec(b.shape)],
        out_specs=row_spec, out_shape=jax.ShapeDtypeStruct(x.shape, F32), input_output_aliases={0: 0},
        compiler_params=_cp("arbitrary", "arbitrary"), name="cross_attn")(x, wq, mem_k, mem_v, wo, g, b)


SWIGLU_ROWS = 256


def _swiglu_accumulate(xb_ref, wg, wu, wd, acc_ref, n_valid=None):
    wgb, wub, wdb = wg.astype(BF16), wu.astype(BF16), wd.astype(BF16)
    n_sub = xb_ref.shape[0] // SWIGLU_ROWS

    def hidden(r):
        xb = xb_ref[pl.ds(r * SWIGLU_ROWS, SWIGLU_ROWS), :]
        gate = _dot(xb, wgb)
        up = _dot(xb, wub)
        return (gate * jax.nn.sigmoid(gate) * up).astype(BF16)

    def first_sub_blocks(n):
        hid = hidden(0)
        for r in range(n):
            nxt = hidden(r + 1) if r + 1 < n else None
            acc_ref[pl.ds(r * SWIGLU_ROWS, SWIGLU_ROWS), :] += _dot(hid, wdb)
            hid = nxt

    if n_valid is None:
        first_sub_blocks(n_sub)
        return
    need = (n_valid + SWIGLU_ROWS - 1) // SWIGLU_ROWS
    for n in range(1, n_sub + 1):
        pl.when(need == n)(functools.partial(first_sub_blocks, n))


def _ffn_kernel(x_ref, wg_ref, wu_ref, wd_ref, g_ref, b_ref, o_ref, xb_ref, *, nf):
    j = pl.program_id(1)

    @pl.when(j == 0)
    def _():
        xb_ref[...] = x_ref[...].astype(BF16)
        o_ref[...] = jnp.zeros_like(o_ref)

    _swiglu_accumulate(xb_ref, wg_ref[...], wu_ref[...], wd_ref[...], o_ref)

    @pl.when(j == nf - 1)
    def _():
        o_ref[...] = _layer_norm(ALPHA * x_ref[...] + o_ref[...], g_ref[...], b_ref[...])


def ffn_ln(x, wg, wu, wd, g, b, tm, tf):
    m = x.shape[0]
    nf = D_FF // tf
    kern = functools.partial(_ffn_kernel, nf=nf)
    return pl.pallas_call(
        kern, grid=(m // tm, nf),
        in_specs=[pl.BlockSpec((tm, D_MODEL), lambda i, j: (i, 0)),
                  pl.BlockSpec((D_MODEL, tf), lambda i, j: (0, j)), pl.BlockSpec((D_MODEL, tf), lambda i, j: (0, j)),
                  pl.BlockSpec((tf, D_MODEL), lambda i, j: (j, 0)),
                  pl.BlockSpec((1, D_MODEL), lambda i, j: (0, 0)), pl.BlockSpec((1, D_MODEL), lambda i, j: (0, 0))],
        out_specs=pl.BlockSpec((tm, D_MODEL), lambda i, j: (i, 0)),
        out_shape=jax.ShapeDtypeStruct((m, D_MODEL), F32),
        scratch_shapes=[pltpu.VMEM((tm, D_MODEL), BF16)],
        compiler_params=_cp("arbitrary", "arbitrary"), name="ffn")(x, wg, wu, wd, g, b)


def _router_kernel(x_ref, w_ref, b_ref, o_ref):
    logits = jnp.dot(x_ref[...], w_ref[...], preferred_element_type=F32, precision=lax.Precision.HIGHEST) + b_ref[...]
    lane = lax.broadcasted_iota(jnp.int32, logits.shape, 1)
    neg = jnp.float32(-jnp.inf)
    lg = jnp.where(lane < N_EXPERTS, logits, neg)
    m1 = jnp.max(lg, axis=1, keepdims=True)
    i1 = jnp.min(jnp.where(lg == m1, lane, LANES), axis=1, keepdims=True)
    lg2 = jnp.where(lane == i1, neg, lg)
    m2 = jnp.max(lg2, axis=1, keepdims=True)
    i2 = jnp.min(jnp.where(lg2 == m2, lane, LANES), axis=1, keepdims=True)
    e2 = jnp.exp(m2 - m1)
    g1 = 1.0 / (1.0 + e2)
    g2 = e2 / (1.0 + e2)
    out = jnp.where(lane == 0, i1.astype(F32), jnp.where(lane == 1, i2.astype(F32),
                    jnp.where(lane == 2, g1, jnp.where(lane == 3, g2, 0.0))))
    o_ref[...] = out


def router(x, w_pad, b_pad, tm):
    m = x.shape[0]
    return pl.pallas_call(
        _router_kernel, grid=(m // tm,),
        in_specs=[pl.BlockSpec((tm, D_MODEL), lambda i: (i, 0)), pl.BlockSpec((D_MODEL, LANES), lambda i: (0, 0)),
                  pl.BlockSpec((1, LANES), lambda i: (0, 0))],
        out_specs=pl.BlockSpec((tm, LANES), lambda i: (i, 0)),
        out_shape=jax.ShapeDtypeStruct((m, LANES), F32),
        compiler_params=_cp("arbitrary"), name="router")(x, w_pad, b_pad)


def _moe_ffn_kernel(te_ref, nu_ref, tr_ref, x_ref, wg_ref, wu_ref, wd_ref, o_ref, xb_ref):
    i = pl.program_id(0)
    j = pl.program_id(1)
    used = i < nu_ref[0]

    @pl.when(used)
    def _():
        @pl.when(j == 0)
        def _():
            xb_ref[...] = x_ref[...].astype(BF16)
            o_ref[...] = jnp.zeros_like(o_ref)

        _swiglu_accumulate(xb_ref, wg_ref[0], wu_ref[0], wd_ref[0], o_ref, tr_ref[i])

    @pl.when(jnp.logical_and(jnp.logical_not(used), j == 0))
    def _():
        o_ref[...] = jnp.zeros_like(o_ref)


def moe_ffn(x_sorted, tile_expert, n_used, tile_rows, wg, wu, wd, tf):
    n_rows = x_sorted.shape[0]
    n_tiles = n_rows // MOE_TILE
    nf = D_FF // tf

    def jj(i, j, nu):
        return jnp.where(i < nu[0], j, nf - 1)

    gs = pltpu.PrefetchScalarGridSpec(
        num_scalar_prefetch=3, grid=(n_tiles, nf),
        in_specs=[pl.BlockSpec((MOE_TILE, D_MODEL), lambda i, j, te, nu, tr: (i, 0)),
                  pl.BlockSpec((1, D_MODEL, tf), lambda i, j, te, nu, tr: (te[i], 0, jj(i, j, nu))),
                  pl.BlockSpec((1, D_MODEL, tf), lambda i, j, te, nu, tr: (te[i], 0, jj(i, j, nu))),
                  pl.BlockSpec((1, tf, D_MODEL), lambda i, j, te, nu, tr: (te[i], jj(i, j, nu), 0))],
        out_specs=pl.BlockSpec((MOE_TILE, D_MODEL), lambda i, j, te, nu, tr: (i, 0)),
        scratch_shapes=[pltpu.VMEM((MOE_TILE, D_MODEL), BF16)])
    return pl.pallas_call(
        _moe_ffn_kernel, grid_spec=gs, out_shape=jax.ShapeDtypeStruct((n_rows, D_MODEL), F32),
        compiler_params=_cp("arbitrary", "arbitrary"), name="moe_ffn")(
            tile_expert, n_used, tile_rows, x_sorted, wg, wu, wd)


def _combine_ln_kernel(x_ref, r_ref, ya_ref, yb_ref, g_ref, b_ref, o_ref):
    r = r_ref[...]
    y = r[:, TOP_K:TOP_K + 1] * ya_ref[...] + r[:, TOP_K + 1:TOP_K + 2] * yb_ref[...]
    o_ref[...] = _layer_norm(ALPHA * x_ref[...] + y, g_ref[...], b_ref[...])


def combine_ln(x, r, ya, yb, g, b, tm, row_blk0, rows):
    row = pl.BlockSpec((tm, D_MODEL), lambda i: (row_blk0 + i, 0))
    vec = pl.BlockSpec((1, D_MODEL), lambda i: (0, 0))
    return pl.pallas_call(
        _combine_ln_kernel, grid=(rows // tm,),
        in_specs=[row, pl.BlockSpec((tm, LANES), lambda i: (row_blk0 + i, 0)), row, row, vec, vec],
        out_specs=pl.BlockSpec((tm, D_MODEL), lambda i: (i, 0)),
        out_shape=jax.ShapeDtypeStruct((rows, D_MODEL), F32),
        compiler_params=_cp("arbitrary"), name="combine_ln")(x, r, ya, yb, g, b)


def moe_ln(x, router_w, router_b, wg, wu, wd, g, b, tm, splits):
    m = x.shape[0]
    w_pad = jnp.zeros((D_MODEL, LANES), F32).at[:, :N_EXPERTS].set(router_w)
    b_pad = jnp.zeros((1, LANES), F32).at[0, :N_EXPERTS].set(router_b)
    r = router(x, w_pad, b_pad, tm)
    top_idx = r[:, :TOP_K].astype(jnp.int32)
    n_slot = m * TOP_K
    onehot = (top_idx[:, :, None] == jnp.arange(N_EXPERTS, dtype=jnp.int32)).astype(jnp.int32)
    per_tok = onehot[:, 0] + onehot[:, 1]
    before = jnp.cumsum(per_tok, axis=0) - per_tok
    counts = jnp.sum(per_tok, axis=0)
    tiles_per = (counts + MOE_TILE - 1) // MOE_TILE
    tile_end = jnp.cumsum(tiles_per)
    row0 = (tile_end - tiles_per) * MOE_TILE
    dest = jnp.sum(onehot * (before + row0)[:, None, :], axis=2)
    n_tiles = -(-n_slot // MOE_TILE) + N_EXPERTS
    n_rows = n_tiles * MOE_TILE
    row_tok = (jnp.arange(n_rows, dtype=jnp.int32) % m).at[dest.reshape(-1)].set(
        jnp.arange(n_slot, dtype=jnp.int32) // TOP_K, unique_indices=True, mode='promise_in_bounds')
    n_used = tile_end[-1:].astype(jnp.int32)
    tile_ids = jnp.minimum(jnp.arange(n_tiles, dtype=jnp.int32), n_used[0] - 1)
    tile_expert = jnp.minimum(jnp.sum((tile_end[None, :] <= tile_ids[:, None]).astype(jnp.int32), axis=1),
                              N_EXPERTS - 1)
    tile_rows = jnp.clip(counts[tile_expert] - (tile_ids - (tile_end - tiles_per)[tile_expert]) * MOE_TILE,
                         0, MOE_TILE).astype(jnp.int32)
    x_sorted = x.at[row_tok].get(mode='promise_in_bounds')
    y_sorted = moe_ffn(x_sorted, tile_expert, n_used, tile_rows, wg, wu, wd, tf=512)
    ya = y_sorted.at[dest[:, 0]].get(mode='promise_in_bounds')
    yb = y_sorted.at[dest[:, 1]].get(mode='promise_in_bounds')
    return [combine_ln(x, r, ya, yb, g, b, t, blk0, rows) for blk0, rows, t in splits]


def kernel(x_prompt, x_sample, cache_sb_k, cache_sb_v, cache_mem_k, cache_mem_v, state_ml_C, state_ml_n, state_ml_m, state_rt_S, state_s5_re, state_s5_im, page_table, mem_prompt, w_in, sb_bias, ml_b_i, ml_b_f, ml_norm_g, rt_norm_g, s5_A_re, s5_A_im, s5_log_dt, s5_B_re, s5_B_im, s5_C_re, s5_C_im, s5_D, s5_glu_w, s5_glu_b, w_out, ca_wq, ca_wk, ca_wv, ca_wo, ln_g, ln_b, ffn_w_gate, ffn_w_up, ffn_w_down, moe_router_w, moe_router_b, moe_w_gate, moe_w_up, moe_w_down):
    bp, tp, _ = x_prompt.shape
    bs, ts, _ = x_sample.shape
    n_p, n_s = bp * tp, bs * ts
    tm = 640
    assert (n_p + n_s) % (2 * tm) == 0 and n_p % n_s == 0 and tp % 512 == 0 and bp % 8 == 0 and bs % 8 == 0
    x = jnp.concatenate([x_prompt.reshape(n_p, D_MODEL), x_sample.reshape(n_s, D_MODEL)], axis=0)
    uu = _suffix_matrix()
    g_off = 7 * GROUP_W
    half = HEAD_DIM // 2
    freq = ROPE_BASE ** (-jnp.arange(half, dtype=F32) / half)

    def rope_tables(pos):
        ang = pos.astype(F32)[:, None] * freq[None, :]
        cos, sin = jnp.cos(ang), jnp.sin(ang)
        return (jnp.tile(jnp.concatenate([cos, cos], axis=1), (1, N_HEADS)),
                jnp.tile(jnp.concatenate([-sin, sin], axis=1), (1, N_HEADS)))

    cos_p, sin_p = rope_tables(jnp.arange(tp, dtype=jnp.int32))
    cos_s, sin_s = rope_tables(PAST_LEN + jnp.arange(ts, dtype=jnp.int32))
    cache_kt = cache_sb_k.transpose(0, 1, 3, 4, 2)
    cache_vt = cache_sb_v.transpose(0, 1, 3, 4, 2)

    p_st = [[] for _ in range(10)]
    s_st = [[] for _ in range(8)]
    for l in range(DEPTH):
        wl = w_in[l]
        w_cat = jnp.concatenate([wl[:, :g_off], wl[:, g_off + 2 * N_HEADS:], wl[:, g_off:g_off + 2 * N_HEADS],
                                 jnp.zeros((D_MODEL, PROJ_W - wl.shape[1]), F32)], axis=1).astype(BF16)
        proj_p, kt_p, vt_p = in_proj_prompt(x, w_cat, bp, tp, 512)
        proj_s = linear(x, w_cat, n_s, n_p // n_s, n_s)
        proj_p3 = proj_p.reshape(bp, tp, PROJ_W)
        proj_s3 = proj_s.reshape(bs, ts, PROJ_W)
        gate_bias = jnp.zeros((1, LANES), F32).at[0, :2 * N_HEADS].set(jnp.concatenate([ml_b_i[l], ml_b_f[l]]))
        ml_g = ml_norm_g[l][None, :]
        rt_g = rt_norm_g[l][None, :]
        wb, a1, a2, wc = _s5_weights(s5_A_re[l], s5_A_im[l], s5_log_dt[l], s5_B_re[l], s5_B_im[l],
                                     s5_C_re[l], s5_C_im[l])
        s5_d = s5_D[l][None, :]
        glu_w = s5_glu_w[l].astype(BF16)
        glu_b = s5_glu_b[l][None, :]

        o_sb_p = sb_attention_prompt(proj_p, sb_bias[l], uu, bp, tp, tq=256)
        o_ml_p, ml_c_p, ml_n_p, ml_m_p = mlstm_mixer_prompt(proj_p3, gate_bias, ml_g, 8, MIX_DTYPE)
        o_rt_p, rs_p = retention_mixer_prompt(proj_p3, cos_p, sin_p, rt_g, 8, MIX_DTYPE)
        o_ml_p = o_ml_p.reshape(n_p, GROUP_W)
        o_rt_p = o_rt_p.reshape(n_p, GROUP_W)
        o_s5_p, h5_p = s5_mixer(proj_p3, 64, wb, a1, a2, jnp.zeros((bp, 2 * S5_W), F32), wc, s5_d, glu_w, glu_b,
                                MIX_DTYPE)
        o_s5_p = o_s5_p.reshape(n_p, GROUP_W)

        o_sb_s = sb_attention_sample(proj_s, 0, cache_kt, cache_vt, page_table, l, sb_bias[l], uu, n_pp=16)
        cn0 = jnp.concatenate([state_ml_C[:, l], state_ml_n[:, l][..., None],
                               jnp.zeros((bs, N_HEADS, HEAD_DIM, LANES - HEAD_DIM - 1), F32)], axis=-1)
        o_ml_s, cn_s, m_s = mlstm_mixer(proj_s3, gate_bias, ml_g, cn0, state_ml_m[:, l], 8, F32)
        o_rt_s, rs_s = retention_mixer(proj_s3, cos_s, sin_s, rt_g, state_rt_S[:, l], 8, F32)
        o_ml_s = o_ml_s.reshape(n_s, GROUP_W)
        o_rt_s = o_rt_s.reshape(n_s, GROUP_W)
        h0_s = jnp.concatenate([state_s5_re[:, l].reshape(bs, S5_W), state_s5_im[:, l].reshape(bs, S5_W)], axis=1)
        o_s5_s, h5_s = s5_mixer(proj_s3, ts, wb, a1, a2, h0_s, wc, s5_d, glu_w, glu_b, F32)
        o_s5_s = o_s5_s.reshape(n_s, GROUP_W)

        wo_mix = w_out[l].astype(BF16)
        g0, b0 = ln_g[l, 0][None, :], ln_b[l, 0][None, :]
        x = mix_out_ln(x, (o_sb_p, o_ml_p, o_rt_p, o_s5_p), wo_mix, g0, b0, 512, 0)
        x = mix_out_ln(x, (o_sb_s, o_ml_s, o_rt_s, o_s5_s), wo_mix, g0, b0, n_s, n_p // n_s)

        mem_kv = linear(mem_prompt.reshape(bp * N_MEM, D_MODEL),
                        jnp.concatenate([ca_wk[l], ca_wv[l]], axis=1).astype(BF16), 512)
        mk_p = mem_kv[:, :GROUP_W].reshape(bp, N_MEM, GROUP_W)
        mv_p = mem_kv[:, GROUP_W:].reshape(bp, N_MEM, GROUP_W)
        wq = ca_wq[l].astype(BF16)
        wo = ca_wo[l].astype(BF16)
        g1, b1 = ln_g[l, 1][None, :], ln_b[l, 1][None, :]
        x = cross_attn_ln(x, 0, bp, tp, 512, mk_p, mv_p, wq, wo, g1, b1)
        x = cross_attn_ln(x, n_p // ts, bs, ts, ts, cache_mem_k[:, l].reshape(bs, N_MEM, GROUP_W),
                          cache_mem_v[:, l].reshape(bs, N_MEM, GROUP_W), wq, wo, g1, b1)

        g2, b2 = ln_g[l, 2][None, :], ln_b[l, 2][None, :]
        j = l // 2
        last = l == DEPTH - 1
        if l % 2 == 0:
            x = ffn_ln(x, ffn_w_gate[j], ffn_w_up[j], ffn_w_down[j], g2, b2, 2 * tm, tf=512)
            y_out = (x[:n_p], x[n_p:]) if last else None
        else:
            splits = [(0, n_p, 512), (n_p // n_s, n_s, n_s)] if last else [(0, n_p + n_s, tm)]
            y_out = moe_ln(x, moe_router_w[j], moe_router_b[j], moe_w_gate[j], moe_w_up[j], moe_w_down[j],
                           g2, b2, tm, splits)
            x = None if last else y_out[0]

        def heads(a, nb_, t_):
            return a.reshape(nb_, t_, N_HEADS, HEAD_DIM)

        p_st[0].append(kt_p)
        p_st[1].append(vt_p)
        p_st[2].append(heads(mk_p, bp, N_MEM))
        p_st[3].append(heads(mv_p, bp, N_MEM))
        s_st[0].append(heads(proj_s[:, C_SK * GROUP_W:(C_SK + 1) * GROUP_W], bs, ts))
        s_st[1].append(heads(proj_s[:, C_SV * GROUP_W:(C_SV + 1) * GROUP_W], bs, ts))
        ml_p = (ml_c_p, ml_n_p, ml_m_p)
        ml_s = (cn_s[..., :HEAD_DIM], cn_s[..., HEAD_DIM], m_s[:, 0, :N_HEADS])
        for st, ml, rs, h5, nb_ in ((p_st, ml_p, rs_p, h5_p, bp), (s_st, ml_s, rs_s, h5_s, bs)):
            off = 4 if st is p_st else 2
            st[off + 0].append(ml[0])
            st[off + 1].append(ml[1])
            st[off + 2].append(ml[2])
            st[off + 3].append(rs)
            st[off + 4].append(h5[:, :S5_W].reshape(nb_, S5_GROUPS, S5_STATE))
            st[off + 5].append(h5[:, S5_W:].reshape(nb_, S5_GROUPS, S5_STATE))

    y_prompt = y_out[0].reshape(bp, tp, D_MODEL)
    y_sample = y_out[1].reshape(bs, ts, D_MODEL)
    p_out = [jnp.stack(a, axis=1) for a in p_st]
    for i in range(2):
        p_out[i] = p_out[i].reshape(bp, DEPTH, N_HEADS, HEAD_DIM, tp).transpose(0, 1, 4, 2, 3)
    s_out = [jnp.stack(a, axis=1) for a in s_st]
    return (y_prompt, y_sample, *p_out, *s_out)
```

```python
import functools
import math

import numpy as np
import jax
import jax.numpy as jnp
from jax import lax
from jax.experimental import pallas as pl
from jax.experimental.pallas import tpu as pltpu

F32 = jnp.float32
BF16 = jnp.bfloat16

D_MODEL = 1024
DEPTH = 2
PAST_LEN = 8192
PAGE_SIZE = 128
HEAD_DIM = 64
N_HEADS = 4
GROUP_W = N_HEADS * HEAD_DIM
S5_GROUPS = 16
S5_GROUP = 16
S5_STATE = 64
S5_W = S5_GROUPS * S5_STATE
N_MEM = 256
D_FF = 3584
N_EXPERTS = 8
TOP_K = 2
CHUNK = 64
ROPE_BASE = 10000.0
LN_EPS = 1e-5
GN_EPS = 1e-6
ALPHA = (2 * DEPTH) ** 0.25
QK_SCALE = HEAD_DIM ** -0.5
LOG2E = math.log2(math.e)

LANES = 128
PROJ_W = 25 * LANES
C_SQ, C_SK, C_SV, C_MQ, C_MK, C_MV, C_MO, C_RQ, C_RK, C_RV, C_RG, C_SU = range(12)
C_GATES = 12 * GROUP_W // LANES
VMEM_LIMIT = 48 * 1024 * 1024
MOE_TILE = 1024
MIX_DTYPE = BF16


def _cp(*sem):
    return pltpu.CompilerParams(dimension_semantics=sem, vmem_limit_bytes=VMEM_LIMIT)


def _dot(a, b):
    return jnp.dot(a, b, preferred_element_type=F32)


def _dot_nt(a, b):
    return lax.dot_general(a, b, (((1,), (1,)), ((), ())), preferred_element_type=F32)


def _dot_tn(a, b):
    return lax.dot_general(a, b, (((0,), (0,)), ((), ())), preferred_element_type=F32)


def _layer_norm(y, g, b):
    mu = jnp.mean(y, axis=-1, keepdims=True)
    yc = y - mu
    var = jnp.mean(yc * yc, axis=-1, keepdims=True)
    return yc * lax.rsqrt(var + LN_EPS) * g + b


def _row_sum(x, scale=1.0):
    ones = jnp.full((x.shape[1], LANES), scale, BF16)
    hi = x.astype(BF16)
    lo = (x - hi.astype(F32)).astype(BF16)
    return (_dot(hi, ones) + _dot(lo, ones))[:, :x.shape[1]]


def _cumsum_rows(tril, x):
    hi = x.astype(BF16)
    lo = (x - hi.astype(F32)).astype(BF16)
    return _dot(tril, hi) + _dot(tril, lo)


def _head_norm_all(hd):
    inv = 1.0 / HEAD_DIM
    mu = {p: _row_sum(x, inv) for p, x in hd.items()}
    hc = {p: hd[p] - mu[p] for p in hd}
    var = {p: _row_sum(hc[p] * hc[p], inv) for p in hd}
    return {p: hc[p] * lax.rsqrt(var[p] + GN_EPS) for p in hd}


def _neg_softplus(z):
    return -(jnp.maximum(z, 0.0) + jnp.log1p(jnp.exp(-jnp.abs(z))))


def _log_sigmoid(z):
    return _neg_softplus(-z)


def _linear_kernel(x_ref, w_ref, o_ref):
    o_ref[...] = _dot(x_ref[...].astype(BF16), w_ref[...]).astype(o_ref.dtype)


def linear(x, w, tm, row_blk0=0, n_rows=None, out_dtype=F32):
    m, k = x.shape
    m = m if n_rows is None else n_rows
    n = w.shape[1]
    return pl.pallas_call(
        _linear_kernel, grid=(m // tm,),
        in_specs=[pl.BlockSpec((tm, k), lambda i: (row_blk0 + i, 0)), pl.BlockSpec((k, n), lambda i: (0, 0))],
        out_specs=pl.BlockSpec((tm, n), lambda i: (i, 0)),
        out_shape=jax.ShapeDtypeStruct((m, n), out_dtype),
        compiler_params=_cp("arbitrary"), name="linear")(x, w)


def _in_proj_kernel(x_ref, w_ref, o_ref, kt_ref, vt_ref):
    o = _dot(x_ref[...].astype(BF16), w_ref[...])
    o_ref[...] = o
    kt_ref[0] = o[:, C_SK * GROUP_W:(C_SK + 1) * GROUP_W].T
    vt_ref[0] = o[:, C_SV * GROUP_W:(C_SV + 1) * GROUP_W].T


def in_proj_prompt(x, w, n_batch, seq, tm):
    k = x.shape[1]
    n = w.shape[1]
    nt = seq // tm
    t_spec = pl.BlockSpec((1, GROUP_W, tm), lambda i: (i // nt, 0, i % nt))
    t_shape = jax.ShapeDtypeStruct((n_batch, GROUP_W, seq), F32)
    return pl.pallas_call(
        _in_proj_kernel, grid=(n_batch * nt,),
        in_specs=[pl.BlockSpec((tm, k), lambda i: (i, 0)), pl.BlockSpec((k, n), lambda i: (0, 0))],
        out_specs=[pl.BlockSpec((tm, n), lambda i: (i, 0)), t_spec, t_spec],
        out_shape=[jax.ShapeDtypeStruct((n_batch * seq, n), F32), t_shape, t_shape],
        compiler_params=_cp("arbitrary"), name="in_proj")(x, w)


def _mix_out_ln_kernel(x_ref, a_ref, b_ref, c_ref, d_ref, w_ref, g_ref, bias_ref, o_ref):
    h = sum(_dot(p[...].astype(BF16), w_ref[pl.ds(n * GROUP_W, GROUP_W), :])
            for n, p in enumerate((a_ref, b_ref, c_ref, d_ref)))
    o_ref[...] = _layer_norm(ALPHA * x_ref[...] + h, g_ref[...], bias_ref[...])


def mix_out_ln(x, parts, w, g, b, tm, row_blk0):
    rows = parts[0].shape[0]
    row_spec = pl.BlockSpec((tm, D_MODEL), lambda i: (row_blk0 + i, 0))
    part_spec = pl.BlockSpec((tm, GROUP_W), lambda i: (i, 0))
    vec = pl.BlockSpec((1, D_MODEL), lambda i: (0, 0))
    return pl.pallas_call(
        _mix_out_ln_kernel, grid=(rows // tm,),
        in_specs=[row_spec, part_spec, part_spec, part_spec, part_spec,
                  pl.BlockSpec((D_MODEL, D_MODEL), lambda i: (0, 0)), vec, vec],
        out_specs=row_spec, out_shape=jax.ShapeDtypeStruct(x.shape, F32), input_output_aliases={0: 0},
        compiler_params=_cp("arbitrary"), name="mix_out_ln")(x, *parts, w, g, b)


def _suffix_matrix():
    j = np.arange(LANES)
    u = (j[:, None] >= j[None, :]).astype(np.float32)
    uu = np.concatenate([u, np.ones((LANES, LANES), np.float32)], axis=1)
    return jnp.asarray(np.concatenate([uu, uu], axis=0), dtype=BF16)


def _suffix_sums(lr, uu):
    hi = lr.astype(BF16)
    lo = (lr - hi.astype(F32)).astype(BF16)
    r = _dot(jnp.concatenate([hi, lo], axis=1), uu)
    return r[:, :LANES], r[:, LANES:]


def _log2_rem(z2):
    return jnp.minimum(-z2, 0.0) - jnp.log2(1.0 + jnp.exp2(-jnp.abs(z2)))


def _sb_prompt_kernel(bias_ref, q_ref, k_ref, v_ref, uu_ref, o_ref, acc_ref, car_ref, kb_ref, vb_ref, *, tq):
    i = pl.program_id(1)
    tk = LANES
    nsub = tq // tk
    acc_ref[...] = jnp.zeros_like(acc_ref)
    car_ref[...] = jnp.zeros_like(car_ref)

    @pl.when(i == 0)
    def _():
        for h in range(N_HEADS):
            kb_ref[h] = k_ref[:, h * HEAD_DIM:(h + 1) * HEAD_DIM].astype(BF16)
            vb_ref[h] = v_ref[:, h * HEAD_DIM:(h + 1) * HEAD_DIM].astype(BF16)

    q = (q_ref[...] * (QK_SCALE * LOG2E)).astype(BF16)
    qh = [q[:, h * HEAD_DIM:(h + 1) * HEAD_DIM] for h in range(N_HEADS)]
    b2 = [bias_ref[h] * LOG2E for h in range(N_HEADS)]
    uu = uu_ref[...]
    row = lax.broadcasted_iota(jnp.int32, (tq, tk), 0)
    col = lax.broadcasted_iota(jnp.int32, (tq, tk), 1)
    heads = range(N_HEADS)

    def block_pair(j_hi, causal_hi, causal_lo):
        r0 = [pl.multiple_of((j_hi - d) * tk, tk) for d in range(2)]
        z2 = [[_dot_nt(qh[h], kb_ref[h, pl.ds(r0[d], tk), :]) + b2[h] for h in heads] for d in range(2)]
        cs, tot = [], []
        for d, causal in enumerate((causal_hi, causal_lo)):
            lr = [_log2_rem(z) for z in z2[d]]
            if causal is not None:
                lr = [jnp.where(causal, a, 0.0) for a in lr]
            c, t = _suffix_sums(jnp.concatenate(lr, axis=0), uu)
            cs.append(c)
            tot.append(t)
        car = [car_ref[h] for h in heads]
        pv = []
        for d, causal in enumerate((causal_hi, causal_lo)):
            w = [jnp.exp2(z2[d][h] + cs[d][h * tq:(h + 1) * tq] + car[h]) for h in heads]
            if causal is not None:
                w = [jnp.where(causal, a, 0.0) for a in w]
            pv.append([_dot(w[h].astype(BF16), vb_ref[h, pl.ds(r0[d], tk), :]) for h in heads])
            car = [car[h] + tot[d][h * tq:(h + 1) * tq] for h in heads]
        for h in heads:
            acc_ref[h] += pv[0][h] + pv[1][h]
            car_ref[h] = car[h]

    assert nsub == 2
    block_pair(i * nsub + 1, (col + tk) < row, col < row)

    def body(jj, carry):
        block_pair(i * nsub - 1 - 2 * jj, None, None)
        return carry

    lax.fori_loop(0, i, body, 0)
    o_ref[...] = jnp.concatenate([acc_ref[h] for h in range(N_HEADS)], axis=1).astype(o_ref.dtype)


def sb_attention_prompt(proj, sb_bias, uu, n_batch, seq, tq):
    nq = seq // tq
    kern = functools.partial(_sb_prompt_kernel, tq=tq)
    return pl.pallas_call(
        kern, grid=(n_batch, nq),
        in_specs=[pl.BlockSpec(memory_space=pltpu.SMEM),
                  pl.BlockSpec((tq, GROUP_W), lambda b, i: (b * nq + i, C_SQ)),
                  pl.BlockSpec((seq, GROUP_W), lambda b, i: (b, C_SK)),
                  pl.BlockSpec((seq, GROUP_W), lambda b, i: (b, C_SV)),
                  pl.BlockSpec((2 * LANES, 2 * LANES), lambda b, i: (0, 0))],
        out_specs=pl.BlockSpec((tq, GROUP_W), lambda b, i: (b * nq + i, 0)),
        out_shape=jax.ShapeDtypeStruct((n_batch * seq, GROUP_W), MIX_DTYPE),
        scratch_shapes=[pltpu.VMEM((N_HEADS, tq, HEAD_DIM), F32), pltpu.VMEM((N_HEADS, tq, LANES), F32),
                        pltpu.VMEM((N_HEADS, seq, HEAD_DIM), BF16), pltpu.VMEM((N_HEADS, seq, HEAD_DIM), BF16)],
        compiler_params=_cp("arbitrary", "arbitrary"), name="sb_prompt")(sb_bias, proj, proj, proj, uu)


def _sb_sample_kernel(pt_ref, bias_ref, q_ref, kn_ref, vn_ref, u8_ref, uu_ref, *rest, n_pp, n_steps):
    k_refs = rest[:n_pp]
    v_refs = rest[n_pp:2 * n_pp]
    o_ref = rest[2 * n_pp]
    acc_ref, car_ref = rest[2 * n_pp + 1:]
    s = pl.program_id(1)
    nq = q_ref.shape[0]
    rows = N_HEADS * nq
    q = (q_ref[...] * (QK_SCALE * LOG2E)).astype(BF16)
    row_head = lax.broadcasted_iota(jnp.int32, (rows, GROUP_W), 0) // nq
    own = row_head == lax.broadcasted_iota(jnp.int32, (rows, GROUP_W), 1) // HEAD_DIM
    q_bd = jnp.where(own, jnp.concatenate([q] * N_HEADS, axis=0), jnp.zeros((), BF16))
    row_head_l = lax.broadcasted_iota(jnp.int32, (rows, LANES), 0) // nq
    b2 = jnp.zeros((rows, LANES), F32)
    for h in range(N_HEADS):
        b2 = jnp.where(row_head_l == h, bias_ref[h] * LOG2E, b2)

    @pl.when(s == 0)
    def _():
        kn = kn_ref[...].astype(BF16)
        vn = vn_ref[...].astype(BF16)
        t = lax.broadcasted_iota(jnp.int32, (rows, nq), 0) % nq
        causal = lax.broadcasted_iota(jnp.int32, (rows, nq), 1) < t
        z2 = _dot_nt(q_bd, kn) + b2[:, :nq]
        lr = jnp.where(causal, _log2_rem(z2), 0.0)
        cs = jnp.dot(lr, u8_ref[...], preferred_element_type=F32, precision=lax.Precision.HIGHEST)
        w = jnp.where(causal, jnp.exp2(z2 + cs), 0.0)
        acc_ref[...] = _dot(w.astype(BF16), vn)
        car_ref[...] = jnp.broadcast_to(jnp.sum(lr, axis=1, keepdims=True), (rows, LANES))

    z2s = [_dot(q_bd, k_refs[p][0, 0].reshape(GROUP_W, PAGE_SIZE).astype(BF16)) + b2 for p in range(n_pp)]
    cs_all, tot_all = _suffix_sums(_log2_rem(jnp.concatenate(z2s, axis=0)), uu_ref[...])
    car = car_ref[...]
    acc = acc_ref[...]
    for p in range(n_pp):
        w = jnp.exp2(z2s[p] + cs_all[p * rows:(p + 1) * rows] + car)
        acc = acc + _dot_nt(w.astype(BF16), v_refs[p][0, 0].reshape(GROUP_W, PAGE_SIZE).astype(BF16))
        car = car + tot_all[p * rows:(p + 1) * rows]
    car_ref[...] = car
    acc_ref[...] = acc

    @pl.when(s == n_steps - 1)
    def _():
        kept = jnp.where(own, acc, 0.0)
        o_ref[...] = sum(kept[h * nq:(h + 1) * nq] for h in range(N_HEADS))


def sb_attention_sample(proj, row_blk0, cache_kt, cache_vt, page_table, layer, sb_bias, uu, n_pp):
    n_batch, n_pages = page_table.shape
    nq = 8
    n_steps = n_pages // n_pp
    u8 = jnp.asarray((np.arange(nq)[:, None] >= np.arange(nq)[None, :]).astype(np.float32))

    def page_spec(p):
        return pl.BlockSpec((1, 1, N_HEADS, HEAD_DIM, PAGE_SIZE),
                            lambda b, s, pt: (pt[b, n_pages - 1 - (s * n_pp + p)], layer, 0, 0, 0))

    def row_spec(cblk):
        return pl.BlockSpec((nq, GROUP_W), lambda b, s, pt: (row_blk0 + b, cblk))

    kern = functools.partial(_sb_sample_kernel, n_pp=n_pp, n_steps=n_steps)
    gs = pltpu.PrefetchScalarGridSpec(
        num_scalar_prefetch=1, grid=(n_batch, n_steps),
        in_specs=[pl.BlockSpec(memory_space=pltpu.SMEM), row_spec(C_SQ), row_spec(C_SK), row_spec(C_SV),
                  pl.BlockSpec((nq, nq), lambda b, s, pt: (0, 0)),
                  pl.BlockSpec((2 * LANES, 2 * LANES), lambda b, s, pt: (0, 0))]
                 + [page_spec(p) for p in range(n_pp)] * 2,
        out_specs=pl.BlockSpec((nq, GROUP_W), lambda b, s, pt: (b, 0)),
        scratch_shapes=[pltpu.VMEM((N_HEADS * nq, GROUP_W), F32), pltpu.VMEM((N_HEADS * nq, LANES), F32)])
    return pl.pallas_call(
        kern, grid_spec=gs, out_shape=jax.ShapeDtypeStruct((n_batch * nq, GROUP_W), F32),
        compiler_params=_cp("arbitrary", "arbitrary"), name="sb_sample")(
            page_table, sb_bias, proj, proj, proj, u8, uu, *([cache_kt] * n_pp), *([cache_vt] * n_pp))


def _mlstm_kernel(m0_ref, q_ref, k_ref, v_ref, og_ref, gt_ref, gb_ref, ng_ref, tril_ref, cn0_ref,
                  o_ref, cn_ref, m_ref, ms_ref, *, bb, chunk, n_t):
    bi = pl.program_id(0)
    t = pl.program_id(1)
    seqs = range(bb)
    pairs = [(b, h) for b in seqs for h in range(N_HEADS)]

    @pl.when(t == 0)
    def _():
        cn_ref[...] = cn0_ref[...]
        for b, h in pairs:
            ms_ref[b * N_HEADS + h] = jnp.full((1, LANES), m0_ref[bi * bb + b, h], F32)

    tril = tril_ref[...]
    tri_mask = lax.broadcasted_iota(jnp.int32, (chunk, chunk), 1) <= lax.broadcasted_iota(jnp.int32, (chunk, chunk), 0)
    lane = lax.broadcasted_iota(jnp.int32, (chunk, HEAD_DIM), 1)
    ones_col = jnp.where(lane == 0, 1.0, 0.0).astype(F32)
    hs = lambda h: slice(h * HEAD_DIM, (h + 1) * HEAD_DIM)

    gt = [gt_ref[b] + gb_ref[...] for b in seqs]
    bc = [_cumsum_rows(tril, _log_sigmoid(g)) for g in gt]
    gt_t = [g.T for g in gt]
    bc_t = [x.T for x in bc]
    q = [q_ref[b].astype(BF16) for b in seqs]
    k = [(k_ref[b] * QK_SCALE).astype(BF16) for b in seqs]
    v = [v_ref[b] for b in seqs]
    m_prev = {p: ms_ref[p[0] * N_HEADS + p[1]][:, :1] for p in pairs}
    ig_col = {(b, h): gt[b][:, h:h + 1] for b, h in pairs}
    bc_col = {(b, h): bc[b][:, N_HEADS + h:N_HEADS + h + 1] for b, h in pairs}
    dm = {(b, h): jnp.where(tri_mask, bc_col[b, h] - (bc_t[b][N_HEADS + h:N_HEADS + h + 1, :] - gt_t[b][h:h + 1, :]),
                            -jnp.inf) for b, h in pairs}
    a = {p: bc_col[p] + m_prev[p] for p in pairs}
    m_new = {p: jnp.maximum(a[p], jnp.max(dm[p], axis=1, keepdims=True)) for p in pairs}
    inter = {p: jnp.exp(a[p] - m_new[p]) for p in pairs}
    s = {(b, h): _dot_nt(q[b][:, hs(h)], k[b][:, hs(h)]) * jnp.exp(dm[b, h] - m_new[b, h]) for b, h in pairs}
    v_ext = {(b, h): jnp.concatenate([v[b][:, hs(h)], ones_col], axis=1) for b, h in pairs}
    qc = {(b, h): _dot(q[b][:, hs(h)], cn_ref[b, h].astype(BF16)) for b, h in pairs}
    sv = {p: _dot(s[p].astype(BF16), v_ext[p].astype(BF16)) for p in pairs}
    m_last = {p: m_new[p][chunk - 1:chunk, :] for p in pairs}
    wl = {p: jnp.exp(bc_col[p][chunk - 1:chunk, :] - bc_col[p] + ig_col[p] - m_last[p]) for p in pairs}
    dl = {p: jnp.exp(a[p][chunk - 1:chunk, :] - m_last[p]) for p in pairs}
    upd = {(b, h): _dot_tn(k[b][:, hs(h)], (wl[b, h] * v_ext[b, h]).astype(BF16)) for b, h in pairs}
    rs = {p: _row_sum(s[p])[:, :1] for p in pairs}
    num = {p: inter[p] * qc[p][:, :HEAD_DIM] + sv[p][:, :HEAD_DIM] for p in pairs}
    den = {p: inter[p] * qc[p][:, HEAD_DIM:HEAD_DIM + 1] + rs[p] for p in pairs}
    hh = _head_norm_all({p: num[p] / jnp.maximum(jnp.abs(den[p]), jnp.exp(-m_new[p])) for p in pairs})
    for b, h in pairs:
        cn_ref[b, h] = dl[b, h] * cn_ref[b, h] + upd[b, h]
        ms_ref[b * N_HEADS + h] = jnp.broadcast_to(m_last[b, h], (1, LANES))
    for b in seqs:
        y = jnp.concatenate([hh[b, h] for h in range(N_HEADS)], axis=1)
        o_ref[b] = (y * ng_ref[...] * jax.nn.sigmoid(og_ref[b])).astype(o_ref.dtype)

    @pl.when(t == n_t - 1)
    def _():
        lane_m = lax.broadcasted_iota(jnp.int32, (1, LANES), 1)
        for b in seqs:
            m_out = jnp.zeros((1, LANES), F32)
            for h in range(N_HEADS):
                m_out = jnp.where(lane_m == h, ms_ref[b * N_HEADS + h], m_out)
            m_ref[b] = m_out


def mlstm_mixer(proj3, gate_bias, norm_g, cn0, m0, bb, out_dtype):
    n_batch, seq, _ = proj3.shape
    chunk = math.gcd(seq, CHUNK)
    n_t = seq // chunk
    tril = jnp.asarray(np.tril(np.ones((chunk, chunk), np.float32)), dtype=BF16)

    def row_spec(cblk, w=GROUP_W):
        return pl.BlockSpec((bb, chunk, w), lambda bi, t: (bi, t, cblk))

    def const_spec(shape):
        return pl.BlockSpec(shape, lambda bi, t: (0,) * len(shape))

    state_spec = pl.BlockSpec((bb, N_HEADS, HEAD_DIM, LANES), lambda bi, t: (bi, 0, 0, 0))
    kern = functools.partial(_mlstm_kernel, bb=bb, chunk=chunk, n_t=n_t)
    return pl.pallas_call(
        kern, grid=(n_batch // bb, n_t),
        in_specs=[pl.BlockSpec(memory_space=pltpu.SMEM),
                  row_spec(C_MQ), row_spec(C_MK), row_spec(C_MV), row_spec(C_MO), row_spec(C_GATES, LANES),
                  const_spec((1, LANES)), const_spec((1, GROUP_W)), const_spec((chunk, chunk)), state_spec],
        out_specs=[pl.BlockSpec((bb, chunk, GROUP_W), lambda bi, t: (bi, t, 0)), state_spec,
                   pl.BlockSpec((bb, 1, LANES), lambda bi, t: (bi, 0, 0))],
        out_shape=[jax.ShapeDtypeStruct((n_batch, seq, GROUP_W), out_dtype),
                   jax.ShapeDtypeStruct((n_batch, N_HEADS, HEAD_DIM, LANES), F32),
                   jax.ShapeDtypeStruct((n_batch, 1, LANES), F32)],
        scratch_shapes=[pltpu.VMEM((bb * N_HEADS, 1, LANES), F32)],
        compiler_params=_cp("arbitrary", "arbitrary"), name="mlstm")(
            m0, proj3, proj3, proj3, proj3, proj3, gate_bias, norm_g, tril, cn0)


def _split_dot(x, w):
    hi = x.astype(BF16)
    lo = (x - hi.astype(F32)).astype(BF16)
    return _dot(hi, w) + _dot(lo, w)


def _mlstm_pair_kernel(q_ref, k_ref, v_ref, og_ref, gt_ref, gb_ref, ng_ref, tril_ref, sel_ref, blk_ref,
                       o_ref, cn_ref, st_ref, *, bb, chunk):
    t_id = pl.program_id(1)

    @pl.when(t_id == 0)
    def _():
        cn_ref[...] = jnp.zeros_like(cn_ref)
        st_ref[...] = jnp.zeros_like(st_ref)

    seqs = range(bb)
    n_pair = N_HEADS // 2
    chains = [(b, p) for b in seqs for p in range(n_pair)]
    lane = lax.broadcasted_iota(jnp.int32, (chunk, LANES), 1)
    row = lax.broadcasted_iota(jnp.int32, (chunk, LANES), 0)
    low = lane < HEAD_DIM
    causal = (lane % HEAD_DIM) <= row
    eye2 = (lane % HEAD_DIM) == row
    blk = blk_ref[...]
    same = (lax.broadcasted_iota(jnp.int32, (LANES, LANES), 0) // HEAD_DIM
            == lax.broadcasted_iota(jnp.int32, (LANES, LANES), 1) // HEAD_DIM)
    ones_ll = jnp.ones((chunk, chunk), BF16)
    tril = tril_ref[...]
    pl_ = lambda p: slice(p * LANES, (p + 1) * LANES)
    zero_b = jnp.zeros((), BF16)

    gt = [gt_ref[b] + gb_ref[...] for b in seqs]
    bc = [_cumsum_rows(tril, _log_sigmoid(g)) for g in gt]
    def stacked(fn, xs):
        keys = list(xs)
        y = fn(jnp.concatenate([xs[c] for c in keys], axis=0))
        return {c: y[i * chunk:(i + 1) * chunk] for i, c in enumerate(keys)}

    by_blk = lambda x: _split_dot(x, blk)
    ig_all = stacked(lambda x: _split_dot(x, sel_ref[0]), dict(enumerate(gt)))
    bc_all = stacked(lambda x: _split_dot(x, sel_ref[1]), dict(enumerate(bc)))
    igc = {(b, p): ig_all[b][:, pl_(p)] for b, p in chains}
    bcc = {(b, p): bc_all[b][:, pl_(p)] for b, p in chains}
    gc = {c: igc[c] - bcc[c] for c in chains}
    grow = {c: _split_dot_left(ones_ll, jnp.where(eye2, gc[c], 0.0)) for c in chains}
    gmax = {c: jnp.max(gc[c], axis=0, keepdims=True) for c in chains}
    m_prev = {(b, p): st_ref[b, p, 1:2, :] for b, p in chains}
    a = {c: bcc[c] + m_prev[c] for c in chains}
    m_stab = {c: jnp.maximum(a[c], bcc[c] + gmax[c]) for c in chains}
    inter = {c: jnp.exp(a[c] - m_stab[c]) for c in chains}
    dw = {c: jnp.exp(jnp.where(causal, bcc[c] + grow[c], -jnp.inf) - m_stab[c]) for c in chains}
    q2 = {(b, p): q_ref[b, :, pl_(p)].astype(BF16) for b, p in chains}
    k2f = {(b, p): k_ref[b, :, pl_(p)] * QK_SCALE for b, p in chains}
    k2 = {c: k2f[c].astype(BF16) for c in chains}
    v2 = {(b, p): v_ref[b, :, pl_(p)] for b, p in chains}
    kbd = {c: jnp.concatenate([jnp.where(low, k2[c], zero_b), jnp.where(low, zero_b, k2[c])], axis=0) for c in chains}
    v2b = {c: v2[c].astype(BF16) for c in chains}
    vbd = {c: jnp.concatenate([jnp.where(low, v2b[c], zero_b), jnp.where(low, zero_b, v2b[c])], axis=0)
           for c in chains}
    s = {c: _dot_nt(q2[c], kbd[c]) * dw[c] for c in chains}
    rs = stacked(by_blk, s)
    sv = {c: _dot(s[c].astype(BF16), vbd[c]) for c in chains}
    qc = {(b, p): _dot(q2[b, p], cn_ref[b, p].astype(BF16)) for b, p in chains}
    n_row = {(b, p): st_ref[b, p, 0:1, :] for b, p in chains}
    qn = stacked(by_blk, {c: q2[c].astype(F32) * n_row[c] for c in chains})
    hh = {}
    for c in chains:
        num = inter[c] * qc[c] + sv[c]
        den = inter[c] * qn[c] + rs[c]
        hh[c] = num / jnp.maximum(jnp.abs(den), jnp.exp(-m_stab[c]))
    inv = 1.0 / HEAD_DIM
    mu = stacked(by_blk, hh)
    hc = {c: hh[c] - mu[c] * inv for c in chains}
    var = stacked(by_blk, {c: hc[c] * hc[c] for c in chains})
    m_last = {c: m_stab[c][chunk - 1:chunk, :] for c in chains}
    wl = {c: jnp.exp(bcc[c][chunk - 1:chunk, :] - bcc[c] + igc[c] - m_last[c]) for c in chains}
    dl = {c: jnp.exp(a[c][chunk - 1:chunk, :] - m_last[c]) for c in chains}
    upd = {c: _dot_tn(k2[c], (wl[c] * v2[c]).astype(BF16)) for c in chains}
    for b, p in chains:
        c = (b, p)
        y = hc[c] * lax.rsqrt(var[c] * inv + GN_EPS)
        o_ref[b, :, pl_(p)] = (y * ng_ref[:, pl_(p)] * jax.nn.sigmoid(og_ref[b, :, pl_(p)])).astype(o_ref.dtype)
        cn_ref[b, p] = dl[c] * cn_ref[b, p] + jnp.where(same, upd[c], 0.0)
        st_ref[b, p, 0:1, :] = dl[c] * n_row[c] + jnp.sum(wl[c] * k2f[c], axis=0, keepdims=True)
        st_ref[b, p, 1:2, :] = m_last[c]


def _split_dot_left(w, x):
    hi = x.astype(BF16)
    lo = (x - hi.astype(F32)).astype(BF16)
    return _dot(w, hi) + _dot(w, lo)


def mlstm_mixer_prompt(proj3, gate_bias, norm_g, bb, out_dtype):
    n_batch, seq, _ = proj3.shape
    chunk = CHUNK
    n_pair = N_HEADS // 2
    tril = jnp.asarray(np.tril(np.ones((chunk, chunk), np.float32)), dtype=BF16)
    sel = np.zeros((2, LANES, GROUP_W), np.float32)
    for h in range(N_HEADS):
        sel[0, h, h * HEAD_DIM:(h + 1) * HEAD_DIM] = 1.0
        sel[1, N_HEADS + h, h * HEAD_DIM:(h + 1) * HEAD_DIM] = 1.0
    idx = np.arange(LANES) // HEAD_DIM
    blk = (idx[:, None] == idx[None, :]).astype(np.float32)

    def row_spec(cblk, w=GROUP_W):
        return pl.BlockSpec((bb, chunk, w), lambda bi, t: (bi, t, cblk))

    def const_spec(shape):
        return pl.BlockSpec(shape, lambda bi, t: (0,) * len(shape))

    cn_spec = pl.BlockSpec((bb, n_pair, LANES, LANES), lambda bi, t: (bi, 0, 0, 0))
    st_spec = pl.BlockSpec((bb, n_pair, 8, LANES), lambda bi, t: (bi, 0, 0, 0))
    kern = functools.partial(_mlstm_pair_kernel, bb=bb, chunk=chunk)
    o, cn, st = pl.pallas_call(
        kern, grid=(n_batch // bb, seq // chunk),
        in_specs=[row_spec(C_MQ), row_spec(C_MK), row_spec(C_MV), row_spec(C_MO), row_spec(C_GATES, LANES),
                  const_spec((1, LANES)), const_spec((1, GROUP_W)), const_spec((chunk, chunk)),
                  const_spec(sel.shape), const_spec(blk.shape)],
        out_specs=[pl.BlockSpec((bb, chunk, GROUP_W), lambda bi, t: (bi, t, 0)), cn_spec, st_spec],
        out_shape=[jax.ShapeDtypeStruct((n_batch, seq, GROUP_W), out_dtype),
                   jax.ShapeDtypeStruct((n_batch, n_pair, LANES, LANES), F32),
                   jax.ShapeDtypeStruct((n_batch, n_pair, 8, LANES), F32)],
        compiler_params=_cp("arbitrary", "arbitrary"), name="mlstm_prompt")(
            proj3, proj3, proj3, proj3, proj3, gate_bias, norm_g, tril,
            jnp.asarray(sel, dtype=BF16), jnp.asarray(blk, dtype=BF16))
    c5 = cn.reshape(n_batch, n_pair, 2, HEAD_DIM, 2, HEAD_DIM)
    c_state = jnp.stack([c5[:, :, j, :, j, :] for j in range(2)], axis=2).reshape(n_batch, N_HEADS, HEAD_DIM, HEAD_DIM)
    n_state = st[:, :, 0, :].reshape(n_batch, N_HEADS, HEAD_DIM)
    m_state = st[:, :, 1, :].reshape(n_batch, N_HEADS, HEAD_DIM)[:, :, 0]
    return o, c_state, n_state, m_state


def _rope(x, cos, sin_signed):
    lane = lax.broadcasted_iota(jnp.int32, x.shape, 1)
    half = HEAD_DIM // 2
    swapped = jnp.where((lane % HEAD_DIM) < half, pltpu.roll(x, x.shape[1] - half, 1), pltpu.roll(x, half, 1))
    return x * cos + swapped * sin_signed


def _retention_kernel(q_ref, k_ref, v_ref, gg_ref, cos_ref, sin_ref, ng_ref, dec_ref, int_ref, wl_ref, dl_ref,
                      s0_ref, o_ref, s_ref, *, bb):
    t = pl.program_id(1)
    seqs = range(bb)
    pairs = [(b, h) for b in seqs for h in range(N_HEADS)]
    hs = lambda h: slice(h * HEAD_DIM, (h + 1) * HEAD_DIM)

    @pl.when(t == 0)
    def _():
        s_ref[...] = s0_ref[...]

    cos = cos_ref[...]
    sin = sin_ref[...]
    q = [_rope(q_ref[b], cos, sin).astype(BF16) for b in seqs]
    k = [(_rope(k_ref[b], cos, sin) * QK_SCALE).astype(BF16) for b in seqs]
    v = [v_ref[b] for b in seqs]
    s = {(b, h): _dot_nt(q[b][:, hs(h)], k[b][:, hs(h)]) * dec_ref[h] for b, h in pairs}
    qs = {(b, h): _dot(q[b][:, hs(h)], s_ref[b, h].astype(BF16)) for b, h in pairs}
    sv = {(b, h): _dot(s[b, h].astype(BF16), v[b][:, hs(h)].astype(BF16)) for b, h in pairs}
    upd = {(b, h): _dot_tn(k[b][:, hs(h)], (wl_ref[h] * v[b][:, hs(h)]).astype(BF16)) for b, h in pairs}
    o = _head_norm_all({(b, h): int_ref[h] * qs[b, h] + sv[b, h] for b, h in pairs})
    for b, h in pairs:
        s_ref[b, h] = dl_ref[h] * s_ref[b, h] + upd[b, h]
    for b in seqs:
        gg = gg_ref[b]
        y = jnp.concatenate([o[b, h] for h in range(N_HEADS)], axis=1) * ng_ref[...] * (gg * jax.nn.sigmoid(gg))
        o_ref[b] = y.astype(o_ref.dtype)


def _retention_consts(chunk):
    log_g = np.log(1.0 - np.exp2(-5.0 - np.arange(N_HEADS, dtype=np.float64)))
    tau = np.arange(chunk, dtype=np.float64)
    rel = tau[:, None] - tau[None, :]
    decay = np.where(rel >= 0, np.exp(log_g[:, None, None] * np.maximum(rel, 0.0)), 0.0)
    inter = np.exp(log_g[:, None] * (tau + 1.0))[..., None]
    wl = np.exp(log_g[:, None] * (chunk - 1.0 - tau))[..., None]
    dl = np.exp(log_g * chunk)[:, None, None]
    return tuple(jnp.asarray(a, F32) for a in (decay, inter, wl, dl))


def _retention_pair_kernel(q_ref, k_ref, v_ref, gg_ref, cos_ref, sin_ref, ng_ref, dec_ref, int_ref, wl_ref, dl_ref,
                           swap_ref, blk_ref, o_ref, s_ref, *, bb, chunk):
    t_id = pl.program_id(1)

    @pl.when(t_id == 0)
    def _():
        s_ref[...] = jnp.zeros_like(s_ref)

    seqs = range(bb)
    n_pair = N_HEADS // 2
    chains = [(b, p) for b in seqs for p in range(n_pair)]
    low = lax.broadcasted_iota(jnp.int32, (chunk, LANES), 1) < HEAD_DIM
    same = (lax.broadcasted_iota(jnp.int32, (LANES, LANES), 0) // HEAD_DIM
            == lax.broadcasted_iota(jnp.int32, (LANES, LANES), 1) // HEAD_DIM)
    pl_ = lambda p: slice(p * LANES, (p + 1) * LANES)
    zero_b = jnp.zeros((), BF16)
    blk = blk_ref[...]
    cos = cos_ref[...]
    sin = sin_ref[...]

    def stacked(fn, xs):
        keys = list(xs)
        y = fn(jnp.concatenate([xs[c] for c in keys], axis=0))
        return {c: y[i * chunk:(i + 1) * chunk] for i, c in enumerate(keys)}

    swap = lambda x: _split_dot(x, swap_ref[...])
    qf = {b: q_ref[b] for b in seqs}
    kf = {b: k_ref[b] for b in seqs}
    q_sw = stacked(swap, qf)
    k_sw = stacked(swap, kf)
    q = {b: (qf[b] * cos + q_sw[b] * sin).astype(BF16) for b in seqs}
    k = {b: ((kf[b] * cos + k_sw[b] * sin) * QK_SCALE).astype(BF16) for b in seqs}
    k2 = {(b, p): k[b][:, pl_(p)] for b, p in chains}
    v2 = {(b, p): v_ref[b, :, pl_(p)] for b, p in chains}
    v2b = {c: v2[c].astype(BF16) for c in chains}
    kbd = {c: jnp.concatenate([jnp.where(low, k2[c], zero_b), jnp.where(low, zero_b, k2[c])], axis=0) for c in chains}
    vbd = {c: jnp.concatenate([jnp.where(low, v2b[c], zero_b), jnp.where(low, zero_b, v2b[c])], axis=0)
           for c in chains}
    s = {(b, p): _dot_nt(q[b][:, pl_(p)], kbd[b, p]) * dec_ref[p] for b, p in chains}
    qs = {(b, p): _dot(q[b][:, pl_(p)], s_ref[b, p].astype(BF16)) for b, p in chains}
    sv = {c: _dot(s[c].astype(BF16), vbd[c]) for c in chains}
    upd = {(b, p): _dot_tn(k2[b, p], (wl_ref[p] * v2[b, p]).astype(BF16)) for b, p in chains}
    o = {(b, p): int_ref[p] * qs[b, p] + sv[b, p] for b, p in chains}
    inv = 1.0 / HEAD_DIM
    by_blk = lambda x: _split_dot(x, blk)
    mu = stacked(by_blk, o)
    oc = {c: o[c] - mu[c] * inv for c in chains}
    var = stacked(by_blk, {c: oc[c] * oc[c] for c in chains})
    for b, p in chains:
        c = (b, p)
        gg = gg_ref[b, :, pl_(p)]
        y = oc[c] * lax.rsqrt(var[c] * inv + GN_EPS) * ng_ref[:, pl_(p)] * (gg * jax.nn.sigmoid(gg))
        o_ref[b, :, pl_(p)] = y.astype(o_ref.dtype)
        s_ref[b, p] = dl_ref[p] * s_ref[b, p] + jnp.where(same, upd[c], 0.0)


def retention_mixer_prompt(proj3, cos, sin_signed, norm_g, bb, out_dtype):
    n_batch, seq, _ = proj3.shape
    chunk = CHUNK
    n_pair = N_HEADS // 2
    dec, inter, wl, dl = _retention_consts(chunk)
    rep = lambda a: jnp.broadcast_to(a, a.shape[:-1] + (HEAD_DIM,))
    pair = lambda a: jnp.concatenate([a[0::2], a[1::2]], axis=-1)
    dec2, int2, wl2, dl2 = pair(dec), pair(rep(inter)), pair(rep(wl)), pair(rep(dl))
    src = np.arange(GROUP_W)
    partner = (src // HEAD_DIM) * HEAD_DIM + (src % HEAD_DIM + HEAD_DIM // 2) % HEAD_DIM
    swap = np.zeros((GROUP_W, GROUP_W), np.float32)
    swap[partner, src] = 1.0
    idx = np.arange(LANES) // HEAD_DIM
    blk = (idx[:, None] == idx[None, :]).astype(np.float32)

    def row_spec(cblk):
        return pl.BlockSpec((bb, chunk, GROUP_W), lambda bi, t: (bi, t, cblk))

    def const_spec(shape):
        return pl.BlockSpec(shape, lambda bi, t: (0,) * len(shape))

    pos_spec = pl.BlockSpec((chunk, GROUP_W), lambda bi, t: (t, 0))
    state_spec = pl.BlockSpec((bb, n_pair, LANES, LANES), lambda bi, t: (bi, 0, 0, 0))
    kern = functools.partial(_retention_pair_kernel, bb=bb, chunk=chunk)
    o, sbd = pl.pallas_call(
        kern, grid=(n_batch // bb, seq // chunk),
        in_specs=[row_spec(C_RQ), row_spec(C_RK), row_spec(C_RV), row_spec(C_RG), pos_spec, pos_spec,
                  const_spec((1, GROUP_W)), const_spec(dec2.shape), const_spec(int2.shape), const_spec(wl2.shape),
                  const_spec(dl2.shape), const_spec(swap.shape), const_spec(blk.shape)],
        out_specs=[pl.BlockSpec((bb, chunk, GROUP_W), lambda bi, t: (bi, t, 0)), state_spec],
        out_shape=[jax.ShapeDtypeStruct((n_batch, seq, GROUP_W), out_dtype),
                   jax.ShapeDtypeStruct((n_batch, n_pair, LANES, LANES), F32)],
        compiler_params=_cp("arbitrary", "arbitrary"), name="retention_prompt")(
            proj3, proj3, proj3, proj3, cos, sin_signed, norm_g, dec2, int2, wl2, dl2,
            jnp.asarray(swap, dtype=BF16), jnp.asarray(blk, dtype=BF16))
    s5d = sbd.reshape(n_batch, n_pair, 2, HEAD_DIM, 2, HEAD_DIM)
    state = jnp.stack([s5d[:, :, j, :, j, :] for j in range(2)], axis=2).reshape(n_batch, N_HEADS, HEAD_DIM, HEAD_DIM)
    return o, state


def retention_mixer(proj3, cos, sin_signed, norm_g, s0, bb, out_dtype):
    n_batch, seq, _ = proj3.shape
    chunk = math.gcd(seq, CHUNK)
    dec, inter, wl, dl = _retention_consts(chunk)

    def row_spec(cblk):
        return pl.BlockSpec((bb, chunk, GROUP_W), lambda bi, t: (bi, t, cblk))

    def const_spec(shape):
        return pl.BlockSpec(shape, lambda bi, t: (0,) * len(shape))

    pos_spec = pl.BlockSpec((chunk, GROUP_W), lambda bi, t: (t, 0))
    state_spec = pl.BlockSpec((bb, N_HEADS, HEAD_DIM, HEAD_DIM), lambda bi, t: (bi, 0, 0, 0))
    kern = functools.partial(_retention_kernel, bb=bb)
    return pl.pallas_call(
        kern, grid=(n_batch // bb, seq // chunk),
        in_specs=[row_spec(C_RQ), row_spec(C_RK), row_spec(C_RV), row_spec(C_RG), pos_spec, pos_spec,
                  const_spec((1, GROUP_W)), const_spec(dec.shape), const_spec(inter.shape), const_spec(wl.shape),
                  const_spec(dl.shape), state_spec],
        out_specs=[pl.BlockSpec((bb, chunk, GROUP_W), lambda bi, t: (bi, t, 0)), state_spec],
        out_shape=[jax.ShapeDtypeStruct((n_batch, seq, GROUP_W), out_dtype),
                   jax.ShapeDtypeStruct((n_batch, N_HEADS, HEAD_DIM, HEAD_DIM), F32)],
        compiler_params=_cp("arbitrary", "arbitrary"), name="retention")(
            proj3, proj3, proj3, proj3, cos, sin_signed, norm_g, dec, inter, wl, dl, s0)


def _s5_kernel(u_ref, wb_ref, a1_ref, a2_ref, h0_ref, wc_ref, d_ref, gw_ref, gb_ref, o_ref, hl_ref,
               hs_ref, ut_ref, yt_ref, *, nb, tt):
    c = pl.program_id(0)

    @pl.when(c == 0)
    def _():
        hl_ref[...] = h0_ref[...]

    halves = range(GROUP_W // LANES)
    for b in range(nb):
        ub = u_ref[b]
        for hf in halves:
            ut_ref[hf, pl.ds(b, tt, stride=nb), :] = ub[:, hf * LANES:(hf + 1) * LANES]
    u = jnp.concatenate([ut_ref[hf] for hf in halves], axis=1)
    hs_ref[...] = _dot(u.astype(BF16), wb_ref[...])
    a1 = jnp.broadcast_to(a1_ref[...], (nb, 2 * S5_W))
    a2 = jnp.broadcast_to(a2_ref[...], (nb, 2 * S5_W))

    def step(t, h):
        r0 = pl.multiple_of(t * nb, nb)
        swapped = jnp.concatenate([h[:, S5_W:], h[:, :S5_W]], axis=1)
        h = a1 * h + a2 * swapped + hs_ref[pl.ds(r0, nb), :]
        hs_ref[pl.ds(r0, nb), :] = h
        return h

    hl_ref[...] = lax.fori_loop(0, tt, step, hl_ref[...])
    y = _dot(hs_ref[...].astype(BF16), wc_ref[...]) + d_ref[...] * u
    g5 = jax.nn.gelu(y)
    yt = g5 * jax.nn.sigmoid(_dot(g5.astype(BF16), gw_ref[...]) + gb_ref[...])
    for hf in halves:
        yt_ref[hf] = yt[:, hf * LANES:(hf + 1) * LANES]
    for b in range(nb):
        o_ref[b] = jnp.concatenate([yt_ref[hf, pl.ds(b, tt, stride=nb), :] for hf in halves],
                                   axis=1).astype(o_ref.dtype)


def s5_mixer(proj3, tt, wb, a1, a2, h0, wc, d, glu_w, glu_b, out_dtype):
    nb, seq, _ = proj3.shape
    rows = tt * nb

    def const_spec(shape):
        return pl.BlockSpec(shape, lambda c: (0,) * len(shape))

    kern = functools.partial(_s5_kernel, nb=nb, tt=tt)
    return pl.pallas_call(
        kern, grid=(seq // tt,),
        in_specs=[pl.BlockSpec((nb, tt, GROUP_W), lambda c: (0, c, C_SU)),
                  const_spec(wb.shape), const_spec(a1.shape), const_spec(a2.shape), const_spec(h0.shape),
                  const_spec(wc.shape), const_spec(d.shape), const_spec(glu_w.shape), const_spec(glu_b.shape)],
        out_specs=[pl.BlockSpec((nb, tt, GROUP_W), lambda c: (0, c, 0)), const_spec(h0.shape)],
        out_shape=[jax.ShapeDtypeStruct((nb, seq, GROUP_W), out_dtype), jax.ShapeDtypeStruct(h0.shape, F32)],
        scratch_shapes=[pltpu.VMEM((rows, 2 * S5_W), F32), pltpu.VMEM((GROUP_W // LANES, rows, LANES), F32),
                        pltpu.VMEM((GROUP_W // LANES, rows, LANES), F32)],
        compiler_params=_cp("arbitrary"), name="s5")(proj3, wb, a1, a2, h0, wc, d, glu_w, glu_b)


def _s5_weights(a_re, a_im, log_dt, b_re, b_im, c_re, c_im):
    lam = lax.complex(a_re, a_im)
    a_bar = jnp.exp(lam * jnp.exp(log_dt))
    b_bar = ((a_bar - 1.0) / lam)[..., None] * lax.complex(b_re, b_im)
    eye = jnp.eye(S5_GROUPS, dtype=F32)

    def in_map(m):
        return jnp.einsum('gpc,gh->gchp', m, eye).reshape(S5_GROUPS * S5_GROUP, S5_W)

    def out_map(m):
        return jnp.einsum('gcp,gh->gphc', m, eye).reshape(S5_W, S5_GROUPS * S5_GROUP)

    wb = jnp.concatenate([in_map(b_bar.real), in_map(b_bar.imag)], axis=1).astype(BF16)
    wc = jnp.concatenate([out_map(c_re), -out_map(c_im)], axis=0).astype(BF16)
    ar = a_bar.real.reshape(1, S5_W)
    ai = a_bar.imag.reshape(1, S5_W)
    return wb, jnp.concatenate([ar, ar], axis=1), jnp.concatenate([-ai, ai], axis=1), wc


def _cross_attn_kernel(x_ref, wq_ref, k_ref, v_ref, wo_ref, g_ref, b_ref, o_ref, *, nb):
    x = x_ref[...]
    rows = x.shape[0] // nb
    q = (_dot(x.astype(BF16), wq_ref[...]) * QK_SCALE).astype(BF16)
    hs = [slice(h * HEAD_DIM, (h + 1) * HEAD_DIM) for h in range(N_HEADS)]
    pairs = [(b, h) for b in range(nb) for h in range(N_HEADS)]
    k = [k_ref[b].astype(BF16) for b in range(nb)]
    v = [v_ref[b].astype(BF16) for b in range(nb)]
    s = {(b, h): _dot_nt(q[b * rows:(b + 1) * rows, hs[h]], k[b][:, hs[h]]) for b, h in pairs}
    e = {c: jnp.exp(s[c] - jnp.max(s[c], axis=1, keepdims=True)) for c in pairs}
    p = {c: e[c] / jnp.sum(e[c], axis=1, keepdims=True) for c in pairs}
    pv = {(b, h): _dot(p[b, h].astype(BF16), v[b][:, hs[h]]) for b, h in pairs}
    o = jnp.concatenate([jnp.concatenate([pv[b, h] for h in range(N_HEADS)], axis=1) for b in range(nb)], axis=0)
    y = ALPHA * x + _dot(o.astype(BF16), wo_ref[...])
    o_ref[...] = _layer_norm(y, g_ref[...], b_ref[...])


def cross_attn_ln(x, row_blk0, n_batch, seq, tq, nb, mem_k, mem_v, wq, wo, g, b):
    nq = seq // tq if nb == 1 else 1

    def const_spec(shape):
        return pl.BlockSpec(shape, lambda bb, i: (0,) * len(shape))

    row_spec = pl.BlockSpec((tq, D_MODEL), lambda bb, i: (row_blk0 + bb * nq + i, 0))
    mem_spec = pl.BlockSpec((nb, N_MEM, GROUP_W), lambda bb, i: (bb, 0, 0))
    return pl.pallas_call(
        functools.partial(_cross_attn_kernel, nb=nb), grid=(n_batch // nb, nq),
        in_specs=[row_spec, const_spec(wq.shape), mem_spec, mem_spec, const_spec(wo.shape),
                  const_spec(g.shape), const_spec(b.shape)],
        out_specs=row_spec, out_shape=jax.ShapeDtypeStruct(x.shape, F32), input_output_aliases={0: 0},
        compiler_params=_cp("arbitrary", "arbitrary"), name="cross_attn")(x, wq, mem_k, mem_v, wo, g, b)


SWIGLU_ROWS = 256


def _swiglu_accumulate(xb_ref, wg, wu, wd, acc_ref, n_valid=None):
    wgb, wub, wdb = wg.astype(BF16), wu.astype(BF16), wd.astype(BF16)
    n_sub = xb_ref.shape[0] // SWIGLU_ROWS

    def hidden(r):
        xb = xb_ref[pl.ds(r * SWIGLU_ROWS, SWIGLU_ROWS), :]
        gate = _dot(xb, wgb)
        up = _dot(xb, wub)
        return (gate * jax.nn.sigmoid(gate) * up).astype(BF16)

    def first_sub_blocks(n):
        hid = hidden(0)
        for r in range(n):
            nxt = hidden(r + 1) if r + 1 < n else None
            acc_ref[pl.ds(r * SWIGLU_ROWS, SWIGLU_ROWS), :] += _dot(hid, wdb)
            hid = nxt

    if n_valid is None:
        first_sub_blocks(n_sub)
        return
    need = (n_valid + SWIGLU_ROWS - 1) // SWIGLU_ROWS
    for n in range(1, n_sub + 1):
        pl.when(need == n)(functools.partial(first_sub_blocks, n))


def _ffn_kernel(x_ref, wg_ref, wu_ref, wd_ref, g_ref, b_ref, o_ref, xb_ref, *, nf):
    j = pl.program_id(1)

    @pl.when(j == 0)
    def _():
        xb_ref[...] = x_ref[...].astype(BF16)
        o_ref[...] = jnp.zeros_like(o_ref)

    _swiglu_accumulate(xb_ref, wg_ref[...], wu_ref[...], wd_ref[...], o_ref)

    @pl.when(j == nf - 1)
    def _():
        o_ref[...] = _layer_norm(ALPHA * x_ref[...] + o_ref[...], g_ref[...], b_ref[...])


def ffn_ln(x, wg, wu, wd, g, b, tm, tf):
    m = x.shape[0]
    nf = D_FF // tf
    kern = functools.partial(_ffn_kernel, nf=nf)
    return pl.pallas_call(
        kern, grid=(m // tm, nf),
        in_specs=[pl.BlockSpec((tm, D_MODEL), lambda i, j: (i, 0)),
                  pl.BlockSpec((D_MODEL, tf), lambda i, j: (0, j)), pl.BlockSpec((D_MODEL, tf), lambda i, j: (0, j)),
                  pl.BlockSpec((tf, D_MODEL), lambda i, j: (j, 0)),
                  pl.BlockSpec((1, D_MODEL), lambda i, j: (0, 0)), pl.BlockSpec((1, D_MODEL), lambda i, j: (0, 0))],
        out_specs=pl.BlockSpec((tm, D_MODEL), lambda i, j: (i, 0)),
        out_shape=jax.ShapeDtypeStruct((m, D_MODEL), F32),
        scratch_shapes=[pltpu.VMEM((tm, D_MODEL), BF16)],
        compiler_params=_cp("arbitrary", "arbitrary"), name="ffn")(x, wg, wu, wd, g, b)


def _router_kernel(x_ref, w_ref, b_ref, lt_ref, o_ref, before_ref, cnt_ref):
    @pl.when(pl.program_id(0) == 0)
    def _():
        cnt_ref[...] = jnp.zeros_like(cnt_ref)

    logits = jnp.dot(x_ref[...], w_ref[...], preferred_element_type=F32, precision=lax.Precision.HIGHEST) + b_ref[...]
    lane = lax.broadcasted_iota(jnp.int32, logits.shape, 1)
    neg = jnp.float32(-jnp.inf)
    lg = jnp.where(lane < N_EXPERTS, logits, neg)
    m1 = jnp.max(lg, axis=1, keepdims=True)
    i1 = jnp.min(jnp.where(lg == m1, lane, LANES), axis=1, keepdims=True)
    lg2 = jnp.where(lane == i1, neg, lg)
    m2 = jnp.max(lg2, axis=1, keepdims=True)
    i2 = jnp.min(jnp.where(lg2 == m2, lane, LANES), axis=1, keepdims=True)
    e2 = jnp.exp(m2 - m1)
    g1 = 1.0 / (1.0 + e2)
    g2 = e2 / (1.0 + e2)
    out = jnp.where(lane == 0, i1.astype(F32), jnp.where(lane == 1, i2.astype(F32),
                    jnp.where(lane == 2, g1, jnp.where(lane == 3, g2, 0.0))))
    o_ref[...] = out
    chosen = jnp.where(jnp.logical_or(lane == i1, lane == i2), 1.0, 0.0)
    cnt = cnt_ref[0:1, :]
    before_ref[...] = _dot(lt_ref[...], chosen.astype(BF16)) + cnt
    cnt_ref[...] = jnp.broadcast_to(cnt + jnp.sum(chosen, axis=0, keepdims=True), cnt_ref.shape)


def router(x, w_pad, b_pad, tm):
    m = x.shape[0]
    lower = jnp.asarray(np.tril(np.ones((tm, tm), np.float32), -1), dtype=BF16)
    row = pl.BlockSpec((tm, LANES), lambda i: (i, 0))
    return pl.pallas_call(
        _router_kernel, grid=(m // tm,),
        in_specs=[pl.BlockSpec((tm, D_MODEL), lambda i: (i, 0)), pl.BlockSpec((D_MODEL, LANES), lambda i: (0, 0)),
                  pl.BlockSpec((1, LANES), lambda i: (0, 0)), pl.BlockSpec((tm, tm), lambda i: (0, 0))],
        out_specs=[row, row, pl.BlockSpec((8, LANES), lambda i: (0, 0))],
        out_shape=[jax.ShapeDtypeStruct((m, LANES), F32), jax.ShapeDtypeStruct((m, LANES), F32),
                   jax.ShapeDtypeStruct((8, LANES), F32)],
        compiler_params=_cp("arbitrary"), name="router")(x, w_pad, b_pad, lower)


def _moe_ffn_kernel(te_ref, nu_ref, tr_ref, x_ref, wg_ref, wu_ref, wd_ref, o_ref, xb_ref):
    i = pl.program_id(0)
    j = pl.program_id(1)
    used = i < nu_ref[0]

    @pl.when(used)
    def _():
        @pl.when(j == 0)
        def _():
            xb_ref[...] = x_ref[...].astype(BF16)
            o_ref[...] = jnp.zeros_like(o_ref)

        _swiglu_accumulate(xb_ref, wg_ref[0], wu_ref[0], wd_ref[0], o_ref, tr_ref[i])

    @pl.when(jnp.logical_and(jnp.logical_not(used), j == 0))
    def _():
        o_ref[...] = jnp.zeros_like(o_ref)


def moe_ffn(x_sorted, tile_expert, n_used, tile_rows, wg, wu, wd, tf):
    n_rows = x_sorted.shape[0]
    n_tiles = n_rows // MOE_TILE
    nf = D_FF // tf

    def jj(i, j, nu):
        return jnp.where(i < nu[0], j, nf - 1)

    gs = pltpu.PrefetchScalarGridSpec(
        num_scalar_prefetch=3, grid=(n_tiles, nf),
        in_specs=[pl.BlockSpec((MOE_TILE, D_MODEL), lambda i, j, te, nu, tr: (i, 0)),
                  pl.BlockSpec((1, D_MODEL, tf), lambda i, j, te, nu, tr: (te[i], 0, jj(i, j, nu))),
                  pl.BlockSpec((1, D_MODEL, tf), lambda i, j, te, nu, tr: (te[i], 0, jj(i, j, nu))),
                  pl.BlockSpec((1, tf, D_MODEL), lambda i, j, te, nu, tr: (te[i], jj(i, j, nu), 0))],
        out_specs=pl.BlockSpec((MOE_TILE, D_MODEL), lambda i, j, te, nu, tr: (i, 0)),
        scratch_shapes=[pltpu.VMEM((MOE_TILE, D_MODEL), BF16)])
    return pl.pallas_call(
        _moe_ffn_kernel, grid_spec=gs, out_shape=jax.ShapeDtypeStruct((n_rows, D_MODEL), F32),
        compiler_params=_cp("arbitrary", "arbitrary"), name="moe_ffn")(
            tile_expert, n_used, tile_rows, x_sorted, wg, wu, wd)


def _combine_ln_kernel(x_ref, r_ref, ya_ref, yb_ref, g_ref, b_ref, o_ref):
    r = r_ref[...]
    y = r[:, TOP_K:TOP_K + 1] * ya_ref[...] + r[:, TOP_K + 1:TOP_K + 2] * yb_ref[...]
    o_ref[...] = _layer_norm(ALPHA * x_ref[...] + y, g_ref[...], b_ref[...])


def combine_ln(x, r, ya, yb, g, b, tm, row_blk0, rows):
    row = pl.BlockSpec((tm, D_MODEL), lambda i: (row_blk0 + i, 0))
    vec = pl.BlockSpec((1, D_MODEL), lambda i: (0, 0))
    return pl.pallas_call(
        _combine_ln_kernel, grid=(rows // tm,),
        in_specs=[row, pl.BlockSpec((tm, LANES), lambda i: (row_blk0 + i, 0)), row, row, vec, vec],
        out_specs=pl.BlockSpec((tm, D_MODEL), lambda i: (i, 0)),
        out_shape=jax.ShapeDtypeStruct((rows, D_MODEL), F32),
        compiler_params=_cp("arbitrary"), name="combine_ln")(x, r, ya, yb, g, b)


def moe_ln(x, router_w, router_b, wg, wu, wd, g, b, tm, splits):
    m = x.shape[0]
    w_pad = jnp.zeros((D_MODEL, LANES), F32).at[:, :N_EXPERTS].set(router_w)
    b_pad = jnp.zeros((1, LANES), F32).at[0, :N_EXPERTS].set(router_b)
    r, before_f, counts_f = router(x, w_pad, b_pad, tm)
    top_idx = r[:, :TOP_K].astype(jnp.int32)
    n_slot = m * TOP_K
    onehot = (top_idx[:, :, None] == jnp.arange(N_EXPERTS, dtype=jnp.int32)).astype(jnp.int32)
    before = before_f[:, :N_EXPERTS].astype(jnp.int32)
    counts = counts_f[0, :N_EXPERTS].astype(jnp.int32)
    tiles_per = (counts + MOE_TILE - 1) // MOE_TILE
    tile_end = jnp.cumsum(tiles_per)
    row0 = (tile_end - tiles_per) * MOE_TILE
    dest = jnp.sum(onehot * (before + row0)[:, None, :], axis=2)
    n_tiles = -(-n_slot // MOE_TILE) + N_EXPERTS
    n_rows = n_tiles * MOE_TILE
    row_tok = (jnp.arange(n_rows, dtype=jnp.int32) % m).at[dest.reshape(-1)].set(
        jnp.arange(n_slot, dtype=jnp.int32) // TOP_K, unique_indices=True, mode='promise_in_bounds')
    n_used = tile_end[-1:].astype(jnp.int32)
    tile_ids = jnp.minimum(jnp.arange(n_tiles, dtype=jnp.int32), n_used[0] - 1)
    tile_expert = jnp.minimum(jnp.sum((tile_end[None, :] <= tile_ids[:, None]).astype(jnp.int32), axis=1),
                              N_EXPERTS - 1)
    tile_rows = jnp.clip(counts[tile_expert] - (tile_ids - (tile_end - tiles_per)[tile_expert]) * MOE_TILE,
                         0, MOE_TILE).astype(jnp.int32)
    x_sorted = x.at[row_tok].get(mode='promise_in_bounds')
    y_sorted = moe_ffn(x_sorted, tile_expert, n_used, tile_rows, wg, wu, wd, tf=512)
    ya = y_sorted.at[dest[:, 0]].get(mode='promise_in_bounds')
    yb = y_sorted.at[dest[:, 1]].get(mode='promise_in_bounds')
    return [combine_ln(x, r, ya, yb, g, b, t, blk0, rows) for blk0, rows, t in splits]


def kernel(x_prompt, x_sample, cache_sb_k, cache_sb_v, cache_mem_k, cache_mem_v, state_ml_C, state_ml_n, state_ml_m, state_rt_S, state_s5_re, state_s5_im, page_table, mem_prompt, w_in, sb_bias, ml_b_i, ml_b_f, ml_norm_g, rt_norm_g, s5_A_re, s5_A_im, s5_log_dt, s5_B_re, s5_B_im, s5_C_re, s5_C_im, s5_D, s5_glu_w, s5_glu_b, w_out, ca_wq, ca_wk, ca_wv, ca_wo, ln_g, ln_b, ffn_w_gate, ffn_w_up, ffn_w_down, moe_router_w, moe_router_b, moe_w_gate, moe_w_up, moe_w_down):
    bp, tp, _ = x_prompt.shape
    bs, ts, _ = x_sample.shape
    n_p, n_s = bp * tp, bs * ts
    tm = 640
    assert (n_p + n_s) % (2 * tm) == 0 and n_p % n_s == 0 and tp % 512 == 0 and bp % 8 == 0 and bs % 8 == 0
    x = jnp.concatenate([x_prompt.reshape(n_p, D_MODEL), x_sample.reshape(n_s, D_MODEL)], axis=0)
    uu = _suffix_matrix()
    g_off = 7 * GROUP_W
    half = HEAD_DIM // 2
    freq = ROPE_BASE ** (-jnp.arange(half, dtype=F32) / half)

    def rope_tables(pos):
        ang = pos.astype(F32)[:, None] * freq[None, :]
        cos, sin = jnp.cos(ang), jnp.sin(ang)
        return (jnp.tile(jnp.concatenate([cos, cos], axis=1), (1, N_HEADS)),
                jnp.tile(jnp.concatenate([-sin, sin], axis=1), (1, N_HEADS)))

    cos_p, sin_p = rope_tables(jnp.arange(tp, dtype=jnp.int32))
    cos_s, sin_s = rope_tables(PAST_LEN + jnp.arange(ts, dtype=jnp.int32))
    cache_kt = cache_sb_k.transpose(0, 1, 3, 4, 2)
    cache_vt = cache_sb_v.transpose(0, 1, 3, 4, 2)

    p_st = [[] for _ in range(10)]
    s_st = [[] for _ in range(8)]
    for l in range(DEPTH):
        wl = w_in[l]
        w_cat = jnp.concatenate([wl[:, :g_off], wl[:, g_off + 2 * N_HEADS:], wl[:, g_off:g_off + 2 * N_HEADS],
                                 jnp.zeros((D_MODEL, PROJ_W - wl.shape[1]), F32)], axis=1).astype(BF16)
        proj_p, kt_p, vt_p = in_proj_prompt(x, w_cat, bp, tp, 512)
        proj_s = linear(x, w_cat, n_s, n_p // n_s, n_s)
        proj_p3 = proj_p.reshape(bp, tp, PROJ_W)
        proj_s3 = proj_s.reshape(bs, ts, PROJ_W)
        gate_bias = jnp.zeros((1, LANES), F32).at[0, :2 * N_HEADS].set(jnp.concatenate([ml_b_i[l], ml_b_f[l]]))
        ml_g = ml_norm_g[l][None, :]
        rt_g = rt_norm_g[l][None, :]
        wb, a1, a2, wc = _s5_weights(s5_A_re[l], s5_A_im[l], s5_log_dt[l], s5_B_re[l], s5_B_im[l],
                                     s5_C_re[l], s5_C_im[l])
        s5_d = s5_D[l][None, :]
        glu_w = s5_glu_w[l].astype(BF16)
        glu_b = s5_glu_b[l][None, :]

        o_sb_p = sb_attention_prompt(proj_p, sb_bias[l], uu, bp, tp, tq=256)
        o_ml_p, ml_c_p, ml_n_p, ml_m_p = mlstm_mixer_prompt(proj_p3, gate_bias, ml_g, 8, MIX_DTYPE)
        o_rt_p, rs_p = retention_mixer_prompt(proj_p3, cos_p, sin_p, rt_g, 8, MIX_DTYPE)
        o_ml_p = o_ml_p.reshape(n_p, GROUP_W)
        o_rt_p = o_rt_p.reshape(n_p, GROUP_W)
        o_s5_p, h5_p = s5_mixer(proj_p3, 64, wb, a1, a2, jnp.zeros((bp, 2 * S5_W), F32), wc, s5_d, glu_w, glu_b,
                                MIX_DTYPE)
        o_s5_p = o_s5_p.reshape(n_p, GROUP_W)

        o_sb_s = sb_attention_sample(proj_s, 0, cache_kt, cache_vt, page_table, l, sb_bias[l], uu, n_pp=16)
        cn0 = jnp.concatenate([state_ml_C[:, l], state_ml_n[:, l][..., None],
                               jnp.zeros((bs, N_HEADS, HEAD_DIM, LANES - HEAD_DIM - 1), F32)], axis=-1)
        o_ml_s, cn_s, m_s = mlstm_mixer(proj_s3, gate_bias, ml_g, cn0, state_ml_m[:, l], 8, F32)
        o_rt_s, rs_s = retention_mixer(proj_s3, cos_s, sin_s, rt_g, state_rt_S[:, l], 8, F32)
        o_ml_s = o_ml_s.reshape(n_s, GROUP_W)
        o_rt_s = o_rt_s.reshape(n_s, GROUP_W)
        h0_s = jnp.concatenate([state_s5_re[:, l].reshape(bs, S5_W), state_s5_im[:, l].reshape(bs, S5_W)], axis=1)
        o_s5_s, h5_s = s5_mixer(proj_s3, ts, wb, a1, a2, h0_s, wc, s5_d, glu_w, glu_b, F32)
        o_s5_s = o_s5_s.reshape(n_s, GROUP_W)

        wo_mix = w_out[l].astype(BF16)
        g0, b0 = ln_g[l, 0][None, :], ln_b[l, 0][None, :]
        x = mix_out_ln(x, (o_sb_p, o_ml_p, o_rt_p, o_s5_p), wo_mix, g0, b0, 512, 0)
        x = mix_out_ln(x, (o_sb_s, o_ml_s, o_rt_s, o_s5_s), wo_mix, g0, b0, n_s, n_p // n_s)

        mem_kv = linear(mem_prompt.reshape(bp * N_MEM, D_MODEL),
                        jnp.concatenate([ca_wk[l], ca_wv[l]], axis=1).astype(BF16), 512)
        mk_p = mem_kv[:, :GROUP_W].reshape(bp, N_MEM, GROUP_W)
        mv_p = mem_kv[:, GROUP_W:].reshape(bp, N_MEM, GROUP_W)
        wq = ca_wq[l].astype(BF16)
        wo = ca_wo[l].astype(BF16)
        g1, b1 = ln_g[l, 1][None, :], ln_b[l, 1][None, :]
        x = cross_attn_ln(x, 0, bp, tp, 512, 1, mk_p, mv_p, wq, wo, g1, b1)
        x = cross_attn_ln(x, n_p // (8 * ts), bs, ts, 8 * ts, 8, cache_mem_k[:, l].reshape(bs, N_MEM, GROUP_W),
                          cache_mem_v[:, l].reshape(bs, N_MEM, GROUP_W), wq, wo, g1, b1)

        g2, b2 = ln_g[l, 2][None, :], ln_b[l, 2][None, :]
        j = l // 2
        last = l == DEPTH - 1
        if l % 2 == 0:
            x = ffn_ln(x, ffn_w_gate[j], ffn_w_up[j], ffn_w_down[j], g2, b2, 2 * tm, tf=512)
            y_out = (x[:n_p], x[n_p:]) if last else None
        else:
            splits = [(0, n_p, 512), (n_p // n_s, n_s, n_s)] if last else [(0, n_p + n_s, tm)]
            y_out = moe_ln(x, moe_router_w[j], moe_router_b[j], moe_w_gate[j], moe_w_up[j], moe_w_down[j],
                           g2, b2, tm, splits)
            x = None if last else y_out[0]

        def heads(a, nb_, t_):
            return a.reshape(nb_, t_, N_HEADS, HEAD_DIM)

        p_st[0].append(kt_p)
        p_st[1].append(vt_p)
        p_st[2].append(heads(mk_p, bp, N_MEM))
        p_st[3].append(heads(mv_p, bp, N_MEM))
        s_st[0].append(heads(proj_s[:, C_SK * GROUP_W:(C_SK + 1) * GROUP_W], bs, ts))
        s_st[1].append(heads(proj_s[:, C_SV * GROUP_W:(C_SV + 1) * GROUP_W], bs, ts))
        ml_p = (ml_c_p, ml_n_p, ml_m_p)
        ml_s = (cn_s[..., :HEAD_DIM], cn_s[..., HEAD_DIM], m_s[:, 0, :N_HEADS])
        for st, ml, rs, h5, nb_ in ((p_st, ml_p, rs_p, h5_p, bp), (s_st, ml_s, rs_s, h5_s, bs)):
            off = 4 if st is p_st else 2
            st[off + 0].append(ml[0])
            st[off + 1].append(ml[1])
            st[off + 2].append(ml[2])
            st[off + 3].append(rs)
            st[off + 4].append(h5[:, :S5_W].reshape(nb_, S5_GROUPS, S5_STATE))
            st[off + 5].append(h5[:, S5_W:].reshape(nb_, S5_GROUPS, S5_STATE))

    y_prompt = y_out[0].reshape(bp, tp, D_MODEL)
    y_sample = y_out[1].reshape(bs, ts, D_MODEL)
    p_out = [jnp.stack(a, axis=1) for a in p_st]
    for i in range(2):
        p_out[i] = p_out[i].reshape(bp, DEPTH, N_HEADS, HEAD_DIM, tp).transpose(0, 1, 4, 2, 3)
    s_out = [jnp.stack(a, axis=1) for a in s_st]
    return (y_prompt, y_sample, *p_out, *s_out)
```

```python
import functools
import math

import numpy as np
import jax
import jax.numpy as jnp
from jax import lax
from jax.experimental import pallas as pl
from jax.experimental.pallas import tpu as pltpu

F32 = jnp.float32
BF16 = jnp.bfloat16

D_MODEL = 1024
DEPTH = 2
PAST_LEN = 8192
PAGE_SIZE = 128
HEAD_DIM = 64
N_HEADS = 4
GROUP_W = N_HEADS * HEAD_DIM
S5_GROUPS = 16
S5_GROUP = 16
S5_STATE = 64
S5_W = S5_GROUPS * S5_STATE
N_MEM = 256
D_FF = 3584
N_EXPERTS = 8
TOP_K = 2
CHUNK = 64
ROPE_BASE = 10000.0
LN_EPS = 1e-5
GN_EPS = 1e-6
ALPHA = (2 * DEPTH) ** 0.25
QK_SCALE = HEAD_DIM ** -0.5
LOG2E = math.log2(math.e)

LANES = 128
PROJ_W = 25 * LANES
C_SQ, C_SK, C_SV, C_MQ, C_MK, C_MV, C_MO, C_RQ, C_RK, C_RV, C_RG, C_SU = range(12)
C_GATES = 12 * GROUP_W // LANES
VMEM_LIMIT = 48 * 1024 * 1024
ROW_TILE = 640
PROMPT_TILE = 512
FFN_ROW_TILE = 2 * ROW_TILE
FFN_COL_TILE = 512
MOE_TILE = 1024
SB_QUERY_TILE = 256
SB_PAGES_PER_STEP = 16
SEQS_PER_STEP = 8
S5_STEPS = CHUNK
MIX_DTYPE = BF16


def _cp(*sem):
    return pltpu.CompilerParams(dimension_semantics=sem, vmem_limit_bytes=VMEM_LIMIT)


def _dot(a, b):
    return jnp.dot(a, b, preferred_element_type=F32)


def _dot_nt(a, b):
    return lax.dot_general(a, b, (((1,), (1,)), ((), ())), preferred_element_type=F32)


def _dot_tn(a, b):
    return lax.dot_general(a, b, (((0,), (0,)), ((), ())), preferred_element_type=F32)


def _layer_norm(y, g, b):
    mu = jnp.mean(y, axis=-1, keepdims=True)
    yc = y - mu
    var = jnp.mean(yc * yc, axis=-1, keepdims=True)
    return yc * lax.rsqrt(var + LN_EPS) * g + b


def _row_sum(x, scale=1.0):
    ones = jnp.full((x.shape[1], LANES), scale, BF16)
    hi = x.astype(BF16)
    lo = (x - hi.astype(F32)).astype(BF16)
    return (_dot(hi, ones) + _dot(lo, ones))[:, :x.shape[1]]


def _cumsum_rows(tril, x):
    hi = x.astype(BF16)
    lo = (x - hi.astype(F32)).astype(BF16)
    return _dot(tril, hi) + _dot(tril, lo)


def _head_norm_all(hd):
    inv = 1.0 / HEAD_DIM
    mu = {p: _row_sum(x, inv) for p, x in hd.items()}
    hc = {p: hd[p] - mu[p] for p in hd}
    var = {p: _row_sum(hc[p] * hc[p], inv) for p in hd}
    return {p: hc[p] * lax.rsqrt(var[p] + GN_EPS) for p in hd}


def _neg_softplus(z):
    return -(jnp.maximum(z, 0.0) + jnp.log1p(jnp.exp(-jnp.abs(z))))


def _log_sigmoid(z):
    return _neg_softplus(-z)


def _linear_kernel(x_ref, w_ref, o_ref):
    o_ref[...] = _dot(x_ref[...].astype(BF16), w_ref[...]).astype(o_ref.dtype)


def linear(x, w, tm, row_blk0=0, n_rows=None, out_dtype=F32):
    m, k = x.shape
    m = m if n_rows is None else n_rows
    n = w.shape[1]
    return pl.pallas_call(
        _linear_kernel, grid=(m // tm,),
        in_specs=[pl.BlockSpec((tm, k), lambda i: (row_blk0 + i, 0)), pl.BlockSpec((k, n), lambda i: (0, 0))],
        out_specs=pl.BlockSpec((tm, n), lambda i: (i, 0)),
        out_shape=jax.ShapeDtypeStruct((m, n), out_dtype),
        compiler_params=_cp("arbitrary"), name="linear")(x, w)


def _in_proj_kernel(x_ref, w_ref, o_ref, kt_ref, vt_ref):
    o = _dot(x_ref[...].astype(BF16), w_ref[...])
    o_ref[...] = o
    kt_ref[0] = o[:, C_SK * GROUP_W:(C_SK + 1) * GROUP_W].T
    vt_ref[0] = o[:, C_SV * GROUP_W:(C_SV + 1) * GROUP_W].T


def in_proj_prompt(x, w, n_batch, seq, tm):
    k = x.shape[1]
    n = w.shape[1]
    nt = seq // tm
    t_spec = pl.BlockSpec((1, GROUP_W, tm), lambda i: (i // nt, 0, i % nt))
    t_shape = jax.ShapeDtypeStruct((n_batch, GROUP_W, seq), F32)
    return pl.pallas_call(
        _in_proj_kernel, grid=(n_batch * nt,),
        in_specs=[pl.BlockSpec((tm, k), lambda i: (i, 0)), pl.BlockSpec((k, n), lambda i: (0, 0))],
        out_specs=[pl.BlockSpec((tm, n), lambda i: (i, 0)), t_spec, t_spec],
        out_shape=[jax.ShapeDtypeStruct((n_batch * seq, n), F32), t_shape, t_shape],
        compiler_params=_cp("arbitrary"), name="in_proj")(x, w)


def _mix_out_ln_kernel(x_ref, a_ref, b_ref, c_ref, d_ref, w_ref, g_ref, bias_ref, o_ref):
    h = sum(_dot(p[...].astype(BF16), w_ref[pl.ds(n * GROUP_W, GROUP_W), :])
            for n, p in enumerate((a_ref, b_ref, c_ref, d_ref)))
    o_ref[...] = _layer_norm(ALPHA * x_ref[...] + h, g_ref[...], bias_ref[...])


def mix_out_ln(x, parts, w, g, b, tm, row_blk0):
    rows = parts[0].shape[0]
    row_spec = pl.BlockSpec((tm, D_MODEL), lambda i: (row_blk0 + i, 0))
    part_spec = pl.BlockSpec((tm, GROUP_W), lambda i: (i, 0))
    vec = pl.BlockSpec((1, D_MODEL), lambda i: (0, 0))
    return pl.pallas_call(
        _mix_out_ln_kernel, grid=(rows // tm,),
        in_specs=[row_spec, part_spec, part_spec, part_spec, part_spec,
                  pl.BlockSpec((D_MODEL, D_MODEL), lambda i: (0, 0)), vec, vec],
        out_specs=row_spec, out_shape=jax.ShapeDtypeStruct(x.shape, F32), input_output_aliases={0: 0},
        compiler_params=_cp("arbitrary"), name="mix_out_ln")(x, *parts, w, g, b)


def _suffix_matrix():
    j = np.arange(LANES)
    u = (j[:, None] >= j[None, :]).astype(np.float32)
    uu = np.concatenate([u, np.ones((LANES, LANES), np.float32)], axis=1)
    return jnp.asarray(np.concatenate([uu, uu], axis=0), dtype=BF16)


def _suffix_sums(lr, uu):
    hi = lr.astype(BF16)
    lo = (lr - hi.astype(F32)).astype(BF16)
    r = _dot(jnp.concatenate([hi, lo], axis=1), uu)
    return r[:, :LANES], r[:, LANES:]


def _log2_rem(z2):
    return jnp.minimum(-z2, 0.0) - jnp.log2(1.0 + jnp.exp2(-jnp.abs(z2)))


def _sb_prompt_kernel(bias_ref, q_ref, k_ref, v_ref, uu_ref, o_ref, acc_ref, car_ref, kb_ref, vb_ref, *, tq):
    i = pl.program_id(1)
    tk = LANES
    nsub = tq // tk
    acc_ref[...] = jnp.zeros_like(acc_ref)
    car_ref[...] = jnp.zeros_like(car_ref)

    @pl.when(i == 0)
    def _():
        for h in range(N_HEADS):
            kb_ref[h] = k_ref[:, h * HEAD_DIM:(h + 1) * HEAD_DIM].astype(BF16)
            vb_ref[h] = v_ref[:, h * HEAD_DIM:(h + 1) * HEAD_DIM].astype(BF16)

    q = (q_ref[...] * (QK_SCALE * LOG2E)).astype(BF16)
    qh = [q[:, h * HEAD_DIM:(h + 1) * HEAD_DIM] for h in range(N_HEADS)]
    b2 = [bias_ref[h] * LOG2E for h in range(N_HEADS)]
    uu = uu_ref[...]
    row = lax.broadcasted_iota(jnp.int32, (tq, tk), 0)
    col = lax.broadcasted_iota(jnp.int32, (tq, tk), 1)
    heads = range(N_HEADS)

    def block_pair(j_hi, causal_hi, causal_lo):
        r0 = [pl.multiple_of((j_hi - d) * tk, tk) for d in range(2)]
        z2 = [[_dot_nt(qh[h], kb_ref[h, pl.ds(r0[d], tk), :]) + b2[h] for h in heads] for d in range(2)]
        cs, tot = [], []
        for d, causal in enumerate((causal_hi, causal_lo)):
            lr = [_log2_rem(z) for z in z2[d]]
            if causal is not None:
                lr = [jnp.where(causal, a, 0.0) for a in lr]
            c, t = _suffix_sums(jnp.concatenate(lr, axis=0), uu)
            cs.append(c)
            tot.append(t)
        car = [car_ref[h] for h in heads]
        pv = []
        for d, causal in enumerate((causal_hi, causal_lo)):
            w = [jnp.exp2(z2[d][h] + cs[d][h * tq:(h + 1) * tq] + car[h]) for h in heads]
            if causal is not None:
                w = [jnp.where(causal, a, 0.0) for a in w]
            pv.append([_dot(w[h].astype(BF16), vb_ref[h, pl.ds(r0[d], tk), :]) for h in heads])
            car = [car[h] + tot[d][h * tq:(h + 1) * tq] for h in heads]
        for h in heads:
            acc_ref[h] += pv[0][h] + pv[1][h]
            car_ref[h] = car[h]

    assert nsub == 2
    block_pair(i * nsub + 1, (col + tk) < row, col < row)

    def body(jj, carry):
        block_pair(i * nsub - 1 - 2 * jj, None, None)
        return carry

    lax.fori_loop(0, i, body, 0)
    o_ref[...] = jnp.concatenate([acc_ref[h] for h in range(N_HEADS)], axis=1).astype(o_ref.dtype)


def sb_attention_prompt(proj, sb_bias, uu, n_batch, seq, tq):
    nq = seq // tq
    kern = functools.partial(_sb_prompt_kernel, tq=tq)
    return pl.pallas_call(
        kern, grid=(n_batch, nq),
        in_specs=[pl.BlockSpec(memory_space=pltpu.SMEM),
                  pl.BlockSpec((tq, GROUP_W), lambda b, i: (b * nq + i, C_SQ)),
                  pl.BlockSpec((seq, GROUP_W), lambda b, i: (b, C_SK)),
                  pl.BlockSpec((seq, GROUP_W), lambda b, i: (b, C_SV)),
                  pl.BlockSpec((2 * LANES, 2 * LANES), lambda b, i: (0, 0))],
        out_specs=pl.BlockSpec((tq, GROUP_W), lambda b, i: (b * nq + i, 0)),
        out_shape=jax.ShapeDtypeStruct((n_batch * seq, GROUP_W), MIX_DTYPE),
        scratch_shapes=[pltpu.VMEM((N_HEADS, tq, HEAD_DIM), F32), pltpu.VMEM((N_HEADS, tq, LANES), F32),
                        pltpu.VMEM((N_HEADS, seq, HEAD_DIM), BF16), pltpu.VMEM((N_HEADS, seq, HEAD_DIM), BF16)],
        compiler_params=_cp("arbitrary", "arbitrary"), name="sb_prompt")(sb_bias, proj, proj, proj, uu)


def _sb_sample_kernel(pt_ref, bias_ref, q_ref, kn_ref, vn_ref, u8_ref, uu_ref, *rest, n_pp, n_steps):
    k_refs = rest[:n_pp]
    v_refs = rest[n_pp:2 * n_pp]
    o_ref = rest[2 * n_pp]
    acc_ref, car_ref = rest[2 * n_pp + 1:]
    s = pl.program_id(1)
    nq = q_ref.shape[0]
    rows = N_HEADS * nq
    q = (q_ref[...] * (QK_SCALE * LOG2E)).astype(BF16)
    row_head = lax.broadcasted_iota(jnp.int32, (rows, GROUP_W), 0) // nq
    own = row_head == lax.broadcasted_iota(jnp.int32, (rows, GROUP_W), 1) // HEAD_DIM
    q_bd = jnp.where(own, jnp.concatenate([q] * N_HEADS, axis=0), jnp.zeros((), BF16))
    row_head_l = lax.broadcasted_iota(jnp.int32, (rows, LANES), 0) // nq
    b2 = jnp.zeros((rows, LANES), F32)
    for h in range(N_HEADS):
        b2 = jnp.where(row_head_l == h, bias_ref[h] * LOG2E, b2)

    @pl.when(s == 0)
    def _():
        kn = kn_ref[...].astype(BF16)
        vn = vn_ref[...].astype(BF16)
        t = lax.broadcasted_iota(jnp.int32, (rows, nq), 0) % nq
        causal = lax.broadcasted_iota(jnp.int32, (rows, nq), 1) < t
        z2 = _dot_nt(q_bd, kn) + b2[:, :nq]
        lr = jnp.where(causal, _log2_rem(z2), 0.0)
        cs = jnp.dot(lr, u8_ref[...], preferred_element_type=F32, precision=lax.Precision.HIGHEST)
        w = jnp.where(causal, jnp.exp2(z2 + cs), 0.0)
        acc_ref[...] = _dot(w.astype(BF16), vn)
        car_ref[...] = jnp.broadcast_to(jnp.sum(lr, axis=1, keepdims=True), (rows, LANES))

    z2s = [_dot(q_bd, k_refs[p][0, 0].reshape(GROUP_W, PAGE_SIZE).astype(BF16)) + b2 for p in range(n_pp)]
    cs_all, tot_all = _suffix_sums(_log2_rem(jnp.concatenate(z2s, axis=0)), uu_ref[...])
    car = car_ref[...]
    acc = acc_ref[...]
    for p in range(n_pp):
        w = jnp.exp2(z2s[p] + cs_all[p * rows:(p + 1) * rows] + car)
        acc = acc + _dot_nt(w.astype(BF16), v_refs[p][0, 0].reshape(GROUP_W, PAGE_SIZE).astype(BF16))
        car = car + tot_all[p * rows:(p + 1) * rows]
    car_ref[...] = car
    acc_ref[...] = acc

    @pl.when(s == n_steps - 1)
    def _():
        kept = jnp.where(own, acc, 0.0)
        o_ref[...] = sum(kept[h * nq:(h + 1) * nq] for h in range(N_HEADS))


def sb_attention_sample(proj, row_blk0, nq, cache_kt, cache_vt, page_table, layer, sb_bias, uu, n_pp):
    n_batch, n_pages = page_table.shape
    n_steps = n_pages // n_pp
    u8 = jnp.asarray((np.arange(nq)[:, None] >= np.arange(nq)[None, :]).astype(np.float32))

    def page_spec(p):
        return pl.BlockSpec((1, 1, N_HEADS, HEAD_DIM, PAGE_SIZE),
                            lambda b, s, pt: (pt[b, n_pages - 1 - (s * n_pp + p)], layer, 0, 0, 0))

    def row_spec(cblk):
        return pl.BlockSpec((nq, GROUP_W), lambda b, s, pt: (row_blk0 + b, cblk))

    kern = functools.partial(_sb_sample_kernel, n_pp=n_pp, n_steps=n_steps)
    gs = pltpu.PrefetchScalarGridSpec(
        num_scalar_prefetch=1, grid=(n_batch, n_steps),
        in_specs=[pl.BlockSpec(memory_space=pltpu.SMEM), row_spec(C_SQ), row_spec(C_SK), row_spec(C_SV),
                  pl.BlockSpec((nq, nq), lambda b, s, pt: (0, 0)),
                  pl.BlockSpec((2 * LANES, 2 * LANES), lambda b, s, pt: (0, 0))]
                 + [page_spec(p) for p in range(n_pp)] * 2,
        out_specs=pl.BlockSpec((nq, GROUP_W), lambda b, s, pt: (b, 0)),
        scratch_shapes=[pltpu.VMEM((N_HEADS * nq, GROUP_W), F32), pltpu.VMEM((N_HEADS * nq, LANES), F32)])
    return pl.pallas_call(
        kern, grid_spec=gs, out_shape=jax.ShapeDtypeStruct((n_batch * nq, GROUP_W), F32),
        compiler_params=_cp("arbitrary", "arbitrary"), name="sb_sample")(
            page_table, sb_bias, proj, proj, proj, u8, uu, *([cache_kt] * n_pp), *([cache_vt] * n_pp))


def _mlstm_kernel(m0_ref, q_ref, k_ref, v_ref, og_ref, gt_ref, gb_ref, ng_ref, tril_ref, cn0_ref,
                  o_ref, cn_ref, m_ref, ms_ref, *, bb, chunk, n_t):
    bi = pl.program_id(0)
    t = pl.program_id(1)
    seqs = range(bb)
    pairs = [(b, h) for b in seqs for h in range(N_HEADS)]

    @pl.when(t == 0)
    def _():
        cn_ref[...] = cn0_ref[...]
        for b, h in pairs:
            ms_ref[b * N_HEADS + h] = jnp.full((1, LANES), m0_ref[bi * bb + b, h], F32)

    tril = tril_ref[...]
    tri_mask = lax.broadcasted_iota(jnp.int32, (chunk, chunk), 1) <= lax.broadcasted_iota(jnp.int32, (chunk, chunk), 0)
    lane = lax.broadcasted_iota(jnp.int32, (chunk, HEAD_DIM), 1)
    ones_col = jnp.where(lane == 0, 1.0, 0.0).astype(F32)
    hs = lambda h: slice(h * HEAD_DIM, (h + 1) * HEAD_DIM)

    gt = [gt_ref[b] + gb_ref[...] for b in seqs]
    bc = [_cumsum_rows(tril, _log_sigmoid(g)) for g in gt]
    gt_t = [g.T for g in gt]
    bc_t = [x.T for x in bc]
    q = [q_ref[b].astype(BF16) for b in seqs]
    k = [(k_ref[b] * QK_SCALE).astype(BF16) for b in seqs]
    v = [v_ref[b] for b in seqs]
    m_prev = {p: ms_ref[p[0] * N_HEADS + p[1]][:, :1] for p in pairs}
    ig_col = {(b, h): gt[b][:, h:h + 1] for b, h in pairs}
    bc_col = {(b, h): bc[b][:, N_HEADS + h:N_HEADS + h + 1] for b, h in pairs}
    dm = {(b, h): jnp.where(tri_mask, bc_col[b, h] - (bc_t[b][N_HEADS + h:N_HEADS + h + 1, :] - gt_t[b][h:h + 1, :]),
                            -jnp.inf) for b, h in pairs}
    a = {p: bc_col[p] + m_prev[p] for p in pairs}
    m_new = {p: jnp.maximum(a[p], jnp.max(dm[p], axis=1, keepdims=True)) for p in pairs}
    inter = {p: jnp.exp(a[p] - m_new[p]) for p in pairs}
    s = {(b, h): _dot_nt(q[b][:, hs(h)], k[b][:, hs(h)]) * jnp.exp(dm[b, h] - m_new[b, h]) for b, h in pairs}
    v_ext = {(b, h): jnp.concatenate([v[b][:, hs(h)], ones_col], axis=1) for b, h in pairs}
    qc = {(b, h): _dot(q[b][:, hs(h)], cn_ref[b, h].astype(BF16)) for b, h in pairs}
    sv = {p: _dot(s[p].astype(BF16), v_ext[p].astype(BF16)) for p in pairs}
    m_last = {p: m_new[p][chunk - 1:chunk, :] for p in pairs}
    wl = {p: jnp.exp(bc_col[p][chunk - 1:chunk, :] - bc_col[p] + ig_col[p] - m_last[p]) for p in pairs}
    dl = {p: jnp.exp(a[p][chunk - 1:chunk, :] - m_last[p]) for p in pairs}
    upd = {(b, h): _dot_tn(k[b][:, hs(h)], (wl[b, h] * v_ext[b, h]).astype(BF16)) for b, h in pairs}
    rs = {p: _row_sum(s[p])[:, :1] for p in pairs}
    num = {p: inter[p] * qc[p][:, :HEAD_DIM] + sv[p][:, :HEAD_DIM] for p in pairs}
    den = {p: inter[p] * qc[p][:, HEAD_DIM:HEAD_DIM + 1] + rs[p] for p in pairs}
    hh = _head_norm_all({p: num[p] / jnp.maximum(jnp.abs(den[p]), jnp.exp(-m_new[p])) for p in pairs})
    for b, h in pairs:
        cn_ref[b, h] = dl[b, h] * cn_ref[b, h] + upd[b, h]
        ms_ref[b * N_HEADS + h] = jnp.broadcast_to(m_last[b, h], (1, LANES))
    for b in seqs:
        y = jnp.concatenate([hh[b, h] for h in range(N_HEADS)], axis=1)
        o_ref[b] = (y * ng_ref[...] * jax.nn.sigmoid(og_ref[b])).astype(o_ref.dtype)

    @pl.when(t == n_t - 1)
    def _():
        lane_m = lax.broadcasted_iota(jnp.int32, (1, LANES), 1)
        for b in seqs:
            m_out = jnp.zeros((1, LANES), F32)
            for h in range(N_HEADS):
                m_out = jnp.where(lane_m == h, ms_ref[b * N_HEADS + h], m_out)
            m_ref[b] = m_out


def mlstm_mixer(proj3, gate_bias, norm_g, cn0, m0, bb, out_dtype):
    n_batch, seq, _ = proj3.shape
    chunk = math.gcd(seq, CHUNK)
    n_t = seq // chunk
    tril = jnp.asarray(np.tril(np.ones((chunk, chunk), np.float32)), dtype=BF16)

    def row_spec(cblk, w=GROUP_W):
        return pl.BlockSpec((bb, chunk, w), lambda bi, t: (bi, t, cblk))

    def const_spec(shape):
        return pl.BlockSpec(shape, lambda bi, t: (0,) * len(shape))

    state_spec = pl.BlockSpec((bb, N_HEADS, HEAD_DIM, LANES), lambda bi, t: (bi, 0, 0, 0))
    kern = functools.partial(_mlstm_kernel, bb=bb, chunk=chunk, n_t=n_t)
    return pl.pallas_call(
        kern, grid=(n_batch // bb, n_t),
        in_specs=[pl.BlockSpec(memory_space=pltpu.SMEM),
                  row_spec(C_MQ), row_spec(C_MK), row_spec(C_MV), row_spec(C_MO), row_spec(C_GATES, LANES),
                  const_spec((1, LANES)), const_spec((1, GROUP_W)), const_spec((chunk, chunk)), state_spec],
        out_specs=[pl.BlockSpec((bb, chunk, GROUP_W), lambda bi, t: (bi, t, 0)), state_spec,
                   pl.BlockSpec((bb, 1, LANES), lambda bi, t: (bi, 0, 0))],
        out_shape=[jax.ShapeDtypeStruct((n_batch, seq, GROUP_W), out_dtype),
                   jax.ShapeDtypeStruct((n_batch, N_HEADS, HEAD_DIM, LANES), F32),
                   jax.ShapeDtypeStruct((n_batch, 1, LANES), F32)],
        scratch_shapes=[pltpu.VMEM((bb * N_HEADS, 1, LANES), F32)],
        compiler_params=_cp("arbitrary", "arbitrary"), name="mlstm")(
            m0, proj3, proj3, proj3, proj3, proj3, gate_bias, norm_g, tril, cn0)


def _split_dot(x, w):
    hi = x.astype(BF16)
    lo = (x - hi.astype(F32)).astype(BF16)
    return _dot(hi, w) + _dot(lo, w)


def _mlstm_pair_kernel(q_ref, k_ref, v_ref, og_ref, gt_ref, gb_ref, ng_ref, tril_ref, sel_ref, blk_ref,
                       o_ref, cn_ref, st_ref, *, bb, chunk):
    t_id = pl.program_id(1)

    @pl.when(t_id == 0)
    def _():
        cn_ref[...] = jnp.zeros_like(cn_ref)
        st_ref[...] = jnp.zeros_like(st_ref)

    seqs = range(bb)
    n_pair = N_HEADS // 2
    chains = [(b, p) for b in seqs for p in range(n_pair)]
    lane = lax.broadcasted_iota(jnp.int32, (chunk, LANES), 1)
    row = lax.broadcasted_iota(jnp.int32, (chunk, LANES), 0)
    low = lane < HEAD_DIM
    causal = (lane % HEAD_DIM) <= row
    eye2 = (lane % HEAD_DIM) == row
    blk = blk_ref[...]
    same = (lax.broadcasted_iota(jnp.int32, (LANES, LANES), 0) // HEAD_DIM
            == lax.broadcasted_iota(jnp.int32, (LANES, LANES), 1) // HEAD_DIM)
    ones_ll = jnp.ones((chunk, chunk), BF16)
    tril = tril_ref[...]
    pl_ = lambda p: slice(p * LANES, (p + 1) * LANES)
    zero_b = jnp.zeros((), BF16)

    gt = [gt_ref[b] + gb_ref[...] for b in seqs]
    bc = [_cumsum_rows(tril, _log_sigmoid(g)) for g in gt]
    def stacked(fn, xs):
        keys = list(xs)
        y = fn(jnp.concatenate([xs[c] for c in keys], axis=0))
        return {c: y[i * chunk:(i + 1) * chunk] for i, c in enumerate(keys)}

    by_blk = lambda x: _split_dot(x, blk)
    ig_all = stacked(lambda x: _split_dot(x, sel_ref[0]), dict(enumerate(gt)))
    bc_all = stacked(lambda x: _split_dot(x, sel_ref[1]), dict(enumerate(bc)))
    igc = {(b, p): ig_all[b][:, pl_(p)] for b, p in chains}
    bcc = {(b, p): bc_all[b][:, pl_(p)] for b, p in chains}
    gc = {c: igc[c] - bcc[c] for c in chains}
    grow = {c: _split_dot_left(ones_ll, jnp.where(eye2, gc[c], 0.0)) for c in chains}
    gmax = {c: jnp.max(gc[c], axis=0, keepdims=True) for c in chains}
    m_prev = {(b, p): st_ref[b, p, 1:2, :] for b, p in chains}
    a = {c: bcc[c] + m_prev[c] for c in chains}
    m_stab = {c: jnp.maximum(a[c], bcc[c] + gmax[c]) for c in chains}
    inter = {c: jnp.exp(a[c] - m_stab[c]) for c in chains}
    dw = {c: jnp.exp(jnp.where(causal, bcc[c] + grow[c], -jnp.inf) - m_stab[c]) for c in chains}
    q2 = {(b, p): q_ref[b, :, pl_(p)].astype(BF16) for b, p in chains}
    k2f = {(b, p): k_ref[b, :, pl_(p)] * QK_SCALE for b, p in chains}
    k2 = {c: k2f[c].astype(BF16) for c in chains}
    v2 = {(b, p): v_ref[b, :, pl_(p)] for b, p in chains}
    kbd = {c: jnp.concatenate([jnp.where(low, k2[c], zero_b), jnp.where(low, zero_b, k2[c])], axis=0) for c in chains}
    v2b = {c: v2[c].astype(BF16) for c in chains}
    vbd = {c: jnp.concatenate([jnp.where(low, v2b[c], zero_b), jnp.where(low, zero_b, v2b[c])], axis=0)
           for c in chains}
    s = {c: _dot_nt(q2[c], kbd[c]) * dw[c] for c in chains}
    rs = stacked(by_blk, s)
    sv = {c: _dot(s[c].astype(BF16), vbd[c]) for c in chains}
    qc = {(b, p): _dot(q2[b, p], cn_ref[b, p].astype(BF16)) for b, p in chains}
    n_row = {(b, p): st_ref[b, p, 0:1, :] for b, p in chains}
    qn = stacked(by_blk, {c: q2[c].astype(F32) * n_row[c] for c in chains})
    hh = {}
    for c in chains:
        num = inter[c] * qc[c] + sv[c]
        den = inter[c] * qn[c] + rs[c]
        hh[c] = num / jnp.maximum(jnp.abs(den), jnp.exp(-m_stab[c]))
    inv = 1.0 / HEAD_DIM
    mu = stacked(by_blk, hh)
    hc = {c: hh[c] - mu[c] * inv for c in chains}
    var = stacked(by_blk, {c: hc[c] * hc[c] for c in chains})
    m_last = {c: m_stab[c][chunk - 1:chunk, :] for c in chains}
    wl = {c: jnp.exp(bcc[c][chunk - 1:chunk, :] - bcc[c] + igc[c] - m_last[c]) for c in chains}
    dl = {c: jnp.exp(a[c][chunk - 1:chunk, :] - m_last[c]) for c in chains}
    upd = {c: _dot_tn(k2[c], (wl[c] * v2[c]).astype(BF16)) for c in chains}
    for b, p in chains:
        c = (b, p)
        y = hc[c] * lax.rsqrt(var[c] * inv + GN_EPS)
        o_ref[b, :, pl_(p)] = (y * ng_ref[:, pl_(p)] * jax.nn.sigmoid(og_ref[b, :, pl_(p)])).astype(o_ref.dtype)
        cn_ref[b, p] = dl[c] * cn_ref[b, p] + jnp.where(same, upd[c], 0.0)
        st_ref[b, p, 0:1, :] = dl[c] * n_row[c] + jnp.sum(wl[c] * k2f[c], axis=0, keepdims=True)
        st_ref[b, p, 1:2, :] = m_last[c]


def _split_dot_left(w, x):
    hi = x.astype(BF16)
    lo = (x - hi.astype(F32)).astype(BF16)
    return _dot(w, hi) + _dot(w, lo)


def mlstm_mixer_prompt(proj3, gate_bias, norm_g, bb, out_dtype):
    n_batch, seq, _ = proj3.shape
    chunk = CHUNK
    n_pair = N_HEADS // 2
    tril = jnp.asarray(np.tril(np.ones((chunk, chunk), np.float32)), dtype=BF16)
    sel = np.zeros((2, LANES, GROUP_W), np.float32)
    for h in range(N_HEADS):
        sel[0, h, h * HEAD_DIM:(h + 1) * HEAD_DIM] = 1.0
        sel[1, N_HEADS + h, h * HEAD_DIM:(h + 1) * HEAD_DIM] = 1.0
    idx = np.arange(LANES) // HEAD_DIM
    blk = (idx[:, None] == idx[None, :]).astype(np.float32)

    def row_spec(cblk, w=GROUP_W):
        return pl.BlockSpec((bb, chunk, w), lambda bi, t: (bi, t, cblk))

    def const_spec(shape):
        return pl.BlockSpec(shape, lambda bi, t: (0,) * len(shape))

    cn_spec = pl.BlockSpec((bb, n_pair, LANES, LANES), lambda bi, t: (bi, 0, 0, 0))
    st_spec = pl.BlockSpec((bb, n_pair, 8, LANES), lambda bi, t: (bi, 0, 0, 0))
    kern = functools.partial(_mlstm_pair_kernel, bb=bb, chunk=chunk)
    o, cn, st = pl.pallas_call(
        kern, grid=(n_batch // bb, seq // chunk),
        in_specs=[row_spec(C_MQ), row_spec(C_MK), row_spec(C_MV), row_spec(C_MO), row_spec(C_GATES, LANES),
                  const_spec((1, LANES)), const_spec((1, GROUP_W)), const_spec((chunk, chunk)),
                  const_spec(sel.shape), const_spec(blk.shape)],
        out_specs=[pl.BlockSpec((bb, chunk, GROUP_W), lambda bi, t: (bi, t, 0)), cn_spec, st_spec],
        out_shape=[jax.ShapeDtypeStruct((n_batch, seq, GROUP_W), out_dtype),
                   jax.ShapeDtypeStruct((n_batch, n_pair, LANES, LANES), F32),
                   jax.ShapeDtypeStruct((n_batch, n_pair, 8, LANES), F32)],
        compiler_params=_cp("arbitrary", "arbitrary"), name="mlstm_prompt")(
            proj3, proj3, proj3, proj3, proj3, gate_bias, norm_g, tril,
            jnp.asarray(sel, dtype=BF16), jnp.asarray(blk, dtype=BF16))
    c5 = cn.reshape(n_batch, n_pair, 2, HEAD_DIM, 2, HEAD_DIM)
    c_state = jnp.stack([c5[:, :, j, :, j, :] for j in range(2)], axis=2).reshape(n_batch, N_HEADS, HEAD_DIM, HEAD_DIM)
    n_state = st[:, :, 0, :].reshape(n_batch, N_HEADS, HEAD_DIM)
    m_state = st[:, :, 1, :].reshape(n_batch, N_HEADS, HEAD_DIM)[:, :, 0]
    return o, c_state, n_state, m_state


def _rope(x, cos, sin_signed):
    lane = lax.broadcasted_iota(jnp.int32, x.shape, 1)
    half = HEAD_DIM // 2
    swapped = jnp.where((lane % HEAD_DIM) < half, pltpu.roll(x, x.shape[1] - half, 1), pltpu.roll(x, half, 1))
    return x * cos + swapped * sin_signed


def _retention_kernel(q_ref, k_ref, v_ref, gg_ref, cos_ref, sin_ref, ng_ref, dec_ref, int_ref, wl_ref, dl_ref,
                      s0_ref, o_ref, s_ref, *, bb):
    t = pl.program_id(1)
    seqs = range(bb)
    pairs = [(b, h) for b in seqs for h in range(N_HEADS)]
    hs = lambda h: slice(h * HEAD_DIM, (h + 1) * HEAD_DIM)

    @pl.when(t == 0)
    def _():
        s_ref[...] = s0_ref[...]

    cos = cos_ref[...]
    sin = sin_ref[...]
    q = [_rope(q_ref[b], cos, sin).astype(BF16) for b in seqs]
    k = [(_rope(k_ref[b], cos, sin) * QK_SCALE).astype(BF16) for b in seqs]
    v = [v_ref[b] for b in seqs]
    s = {(b, h): _dot_nt(q[b][:, hs(h)], k[b][:, hs(h)]) * dec_ref[h] for b, h in pairs}
    qs = {(b, h): _dot(q[b][:, hs(h)], s_ref[b, h].astype(BF16)) for b, h in pairs}
    sv = {(b, h): _dot(s[b, h].astype(BF16), v[b][:, hs(h)].astype(BF16)) for b, h in pairs}
    upd = {(b, h): _dot_tn(k[b][:, hs(h)], (wl_ref[h] * v[b][:, hs(h)]).astype(BF16)) for b, h in pairs}
    o = _head_norm_all({(b, h): int_ref[h] * qs[b, h] + sv[b, h] for b, h in pairs})
    for b, h in pairs:
        s_ref[b, h] = dl_ref[h] * s_ref[b, h] + upd[b, h]
    for b in seqs:
        gg = gg_ref[b]
        y = jnp.concatenate([o[b, h] for h in range(N_HEADS)], axis=1) * ng_ref[...] * (gg * jax.nn.sigmoid(gg))
        o_ref[b] = y.astype(o_ref.dtype)


def _retention_consts(chunk):
    log_g = np.log(1.0 - np.exp2(-5.0 - np.arange(N_HEADS, dtype=np.float64)))
    tau = np.arange(chunk, dtype=np.float64)
    rel = tau[:, None] - tau[None, :]
    decay = np.where(rel >= 0, np.exp(log_g[:, None, None] * np.maximum(rel, 0.0)), 0.0)
    inter = np.exp(log_g[:, None] * (tau + 1.0))[..., None]
    wl = np.exp(log_g[:, None] * (chunk - 1.0 - tau))[..., None]
    dl = np.exp(log_g * chunk)[:, None, None]
    return tuple(jnp.asarray(a, F32) for a in (decay, inter, wl, dl))


def _retention_pair_kernel(q_ref, k_ref, v_ref, gg_ref, cos_ref, sin_ref, ng_ref, dec_ref, int_ref, wl_ref, dl_ref,
                           swap_ref, blk_ref, o_ref, s_ref, *, bb, chunk):
    t_id = pl.program_id(1)

    @pl.when(t_id == 0)
    def _():
        s_ref[...] = jnp.zeros_like(s_ref)

    seqs = range(bb)
    n_pair = N_HEADS // 2
    chains = [(b, p) for b in seqs for p in range(n_pair)]
    low = lax.broadcasted_iota(jnp.int32, (chunk, LANES), 1) < HEAD_DIM
    same = (lax.broadcasted_iota(jnp.int32, (LANES, LANES), 0) // HEAD_DIM
            == lax.broadcasted_iota(jnp.int32, (LANES, LANES), 1) // HEAD_DIM)
    pl_ = lambda p: slice(p * LANES, (p + 1) * LANES)
    zero_b = jnp.zeros((), BF16)
    blk = blk_ref[...]
    cos = cos_ref[...]
    sin = sin_ref[...]

    def stacked(fn, xs):
        keys = list(xs)
        y = fn(jnp.concatenate([xs[c] for c in keys], axis=0))
        return {c: y[i * chunk:(i + 1) * chunk] for i, c in enumerate(keys)}

    swap = lambda x: _split_dot(x, swap_ref[...])
    qf = {b: q_ref[b] for b in seqs}
    kf = {b: k_ref[b] for b in seqs}
    q_sw = stacked(swap, qf)
    k_sw = stacked(swap, kf)
    q = {b: (qf[b] * cos + q_sw[b] * sin).astype(BF16) for b in seqs}
    k = {b: ((kf[b] * cos + k_sw[b] * sin) * QK_SCALE).astype(BF16) for b in seqs}
    k2 = {(b, p): k[b][:, pl_(p)] for b, p in chains}
    v2 = {(b, p): v_ref[b, :, pl_(p)] for b, p in chains}
    v2b = {c: v2[c].astype(BF16) for c in chains}
    kbd = {c: jnp.concatenate([jnp.where(low, k2[c], zero_b), jnp.where(low, zero_b, k2[c])], axis=0) for c in chains}
    vbd = {c: jnp.concatenate([jnp.where(low, v2b[c], zero_b), jnp.where(low, zero_b, v2b[c])], axis=0)
           for c in chains}
    s = {(b, p): _dot_nt(q[b][:, pl_(p)], kbd[b, p]) * dec_ref[p] for b, p in chains}
    qs = {(b, p): _dot(q[b][:, pl_(p)], s_ref[b, p].astype(BF16)) for b, p in chains}
    sv = {c: _dot(s[c].astype(BF16), vbd[c]) for c in chains}
    upd = {(b, p): _dot_tn(k2[b, p], (wl_ref[p] * v2[b, p]).astype(BF16)) for b, p in chains}
    o = {(b, p): int_ref[p] * qs[b, p] + sv[b, p] for b, p in chains}
    inv = 1.0 / HEAD_DIM
    by_blk = lambda x: _split_dot(x, blk)
    mu = stacked(by_blk, o)
    oc = {c: o[c] - mu[c] * inv for c in chains}
    var = stacked(by_blk, {c: oc[c] * oc[c] for c in chains})
    for b, p in chains:
        c = (b, p)
        gg = gg_ref[b, :, pl_(p)]
        y = oc[c] * lax.rsqrt(var[c] * inv + GN_EPS) * ng_ref[:, pl_(p)] * (gg * jax.nn.sigmoid(gg))
        o_ref[b, :, pl_(p)] = y.astype(o_ref.dtype)
        s_ref[b, p] = dl_ref[p] * s_ref[b, p] + jnp.where(same, upd[c], 0.0)


def retention_mixer_prompt(proj3, cos, sin_signed, norm_g, bb, out_dtype):
    n_batch, seq, _ = proj3.shape
    chunk = CHUNK
    n_pair = N_HEADS // 2
    dec, inter, wl, dl = _retention_consts(chunk)
    rep = lambda a: jnp.broadcast_to(a, a.shape[:-1] + (HEAD_DIM,))
    pair = lambda a: jnp.concatenate([a[0::2], a[1::2]], axis=-1)
    dec2, int2, wl2, dl2 = pair(dec), pair(rep(inter)), pair(rep(wl)), pair(rep(dl))
    src = np.arange(GROUP_W)
    partner = (src // HEAD_DIM) * HEAD_DIM + (src % HEAD_DIM + HEAD_DIM // 2) % HEAD_DIM
    swap = np.zeros((GROUP_W, GROUP_W), np.float32)
    swap[partner, src] = 1.0
    idx = np.arange(LANES) // HEAD_DIM
    blk = (idx[:, None] == idx[None, :]).astype(np.float32)

    def row_spec(cblk):
        return pl.BlockSpec((bb, chunk, GROUP_W), lambda bi, t: (bi, t, cblk))

    def const_spec(shape):
        return pl.BlockSpec(shape, lambda bi, t: (0,) * len(shape))

    pos_spec = pl.BlockSpec((chunk, GROUP_W), lambda bi, t: (t, 0))
    state_spec = pl.BlockSpec((bb, n_pair, LANES, LANES), lambda bi, t: (bi, 0, 0, 0))
    kern = functools.partial(_retention_pair_kernel, bb=bb, chunk=chunk)
    o, sbd = pl.pallas_call(
        kern, grid=(n_batch // bb, seq // chunk),
        in_specs=[row_spec(C_RQ), row_spec(C_RK), row_spec(C_RV), row_spec(C_RG), pos_spec, pos_spec,
                  const_spec((1, GROUP_W)), const_spec(dec2.shape), const_spec(int2.shape), const_spec(wl2.shape),
                  const_spec(dl2.shape), const_spec(swap.shape), const_spec(blk.shape)],
        out_specs=[pl.BlockSpec((bb, chunk, GROUP_W), lambda bi, t: (bi, t, 0)), state_spec],
        out_shape=[jax.ShapeDtypeStruct((n_batch, seq, GROUP_W), out_dtype),
                   jax.ShapeDtypeStruct((n_batch, n_pair, LANES, LANES), F32)],
        compiler_params=_cp("arbitrary", "arbitrary"), name="retention_prompt")(
            proj3, proj3, proj3, proj3, cos, sin_signed, norm_g, dec2, int2, wl2, dl2,
            jnp.asarray(swap, dtype=BF16), jnp.asarray(blk, dtype=BF16))
    s5d = sbd.reshape(n_batch, n_pair, 2, HEAD_DIM, 2, HEAD_DIM)
    state = jnp.stack([s5d[:, :, j, :, j, :] for j in range(2)], axis=2).reshape(n_batch, N_HEADS, HEAD_DIM, HEAD_DIM)
    return o, state


def retention_mixer(proj3, cos, sin_signed, norm_g, s0, bb, out_dtype):
    n_batch, seq, _ = proj3.shape
    chunk = math.gcd(seq, CHUNK)
    dec, inter, wl, dl = _retention_consts(chunk)

    def row_spec(cblk):
        return pl.BlockSpec((bb, chunk, GROUP_W), lambda bi, t: (bi, t, cblk))

    def const_spec(shape):
        return pl.BlockSpec(shape, lambda bi, t: (0,) * len(shape))

    pos_spec = pl.BlockSpec((chunk, GROUP_W), lambda bi, t: (t, 0))
    state_spec = pl.BlockSpec((bb, N_HEADS, HEAD_DIM, HEAD_DIM), lambda bi, t: (bi, 0, 0, 0))
    kern = functools.partial(_retention_kernel, bb=bb)
    return pl.pallas_call(
        kern, grid=(n_batch // bb, seq // chunk),
        in_specs=[row_spec(C_RQ), row_spec(C_RK), row_spec(C_RV), row_spec(C_RG), pos_spec, pos_spec,
                  const_spec((1, GROUP_W)), const_spec(dec.shape), const_spec(inter.shape), const_spec(wl.shape),
                  const_spec(dl.shape), state_spec],
        out_specs=[pl.BlockSpec((bb, chunk, GROUP_W), lambda bi, t: (bi, t, 0)), state_spec],
        out_shape=[jax.ShapeDtypeStruct((n_batch, seq, GROUP_W), out_dtype),
                   jax.ShapeDtypeStruct((n_batch, N_HEADS, HEAD_DIM, HEAD_DIM), F32)],
        compiler_params=_cp("arbitrary", "arbitrary"), name="retention")(
            proj3, proj3, proj3, proj3, cos, sin_signed, norm_g, dec, inter, wl, dl, s0)


def _s5_kernel(u_ref, wb_ref, a1_ref, a2_ref, h0_ref, wc_ref, d_ref, gw_ref, gb_ref, o_ref, hl_ref,
               hs_ref, ut_ref, yt_ref, *, nb, tt):
    c = pl.program_id(0)

    @pl.when(c == 0)
    def _():
        hl_ref[...] = h0_ref[...]

    halves = range(GROUP_W // LANES)
    for b in range(nb):
        ub = u_ref[b]
        for hf in halves:
            ut_ref[hf, pl.ds(b, tt, stride=nb), :] = ub[:, hf * LANES:(hf + 1) * LANES]
    u = jnp.concatenate([ut_ref[hf] for hf in halves], axis=1)
    hs_ref[...] = _dot(u.astype(BF16), wb_ref[...])
    a1 = jnp.broadcast_to(a1_ref[...], (nb, 2 * S5_W))
    a2 = jnp.broadcast_to(a2_ref[...], (nb, 2 * S5_W))

    def step(t, h):
        r0 = pl.multiple_of(t * nb, nb)
        swapped = jnp.concatenate([h[:, S5_W:], h[:, :S5_W]], axis=1)
        h = a1 * h + a2 * swapped + hs_ref[pl.ds(r0, nb), :]
        hs_ref[pl.ds(r0, nb), :] = h
        return h

    hl_ref[...] = lax.fori_loop(0, tt, step, hl_ref[...])
    y = _dot(hs_ref[...].astype(BF16), wc_ref[...]) + d_ref[...] * u
    g5 = jax.nn.gelu(y)
    yt = g5 * jax.nn.sigmoid(_dot(g5.astype(BF16), gw_ref[...]) + gb_ref[...])
    for hf in halves:
        yt_ref[hf] = yt[:, hf * LANES:(hf + 1) * LANES]
    for b in range(nb):
        o_ref[b] = jnp.concatenate([yt_ref[hf, pl.ds(b, tt, stride=nb), :] for hf in halves],
                                   axis=1).astype(o_ref.dtype)


def s5_mixer(proj3, tt, wb, a1, a2, h0, wc, d, glu_w, glu_b, out_dtype):
    nb, seq, _ = proj3.shape
    rows = tt * nb

    def const_spec(shape):
        return pl.BlockSpec(shape, lambda c: (0,) * len(shape))

    kern = functools.partial(_s5_kernel, nb=nb, tt=tt)
    return pl.pallas_call(
        kern, grid=(seq // tt,),
        in_specs=[pl.BlockSpec((nb, tt, GROUP_W), lambda c: (0, c, C_SU)),
                  const_spec(wb.shape), const_spec(a1.shape), const_spec(a2.shape), const_spec(h0.shape),
                  const_spec(wc.shape), const_spec(d.shape), const_spec(glu_w.shape), const_spec(glu_b.shape)],
        out_specs=[pl.BlockSpec((nb, tt, GROUP_W), lambda c: (0, c, 0)), const_spec(h0.shape)],
        out_shape=[jax.ShapeDtypeStruct((nb, seq, GROUP_W), out_dtype), jax.ShapeDtypeStruct(h0.shape, F32)],
        scratch_shapes=[pltpu.VMEM((rows, 2 * S5_W), F32), pltpu.VMEM((GROUP_W // LANES, rows, LANES), F32),
                        pltpu.VMEM((GROUP_W // LANES, rows, LANES), F32)],
        compiler_params=_cp("arbitrary"), name="s5")(proj3, wb, a1, a2, h0, wc, d, glu_w, glu_b)


def _s5_weights(a_re, a_im, log_dt, b_re, b_im, c_re, c_im):
    lam = lax.complex(a_re, a_im)
    a_bar = jnp.exp(lam * jnp.exp(log_dt))
    b_bar = ((a_bar - 1.0) / lam)[..., None] * lax.complex(b_re, b_im)
    eye = jnp.eye(S5_GROUPS, dtype=F32)

    def in_map(m):
        return jnp.einsum('gpc,gh->gchp', m, eye).reshape(S5_GROUPS * S5_GROUP, S5_W)

    def out_map(m):
        return jnp.einsum('gcp,gh->gphc', m, eye).reshape(S5_W, S5_GROUPS * S5_GROUP)

    wb = jnp.concatenate([in_map(b_bar.real), in_map(b_bar.imag)], axis=1).astype(BF16)
    wc = jnp.concatenate([out_map(c_re), -out_map(c_im)], axis=0).astype(BF16)
    ar = a_bar.real.reshape(1, S5_W)
    ai = a_bar.imag.reshape(1, S5_W)
    return wb, jnp.concatenate([ar, ar], axis=1), jnp.concatenate([-ai, ai], axis=1), wc


def _cross_attn_kernel(x_ref, wq_ref, k_ref, v_ref, wo_ref, g_ref, b_ref, o_ref, *, nb):
    x = x_ref[...]
    rows = x.shape[0] // nb
    q = (_dot(x.astype(BF16), wq_ref[...]) * QK_SCALE).astype(BF16)
    hs = [slice(h * HEAD_DIM, (h + 1) * HEAD_DIM) for h in range(N_HEADS)]
    pairs = [(b, h) for b in range(nb) for h in range(N_HEADS)]
    k = [k_ref[b].astype(BF16) for b in range(nb)]
    v = [v_ref[b].astype(BF16) for b in range(nb)]
    s = {(b, h): _dot_nt(q[b * rows:(b + 1) * rows, hs[h]], k[b][:, hs[h]]) for b, h in pairs}
    e = {c: jnp.exp(s[c] - jnp.max(s[c], axis=1, keepdims=True)) for c in pairs}
    p = {c: e[c] / jnp.sum(e[c], axis=1, keepdims=True) for c in pairs}
    pv = {(b, h): _dot(p[b, h].astype(BF16), v[b][:, hs[h]]) for b, h in pairs}
    o = jnp.concatenate([jnp.concatenate([pv[b, h] for h in range(N_HEADS)], axis=1) for b in range(nb)], axis=0)
    y = ALPHA * x + _dot(o.astype(BF16), wo_ref[...])
    o_ref[...] = _layer_norm(y, g_ref[...], b_ref[...])


def cross_attn_ln(x, row_blk0, n_batch, seq, tq, nb, mem_k, mem_v, wq, wo, g, b):
    nq = seq // tq if nb == 1 else 1

    def const_spec(shape):
        return pl.BlockSpec(shape, lambda bb, i: (0,) * len(shape))

    row_spec = pl.BlockSpec((tq, D_MODEL), lambda bb, i: (row_blk0 + bb * nq + i, 0))
    mem_spec = pl.BlockSpec((nb, N_MEM, GROUP_W), lambda bb, i: (bb, 0, 0))
    return pl.pallas_call(
        functools.partial(_cross_attn_kernel, nb=nb), grid=(n_batch // nb, nq),
        in_specs=[row_spec, const_spec(wq.shape), mem_spec, mem_spec, const_spec(wo.shape),
                  const_spec(g.shape), const_spec(b.shape)],
        out_specs=row_spec, out_shape=jax.ShapeDtypeStruct(x.shape, F32), input_output_aliases={0: 0},
        compiler_params=_cp("arbitrary", "arbitrary"), name="cross_attn")(x, wq, mem_k, mem_v, wo, g, b)


SWIGLU_ROWS = 256


def _swiglu_accumulate(xb_ref, wg, wu, wd, acc_ref, n_valid=None):
    wgb, wub, wdb = wg.astype(BF16), wu.astype(BF16), wd.astype(BF16)
    n_sub = xb_ref.shape[0] // SWIGLU_ROWS

    def hidden(r):
        xb = xb_ref[pl.ds(r * SWIGLU_ROWS, SWIGLU_ROWS), :]
        gate = _dot(xb, wgb)
        up = _dot(xb, wub)
        return (gate * jax.nn.sigmoid(gate) * up).astype(BF16)

    def first_sub_blocks(n):
        hid = hidden(0)
        for r in range(n):
            nxt = hidden(r + 1) if r + 1 < n else None
            acc_ref[pl.ds(r * SWIGLU_ROWS, SWIGLU_ROWS), :] += _dot(hid, wdb)
            hid = nxt

    if n_valid is None:
        first_sub_blocks(n_sub)
        return
    need = (n_valid + SWIGLU_ROWS - 1) // SWIGLU_ROWS
    for n in range(1, n_sub + 1):
        pl.when(need == n)(functools.partial(first_sub_blocks, n))


def _ffn_kernel(x_ref, wg_ref, wu_ref, wd_ref, g_ref, b_ref, o_ref, xb_ref, *, nf):
    j = pl.program_id(1)

    @pl.when(j == 0)
    def _():
        xb_ref[...] = x_ref[...].astype(BF16)
        o_ref[...] = jnp.zeros_like(o_ref)

    _swiglu_accumulate(xb_ref, wg_ref[...], wu_ref[...], wd_ref[...], o_ref)

    @pl.when(j == nf - 1)
    def _():
        o_ref[...] = _layer_norm(ALPHA * x_ref[...] + o_ref[...], g_ref[...], b_ref[...])


def ffn_ln(x, wg, wu, wd, g, b, tm, tf):
    m = x.shape[0]
    nf = D_FF // tf
    kern = functools.partial(_ffn_kernel, nf=nf)
    return pl.pallas_call(
        kern, grid=(m // tm, nf),
        in_specs=[pl.BlockSpec((tm, D_MODEL), lambda i, j: (i, 0)),
                  pl.BlockSpec((D_MODEL, tf), lambda i, j: (0, j)), pl.BlockSpec((D_MODEL, tf), lambda i, j: (0, j)),
                  pl.BlockSpec((tf, D_MODEL), lambda i, j: (j, 0)),
                  pl.BlockSpec((1, D_MODEL), lambda i, j: (0, 0)), pl.BlockSpec((1, D_MODEL), lambda i, j: (0, 0))],
        out_specs=pl.BlockSpec((tm, D_MODEL), lambda i, j: (i, 0)),
        out_shape=jax.ShapeDtypeStruct((m, D_MODEL), F32),
        scratch_shapes=[pltpu.VMEM((tm, D_MODEL), BF16)],
        compiler_params=_cp("arbitrary", "arbitrary"), name="ffn")(x, wg, wu, wd, g, b)


def _router_kernel(x_ref, w_ref, b_ref, lt_ref, o_ref, before_ref, cnt_ref):
    @pl.when(pl.program_id(0) == 0)
    def _():
        cnt_ref[...] = jnp.zeros_like(cnt_ref)

    logits = jnp.dot(x_ref[...], w_ref[...], preferred_element_type=F32, precision=lax.Precision.HIGHEST) + b_ref[...]
    lane = lax.broadcasted_iota(jnp.int32, logits.shape, 1)
    neg = jnp.float32(-jnp.inf)
    lg = jnp.where(lane < N_EXPERTS, logits, neg)
    m1 = jnp.max(lg, axis=1, keepdims=True)
    i1 = jnp.min(jnp.where(lg == m1, lane, LANES), axis=1, keepdims=True)
    lg2 = jnp.where(lane == i1, neg, lg)
    m2 = jnp.max(lg2, axis=1, keepdims=True)
    i2 = jnp.min(jnp.where(lg2 == m2, lane, LANES), axis=1, keepdims=True)
    e2 = jnp.exp(m2 - m1)
    g1 = 1.0 / (1.0 + e2)
    g2 = e2 / (1.0 + e2)
    out = jnp.where(lane == 0, i1.astype(F32), jnp.where(lane == 1, i2.astype(F32),
                    jnp.where(lane == 2, g1, jnp.where(lane == 3, g2, 0.0))))
    o_ref[...] = out
    chosen = jnp.where(jnp.logical_or(lane == i1, lane == i2), 1.0, 0.0)
    cnt = cnt_ref[0:1, :]
    before_ref[...] = _dot(lt_ref[...], chosen.astype(BF16)) + cnt
    cnt_ref[...] = jnp.broadcast_to(cnt + jnp.sum(chosen, axis=0, keepdims=True), cnt_ref.shape)


def router(x, w_pad, b_pad, tm):
    m = x.shape[0]
    lower = jnp.asarray(np.tril(np.ones((tm, tm), np.float32), -1), dtype=BF16)
    row = pl.BlockSpec((tm, LANES), lambda i: (i, 0))
    return pl.pallas_call(
        _router_kernel, grid=(m // tm,),
        in_specs=[pl.BlockSpec((tm, D_MODEL), lambda i: (i, 0)), pl.BlockSpec((D_MODEL, LANES), lambda i: (0, 0)),
                  pl.BlockSpec((1, LANES), lambda i: (0, 0)), pl.BlockSpec((tm, tm), lambda i: (0, 0))],
        out_specs=[row, row, pl.BlockSpec((8, LANES), lambda i: (0, 0))],
        out_shape=[jax.ShapeDtypeStruct((m, LANES), F32), jax.ShapeDtypeStruct((m, LANES), F32),
                   jax.ShapeDtypeStruct((8, LANES), F32)],
        compiler_params=_cp("arbitrary"), name="router")(x, w_pad, b_pad, lower)


def _moe_ffn_kernel(te_ref, nu_ref, tr_ref, x_ref, wg_ref, wu_ref, wd_ref, o_ref, xb_ref):
    i = pl.program_id(0)
    j = pl.program_id(1)
    used = i < nu_ref[0]

    @pl.when(used)
    def _():
        @pl.when(j == 0)
        def _():
            xb_ref[...] = x_ref[...].astype(BF16)
            o_ref[...] = jnp.zeros_like(o_ref)

        _swiglu_accumulate(xb_ref, wg_ref[0], wu_ref[0], wd_ref[0], o_ref, tr_ref[i])

    @pl.when(jnp.logical_and(jnp.logical_not(used), j == 0))
    def _():
        o_ref[...] = jnp.zeros_like(o_ref)


def moe_ffn(x_sorted, tile_expert, n_used, tile_rows, wg, wu, wd, tf):
    n_rows = x_sorted.shape[0]
    n_tiles = n_rows // MOE_TILE
    nf = D_FF // tf

    def jj(i, j, nu):
        return jnp.where(i < nu[0], j, nf - 1)

    gs = pltpu.PrefetchScalarGridSpec(
        num_scalar_prefetch=3, grid=(n_tiles, nf),
        in_specs=[pl.BlockSpec((MOE_TILE, D_MODEL), lambda i, j, te, nu, tr: (i, 0)),
                  pl.BlockSpec((1, D_MODEL, tf), lambda i, j, te, nu, tr: (te[i], 0, jj(i, j, nu))),
                  pl.BlockSpec((1, D_MODEL, tf), lambda i, j, te, nu, tr: (te[i], 0, jj(i, j, nu))),
                  pl.BlockSpec((1, tf, D_MODEL), lambda i, j, te, nu, tr: (te[i], jj(i, j, nu), 0))],
        out_specs=pl.BlockSpec((MOE_TILE, D_MODEL), lambda i, j, te, nu, tr: (i, 0)),
        scratch_shapes=[pltpu.VMEM((MOE_TILE, D_MODEL), BF16)])
    return pl.pallas_call(
        _moe_ffn_kernel, grid_spec=gs, out_shape=jax.ShapeDtypeStruct((n_rows, D_MODEL), F32),
        compiler_params=_cp("arbitrary", "arbitrary"), name="moe_ffn")(
            tile_expert, n_used, tile_rows, x_sorted, wg, wu, wd)


def _combine_ln_kernel(x_ref, r_ref, ya_ref, yb_ref, g_ref, b_ref, o_ref):
    r = r_ref[...]
    y = r[:, TOP_K:TOP_K + 1] * ya_ref[...] + r[:, TOP_K + 1:TOP_K + 2] * yb_ref[...]
    o_ref[...] = _layer_norm(ALPHA * x_ref[...] + y, g_ref[...], b_ref[...])


def combine_ln(x, r, ya, yb, g, b, tm, row_blk0, rows):
    row = pl.BlockSpec((tm, D_MODEL), lambda i: (row_blk0 + i, 0))
    vec = pl.BlockSpec((1, D_MODEL), lambda i: (0, 0))
    return pl.pallas_call(
        _combine_ln_kernel, grid=(rows // tm,),
        in_specs=[row, pl.BlockSpec((tm, LANES), lambda i: (row_blk0 + i, 0)), row, row, vec, vec],
        out_specs=pl.BlockSpec((tm, D_MODEL), lambda i: (i, 0)),
        out_shape=jax.ShapeDtypeStruct((rows, D_MODEL), F32),
        compiler_params=_cp("arbitrary"), name="combine_ln")(x, r, ya, yb, g, b)


def moe_ln(x, router_w, router_b, wg, wu, wd, g, b, tm, splits):
    m = x.shape[0]
    w_pad = jnp.zeros((D_MODEL, LANES), F32).at[:, :N_EXPERTS].set(router_w)
    b_pad = jnp.zeros((1, LANES), F32).at[0, :N_EXPERTS].set(router_b)
    r, before_f, counts_f = router(x, w_pad, b_pad, tm)
    top_idx = r[:, :TOP_K].astype(jnp.int32)
    n_slot = m * TOP_K
    onehot = (top_idx[:, :, None] == jnp.arange(N_EXPERTS, dtype=jnp.int32)).astype(jnp.int32)
    before = before_f[:, :N_EXPERTS].astype(jnp.int32)
    counts = counts_f[0, :N_EXPERTS].astype(jnp.int32)
    tiles_per = (counts + MOE_TILE - 1) // MOE_TILE
    tile_end = jnp.cumsum(tiles_per)
    row0 = (tile_end - tiles_per) * MOE_TILE
    dest = jnp.sum(onehot * (before + row0)[:, None, :], axis=2)
    n_tiles = -(-n_slot // MOE_TILE) + N_EXPERTS
    n_rows = n_tiles * MOE_TILE
    row_tok = (jnp.arange(n_rows, dtype=jnp.int32) % m).at[dest.reshape(-1)].set(
        jnp.arange(n_slot, dtype=jnp.int32) // TOP_K, unique_indices=True, mode='promise_in_bounds')
    n_used = tile_end[-1:].astype(jnp.int32)
    tile_ids = jnp.minimum(jnp.arange(n_tiles, dtype=jnp.int32), n_used[0] - 1)
    tile_expert = jnp.minimum(jnp.sum((tile_end[None, :] <= tile_ids[:, None]).astype(jnp.int32), axis=1),
                              N_EXPERTS - 1)
    tile_rows = jnp.clip(counts[tile_expert] - (tile_ids - (tile_end - tiles_per)[tile_expert]) * MOE_TILE,
                         0, MOE_TILE).astype(jnp.int32)
    x_sorted = x.at[row_tok].get(mode='promise_in_bounds')
    y_sorted = moe_ffn(x_sorted, tile_expert, n_used, tile_rows, wg, wu, wd, FFN_COL_TILE)
    ya = y_sorted.at[dest[:, 0]].get(mode='promise_in_bounds')
    yb = y_sorted.at[dest[:, 1]].get(mode='promise_in_bounds')
    return [combine_ln(x, r, ya, yb, g, b, t, blk0, rows) for blk0, rows, t in splits]


def kernel(x_prompt, x_sample, cache_sb_k, cache_sb_v, cache_mem_k, cache_mem_v, state_ml_C, state_ml_n, state_ml_m, state_rt_S, state_s5_re, state_s5_im, page_table, mem_prompt, w_in, sb_bias, ml_b_i, ml_b_f, ml_norm_g, rt_norm_g, s5_A_re, s5_A_im, s5_log_dt, s5_B_re, s5_B_im, s5_C_re, s5_C_im, s5_D, s5_glu_w, s5_glu_b, w_out, ca_wq, ca_wk, ca_wv, ca_wo, ln_g, ln_b, ffn_w_gate, ffn_w_up, ffn_w_down, moe_router_w, moe_router_b, moe_w_gate, moe_w_up, moe_w_down):
    bp, tp, _ = x_prompt.shape
    bs, ts, _ = x_sample.shape
    n_p, n_s = bp * tp, bs * ts
    tm = ROW_TILE
    assert (n_p + n_s) % FFN_ROW_TILE == 0 and n_p % n_s == 0 and tp % PROMPT_TILE == 0
    assert bp % SEQS_PER_STEP == 0 and bs % SEQS_PER_STEP == 0 and n_p % (SEQS_PER_STEP * ts) == 0
    x = jnp.concatenate([x_prompt.reshape(n_p, D_MODEL), x_sample.reshape(n_s, D_MODEL)], axis=0)
    uu = _suffix_matrix()
    g_off = 7 * GROUP_W
    half = HEAD_DIM // 2
    freq = ROPE_BASE ** (-jnp.arange(half, dtype=F32) / half)

    def rope_tables(pos):
        ang = pos.astype(F32)[:, None] * freq[None, :]
        cos, sin = jnp.cos(ang), jnp.sin(ang)
        return (jnp.tile(jnp.concatenate([cos, cos], axis=1), (1, N_HEADS)),
                jnp.tile(jnp.concatenate([-sin, sin], axis=1), (1, N_HEADS)))

    cos_p, sin_p = rope_tables(jnp.arange(tp, dtype=jnp.int32))
    cos_s, sin_s = rope_tables(PAST_LEN + jnp.arange(ts, dtype=jnp.int32))
    cache_kt = cache_sb_k.transpose(0, 1, 3, 4, 2)
    cache_vt = cache_sb_v.transpose(0, 1, 3, 4, 2)

    p_st = [[] for _ in range(10)]
    s_st = [[] for _ in range(8)]
    for l in range(DEPTH):
        wl = w_in[l]
        w_cat = jnp.concatenate([wl[:, :g_off], wl[:, g_off + 2 * N_HEADS:], wl[:, g_off:g_off + 2 * N_HEADS],
                                 jnp.zeros((D_MODEL, PROJ_W - wl.shape[1]), F32)], axis=1).astype(BF16)
        proj_p, kt_p, vt_p = in_proj_prompt(x, w_cat, bp, tp, PROMPT_TILE)
        proj_s = linear(x, w_cat, n_s, n_p // n_s, n_s)
        proj_p3 = proj_p.reshape(bp, tp, PROJ_W)
        proj_s3 = proj_s.reshape(bs, ts, PROJ_W)
        gate_bias = jnp.zeros((1, LANES), F32).at[0, :2 * N_HEADS].set(jnp.concatenate([ml_b_i[l], ml_b_f[l]]))
        ml_g = ml_norm_g[l][None, :]
        rt_g = rt_norm_g[l][None, :]
        wb, a1, a2, wc = _s5_weights(s5_A_re[l], s5_A_im[l], s5_log_dt[l], s5_B_re[l], s5_B_im[l],
                                     s5_C_re[l], s5_C_im[l])
        s5_d = s5_D[l][None, :]
        glu_w = s5_glu_w[l].astype(BF16)
        glu_b = s5_glu_b[l][None, :]

        o_sb_p = sb_attention_prompt(proj_p, sb_bias[l], uu, bp, tp, SB_QUERY_TILE)
        o_ml_p, ml_c_p, ml_n_p, ml_m_p = mlstm_mixer_prompt(proj_p3, gate_bias, ml_g, SEQS_PER_STEP, MIX_DTYPE)
        o_rt_p, rs_p = retention_mixer_prompt(proj_p3, cos_p, sin_p, rt_g, SEQS_PER_STEP, MIX_DTYPE)
        o_ml_p = o_ml_p.reshape(n_p, GROUP_W)
        o_rt_p = o_rt_p.reshape(n_p, GROUP_W)
        o_s5_p, h5_p = s5_mixer(proj_p3, S5_STEPS, wb, a1, a2, jnp.zeros((bp, 2 * S5_W), F32), wc, s5_d, glu_w,
                                glu_b, MIX_DTYPE)
        o_s5_p = o_s5_p.reshape(n_p, GROUP_W)

        o_sb_s = sb_attention_sample(proj_s, 0, ts, cache_kt, cache_vt, page_table, l, sb_bias[l], uu,
                                     SB_PAGES_PER_STEP)
        cn0 = jnp.concatenate([state_ml_C[:, l], state_ml_n[:, l][..., None],
                               jnp.zeros((bs, N_HEADS, HEAD_DIM, LANES - HEAD_DIM - 1), F32)], axis=-1)
        o_ml_s, cn_s, m_s = mlstm_mixer(proj_s3, gate_bias, ml_g, cn0, state_ml_m[:, l], SEQS_PER_STEP, F32)
        o_rt_s, rs_s = retention_mixer(proj_s3, cos_s, sin_s, rt_g, state_rt_S[:, l], SEQS_PER_STEP, F32)
        o_ml_s = o_ml_s.reshape(n_s, GROUP_W)
        o_rt_s = o_rt_s.reshape(n_s, GROUP_W)
        h0_s = jnp.concatenate([state_s5_re[:, l].reshape(bs, S5_W), state_s5_im[:, l].reshape(bs, S5_W)], axis=1)
        o_s5_s, h5_s = s5_mixer(proj_s3, ts, wb, a1, a2, h0_s, wc, s5_d, glu_w, glu_b, F32)
        o_s5_s = o_s5_s.reshape(n_s, GROUP_W)

        wo_mix = w_out[l].astype(BF16)
        g0, b0 = ln_g[l, 0][None, :], ln_b[l, 0][None, :]
        x = mix_out_ln(x, (o_sb_p, o_ml_p, o_rt_p, o_s5_p), wo_mix, g0, b0, PROMPT_TILE, 0)
        x = mix_out_ln(x, (o_sb_s, o_ml_s, o_rt_s, o_s5_s), wo_mix, g0, b0, n_s, n_p // n_s)

        mem_kv = linear(mem_prompt.reshape(bp * N_MEM, D_MODEL),
                        jnp.concatenate([ca_wk[l], ca_wv[l]], axis=1).astype(BF16), PROMPT_TILE)
        mk_p = mem_kv[:, :GROUP_W].reshape(bp, N_MEM, GROUP_W)
        mv_p = mem_kv[:, GROUP_W:].reshape(bp, N_MEM, GROUP_W)
        wq = ca_wq[l].astype(BF16)
        wo = ca_wo[l].astype(BF16)
        g1, b1 = ln_g[l, 1][None, :], ln_b[l, 1][None, :]
        x = cross_attn_ln(x, 0, bp, tp, PROMPT_TILE, 1, mk_p, mv_p, wq, wo, g1, b1)
        s_tile = SEQS_PER_STEP * ts
        x = cross_attn_ln(x, n_p // s_tile, bs, ts, s_tile, SEQS_PER_STEP,
                          cache_mem_k[:, l].reshape(bs, N_MEM, GROUP_W),
                          cache_mem_v[:, l].reshape(bs, N_MEM, GROUP_W), wq, wo, g1, b1)

        g2, b2 = ln_g[l, 2][None, :], ln_b[l, 2][None, :]
        j = l // 2
        last = l == DEPTH - 1
        if l % 2 == 0:
            x = ffn_ln(x, ffn_w_gate[j], ffn_w_up[j], ffn_w_down[j], g2, b2, FFN_ROW_TILE, FFN_COL_TILE)
            y_out = (x[:n_p], x[n_p:]) if last else None
        else:
            splits = [(0, n_p, PROMPT_TILE), (n_p // n_s, n_s, n_s)] if last else [(0, n_p + n_s, tm)]
            y_out = moe_ln(x, moe_router_w[j], moe_router_b[j], moe_w_gate[j], moe_w_up[j], moe_w_down[j],
                           g2, b2, tm, splits)
            x = None if last else y_out[0]

        def heads(a, nb_, t_):
            return a.reshape(nb_, t_, N_HEADS, HEAD_DIM)

        p_st[0].append(kt_p)
        p_st[1].append(vt_p)
        p_st[2].append(heads(mk_p, bp, N_MEM))
        p_st[3].append(heads(mv_p, bp, N_MEM))
        s_st[0].append(heads(proj_s[:, C_SK * GROUP_W:(C_SK + 1) * GROUP_W], bs, ts))
        s_st[1].append(heads(proj_s[:, C_SV * GROUP_W:(C_SV + 1) * GROUP_W], bs, ts))
        ml_p = (ml_c_p, ml_n_p, ml_m_p)
        ml_s = (cn_s[..., :HEAD_DIM], cn_s[..., HEAD_DIM], m_s[:, 0, :N_HEADS])
        for st, ml, rs, h5, nb_ in ((p_st, ml_p, rs_p, h5_p, bp), (s_st, ml_s, rs_s, h5_s, bs)):
            off = 4 if st is p_st else 2
            st[off + 0].append(ml[0])
            st[off + 1].append(ml[1])
            st[off + 2].append(ml[2])
            st[off + 3].append(rs)
            st[off + 4].append(h5[:, :S5_W].reshape(nb_, S5_GROUPS, S5_STATE))
            st[off + 5].append(h5[:, S5_W:].reshape(nb_, S5_GROUPS, S5_STATE))

    y_prompt = y_out[0].reshape(bp, tp, D_MODEL)
    y_sample = y_out[1].reshape(bs, ts, D_MODEL)
    p_out = [jnp.stack(a, axis=1) for a in p_st]
    for i in range(2):
        p_out[i] = p_out[i].reshape(bp, DEPTH, N_HEADS, HEAD_DIM, tp).transpose(0, 1, 4, 2, 3)
    s_out = [jnp.stack(a, axis=1) for a in s_st]
    return (y_prompt, y_sample, *p_out, *s_out)
```

```python
import functools
import math

import numpy as np
import jax
import jax.numpy as jnp
from jax import lax
from jax.experimental import pallas as pl
from jax.experimental.pallas import tpu as pltpu

F32 = jnp.float32
BF16 = jnp.bfloat16

D_MODEL = 1024
DEPTH = 2
PAST_LEN = 8192
PAGE_SIZE = 128
HEAD_DIM = 64
N_HEADS = 4
GROUP_W = N_HEADS * HEAD_DIM
S5_GROUPS = 16
S5_GROUP = 16
S5_STATE = 64
S5_W = S5_GROUPS * S5_STATE
N_MEM = 256
D_FF = 3584
N_EXPERTS = 8
TOP_K = 2
CHUNK = 64
ROPE_BASE = 10000.0
LN_EPS = 1e-5
GN_EPS = 1e-6
ALPHA = (2 * DEPTH) ** 0.25
QK_SCALE = HEAD_DIM ** -0.5
LOG2E = math.log2(math.e)

LANES = 128
PROJ_W = 25 * LANES
C_SQ, C_SK, C_SV, C_MQ, C_MK, C_MV, C_MO, C_RQ, C_RK, C_RV, C_RG, C_SU = range(12)
C_GATES = 12 * GROUP_W // LANES
VMEM_LIMIT = 48 * 1024 * 1024
ROW_TILE = 640
PROMPT_TILE = 512
FFN_ROW_TILE = 2 * ROW_TILE
FFN_COL_TILE = 512
MOE_TILE = 1024
SB_QUERY_TILE = 256
SB_PAGES_PER_STEP = 16
SEQS_PER_STEP = 8
S5_STEPS = CHUNK
MIX_DTYPE = BF16


def _cp(*sem):
    return pltpu.CompilerParams(dimension_semantics=sem, vmem_limit_bytes=VMEM_LIMIT)


def _dot(a, b):
    return jnp.dot(a, b, preferred_element_type=F32)


def _dot_nt(a, b):
    return lax.dot_general(a, b, (((1,), (1,)), ((), ())), preferred_element_type=F32)


def _dot_tn(a, b):
    return lax.dot_general(a, b, (((0,), (0,)), ((), ())), preferred_element_type=F32)


def _layer_norm(y, g, b):
    mu = jnp.mean(y, axis=-1, keepdims=True)
    yc = y - mu
    var = jnp.mean(yc * yc, axis=-1, keepdims=True)
    return yc * lax.rsqrt(var + LN_EPS) * g + b


def _row_sum(x, scale=1.0):
    ones = jnp.full((x.shape[1], LANES), scale, BF16)
    hi = x.astype(BF16)
    lo = (x - hi.astype(F32)).astype(BF16)
    return (_dot(hi, ones) + _dot(lo, ones))[:, :x.shape[1]]


def _cumsum_rows(tril, x):
    hi = x.astype(BF16)
    lo = (x - hi.astype(F32)).astype(BF16)
    return _dot(tril, hi) + _dot(tril, lo)


def _head_norm_all(hd):
    inv = 1.0 / HEAD_DIM
    mu = {p: _row_sum(x, inv) for p, x in hd.items()}
    hc = {p: hd[p] - mu[p] for p in hd}
    var = {p: _row_sum(hc[p] * hc[p], inv) for p in hd}
    return {p: hc[p] * lax.rsqrt(var[p] + GN_EPS) for p in hd}


def _neg_softplus(z):
    return -(jnp.maximum(z, 0.0) + jnp.log1p(jnp.exp(-jnp.abs(z))))


def _log_sigmoid(z):
    return _neg_softplus(-z)


def _linear_kernel(x_ref, w_ref, o_ref):
    o_ref[...] = _dot(x_ref[...].astype(BF16), w_ref[...]).astype(o_ref.dtype)


def linear(x, w, tm, row_blk0=0, n_rows=None, out_dtype=F32):
    m, k = x.shape
    m = m if n_rows is None else n_rows
    n = w.shape[1]
    return pl.pallas_call(
        _linear_kernel, grid=(m // tm,),
        in_specs=[pl.BlockSpec((tm, k), lambda i: (row_blk0 + i, 0)), pl.BlockSpec((k, n), lambda i: (0, 0))],
        out_specs=pl.BlockSpec((tm, n), lambda i: (i, 0)),
        out_shape=jax.ShapeDtypeStruct((m, n), out_dtype),
        compiler_params=_cp("arbitrary"), name="linear")(x, w)


def _in_proj_kernel(x_ref, w_ref, o_ref, kt_ref, vt_ref):
    o = _dot(x_ref[...].astype(BF16), w_ref[...])
    o_ref[...] = o
    kt_ref[0] = o[:, C_SK * GROUP_W:(C_SK + 1) * GROUP_W].T
    vt_ref[0] = o[:, C_SV * GROUP_W:(C_SV + 1) * GROUP_W].T


def in_proj_prompt(x, w, n_batch, seq, tm):
    k = x.shape[1]
    n = w.shape[1]
    nt = seq // tm
    t_spec = pl.BlockSpec((1, GROUP_W, tm), lambda i: (i // nt, 0, i % nt))
    t_shape = jax.ShapeDtypeStruct((n_batch, GROUP_W, seq), F32)
    return pl.pallas_call(
        _in_proj_kernel, grid=(n_batch * nt,),
        in_specs=[pl.BlockSpec((tm, k), lambda i: (i, 0)), pl.BlockSpec((k, n), lambda i: (0, 0))],
        out_specs=[pl.BlockSpec((tm, n), lambda i: (i, 0)), t_spec, t_spec],
        out_shape=[jax.ShapeDtypeStruct((n_batch * seq, n), F32), t_shape, t_shape],
        compiler_params=_cp("arbitrary"), name="in_proj")(x, w)


def _mix_out_ln_kernel(x_ref, a_ref, b_ref, c_ref, d_ref, w_ref, g_ref, bias_ref, o_ref):
    h = sum(_dot(p[...].astype(BF16), w_ref[pl.ds(n * GROUP_W, GROUP_W), :])
            for n, p in enumerate((a_ref, b_ref, c_ref, d_ref)))
    o_ref[...] = _layer_norm(ALPHA * x_ref[...] + h, g_ref[...], bias_ref[...])


def mix_out_ln(x, parts, w, g, b, tm, row_blk0):
    rows = parts[0].shape[0]
    row_spec = pl.BlockSpec((tm, D_MODEL), lambda i: (row_blk0 + i, 0))
    part_spec = pl.BlockSpec((tm, GROUP_W), lambda i: (i, 0))
    vec = pl.BlockSpec((1, D_MODEL), lambda i: (0, 0))
    return pl.pallas_call(
        _mix_out_ln_kernel, grid=(rows // tm,),
        in_specs=[row_spec, part_spec, part_spec, part_spec, part_spec,
                  pl.BlockSpec((D_MODEL, D_MODEL), lambda i: (0, 0)), vec, vec],
        out_specs=row_spec, out_shape=jax.ShapeDtypeStruct(x.shape, F32), input_output_aliases={0: 0},
        compiler_params=_cp("arbitrary"), name="mix_out_ln")(x, *parts, w, g, b)


def _suffix_matrix():
    j = np.arange(LANES)
    u = (j[:, None] >= j[None, :]).astype(np.float32)
    uu = np.concatenate([u, np.ones((LANES, LANES), np.float32)], axis=1)
    return jnp.asarray(np.concatenate([uu, uu], axis=0), dtype=BF16)


def _suffix_sums(lr, uu):
    hi = lr.astype(BF16)
    lo = (lr - hi.astype(F32)).astype(BF16)
    r = _dot(jnp.concatenate([hi, lo], axis=1), uu)
    return r[:, :LANES], r[:, LANES:]


def _log2_rem(z2):
    return jnp.minimum(-z2, 0.0) - jnp.log2(1.0 + jnp.exp2(-jnp.abs(z2)))


def _sb_prompt_kernel(bias_ref, q_ref, k_ref, v_ref, uu_ref, o_ref, acc_ref, car_ref, kb_ref, vb_ref, *, tq):
    i = pl.program_id(1)
    tk = LANES
    nsub = tq // tk
    acc_ref[...] = jnp.zeros_like(acc_ref)
    car_ref[...] = jnp.zeros_like(car_ref)

    @pl.when(i == 0)
    def _():
        for h in range(N_HEADS):
            kb_ref[h] = k_ref[:, h * HEAD_DIM:(h + 1) * HEAD_DIM].astype(BF16)
            vb_ref[h] = v_ref[:, h * HEAD_DIM:(h + 1) * HEAD_DIM].astype(BF16)

    q = (q_ref[...] * (QK_SCALE * LOG2E)).astype(BF16)
    qh = [q[:, h * HEAD_DIM:(h + 1) * HEAD_DIM] for h in range(N_HEADS)]
    b2 = [bias_ref[h] * LOG2E for h in range(N_HEADS)]
    uu = uu_ref[...]
    row = lax.broadcasted_iota(jnp.int32, (tq, tk), 0)
    col = lax.broadcasted_iota(jnp.int32, (tq, tk), 1)
    heads = range(N_HEADS)

    def block_pair(j_hi, causal_hi, causal_lo):
        r0 = [pl.multiple_of((j_hi - d) * tk, tk) for d in range(2)]
        z2 = [[_dot_nt(qh[h], kb_ref[h, pl.ds(r0[d], tk), :]) + b2[h] for h in heads] for d in range(2)]
        cs, tot = [], []
        for d, causal in enumerate((causal_hi, causal_lo)):
            lr = [_log2_rem(z) for z in z2[d]]
            if causal is not None:
                lr = [jnp.where(causal, a, 0.0) for a in lr]
            c, t = _suffix_sums(jnp.concatenate(lr, axis=0), uu)
            cs.append(c)
            tot.append(t)
        car = [car_ref[h] for h in heads]
        pv = []
        for d, causal in enumerate((causal_hi, causal_lo)):
            w = [jnp.exp2(z2[d][h] + cs[d][h * tq:(h + 1) * tq] + car[h]) for h in heads]
            if causal is not None:
                w = [jnp.where(causal, a, 0.0) for a in w]
            pv.append([_dot(w[h].astype(BF16), vb_ref[h, pl.ds(r0[d], tk), :]) for h in heads])
            car = [car[h] + tot[d][h * tq:(h + 1) * tq] for h in heads]
        for h in heads:
            acc_ref[h] += pv[0][h] + pv[1][h]
            car_ref[h] = car[h]

    assert nsub == 2
    block_pair(i * nsub + 1, (col + tk) < row, col < row)

    def body(jj, carry):
        block_pair(i * nsub - 1 - 2 * jj, None, None)
        return carry

    lax.fori_loop(0, i, body, 0)
    o_ref[...] = jnp.concatenate([acc_ref[h] for h in range(N_HEADS)], axis=1).astype(o_ref.dtype)


def sb_attention_prompt(proj, sb_bias, uu, n_batch, seq, tq):
    nq = seq // tq
    kern = functools.partial(_sb_prompt_kernel, tq=tq)
    return pl.pallas_call(
        kern, grid=(n_batch, nq),
        in_specs=[pl.BlockSpec(memory_space=pltpu.SMEM),
                  pl.BlockSpec((tq, GROUP_W), lambda b, i: (b * nq + i, C_SQ)),
                  pl.BlockSpec((seq, GROUP_W), lambda b, i: (b, C_SK)),
                  pl.BlockSpec((seq, GROUP_W), lambda b, i: (b, C_SV)),
                  pl.BlockSpec((2 * LANES, 2 * LANES), lambda b, i: (0, 0))],
        out_specs=pl.BlockSpec((tq, GROUP_W), lambda b, i: (b * nq + i, 0)),
        out_shape=jax.ShapeDtypeStruct((n_batch * seq, GROUP_W), MIX_DTYPE),
        scratch_shapes=[pltpu.VMEM((N_HEADS, tq, HEAD_DIM), F32), pltpu.VMEM((N_HEADS, tq, LANES), F32),
                        pltpu.VMEM((N_HEADS, seq, HEAD_DIM), BF16), pltpu.VMEM((N_HEADS, seq, HEAD_DIM), BF16)],
        compiler_params=_cp("arbitrary", "arbitrary"), name="sb_prompt")(sb_bias, proj, proj, proj, uu)


def _sb_sample_kernel(pt_ref, bias_ref, q_ref, kn_ref, vn_ref, u8_ref, uu_ref, *rest, n_pp, n_steps):
    k_refs = rest[:n_pp]
    v_refs = rest[n_pp:2 * n_pp]
    o_ref = rest[2 * n_pp]
    acc_ref, car_ref = rest[2 * n_pp + 1:]
    s = pl.program_id(1)
    nq = q_ref.shape[0]
    rows = N_HEADS * nq
    q = (q_ref[...] * (QK_SCALE * LOG2E)).astype(BF16)
    row_head = lax.broadcasted_iota(jnp.int32, (rows, GROUP_W), 0) // nq
    own = row_head == lax.broadcasted_iota(jnp.int32, (rows, GROUP_W), 1) // HEAD_DIM
    q_bd = jnp.where(own, jnp.concatenate([q] * N_HEADS, axis=0), jnp.zeros((), BF16))
    row_head_l = lax.broadcasted_iota(jnp.int32, (rows, LANES), 0) // nq
    b2 = jnp.zeros((rows, LANES), F32)
    for h in range(N_HEADS):
        b2 = jnp.where(row_head_l == h, bias_ref[h] * LOG2E, b2)

    @pl.when(s == 0)
    def _():
        kn = kn_ref[...].astype(BF16)
        vn = vn_ref[...].astype(BF16)
        t = lax.broadcasted_iota(jnp.int32, (rows, nq), 0) % nq
        causal = lax.broadcasted_iota(jnp.int32, (rows, nq), 1) < t
        z2 = _dot_nt(q_bd, kn) + b2[:, :nq]
        lr = jnp.where(causal, _log2_rem(z2), 0.0)
        cs = jnp.dot(lr, u8_ref[...], preferred_element_type=F32, precision=lax.Precision.HIGHEST)
        w = jnp.where(causal, jnp.exp2(z2 + cs), 0.0)
        acc_ref[...] = _dot(w.astype(BF16), vn)
        car_ref[...] = jnp.broadcast_to(jnp.sum(lr, axis=1, keepdims=True), (rows, LANES))

    z2s = [_dot(q_bd, k_refs[p][0, 0].reshape(GROUP_W, PAGE_SIZE).astype(BF16)) + b2 for p in range(n_pp)]
    cs_all, tot_all = _suffix_sums(_log2_rem(jnp.concatenate(z2s, axis=0)), uu_ref[...])
    car = car_ref[...]
    acc = acc_ref[...]
    for p in range(n_pp):
        w = jnp.exp2(z2s[p] + cs_all[p * rows:(p + 1) * rows] + car)
        acc = acc + _dot_nt(w.astype(BF16), v_refs[p][0, 0].reshape(GROUP_W, PAGE_SIZE).astype(BF16))
        car = car + tot_all[p * rows:(p + 1) * rows]
    car_ref[...] = car
    acc_ref[...] = acc

    @pl.when(s == n_steps - 1)
    def _():
        kept = jnp.where(own, acc, 0.0)
        o_ref[...] = sum(kept[h * nq:(h + 1) * nq] for h in range(N_HEADS))


def sb_attention_sample(proj, row_blk0, nq, cache_kt, cache_vt, page_table, layer, sb_bias, uu, n_pp):
    n_batch, n_pages = page_table.shape
    n_steps = n_pages // n_pp
    u8 = jnp.asarray((np.arange(nq)[:, None] >= np.arange(nq)[None, :]).astype(np.float32))

    def page_spec(p):
        return pl.BlockSpec((1, 1, N_HEADS, HEAD_DIM, PAGE_SIZE),
                            lambda b, s, pt: (pt[b, n_pages - 1 - (s * n_pp + p)], layer, 0, 0, 0))

    def row_spec(cblk):
        return pl.BlockSpec((nq, GROUP_W), lambda b, s, pt: (row_blk0 + b, cblk))

    kern = functools.partial(_sb_sample_kernel, n_pp=n_pp, n_steps=n_steps)
    gs = pltpu.PrefetchScalarGridSpec(
        num_scalar_prefetch=1, grid=(n_batch, n_steps),
        in_specs=[pl.BlockSpec(memory_space=pltpu.SMEM), row_spec(C_SQ), row_spec(C_SK), row_spec(C_SV),
                  pl.BlockSpec((nq, nq), lambda b, s, pt: (0, 0)),
                  pl.BlockSpec((2 * LANES, 2 * LANES), lambda b, s, pt: (0, 0))]
                 + [page_spec(p) for p in range(n_pp)] * 2,
        out_specs=pl.BlockSpec((nq, GROUP_W), lambda b, s, pt: (b, 0)),
        scratch_shapes=[pltpu.VMEM((N_HEADS * nq, GROUP_W), F32), pltpu.VMEM((N_HEADS * nq, LANES), F32)])
    return pl.pallas_call(
        kern, grid_spec=gs, out_shape=jax.ShapeDtypeStruct((n_batch * nq, GROUP_W), F32),
        compiler_params=_cp("arbitrary", "arbitrary"), name="sb_sample")(
            page_table, sb_bias, proj, proj, proj, u8, uu, *([cache_kt] * n_pp), *([cache_vt] * n_pp))


def _mlstm_kernel(m0_ref, q_ref, k_ref, v_ref, og_ref, gt_ref, gb_ref, ng_ref, tril_ref, cn0_ref,
                  o_ref, cn_ref, m_ref, ms_ref, *, bb, chunk, n_t):
    bi = pl.program_id(0)
    t = pl.program_id(1)
    seqs = range(bb)
    pairs = [(b, h) for b in seqs for h in range(N_HEADS)]

    @pl.when(t == 0)
    def _():
        cn_ref[...] = cn0_ref[...]
        for b, h in pairs:
            ms_ref[b * N_HEADS + h] = jnp.full((1, LANES), m0_ref[bi * bb + b, h], F32)

    tril = tril_ref[...]
    tri_mask = lax.broadcasted_iota(jnp.int32, (chunk, chunk), 1) <= lax.broadcasted_iota(jnp.int32, (chunk, chunk), 0)
    lane = lax.broadcasted_iota(jnp.int32, (chunk, HEAD_DIM), 1)
    ones_col = jnp.where(lane == 0, 1.0, 0.0).astype(F32)
    hs = lambda h: slice(h * HEAD_DIM, (h + 1) * HEAD_DIM)

    gt = [gt_ref[b] + gb_ref[...] for b in seqs]
    bc = [_cumsum_rows(tril, _log_sigmoid(g)) for g in gt]
    gt_t = [g.T for g in gt]
    bc_t = [x.T for x in bc]
    q = [q_ref[b].astype(BF16) for b in seqs]
    k = [(k_ref[b] * QK_SCALE).astype(BF16) for b in seqs]
    v = [v_ref[b] for b in seqs]
    m_prev = {p: ms_ref[p[0] * N_HEADS + p[1]][:, :1] for p in pairs}
    ig_col = {(b, h): gt[b][:, h:h + 1] for b, h in pairs}
    bc_col = {(b, h): bc[b][:, N_HEADS + h:N_HEADS + h + 1] for b, h in pairs}
    dm = {(b, h): jnp.where(tri_mask, bc_col[b, h] - (bc_t[b][N_HEADS + h:N_HEADS + h + 1, :] - gt_t[b][h:h + 1, :]),
                            -jnp.inf) for b, h in pairs}
    a = {p: bc_col[p] + m_prev[p] for p in pairs}
    m_new = {p: jnp.maximum(a[p], jnp.max(dm[p], axis=1, keepdims=True)) for p in pairs}
    inter = {p: jnp.exp(a[p] - m_new[p]) for p in pairs}
    s = {(b, h): _dot_nt(q[b][:, hs(h)], k[b][:, hs(h)]) * jnp.exp(dm[b, h] - m_new[b, h]) for b, h in pairs}
    v_ext = {(b, h): jnp.concatenate([v[b][:, hs(h)], ones_col], axis=1) for b, h in pairs}
    qc = {(b, h): _dot(q[b][:, hs(h)], cn_ref[b, h].astype(BF16)) for b, h in pairs}
    sv = {p: _dot(s[p].astype(BF16), v_ext[p].astype(BF16)) for p in pairs}
    m_last = {p: m_new[p][chunk - 1:chunk, :] for p in pairs}
    wl = {p: jnp.exp(bc_col[p][chunk - 1:chunk, :] - bc_col[p] + ig_col[p] - m_last[p]) for p in pairs}
    dl = {p: jnp.exp(a[p][chunk - 1:chunk, :] - m_last[p]) for p in pairs}
    upd = {(b, h): _dot_tn(k[b][:, hs(h)], (wl[b, h] * v_ext[b, h]).astype(BF16)) for b, h in pairs}
    rs = {p: _row_sum(s[p])[:, :1] for p in pairs}
    num = {p: inter[p] * qc[p][:, :HEAD_DIM] + sv[p][:, :HEAD_DIM] for p in pairs}
    den = {p: inter[p] * qc[p][:, HEAD_DIM:HEAD_DIM + 1] + rs[p] for p in pairs}
    hh = _head_norm_all({p: num[p] / jnp.maximum(jnp.abs(den[p]), jnp.exp(-m_new[p])) for p in pairs})
    for b, h in pairs:
        cn_ref[b, h] = dl[b, h] * cn_ref[b, h] + upd[b, h]
        ms_ref[b * N_HEADS + h] = jnp.broadcast_to(m_last[b, h], (1, LANES))
    for b in seqs:
        y = jnp.concatenate([hh[b, h] for h in range(N_HEADS)], axis=1)
        o_ref[b] = (y * ng_ref[...] * jax.nn.sigmoid(og_ref[b])).astype(o_ref.dtype)

    @pl.when(t == n_t - 1)
    def _():
        lane_m = lax.broadcasted_iota(jnp.int32, (1, LANES), 1)
        for b in seqs:
            m_out = jnp.zeros((1, LANES), F32)
            for h in range(N_HEADS):
                m_out = jnp.where(lane_m == h, ms_ref[b * N_HEADS + h], m_out)
            m_ref[b] = m_out


def mlstm_mixer(proj3, gate_bias, norm_g, cn0, m0, bb, out_dtype):
    n_batch, seq, _ = proj3.shape
    chunk = math.gcd(seq, CHUNK)
    n_t = seq // chunk
    tril = jnp.asarray(np.tril(np.ones((chunk, chunk), np.float32)), dtype=BF16)

    def row_spec(cblk, w=GROUP_W):
        return pl.BlockSpec((bb, chunk, w), lambda bi, t: (bi, t, cblk))

    def const_spec(shape):
        return pl.BlockSpec(shape, lambda bi, t: (0,) * len(shape))

    state_spec = pl.BlockSpec((bb, N_HEADS, HEAD_DIM, LANES), lambda bi, t: (bi, 0, 0, 0))
    kern = functools.partial(_mlstm_kernel, bb=bb, chunk=chunk, n_t=n_t)
    return pl.pallas_call(
        kern, grid=(n_batch // bb, n_t),
        in_specs=[pl.BlockSpec(memory_space=pltpu.SMEM),
                  row_spec(C_MQ), row_spec(C_MK), row_spec(C_MV), row_spec(C_MO), row_spec(C_GATES, LANES),
                  const_spec((1, LANES)), const_spec((1, GROUP_W)), const_spec((chunk, chunk)), state_spec],
        out_specs=[pl.BlockSpec((bb, chunk, GROUP_W), lambda bi, t: (bi, t, 0)), state_spec,
                   pl.BlockSpec((bb, 1, LANES), lambda bi, t: (bi, 0, 0))],
        out_shape=[jax.ShapeDtypeStruct((n_batch, seq, GROUP_W), out_dtype),
                   jax.ShapeDtypeStruct((n_batch, N_HEADS, HEAD_DIM, LANES), F32),
                   jax.ShapeDtypeStruct((n_batch, 1, LANES), F32)],
        scratch_shapes=[pltpu.VMEM((bb * N_HEADS, 1, LANES), F32)],
        compiler_params=_cp("arbitrary", "arbitrary"), name="mlstm")(
            m0, proj3, proj3, proj3, proj3, proj3, gate_bias, norm_g, tril, cn0)


def _split_dot(x, w):
    hi = x.astype(BF16)
    lo = (x - hi.astype(F32)).astype(BF16)
    return _dot(hi, w) + _dot(lo, w)


def _mlstm_pair_kernel(q_ref, k_ref, v_ref, og_ref, gt_ref, gb_ref, ng_ref, tril_ref, sel_ref, blk_ref,
                       o_ref, cn_ref, st_ref, *, bb, chunk):
    t_id = pl.program_id(1)

    @pl.when(t_id == 0)
    def _():
        cn_ref[...] = jnp.zeros_like(cn_ref)
        st_ref[...] = jnp.zeros_like(st_ref)

    seqs = range(bb)
    n_pair = N_HEADS // 2
    chains = [(b, p) for b in seqs for p in range(n_pair)]
    lane = lax.broadcasted_iota(jnp.int32, (chunk, LANES), 1)
    row = lax.broadcasted_iota(jnp.int32, (chunk, LANES), 0)
    low = lane < HEAD_DIM
    causal = (lane % HEAD_DIM) <= row
    eye2 = (lane % HEAD_DIM) == row
    blk = blk_ref[...]
    same = (lax.broadcasted_iota(jnp.int32, (LANES, LANES), 0) // HEAD_DIM
            == lax.broadcasted_iota(jnp.int32, (LANES, LANES), 1) // HEAD_DIM)
    ones_ll = jnp.ones((chunk, chunk), BF16)
    tril = tril_ref[...]
    pl_ = lambda p: slice(p * LANES, (p + 1) * LANES)
    zero_b = jnp.zeros((), BF16)

    gt = [gt_ref[b] + gb_ref[...] for b in seqs]
    bc = [_cumsum_rows(tril, _log_sigmoid(g)) for g in gt]
    def stacked(fn, xs):
        keys = list(xs)
        y = fn(jnp.concatenate([xs[c] for c in keys], axis=0))
        return {c: y[i * chunk:(i + 1) * chunk] for i, c in enumerate(keys)}

    by_blk = lambda x: _split_dot(x, blk)
    ig_all = stacked(lambda x: _split_dot(x, sel_ref[0]), dict(enumerate(gt)))
    bc_all = stacked(lambda x: _split_dot(x, sel_ref[1]), dict(enumerate(bc)))
    igc = {(b, p): ig_all[b][:, pl_(p)] for b, p in chains}
    bcc = {(b, p): bc_all[b][:, pl_(p)] for b, p in chains}
    gc = {c: igc[c] - bcc[c] for c in chains}
    grow = {c: _split_dot_left(ones_ll, jnp.where(eye2, gc[c], 0.0)) for c in chains}
    gmax = {c: jnp.max(gc[c], axis=0, keepdims=True) for c in chains}
    m_prev = {(b, p): st_ref[b, p, 1:2, :] for b, p in chains}
    a = {c: bcc[c] + m_prev[c] for c in chains}
    m_stab = {c: jnp.maximum(a[c], bcc[c] + gmax[c]) for c in chains}
    inter = {c: jnp.exp(a[c] - m_stab[c]) for c in chains}
    dw = {c: jnp.exp(jnp.where(causal, bcc[c] + grow[c], -jnp.inf) - m_stab[c]) for c in chains}
    q2 = {(b, p): q_ref[b, :, pl_(p)].astype(BF16) for b, p in chains}
    k2f = {(b, p): k_ref[b, :, pl_(p)] * QK_SCALE for b, p in chains}
    k2 = {c: k2f[c].astype(BF16) for c in chains}
    v2 = {(b, p): v_ref[b, :, pl_(p)] for b, p in chains}
    kbd = {c: jnp.concatenate([jnp.where(low, k2[c], zero_b), jnp.where(low, zero_b, k2[c])], axis=0) for c in chains}
    v2b = {c: v2[c].astype(BF16) for c in chains}
    vbd = {c: jnp.concatenate([jnp.where(low, v2b[c], zero_b), jnp.where(low, zero_b, v2b[c])], axis=0)
           for c in chains}
    s = {c: _dot_nt(q2[c], kbd[c]) * dw[c] for c in chains}
    rs = stacked(by_blk, s)
    sv = {c: _dot(s[c].astype(BF16), vbd[c]) for c in chains}
    qc = {(b, p): _dot(q2[b, p], cn_ref[b, p].astype(BF16)) for b, p in chains}
    n_row = {(b, p): st_ref[b, p, 0:1, :] for b, p in chains}
    qn = stacked(by_blk, {c: q2[c].astype(F32) * n_row[c] for c in chains})
    hh = {}
    for c in chains:
        num = inter[c] * qc[c] + sv[c]
        den = inter[c] * qn[c] + rs[c]
        hh[c] = num / jnp.maximum(jnp.abs(den), jnp.exp(-m_stab[c]))
    inv = 1.0 / HEAD_DIM
    mu = stacked(by_blk, hh)
    hc = {c: hh[c] - mu[c] * inv for c in chains}
    var = stacked(by_blk, {c: hc[c] * hc[c] for c in chains})
    m_last = {c: m_stab[c][chunk - 1:chunk, :] for c in chains}
    wl = {c: jnp.exp(bcc[c][chunk - 1:chunk, :] - bcc[c] + igc[c] - m_last[c]) for c in chains}
    dl = {c: jnp.exp(a[c][chunk - 1:chunk, :] - m_last[c]) for c in chains}
    upd = {c: _dot_tn(k2[c], (wl[c] * v2[c]).astype(BF16)) for c in chains}
    for b, p in chains:
        c = (b, p)
        y = hc[c] * lax.rsqrt(var[c] * inv + GN_EPS)
        o_ref[b, :, pl_(p)] = (y * ng_ref[:, pl_(p)] * jax.nn.sigmoid(og_ref[b, :, pl_(p)])).astype(o_ref.dtype)
        cn_ref[b, p] = dl[c] * cn_ref[b, p] + jnp.where(same, upd[c], 0.0)
        st_ref[b, p, 0:1, :] = dl[c] * n_row[c] + jnp.sum(wl[c] * k2f[c], axis=0, keepdims=True)
        st_ref[b, p, 1:2, :] = m_last[c]


def _split_dot_left(w, x):
    hi = x.astype(BF16)
    lo = (x - hi.astype(F32)).astype(BF16)
    return _dot(w, hi) + _dot(w, lo)


def mlstm_mixer_prompt(proj3, gate_bias, norm_g, bb, out_dtype):
    n_batch, seq, _ = proj3.shape
    chunk = CHUNK
    n_pair = N_HEADS // 2
    tril = jnp.asarray(np.tril(np.ones((chunk, chunk), np.float32)), dtype=BF16)
    sel = np.zeros((2, LANES, GROUP_W), np.float32)
    for h in range(N_HEADS):
        sel[0, h, h * HEAD_DIM:(h + 1) * HEAD_DIM] = 1.0
        sel[1, N_HEADS + h, h * HEAD_DIM:(h + 1) * HEAD_DIM] = 1.0
    idx = np.arange(LANES) // HEAD_DIM
    blk = (idx[:, None] == idx[None, :]).astype(np.float32)

    def row_spec(cblk, w=GROUP_W):
        return pl.BlockSpec((bb, chunk, w), lambda bi, t: (bi, t, cblk))

    def const_spec(shape):
        return pl.BlockSpec(shape, lambda bi, t: (0,) * len(shape))

    cn_spec = pl.BlockSpec((bb, n_pair, LANES, LANES), lambda bi, t: (bi, 0, 0, 0))
    st_spec = pl.BlockSpec((bb, n_pair, 8, LANES), lambda bi, t: (bi, 0, 0, 0))
    kern = functools.partial(_mlstm_pair_kernel, bb=bb, chunk=chunk)
    o, cn, st = pl.pallas_call(
        kern, grid=(n_batch // bb, seq // chunk),
        in_specs=[row_spec(C_MQ), row_spec(C_MK), row_spec(C_MV), row_spec(C_MO), row_spec(C_GATES, LANES),
                  const_spec((1, LANES)), const_spec((1, GROUP_W)), const_spec((chunk, chunk)),
                  const_spec(sel.shape), const_spec(blk.shape)],
        out_specs=[pl.BlockSpec((bb, chunk, GROUP_W), lambda bi, t: (bi, t, 0)), cn_spec, st_spec],
        out_shape=[jax.ShapeDtypeStruct((n_batch, seq, GROUP_W), out_dtype),
                   jax.ShapeDtypeStruct((n_batch, n_pair, LANES, LANES), F32),
                   jax.ShapeDtypeStruct((n_batch, n_pair, 8, LANES), F32)],
        compiler_params=_cp("arbitrary", "arbitrary"), name="mlstm_prompt")(
            proj3, proj3, proj3, proj3, proj3, gate_bias, norm_g, tril,
            jnp.asarray(sel, dtype=BF16), jnp.asarray(blk, dtype=BF16))
    c5 = cn.reshape(n_batch, n_pair, 2, HEAD_DIM, 2, HEAD_DIM)
    c_state = jnp.stack([c5[:, :, j, :, j, :] for j in range(2)], axis=2).reshape(n_batch, N_HEADS, HEAD_DIM, HEAD_DIM)
    n_state = st[:, :, 0, :].reshape(n_batch, N_HEADS, HEAD_DIM)
    m_state = st[:, :, 1, :].reshape(n_batch, N_HEADS, HEAD_DIM)[:, :, 0]
    return o, c_state, n_state, m_state


def _rope(x, cos, sin_signed):
    lane = lax.broadcasted_iota(jnp.int32, x.shape, 1)
    half = HEAD_DIM // 2
    swapped = jnp.where((lane % HEAD_DIM) < half, pltpu.roll(x, x.shape[1] - half, 1), pltpu.roll(x, half, 1))
    return x * cos + swapped * sin_signed


def _retention_kernel(q_ref, k_ref, v_ref, gg_ref, cos_ref, sin_ref, ng_ref, dec_ref, int_ref, wl_ref, dl_ref,
                      s0_ref, o_ref, s_ref, *, bb):
    t = pl.program_id(1)
    seqs = range(bb)
    pairs = [(b, h) for b in seqs for h in range(N_HEADS)]
    hs = lambda h: slice(h * HEAD_DIM, (h + 1) * HEAD_DIM)

    @pl.when(t == 0)
    def _():
        s_ref[...] = s0_ref[...]

    cos = cos_ref[...]
    sin = sin_ref[...]
    q = [_rope(q_ref[b], cos, sin).astype(BF16) for b in seqs]
    k = [(_rope(k_ref[b], cos, sin) * QK_SCALE).astype(BF16) for b in seqs]
    v = [v_ref[b] for b in seqs]
    s = {(b, h): _dot_nt(q[b][:, hs(h)], k[b][:, hs(h)]) * dec_ref[h] for b, h in pairs}
    qs = {(b, h): _dot(q[b][:, hs(h)], s_ref[b, h].astype(BF16)) for b, h in pairs}
    sv = {(b, h): _dot(s[b, h].astype(BF16), v[b][:, hs(h)].astype(BF16)) for b, h in pairs}
    upd = {(b, h): _dot_tn(k[b][:, hs(h)], (wl_ref[h] * v[b][:, hs(h)]).astype(BF16)) for b, h in pairs}
    o = _head_norm_all({(b, h): int_ref[h] * qs[b, h] + sv[b, h] for b, h in pairs})
    for b, h in pairs:
        s_ref[b, h] = dl_ref[h] * s_ref[b, h] + upd[b, h]
    for b in seqs:
        gg = gg_ref[b]
        y = jnp.concatenate([o[b, h] for h in range(N_HEADS)], axis=1) * ng_ref[...] * (gg * jax.nn.sigmoid(gg))
        o_ref[b] = y.astype(o_ref.dtype)


def _retention_consts(chunk):
    log_g = np.log(1.0 - np.exp2(-5.0 - np.arange(N_HEADS, dtype=np.float64)))
    tau = np.arange(chunk, dtype=np.float64)
    rel = tau[:, None] - tau[None, :]
    decay = np.where(rel >= 0, np.exp(log_g[:, None, None] * np.maximum(rel, 0.0)), 0.0)
    inter = np.exp(log_g[:, None] * (tau + 1.0))[..., None]
    wl = np.exp(log_g[:, None] * (chunk - 1.0 - tau))[..., None]
    dl = np.exp(log_g * chunk)[:, None, None]
    return tuple(jnp.asarray(a, F32) for a in (decay, inter, wl, dl))


def _retention_pair_kernel(q_ref, k_ref, v_ref, gg_ref, cos_ref, sin_ref, ng_ref, dec_ref, int_ref, wl_ref, dl_ref,
                           swap_ref, blk_ref, o_ref, s_ref, *, bb, chunk):
    t_id = pl.program_id(1)

    @pl.when(t_id == 0)
    def _():
        s_ref[...] = jnp.zeros_like(s_ref)

    seqs = range(bb)
    n_pair = N_HEADS // 2
    chains = [(b, p) for b in seqs for p in range(n_pair)]
    low = lax.broadcasted_iota(jnp.int32, (chunk, LANES), 1) < HEAD_DIM
    same = (lax.broadcasted_iota(jnp.int32, (LANES, LANES), 0) // HEAD_DIM
            == lax.broadcasted_iota(jnp.int32, (LANES, LANES), 1) // HEAD_DIM)
    pl_ = lambda p: slice(p * LANES, (p + 1) * LANES)
    zero_b = jnp.zeros((), BF16)
    blk = blk_ref[...]
    cos = cos_ref[...]
    sin = sin_ref[...]

    def stacked(fn, xs):
        keys = list(xs)
        y = fn(jnp.concatenate([xs[c] for c in keys], axis=0))
        return {c: y[i * chunk:(i + 1) * chunk] for i, c in enumerate(keys)}

    swap = lambda x: _split_dot(x, swap_ref[...])
    qf = {b: q_ref[b] for b in seqs}
    kf = {b: k_ref[b] for b in seqs}
    q_sw = stacked(swap, qf)
    k_sw = stacked(swap, kf)
    q = {b: (qf[b] * cos + q_sw[b] * sin).astype(BF16) for b in seqs}
    k = {b: ((kf[b] * cos + k_sw[b] * sin) * QK_SCALE).astype(BF16) for b in seqs}
    k2 = {(b, p): k[b][:, pl_(p)] for b, p in chains}
    v2 = {(b, p): v_ref[b, :, pl_(p)] for b, p in chains}
    v2b = {c: v2[c].astype(BF16) for c in chains}
    kbd = {c: jnp.concatenate([jnp.where(low, k2[c], zero_b), jnp.where(low, zero_b, k2[c])], axis=0) for c in chains}
    vbd = {c: jnp.concatenate([jnp.where(low, v2b[c], zero_b), jnp.where(low, zero_b, v2b[c])], axis=0)
           for c in chains}
    s = {(b, p): _dot_nt(q[b][:, pl_(p)], kbd[b, p]) * dec_ref[p] for b, p in chains}
    qs = {(b, p): _dot(q[b][:, pl_(p)], s_ref[b, p].astype(BF16)) for b, p in chains}
    sv = {c: _dot(s[c].astype(BF16), vbd[c]) for c in chains}
    upd = {(b, p): _dot_tn(k2[b, p], (wl_ref[p] * v2[b, p]).astype(BF16)) for b, p in chains}
    o = {(b, p): int_ref[p] * qs[b, p] + sv[b, p] for b, p in chains}
    inv = 1.0 / HEAD_DIM
    by_blk = lambda x: _split_dot(x, blk)
    mu = stacked(by_blk, o)
    oc = {c: o[c] - mu[c] * inv for c in chains}
    var = stacked(by_blk, {c: oc[c] * oc[c] for c in chains})
    for b, p in chains:
        c = (b, p)
        gg = gg_ref[b, :, pl_(p)]
        y = oc[c] * lax.rsqrt(var[c] * inv + GN_EPS) * ng_ref[:, pl_(p)] * (gg * jax.nn.sigmoid(gg))
        o_ref[b, :, pl_(p)] = y.astype(o_ref.dtype)
        s_ref[b, p] = dl_ref[p] * s_ref[b, p] + jnp.where(same, upd[c], 0.0)


def retention_mixer_prompt(proj3, cos, sin_signed, norm_g, bb, out_dtype):
    n_batch, seq, _ = proj3.shape
    chunk = CHUNK
    n_pair = N_HEADS // 2
    dec, inter, wl, dl = _retention_consts(chunk)
    rep = lambda a: jnp.broadcast_to(a, a.shape[:-1] + (HEAD_DIM,))
    pair = lambda a: jnp.concatenate([a[0::2], a[1::2]], axis=-1)
    dec2, int2, wl2, dl2 = pair(dec), pair(rep(inter)), pair(rep(wl)), pair(rep(dl))
    src = np.arange(GROUP_W)
    partner = (src // HEAD_DIM) * HEAD_DIM + (src % HEAD_DIM + HEAD_DIM // 2) % HEAD_DIM
    swap = np.zeros((GROUP_W, GROUP_W), np.float32)
    swap[partner, src] = 1.0
    idx = np.arange(LANES) // HEAD_DIM
    blk = (idx[:, None] == idx[None, :]).astype(np.float32)

    def row_spec(cblk):
        return pl.BlockSpec((bb, chunk, GROUP_W), lambda bi, t: (bi, t, cblk))

    def const_spec(shape):
        return pl.BlockSpec(shape, lambda bi, t: (0,) * len(shape))

    pos_spec = pl.BlockSpec((chunk, GROUP_W), lambda bi, t: (t, 0))
    state_spec = pl.BlockSpec((bb, n_pair, LANES, LANES), lambda bi, t: (bi, 0, 0, 0))
    kern = functools.partial(_retention_pair_kernel, bb=bb, chunk=chunk)
    o, sbd = pl.pallas_call(
        kern, grid=(n_batch // bb, seq // chunk),
        in_specs=[row_spec(C_RQ), row_spec(C_RK), row_spec(C_RV), row_spec(C_RG), pos_spec, pos_spec,
                  const_spec((1, GROUP_W)), const_spec(dec2.shape), const_spec(int2.shape), const_spec(wl2.shape),
                  const_spec(dl2.shape), const_spec(swap.shape), const_spec(blk.shape)],
        out_specs=[pl.BlockSpec((bb, chunk, GROUP_W), lambda bi, t: (bi, t, 0)), state_spec],
        out_shape=[jax.ShapeDtypeStruct((n_batch, seq, GROUP_W), out_dtype),
                   jax.ShapeDtypeStruct((n_batch, n_pair, LANES, LANES), F32)],
        compiler_params=_cp("arbitrary", "arbitrary"), name="retention_prompt")(
            proj3, proj3, proj3, proj3, cos, sin_signed, norm_g, dec2, int2, wl2, dl2,
            jnp.asarray(swap, dtype=BF16), jnp.asarray(blk, dtype=BF16))
    s5d = sbd.reshape(n_batch, n_pair, 2, HEAD_DIM, 2, HEAD_DIM)
    state = jnp.stack([s5d[:, :, j, :, j, :] for j in range(2)], axis=2).reshape(n_batch, N_HEADS, HEAD_DIM, HEAD_DIM)
    return o, state


def retention_mixer(proj3, cos, sin_signed, norm_g, s0, bb, out_dtype):
    n_batch, seq, _ = proj3.shape
    chunk = math.gcd(seq, CHUNK)
    dec, inter, wl, dl = _retention_consts(chunk)

    def row_spec(cblk):
        return pl.BlockSpec((bb, chunk, GROUP_W), lambda bi, t: (bi, t, cblk))

    def const_spec(shape):
        return pl.BlockSpec(shape, lambda bi, t: (0,) * len(shape))

    pos_spec = pl.BlockSpec((chunk, GROUP_W), lambda bi, t: (t, 0))
    state_spec = pl.BlockSpec((bb, N_HEADS, HEAD_DIM, HEAD_DIM), lambda bi, t: (bi, 0, 0, 0))
    kern = functools.partial(_retention_kernel, bb=bb)
    return pl.pallas_call(
        kern, grid=(n_batch // bb, seq // chunk),
        in_specs=[row_spec(C_RQ), row_spec(C_RK), row_spec(C_RV), row_spec(C_RG), pos_spec, pos_spec,
                  const_spec((1, GROUP_W)), const_spec(dec.shape), const_spec(inter.shape), const_spec(wl.shape),
                  const_spec(dl.shape), state_spec],
        out_specs=[pl.BlockSpec((bb, chunk, GROUP_W), lambda bi, t: (bi, t, 0)), state_spec],
        out_shape=[jax.ShapeDtypeStruct((n_batch, seq, GROUP_W), out_dtype),
                   jax.ShapeDtypeStruct((n_batch, N_HEADS, HEAD_DIM, HEAD_DIM), F32)],
        compiler_params=_cp("arbitrary", "arbitrary"), name="retention")(
            proj3, proj3, proj3, proj3, cos, sin_signed, norm_g, dec, inter, wl, dl, s0)


def _s5_kernel(u_ref, wb_ref, a1_ref, a2_ref, h0_ref, wc_ref, d_ref, gw_ref, gb_ref, o_ref, hl_ref,
               hs_ref, ut_ref, yt_ref, *, nb, tt):
    c = pl.program_id(0)

    @pl.when(c == 0)
    def _():
        hl_ref[...] = h0_ref[...]

    halves = range(GROUP_W // LANES)
    for b in range(nb):
        ub = u_ref[b]
        for hf in halves:
            ut_ref[hf, pl.ds(b, tt, stride=nb), :] = ub[:, hf * LANES:(hf + 1) * LANES]
    u = jnp.concatenate([ut_ref[hf] for hf in halves], axis=1)
    hs_ref[...] = _dot(u.astype(BF16), wb_ref[...])
    a1 = jnp.broadcast_to(a1_ref[...], (nb, 2 * S5_W))
    a2 = jnp.broadcast_to(a2_ref[...], (nb, 2 * S5_W))

    def step(t, h):
        r0 = pl.multiple_of(t * nb, nb)
        swapped = jnp.concatenate([h[:, S5_W:], h[:, :S5_W]], axis=1)
        h = a1 * h + a2 * swapped + hs_ref[pl.ds(r0, nb), :]
        hs_ref[pl.ds(r0, nb), :] = h
        return h

    hl_ref[...] = lax.fori_loop(0, tt, step, hl_ref[...])
    y = _dot(hs_ref[...].astype(BF16), wc_ref[...]) + d_ref[...] * u
    g5 = jax.nn.gelu(y)
    yt = g5 * jax.nn.sigmoid(_dot(g5.astype(BF16), gw_ref[...]) + gb_ref[...])
    for hf in halves:
        yt_ref[hf] = yt[:, hf * LANES:(hf + 1) * LANES]
    for b in range(nb):
        o_ref[b] = jnp.concatenate([yt_ref[hf, pl.ds(b, tt, stride=nb), :] for hf in halves],
                                   axis=1).astype(o_ref.dtype)


def s5_mixer(proj3, tt, wb, a1, a2, h0, wc, d, glu_w, glu_b, out_dtype):
    nb, seq, _ = proj3.shape
    rows = tt * nb

    def const_spec(shape):
        return pl.BlockSpec(shape, lambda c: (0,) * len(shape))

    kern = functools.partial(_s5_kernel, nb=nb, tt=tt)
    return pl.pallas_call(
        kern, grid=(seq // tt,),
        in_specs=[pl.BlockSpec((nb, tt, GROUP_W), lambda c: (0, c, C_SU)),
                  const_spec(wb.shape), const_spec(a1.shape), const_spec(a2.shape), const_spec(h0.shape),
                  const_spec(wc.shape), const_spec(d.shape), const_spec(glu_w.shape), const_spec(glu_b.shape)],
        out_specs=[pl.BlockSpec((nb, tt, GROUP_W), lambda c: (0, c, 0)), const_spec(h0.shape)],
        out_shape=[jax.ShapeDtypeStruct((nb, seq, GROUP_W), out_dtype), jax.ShapeDtypeStruct(h0.shape, F32)],
        scratch_shapes=[pltpu.VMEM((rows, 2 * S5_W), F32), pltpu.VMEM((GROUP_W // LANES, rows, LANES), F32),
                        pltpu.VMEM((GROUP_W // LANES, rows, LANES), F32)],
        compiler_params=_cp("arbitrary"), name="s5")(proj3, wb, a1, a2, h0, wc, d, glu_w, glu_b)


def _s5_weights(a_re, a_im, log_dt, b_re, b_im, c_re, c_im):
    dt = jnp.exp(log_dt)
    mag = jnp.exp(a_re * dt)
    ar, ai = mag * jnp.cos(a_im * dt), mag * jnp.sin(a_im * dt)
    den = a_re * a_re + a_im * a_im
    cr = ((ar - 1.0) * a_re + ai * a_im) / den
    ci = (ai * a_re - (ar - 1.0) * a_im) / den
    bb_re = cr[..., None] * b_re - ci[..., None] * b_im
    bb_im = cr[..., None] * b_im + ci[..., None] * b_re
    eye = jnp.eye(S5_GROUPS, dtype=F32)

    def in_map(m):
        return jnp.einsum('gpc,gh->gchp', m, eye).reshape(S5_GROUPS * S5_GROUP, S5_W)

    def out_map(m):
        return jnp.einsum('gcp,gh->gphc', m, eye).reshape(S5_W, S5_GROUPS * S5_GROUP)

    wb = jnp.concatenate([in_map(bb_re), in_map(bb_im)], axis=1).astype(BF16)
    wc = jnp.concatenate([out_map(c_re), -out_map(c_im)], axis=0).astype(BF16)
    ar = ar.reshape(1, S5_W)
    ai = ai.reshape(1, S5_W)
    return wb, jnp.concatenate([ar, ar], axis=1), jnp.concatenate([-ai, ai], axis=1), wc


def _cross_attn_kernel(x_ref, wq_ref, k_ref, v_ref, wo_ref, g_ref, b_ref, o_ref, *, nb):
    x = x_ref[...]
    rows = x.shape[0] // nb
    q = (_dot(x.astype(BF16), wq_ref[...]) * QK_SCALE).astype(BF16)
    hs = [slice(h * HEAD_DIM, (h + 1) * HEAD_DIM) for h in range(N_HEADS)]
    pairs = [(b, h) for b in range(nb) for h in range(N_HEADS)]
    k = [k_ref[b].astype(BF16) for b in range(nb)]
    v = [v_ref[b].astype(BF16) for b in range(nb)]
    s = {(b, h): _dot_nt(q[b * rows:(b + 1) * rows, hs[h]], k[b][:, hs[h]]) for b, h in pairs}
    e = {c: jnp.exp(s[c] - jnp.max(s[c], axis=1, keepdims=True)) for c in pairs}
    p = {c: e[c] / jnp.sum(e[c], axis=1, keepdims=True) for c in pairs}
    pv = {(b, h): _dot(p[b, h].astype(BF16), v[b][:, hs[h]]) for b, h in pairs}
    o = jnp.concatenate([jnp.concatenate([pv[b, h] for h in range(N_HEADS)], axis=1) for b in range(nb)], axis=0)
    y = ALPHA * x + _dot(o.astype(BF16), wo_ref[...])
    o_ref[...] = _layer_norm(y, g_ref[...], b_ref[...])


def cross_attn_ln(x, row_blk0, n_batch, seq, tq, nb, mem_k, mem_v, wq, wo, g, b):
    nq = seq // tq if nb == 1 else 1

    def const_spec(shape):
        return pl.BlockSpec(shape, lambda bb, i: (0,) * len(shape))

    row_spec = pl.BlockSpec((tq, D_MODEL), lambda bb, i: (row_blk0 + bb * nq + i, 0))
    mem_spec = pl.BlockSpec((nb, N_MEM, GROUP_W), lambda bb, i: (bb, 0, 0))
    return pl.pallas_call(
        functools.partial(_cross_attn_kernel, nb=nb), grid=(n_batch // nb, nq),
        in_specs=[row_spec, const_spec(wq.shape), mem_spec, mem_spec, const_spec(wo.shape),
                  const_spec(g.shape), const_spec(b.shape)],
        out_specs=row_spec, out_shape=jax.ShapeDtypeStruct(x.shape, F32), input_output_aliases={0: 0},
        compiler_params=_cp("arbitrary", "arbitrary"), name="cross_attn")(x, wq, mem_k, mem_v, wo, g, b)


SWIGLU_ROWS = 256


def _swiglu_accumulate(xb_ref, wg, wu, wd, acc_ref, n_valid=None):
    wgb, wub, wdb = wg.astype(BF16), wu.astype(BF16), wd.astype(BF16)
    n_sub = xb_ref.shape[0] // SWIGLU_ROWS

    def hidden(r):
        xb = xb_ref[pl.ds(r * SWIGLU_ROWS, SWIGLU_ROWS), :]
        gate = _dot(xb, wgb)
        up = _dot(xb, wub)
        return (gate * jax.nn.sigmoid(gate) * up).astype(BF16)

    def first_sub_blocks(n):
        hid = hidden(0)
        for r in range(n):
            nxt = hidden(r + 1) if r + 1 < n else None
            acc_ref[pl.ds(r * SWIGLU_ROWS, SWIGLU_ROWS), :] += _dot(hid, wdb)
            hid = nxt

    if n_valid is None:
        first_sub_blocks(n_sub)
        return
    need = (n_valid + SWIGLU_ROWS - 1) // SWIGLU_ROWS
    for n in range(1, n_sub + 1):
        pl.when(need == n)(functools.partial(first_sub_blocks, n))


def _ffn_kernel(x_ref, wg_ref, wu_ref, wd_ref, g_ref, b_ref, o_ref, xb_ref, *, nf):
    j = pl.program_id(1)

    @pl.when(j == 0)
    def _():
        xb_ref[...] = x_ref[...].astype(BF16)
        o_ref[...] = jnp.zeros_like(o_ref)

    _swiglu_accumulate(xb_ref, wg_ref[...], wu_ref[...], wd_ref[...], o_ref)

    @pl.when(j == nf - 1)
    def _():
        o_ref[...] = _layer_norm(ALPHA * x_ref[...] + o_ref[...], g_ref[...], b_ref[...])


def ffn_ln(x, wg, wu, wd, g, b, tm, tf):
    m = x.shape[0]
    nf = D_FF // tf
    kern = functools.partial(_ffn_kernel, nf=nf)
    return pl.pallas_call(
        kern, grid=(m // tm, nf),
        in_specs=[pl.BlockSpec((tm, D_MODEL), lambda i, j: (i, 0)),
                  pl.BlockSpec((D_MODEL, tf), lambda i, j: (0, j)), pl.BlockSpec((D_MODEL, tf), lambda i, j: (0, j)),
                  pl.BlockSpec((tf, D_MODEL), lambda i, j: (j, 0)),
                  pl.BlockSpec((1, D_MODEL), lambda i, j: (0, 0)), pl.BlockSpec((1, D_MODEL), lambda i, j: (0, 0))],
        out_specs=pl.BlockSpec((tm, D_MODEL), lambda i, j: (i, 0)),
        out_shape=jax.ShapeDtypeStruct((m, D_MODEL), F32),
        scratch_shapes=[pltpu.VMEM((tm, D_MODEL), BF16)],
        compiler_params=_cp("arbitrary", "arbitrary"), name="ffn")(x, wg, wu, wd, g, b)


def _router_kernel(x_ref, w_ref, b_ref, lt_ref, o_ref, before_ref, cnt_ref):
    @pl.when(pl.program_id(0) == 0)
    def _():
        cnt_ref[...] = jnp.zeros_like(cnt_ref)

    logits = jnp.dot(x_ref[...], w_ref[...], preferred_element_type=F32, precision=lax.Precision.HIGHEST) + b_ref[...]
    lane = lax.broadcasted_iota(jnp.int32, logits.shape, 1)
    neg = jnp.float32(-jnp.inf)
    lg = jnp.where(lane < N_EXPERTS, logits, neg)
    m1 = jnp.max(lg, axis=1, keepdims=True)
    i1 = jnp.min(jnp.where(lg == m1, lane, LANES), axis=1, keepdims=True)
    lg2 = jnp.where(lane == i1, neg, lg)
    m2 = jnp.max(lg2, axis=1, keepdims=True)
    i2 = jnp.min(jnp.where(lg2 == m2, lane, LANES), axis=1, keepdims=True)
    e2 = jnp.exp(m2 - m1)
    g1 = 1.0 / (1.0 + e2)
    g2 = e2 / (1.0 + e2)
    out = jnp.where(lane == 0, i1.astype(F32), jnp.where(lane == 1, i2.astype(F32),
                    jnp.where(lane == 2, g1, jnp.where(lane == 3, g2, 0.0))))
    o_ref[...] = out
    chosen = jnp.where(jnp.logical_or(lane == i1, lane == i2), 1.0, 0.0)
    cnt = cnt_ref[0:1, :]
    before_ref[...] = _dot(lt_ref[...], chosen.astype(BF16)) + cnt
    cnt_ref[...] = jnp.broadcast_to(cnt + jnp.sum(chosen, axis=0, keepdims=True), cnt_ref.shape)


def router(x, w_pad, b_pad, tm):
    m = x.shape[0]
    lower = jnp.asarray(np.tril(np.ones((tm, tm), np.float32), -1), dtype=BF16)
    row = pl.BlockSpec((tm, LANES), lambda i: (i, 0))
    return pl.pallas_call(
        _router_kernel, grid=(m // tm,),
        in_specs=[pl.BlockSpec((tm, D_MODEL), lambda i: (i, 0)), pl.BlockSpec((D_MODEL, LANES), lambda i: (0, 0)),
                  pl.BlockSpec((1, LANES), lambda i: (0, 0)), pl.BlockSpec((tm, tm), lambda i: (0, 0))],
        out_specs=[row, row, pl.BlockSpec((8, LANES), lambda i: (0, 0))],
        out_shape=[jax.ShapeDtypeStruct((m, LANES), F32), jax.ShapeDtypeStruct((m, LANES), F32),
                   jax.ShapeDtypeStruct((8, LANES), F32)],
        compiler_params=_cp("arbitrary"), name="router")(x, w_pad, b_pad, lower)


def _moe_ffn_kernel(te_ref, nu_ref, tr_ref, x_ref, wg_ref, wu_ref, wd_ref, o_ref, xb_ref):
    i = pl.program_id(0)
    j = pl.program_id(1)
    used = i < nu_ref[0]

    @pl.when(used)
    def _():
        @pl.when(j == 0)
        def _():
            xb_ref[...] = x_ref[...].astype(BF16)
            o_ref[...] = jnp.zeros_like(o_ref)

        _swiglu_accumulate(xb_ref, wg_ref[0], wu_ref[0], wd_ref[0], o_ref, tr_ref[i])

    @pl.when(jnp.logical_and(jnp.logical_not(used), j == 0))
    def _():
        o_ref[...] = jnp.zeros_like(o_ref)


def moe_ffn(x_sorted, tile_expert, n_used, tile_rows, wg, wu, wd, tf):
    n_rows = x_sorted.shape[0]
    n_tiles = n_rows // MOE_TILE
    nf = D_FF // tf

    def jj(i, j, nu):
        return jnp.where(i < nu[0], j, nf - 1)

    gs = pltpu.PrefetchScalarGridSpec(
        num_scalar_prefetch=3, grid=(n_tiles, nf),
        in_specs=[pl.BlockSpec((MOE_TILE, D_MODEL), lambda i, j, te, nu, tr: (i, 0)),
                  pl.BlockSpec((1, D_MODEL, tf), lambda i, j, te, nu, tr: (te[i], 0, jj(i, j, nu))),
                  pl.BlockSpec((1, D_MODEL, tf), lambda i, j, te, nu, tr: (te[i], 0, jj(i, j, nu))),
                  pl.BlockSpec((1, tf, D_MODEL), lambda i, j, te, nu, tr: (te[i], jj(i, j, nu), 0))],
        out_specs=pl.BlockSpec((MOE_TILE, D_MODEL), lambda i, j, te, nu, tr: (i, 0)),
        scratch_shapes=[pltpu.VMEM((MOE_TILE, D_MODEL), BF16)])
    return pl.pallas_call(
        _moe_ffn_kernel, grid_spec=gs, out_shape=jax.ShapeDtypeStruct((n_rows, D_MODEL), F32),
        compiler_params=_cp("arbitrary", "arbitrary"), name="moe_ffn")(
            tile_expert, n_used, tile_rows, x_sorted, wg, wu, wd)


def _combine_ln_kernel(x_ref, r_ref, ya_ref, yb_ref, g_ref, b_ref, o_ref):
    r = r_ref[...]
    y = r[:, TOP_K:TOP_K + 1] * ya_ref[...] + r[:, TOP_K + 1:TOP_K + 2] * yb_ref[...]
    o_ref[...] = _layer_norm(ALPHA * x_ref[...] + y, g_ref[...], b_ref[...])


def combine_ln(x, r, ya, yb, g, b, tm, row_blk0, rows):
    row = pl.BlockSpec((tm, D_MODEL), lambda i: (row_blk0 + i, 0))
    vec = pl.BlockSpec((1, D_MODEL), lambda i: (0, 0))
    return pl.pallas_call(
        _combine_ln_kernel, grid=(rows // tm,),
        in_specs=[row, pl.BlockSpec((tm, LANES), lambda i: (row_blk0 + i, 0)), row, row, vec, vec],
        out_specs=pl.BlockSpec((tm, D_MODEL), lambda i: (i, 0)),
        out_shape=jax.ShapeDtypeStruct((rows, D_MODEL), F32),
        compiler_params=_cp("arbitrary"), name="combine_ln")(x, r, ya, yb, g, b)


def moe_ln(x, router_w, router_b, wg, wu, wd, g, b, tm, splits):
    m = x.shape[0]
    w_pad = jnp.zeros((D_MODEL, LANES), F32).at[:, :N_EXPERTS].set(router_w)
    b_pad = jnp.zeros((1, LANES), F32).at[0, :N_EXPERTS].set(router_b)
    r, before_f, counts_f = router(x, w_pad, b_pad, tm)
    top_idx = r[:, :TOP_K].astype(jnp.int32)
    n_slot = m * TOP_K
    onehot = (top_idx[:, :, None] == jnp.arange(N_EXPERTS, dtype=jnp.int32)).astype(jnp.int32)
    before = before_f[:, :N_EXPERTS].astype(jnp.int32)
    counts = counts_f[0, :N_EXPERTS].astype(jnp.int32)
    tiles_per = (counts + MOE_TILE - 1) // MOE_TILE
    tile_end = jnp.cumsum(tiles_per)
    row0 = (tile_end - tiles_per) * MOE_TILE
    dest = jnp.sum(onehot * (before + row0)[:, None, :], axis=2)
    n_tiles = -(-n_slot // MOE_TILE) + N_EXPERTS
    n_rows = n_tiles * MOE_TILE
    row_tok = (jnp.arange(n_rows, dtype=jnp.int32) % m).at[dest.reshape(-1)].set(
        jnp.arange(n_slot, dtype=jnp.int32) // TOP_K, unique_indices=True, mode='promise_in_bounds')
    n_used = tile_end[-1:].astype(jnp.int32)
    tile_ids = jnp.minimum(jnp.arange(n_tiles, dtype=jnp.int32), n_used[0] - 1)
    tile_expert = jnp.minimum(jnp.sum((tile_end[None, :] <= tile_ids[:, None]).astype(jnp.int32), axis=1),
                              N_EXPERTS - 1)
    tile_rows = jnp.clip(counts[tile_expert] - (tile_ids - (tile_end - tiles_per)[tile_expert]) * MOE_TILE,
                         0, MOE_TILE).astype(jnp.int32)
    x_sorted = x.at[row_tok].get(mode='promise_in_bounds')
    y_sorted = moe_ffn(x_sorted, tile_expert, n_used, tile_rows, wg, wu, wd, FFN_COL_TILE)
    ya = y_sorted.at[dest[:, 0]].get(mode='promise_in_bounds')
    yb = y_sorted.at[dest[:, 1]].get(mode='promise_in_bounds')
    return [combine_ln(x, r, ya, yb, g, b, t, blk0, rows) for blk0, rows, t in splits]


def kernel(x_prompt, x_sample, cache_sb_k, cache_sb_v, cache_mem_k, cache_mem_v, state_ml_C, state_ml_n, state_ml_m, state_rt_S, state_s5_re, state_s5_im, page_table, mem_prompt, w_in, sb_bias, ml_b_i, ml_b_f, ml_norm_g, rt_norm_g, s5_A_re, s5_A_im, s5_log_dt, s5_B_re, s5_B_im, s5_C_re, s5_C_im, s5_D, s5_glu_w, s5_glu_b, w_out, ca_wq, ca_wk, ca_wv, ca_wo, ln_g, ln_b, ffn_w_gate, ffn_w_up, ffn_w_down, moe_router_w, moe_router_b, moe_w_gate, moe_w_up, moe_w_down):
    bp, tp, _ = x_prompt.shape
    bs, ts, _ = x_sample.shape
    n_p, n_s = bp * tp, bs * ts
    tm = ROW_TILE
    assert (n_p + n_s) % FFN_ROW_TILE == 0 and n_p % n_s == 0 and tp % PROMPT_TILE == 0
    assert bp % SEQS_PER_STEP == 0 and bs % SEQS_PER_STEP == 0 and n_p % (SEQS_PER_STEP * ts) == 0
    x = jnp.concatenate([x_prompt.reshape(n_p, D_MODEL), x_sample.reshape(n_s, D_MODEL)], axis=0)
    uu = _suffix_matrix()
    g_off = 7 * GROUP_W
    half = HEAD_DIM // 2
    freq = ROPE_BASE ** (-jnp.arange(half, dtype=F32) / half)

    def rope_tables(pos):
        ang = pos.astype(F32)[:, None] * freq[None, :]
        cos, sin = jnp.cos(ang), jnp.sin(ang)
        return (jnp.tile(jnp.concatenate([cos, cos], axis=1), (1, N_HEADS)),
                jnp.tile(jnp.concatenate([-sin, sin], axis=1), (1, N_HEADS)))

    cos_p, sin_p = rope_tables(jnp.arange(tp, dtype=jnp.int32))
    cos_s, sin_s = rope_tables(PAST_LEN + jnp.arange(ts, dtype=jnp.int32))
    cache_kt = cache_sb_k.transpose(0, 1, 3, 4, 2)
    cache_vt = cache_sb_v.transpose(0, 1, 3, 4, 2)

    p_st = [[] for _ in range(10)]
    s_st = [[] for _ in range(8)]
    for l in range(DEPTH):
        wl = w_in[l]
        w_cat = jnp.concatenate([wl[:, :g_off], wl[:, g_off + 2 * N_HEADS:], wl[:, g_off:g_off + 2 * N_HEADS],
                                 jnp.zeros((D_MODEL, PROJ_W - wl.shape[1]), F32)], axis=1).astype(BF16)
        proj_p, kt_p, vt_p = in_proj_prompt(x, w_cat, bp, tp, PROMPT_TILE)
        proj_s = linear(x, w_cat, n_s, n_p // n_s, n_s)
        proj_p3 = proj_p.reshape(bp, tp, PROJ_W)
        proj_s3 = proj_s.reshape(bs, ts, PROJ_W)
        gate_bias = jnp.zeros((1, LANES), F32).at[0, :2 * N_HEADS].set(jnp.concatenate([ml_b_i[l], ml_b_f[l]]))
        ml_g = ml_norm_g[l][None, :]
        rt_g = rt_norm_g[l][None, :]
        wb, a1, a2, wc = _s5_weights(s5_A_re[l], s5_A_im[l], s5_log_dt[l], s5_B_re[l], s5_B_im[l],
                                     s5_C_re[l], s5_C_im[l])
        s5_d = s5_D[l][None, :]
        glu_w = s5_glu_w[l].astype(BF16)
        glu_b = s5_glu_b[l][None, :]

        o_sb_p = sb_attention_prompt(proj_p, sb_bias[l], uu, bp, tp, SB_QUERY_TILE)
        o_ml_p, ml_c_p, ml_n_p, ml_m_p = mlstm_mixer_prompt(proj_p3, gate_bias, ml_g, SEQS_PER_STEP, MIX_DTYPE)
        o_rt_p, rs_p = retention_mixer_prompt(proj_p3, cos_p, sin_p, rt_g, SEQS_PER_STEP, MIX_DTYPE)
        o_ml_p = o_ml_p.reshape(n_p, GROUP_W)
        o_rt_p = o_rt_p.reshape(n_p, GROUP_W)
        o_s5_p, h5_p = s5_mixer(proj_p3, S5_STEPS, wb, a1, a2, jnp.zeros((bp, 2 * S5_W), F32), wc, s5_d, glu_w,
                                glu_b, MIX_DTYPE)
        o_s5_p = o_s5_p.reshape(n_p, GROUP_W)

        o_sb_s = sb_attention_sample(proj_s, 0, ts, cache_kt, cache_vt, page_table, l, sb_bias[l], uu,
                                     SB_PAGES_PER_STEP)
        cn0 = jnp.concatenate([state_ml_C[:, l], state_ml_n[:, l][..., None],
                               jnp.zeros((bs, N_HEADS, HEAD_DIM, LANES - HEAD_DIM - 1), F32)], axis=-1)
        o_ml_s, cn_s, m_s = mlstm_mixer(proj_s3, gate_bias, ml_g, cn0, state_ml_m[:, l], SEQS_PER_STEP, F32)
        o_rt_s, rs_s = retention_mixer(proj_s3, cos_s, sin_s, rt_g, state_rt_S[:, l], SEQS_PER_STEP, F32)
        o_ml_s = o_ml_s.reshape(n_s, GROUP_W)
        o_rt_s = o_rt_s.reshape(n_s, GROUP_W)
        h0_s = jnp.concatenate([state_s5_re[:, l].reshape(bs, S5_W), state_s5_im[:, l].reshape(bs, S5_W)], axis=1)
        o_s5_s, h5_s = s5_mixer(proj_s3, ts, wb, a1, a2, h0_s, wc, s5_d, glu_w, glu_b, F32)
        o_s5_s = o_s5_s.reshape(n_s, GROUP_W)

        wo_mix = w_out[l].astype(BF16)
        g0, b0 = ln_g[l, 0][None, :], ln_b[l, 0][None, :]
        x = mix_out_ln(x, (o_sb_p, o_ml_p, o_rt_p, o_s5_p), wo_mix, g0, b0, PROMPT_TILE, 0)
        x = mix_out_ln(x, (o_sb_s, o_ml_s, o_rt_s, o_s5_s), wo_mix, g0, b0, n_s, n_p // n_s)

        mem_kv = linear(mem_prompt.reshape(bp * N_MEM, D_MODEL),
                        jnp.concatenate([ca_wk[l], ca_wv[l]], axis=1).astype(BF16), PROMPT_TILE)
        mk_p = mem_kv[:, :GROUP_W].reshape(bp, N_MEM, GROUP_W)
        mv_p = mem_kv[:, GROUP_W:].reshape(bp, N_MEM, GROUP_W)
        wq = ca_wq[l].astype(BF16)
        wo = ca_wo[l].astype(BF16)
        g1, b1 = ln_g[l, 1][None, :], ln_b[l, 1][None, :]
        x = cross_attn_ln(x, 0, bp, tp, PROMPT_TILE, 1, mk_p, mv_p, wq, wo, g1, b1)
        s_tile = SEQS_PER_STEP * ts
        x = cross_attn_ln(x, n_p // s_tile, bs, ts, s_tile, SEQS_PER_STEP,
                          cache_mem_k[:, l].reshape(bs, N_MEM, GROUP_W),
                          cache_mem_v[:, l].reshape(bs, N_MEM, GROUP_W), wq, wo, g1, b1)

        g2, b2 = ln_g[l, 2][None, :], ln_b[l, 2][None, :]
        j = l // 2
        last = l == DEPTH - 1
        if l % 2 == 0:
            x = ffn_ln(x, ffn_w_gate[j], ffn_w_up[j], ffn_w_down[j], g2, b2, FFN_ROW_TILE, FFN_COL_TILE)
            y_out = (x[:n_p], x[n_p:]) if last else None
        else:
            splits = [(0, n_p, PROMPT_TILE), (n_p // n_s, n_s, n_s)] if last else [(0, n_p + n_s, tm)]
            y_out = moe_ln(x, moe_router_w[j], moe_router_b[j], moe_w_gate[j], moe_w_up[j], moe_w_down[j],
                           g2, b2, tm, splits)
            x = None if last else y_out[0]

        def heads(a, nb_, t_):
            return a.reshape(nb_, t_, N_HEADS, HEAD_DIM)

        p_st[0].append(kt_p)
        p_st[1].append(vt_p)
        p_st[2].append(heads(mk_p, bp, N_MEM))
        p_st[3].append(heads(mv_p, bp, N_MEM))
        s_st[0].append(heads(proj_s[:, C_SK * GROUP_W:(C_SK + 1) * GROUP_W], bs, ts))
        s_st[1].append(heads(proj_s[:, C_SV * GROUP_W:(C_SV + 1) * GROUP_W], bs, ts))
        ml_p = (ml_c_p, ml_n_p, ml_m_p)
        ml_s = (cn_s[..., :HEAD_DIM], cn_s[..., HEAD_DIM], m_s[:, 0, :N_HEADS])
        for st, ml, rs, h5, nb_ in ((p_st, ml_p, rs_p, h5_p, bp), (s_st, ml_s, rs_s, h5_s, bs)):
            off = 4 if st is p_st else 2
            st[off + 0].append(ml[0])
            st[off + 1].append(ml[1])
            st[off + 2].append(ml[2])
            st[off + 3].append(rs)
            st[off + 4].append(h5[:, :S5_W].reshape(nb_, S5_GROUPS, S5_STATE))
            st[off + 5].append(h5[:, S5_W:].reshape(nb_, S5_GROUPS, S5_STATE))

    y_prompt = y_out[0].reshape(bp, tp, D_MODEL)
    y_sample = y_out[1].reshape(bs, ts, D_MODEL)
    p_out = [jnp.stack(a, axis=1) for a in p_st]
    for i in range(2):
        p_out[i] = p_out[i].reshape(bp, DEPTH, N_HEADS, HEAD_DIM, tp).transpose(0, 1, 4, 2, 3)
    s_out = [jnp.stack(a, axis=1) for a in s_st]
    return (y_prompt, y_sample, *p_out, *s_out)
```

```python
import functools
import math

import numpy as np
import jax
import jax.numpy as jnp
from jax import lax
from jax.experimental import pallas as pl
from jax.experimental.pallas import tpu as pltpu

F32 = jnp.float32
BF16 = jnp.bfloat16

D_MODEL = 1024
DEPTH = 2
PAST_LEN = 8192
PAGE_SIZE = 128
HEAD_DIM = 64
N_HEADS = 4
GROUP_W = N_HEADS * HEAD_DIM
S5_GROUPS = 16
S5_GROUP = 16
S5_STATE = 64
S5_W = S5_GROUPS * S5_STATE
N_MEM = 256
D_FF = 3584
N_EXPERTS = 8
TOP_K = 2
CHUNK = 64
ROPE_BASE = 10000.0
LN_EPS = 1e-5
GN_EPS = 1e-6
ALPHA = (2 * DEPTH) ** 0.25
QK_SCALE = HEAD_DIM ** -0.5
LOG2E = math.log2(math.e)

LANES = 128
PROJ_W = 25 * LANES
C_SQ, C_SK, C_SV, C_MQ, C_MK, C_MV, C_MO, C_RQ, C_RK, C_RV, C_RG, C_SU = range(12)
C_GATES = 12 * GROUP_W // LANES
VMEM_LIMIT = 48 * 1024 * 1024
ROW_TILE = 640
PROMPT_TILE = 512
FFN_ROW_TILE = 2 * ROW_TILE
FFN_COL_TILE = 512
MOE_TILE = 1024
SB_QUERY_TILE = 256
SB_PAGES_PER_STEP = 16
SEQS_PER_STEP = 8
S5_STEPS = CHUNK
MIX_DTYPE = BF16


def _cp(*sem):
    return pltpu.CompilerParams(dimension_semantics=sem, vmem_limit_bytes=VMEM_LIMIT)


def _dot(a, b):
    return jnp.dot(a, b, preferred_element_type=F32)


def _dot_nt(a, b):
    return lax.dot_general(a, b, (((1,), (1,)), ((), ())), preferred_element_type=F32)


def _dot_tn(a, b):
    return lax.dot_general(a, b, (((0,), (0,)), ((), ())), preferred_element_type=F32)


def _layer_norm(y, g, b):
    mu = jnp.mean(y, axis=-1, keepdims=True)
    yc = y - mu
    var = jnp.mean(yc * yc, axis=-1, keepdims=True)
    return yc * lax.rsqrt(var + LN_EPS) * g + b


def _row_sum(x, scale=1.0):
    ones = jnp.full((x.shape[1], LANES), scale, BF16)
    hi = x.astype(BF16)
    lo = (x - hi.astype(F32)).astype(BF16)
    return (_dot(hi, ones) + _dot(lo, ones))[:, :x.shape[1]]


def _cumsum_rows(tril, x):
    hi = x.astype(BF16)
    lo = (x - hi.astype(F32)).astype(BF16)
    return _dot(tril, hi) + _dot(tril, lo)


def _head_norm_all(hd):
    inv = 1.0 / HEAD_DIM
    mu = {p: _row_sum(x, inv) for p, x in hd.items()}
    hc = {p: hd[p] - mu[p] for p in hd}
    var = {p: _row_sum(hc[p] * hc[p], inv) for p in hd}
    return {p: hc[p] * lax.rsqrt(var[p] + GN_EPS) for p in hd}


def _neg_softplus(z):
    return -(jnp.maximum(z, 0.0) + jnp.log1p(jnp.exp(-jnp.abs(z))))


def _log_sigmoid(z):
    return _neg_softplus(-z)


def _linear_kernel(x_ref, w_ref, o_ref):
    o_ref[...] = _dot(x_ref[...].astype(BF16), w_ref[...]).astype(o_ref.dtype)


def linear(x, w, tm, row_blk0=0, n_rows=None, out_dtype=F32):
    m, k = x.shape
    m = m if n_rows is None else n_rows
    n = w.shape[1]
    return pl.pallas_call(
        _linear_kernel, grid=(m // tm,),
        in_specs=[pl.BlockSpec((tm, k), lambda i: (row_blk0 + i, 0)), pl.BlockSpec((k, n), lambda i: (0, 0))],
        out_specs=pl.BlockSpec((tm, n), lambda i: (i, 0)),
        out_shape=jax.ShapeDtypeStruct((m, n), out_dtype),
        compiler_params=_cp("arbitrary"), name="linear")(x, w)


def _in_proj_kernel(x_ref, w_ref, o_ref, kt_ref, vt_ref):
    o = _dot(x_ref[...].astype(BF16), w_ref[...])
    o_ref[...] = o
    kt_ref[0] = o[:, C_SK * GROUP_W:(C_SK + 1) * GROUP_W].T
    vt_ref[0] = o[:, C_SV * GROUP_W:(C_SV + 1) * GROUP_W].T


def in_proj_prompt(x, w, n_batch, seq, tm):
    k = x.shape[1]
    n = w.shape[1]
    nt = seq // tm
    t_spec = pl.BlockSpec((1, GROUP_W, tm), lambda i: (i // nt, 0, i % nt))
    t_shape = jax.ShapeDtypeStruct((n_batch, GROUP_W, seq), F32)
    return pl.pallas_call(
        _in_proj_kernel, grid=(n_batch * nt,),
        in_specs=[pl.BlockSpec((tm, k), lambda i: (i, 0)), pl.BlockSpec((k, n), lambda i: (0, 0))],
        out_specs=[pl.BlockSpec((tm, n), lambda i: (i, 0)), t_spec, t_spec],
        out_shape=[jax.ShapeDtypeStruct((n_batch * seq, n), F32), t_shape, t_shape],
        compiler_params=_cp("arbitrary"), name="in_proj")(x, w)


def _suffix_matrix():
    j = np.arange(LANES)
    u = (j[:, None] >= j[None, :]).astype(np.float32)
    uu = np.concatenate([u, np.ones((LANES, LANES), np.float32)], axis=1)
    return jnp.asarray(np.concatenate([uu, uu], axis=0), dtype=BF16)


def _suffix_sums(lr, uu):
    hi = lr.astype(BF16)
    lo = (lr - hi.astype(F32)).astype(BF16)
    r = _dot(jnp.concatenate([hi, lo], axis=1), uu)
    return r[:, :LANES], r[:, LANES:]


def _log2_rem(z2):
    return jnp.minimum(-z2, 0.0) - jnp.log2(1.0 + jnp.exp2(-jnp.abs(z2)))


def _sb_prompt_kernel(bias_ref, q_ref, k_ref, v_ref, uu_ref, o_ref, acc_ref, car_ref, kb_ref, vb_ref, *, tq):
    i = pl.program_id(1)
    tk = LANES
    nsub = tq // tk
    acc_ref[...] = jnp.zeros_like(acc_ref)
    car_ref[...] = jnp.zeros_like(car_ref)

    @pl.when(i == 0)
    def _():
        for h in range(N_HEADS):
            kb_ref[h] = k_ref[:, h * HEAD_DIM:(h + 1) * HEAD_DIM].astype(BF16)
            vb_ref[h] = v_ref[:, h * HEAD_DIM:(h + 1) * HEAD_DIM].astype(BF16)

    q = (q_ref[...] * (QK_SCALE * LOG2E)).astype(BF16)
    qh = [q[:, h * HEAD_DIM:(h + 1) * HEAD_DIM] for h in range(N_HEADS)]
    b2 = [bias_ref[h] * LOG2E for h in range(N_HEADS)]
    uu = uu_ref[...]
    row = lax.broadcasted_iota(jnp.int32, (tq, tk), 0)
    col = lax.broadcasted_iota(jnp.int32, (tq, tk), 1)
    heads = range(N_HEADS)

    def block_pair(j_hi, causal_hi, causal_lo):
        r0 = [pl.multiple_of((j_hi - d) * tk, tk) for d in range(2)]
        z2 = [[_dot_nt(qh[h], kb_ref[h, pl.ds(r0[d], tk), :]) + b2[h] for h in heads] for d in range(2)]
        cs, tot = [], []
        for d, causal in enumerate((causal_hi, causal_lo)):
            lr = [_log2_rem(z) for z in z2[d]]
            if causal is not None:
                lr = [jnp.where(causal, a, 0.0) for a in lr]
            c, t = _suffix_sums(jnp.concatenate(lr, axis=0), uu)
            cs.append(c)
            tot.append(t)
        car = [car_ref[h] for h in heads]
        pv = []
        for d, causal in enumerate((causal_hi, causal_lo)):
            w = [jnp.exp2(z2[d][h] + cs[d][h * tq:(h + 1) * tq] + car[h]) for h in heads]
            if causal is not None:
                w = [jnp.where(causal, a, 0.0) for a in w]
            pv.append([_dot(w[h].astype(BF16), vb_ref[h, pl.ds(r0[d], tk), :]) for h in heads])
            car = [car[h] + tot[d][h * tq:(h + 1) * tq] for h in heads]
        for h in heads:
            acc_ref[h] += pv[0][h] + pv[1][h]
            car_ref[h] = car[h]

    assert nsub == 2
    block_pair(i * nsub + 1, (col + tk) < row, col < row)

    def body(jj, carry):
        block_pair(i * nsub - 1 - 2 * jj, None, None)
        return carry

    lax.fori_loop(0, i, body, 0)
    o_ref[...] = jnp.concatenate([acc_ref[h] for h in range(N_HEADS)], axis=1).astype(o_ref.dtype)


def sb_attention_prompt(proj, sb_bias, uu, n_batch, seq, tq):
    nq = seq // tq
    kern = functools.partial(_sb_prompt_kernel, tq=tq)
    return pl.pallas_call(
        kern, grid=(n_batch, nq),
        in_specs=[pl.BlockSpec(memory_space=pltpu.SMEM),
                  pl.BlockSpec((tq, GROUP_W), lambda b, i: (b * nq + i, C_SQ)),
                  pl.BlockSpec((seq, GROUP_W), lambda b, i: (b, C_SK)),
                  pl.BlockSpec((seq, GROUP_W), lambda b, i: (b, C_SV)),
                  pl.BlockSpec((2 * LANES, 2 * LANES), lambda b, i: (0, 0))],
        out_specs=pl.BlockSpec((tq, GROUP_W), lambda b, i: (b * nq + i, 0)),
        out_shape=jax.ShapeDtypeStruct((n_batch * seq, GROUP_W), MIX_DTYPE),
        scratch_shapes=[pltpu.VMEM((N_HEADS, tq, HEAD_DIM), F32), pltpu.VMEM((N_HEADS, tq, LANES), F32),
                        pltpu.VMEM((N_HEADS, seq, HEAD_DIM), BF16), pltpu.VMEM((N_HEADS, seq, HEAD_DIM), BF16)],
        compiler_params=_cp("arbitrary", "arbitrary"), name="sb_prompt")(sb_bias, proj, proj, proj, uu)


def _sb_sample_kernel(pt_ref, bias_ref, q_ref, kn_ref, vn_ref, u8_ref, uu_ref, *rest, n_pp, n_steps):
    k_refs = rest[:n_pp]
    v_refs = rest[n_pp:2 * n_pp]
    o_ref = rest[2 * n_pp]
    acc_ref, car_ref = rest[2 * n_pp + 1:]
    s = pl.program_id(1)
    nq = q_ref.shape[0]
    rows = N_HEADS * nq
    q = (q_ref[...] * (QK_SCALE * LOG2E)).astype(BF16)
    row_head = lax.broadcasted_iota(jnp.int32, (rows, GROUP_W), 0) // nq
    own = row_head == lax.broadcasted_iota(jnp.int32, (rows, GROUP_W), 1) // HEAD_DIM
    q_bd = jnp.where(own, jnp.concatenate([q] * N_HEADS, axis=0), jnp.zeros((), BF16))
    row_head_l = lax.broadcasted_iota(jnp.int32, (rows, LANES), 0) // nq
    b2 = jnp.zeros((rows, LANES), F32)
    for h in range(N_HEADS):
        b2 = jnp.where(row_head_l == h, bias_ref[h] * LOG2E, b2)

    @pl.when(s == 0)
    def _():
        kn = kn_ref[...].astype(BF16)
        vn = vn_ref[...].astype(BF16)
        t = lax.broadcasted_iota(jnp.int32, (rows, nq), 0) % nq
        causal = lax.broadcasted_iota(jnp.int32, (rows, nq), 1) < t
        z2 = _dot_nt(q_bd, kn) + b2[:, :nq]
        lr = jnp.where(causal, _log2_rem(z2), 0.0)
        cs = jnp.dot(lr, u8_ref[...], preferred_element_type=F32, precision=lax.Precision.HIGHEST)
        w = jnp.where(causal, jnp.exp2(z2 + cs), 0.0)
        acc_ref[...] = _dot(w.astype(BF16), vn)
        car_ref[...] = jnp.broadcast_to(jnp.sum(lr, axis=1, keepdims=True), (rows, LANES))

    z2s = [_dot(q_bd, k_refs[p][0, 0].reshape(GROUP_W, PAGE_SIZE).astype(BF16)) + b2 for p in range(n_pp)]
    cs_all, tot_all = _suffix_sums(_log2_rem(jnp.concatenate(z2s, axis=0)), uu_ref[...])
    car = car_ref[...]
    acc = acc_ref[...]
    for p in range(n_pp):
        w = jnp.exp2(z2s[p] + cs_all[p * rows:(p + 1) * rows] + car)
        acc = acc + _dot_nt(w.astype(BF16), v_refs[p][0, 0].reshape(GROUP_W, PAGE_SIZE).astype(BF16))
        car = car + tot_all[p * rows:(p + 1) * rows]
    car_ref[...] = car
    acc_ref[...] = acc

    @pl.when(s == n_steps - 1)
    def _():
        kept = jnp.where(own, acc, 0.0)
        o_ref[...] = sum(kept[h * nq:(h + 1) * nq] for h in range(N_HEADS))


def sb_attention_sample(proj, row_blk0, nq, cache_kt, cache_vt, page_table, layer, sb_bias, uu, n_pp):
    n_batch, n_pages = page_table.shape
    n_steps = n_pages // n_pp
    u8 = jnp.asarray((np.arange(nq)[:, None] >= np.arange(nq)[None, :]).astype(np.float32))

    def page_spec(p):
        return pl.BlockSpec((1, 1, N_HEADS, HEAD_DIM, PAGE_SIZE),
                            lambda b, s, pt: (pt[b, n_pages - 1 - (s * n_pp + p)], layer, 0, 0, 0))

    def row_spec(cblk):
        return pl.BlockSpec((nq, GROUP_W), lambda b, s, pt: (row_blk0 + b, cblk))

    kern = functools.partial(_sb_sample_kernel, n_pp=n_pp, n_steps=n_steps)
    gs = pltpu.PrefetchScalarGridSpec(
        num_scalar_prefetch=1, grid=(n_batch, n_steps),
        in_specs=[pl.BlockSpec(memory_space=pltpu.SMEM), row_spec(C_SQ), row_spec(C_SK), row_spec(C_SV),
                  pl.BlockSpec((nq, nq), lambda b, s, pt: (0, 0)),
                  pl.BlockSpec((2 * LANES, 2 * LANES), lambda b, s, pt: (0, 0))]
                 + [page_spec(p) for p in range(n_pp)] * 2,
        out_specs=pl.BlockSpec((nq, GROUP_W), lambda b, s, pt: (b, 0)),
        scratch_shapes=[pltpu.VMEM((N_HEADS * nq, GROUP_W), F32), pltpu.VMEM((N_HEADS * nq, LANES), F32)])
    return pl.pallas_call(
        kern, grid_spec=gs, out_shape=jax.ShapeDtypeStruct((n_batch * nq, GROUP_W), F32),
        compiler_params=_cp("arbitrary", "arbitrary"), name="sb_sample")(
            page_table, sb_bias, proj, proj, proj, u8, uu, *([cache_kt] * n_pp), *([cache_vt] * n_pp))


def _mlstm_kernel(m0_ref, q_ref, k_ref, v_ref, og_ref, gt_ref, gb_ref, ng_ref, tril_ref, cn0_ref,
                  o_ref, cn_ref, m_ref, ms_ref, *, bb, chunk, n_t):
    bi = pl.program_id(0)
    t = pl.program_id(1)
    seqs = range(bb)
    pairs = [(b, h) for b in seqs for h in range(N_HEADS)]

    @pl.when(t == 0)
    def _():
        cn_ref[...] = cn0_ref[...]
        for b, h in pairs:
            ms_ref[b * N_HEADS + h] = jnp.full((1, LANES), m0_ref[bi * bb + b, h], F32)

    tril = tril_ref[...]
    tri_mask = lax.broadcasted_iota(jnp.int32, (chunk, chunk), 1) <= lax.broadcasted_iota(jnp.int32, (chunk, chunk), 0)
    lane = lax.broadcasted_iota(jnp.int32, (chunk, HEAD_DIM), 1)
    ones_col = jnp.where(lane == 0, 1.0, 0.0).astype(F32)
    hs = lambda h: slice(h * HEAD_DIM, (h + 1) * HEAD_DIM)

    gt = [gt_ref[b] + gb_ref[...] for b in seqs]
    bc = [_cumsum_rows(tril, _log_sigmoid(g)) for g in gt]
    gt_t = [g.T for g in gt]
    bc_t = [x.T for x in bc]
    q = [q_ref[b].astype(BF16) for b in seqs]
    k = [(k_ref[b] * QK_SCALE).astype(BF16) for b in seqs]
    v = [v_ref[b] for b in seqs]
    m_prev = {p: ms_ref[p[0] * N_HEADS + p[1]][:, :1] for p in pairs}
    ig_col = {(b, h): gt[b][:, h:h + 1] for b, h in pairs}
    bc_col = {(b, h): bc[b][:, N_HEADS + h:N_HEADS + h + 1] for b, h in pairs}
    dm = {(b, h): jnp.where(tri_mask, bc_col[b, h] - (bc_t[b][N_HEADS + h:N_HEADS + h + 1, :] - gt_t[b][h:h + 1, :]),
                            -jnp.inf) for b, h in pairs}
    a = {p: bc_col[p] + m_prev[p] for p in pairs}
    m_new = {p: jnp.maximum(a[p], jnp.max(dm[p], axis=1, keepdims=True)) for p in pairs}
    inter = {p: jnp.exp(a[p] - m_new[p]) for p in pairs}
    s = {(b, h): _dot_nt(q[b][:, hs(h)], k[b][:, hs(h)]) * jnp.exp(dm[b, h] - m_new[b, h]) for b, h in pairs}
    v_ext = {(b, h): jnp.concatenate([v[b][:, hs(h)], ones_col], axis=1) for b, h in pairs}
    qc = {(b, h): _dot(q[b][:, hs(h)], cn_ref[b, h].astype(BF16)) for b, h in pairs}
    sv = {p: _dot(s[p].astype(BF16), v_ext[p].astype(BF16)) for p in pairs}
    m_last = {p: m_new[p][chunk - 1:chunk, :] for p in pairs}
    wl = {p: jnp.exp(bc_col[p][chunk - 1:chunk, :] - bc_col[p] + ig_col[p] - m_last[p]) for p in pairs}
    dl = {p: jnp.exp(a[p][chunk - 1:chunk, :] - m_last[p]) for p in pairs}
    upd = {(b, h): _dot_tn(k[b][:, hs(h)], (wl[b, h] * v_ext[b, h]).astype(BF16)) for b, h in pairs}
    rs = {p: _row_sum(s[p])[:, :1] for p in pairs}
    num = {p: inter[p] * qc[p][:, :HEAD_DIM] + sv[p][:, :HEAD_DIM] for p in pairs}
    den = {p: inter[p] * qc[p][:, HEAD_DIM:HEAD_DIM + 1] + rs[p] for p in pairs}
    hh = _head_norm_all({p: num[p] / jnp.maximum(jnp.abs(den[p]), jnp.exp(-m_new[p])) for p in pairs})
    for b, h in pairs:
        cn_ref[b, h] = dl[b, h] * cn_ref[b, h] + upd[b, h]
        ms_ref[b * N_HEADS + h] = jnp.broadcast_to(m_last[b, h], (1, LANES))
    for b in seqs:
        y = jnp.concatenate([hh[b, h] for h in range(N_HEADS)], axis=1)
        o_ref[b] = (y * ng_ref[...] * jax.nn.sigmoid(og_ref[b])).astype(o_ref.dtype)

    @pl.when(t == n_t - 1)
    def _():
        lane_m = lax.broadcasted_iota(jnp.int32, (1, LANES), 1)
        for b in seqs:
            m_out = jnp.zeros((1, LANES), F32)
            for h in range(N_HEADS):
                m_out = jnp.where(lane_m == h, ms_ref[b * N_HEADS + h], m_out)
            m_ref[b] = m_out


def mlstm_mixer(proj3, gate_bias, norm_g, cn0, m0, bb, out_dtype):
    n_batch, seq, _ = proj3.shape
    chunk = math.gcd(seq, CHUNK)
    n_t = seq // chunk
    tril = jnp.asarray(np.tril(np.ones((chunk, chunk), np.float32)), dtype=BF16)

    def row_spec(cblk, w=GROUP_W):
        return pl.BlockSpec((bb, chunk, w), lambda bi, t: (bi, t, cblk))

    def const_spec(shape):
        return pl.BlockSpec(shape, lambda bi, t: (0,) * len(shape))

    state_spec = pl.BlockSpec((bb, N_HEADS, HEAD_DIM, LANES), lambda bi, t: (bi, 0, 0, 0))
    kern = functools.partial(_mlstm_kernel, bb=bb, chunk=chunk, n_t=n_t)
    return pl.pallas_call(
        kern, grid=(n_batch // bb, n_t),
        in_specs=[pl.BlockSpec(memory_space=pltpu.SMEM),
                  row_spec(C_MQ), row_spec(C_MK), row_spec(C_MV), row_spec(C_MO), row_spec(C_GATES, LANES),
                  const_spec((1, LANES)), const_spec((1, GROUP_W)), const_spec((chunk, chunk)), state_spec],
        out_specs=[pl.BlockSpec((bb, chunk, GROUP_W), lambda bi, t: (bi, t, 0)), state_spec,
                   pl.BlockSpec((bb, 1, LANES), lambda bi, t: (bi, 0, 0))],
        out_shape=[jax.ShapeDtypeStruct((n_batch, seq, GROUP_W), out_dtype),
                   jax.ShapeDtypeStruct((n_batch, N_HEADS, HEAD_DIM, LANES), F32),
                   jax.ShapeDtypeStruct((n_batch, 1, LANES), F32)],
        scratch_shapes=[pltpu.VMEM((bb * N_HEADS, 1, LANES), F32)],
        compiler_params=_cp("arbitrary", "arbitrary"), name="mlstm")(
            m0, proj3, proj3, proj3, proj3, proj3, gate_bias, norm_g, tril, cn0)


def _split_dot(x, w):
    hi = x.astype(BF16)
    lo = (x - hi.astype(F32)).astype(BF16)
    return _dot(hi, w) + _dot(lo, w)


def _mlstm_pair_kernel(q_ref, k_ref, v_ref, og_ref, gt_ref, gb_ref, ng_ref, tril_ref, sel_ref, blk_ref,
                       o_ref, cn_ref, st_ref, *, bb, chunk):
    t_id = pl.program_id(1)

    @pl.when(t_id == 0)
    def _():
        cn_ref[...] = jnp.zeros_like(cn_ref)
        st_ref[...] = jnp.zeros_like(st_ref)

    seqs = range(bb)
    n_pair = N_HEADS // 2
    chains = [(b, p) for b in seqs for p in range(n_pair)]
    lane = lax.broadcasted_iota(jnp.int32, (chunk, LANES), 1)
    row = lax.broadcasted_iota(jnp.int32, (chunk, LANES), 0)
    low = lane < HEAD_DIM
    causal = (lane % HEAD_DIM) <= row
    eye2 = (lane % HEAD_DIM) == row
    blk = blk_ref[...]
    same = (lax.broadcasted_iota(jnp.int32, (LANES, LANES), 0) // HEAD_DIM
            == lax.broadcasted_iota(jnp.int32, (LANES, LANES), 1) // HEAD_DIM)
    ones_ll = jnp.ones((chunk, chunk), BF16)
    tril = tril_ref[...]
    pl_ = lambda p: slice(p * LANES, (p + 1) * LANES)
    zero_b = jnp.zeros((), BF16)

    gt = [gt_ref[b] + gb_ref[...] for b in seqs]
    bc = [_cumsum_rows(tril, _log_sigmoid(g)) for g in gt]
    def stacked(fn, xs):
        keys = list(xs)
        y = fn(jnp.concatenate([xs[c] for c in keys], axis=0))
        return {c: y[i * chunk:(i + 1) * chunk] for i, c in enumerate(keys)}

    by_blk = lambda x: _split_dot(x, blk)
    ig_all = stacked(lambda x: _split_dot(x, sel_ref[0]), dict(enumerate(gt)))
    bc_all = stacked(lambda x: _split_dot(x, sel_ref[1]), dict(enumerate(bc)))
    igc = {(b, p): ig_all[b][:, pl_(p)] for b, p in chains}
    bcc = {(b, p): bc_all[b][:, pl_(p)] for b, p in chains}
    gc = {c: igc[c] - bcc[c] for c in chains}
    grow = {c: _split_dot_left(ones_ll, jnp.where(eye2, gc[c], 0.0)) for c in chains}
    gmax = {c: jnp.max(gc[c], axis=0, keepdims=True) for c in chains}
    m_prev = {(b, p): st_ref[b, p, 1:2, :] for b, p in chains}
    a = {c: bcc[c] + m_prev[c] for c in chains}
    m_stab = {c: jnp.maximum(a[c], bcc[c] + gmax[c]) for c in chains}
    inter = {c: jnp.exp(a[c] - m_stab[c]) for c in chains}
    dw = {c: jnp.exp(jnp.where(causal, bcc[c] + grow[c], -jnp.inf) - m_stab[c]) for c in chains}
    q2 = {(b, p): q_ref[b, :, pl_(p)].astype(BF16) for b, p in chains}
    k2f = {(b, p): k_ref[b, :, pl_(p)] * QK_SCALE for b, p in chains}
    k2 = {c: k2f[c].astype(BF16) for c in chains}
    v2 = {(b, p): v_ref[b, :, pl_(p)] for b, p in chains}
    kbd = {c: jnp.concatenate([jnp.where(low, k2[c], zero_b), jnp.where(low, zero_b, k2[c])], axis=0) for c in chains}
    v2b = {c: v2[c].astype(BF16) for c in chains}
    vbd = {c: jnp.concatenate([jnp.where(low, v2b[c], zero_b), jnp.where(low, zero_b, v2b[c])], axis=0)
           for c in chains}
    s = {c: _dot_nt(q2[c], kbd[c]) * dw[c] for c in chains}
    rs = stacked(by_blk, s)
    sv = {c: _dot(s[c].astype(BF16), vbd[c]) for c in chains}
    qc = {(b, p): _dot(q2[b, p], cn_ref[b, p].astype(BF16)) for b, p in chains}
    n_row = {(b, p): st_ref[b, p, 0:1, :] for b, p in chains}
    qn = stacked(by_blk, {c: q2[c].astype(F32) * n_row[c] for c in chains})
    hh = {}
    for c in chains:
        num = inter[c] * qc[c] + sv[c]
        den = inter[c] * qn[c] + rs[c]
        hh[c] = num / jnp.maximum(jnp.abs(den), jnp.exp(-m_stab[c]))
    inv = 1.0 / HEAD_DIM
    mu = stacked(by_blk, hh)
    hc = {c: hh[c] - mu[c] * inv for c in chains}
    var = stacked(by_blk, {c: hc[c] * hc[c] for c in chains})
    m_last = {c: m_stab[c][chunk - 1:chunk, :] for c in chains}
    wl = {c: jnp.exp(bcc[c][chunk - 1:chunk, :] - bcc[c] + igc[c] - m_last[c]) for c in chains}
    dl = {c: jnp.exp(a[c][chunk - 1:chunk, :] - m_last[c]) for c in chains}
    upd = {c: _dot_tn(k2[c], (wl[c] * v2[c]).astype(BF16)) for c in chains}
    for b, p in chains:
        c = (b, p)
        y = hc[c] * lax.rsqrt(var[c] * inv + GN_EPS)
        o_ref[b, :, pl_(p)] = (y * ng_ref[:, pl_(p)] * jax.nn.sigmoid(og_ref[b, :, pl_(p)])).astype(o_ref.dtype)
        cn_ref[b, p] = dl[c] * cn_ref[b, p] + jnp.where(same, upd[c], 0.0)
        st_ref[b, p, 0:1, :] = dl[c] * n_row[c] + jnp.sum(wl[c] * k2f[c], axis=0, keepdims=True)
        st_ref[b, p, 1:2, :] = m_last[c]


def _split_dot_left(w, x):
    hi = x.astype(BF16)
    lo = (x - hi.astype(F32)).astype(BF16)
    return _dot(w, hi) + _dot(w, lo)


def mlstm_mixer_prompt(proj3, gate_bias, norm_g, bb, out_dtype):
    n_batch, seq, _ = proj3.shape
    chunk = CHUNK
    n_pair = N_HEADS // 2
    tril = jnp.asarray(np.tril(np.ones((chunk, chunk), np.float32)), dtype=BF16)
    sel = np.zeros((2, LANES, GROUP_W), np.float32)
    for h in range(N_HEADS):
        sel[0, h, h * HEAD_DIM:(h + 1) * HEAD_DIM] = 1.0
        sel[1, N_HEADS + h, h * HEAD_DIM:(h + 1) * HEAD_DIM] = 1.0
    idx = np.arange(LANES) // HEAD_DIM
    blk = (idx[:, None] == idx[None, :]).astype(np.float32)

    def row_spec(cblk, w=GROUP_W):
        return pl.BlockSpec((bb, chunk, w), lambda bi, t: (bi, t, cblk))

    def const_spec(shape):
        return pl.BlockSpec(shape, lambda bi, t: (0,) * len(shape))

    cn_spec = pl.BlockSpec((bb, n_pair, LANES, LANES), lambda bi, t: (bi, 0, 0, 0))
    st_spec = pl.BlockSpec((bb, n_pair, 8, LANES), lambda bi, t: (bi, 0, 0, 0))
    kern = functools.partial(_mlstm_pair_kernel, bb=bb, chunk=chunk)
    o, cn, st = pl.pallas_call(
        kern, grid=(n_batch // bb, seq // chunk),
        in_specs=[row_spec(C_MQ), row_spec(C_MK), row_spec(C_MV), row_spec(C_MO), row_spec(C_GATES, LANES),
                  const_spec((1, LANES)), const_spec((1, GROUP_W)), const_spec((chunk, chunk)),
                  const_spec(sel.shape), const_spec(blk.shape)],
        out_specs=[pl.BlockSpec((bb, chunk, GROUP_W), lambda bi, t: (bi, t, 0)), cn_spec, st_spec],
        out_shape=[jax.ShapeDtypeStruct((n_batch, seq, GROUP_W), out_dtype),
                   jax.ShapeDtypeStruct((n_batch, n_pair, LANES, LANES), F32),
                   jax.ShapeDtypeStruct((n_batch, n_pair, 8, LANES), F32)],
        compiler_params=_cp("arbitrary", "arbitrary"), name="mlstm_prompt")(
            proj3, proj3, proj3, proj3, proj3, gate_bias, norm_g, tril,
            jnp.asarray(sel, dtype=BF16), jnp.asarray(blk, dtype=BF16))
    c5 = cn.reshape(n_batch, n_pair, 2, HEAD_DIM, 2, HEAD_DIM)
    c_state = jnp.stack([c5[:, :, j, :, j, :] for j in range(2)], axis=2).reshape(n_batch, N_HEADS, HEAD_DIM, HEAD_DIM)
    n_state = st[:, :, 0, :].reshape(n_batch, N_HEADS, HEAD_DIM)
    m_state = st[:, :, 1, :].reshape(n_batch, N_HEADS, HEAD_DIM)[:, :, 0]
    return o, c_state, n_state, m_state


def _rope(x, cos, sin_signed):
    lane = lax.broadcasted_iota(jnp.int32, x.shape, 1)
    half = HEAD_DIM // 2
    swapped = jnp.where((lane % HEAD_DIM) < half, pltpu.roll(x, x.shape[1] - half, 1), pltpu.roll(x, half, 1))
    return x * cos + swapped * sin_signed


def _retention_kernel(q_ref, k_ref, v_ref, gg_ref, cos_ref, sin_ref, ng_ref, dec_ref, int_ref, wl_ref, dl_ref,
                      s0_ref, o_ref, s_ref, *, bb):
    t = pl.program_id(1)
    seqs = range(bb)
    pairs = [(b, h) for b in seqs for h in range(N_HEADS)]
    hs = lambda h: slice(h * HEAD_DIM, (h + 1) * HEAD_DIM)

    @pl.when(t == 0)
    def _():
        s_ref[...] = s0_ref[...]

    cos = cos_ref[...]
    sin = sin_ref[...]
    q = [_rope(q_ref[b], cos, sin).astype(BF16) for b in seqs]
    k = [(_rope(k_ref[b], cos, sin) * QK_SCALE).astype(BF16) for b in seqs]
    v = [v_ref[b] for b in seqs]
    s = {(b, h): _dot_nt(q[b][:, hs(h)], k[b][:, hs(h)]) * dec_ref[h] for b, h in pairs}
    qs = {(b, h): _dot(q[b][:, hs(h)], s_ref[b, h].astype(BF16)) for b, h in pairs}
    sv = {(b, h): _dot(s[b, h].astype(BF16), v[b][:, hs(h)].astype(BF16)) for b, h in pairs}
    upd = {(b, h): _dot_tn(k[b][:, hs(h)], (wl_ref[h] * v[b][:, hs(h)]).astype(BF16)) for b, h in pairs}
    o = _head_norm_all({(b, h): int_ref[h] * qs[b, h] + sv[b, h] for b, h in pairs})
    for b, h in pairs:
        s_ref[b, h] = dl_ref[h] * s_ref[b, h] + upd[b, h]
    for b in seqs:
        gg = gg_ref[b]
        y = jnp.concatenate([o[b, h] for h in range(N_HEADS)], axis=1) * ng_ref[...] * (gg * jax.nn.sigmoid(gg))
        o_ref[b] = y.astype(o_ref.dtype)


def _retention_consts(chunk):
    log_g = np.log(1.0 - np.exp2(-5.0 - np.arange(N_HEADS, dtype=np.float64)))
    tau = np.arange(chunk, dtype=np.float64)
    rel = tau[:, None] - tau[None, :]
    decay = np.where(rel >= 0, np.exp(log_g[:, None, None] * np.maximum(rel, 0.0)), 0.0)
    inter = np.exp(log_g[:, None] * (tau + 1.0))[..., None]
    wl = np.exp(log_g[:, None] * (chunk - 1.0 - tau))[..., None]
    dl = np.exp(log_g * chunk)[:, None, None]
    return tuple(jnp.asarray(a, F32) for a in (decay, inter, wl, dl))


def _retention_pair_kernel(q_ref, k_ref, v_ref, gg_ref, cos_ref, sin_ref, ng_ref, dec_ref, int_ref, wl_ref, dl_ref,
                           swap_ref, blk_ref, o_ref, s_ref, *, bb, chunk):
    t_id = pl.program_id(1)

    @pl.when(t_id == 0)
    def _():
        s_ref[...] = jnp.zeros_like(s_ref)

    seqs = range(bb)
    n_pair = N_HEADS // 2
    chains = [(b, p) for b in seqs for p in range(n_pair)]
    low = lax.broadcasted_iota(jnp.int32, (chunk, LANES), 1) < HEAD_DIM
    same = (lax.broadcasted_iota(jnp.int32, (LANES, LANES), 0) // HEAD_DIM
            == lax.broadcasted_iota(jnp.int32, (LANES, LANES), 1) // HEAD_DIM)
    pl_ = lambda p: slice(p * LANES, (p + 1) * LANES)
    zero_b = jnp.zeros((), BF16)
    blk = blk_ref[...]
    cos = cos_ref[...]
    sin = sin_ref[...]

    def stacked(fn, xs):
        keys = list(xs)
        y = fn(jnp.concatenate([xs[c] for c in keys], axis=0))
        return {c: y[i * chunk:(i + 1) * chunk] for i, c in enumerate(keys)}

    swap = lambda x: _split_dot(x, swap_ref[...])
    qf = {b: q_ref[b] for b in seqs}
    kf = {b: k_ref[b] for b in seqs}
    q_sw = stacked(swap, qf)
    k_sw = stacked(swap, kf)
    q = {b: (qf[b] * cos + q_sw[b] * sin).astype(BF16) for b in seqs}
    k = {b: ((kf[b] * cos + k_sw[b] * sin) * QK_SCALE).astype(BF16) for b in seqs}
    k2 = {(b, p): k[b][:, pl_(p)] for b, p in chains}
    v2 = {(b, p): v_ref[b, :, pl_(p)] for b, p in chains}
    v2b = {c: v2[c].astype(BF16) for c in chains}
    kbd = {c: jnp.concatenate([jnp.where(low, k2[c], zero_b), jnp.where(low, zero_b, k2[c])], axis=0) for c in chains}
    vbd = {c: jnp.concatenate([jnp.where(low, v2b[c], zero_b), jnp.where(low, zero_b, v2b[c])], axis=0)
           for c in chains}
    s = {(b, p): _dot_nt(q[b][:, pl_(p)], kbd[b, p]) * dec_ref[p] for b, p in chains}
    qs = {(b, p): _dot(q[b][:, pl_(p)], s_ref[b, p].astype(BF16)) for b, p in chains}
    sv = {c: _dot(s[c].astype(BF16), vbd[c]) for c in chains}
    upd = {(b, p): _dot_tn(k2[b, p], (wl_ref[p] * v2[b, p]).astype(BF16)) for b, p in chains}
    o = {(b, p): int_ref[p] * qs[b, p] + sv[b, p] for b, p in chains}
    inv = 1.0 / HEAD_DIM
    by_blk = lambda x: _split_dot(x, blk)
    mu = stacked(by_blk, o)
    oc = {c: o[c] - mu[c] * inv for c in chains}
    var = stacked(by_blk, {c: oc[c] * oc[c] for c in chains})
    for b, p in chains:
        c = (b, p)
        gg = gg_ref[b, :, pl_(p)]
        y = oc[c] * lax.rsqrt(var[c] * inv + GN_EPS) * ng_ref[:, pl_(p)] * (gg * jax.nn.sigmoid(gg))
        o_ref[b, :, pl_(p)] = y.astype(o_ref.dtype)
        s_ref[b, p] = dl_ref[p] * s_ref[b, p] + jnp.where(same, upd[c], 0.0)


def retention_mixer_prompt(proj3, cos, sin_signed, norm_g, bb, out_dtype):
    n_batch, seq, _ = proj3.shape
    chunk = CHUNK
    n_pair = N_HEADS // 2
    dec, inter, wl, dl = _retention_consts(chunk)
    rep = lambda a: jnp.broadcast_to(a, a.shape[:-1] + (HEAD_DIM,))
    pair = lambda a: jnp.concatenate([a[0::2], a[1::2]], axis=-1)
    dec2, int2, wl2, dl2 = pair(dec), pair(rep(inter)), pair(rep(wl)), pair(rep(dl))
    src = np.arange(GROUP_W)
    partner = (src // HEAD_DIM) * HEAD_DIM + (src % HEAD_DIM + HEAD_DIM // 2) % HEAD_DIM
    swap = np.zeros((GROUP_W, GROUP_W), np.float32)
    swap[partner, src] = 1.0
    idx = np.arange(LANES) // HEAD_DIM
    blk = (idx[:, None] == idx[None, :]).astype(np.float32)

    def row_spec(cblk):
        return pl.BlockSpec((bb, chunk, GROUP_W), lambda bi, t: (bi, t, cblk))

    def const_spec(shape):
        return pl.BlockSpec(shape, lambda bi, t: (0,) * len(shape))

    pos_spec = pl.BlockSpec((chunk, GROUP_W), lambda bi, t: (t, 0))
    state_spec = pl.BlockSpec((bb, n_pair, LANES, LANES), lambda bi, t: (bi, 0, 0, 0))
    kern = functools.partial(_retention_pair_kernel, bb=bb, chunk=chunk)
    o, sbd = pl.pallas_call(
        kern, grid=(n_batch // bb, seq // chunk),
        in_specs=[row_spec(C_RQ), row_spec(C_RK), row_spec(C_RV), row_spec(C_RG), pos_spec, pos_spec,
                  const_spec((1, GROUP_W)), const_spec(dec2.shape), const_spec(int2.shape), const_spec(wl2.shape),
                  const_spec(dl2.shape), const_spec(swap.shape), const_spec(blk.shape)],
        out_specs=[pl.BlockSpec((bb, chunk, GROUP_W), lambda bi, t: (bi, t, 0)), state_spec],
        out_shape=[jax.ShapeDtypeStruct((n_batch, seq, GROUP_W), out_dtype),
                   jax.ShapeDtypeStruct((n_batch, n_pair, LANES, LANES), F32)],
        compiler_params=_cp("arbitrary", "arbitrary"), name="retention_prompt")(
            proj3, proj3, proj3, proj3, cos, sin_signed, norm_g, dec2, int2, wl2, dl2,
            jnp.asarray(swap, dtype=BF16), jnp.asarray(blk, dtype=BF16))
    s5d = sbd.reshape(n_batch, n_pair, 2, HEAD_DIM, 2, HEAD_DIM)
    state = jnp.stack([s5d[:, :, j, :, j, :] for j in range(2)], axis=2).reshape(n_batch, N_HEADS, HEAD_DIM, HEAD_DIM)
    return o, state


def retention_mixer(proj3, cos, sin_signed, norm_g, s0, bb, out_dtype):
    n_batch, seq, _ = proj3.shape
    chunk = math.gcd(seq, CHUNK)
    dec, inter, wl, dl = _retention_consts(chunk)

    def row_spec(cblk):
        return pl.BlockSpec((bb, chunk, GROUP_W), lambda bi, t: (bi, t, cblk))

    def const_spec(shape):
        return pl.BlockSpec(shape, lambda bi, t: (0,) * len(shape))

    pos_spec = pl.BlockSpec((chunk, GROUP_W), lambda bi, t: (t, 0))
    state_spec = pl.BlockSpec((bb, N_HEADS, HEAD_DIM, HEAD_DIM), lambda bi, t: (bi, 0, 0, 0))
    kern = functools.partial(_retention_kernel, bb=bb)
    return pl.pallas_call(
        kern, grid=(n_batch // bb, seq // chunk),
        in_specs=[row_spec(C_RQ), row_spec(C_RK), row_spec(C_RV), row_spec(C_RG), pos_spec, pos_spec,
                  const_spec((1, GROUP_W)), const_spec(dec.shape), const_spec(inter.shape), const_spec(wl.shape),
                  const_spec(dl.shape), state_spec],
        out_specs=[pl.BlockSpec((bb, chunk, GROUP_W), lambda bi, t: (bi, t, 0)), state_spec],
        out_shape=[jax.ShapeDtypeStruct((n_batch, seq, GROUP_W), out_dtype),
                   jax.ShapeDtypeStruct((n_batch, N_HEADS, HEAD_DIM, HEAD_DIM), F32)],
        compiler_params=_cp("arbitrary", "arbitrary"), name="retention")(
            proj3, proj3, proj3, proj3, cos, sin_signed, norm_g, dec, inter, wl, dl, s0)


def _s5_kernel(u_ref, wb_ref, a1_ref, a2_ref, h0_ref, wc_ref, d_ref, gw_ref, gb_ref, o_ref, hl_ref,
               hs_ref, ut_ref, yt_ref, *, nb, tt):
    c = pl.program_id(0)

    @pl.when(c == 0)
    def _():
        hl_ref[...] = h0_ref[...]

    halves = range(GROUP_W // LANES)
    for b in range(nb):
        ub = u_ref[b]
        for hf in halves:
            ut_ref[hf, pl.ds(b, tt, stride=nb), :] = ub[:, hf * LANES:(hf + 1) * LANES]
    u = jnp.concatenate([ut_ref[hf] for hf in halves], axis=1)
    hs_ref[...] = _dot(u.astype(BF16), wb_ref[...])
    a1 = jnp.broadcast_to(a1_ref[...], (nb, 2 * S5_W))
    a2 = jnp.broadcast_to(a2_ref[...], (nb, 2 * S5_W))

    def step(t, h):
        r0 = pl.multiple_of(t * nb, nb)
        swapped = jnp.concatenate([h[:, S5_W:], h[:, :S5_W]], axis=1)
        h = a1 * h + a2 * swapped + hs_ref[pl.ds(r0, nb), :]
        hs_ref[pl.ds(r0, nb), :] = h
        return h

    hl_ref[...] = lax.fori_loop(0, tt, step, hl_ref[...])
    y = _dot(hs_ref[...].astype(BF16), wc_ref[...]) + d_ref[...] * u
    g5 = jax.nn.gelu(y)
    yt = g5 * jax.nn.sigmoid(_dot(g5.astype(BF16), gw_ref[...]) + gb_ref[...])
    for hf in halves:
        yt_ref[hf] = yt[:, hf * LANES:(hf + 1) * LANES]
    for b in range(nb):
        o_ref[b] = jnp.concatenate([yt_ref[hf, pl.ds(b, tt, stride=nb), :] for hf in halves],
                                   axis=1).astype(o_ref.dtype)


def s5_mixer(proj3, tt, wb, a1, a2, h0, wc, d, glu_w, glu_b, out_dtype):
    nb, seq, _ = proj3.shape
    rows = tt * nb

    def const_spec(shape):
        return pl.BlockSpec(shape, lambda c: (0,) * len(shape))

    kern = functools.partial(_s5_kernel, nb=nb, tt=tt)
    return pl.pallas_call(
        kern, grid=(seq // tt,),
        in_specs=[pl.BlockSpec((nb, tt, GROUP_W), lambda c: (0, c, C_SU)),
                  const_spec(wb.shape), const_spec(a1.shape), const_spec(a2.shape), const_spec(h0.shape),
                  const_spec(wc.shape), const_spec(d.shape), const_spec(glu_w.shape), const_spec(glu_b.shape)],
        out_specs=[pl.BlockSpec((nb, tt, GROUP_W), lambda c: (0, c, 0)), const_spec(h0.shape)],
        out_shape=[jax.ShapeDtypeStruct((nb, seq, GROUP_W), out_dtype), jax.ShapeDtypeStruct(h0.shape, F32)],
        scratch_shapes=[pltpu.VMEM((rows, 2 * S5_W), F32), pltpu.VMEM((GROUP_W // LANES, rows, LANES), F32),
                        pltpu.VMEM((GROUP_W // LANES, rows, LANES), F32)],
        compiler_params=_cp("arbitrary"), name="s5")(proj3, wb, a1, a2, h0, wc, d, glu_w, glu_b)


def _s5_weights(a_re, a_im, log_dt, b_re, b_im, c_re, c_im):
    dt = jnp.exp(log_dt)
    mag = jnp.exp(a_re * dt)
    ar, ai = mag * jnp.cos(a_im * dt), mag * jnp.sin(a_im * dt)
    den = a_re * a_re + a_im * a_im
    cr = ((ar - 1.0) * a_re + ai * a_im) / den
    ci = (ai * a_re - (ar - 1.0) * a_im) / den
    bb_re = cr[..., None] * b_re - ci[..., None] * b_im
    bb_im = cr[..., None] * b_im + ci[..., None] * b_re
    eye = jnp.eye(S5_GROUPS, dtype=F32)

    def in_map(m):
        return jnp.einsum('gpc,gh->gchp', m, eye).reshape(S5_GROUPS * S5_GROUP, S5_W)

    def out_map(m):
        return jnp.einsum('gcp,gh->gphc', m, eye).reshape(S5_W, S5_GROUPS * S5_GROUP)

    wb = jnp.concatenate([in_map(bb_re), in_map(bb_im)], axis=1).astype(BF16)
    wc = jnp.concatenate([out_map(c_re), -out_map(c_im)], axis=0).astype(BF16)
    ar = ar.reshape(1, S5_W)
    ai = ai.reshape(1, S5_W)
    return wb, jnp.concatenate([ar, ar], axis=1), jnp.concatenate([-ai, ai], axis=1), wc


def _mix_cross_kernel(x_ref, pa_ref, pb_ref, pc_ref, pd_ref, wm_ref, g0_ref, b0_ref,
                      wq_ref, k_ref, v_ref, wo_ref, g_ref, b_ref, o_ref, *, nb):
    mixed = sum(_dot(p[...].astype(BF16), wm_ref[pl.ds(n * GROUP_W, GROUP_W), :])
                for n, p in enumerate((pa_ref, pb_ref, pc_ref, pd_ref)))
    x = _layer_norm(ALPHA * x_ref[...] + mixed, g0_ref[...], b0_ref[...])
    rows = x.shape[0] // nb
    q = (_dot(x.astype(BF16), wq_ref[...]) * QK_SCALE).astype(BF16)
    hs = [slice(h * HEAD_DIM, (h + 1) * HEAD_DIM) for h in range(N_HEADS)]
    pairs = [(b, h) for b in range(nb) for h in range(N_HEADS)]
    k = [k_ref[b].astype(BF16) for b in range(nb)]
    v = [v_ref[b].astype(BF16) for b in range(nb)]
    s = {(b, h): _dot_nt(q[b * rows:(b + 1) * rows, hs[h]], k[b][:, hs[h]]) for b, h in pairs}
    e = {c: jnp.exp(s[c] - jnp.max(s[c], axis=1, keepdims=True)) for c in pairs}
    p = {c: e[c] / jnp.sum(e[c], axis=1, keepdims=True) for c in pairs}
    pv = {(b, h): _dot(p[b, h].astype(BF16), v[b][:, hs[h]]) for b, h in pairs}
    o = jnp.concatenate([jnp.concatenate([pv[b, h] for h in range(N_HEADS)], axis=1) for b in range(nb)], axis=0)
    y = ALPHA * x + _dot(o.astype(BF16), wo_ref[...])
    o_ref[...] = _layer_norm(y, g_ref[...], b_ref[...])


def mix_cross_ln(x, parts, w_mix, g0, b0, row_blk0, n_batch, seq, tq, nb, mem_k, mem_v, wq, wo, g, b):
    nq = seq // tq if nb == 1 else 1

    def const_spec(shape):
        return pl.BlockSpec(shape, lambda bb, i: (0,) * len(shape))

    row_spec = pl.BlockSpec((tq, D_MODEL), lambda bb, i: (row_blk0 + bb * nq + i, 0))
    part_spec = pl.BlockSpec((tq, GROUP_W), lambda bb, i: (bb * nq + i, 0))
    mem_spec = pl.BlockSpec((nb, N_MEM, GROUP_W), lambda bb, i: (bb, 0, 0))
    return pl.pallas_call(
        functools.partial(_mix_cross_kernel, nb=nb), grid=(n_batch // nb, nq),
        in_specs=[row_spec, part_spec, part_spec, part_spec, part_spec, const_spec(w_mix.shape),
                  const_spec(g0.shape), const_spec(b0.shape), const_spec(wq.shape), mem_spec, mem_spec,
                  const_spec(wo.shape), const_spec(g.shape), const_spec(b.shape)],
        out_specs=row_spec, out_shape=jax.ShapeDtypeStruct(x.shape, F32), input_output_aliases={0: 0},
        compiler_params=_cp("arbitrary", "arbitrary"), name="mix_cross")(
            x, *parts, w_mix, g0, b0, wq, mem_k, mem_v, wo, g, b)


SWIGLU_ROWS = 256


def _swiglu_accumulate(xb_ref, wg, wu, wd, acc_ref, n_valid=None):
    wgb, wub, wdb = wg.astype(BF16), wu.astype(BF16), wd.astype(BF16)
    n_sub = xb_ref.shape[0] // SWIGLU_ROWS

    def hidden(r):
        xb = xb_ref[pl.ds(r * SWIGLU_ROWS, SWIGLU_ROWS), :]
        gate = _dot(xb, wgb)
        up = _dot(xb, wub)
        return (gate * jax.nn.sigmoid(gate) * up).astype(BF16)

    def first_sub_blocks(n):
        hid = hidden(0)
        for r in range(n):
            nxt = hidden(r + 1) if r + 1 < n else None
            acc_ref[pl.ds(r * SWIGLU_ROWS, SWIGLU_ROWS), :] += _dot(hid, wdb)
            hid = nxt

    if n_valid is None:
        first_sub_blocks(n_sub)
        return
    need = (n_valid + SWIGLU_ROWS - 1) // SWIGLU_ROWS
    for n in range(1, n_sub + 1):
        pl.when(need == n)(functools.partial(first_sub_blocks, n))


def _ffn_kernel(x_ref, wg_ref, wu_ref, wd_ref, g_ref, b_ref, o_ref, xb_ref, *, nf):
    j = pl.program_id(1)

    @pl.when(j == 0)
    def _():
        xb_ref[...] = x_ref[...].astype(BF16)
        o_ref[...] = jnp.zeros_like(o_ref)

    _swiglu_accumulate(xb_ref, wg_ref[...], wu_ref[...], wd_ref[...], o_ref)

    @pl.when(j == nf - 1)
    def _():
        o_ref[...] = _layer_norm(ALPHA * x_ref[...] + o_ref[...], g_ref[...], b_ref[...])


def ffn_ln(x, wg, wu, wd, g, b, tm, tf):
    m = x.shape[0]
    nf = D_FF // tf
    kern = functools.partial(_ffn_kernel, nf=nf)
    return pl.pallas_call(
        kern, grid=(m // tm, nf),
        in_specs=[pl.BlockSpec((tm, D_MODEL), lambda i, j: (i, 0)),
                  pl.BlockSpec((D_MODEL, tf), lambda i, j: (0, j)), pl.BlockSpec((D_MODEL, tf), lambda i, j: (0, j)),
                  pl.BlockSpec((tf, D_MODEL), lambda i, j: (j, 0)),
                  pl.BlockSpec((1, D_MODEL), lambda i, j: (0, 0)), pl.BlockSpec((1, D_MODEL), lambda i, j: (0, 0))],
        out_specs=pl.BlockSpec((tm, D_MODEL), lambda i, j: (i, 0)),
        out_shape=jax.ShapeDtypeStruct((m, D_MODEL), F32),
        scratch_shapes=[pltpu.VMEM((tm, D_MODEL), BF16)],
        compiler_params=_cp("arbitrary", "arbitrary"), name="ffn")(x, wg, wu, wd, g, b)


def _router_kernel(x_ref, w_ref, b_ref, lt_ref, o_ref, before_ref, cnt_ref):
    @pl.when(pl.program_id(0) == 0)
    def _():
        cnt_ref[...] = jnp.zeros_like(cnt_ref)

    logits = jnp.dot(x_ref[...], w_ref[...], preferred_element_type=F32, precision=lax.Precision.HIGHEST) + b_ref[...]
    lane = lax.broadcasted_iota(jnp.int32, logits.shape, 1)
    neg = jnp.float32(-jnp.inf)
    lg = jnp.where(lane < N_EXPERTS, logits, neg)
    m1 = jnp.max(lg, axis=1, keepdims=True)
    i1 = jnp.min(jnp.where(lg == m1, lane, LANES), axis=1, keepdims=True)
    lg2 = jnp.where(lane == i1, neg, lg)
    m2 = jnp.max(lg2, axis=1, keepdims=True)
    i2 = jnp.min(jnp.where(lg2 == m2, lane, LANES), axis=1, keepdims=True)
    e2 = jnp.exp(m2 - m1)
    g1 = 1.0 / (1.0 + e2)
    g2 = e2 / (1.0 + e2)
    out = jnp.where(lane == 0, i1.astype(F32), jnp.where(lane == 1, i2.astype(F32),
                    jnp.where(lane == 2, g1, jnp.where(lane == 3, g2, 0.0))))
    o_ref[...] = out
    chosen = jnp.where(jnp.logical_or(lane == i1, lane == i2), 1.0, 0.0)
    cnt = cnt_ref[0:1, :]
    before_ref[...] = _dot(lt_ref[...], chosen.astype(BF16)) + cnt
    cnt_ref[...] = jnp.broadcast_to(cnt + jnp.sum(chosen, axis=0, keepdims=True), cnt_ref.shape)


def router(x, w_pad, b_pad, tm):
    m = x.shape[0]
    lower = jnp.asarray(np.tril(np.ones((tm, tm), np.float32), -1), dtype=BF16)
    row = pl.BlockSpec((tm, LANES), lambda i: (i, 0))
    return pl.pallas_call(
        _router_kernel, grid=(m // tm,),
        in_specs=[pl.BlockSpec((tm, D_MODEL), lambda i: (i, 0)), pl.BlockSpec((D_MODEL, LANES), lambda i: (0, 0)),
                  pl.BlockSpec((1, LANES), lambda i: (0, 0)), pl.BlockSpec((tm, tm), lambda i: (0, 0))],
        out_specs=[row, row, pl.BlockSpec((8, LANES), lambda i: (0, 0))],
        out_shape=[jax.ShapeDtypeStruct((m, LANES), F32), jax.ShapeDtypeStruct((m, LANES), F32),
                   jax.ShapeDtypeStruct((8, LANES), F32)],
        compiler_params=_cp("arbitrary"), name="router")(x, w_pad, b_pad, lower)


def _moe_ffn_kernel(te_ref, nu_ref, tr_ref, x_ref, wg_ref, wu_ref, wd_ref, o_ref, xb_ref):
    i = pl.program_id(0)
    j = pl.program_id(1)
    used = i < nu_ref[0]

    @pl.when(used)
    def _():
        @pl.when(j == 0)
        def _():
            xb_ref[...] = x_ref[...].astype(BF16)
            o_ref[...] = jnp.zeros_like(o_ref)

        _swiglu_accumulate(xb_ref, wg_ref[0], wu_ref[0], wd_ref[0], o_ref, tr_ref[i])

    @pl.when(jnp.logical_and(jnp.logical_not(used), j == 0))
    def _():
        o_ref[...] = jnp.zeros_like(o_ref)


def moe_ffn(x_sorted, tile_expert, n_used, tile_rows, wg, wu, wd, tf):
    n_rows = x_sorted.shape[0]
    n_tiles = n_rows // MOE_TILE
    nf = D_FF // tf

    def jj(i, j, nu):
        return jnp.where(i < nu[0], j, nf - 1)

    gs = pltpu.PrefetchScalarGridSpec(
        num_scalar_prefetch=3, grid=(n_tiles, nf),
        in_specs=[pl.BlockSpec((MOE_TILE, D_MODEL), lambda i, j, te, nu, tr: (i, 0)),
                  pl.BlockSpec((1, D_MODEL, tf), lambda i, j, te, nu, tr: (te[i], 0, jj(i, j, nu))),
                  pl.BlockSpec((1, D_MODEL, tf), lambda i, j, te, nu, tr: (te[i], 0, jj(i, j, nu))),
                  pl.BlockSpec((1, tf, D_MODEL), lambda i, j, te, nu, tr: (te[i], jj(i, j, nu), 0))],
        out_specs=pl.BlockSpec((MOE_TILE, D_MODEL), lambda i, j, te, nu, tr: (i, 0)),
        scratch_shapes=[pltpu.VMEM((MOE_TILE, D_MODEL), BF16)])
    return pl.pallas_call(
        _moe_ffn_kernel, grid_spec=gs, out_shape=jax.ShapeDtypeStruct((n_rows, D_MODEL), F32),
        compiler_params=_cp("arbitrary", "arbitrary"), name="moe_ffn")(
            tile_expert, n_used, tile_rows, x_sorted, wg, wu, wd)


def _combine_ln_kernel(x_ref, r_ref, ya_ref, yb_ref, g_ref, b_ref, o_ref):
    r = r_ref[...]
    y = r[:, TOP_K:TOP_K + 1] * ya_ref[...] + r[:, TOP_K + 1:TOP_K + 2] * yb_ref[...]
    o_ref[...] = _layer_norm(ALPHA * x_ref[...] + y, g_ref[...], b_ref[...])


def combine_ln(x, r, ya, yb, g, b, tm, row_blk0, rows):
    row = pl.BlockSpec((tm, D_MODEL), lambda i: (row_blk0 + i, 0))
    vec = pl.BlockSpec((1, D_MODEL), lambda i: (0, 0))
    return pl.pallas_call(
        _combine_ln_kernel, grid=(rows // tm,),
        in_specs=[row, pl.BlockSpec((tm, LANES), lambda i: (row_blk0 + i, 0)), row, row, vec, vec],
        out_specs=pl.BlockSpec((tm, D_MODEL), lambda i: (i, 0)),
        out_shape=jax.ShapeDtypeStruct((rows, D_MODEL), F32),
        compiler_params=_cp("arbitrary"), name="combine_ln")(x, r, ya, yb, g, b)


def moe_ln(x, router_w, router_b, wg, wu, wd, g, b, tm, splits):
    m = x.shape[0]
    w_pad = jnp.zeros((D_MODEL, LANES), F32).at[:, :N_EXPERTS].set(router_w)
    b_pad = jnp.zeros((1, LANES), F32).at[0, :N_EXPERTS].set(router_b)
    r, before_f, counts_f = router(x, w_pad, b_pad, tm)
    top_idx = r[:, :TOP_K].astype(jnp.int32)
    n_slot = m * TOP_K
    onehot = (top_idx[:, :, None] == jnp.arange(N_EXPERTS, dtype=jnp.int32)).astype(jnp.int32)
    before = before_f[:, :N_EXPERTS].astype(jnp.int32)
    counts = counts_f[0, :N_EXPERTS].astype(jnp.int32)
    tiles_per = (counts + MOE_TILE - 1) // MOE_TILE
    tile_end = jnp.cumsum(tiles_per)
    row0 = (tile_end - tiles_per) * MOE_TILE
    dest = jnp.sum(onehot * (before + row0)[:, None, :], axis=2)
    n_tiles = -(-n_slot // MOE_TILE) + N_EXPERTS
    n_rows = n_tiles * MOE_TILE
    row_tok = (jnp.arange(n_rows, dtype=jnp.int32) % m).at[dest.reshape(-1)].set(
        jnp.arange(n_slot, dtype=jnp.int32) // TOP_K, unique_indices=True, mode='promise_in_bounds')
    n_used = tile_end[-1:].astype(jnp.int32)
    tile_ids = jnp.minimum(jnp.arange(n_tiles, dtype=jnp.int32), n_used[0] - 1)
    tile_expert = jnp.minimum(jnp.sum((tile_end[None, :] <= tile_ids[:, None]).astype(jnp.int32), axis=1),
                              N_EXPERTS - 1)
    tile_rows = jnp.clip(counts[tile_expert] - (tile_ids - (tile_end - tiles_per)[tile_expert]) * MOE_TILE,
                         0, MOE_TILE).astype(jnp.int32)
    x_sorted = x.at[row_tok].get(mode='promise_in_bounds')
    y_sorted = moe_ffn(x_sorted, tile_expert, n_used, tile_rows, wg, wu, wd, FFN_COL_TILE)
    ya = y_sorted.at[dest[:, 0]].get(mode='promise_in_bounds')
    yb = y_sorted.at[dest[:, 1]].get(mode='promise_in_bounds')
    return [combine_ln(x, r, ya, yb, g, b, t, blk0, rows) for blk0, rows, t in splits]


def kernel(x_prompt, x_sample, cache_sb_k, cache_sb_v, cache_mem_k, cache_mem_v, state_ml_C, state_ml_n, state_ml_m, state_rt_S, state_s5_re, state_s5_im, page_table, mem_prompt, w_in, sb_bias, ml_b_i, ml_b_f, ml_norm_g, rt_norm_g, s5_A_re, s5_A_im, s5_log_dt, s5_B_re, s5_B_im, s5_C_re, s5_C_im, s5_D, s5_glu_w, s5_glu_b, w_out, ca_wq, ca_wk, ca_wv, ca_wo, ln_g, ln_b, ffn_w_gate, ffn_w_up, ffn_w_down, moe_router_w, moe_router_b, moe_w_gate, moe_w_up, moe_w_down):
    bp, tp, _ = x_prompt.shape
    bs, ts, _ = x_sample.shape
    n_p, n_s = bp * tp, bs * ts
    tm = ROW_TILE
    assert (n_p + n_s) % FFN_ROW_TILE == 0 and n_p % n_s == 0 and tp % PROMPT_TILE == 0
    assert bp % SEQS_PER_STEP == 0 and bs % SEQS_PER_STEP == 0 and n_p % (SEQS_PER_STEP * ts) == 0
    x = jnp.concatenate([x_prompt.reshape(n_p, D_MODEL), x_sample.reshape(n_s, D_MODEL)], axis=0)
    uu = _suffix_matrix()
    g_off = 7 * GROUP_W
    half = HEAD_DIM // 2
    freq = ROPE_BASE ** (-jnp.arange(half, dtype=F32) / half)

    def rope_tables(pos):
        ang = pos.astype(F32)[:, None] * freq[None, :]
        cos, sin = jnp.cos(ang), jnp.sin(ang)
        return (jnp.tile(jnp.concatenate([cos, cos], axis=1), (1, N_HEADS)),
                jnp.tile(jnp.concatenate([-sin, sin], axis=1), (1, N_HEADS)))

    cos_p, sin_p = rope_tables(jnp.arange(tp, dtype=jnp.int32))
    cos_s, sin_s = rope_tables(PAST_LEN + jnp.arange(ts, dtype=jnp.int32))
    cache_kt = cache_sb_k.transpose(0, 1, 3, 4, 2)
    cache_vt = cache_sb_v.transpose(0, 1, 3, 4, 2)

    p_st = [[] for _ in range(10)]
    s_st = [[] for _ in range(8)]
    for l in range(DEPTH):
        wl = w_in[l]
        w_cat = jnp.concatenate([wl[:, :g_off], wl[:, g_off + 2 * N_HEADS:], wl[:, g_off:g_off + 2 * N_HEADS],
                                 jnp.zeros((D_MODEL, PROJ_W - wl.shape[1]), F32)], axis=1).astype(BF16)
        proj_p, kt_p, vt_p = in_proj_prompt(x, w_cat, bp, tp, PROMPT_TILE)
        proj_s = linear(x, w_cat, n_s, n_p // n_s, n_s)
        proj_p3 = proj_p.reshape(bp, tp, PROJ_W)
        proj_s3 = proj_s.reshape(bs, ts, PROJ_W)
        gate_bias = jnp.zeros((1, LANES), F32).at[0, :2 * N_HEADS].set(jnp.concatenate([ml_b_i[l], ml_b_f[l]]))
        ml_g = ml_norm_g[l][None, :]
        rt_g = rt_norm_g[l][None, :]
        wb, a1, a2, wc = _s5_weights(s5_A_re[l], s5_A_im[l], s5_log_dt[l], s5_B_re[l], s5_B_im[l],
                                     s5_C_re[l], s5_C_im[l])
        s5_d = s5_D[l][None, :]
        glu_w = s5_glu_w[l].astype(BF16)
        glu_b = s5_glu_b[l][None, :]

        o_sb_p = sb_attention_prompt(proj_p, sb_bias[l], uu, bp, tp, SB_QUERY_TILE)
        o_ml_p, ml_c_p, ml_n_p, ml_m_p = mlstm_mixer_prompt(proj_p3, gate_bias, ml_g, SEQS_PER_STEP, MIX_DTYPE)
        o_rt_p, rs_p = retention_mixer_prompt(proj_p3, cos_p, sin_p, rt_g, SEQS_PER_STEP, MIX_DTYPE)
        o_ml_p = o_ml_p.reshape(n_p, GROUP_W)
        o_rt_p = o_rt_p.reshape(n_p, GROUP_W)
        o_s5_p, h5_p = s5_mixer(proj_p3, S5_STEPS, wb, a1, a2, jnp.zeros((bp, 2 * S5_W), F32), wc, s5_d, glu_w,
                                glu_b, MIX_DTYPE)
        o_s5_p = o_s5_p.reshape(n_p, GROUP_W)

        o_sb_s = sb_attention_sample(proj_s, 0, ts, cache_kt, cache_vt, page_table, l, sb_bias[l], uu,
                                     SB_PAGES_PER_STEP)
        cn0 = jnp.concatenate([state_ml_C[:, l], state_ml_n[:, l][..., None],
                               jnp.zeros((bs, N_HEADS, HEAD_DIM, LANES - HEAD_DIM - 1), F32)], axis=-1)
        o_ml_s, cn_s, m_s = mlstm_mixer(proj_s3, gate_bias, ml_g, cn0, state_ml_m[:, l], SEQS_PER_STEP, F32)
        o_rt_s, rs_s = retention_mixer(proj_s3, cos_s, sin_s, rt_g, state_rt_S[:, l], SEQS_PER_STEP, F32)
        o_ml_s = o_ml_s.reshape(n_s, GROUP_W)
        o_rt_s = o_rt_s.reshape(n_s, GROUP_W)
        h0_s = jnp.concatenate([state_s5_re[:, l].reshape(bs, S5_W), state_s5_im[:, l].reshape(bs, S5_W)], axis=1)
        o_s5_s, h5_s = s5_mixer(proj_s3, ts, wb, a1, a2, h0_s, wc, s5_d, glu_w, glu_b, F32)
        o_s5_s = o_s5_s.reshape(n_s, GROUP_W)

        wo_mix = w_out[l].astype(BF16)
        g0, b0 = ln_g[l, 0][None, :], ln_b[l, 0][None, :]
        parts_p = (o_sb_p, o_ml_p, o_rt_p, o_s5_p)
        parts_s = (o_sb_s, o_ml_s, o_rt_s, o_s5_s)

        mem_kv = linear(mem_prompt.reshape(bp * N_MEM, D_MODEL),
                        jnp.concatenate([ca_wk[l], ca_wv[l]], axis=1).astype(BF16), PROMPT_TILE)
        mk_p = mem_kv[:, :GROUP_W].reshape(bp, N_MEM, GROUP_W)
        mv_p = mem_kv[:, GROUP_W:].reshape(bp, N_MEM, GROUP_W)
        wq = ca_wq[l].astype(BF16)
        wo = ca_wo[l].astype(BF16)
        g1, b1 = ln_g[l, 1][None, :], ln_b[l, 1][None, :]
        x = mix_cross_ln(x, parts_p, wo_mix, g0, b0, 0, bp, tp, PROMPT_TILE, 1, mk_p, mv_p, wq, wo, g1, b1)
        s_tile = SEQS_PER_STEP * ts
        x = mix_cross_ln(x, parts_s, wo_mix, g0, b0, n_p // s_tile, bs, ts, s_tile, SEQS_PER_STEP,
                         cache_mem_k[:, l].reshape(bs, N_MEM, GROUP_W),
                         cache_mem_v[:, l].reshape(bs, N_MEM, GROUP_W), wq, wo, g1, b1)

        g2, b2 = ln_g[l, 2][None, :], ln_b[l, 2][None, :]
        j = l // 2
        last = l == DEPTH - 1
        if l % 2 == 0:
            x = ffn_ln(x, ffn_w_gate[j], ffn_w_up[j], ffn_w_down[j], g2, b2, FFN_ROW_TILE, FFN_COL_TILE)
            y_out = (x[:n_p], x[n_p:]) if last else None
        else:
            splits = [(0, n_p, PROMPT_TILE), (n_p // n_s, n_s, n_s)] if last else [(0, n_p + n_s, tm)]
            y_out = moe_ln(x, moe_router_w[j], moe_router_b[j], moe_w_gate[j], moe_w_up[j], moe_w_down[j],
                           g2, b2, tm, splits)
            x = None if last else y_out[0]

        def heads(a, nb_, t_):
            return a.reshape(nb_, t_, N_HEADS, HEAD_DIM)

        p_st[0].append(kt_p)
        p_st[1].append(vt_p)
        p_st[2].append(heads(mk_p, bp, N_MEM))
        p_st[3].append(heads(mv_p, bp, N_MEM))
        s_st[0].append(heads(proj_s[:, C_SK * GROUP_W:(C_SK + 1) * GROUP_W], bs, ts))
        s_st[1].append(heads(proj_s[:, C_SV * GROUP_W:(C_SV + 1) * GROUP_W], bs, ts))
        ml_p = (ml_c_p, ml_n_p, ml_m_p)
        ml_s = (cn_s[..., :HEAD_DIM], cn_s[..., HEAD_DIM], m_s[:, 0, :N_HEADS])
        for st, ml, rs, h5, nb_ in ((p_st, ml_p, rs_p, h5_p, bp), (s_st, ml_s, rs_s, h5_s, bs)):
            off = 4 if st is p_st else 2
            st[off + 0].append(ml[0])
            st[off + 1].append(ml[1])
            st[off + 2].append(ml[2])
            st[off + 3].append(rs)
            st[off + 4].append(h5[:, :S5_W].reshape(nb_, S5_GROUPS, S5_STATE))
            st[off + 5].append(h5[:, S5_W:].reshape(nb_, S5_GROUPS, S5_STATE))

    y_prompt = y_out[0].reshape(bp, tp, D_MODEL)
    y_sample = y_out[1].reshape(bs, ts, D_MODEL)
    p_out = [jnp.stack(a, axis=1) for a in p_st]
    for i in range(2):
        p_out[i] = p_out[i].reshape(bp, DEPTH, N_HEADS, HEAD_DIM, tp).transpose(0, 1, 4, 2, 3)
    s_out = [jnp.stack(a, axis=1) for a in s_st]
    return (y_prompt, y_sample, *p_out, *s_out)
```

```python
import functools
import math

import numpy as np
import jax
import jax.numpy as jnp
from jax import lax
from jax.experimental import pallas as pl
from jax.experimental.pallas import tpu as pltpu

F32 = jnp.float32
BF16 = jnp.bfloat16

D_MODEL = 1024
DEPTH = 2
PAST_LEN = 8192
PAGE_SIZE = 128
HEAD_DIM = 64
N_HEADS = 4
GROUP_W = N_HEADS * HEAD_DIM
S5_GROUPS = 16
S5_GROUP = 16
S5_STATE = 64
S5_W = S5_GROUPS * S5_STATE
N_MEM = 256
D_FF = 3584
N_EXPERTS = 8
TOP_K = 2
CHUNK = 64
ROPE_BASE = 10000.0
LN_EPS = 1e-5
GN_EPS = 1e-6
ALPHA = (2 * DEPTH) ** 0.25
QK_SCALE = HEAD_DIM ** -0.5
LOG2E = math.log2(math.e)

LANES = 128
PROJ_W = 25 * LANES
C_SQ, C_SK, C_SV, C_MQ, C_MK, C_MV, C_MO, C_RQ, C_RK, C_RV, C_RG, C_SU = range(12)
C_GATES = 12 * GROUP_W // LANES
VMEM_LIMIT = 48 * 1024 * 1024
ROW_TILE = 640
PROMPT_TILE = 512
FFN_ROW_TILE = 2 * ROW_TILE
FFN_COL_TILE = 512
MOE_TILE = 1024
SB_QUERY_TILE = 256
SB_PAGES_PER_STEP = 16
SEQS_PER_STEP = 8
S5_STEPS = CHUNK
MIX_DTYPE = BF16


def _cp(*sem):
    return pltpu.CompilerParams(dimension_semantics=sem, vmem_limit_bytes=VMEM_LIMIT)


def _dot(a, b):
    return jnp.dot(a, b, preferred_element_type=F32)


def _dot_nt(a, b):
    return lax.dot_general(a, b, (((1,), (1,)), ((), ())), preferred_element_type=F32)


def _dot_tn(a, b):
    return lax.dot_general(a, b, (((0,), (0,)), ((), ())), preferred_element_type=F32)


def _layer_norm(y, g, b):
    mu = jnp.mean(y, axis=-1, keepdims=True)
    yc = y - mu
    var = jnp.mean(yc * yc, axis=-1, keepdims=True)
    return yc * lax.rsqrt(var + LN_EPS) * g + b


def _row_sum(x, scale=1.0):
    ones = jnp.full((x.shape[1], LANES), scale, BF16)
    hi = x.astype(BF16)
    lo = (x - hi.astype(F32)).astype(BF16)
    return (_dot(hi, ones) + _dot(lo, ones))[:, :x.shape[1]]


def _cumsum_rows(tril, x):
    hi = x.astype(BF16)
    lo = (x - hi.astype(F32)).astype(BF16)
    return _dot(tril, hi) + _dot(tril, lo)


def _head_norm_all(hd):
    inv = 1.0 / HEAD_DIM
    mu = {p: _row_sum(x, inv) for p, x in hd.items()}
    hc = {p: hd[p] - mu[p] for p in hd}
    var = {p: _row_sum(hc[p] * hc[p], inv) for p in hd}
    return {p: hc[p] * lax.rsqrt(var[p] + GN_EPS) for p in hd}


def _neg_softplus(z):
    return -(jnp.maximum(z, 0.0) + jnp.log1p(jnp.exp(-jnp.abs(z))))


def _log_sigmoid(z):
    return _neg_softplus(-z)


def _linear_kernel(x_ref, w_ref, o_ref):
    o_ref[...] = _dot(x_ref[...].astype(BF16), w_ref[...]).astype(o_ref.dtype)


def linear(x, w, tm, row_blk0=0, n_rows=None, out_dtype=F32):
    m, k = x.shape
    m = m if n_rows is None else n_rows
    n = w.shape[1]
    return pl.pallas_call(
        _linear_kernel, grid=(m // tm,),
        in_specs=[pl.BlockSpec((tm, k), lambda i: (row_blk0 + i, 0)), pl.BlockSpec((k, n), lambda i: (0, 0))],
        out_specs=pl.BlockSpec((tm, n), lambda i: (i, 0)),
        out_shape=jax.ShapeDtypeStruct((m, n), out_dtype),
        compiler_params=_cp("arbitrary"), name="linear")(x, w)


def _in_proj_kernel(x_ref, w_ref, o_ref, kt_ref, vt_ref):
    o = _dot(x_ref[...].astype(BF16), w_ref[...])
    o_ref[...] = o
    kt_ref[0] = o[:, C_SK * GROUP_W:(C_SK + 1) * GROUP_W].T
    vt_ref[0] = o[:, C_SV * GROUP_W:(C_SV + 1) * GROUP_W].T


def in_proj_prompt(x, w, n_batch, seq, tm):
    k = x.shape[1]
    n = w.shape[1]
    nt = seq // tm
    t_spec = pl.BlockSpec((1, GROUP_W, tm), lambda i: (i // nt, 0, i % nt))
    t_shape = jax.ShapeDtypeStruct((n_batch, GROUP_W, seq), F32)
    return pl.pallas_call(
        _in_proj_kernel, grid=(n_batch * nt,),
        in_specs=[pl.BlockSpec((tm, k), lambda i: (i, 0)), pl.BlockSpec((k, n), lambda i: (0, 0))],
        out_specs=[pl.BlockSpec((tm, n), lambda i: (i, 0)), t_spec, t_spec],
        out_shape=[jax.ShapeDtypeStruct((n_batch * seq, n), F32), t_shape, t_shape],
        compiler_params=_cp("arbitrary"), name="in_proj")(x, w)


def _suffix_matrix():
    j = np.arange(LANES)
    u = (j[:, None] >= j[None, :]).astype(np.float32)
    uu = np.concatenate([u, np.ones((LANES, LANES), np.float32)], axis=1)
    return jnp.asarray(np.concatenate([uu, uu], axis=0), dtype=BF16)


def _suffix_sums(lr, uu):
    hi = lr.astype(BF16)
    lo = (lr - hi.astype(F32)).astype(BF16)
    r = _dot(jnp.concatenate([hi, lo], axis=1), uu)
    return r[:, :LANES], r[:, LANES:]


def _log2_rem(z2):
    return jnp.minimum(-z2, 0.0) - jnp.log2(1.0 + jnp.exp2(-jnp.abs(z2)))


def _sb_prompt_kernel(bias_ref, q_ref, k_ref, v_ref, uu_ref, o_ref, acc_ref, car_ref, kb_ref, vb_ref, *, tq):
    i = pl.program_id(1)
    tk = LANES
    nsub = tq // tk
    acc_ref[...] = jnp.zeros_like(acc_ref)
    car_ref[...] = jnp.zeros_like(car_ref)

    @pl.when(i == 0)
    def _():
        for h in range(N_HEADS):
            kb_ref[h] = k_ref[:, h * HEAD_DIM:(h + 1) * HEAD_DIM].astype(BF16)
            vb_ref[h] = v_ref[:, h * HEAD_DIM:(h + 1) * HEAD_DIM].astype(BF16)

    q = (q_ref[...] * (QK_SCALE * LOG2E)).astype(BF16)
    qh = [q[:, h * HEAD_DIM:(h + 1) * HEAD_DIM] for h in range(N_HEADS)]
    b2 = [bias_ref[h] * LOG2E for h in range(N_HEADS)]
    uu = uu_ref[...]
    row = lax.broadcasted_iota(jnp.int32, (tq, tk), 0)
    col = lax.broadcasted_iota(jnp.int32, (tq, tk), 1)
    heads = range(N_HEADS)

    def block_pair(j_hi, causal_hi, causal_lo):
        r0 = [pl.multiple_of((j_hi - d) * tk, tk) for d in range(2)]
        z2 = [[_dot_nt(qh[h], kb_ref[h, pl.ds(r0[d], tk), :]) + b2[h] for h in heads] for d in range(2)]
        cs, tot = [], []
        for d, causal in enumerate((causal_hi, causal_lo)):
            lr = [_log2_rem(z) for z in z2[d]]
            if causal is not None:
                lr = [jnp.where(causal, a, 0.0) for a in lr]
            c, t = _suffix_sums(jnp.concatenate(lr, axis=0), uu)
            cs.append(c)
            tot.append(t)
        car = [car_ref[h] for h in heads]
        pv = []
        for d, causal in enumerate((causal_hi, causal_lo)):
            w = [jnp.exp2(z2[d][h] + cs[d][h * tq:(h + 1) * tq] + car[h]) for h in heads]
            if causal is not None:
                w = [jnp.where(causal, a, 0.0) for a in w]
            pv.append([_dot(w[h].astype(BF16), vb_ref[h, pl.ds(r0[d], tk), :]) for h in heads])
            car = [car[h] + tot[d][h * tq:(h + 1) * tq] for h in heads]
        for h in heads:
            acc_ref[h] += pv[0][h] + pv[1][h]
            car_ref[h] = car[h]

    assert nsub == 2
    block_pair(i * nsub + 1, (col + tk) < row, col < row)

    def body(jj, carry):
        block_pair(i * nsub - 1 - 2 * jj, None, None)
        return carry

    lax.fori_loop(0, i, body, 0)
    o_ref[...] = jnp.concatenate([acc_ref[h] for h in range(N_HEADS)], axis=1).astype(o_ref.dtype)


def sb_attention_prompt(proj, sb_bias, uu, n_batch, seq, tq):
    nq = seq // tq
    kern = functools.partial(_sb_prompt_kernel, tq=tq)
    return pl.pallas_call(
        kern, grid=(n_batch, nq),
        in_specs=[pl.BlockSpec(memory_space=pltpu.SMEM),
                  pl.BlockSpec((tq, GROUP_W), lambda b, i: (b * nq + i, C_SQ)),
                  pl.BlockSpec((seq, GROUP_W), lambda b, i: (b, C_SK)),
                  pl.BlockSpec((seq, GROUP_W), lambda b, i: (b, C_SV)),
                  pl.BlockSpec((2 * LANES, 2 * LANES), lambda b, i: (0, 0))],
        out_specs=pl.BlockSpec((tq, GROUP_W), lambda b, i: (b * nq + i, 0)),
        out_shape=jax.ShapeDtypeStruct((n_batch * seq, GROUP_W), MIX_DTYPE),
        scratch_shapes=[pltpu.VMEM((N_HEADS, tq, HEAD_DIM), F32), pltpu.VMEM((N_HEADS, tq, LANES), F32),
                        pltpu.VMEM((N_HEADS, seq, HEAD_DIM), BF16), pltpu.VMEM((N_HEADS, seq, HEAD_DIM), BF16)],
        compiler_params=_cp("arbitrary", "arbitrary"), name="sb_prompt")(sb_bias, proj, proj, proj, uu)


def _sb_sample_kernel(pt_ref, bias_ref, q_ref, kn_ref, vn_ref, u8_ref, uu_ref, *rest, n_pp, n_steps):
    k_refs = rest[:n_pp]
    v_refs = rest[n_pp:2 * n_pp]
    o_ref = rest[2 * n_pp]
    acc_ref, car_ref = rest[2 * n_pp + 1:]
    s = pl.program_id(1)
    nq = q_ref.shape[0]
    rows = N_HEADS * nq
    q = (q_ref[...] * (QK_SCALE * LOG2E)).astype(BF16)
    row_head = lax.broadcasted_iota(jnp.int32, (rows, GROUP_W), 0) // nq
    own = row_head == lax.broadcasted_iota(jnp.int32, (rows, GROUP_W), 1) // HEAD_DIM
    q_bd = jnp.where(own, jnp.concatenate([q] * N_HEADS, axis=0), jnp.zeros((), BF16))
    row_head_l = lax.broadcasted_iota(jnp.int32, (rows, LANES), 0) // nq
    b2 = jnp.zeros((rows, LANES), F32)
    for h in range(N_HEADS):
        b2 = jnp.where(row_head_l == h, bias_ref[h] * LOG2E, b2)

    @pl.when(s == 0)
    def _():
        kn = kn_ref[...].astype(BF16)
        vn = vn_ref[...].astype(BF16)
        t = lax.broadcasted_iota(jnp.int32, (rows, nq), 0) % nq
        causal = lax.broadcasted_iota(jnp.int32, (rows, nq), 1) < t
        z2 = _dot_nt(q_bd, kn) + b2[:, :nq]
        lr = jnp.where(causal, _log2_rem(z2), 0.0)
        cs = jnp.dot(lr, u8_ref[...], preferred_element_type=F32, precision=lax.Precision.HIGHEST)
        w = jnp.where(causal, jnp.exp2(z2 + cs), 0.0)
        acc_ref[...] = _dot(w.astype(BF16), vn)
        car_ref[...] = jnp.broadcast_to(jnp.sum(lr, axis=1, keepdims=True), (rows, LANES))

    z2s = [_dot(q_bd, k_refs[p][0, 0].reshape(GROUP_W, PAGE_SIZE).astype(BF16)) + b2 for p in range(n_pp)]
    cs_all, tot_all = _suffix_sums(_log2_rem(jnp.concatenate(z2s, axis=0)), uu_ref[...])
    car = car_ref[...]
    acc = acc_ref[...]
    for p in range(n_pp):
        w = jnp.exp2(z2s[p] + cs_all[p * rows:(p + 1) * rows] + car)
        acc = acc + _dot_nt(w.astype(BF16), v_refs[p][0, 0].reshape(GROUP_W, PAGE_SIZE).astype(BF16))
        car = car + tot_all[p * rows:(p + 1) * rows]
    car_ref[...] = car
    acc_ref[...] = acc

    @pl.when(s == n_steps - 1)
    def _():
        kept = jnp.where(own, acc, 0.0)
        o_ref[...] = sum(kept[h * nq:(h + 1) * nq] for h in range(N_HEADS))


def sb_attention_sample(proj, row_blk0, nq, cache_kt, cache_vt, page_table, layer, sb_bias, uu, n_pp):
    n_batch, n_pages = page_table.shape
    n_steps = n_pages // n_pp
    u8 = jnp.asarray((np.arange(nq)[:, None] >= np.arange(nq)[None, :]).astype(np.float32))

    def page_spec(p):
        return pl.BlockSpec((1, 1, N_HEADS, HEAD_DIM, PAGE_SIZE),
                            lambda b, s, pt: (pt[b, n_pages - 1 - (s * n_pp + p)], layer, 0, 0, 0))

    def row_spec(cblk):
        return pl.BlockSpec((nq, GROUP_W), lambda b, s, pt: (row_blk0 + b, cblk))

    kern = functools.partial(_sb_sample_kernel, n_pp=n_pp, n_steps=n_steps)
    gs = pltpu.PrefetchScalarGridSpec(
        num_scalar_prefetch=1, grid=(n_batch, n_steps),
        in_specs=[pl.BlockSpec(memory_space=pltpu.SMEM), row_spec(C_SQ), row_spec(C_SK), row_spec(C_SV),
                  pl.BlockSpec((nq, nq), lambda b, s, pt: (0, 0)),
                  pl.BlockSpec((2 * LANES, 2 * LANES), lambda b, s, pt: (0, 0))]
                 + [page_spec(p) for p in range(n_pp)] * 2,
        out_specs=pl.BlockSpec((nq, GROUP_W), lambda b, s, pt: (b, 0)),
        scratch_shapes=[pltpu.VMEM((N_HEADS * nq, GROUP_W), F32), pltpu.VMEM((N_HEADS * nq, LANES), F32)])
    return pl.pallas_call(
        kern, grid_spec=gs, out_shape=jax.ShapeDtypeStruct((n_batch * nq, GROUP_W), F32),
        compiler_params=_cp("arbitrary", "arbitrary"), name="sb_sample")(
            page_table, sb_bias, proj, proj, proj, u8, uu, *([cache_kt] * n_pp), *([cache_vt] * n_pp))


def _mlstm_kernel(m0_ref, q_ref, k_ref, v_ref, og_ref, gt_ref, gb_ref, ng_ref, tril_ref, cn0_ref,
                  o_ref, cn_ref, m_ref, ms_ref, *, bb, chunk, n_t):
    bi = pl.program_id(0)
    t = pl.program_id(1)
    seqs = range(bb)
    pairs = [(b, h) for b in seqs for h in range(N_HEADS)]

    @pl.when(t == 0)
    def _():
        cn_ref[...] = cn0_ref[...]
        for b, h in pairs:
            ms_ref[b * N_HEADS + h] = jnp.full((1, LANES), m0_ref[bi * bb + b, h], F32)

    tril = tril_ref[...]
    tri_mask = lax.broadcasted_iota(jnp.int32, (chunk, chunk), 1) <= lax.broadcasted_iota(jnp.int32, (chunk, chunk), 0)
    lane = lax.broadcasted_iota(jnp.int32, (chunk, HEAD_DIM), 1)
    ones_col = jnp.where(lane == 0, 1.0, 0.0).astype(F32)
    hs = lambda h: slice(h * HEAD_DIM, (h + 1) * HEAD_DIM)

    gt = [gt_ref[b] + gb_ref[...] for b in seqs]
    bc = [_cumsum_rows(tril, _log_sigmoid(g)) for g in gt]
    gt_t = [g.T for g in gt]
    bc_t = [x.T for x in bc]
    q = [q_ref[b].astype(BF16) for b in seqs]
    k = [(k_ref[b] * QK_SCALE).astype(BF16) for b in seqs]
    v = [v_ref[b] for b in seqs]
    m_prev = {p: ms_ref[p[0] * N_HEADS + p[1]][:, :1] for p in pairs}
    ig_col = {(b, h): gt[b][:, h:h + 1] for b, h in pairs}
    bc_col = {(b, h): bc[b][:, N_HEADS + h:N_HEADS + h + 1] for b, h in pairs}
    dm = {(b, h): jnp.where(tri_mask, bc_col[b, h] - (bc_t[b][N_HEADS + h:N_HEADS + h + 1, :] - gt_t[b][h:h + 1, :]),
                            -jnp.inf) for b, h in pairs}
    a = {p: bc_col[p] + m_prev[p] for p in pairs}
    m_new = {p: jnp.maximum(a[p], jnp.max(dm[p], axis=1, keepdims=True)) for p in pairs}
    inter = {p: jnp.exp(a[p] - m_new[p]) for p in pairs}
    s = {(b, h): _dot_nt(q[b][:, hs(h)], k[b][:, hs(h)]) * jnp.exp(dm[b, h] - m_new[b, h]) for b, h in pairs}
    v_ext = {(b, h): jnp.concatenate([v[b][:, hs(h)], ones_col], axis=1) for b, h in pairs}
    qc = {(b, h): _dot(q[b][:, hs(h)], cn_ref[b, h].astype(BF16)) for b, h in pairs}
    sv = {p: _dot(s[p].astype(BF16), v_ext[p].astype(BF16)) for p in pairs}
    m_last = {p: m_new[p][chunk - 1:chunk, :] for p in pairs}
    wl = {p: jnp.exp(bc_col[p][chunk - 1:chunk, :] - bc_col[p] + ig_col[p] - m_last[p]) for p in pairs}
    dl = {p: jnp.exp(a[p][chunk - 1:chunk, :] - m_last[p]) for p in pairs}
    upd = {(b, h): _dot_tn(k[b][:, hs(h)], (wl[b, h] * v_ext[b, h]).astype(BF16)) for b, h in pairs}
    rs = {p: _row_sum(s[p])[:, :1] for p in pairs}
    num = {p: inter[p] * qc[p][:, :HEAD_DIM] + sv[p][:, :HEAD_DIM] for p in pairs}
    den = {p: inter[p] * qc[p][:, HEAD_DIM:HEAD_DIM + 1] + rs[p] for p in pairs}
    hh = _head_norm_all({p: num[p] / jnp.maximum(jnp.abs(den[p]), jnp.exp(-m_new[p])) for p in pairs})
    for b, h in pairs:
        cn_ref[b, h] = dl[b, h] * cn_ref[b, h] + upd[b, h]
        ms_ref[b * N_HEADS + h] = jnp.broadcast_to(m_last[b, h], (1, LANES))
    for b in seqs:
        y = jnp.concatenate([hh[b, h] for h in range(N_HEADS)], axis=1)
        o_ref[b] = (y * ng_ref[...] * jax.nn.sigmoid(og_ref[b])).astype(o_ref.dtype)

    @pl.when(t == n_t - 1)
    def _():
        lane_m = lax.broadcasted_iota(jnp.int32, (1, LANES), 1)
        for b in seqs:
            m_out = jnp.zeros((1, LANES), F32)
            for h in range(N_HEADS):
                m_out = jnp.where(lane_m == h, ms_ref[b * N_HEADS + h], m_out)
            m_ref[b] = m_out


def mlstm_mixer(proj3, gate_bias, norm_g, cn0, m0, bb, out_dtype):
    n_batch, seq, _ = proj3.shape
    chunk = math.gcd(seq, CHUNK)
    n_t = seq // chunk
    tril = jnp.asarray(np.tril(np.ones((chunk, chunk), np.float32)), dtype=BF16)

    def row_spec(cblk, w=GROUP_W):
        return pl.BlockSpec((bb, chunk, w), lambda bi, t: (bi, t, cblk))

    def const_spec(shape):
        return pl.BlockSpec(shape, lambda bi, t: (0,) * len(shape))

    state_spec = pl.BlockSpec((bb, N_HEADS, HEAD_DIM, LANES), lambda bi, t: (bi, 0, 0, 0))
    kern = functools.partial(_mlstm_kernel, bb=bb, chunk=chunk, n_t=n_t)
    return pl.pallas_call(
        kern, grid=(n_batch // bb, n_t),
        in_specs=[pl.BlockSpec(memory_space=pltpu.SMEM),
                  row_spec(C_MQ), row_spec(C_MK), row_spec(C_MV), row_spec(C_MO), row_spec(C_GATES, LANES),
                  const_spec((1, LANES)), const_spec((1, GROUP_W)), const_spec((chunk, chunk)), state_spec],
        out_specs=[pl.BlockSpec((bb, chunk, GROUP_W), lambda bi, t: (bi, t, 0)), state_spec,
                   pl.BlockSpec((bb, 1, LANES), lambda bi, t: (bi, 0, 0))],
        out_shape=[jax.ShapeDtypeStruct((n_batch, seq, GROUP_W), out_dtype),
                   jax.ShapeDtypeStruct((n_batch, N_HEADS, HEAD_DIM, LANES), F32),
                   jax.ShapeDtypeStruct((n_batch, 1, LANES), F32)],
        scratch_shapes=[pltpu.VMEM((bb * N_HEADS, 1, LANES), F32)],
        compiler_params=_cp("arbitrary", "arbitrary"), name="mlstm")(
            m0, proj3, proj3, proj3, proj3, proj3, gate_bias, norm_g, tril, cn0)


def _split_dot(x, w):
    hi = x.astype(BF16)
    lo = (x - hi.astype(F32)).astype(BF16)
    return _dot(hi, w) + _dot(lo, w)


def _mlstm_pair_kernel(q_ref, k_ref, v_ref, og_ref, gt_ref, gb_ref, ng_ref, tril_ref, sel_ref, blk_ref,
                       o_ref, cn_ref, st_ref, *, bb, chunk):
    t_id = pl.program_id(1)

    @pl.when(t_id == 0)
    def _():
        cn_ref[...] = jnp.zeros_like(cn_ref)
        st_ref[...] = jnp.zeros_like(st_ref)

    seqs = range(bb)
    n_pair = N_HEADS // 2
    chains = [(b, p) for b in seqs for p in range(n_pair)]
    lane = lax.broadcasted_iota(jnp.int32, (chunk, LANES), 1)
    row = lax.broadcasted_iota(jnp.int32, (chunk, LANES), 0)
    low = lane < HEAD_DIM
    causal = (lane % HEAD_DIM) <= row
    eye2 = (lane % HEAD_DIM) == row
    blk = blk_ref[...]
    same = (lax.broadcasted_iota(jnp.int32, (LANES, LANES), 0) // HEAD_DIM
            == lax.broadcasted_iota(jnp.int32, (LANES, LANES), 1) // HEAD_DIM)
    ones_ll = jnp.ones((chunk, chunk), BF16)
    tril = tril_ref[...]
    pl_ = lambda p: slice(p * LANES, (p + 1) * LANES)
    zero_b = jnp.zeros((), BF16)

    gt = [gt_ref[b] + gb_ref[...] for b in seqs]
    bc = [_cumsum_rows(tril, _log_sigmoid(g)) for g in gt]
    def stacked(fn, xs):
        keys = list(xs)
        y = fn(jnp.concatenate([xs[c] for c in keys], axis=0))
        return {c: y[i * chunk:(i + 1) * chunk] for i, c in enumerate(keys)}

    by_blk = lambda x: _split_dot(x, blk)
    ig_all = stacked(lambda x: _split_dot(x, sel_ref[0]), dict(enumerate(gt)))
    bc_all = stacked(lambda x: _split_dot(x, sel_ref[1]), dict(enumerate(bc)))
    igc = {(b, p): ig_all[b][:, pl_(p)] for b, p in chains}
    bcc = {(b, p): bc_all[b][:, pl_(p)] for b, p in chains}
    gc = {c: igc[c] - bcc[c] for c in chains}
    grow = {c: _split_dot_left(ones_ll, jnp.where(eye2, gc[c], 0.0)) for c in chains}
    def running_max(x):
        shift = 1
        while shift < chunk:
            x = jnp.maximum(x, jnp.where(row >= shift, pltpu.roll(x, shift, 0), -jnp.inf))
            shift *= 2
        return x

    gmax = {c: running_max(gc[c]) for c in chains}
    m_prev = {(b, p): st_ref[b, p, 1:2, :] for b, p in chains}
    a = {c: bcc[c] + m_prev[c] for c in chains}
    m_stab = {c: jnp.maximum(a[c], bcc[c] + gmax[c]) for c in chains}
    inter = {c: jnp.exp(a[c] - m_stab[c]) for c in chains}
    dw = {c: jnp.exp(jnp.where(causal, bcc[c] + grow[c], -jnp.inf) - m_stab[c]) for c in chains}
    q2 = {(b, p): q_ref[b, :, pl_(p)].astype(BF16) for b, p in chains}
    k2f = {(b, p): k_ref[b, :, pl_(p)] * QK_SCALE for b, p in chains}
    k2 = {c: k2f[c].astype(BF16) for c in chains}
    v2 = {(b, p): v_ref[b, :, pl_(p)] for b, p in chains}
    kbd = {c: jnp.concatenate([jnp.where(low, k2[c], zero_b), jnp.where(low, zero_b, k2[c])], axis=0) for c in chains}
    v2b = {c: v2[c].astype(BF16) for c in chains}
    vbd = {c: jnp.concatenate([jnp.where(low, v2b[c], zero_b), jnp.where(low, zero_b, v2b[c])], axis=0)
           for c in chains}
    s = {c: _dot_nt(q2[c], kbd[c]) * dw[c] for c in chains}
    rs = stacked(by_blk, s)
    sv = {c: _dot(s[c].astype(BF16), vbd[c]) for c in chains}
    qc = {(b, p): _dot(q2[b, p], cn_ref[b, p].astype(BF16)) for b, p in chains}
    n_row = {(b, p): st_ref[b, p, 0:1, :] for b, p in chains}
    qn = stacked(by_blk, {c: q2[c].astype(F32) * n_row[c] for c in chains})
    hh = {}
    for c in chains:
        num = inter[c] * qc[c] + sv[c]
        den = inter[c] * qn[c] + rs[c]
        hh[c] = num / jnp.maximum(jnp.abs(den), jnp.exp(-m_stab[c]))
    inv = 1.0 / HEAD_DIM
    mu = stacked(by_blk, hh)
    hc = {c: hh[c] - mu[c] * inv for c in chains}
    var = stacked(by_blk, {c: hc[c] * hc[c] for c in chains})
    m_last = {c: m_stab[c][chunk - 1:chunk, :] for c in chains}
    wl = {c: jnp.exp(bcc[c][chunk - 1:chunk, :] - bcc[c] + igc[c] - m_last[c]) for c in chains}
    dl = {c: jnp.exp(a[c][chunk - 1:chunk, :] - m_last[c]) for c in chains}
    upd = {c: _dot_tn(k2[c], (wl[c] * v2[c]).astype(BF16)) for c in chains}
    for b, p in chains:
        c = (b, p)
        y = hc[c] * lax.rsqrt(var[c] * inv + GN_EPS)
        o_ref[b, :, pl_(p)] = (y * ng_ref[:, pl_(p)] * jax.nn.sigmoid(og_ref[b, :, pl_(p)])).astype(o_ref.dtype)
        cn_ref[b, p] = dl[c] * cn_ref[b, p] + jnp.where(same, upd[c], 0.0)
        st_ref[b, p, 0:1, :] = dl[c] * n_row[c] + jnp.sum(wl[c] * k2f[c], axis=0, keepdims=True)
        st_ref[b, p, 1:2, :] = m_last[c]


def _split_dot_left(w, x):
    hi = x.astype(BF16)
    lo = (x - hi.astype(F32)).astype(BF16)
    return _dot(w, hi) + _dot(w, lo)


def mlstm_mixer_prompt(proj3, gate_bias, norm_g, bb, out_dtype):
    n_batch, seq, _ = proj3.shape
    chunk = CHUNK
    n_pair = N_HEADS // 2
    tril = jnp.asarray(np.tril(np.ones((chunk, chunk), np.float32)), dtype=BF16)
    sel = np.zeros((2, LANES, GROUP_W), np.float32)
    for h in range(N_HEADS):
        sel[0, h, h * HEAD_DIM:(h + 1) * HEAD_DIM] = 1.0
        sel[1, N_HEADS + h, h * HEAD_DIM:(h + 1) * HEAD_DIM] = 1.0
    idx = np.arange(LANES) // HEAD_DIM
    blk = (idx[:, None] == idx[None, :]).astype(np.float32)

    def row_spec(cblk, w=GROUP_W):
        return pl.BlockSpec((bb, chunk, w), lambda bi, t: (bi, t, cblk))

    def const_spec(shape):
        return pl.BlockSpec(shape, lambda bi, t: (0,) * len(shape))

    cn_spec = pl.BlockSpec((bb, n_pair, LANES, LANES), lambda bi, t: (bi, 0, 0, 0))
    st_spec = pl.BlockSpec((bb, n_pair, 8, LANES), lambda bi, t: (bi, 0, 0, 0))
    kern = functools.partial(_mlstm_pair_kernel, bb=bb, chunk=chunk)
    o, cn, st = pl.pallas_call(
        kern, grid=(n_batch // bb, seq // chunk),
        in_specs=[row_spec(C_MQ), row_spec(C_MK), row_spec(C_MV), row_spec(C_MO), row_spec(C_GATES, LANES),
                  const_spec((1, LANES)), const_spec((1, GROUP_W)), const_spec((chunk, chunk)),
                  const_spec(sel.shape), const_spec(blk.shape)],
        out_specs=[pl.BlockSpec((bb, chunk, GROUP_W), lambda bi, t: (bi, t, 0)), cn_spec, st_spec],
        out_shape=[jax.ShapeDtypeStruct((n_batch, seq, GROUP_W), out_dtype),
                   jax.ShapeDtypeStruct((n_batch, n_pair, LANES, LANES), F32),
                   jax.ShapeDtypeStruct((n_batch, n_pair, 8, LANES), F32)],
        compiler_params=_cp("arbitrary", "arbitrary"), name="mlstm_prompt")(
            proj3, proj3, proj3, proj3, proj3, gate_bias, norm_g, tril,
            jnp.asarray(sel, dtype=BF16), jnp.asarray(blk, dtype=BF16))
    c5 = cn.reshape(n_batch, n_pair, 2, HEAD_DIM, 2, HEAD_DIM)
    c_state = jnp.stack([c5[:, :, j, :, j, :] for j in range(2)], axis=2).reshape(n_batch, N_HEADS, HEAD_DIM, HEAD_DIM)
    n_state = st[:, :, 0, :].reshape(n_batch, N_HEADS, HEAD_DIM)
    m_state = st[:, :, 1, :].reshape(n_batch, N_HEADS, HEAD_DIM)[:, :, 0]
    return o, c_state, n_state, m_state


def _rope(x, cos, sin_signed):
    lane = lax.broadcasted_iota(jnp.int32, x.shape, 1)
    half = HEAD_DIM // 2
    swapped = jnp.where((lane % HEAD_DIM) < half, pltpu.roll(x, x.shape[1] - half, 1), pltpu.roll(x, half, 1))
    return x * cos + swapped * sin_signed


def _retention_kernel(q_ref, k_ref, v_ref, gg_ref, cos_ref, sin_ref, ng_ref, dec_ref, int_ref, wl_ref, dl_ref,
                      s0_ref, o_ref, s_ref, *, bb):
    t = pl.program_id(1)
    seqs = range(bb)
    pairs = [(b, h) for b in seqs for h in range(N_HEADS)]
    hs = lambda h: slice(h * HEAD_DIM, (h + 1) * HEAD_DIM)

    @pl.when(t == 0)
    def _():
        s_ref[...] = s0_ref[...]

    cos = cos_ref[...]
    sin = sin_ref[...]
    q = [_rope(q_ref[b], cos, sin).astype(BF16) for b in seqs]
    k = [(_rope(k_ref[b], cos, sin) * QK_SCALE).astype(BF16) for b in seqs]
    v = [v_ref[b] for b in seqs]
    s = {(b, h): _dot_nt(q[b][:, hs(h)], k[b][:, hs(h)]) * dec_ref[h] for b, h in pairs}
    qs = {(b, h): _dot(q[b][:, hs(h)], s_ref[b, h].astype(BF16)) for b, h in pairs}
    sv = {(b, h): _dot(s[b, h].astype(BF16), v[b][:, hs(h)].astype(BF16)) for b, h in pairs}
    upd = {(b, h): _dot_tn(k[b][:, hs(h)], (wl_ref[h] * v[b][:, hs(h)]).astype(BF16)) for b, h in pairs}
    o = _head_norm_all({(b, h): int_ref[h] * qs[b, h] + sv[b, h] for b, h in pairs})
    for b, h in pairs:
        s_ref[b, h] = dl_ref[h] * s_ref[b, h] + upd[b, h]
    for b in seqs:
        gg = gg_ref[b]
        y = jnp.concatenate([o[b, h] for h in range(N_HEADS)], axis=1) * ng_ref[...] * (gg * jax.nn.sigmoid(gg))
        o_ref[b] = y.astype(o_ref.dtype)


def _retention_consts(chunk):
    log_g = np.log(1.0 - np.exp2(-5.0 - np.arange(N_HEADS, dtype=np.float64)))
    tau = np.arange(chunk, dtype=np.float64)
    rel = tau[:, None] - tau[None, :]
    decay = np.where(rel >= 0, np.exp(log_g[:, None, None] * np.maximum(rel, 0.0)), 0.0)
    inter = np.exp(log_g[:, None] * (tau + 1.0))[..., None]
    wl = np.exp(log_g[:, None] * (chunk - 1.0 - tau))[..., None]
    dl = np.exp(log_g * chunk)[:, None, None]
    return tuple(jnp.asarray(a, F32) for a in (decay, inter, wl, dl))


def _retention_pair_kernel(q_ref, k_ref, v_ref, gg_ref, cos_ref, sin_ref, ng_ref, dec_ref, int_ref, wl_ref, dl_ref,
                           swap_ref, blk_ref, o_ref, s_ref, *, bb, chunk):
    t_id = pl.program_id(1)

    @pl.when(t_id == 0)
    def _():
        s_ref[...] = jnp.zeros_like(s_ref)

    seqs = range(bb)
    n_pair = N_HEADS // 2
    chains = [(b, p) for b in seqs for p in range(n_pair)]
    low = lax.broadcasted_iota(jnp.int32, (chunk, LANES), 1) < HEAD_DIM
    same = (lax.broadcasted_iota(jnp.int32, (LANES, LANES), 0) // HEAD_DIM
            == lax.broadcasted_iota(jnp.int32, (LANES, LANES), 1) // HEAD_DIM)
    pl_ = lambda p: slice(p * LANES, (p + 1) * LANES)
    zero_b = jnp.zeros((), BF16)
    blk = blk_ref[...]
    cos = cos_ref[...]
    sin = sin_ref[...]

    def stacked(fn, xs):
        keys = list(xs)
        y = fn(jnp.concatenate([xs[c] for c in keys], axis=0))
        return {c: y[i * chunk:(i + 1) * chunk] for i, c in enumerate(keys)}

    swap = lambda x: _split_dot(x, swap_ref[...])
    qf = {b: q_ref[b] for b in seqs}
    kf = {b: k_ref[b] for b in seqs}
    q_sw = stacked(swap, qf)
    k_sw = stacked(swap, kf)
    q = {b: (qf[b] * cos + q_sw[b] * sin).astype(BF16) for b in seqs}
    k = {b: ((kf[b] * cos + k_sw[b] * sin) * QK_SCALE).astype(BF16) for b in seqs}
    k2 = {(b, p): k[b][:, pl_(p)] for b, p in chains}
    v2 = {(b, p): v_ref[b, :, pl_(p)] for b, p in chains}
    v2b = {c: v2[c].astype(BF16) for c in chains}
    kbd = {c: jnp.concatenate([jnp.where(low, k2[c], zero_b), jnp.where(low, zero_b, k2[c])], axis=0) for c in chains}
    vbd = {c: jnp.concatenate([jnp.where(low, v2b[c], zero_b), jnp.where(low, zero_b, v2b[c])], axis=0)
           for c in chains}
    s = {(b, p): _dot_nt(q[b][:, pl_(p)], kbd[b, p]) * dec_ref[p] for b, p in chains}
    qs = {(b, p): _dot(q[b][:, pl_(p)], s_ref[b, p].astype(BF16)) for b, p in chains}
    sv = {c: _dot(s[c].astype(BF16), vbd[c]) for c in chains}
    upd = {(b, p): _dot_tn(k2[b, p], (wl_ref[p] * v2[b, p]).astype(BF16)) for b, p in chains}
    o = {(b, p): int_ref[p] * qs[b, p] + sv[b, p] for b, p in chains}
    inv = 1.0 / HEAD_DIM
    by_blk = lambda x: _split_dot(x, blk)
    mu = stacked(by_blk, o)
    oc = {c: o[c] - mu[c] * inv for c in chains}
    var = stacked(by_blk, {c: oc[c] * oc[c] for c in chains})
    for b, p in chains:
        c = (b, p)
        gg = gg_ref[b, :, pl_(p)]
        y = oc[c] * lax.rsqrt(var[c] * inv + GN_EPS) * ng_ref[:, pl_(p)] * (gg * jax.nn.sigmoid(gg))
        o_ref[b, :, pl_(p)] = y.astype(o_ref.dtype)
        s_ref[b, p] = dl_ref[p] * s_ref[b, p] + jnp.where(same, upd[c], 0.0)


def retention_mixer_prompt(proj3, cos, sin_signed, norm_g, bb, out_dtype):
    n_batch, seq, _ = proj3.shape
    chunk = CHUNK
    n_pair = N_HEADS // 2
    dec, inter, wl, dl = _retention_consts(chunk)
    rep = lambda a: jnp.broadcast_to(a, a.shape[:-1] + (HEAD_DIM,))
    pair = lambda a: jnp.concatenate([a[0::2], a[1::2]], axis=-1)
    dec2, int2, wl2, dl2 = pair(dec), pair(rep(inter)), pair(rep(wl)), pair(rep(dl))
    src = np.arange(GROUP_W)
    partner = (src // HEAD_DIM) * HEAD_DIM + (src % HEAD_DIM + HEAD_DIM // 2) % HEAD_DIM
    swap = np.zeros((GROUP_W, GROUP_W), np.float32)
    swap[partner, src] = 1.0
    idx = np.arange(LANES) // HEAD_DIM
    blk = (idx[:, None] == idx[None, :]).astype(np.float32)

    def row_spec(cblk):
        return pl.BlockSpec((bb, chunk, GROUP_W), lambda bi, t: (bi, t, cblk))

    def const_spec(shape):
        return pl.BlockSpec(shape, lambda bi, t: (0,) * len(shape))

    pos_spec = pl.BlockSpec((chunk, GROUP_W), lambda bi, t: (t, 0))
    state_spec = pl.BlockSpec((bb, n_pair, LANES, LANES), lambda bi, t: (bi, 0, 0, 0))
    kern = functools.partial(_retention_pair_kernel, bb=bb, chunk=chunk)
    o, sbd = pl.pallas_call(
        kern, grid=(n_batch // bb, seq // chunk),
        in_specs=[row_spec(C_RQ), row_spec(C_RK), row_spec(C_RV), row_spec(C_RG), pos_spec, pos_spec,
                  const_spec((1, GROUP_W)), const_spec(dec2.shape), const_spec(int2.shape), const_spec(wl2.shape),
                  const_spec(dl2.shape), const_spec(swap.shape), const_spec(blk.shape)],
        out_specs=[pl.BlockSpec((bb, chunk, GROUP_W), lambda bi, t: (bi, t, 0)), state_spec],
        out_shape=[jax.ShapeDtypeStruct((n_batch, seq, GROUP_W), out_dtype),
                   jax.ShapeDtypeStruct((n_batch, n_pair, LANES, LANES), F32)],
        compiler_params=_cp("arbitrary", "arbitrary"), name="retention_prompt")(
            proj3, proj3, proj3, proj3, cos, sin_signed, norm_g, dec2, int2, wl2, dl2,
            jnp.asarray(swap, dtype=BF16), jnp.asarray(blk, dtype=BF16))
    s5d = sbd.reshape(n_batch, n_pair, 2, HEAD_DIM, 2, HEAD_DIM)
    state = jnp.stack([s5d[:, :, j, :, j, :] for j in range(2)], axis=2).reshape(n_batch, N_HEADS, HEAD_DIM, HEAD_DIM)
    return o, state


def retention_mixer(proj3, cos, sin_signed, norm_g, s0, bb, out_dtype):
    n_batch, seq, _ = proj3.shape
    chunk = math.gcd(seq, CHUNK)
    dec, inter, wl, dl = _retention_consts(chunk)

    def row_spec(cblk):
        return pl.BlockSpec((bb, chunk, GROUP_W), lambda bi, t: (bi, t, cblk))

    def const_spec(shape):
        return pl.BlockSpec(shape, lambda bi, t: (0,) * len(shape))

    pos_spec = pl.BlockSpec((chunk, GROUP_W), lambda bi, t: (t, 0))
    state_spec = pl.BlockSpec((bb, N_HEADS, HEAD_DIM, HEAD_DIM), lambda bi, t: (bi, 0, 0, 0))
    kern = functools.partial(_retention_kernel, bb=bb)
    return pl.pallas_call(
        kern, grid=(n_batch // bb, seq // chunk),
        in_specs=[row_spec(C_RQ), row_spec(C_RK), row_spec(C_RV), row_spec(C_RG), pos_spec, pos_spec,
                  const_spec((1, GROUP_W)), const_spec(dec.shape), const_spec(inter.shape), const_spec(wl.shape),
                  const_spec(dl.shape), state_spec],
        out_specs=[pl.BlockSpec((bb, chunk, GROUP_W), lambda bi, t: (bi, t, 0)), state_spec],
        out_shape=[jax.ShapeDtypeStruct((n_batch, seq, GROUP_W), out_dtype),
                   jax.ShapeDtypeStruct((n_batch, N_HEADS, HEAD_DIM, HEAD_DIM), F32)],
        compiler_params=_cp("arbitrary", "arbitrary"), name="retention")(
            proj3, proj3, proj3, proj3, cos, sin_signed, norm_g, dec, inter, wl, dl, s0)


def _s5_kernel(u_ref, wb_ref, a1_ref, a2_ref, h0_ref, wc_ref, d_ref, gw_ref, gb_ref, o_ref, hl_ref,
               hs_ref, ut_ref, yt_ref, *, nb, tt):
    c = pl.program_id(0)

    @pl.when(c == 0)
    def _():
        hl_ref[...] = h0_ref[...]

    halves = range(GROUP_W // LANES)
    for b in range(nb):
        ub = u_ref[b]
        for hf in halves:
            ut_ref[hf, pl.ds(b, tt, stride=nb), :] = ub[:, hf * LANES:(hf + 1) * LANES]
    u = jnp.concatenate([ut_ref[hf] for hf in halves], axis=1)
    hs_ref[...] = _dot(u.astype(BF16), wb_ref[...])
    a1 = jnp.broadcast_to(a1_ref[...], (nb, 2 * S5_W))
    a2 = jnp.broadcast_to(a2_ref[...], (nb, 2 * S5_W))

    def step(t, h):
        r0 = pl.multiple_of(t * nb, nb)
        swapped = jnp.concatenate([h[:, S5_W:], h[:, :S5_W]], axis=1)
        h = a1 * h + a2 * swapped + hs_ref[pl.ds(r0, nb), :]
        hs_ref[pl.ds(r0, nb), :] = h
        return h

    hl_ref[...] = lax.fori_loop(0, tt, step, hl_ref[...])
    y = _dot(hs_ref[...].astype(BF16), wc_ref[...]) + d_ref[...] * u
    g5 = jax.nn.gelu(y)
    yt = g5 * jax.nn.sigmoid(_dot(g5.astype(BF16), gw_ref[...]) + gb_ref[...])
    for hf in halves:
        yt_ref[hf] = yt[:, hf * LANES:(hf + 1) * LANES]
    for b in range(nb):
        o_ref[b] = jnp.concatenate([yt_ref[hf, pl.ds(b, tt, stride=nb), :] for hf in halves],
                                   axis=1).astype(o_ref.dtype)


def s5_mixer(proj3, tt, wb, a1, a2, h0, wc, d, glu_w, glu_b, out_dtype):
    nb, seq, _ = proj3.shape
    rows = tt * nb

    def const_spec(shape):
        return pl.BlockSpec(shape, lambda c: (0,) * len(shape))

    kern = functools.partial(_s5_kernel, nb=nb, tt=tt)
    return pl.pallas_call(
        kern, grid=(seq // tt,),
        in_specs=[pl.BlockSpec((nb, tt, GROUP_W), lambda c: (0, c, C_SU)),
                  const_spec(wb.shape), const_spec(a1.shape), const_spec(a2.shape), const_spec(h0.shape),
                  const_spec(wc.shape), const_spec(d.shape), const_spec(glu_w.shape), const_spec(glu_b.shape)],
        out_specs=[pl.BlockSpec((nb, tt, GROUP_W), lambda c: (0, c, 0)), const_spec(h0.shape)],
        out_shape=[jax.ShapeDtypeStruct((nb, seq, GROUP_W), out_dtype), jax.ShapeDtypeStruct(h0.shape, F32)],
        scratch_shapes=[pltpu.VMEM((rows, 2 * S5_W), F32), pltpu.VMEM((GROUP_W // LANES, rows, LANES), F32),
                        pltpu.VMEM((GROUP_W // LANES, rows, LANES), F32)],
        compiler_params=_cp("arbitrary"), name="s5")(proj3, wb, a1, a2, h0, wc, d, glu_w, glu_b)


def _s5_weights(a_re, a_im, log_dt, b_re, b_im, c_re, c_im):
    dt = jnp.exp(log_dt)
    mag = jnp.exp(a_re * dt)
    ar, ai = mag * jnp.cos(a_im * dt), mag * jnp.sin(a_im * dt)
    den = a_re * a_re + a_im * a_im
    cr = ((ar - 1.0) * a_re + ai * a_im) / den
    ci = (ai * a_re - (ar - 1.0) * a_im) / den
    bb_re = cr[..., None] * b_re - ci[..., None] * b_im
    bb_im = cr[..., None] * b_im + ci[..., None] * b_re
    eye = jnp.eye(S5_GROUPS, dtype=F32)

    def in_map(m):
        return jnp.einsum('gpc,gh->gchp', m, eye).reshape(S5_GROUPS * S5_GROUP, S5_W)

    def out_map(m):
        return jnp.einsum('gcp,gh->gphc', m, eye).reshape(S5_W, S5_GROUPS * S5_GROUP)

    wb = jnp.concatenate([in_map(bb_re), in_map(bb_im)], axis=1).astype(BF16)
    wc = jnp.concatenate([out_map(c_re), -out_map(c_im)], axis=0).astype(BF16)
    ar = ar.reshape(1, S5_W)
    ai = ai.reshape(1, S5_W)
    return wb, jnp.concatenate([ar, ar], axis=1), jnp.concatenate([-ai, ai], axis=1), wc


def _mix_cross_kernel(x_ref, pa_ref, pb_ref, pc_ref, pd_ref, wm_ref, g0_ref, b0_ref,
                      wq_ref, k_ref, v_ref, wo_ref, g_ref, b_ref, o_ref, *, nb):
    mixed = sum(_dot(p[...].astype(BF16), wm_ref[pl.ds(n * GROUP_W, GROUP_W), :])
                for n, p in enumerate((pa_ref, pb_ref, pc_ref, pd_ref)))
    x = _layer_norm(ALPHA * x_ref[...] + mixed, g0_ref[...], b0_ref[...])
    rows = x.shape[0] // nb
    q = (_dot(x.astype(BF16), wq_ref[...]) * QK_SCALE).astype(BF16)
    hs = [slice(h * HEAD_DIM, (h + 1) * HEAD_DIM) for h in range(N_HEADS)]
    pairs = [(b, h) for b in range(nb) for h in range(N_HEADS)]
    k = [k_ref[b].astype(BF16) for b in range(nb)]
    v = [v_ref[b].astype(BF16) for b in range(nb)]
    s = {(b, h): _dot_nt(q[b * rows:(b + 1) * rows, hs[h]], k[b][:, hs[h]]) for b, h in pairs}
    e = {c: jnp.exp(s[c] - jnp.max(s[c], axis=1, keepdims=True)) for c in pairs}
    p = {c: e[c] / jnp.sum(e[c], axis=1, keepdims=True) for c in pairs}
    pv = {(b, h): _dot(p[b, h].astype(BF16), v[b][:, hs[h]]) for b, h in pairs}
    o = jnp.concatenate([jnp.concatenate([pv[b, h] for h in range(N_HEADS)], axis=1) for b in range(nb)], axis=0)
    y = ALPHA * x + _dot(o.astype(BF16), wo_ref[...])
    o_ref[...] = _layer_norm(y, g_ref[...], b_ref[...])


def mix_cross_ln(x, parts, w_mix, g0, b0, row_blk0, n_batch, seq, tq, nb, mem_k, mem_v, wq, wo, g, b):
    nq = seq // tq if nb == 1 else 1

    def const_spec(shape):
        return pl.BlockSpec(shape, lambda bb, i: (0,) * len(shape))

    row_spec = pl.BlockSpec((tq, D_MODEL), lambda bb, i: (row_blk0 + bb * nq + i, 0))
    part_spec = pl.BlockSpec((tq, GROUP_W), lambda bb, i: (bb * nq + i, 0))
    mem_spec = pl.BlockSpec((nb, N_MEM, GROUP_W), lambda bb, i: (bb, 0, 0))
    return pl.pallas_call(
        functools.partial(_mix_cross_kernel, nb=nb), grid=(n_batch // nb, nq),
        in_specs=[row_spec, part_spec, part_spec, part_spec, part_spec, const_spec(w_mix.shape),
                  const_spec(g0.shape), const_spec(b0.shape), const_spec(wq.shape), mem_spec, mem_spec,
                  const_spec(wo.shape), const_spec(g.shape), const_spec(b.shape)],
        out_specs=row_spec, out_shape=jax.ShapeDtypeStruct(x.shape, F32), input_output_aliases={0: 0},
        compiler_params=_cp("arbitrary", "arbitrary"), name="mix_cross")(
            x, *parts, w_mix, g0, b0, wq, mem_k, mem_v, wo, g, b)


SWIGLU_ROWS = 256


def _swiglu_accumulate(xb_ref, wg, wu, wd, acc_ref, n_valid=None):
    wgb, wub, wdb = wg.astype(BF16), wu.astype(BF16), wd.astype(BF16)
    n_sub = xb_ref.shape[0] // SWIGLU_ROWS

    def hidden(r):
        xb = xb_ref[pl.ds(r * SWIGLU_ROWS, SWIGLU_ROWS), :]
        gate = _dot(xb, wgb)
        up = _dot(xb, wub)
        return (gate * jax.nn.sigmoid(gate) * up).astype(BF16)

    def first_sub_blocks(n):
        hid = hidden(0)
        for r in range(n):
            nxt = hidden(r + 1) if r + 1 < n else None
            acc_ref[pl.ds(r * SWIGLU_ROWS, SWIGLU_ROWS), :] += _dot(hid, wdb)
            hid = nxt

    if n_valid is None:
        first_sub_blocks(n_sub)
        return
    need = (n_valid + SWIGLU_ROWS - 1) // SWIGLU_ROWS
    for n in range(1, n_sub + 1):
        pl.when(need == n)(functools.partial(first_sub_blocks, n))


def _ffn_kernel(x_ref, wg_ref, wu_ref, wd_ref, g_ref, b_ref, o_ref, xb_ref, *, nf):
    j = pl.program_id(1)

    @pl.when(j == 0)
    def _():
        xb_ref[...] = x_ref[...].astype(BF16)
        o_ref[...] = jnp.zeros_like(o_ref)

    _swiglu_accumulate(xb_ref, wg_ref[...], wu_ref[...], wd_ref[...], o_ref)

    @pl.when(j == nf - 1)
    def _():
        o_ref[...] = _layer_norm(ALPHA * x_ref[...] + o_ref[...], g_ref[...], b_ref[...])


def ffn_ln(x, wg, wu, wd, g, b, tm, tf):
    m = x.shape[0]
    nf = D_FF // tf
    kern = functools.partial(_ffn_kernel, nf=nf)
    return pl.pallas_call(
        kern, grid=(m // tm, nf),
        in_specs=[pl.BlockSpec((tm, D_MODEL), lambda i, j: (i, 0)),
                  pl.BlockSpec((D_MODEL, tf), lambda i, j: (0, j)), pl.BlockSpec((D_MODEL, tf), lambda i, j: (0, j)),
                  pl.BlockSpec((tf, D_MODEL), lambda i, j: (j, 0)),
                  pl.BlockSpec((1, D_MODEL), lambda i, j: (0, 0)), pl.BlockSpec((1, D_MODEL), lambda i, j: (0, 0))],
        out_specs=pl.BlockSpec((tm, D_MODEL), lambda i, j: (i, 0)),
        out_shape=jax.ShapeDtypeStruct((m, D_MODEL), F32),
        scratch_shapes=[pltpu.VMEM((tm, D_MODEL), BF16)],
        compiler_params=_cp("arbitrary", "arbitrary"), name="ffn")(x, wg, wu, wd, g, b)


def _router_kernel(x_ref, w_ref, b_ref, lt_ref, o_ref, before_ref, cnt_ref):
    @pl.when(pl.program_id(0) == 0)
    def _():
        cnt_ref[...] = jnp.zeros_like(cnt_ref)

    logits = jnp.dot(x_ref[...], w_ref[...], preferred_element_type=F32, precision=lax.Precision.HIGHEST) + b_ref[...]
    lane = lax.broadcasted_iota(jnp.int32, logits.shape, 1)
    neg = jnp.float32(-jnp.inf)
    lg = jnp.where(lane < N_EXPERTS, logits, neg)
    m1 = jnp.max(lg, axis=1, keepdims=True)
    i1 = jnp.min(jnp.where(lg == m1, lane, LANES), axis=1, keepdims=True)
    lg2 = jnp.where(lane == i1, neg, lg)
    m2 = jnp.max(lg2, axis=1, keepdims=True)
    i2 = jnp.min(jnp.where(lg2 == m2, lane, LANES), axis=1, keepdims=True)
    e2 = jnp.exp(m2 - m1)
    g1 = 1.0 / (1.0 + e2)
    g2 = e2 / (1.0 + e2)
    out = jnp.where(lane == 0, i1.astype(F32), jnp.where(lane == 1, i2.astype(F32),
                    jnp.where(lane == 2, g1, jnp.where(lane == 3, g2, 0.0))))
    o_ref[...] = out
    chosen = jnp.where(jnp.logical_or(lane == i1, lane == i2), 1.0, 0.0)
    cnt = cnt_ref[0:1, :]
    before_ref[...] = _dot(lt_ref[...], chosen.astype(BF16)) + cnt
    cnt_ref[...] = jnp.broadcast_to(cnt + jnp.sum(chosen, axis=0, keepdims=True), cnt_ref.shape)


def router(x, w_pad, b_pad, tm):
    m = x.shape[0]
    lower = jnp.asarray(np.tril(np.ones((tm, tm), np.float32), -1), dtype=BF16)
    row = pl.BlockSpec((tm, LANES), lambda i: (i, 0))
    return pl.pallas_call(
        _router_kernel, grid=(m // tm,),
        in_specs=[pl.BlockSpec((tm, D_MODEL), lambda i: (i, 0)), pl.BlockSpec((D_MODEL, LANES), lambda i: (0, 0)),
                  pl.BlockSpec((1, LANES), lambda i: (0, 0)), pl.BlockSpec((tm, tm), lambda i: (0, 0))],
        out_specs=[row, row, pl.BlockSpec((8, LANES), lambda i: (0, 0))],
        out_shape=[jax.ShapeDtypeStruct((m, LANES), F32), jax.ShapeDtypeStruct((m, LANES), F32),
                   jax.ShapeDtypeStruct((8, LANES), F32)],
        compiler_params=_cp("arbitrary"), name="router")(x, w_pad, b_pad, lower)


def _moe_ffn_kernel(te_ref, nu_ref, tr_ref, x_ref, wg_ref, wu_ref, wd_ref, o_ref, xb_ref):
    i = pl.program_id(0)
    j = pl.program_id(1)
    used = i < nu_ref[0]

    @pl.when(used)
    def _():
        @pl.when(j == 0)
        def _():
            xb_ref[...] = x_ref[...].astype(BF16)
            o_ref[...] = jnp.zeros_like(o_ref)

        _swiglu_accumulate(xb_ref, wg_ref[0], wu_ref[0], wd_ref[0], o_ref, tr_ref[i])

    @pl.when(jnp.logical_and(jnp.logical_not(used), j == 0))
    def _():
        o_ref[...] = jnp.zeros_like(o_ref)


def moe_ffn(x_sorted, tile_expert, n_used, tile_rows, wg, wu, wd, tf):
    n_rows = x_sorted.shape[0]
    n_tiles = n_rows // MOE_TILE
    nf = D_FF // tf

    def jj(i, j, nu):
        return jnp.where(i < nu[0], j, nf - 1)

    gs = pltpu.PrefetchScalarGridSpec(
        num_scalar_prefetch=3, grid=(n_tiles, nf),
        in_specs=[pl.BlockSpec((MOE_TILE, D_MODEL), lambda i, j, te, nu, tr: (i, 0)),
                  pl.BlockSpec((1, D_MODEL, tf), lambda i, j, te, nu, tr: (te[i], 0, jj(i, j, nu))),
                  pl.BlockSpec((1, D_MODEL, tf), lambda i, j, te, nu, tr: (te[i], 0, jj(i, j, nu))),
                  pl.BlockSpec((1, tf, D_MODEL), lambda i, j, te, nu, tr: (te[i], jj(i, j, nu), 0))],
        out_specs=pl.BlockSpec((MOE_TILE, D_MODEL), lambda i, j, te, nu, tr: (i, 0)),
        scratch_shapes=[pltpu.VMEM((MOE_TILE, D_MODEL), BF16)])
    return pl.pallas_call(
        _moe_ffn_kernel, grid_spec=gs, out_shape=jax.ShapeDtypeStruct((n_rows, D_MODEL), F32),
        compiler_params=_cp("arbitrary", "arbitrary"), name="moe_ffn")(
            tile_expert, n_used, tile_rows, x_sorted, wg, wu, wd)


def _combine_ln_kernel(x_ref, r_ref, ya_ref, yb_ref, g_ref, b_ref, o_ref):
    r = r_ref[...]
    y = r[:, TOP_K:TOP_K + 1] * ya_ref[...] + r[:, TOP_K + 1:TOP_K + 2] * yb_ref[...]
    o_ref[...] = _layer_norm(ALPHA * x_ref[...] + y, g_ref[...], b_ref[...])


def combine_ln(x, r, ya, yb, g, b, tm, row_blk0, rows):
    row = pl.BlockSpec((tm, D_MODEL), lambda i: (row_blk0 + i, 0))
    vec = pl.BlockSpec((1, D_MODEL), lambda i: (0, 0))
    return pl.pallas_call(
        _combine_ln_kernel, grid=(rows // tm,),
        in_specs=[row, pl.BlockSpec((tm, LANES), lambda i: (row_blk0 + i, 0)), row, row, vec, vec],
        out_specs=pl.BlockSpec((tm, D_MODEL), lambda i: (i, 0)),
        out_shape=jax.ShapeDtypeStruct((rows, D_MODEL), F32),
        compiler_params=_cp("arbitrary"), name="combine_ln")(x, r, ya, yb, g, b)


def moe_ln(x, router_w, router_b, wg, wu, wd, g, b, tm, splits):
    m = x.shape[0]
    w_pad = jnp.zeros((D_MODEL, LANES), F32).at[:, :N_EXPERTS].set(router_w)
    b_pad = jnp.zeros((1, LANES), F32).at[0, :N_EXPERTS].set(router_b)
    r, before_f, counts_f = router(x, w_pad, b_pad, tm)
    top_idx = r[:, :TOP_K].astype(jnp.int32)
    n_slot = m * TOP_K
    onehot = (top_idx[:, :, None] == jnp.arange(N_EXPERTS, dtype=jnp.int32)).astype(jnp.int32)
    before = before_f[:, :N_EXPERTS].astype(jnp.int32)
    counts = counts_f[0, :N_EXPERTS].astype(jnp.int32)
    tiles_per = (counts + MOE_TILE - 1) // MOE_TILE
    tile_end = jnp.cumsum(tiles_per)
    row0 = (tile_end - tiles_per) * MOE_TILE
    dest = jnp.sum(onehot * (before + row0)[:, None, :], axis=2)
    n_tiles = -(-n_slot // MOE_TILE) + N_EXPERTS
    n_rows = n_tiles * MOE_TILE
    row_tok = (jnp.arange(n_rows, dtype=jnp.int32) % m).at[dest.reshape(-1)].set(
        jnp.arange(n_slot, dtype=jnp.int32) // TOP_K, unique_indices=True, mode='promise_in_bounds')
    n_used = tile_end[-1:].astype(jnp.int32)
    tile_ids = jnp.minimum(jnp.arange(n_tiles, dtype=jnp.int32), n_used[0] - 1)
    tile_expert = jnp.minimum(jnp.sum((tile_end[None, :] <= tile_ids[:, None]).astype(jnp.int32), axis=1),
                              N_EXPERTS - 1)
    tile_rows = jnp.clip(counts[tile_expert] - (tile_ids - (tile_end - tiles_per)[tile_expert]) * MOE_TILE,
                         0, MOE_TILE).astype(jnp.int32)
    x_sorted = x.at[row_tok].get(mode='promise_in_bounds')
    y_sorted = moe_ffn(x_sorted, tile_expert, n_used, tile_rows, wg, wu, wd, FFN_COL_TILE)
    ya = y_sorted.at[dest[:, 0]].get(mode='promise_in_bounds')
    yb = y_sorted.at[dest[:, 1]].get(mode='promise_in_bounds')
    return [combine_ln(x, r, ya, yb, g, b, t, blk0, rows) for blk0, rows, t in splits]


def kernel(x_prompt, x_sample, cache_sb_k, cache_sb_v, cache_mem_k, cache_mem_v, state_ml_C, state_ml_n, state_ml_m, state_rt_S, state_s5_re, state_s5_im, page_table, mem_prompt, w_in, sb_bias, ml_b_i, ml_b_f, ml_norm_g, rt_norm_g, s5_A_re, s5_A_im, s5_log_dt, s5_B_re, s5_B_im, s5_C_re, s5_C_im, s5_D, s5_glu_w, s5_glu_b, w_out, ca_wq, ca_wk, ca_wv, ca_wo, ln_g, ln_b, ffn_w_gate, ffn_w_up, ffn_w_down, moe_router_w, moe_router_b, moe_w_gate, moe_w_up, moe_w_down):
    bp, tp, _ = x_prompt.shape
    bs, ts, _ = x_sample.shape
    n_p, n_s = bp * tp, bs * ts
    tm = ROW_TILE
    assert (n_p + n_s) % FFN_ROW_TILE == 0 and n_p % n_s == 0 and tp % PROMPT_TILE == 0
    assert bp % SEQS_PER_STEP == 0 and bs % SEQS_PER_STEP == 0 and n_p % (SEQS_PER_STEP * ts) == 0
    x = jnp.concatenate([x_prompt.reshape(n_p, D_MODEL), x_sample.reshape(n_s, D_MODEL)], axis=0)
    uu = _suffix_matrix()
    g_off = 7 * GROUP_W
    half = HEAD_DIM // 2
    freq = ROPE_BASE ** (-jnp.arange(half, dtype=F32) / half)

    def rope_tables(pos):
        ang = pos.astype(F32)[:, None] * freq[None, :]
        cos, sin = jnp.cos(ang), jnp.sin(ang)
        return (jnp.tile(jnp.concatenate([cos, cos], axis=1), (1, N_HEADS)),
                jnp.tile(jnp.concatenate([-sin, sin], axis=1), (1, N_HEADS)))

    cos_p, sin_p = rope_tables(jnp.arange(tp, dtype=jnp.int32))
    cos_s, sin_s = rope_tables(PAST_LEN + jnp.arange(ts, dtype=jnp.int32))
    cache_kt = cache_sb_k.transpose(0, 1, 3, 4, 2)
    cache_vt = cache_sb_v.transpose(0, 1, 3, 4, 2)

    p_st = [[] for _ in range(10)]
    s_st = [[] for _ in range(8)]
    for l in range(DEPTH):
        wl = w_in[l]
        w_cat = jnp.concatenate([wl[:, :g_off], wl[:, g_off + 2 * N_HEADS:], wl[:, g_off:g_off + 2 * N_HEADS],
                                 jnp.zeros((D_MODEL, PROJ_W - wl.shape[1]), F32)], axis=1).astype(BF16)
        proj_p, kt_p, vt_p = in_proj_prompt(x, w_cat, bp, tp, PROMPT_TILE)
        proj_s = linear(x, w_cat, n_s, n_p // n_s, n_s)
        proj_p3 = proj_p.reshape(bp, tp, PROJ_W)
        proj_s3 = proj_s.reshape(bs, ts, PROJ_W)
        gate_bias = jnp.zeros((1, LANES), F32).at[0, :2 * N_HEADS].set(jnp.concatenate([ml_b_i[l], ml_b_f[l]]))
        ml_g = ml_norm_g[l][None, :]
        rt_g = rt_norm_g[l][None, :]
        wb, a1, a2, wc = _s5_weights(s5_A_re[l], s5_A_im[l], s5_log_dt[l], s5_B_re[l], s5_B_im[l],
                                     s5_C_re[l], s5_C_im[l])
        s5_d = s5_D[l][None, :]
        glu_w = s5_glu_w[l].astype(BF16)
        glu_b = s5_glu_b[l][None, :]

        o_sb_p = sb_attention_prompt(proj_p, sb_bias[l], uu, bp, tp, SB_QUERY_TILE)
        o_ml_p, ml_c_p, ml_n_p, ml_m_p = mlstm_mixer_prompt(proj_p3, gate_bias, ml_g, SEQS_PER_STEP, MIX_DTYPE)
        o_rt_p, rs_p = retention_mixer_prompt(proj_p3, cos_p, sin_p, rt_g, SEQS_PER_STEP, MIX_DTYPE)
        o_ml_p = o_ml_p.reshape(n_p, GROUP_W)
        o_rt_p = o_rt_p.reshape(n_p, GROUP_W)
        o_s5_p, h5_p = s5_mixer(proj_p3, S5_STEPS, wb, a1, a2, jnp.zeros((bp, 2 * S5_W), F32), wc, s5_d, glu_w,
                                glu_b, MIX_DTYPE)
        o_s5_p = o_s5_p.reshape(n_p, GROUP_W)

        o_sb_s = sb_attention_sample(proj_s, 0, ts, cache_kt, cache_vt, page_table, l, sb_bias[l], uu,
                                     SB_PAGES_PER_STEP)
        cn0 = jnp.concatenate([state_ml_C[:, l], state_ml_n[:, l][..., None],
                               jnp.zeros((bs, N_HEADS, HEAD_DIM, LANES - HEAD_DIM - 1), F32)], axis=-1)
        o_ml_s, cn_s, m_s = mlstm_mixer(proj_s3, gate_bias, ml_g, cn0, state_ml_m[:, l], SEQS_PER_STEP, F32)
        o_rt_s, rs_s = retention_mixer(proj_s3, cos_s, sin_s, rt_g, state_rt_S[:, l], SEQS_PER_STEP, F32)
        o_ml_s = o_ml_s.reshape(n_s, GROUP_W)
        o_rt_s = o_rt_s.reshape(n_s, GROUP_W)
        h0_s = jnp.concatenate([state_s5_re[:, l].reshape(bs, S5_W), state_s5_im[:, l].reshape(bs, S5_W)], axis=1)
        o_s5_s, h5_s = s5_mixer(proj_s3, ts, wb, a1, a2, h0_s, wc, s5_d, glu_w, glu_b, F32)
        o_s5_s = o_s5_s.reshape(n_s, GROUP_W)

        wo_mix = w_out[l].astype(BF16)
        g0, b0 = ln_g[l, 0][None, :], ln_b[l, 0][None, :]
        parts_p = (o_sb_p, o_ml_p, o_rt_p, o_s5_p)
        parts_s = (o_sb_s, o_ml_s, o_rt_s, o_s5_s)

        mem_kv = linear(mem_prompt.reshape(bp * N_MEM, D_MODEL),
                        jnp.concatenate([ca_wk[l], ca_wv[l]], axis=1).astype(BF16), PROMPT_TILE)
        mk_p = mem_kv[:, :GROUP_W].reshape(bp, N_MEM, GROUP_W)
        mv_p = mem_kv[:, GROUP_W:].reshape(bp, N_MEM, GROUP_W)
        wq = ca_wq[l].astype(BF16)
        wo = ca_wo[l].astype(BF16)
        g1, b1 = ln_g[l, 1][None, :], ln_b[l, 1][None, :]
        x = mix_cross_ln(x, parts_p, wo_mix, g0, b0, 0, bp, tp, PROMPT_TILE, 1, mk_p, mv_p, wq, wo, g1, b1)
        s_tile = SEQS_PER_STEP * ts
        x = mix_cross_ln(x, parts_s, wo_mix, g0, b0, n_p // s_tile, bs, ts, s_tile, SEQS_PER_STEP,
                         cache_mem_k[:, l].reshape(bs, N_MEM, GROUP_W),
                         cache_mem_v[:, l].reshape(bs, N_MEM, GROUP_W), wq, wo, g1, b1)

        g2, b2 = ln_g[l, 2][None, :], ln_b[l, 2][None, :]
        j = l // 2
        last = l == DEPTH - 1
        if l % 2 == 0:
            x = ffn_ln(x, ffn_w_gate[j], ffn_w_up[j], ffn_w_down[j], g2, b2, FFN_ROW_TILE, FFN_COL_TILE)
            y_out = (x[:n_p], x[n_p:]) if last else None
        else:
            splits = [(0, n_p, PROMPT_TILE), (n_p // n_s, n_s, n_s)] if last else [(0, n_p + n_s, tm)]
            y_out = moe_ln(x, moe_router_w[j], moe_router_b[j], moe_w_gate[j], moe_w_up[j], moe_w_down[j],
                           g2, b2, tm, splits)
            x = None if last else y_out[0]

        def heads(a, nb_, t_):
            return a.reshape(nb_, t_, N_HEADS, HEAD_DIM)

        p_st[0].append(kt_p)
        p_st[1].append(vt_p)
        p_st[2].append(heads(mk_p, bp, N_MEM))
        p_st[3].append(heads(mv_p, bp, N_MEM))
        s_st[0].append(heads(proj_s[:, C_SK * GROUP_W:(C_SK + 1) * GROUP_W], bs, ts))
        s_st[1].append(heads(proj_s[:, C_SV * GROUP_W:(C_SV + 1) * GROUP_W], bs, ts))
        ml_p = (ml_c_p, ml_n_p, ml_m_p)
        ml_s = (cn_s[..., :HEAD_DIM], cn_s[..., HEAD_DIM], m_s[:, 0, :N_HEADS])
        for st, ml, rs, h5, nb_ in ((p_st, ml_p, rs_p, h5_p, bp), (s_st, ml_s, rs_s, h5_s, bs)):
            off = 4 if st is p_st else 2
            st[off + 0].append(ml[0])
            st[off + 1].append(ml[1])
            st[off + 2].append(ml[2])
            st[off + 3].append(rs)
            st[off + 4].append(h5[:, :S5_W].reshape(nb_, S5_GROUPS, S5_STATE))
            st[off + 5].append(h5[:, S5_W:].reshape(nb_, S5_GROUPS, S5_STATE))

    y_prompt = y_out[0].reshape(bp, tp, D_MODEL)
    y_sample = y_out[1].reshape(bs, ts, D_MODEL)
    p_out = [jnp.stack(a, axis=1) for a in p_st]
    for i in range(2):
        p_out[i] = p_out[i].reshape(bp, DEPTH, N_HEADS, HEAD_DIM, tp).transpose(0, 1, 4, 2, 3)
    s_out = [jnp.stack(a, axis=1) for a in s_st]
    return (y_prompt, y_sample, *p_out, *s_out)
```
